```python
import jax, jax.numpy as jnp
from jax import lax
import numpy as np

D_MODEL = 1024
BATCH = 8
SEQ = 8192
DEPTH = 2

N_A_LAYERS = DEPTH // 2
N_B_LAYERS = DEPTH - N_A_LAYERS
HEAD_DIM = 64
SB_HEADS = D_MODEL // HEAD_DIM
SWA_Q_HEADS = D_MODEL // HEAD_DIM
SWA_KV_HEADS = 4
SWA_GROUP = SWA_Q_HEADS // SWA_KV_HEADS
WINDOW = 128
Q_BLOCK = 128
D_FF = 2816
ROPE_THETA = 10000.0
RMS_EPS = 1e-6
FFN_RES_SCALE = 0.5

kernel_name = "yoco_stickbreak_swa_sink_macaron"


def rms_norm(x, g):
    xf = x.astype(jnp.float32)
    y = xf * lax.rsqrt(jnp.mean(xf * xf, axis=-1, keepdims=True) + RMS_EPS)
    return (y * g.astype(jnp.float32)).astype(x.dtype)


def swiglu(x, w_in, w_out):
    gate, up = jnp.split(x @ w_in, 2, axis=-1)
    return (jax.nn.silu(gate) * up) @ w_out


def rotary(x, pos):
    half = HEAD_DIM // 2
    inv_freq = ROPE_THETA ** (-jnp.arange(half, dtype=jnp.float32) / half)
    ang = pos.astype(jnp.float32)[:, None] * inv_freq[None, :]
    cos = jnp.cos(ang)[None, :, None, :]
    sin = jnp.sin(ang)[None, :, None, :]
    xf = x.astype(jnp.float32)
    x1, x2 = xf[..., :half], xf[..., half:]
    return jnp.concatenate([x1 * cos - x2 * sin, x2 * cos + x1 * sin], axis=-1).astype(x.dtype)


def stick_breaking_attention(q, k, v):
    B, S, H, Dh = q.shape
    nb = S // Q_BLOCK
    scale = Dh ** -0.5
    key_pos = jnp.arange(S)
    qb = q.reshape(B, nb, Q_BLOCK, H, Dh).transpose(1, 0, 2, 3, 4)

    def block(args):
        qi, i = args
        z = jnp.einsum('bqhd,bshd->bhqs', qi, k).astype(jnp.float32) * scale
        q_pos = i * Q_BLOCK + jnp.arange(Q_BLOCK)
        strict = key_pos[None, :] < q_pos[:, None]
        log_beta = jax.nn.log_sigmoid(z)
        log_1m_beta = jnp.where(strict, log_beta - z, 0.0)
        suffix = lax.cumsum(log_1m_beta, axis=3, reverse=True) - log_1m_beta
        w = jnp.where(strict, jnp.exp(log_beta + suffix), 0.0)
        return jnp.einsum('bhqs,bshd->bqhd', w.astype(v.dtype), v)

    out = lax.map(block, (qb, jnp.arange(nb)))
    return out.transpose(1, 0, 2, 3, 4).reshape(B, S, H, Dh)


def sliding_window_sink_attention(q, k, v, sinks):
    B, S, Hq, Dh = q.shape
    nb = S // WINDOW
    qb = q.reshape(B, nb, WINDOW, SWA_KV_HEADS, SWA_GROUP, Dh)

    def band(t):
        tb = t.reshape(B, nb, WINDOW, SWA_KV_HEADS, Dh)
        prev = jnp.pad(tb[:, :-1], ((0, 0), (1, 0), (0, 0), (0, 0), (0, 0)))
        return jnp.concatenate([prev, tb], axis=2)

    kb, vb = band(k), band(v)
    s = jnp.einsum('bnqhgd,bnkhd->bnhgqk', qb, kb).astype(jnp.float32) * (Dh ** -0.5)
    qi = jnp.arange(WINDOW)[:, None]
    ki = jnp.arange(2 * WINDOW)[None, :]
    diff = qi + WINDOW - ki
    in_window = (diff >= 0) & (diff < WINDOW)
    blk = jnp.arange(nb)[:, None, None]
    valid = in_window[None] & ((blk > 0) | (ki[None] >= WINDOW))
    s = jnp.where(valid[None, :, None, None], s, -jnp.inf)
    sink = jnp.broadcast_to(
        sinks.astype(jnp.float32).reshape(SWA_KV_HEADS, SWA_GROUP)[None, None, :, :, None, None],
        s.shape[:-1] + (1,))
    p = jax.nn.softmax(jnp.concatenate([s, sink], axis=-1), axis=-1)[..., :-1]
    out = jnp.einsum('bnhgqk,bnkhd->bnqhgd', p.astype(v.dtype), vb)
    return out.reshape(B, S, Hq, Dh)


def _fwd_setup_inputs(seed: int = 0) -> dict:
    key = jax.random.key(seed)
    ks = jax.random.split(key, 20)
    f32 = jnp.float32

    def w(k, shape, fan_in):
        return jax.random.normal(k, shape, f32) * (fan_in ** -0.5)

    def gain(k, shape):
        return 1.0 + 0.02 * jax.random.normal(k, shape, f32)

    return {
        "x": jax.random.normal(ks[0], (BATCH, SEQ, D_MODEL), f32),
        "ffn1_norm": gain(ks[1], (DEPTH, D_MODEL)),
        "ffn1_w_in": w(ks[2], (DEPTH, D_MODEL, 2 * D_FF), D_MODEL),
        "ffn1_w_out": w(ks[3], (DEPTH, D_FF, D_MODEL), D_FF),
        "mix_norm": gain(ks[4], (DEPTH, D_MODEL)),
        "ffn2_norm": gain(ks[5], (DEPTH, D_MODEL)),
        "ffn2_w_in": w(ks[6], (DEPTH, D_MODEL, 2 * D_FF), D_MODEL),
        "ffn2_w_out": w(ks[7], (DEPTH, D_FF, D_MODEL), D_FF),
        "sb_w_qkv": w(ks[8], (N_A_LAYERS, D_MODEL, 3 * SB_HEADS * HEAD_DIM), D_MODEL),
        "sb_w_o": w(ks[9], (N_A_LAYERS, SB_HEADS * HEAD_DIM, D_MODEL), SB_HEADS * HEAD_DIM),
        "kv_norm": gain(ks[10], (D_MODEL,)),
        "kv_w": w(ks[11], (D_MODEL, 2 * SWA_KV_HEADS * HEAD_DIM), D_MODEL),
        "swa_w_q": w(ks[12], (N_B_LAYERS, D_MODEL, SWA_Q_HEADS * HEAD_DIM), D_MODEL),
        "swa_sinks": 0.5 * jax.random.normal(ks[13], (N_B_LAYERS, SWA_Q_HEADS), f32),
        "swa_w_o": w(ks[14], (N_B_LAYERS, SWA_Q_HEADS * HEAD_DIM, D_MODEL), SWA_Q_HEADS * HEAD_DIM),
        "final_norm": gain(ks[15], (D_MODEL,)),
    }


def _fwd_reference(x, ffn1_norm, ffn1_w_in, ffn1_w_out, mix_norm, ffn2_norm, ffn2_w_in, ffn2_w_out,
              sb_w_qkv, sb_w_o, kv_norm, kv_w, swa_w_q, swa_sinks, swa_w_o, final_norm):
    B, S, D = x.shape
    pos = jnp.arange(S)
    h = x
    k_shared = None
    v_shared = None
    for layer in range(DEPTH):
        h = h + FFN_RES_SCALE * swiglu(rms_norm(h, ffn1_norm[layer]), ffn1_w_in[layer], ffn1_w_out[layer])
        hn = rms_norm(h, mix_norm[layer])
        if layer < N_A_LAYERS:
            qkv = (hn @ sb_w_qkv[layer]).reshape(B, S, 3, SB_HEADS, HEAD_DIM)
            o = stick_breaking_attention(qkv[:, :, 0], qkv[:, :, 1], qkv[:, :, 2])
            h = h + o.reshape(B, S, SB_HEADS * HEAD_DIM) @ sb_w_o[layer]
        else:
            j = layer - N_A_LAYERS
            q = rotary((hn @ swa_w_q[j]).reshape(B, S, SWA_Q_HEADS, HEAD_DIM), pos)
            o = sliding_window_sink_attention(q, k_shared, v_shared, swa_sinks[j])
            h = h + o.reshape(B, S, SWA_Q_HEADS * HEAD_DIM) @ swa_w_o[j]
        h = h + FFN_RES_SCALE * swiglu(rms_norm(h, ffn2_norm[layer]), ffn2_w_in[layer], ffn2_w_out[layer])
        if layer == N_A_LAYERS - 1:
            kv = (rms_norm(h, kv_norm) @ kv_w).reshape(B, S, 2, SWA_KV_HEADS, HEAD_DIM)
            k_shared = rotary(kv[:, :, 0], pos)
            v_shared = kv[:, :, 1]
    return rms_norm(h, final_norm)


import jax as _jax
import jax.numpy as _jnp

TWIN_FORMAT = 'train_step'
FWD_PARAMS = ['x', 'ffn1_norm', 'ffn1_w_in', 'ffn1_w_out', 'mix_norm', 'ffn2_norm', 'ffn2_w_in', 'ffn2_w_out', 'sb_w_qkv', 'sb_w_o', 'kv_norm', 'kv_w', 'swa_w_q', 'swa_sinks', 'swa_w_o', 'final_norm']
TWIN_WEIGHTS = ['ffn1_norm', 'ffn1_w_in', 'ffn1_w_out', 'mix_norm', 'ffn2_norm', 'ffn2_w_in', 'ffn2_w_out', 'sb_w_qkv', 'sb_w_o', 'kv_norm', 'kv_w', 'swa_w_q', 'swa_sinks', 'swa_w_o', 'final_norm']
TWIN_DIFF_INPUT = 'x'
TWIN_INPUTS = ['x', 'ffn1_norm', 'ffn1_w_in', 'ffn1_w_out', 'mix_norm', 'ffn2_norm', 'ffn2_w_in', 'ffn2_w_out', 'sb_w_qkv', 'sb_w_o', 'kv_norm', 'kv_w', 'swa_w_q', 'swa_sinks', 'swa_w_o', 'final_norm', 'loss_target', 'm_ffn1_norm', 'm_ffn1_w_in', 'm_ffn1_w_out', 'm_mix_norm', 'm_ffn2_norm', 'm_ffn2_w_in', 'm_ffn2_w_out', 'm_sb_w_qkv', 'm_sb_w_o', 'm_kv_norm', 'm_kv_w', 'm_swa_w_q', 'm_swa_sinks', 'm_swa_w_o', 'm_final_norm', 'v_ffn1_norm', 'v_ffn1_w_in', 'v_ffn1_w_out', 'v_mix_norm', 'v_ffn2_norm', 'v_ffn2_w_in', 'v_ffn2_w_out', 'v_sb_w_qkv', 'v_sb_w_o', 'v_kv_norm', 'v_kv_w', 'v_swa_w_q', 'v_swa_sinks', 'v_swa_w_o', 'v_final_norm']
TWIN_OUTPUTS = ['loss', 'grad_x', 'grad_ffn1_norm', 'grad_ffn1_w_in', 'grad_ffn1_w_out', 'grad_mix_norm', 'grad_ffn2_norm', 'grad_ffn2_w_in', 'grad_ffn2_w_out', 'grad_sb_w_qkv', 'grad_sb_w_o', 'grad_kv_norm', 'grad_kv_w', 'grad_swa_w_q', 'grad_swa_sinks', 'grad_swa_w_o', 'grad_final_norm', 'delta_ffn1_norm', 'delta_ffn1_w_in', 'delta_ffn1_w_out', 'delta_mix_norm', 'delta_ffn2_norm', 'delta_ffn2_w_in', 'delta_ffn2_w_out', 'delta_sb_w_qkv', 'delta_sb_w_o', 'delta_kv_norm', 'delta_kv_w', 'delta_swa_w_q', 'delta_swa_sinks', 'delta_swa_w_o', 'delta_final_norm', 'new_m_ffn1_norm', 'new_m_ffn1_w_in', 'new_m_ffn1_w_out', 'new_m_mix_norm', 'new_m_ffn2_norm', 'new_m_ffn2_w_in', 'new_m_ffn2_w_out', 'new_m_sb_w_qkv', 'new_m_sb_w_o', 'new_m_kv_norm', 'new_m_kv_w', 'new_m_swa_w_q', 'new_m_swa_sinks', 'new_m_swa_w_o', 'new_m_final_norm', 'new_v_ffn1_norm', 'new_v_ffn1_w_in', 'new_v_ffn1_w_out', 'new_v_mix_norm', 'new_v_ffn2_norm', 'new_v_ffn2_w_in', 'new_v_ffn2_w_out', 'new_v_sb_w_qkv', 'new_v_sb_w_o', 'new_v_kv_norm', 'new_v_kv_w', 'new_v_swa_w_q', 'new_v_swa_sinks', 'new_v_swa_w_o', 'new_v_final_norm']
TWIN_LEAF_KINDS = {'loss': 'loss', 'grad_x': 'grad_x', 'grad_ffn1_norm': 'grad_w', 'grad_ffn1_w_in': 'grad_w', 'grad_ffn1_w_out': 'grad_w', 'grad_mix_norm': 'grad_w', 'grad_ffn2_norm': 'grad_w', 'grad_ffn2_w_in': 'grad_w', 'grad_ffn2_w_out': 'grad_w', 'grad_sb_w_qkv': 'grad_w', 'grad_sb_w_o': 'grad_w', 'grad_kv_norm': 'grad_w', 'grad_kv_w': 'grad_w', 'grad_swa_w_q': 'grad_w', 'grad_swa_sinks': 'grad_w', 'grad_swa_w_o': 'grad_w', 'grad_final_norm': 'grad_w', 'delta_ffn1_norm': 'delta_w', 'delta_ffn1_w_in': 'delta_w', 'delta_ffn1_w_out': 'delta_w', 'delta_mix_norm': 'delta_w', 'delta_ffn2_norm': 'delta_w', 'delta_ffn2_w_in': 'delta_w', 'delta_ffn2_w_out': 'delta_w', 'delta_sb_w_qkv': 'delta_w', 'delta_sb_w_o': 'delta_w', 'delta_kv_norm': 'delta_w', 'delta_kv_w': 'delta_w', 'delta_swa_w_q': 'delta_w', 'delta_swa_sinks': 'delta_w', 'delta_swa_w_o': 'delta_w', 'delta_final_norm': 'delta_w', 'new_m_ffn1_norm': 'new_m', 'new_m_ffn1_w_in': 'new_m', 'new_m_ffn1_w_out': 'new_m', 'new_m_mix_norm': 'new_m', 'new_m_ffn2_norm': 'new_m', 'new_m_ffn2_w_in': 'new_m', 'new_m_ffn2_w_out': 'new_m', 'new_m_sb_w_qkv': 'new_m', 'new_m_sb_w_o': 'new_m', 'new_m_kv_norm': 'new_m', 'new_m_kv_w': 'new_m', 'new_m_swa_w_q': 'new_m', 'new_m_swa_sinks': 'new_m', 'new_m_swa_w_o': 'new_m', 'new_m_final_norm': 'new_m', 'new_v_ffn1_norm': 'new_v', 'new_v_ffn1_w_in': 'new_v', 'new_v_ffn1_w_out': 'new_v', 'new_v_mix_norm': 'new_v', 'new_v_ffn2_norm': 'new_v', 'new_v_ffn2_w_in': 'new_v', 'new_v_ffn2_w_out': 'new_v', 'new_v_sb_w_qkv': 'new_v', 'new_v_sb_w_o': 'new_v', 'new_v_kv_norm': 'new_v', 'new_v_kv_w': 'new_v', 'new_v_swa_w_q': 'new_v', 'new_v_swa_sinks': 'new_v', 'new_v_swa_w_o': 'new_v', 'new_v_final_norm': 'new_v'}


def _forward(args):
    return _fwd_reference(*[args[k] for k in FWD_PARAMS])


def _output_shape():
    def fwd():
        inp = _fwd_setup_inputs(0)
        return _fwd_reference(*[inp[k] for k in FWD_PARAMS])
    out = _jax.eval_shape(fwd)
    return out.shape, out.dtype

N_MICROBATCH = 1
ADAM_LR = 0.001
ADAM_B1 = 0.9
ADAM_B2 = 0.999
ADAM_EPS = 1e-08
ADAM_WD = 0.01
ADAM_STEP = 10
PER_EXAMPLE_BATCH_AXIS = {'x': 0, 'loss_target': 0}
SHARED_INPUTS = []
_WEIGHT_DTYPES = {'ffn1_norm': _jnp.float32, 'ffn1_w_in': _jnp.float32, 'ffn1_w_out': _jnp.float32, 'mix_norm': _jnp.float32, 'ffn2_norm': _jnp.float32, 'ffn2_w_in': _jnp.float32, 'ffn2_w_out': _jnp.float32, 'sb_w_qkv': _jnp.float32, 'sb_w_o': _jnp.float32, 'kv_norm': _jnp.float32, 'kv_w': _jnp.float32, 'swa_w_q': _jnp.float32, 'swa_sinks': _jnp.float32, 'swa_w_o': _jnp.float32, 'final_norm': _jnp.float32}
MOMENT_SCALE = {'ffn1_norm': 1.093490e-01, 'ffn1_w_in': 4.431969e-02, 'ffn1_w_out': 7.212903e-02, 'mix_norm': 1.282965e-01, 'ffn2_norm': 8.686289e-02, 'ffn2_w_in': 3.651607e-02, 'ffn2_w_out': 5.959709e-02, 'sb_w_qkv': 9.993846e-02, 'sb_w_o': 1.445205e-01, 'kv_norm': 5.359725e-02, 'kv_w': 7.723492e-02, 'swa_w_q': 3.459358e-02, 'swa_sinks': 3.425557e-02, 'swa_w_o': 4.224446e-02, 'final_norm': 6.396693e+01}


def _to_microbatches(a, axis):
    t = _jnp.moveaxis(a, axis, 0)
    t = t.reshape((N_MICROBATCH, t.shape[0] // N_MICROBATCH) + t.shape[1:])
    return _jnp.moveaxis(t, 1, axis + 1)


def setup_inputs(seed: int = 0) -> dict:
    inp = _fwd_setup_inputs(seed)
    key = _jax.random.fold_in(_jax.random.key(seed), 7919)
    shape, _ = _output_shape()
    out = dict(inp)
    out["loss_target"] = _jax.random.normal(_jax.random.fold_in(key, 0), shape, _jnp.float32)
    for i, name in enumerate(TWIN_WEIGHTS):
        w = inp[name].astype(_jnp.float32)
        if MOMENT_SCALE is None:
            s = _jnp.sqrt(_jnp.mean(_jnp.square(w)) + 1e-30)
        else:
            s = MOMENT_SCALE[name]
        km, kv = _jax.random.split(_jax.random.fold_in(key, i + 1))
        out[name] = w
        out["m_" + name] = s * _jax.random.normal(km, w.shape, _jnp.float32)
        out["v_" + name] = (s * s) * _jax.random.uniform(kv, w.shape, _jnp.float32, 0.5, 1.5)
    if N_MICROBATCH > 1:
        for name, axis in PER_EXAMPLE_BATCH_AXIS.items():
            out[name] = _to_microbatches(out[name], axis)
    return {'x': out['x'], 'ffn1_norm': out['ffn1_norm'], 'ffn1_w_in': out['ffn1_w_in'], 'ffn1_w_out': out['ffn1_w_out'], 'mix_norm': out['mix_norm'], 'ffn2_norm': out['ffn2_norm'], 'ffn2_w_in': out['ffn2_w_in'], 'ffn2_w_out': out['ffn2_w_out'], 'sb_w_qkv': out['sb_w_qkv'], 'sb_w_o': out['sb_w_o'], 'kv_norm': out['kv_norm'], 'kv_w': out['kv_w'], 'swa_w_q': out['swa_w_q'], 'swa_sinks': out['swa_sinks'], 'swa_w_o': out['swa_w_o'], 'final_norm': out['final_norm'], 'loss_target': out['loss_target'], 'm_ffn1_norm': out['m_ffn1_norm'], 'm_ffn1_w_in': out['m_ffn1_w_in'], 'm_ffn1_w_out': out['m_ffn1_w_out'], 'm_mix_norm': out['m_mix_norm'], 'm_ffn2_norm': out['m_ffn2_norm'], 'm_ffn2_w_in': out['m_ffn2_w_in'], 'm_ffn2_w_out': out['m_ffn2_w_out'], 'm_sb_w_qkv': out['m_sb_w_qkv'], 'm_sb_w_o': out['m_sb_w_o'], 'm_kv_norm': out['m_kv_norm'], 'm_kv_w': out['m_kv_w'], 'm_swa_w_q': out['m_swa_w_q'], 'm_swa_sinks': out['m_swa_sinks'], 'm_swa_w_o': out['m_swa_w_o'], 'm_final_norm': out['m_final_norm'], 'v_ffn1_norm': out['v_ffn1_norm'], 'v_ffn1_w_in': out['v_ffn1_w_in'], 'v_ffn1_w_out': out['v_ffn1_w_out'], 'v_mix_norm': out['v_mix_norm'], 'v_ffn2_norm': out['v_ffn2_norm'], 'v_ffn2_w_in': out['v_ffn2_w_in'], 'v_ffn2_w_out': out['v_ffn2_w_out'], 'v_sb_w_qkv': out['v_sb_w_qkv'], 'v_sb_w_o': out['v_sb_w_o'], 'v_kv_norm': out['v_kv_norm'], 'v_kv_w': out['v_kv_w'], 'v_swa_w_q': out['v_swa_w_q'], 'v_swa_sinks': out['v_swa_sinks'], 'v_swa_w_o': out['v_swa_w_o'], 'v_final_norm': out['v_final_norm']}


def _loss(weights, diff, rest, loss_target):
    with _jax.named_scope("forward"):
        args = {**rest, TWIN_DIFF_INPUT: diff, **{k: w.astype(_WEIGHT_DTYPES[k]) for k, w in weights.items()}}
        y = _forward(args)
    with _jax.named_scope("loss_head"):
        err = _jnp.square(y.astype(_jnp.float32) - loss_target)
        return 0.5 * _jnp.sum(_jnp.mean(err, axis=-1)) if err.ndim else 0.5 * err


def _adamw(w, g, m, v):
    m = ADAM_B1 * m + (1.0 - ADAM_B1) * g
    v = ADAM_B2 * v + (1.0 - ADAM_B2) * _jnp.square(g)
    m_hat = m / (1.0 - ADAM_B1 ** ADAM_STEP)
    v_hat = v / (1.0 - ADAM_B2 ** ADAM_STEP)
    delta = -ADAM_LR * (m_hat / (_jnp.sqrt(v_hat) + ADAM_EPS) + ADAM_WD * w)
    return delta, m, v


def reference(x, ffn1_norm, ffn1_w_in, ffn1_w_out, mix_norm, ffn2_norm, ffn2_w_in, ffn2_w_out, sb_w_qkv, sb_w_o, kv_norm, kv_w, swa_w_q, swa_sinks, swa_w_o, final_norm, loss_target, m_ffn1_norm, m_ffn1_w_in, m_ffn1_w_out, m_mix_norm, m_ffn2_norm, m_ffn2_w_in, m_ffn2_w_out, m_sb_w_qkv, m_sb_w_o, m_kv_norm, m_kv_w, m_swa_w_q, m_swa_sinks, m_swa_w_o, m_final_norm, v_ffn1_norm, v_ffn1_w_in, v_ffn1_w_out, v_mix_norm, v_ffn2_norm, v_ffn2_w_in, v_ffn2_w_out, v_sb_w_qkv, v_sb_w_o, v_kv_norm, v_kv_w, v_swa_w_q, v_swa_sinks, v_swa_w_o, v_final_norm):
    given = dict(x=x, ffn1_norm=ffn1_norm, ffn1_w_in=ffn1_w_in, ffn1_w_out=ffn1_w_out, mix_norm=mix_norm, ffn2_norm=ffn2_norm, ffn2_w_in=ffn2_w_in, ffn2_w_out=ffn2_w_out, sb_w_qkv=sb_w_qkv, sb_w_o=sb_w_o, kv_norm=kv_norm, kv_w=kv_w, swa_w_q=swa_w_q, swa_sinks=swa_sinks, swa_w_o=swa_w_o, final_norm=final_norm, loss_target=loss_target, m_ffn1_norm=m_ffn1_norm, m_ffn1_w_in=m_ffn1_w_in, m_ffn1_w_out=m_ffn1_w_out, m_mix_norm=m_mix_norm, m_ffn2_norm=m_ffn2_norm, m_ffn2_w_in=m_ffn2_w_in, m_ffn2_w_out=m_ffn2_w_out, m_sb_w_qkv=m_sb_w_qkv, m_sb_w_o=m_sb_w_o, m_kv_norm=m_kv_norm, m_kv_w=m_kv_w, m_swa_w_q=m_swa_w_q, m_swa_sinks=m_swa_sinks, m_swa_w_o=m_swa_w_o, m_final_norm=m_final_norm, v_ffn1_norm=v_ffn1_norm, v_ffn1_w_in=v_ffn1_w_in, v_ffn1_w_out=v_ffn1_w_out, v_mix_norm=v_mix_norm, v_ffn2_norm=v_ffn2_norm, v_ffn2_w_in=v_ffn2_w_in, v_ffn2_w_out=v_ffn2_w_out, v_sb_w_qkv=v_sb_w_qkv, v_sb_w_o=v_sb_w_o, v_kv_norm=v_kv_norm, v_kv_w=v_kv_w, v_swa_w_q=v_swa_w_q, v_swa_sinks=v_swa_sinks, v_swa_w_o=v_swa_w_o, v_final_norm=v_final_norm)
    weights = {n: given[n] for n in TWIN_WEIGHTS}
    shared = {n: given[n] for n in SHARED_INPUTS}
    per_example = {n: given[n] for n in ['x']}
    grad_fn = _jax.value_and_grad(_loss, argnums=(0, 1))

    def one_microbatch(ex, loss_target):
        ex = dict(ex)
        diff = ex.pop(TWIN_DIFF_INPUT)
        return grad_fn(weights, diff, {**shared, **ex}, loss_target)

    if N_MICROBATCH == 1:
        loss, (grad_w, grad_x) = one_microbatch(per_example, given["loss_target"])
    else:
        def body(carry, xs):
            loss_sum, grad_sum = carry
            l_k, (gw_k, gx_k) = one_microbatch(xs[0], xs[1])
            with _jax.named_scope("update"):
                return (loss_sum + l_k, _jax.tree.map(_jnp.add, grad_sum, gw_k)), gx_k

        init = (_jnp.zeros((), _jnp.float32), _jax.tree.map(_jnp.zeros_like, weights))
        (loss, grad_w), grad_x = _jax.lax.scan(body, init, (per_example, given["loss_target"]))
    with _jax.named_scope("update"):
        delta_w, new_m, new_v = {}, {}, {}
        for n in TWIN_WEIGHTS:
            delta_w[n], new_m[n], new_v[n] = _adamw(weights[n], grad_w[n], given["m_" + n], given["v_" + n])
    return (loss, grad_x, *[grad_w[n] for n in TWIN_WEIGHTS], *[delta_w[n] for n in TWIN_WEIGHTS],
            *[new_m[n] for n in TWIN_WEIGHTS], *[new_v[n] for n in TWIN_WEIGHTS])
```

```python
import jax
import jax.numpy as jnp
from jax import lax
from jax.experimental import pallas as pl
from jax.experimental.pallas import tpu as pltpu

F32 = jnp.float32
BF16 = jnp.bfloat16

N_DEV = 8
HEAD_DIM = 64
LANES = 128
BLK = 128
PACK_W = 1024
RMS_EPS = 1e-6
FFN_RES_SCALE = 0.5
ROPE_THETA = 10000.0
ATTN_SCALE = HEAD_DIM ** -0.5
SB_LOG_FLOOR = -110.0
NEG_BIG = -1e30
VMEM_LIMIT_V7X = 56 * 1024 * 1024

ADAM_LR = 0.001
ADAM_B1 = 0.9
ADAM_B2 = 0.999
ADAM_EPS = 1e-08
ADAM_WD = 0.01
ADAM_STEP = 10

NN = ((1,), (0,))
NT = ((1,), (1,))
TN = ((0,), (0,))
MESH = pl.DeviceIdType.MESH


def _dot(a, b, dims):
    return lax.dot_general(a, b, (dims, ((), ())), preferred_element_type=F32)


def _tile(n, pref, mult=LANES):
    if n <= pref:
        return n
    t = (pref // mult) * mult
    while t >= mult:
        if n % t == 0:
            return t
        t -= mult
    return n


def _params(*sem):
    return pltpu.CompilerParams(dimension_semantics=sem, vmem_limit_bytes=VMEM_LIMIT_V7X)


def _mm(a, b, dims, out_dtype, name, scale=1.0, res=None, tm=512, tn=512, tk=512):
    if dims == NN:
        (M, K), (_, N) = a.shape, b.shape
    elif dims == NT:
        (M, K), (N, _) = a.shape, b.shape
    else:
        (K, M), (_, N) = a.shape, b.shape
    tm, tn, tk = _tile(M, tm), _tile(N, tn), _tile(K, tk)
    nk = K // tk
    if dims == TN:
        a_spec = pl.BlockSpec((tk, tm), lambda i, j, k: (k, i))
    else:
        a_spec = pl.BlockSpec((tm, tk), lambda i, j, k: (i, k))
    if dims == NT:
        b_spec = pl.BlockSpec((tn, tk), lambda i, j, k: (j, k))
    else:
        b_spec = pl.BlockSpec((tk, tn), lambda i, j, k: (k, j))
    o_spec = pl.BlockSpec((tm, tn), lambda i, j, k: (i, j))
    has_res = res is not None

    def body(*refs):
        a_ref, b_ref = refs[0], refs[1]
        r_ref = refs[2] if has_res else None
        o_ref = refs[3] if has_res else refs[2]

        def finish(acc):
            r = acc * scale if scale != 1.0 else acc
            if has_res:
                r = r + r_ref[...]
            o_ref[...] = r.astype(out_dtype)

        p = _dot(a_ref[...].astype(BF16), b_ref[...].astype(BF16), dims)
        if nk == 1:
            finish(p)
        else:
            acc_ref = refs[-1]
            k = pl.program_id(2)

            @pl.when(k == 0)
            def _():
                acc_ref[...] = p

            @pl.when(k > 0)
            def _():
                acc_ref[...] += p

            @pl.when(k == nk - 1)
            def _():
                finish(acc_ref[...])

    in_specs = [a_spec, b_spec] + ([o_spec] if has_res else [])
    args = (a, b) + ((res,) if has_res else ())
    return pl.pallas_call(
        body, name=name,
        out_shape=jax.ShapeDtypeStruct((M, N), out_dtype),
        grid=(M // tm, N // tn, nk),
        in_specs=in_specs, out_specs=o_spec,
        scratch_shapes=[pltpu.VMEM((tm, tn), F32)] if nk > 1 else [],
        compiler_params=_params("parallel", "parallel", "arbitrary"),
    )(*args)


def _rows8(x):
    r, d = x.shape
    return jnp.sum(x.reshape(r // 8, 8, d), axis=0)


def _rmsnorm(h, g, name):
    S, D = h.shape
    ts = _tile(S, 512, 8)

    def body(h_ref, g_ref, o_ref):
        x = h_ref[...]
        r = lax.rsqrt(jnp.mean(x * x, axis=-1, keepdims=True) + RMS_EPS)
        o_ref[...] = ((x * r) * g_ref[...]).astype(BF16)

    return pl.pallas_call(
        body, name=name,
        out_shape=jax.ShapeDtypeStruct((S, D), BF16),
        grid=(S // ts,),
        in_specs=[pl.BlockSpec((ts, D), lambda i: (i, 0)), pl.BlockSpec((1, D), lambda i: (0, 0))],
        out_specs=pl.BlockSpec((ts, D), lambda i: (i, 0)),
        compiler_params=_params("parallel"),
    )(h, g.reshape(1, D))


def _rmsnorm_bwd(dxn, h, g, res, name):
    S, D = h.shape
    ts = _tile(S, 512, 8)

    def body(d_ref, h_ref, g_ref, r_ref, dh_ref, dg_ref):
        x = h_ref[...]
        r = lax.rsqrt(jnp.mean(x * x, axis=-1, keepdims=True) + RMS_EPS)
        xhat = x * r
        d = d_ref[...].astype(F32)
        dxh = d * g_ref[...]
        c = jnp.mean(dxh * xhat, axis=-1, keepdims=True)
        dh_ref[...] = r * (dxh - xhat * c) + r_ref[...]
        part = _rows8(d * xhat)

        @pl.when(pl.program_id(0) == 0)
        def _():
            dg_ref[...] = part

        @pl.when(pl.program_id(0) > 0)
        def _():
            dg_ref[...] += part

    row = pl.BlockSpec((ts, D), lambda i: (i, 0))
    return pl.pallas_call(
        body, name=name,
        out_shape=(jax.ShapeDtypeStruct((S, D), F32), jax.ShapeDtypeStruct((8, D), F32)),
        grid=(S // ts,),
        in_specs=[row, row, pl.BlockSpec((1, D), lambda i: (0, 0)), row],
        out_specs=(row, pl.BlockSpec((8, D), lambda i: (0, 0))),
        compiler_params=_params("arbitrary"),
    )(dxn, h, g.reshape(1, D), res)


def _final_loss(h, g, tgt, name):
    S, D = h.shape
    ts = _tile(S, 512, 8)

    def body(h_ref, g_ref, t_ref, dh_ref, dg_ref, l_ref):
        x = h_ref[...]
        r = lax.rsqrt(jnp.mean(x * x, axis=-1, keepdims=True) + RMS_EPS)
        xhat = x * r
        err = xhat * g_ref[...] - t_ref[...]
        d = err * (1.0 / D)
        dxh = d * g_ref[...]
        c = jnp.mean(dxh * xhat, axis=-1, keepdims=True)
        dh_ref[...] = r * (dxh - xhat * c)
        part = _rows8(d * xhat)
        lpart = _rows8(err * err)

        @pl.when(pl.program_id(0) == 0)
        def _():
            dg_ref[...] = part
            l_ref[...] = lpart

        @pl.when(pl.program_id(0) > 0)
        def _():
            dg_ref[...] += part
            l_ref[...] += lpart

    row = pl.BlockSpec((ts, D), lambda i: (i, 0))
    acc = pl.BlockSpec((8, D), lambda i: (0, 0))
    return pl.pallas_call(
        body, name=name,
        out_shape=(jax.ShapeDtypeStruct((S, D), F32), jax.ShapeDtypeStruct((8, D), F32),
                   jax.ShapeDtypeStruct((8, D), F32)),
        grid=(S // ts,),
        in_specs=[row, pl.BlockSpec((1, D), lambda i: (0, 0)), row],
        out_specs=(row, acc, acc),
        compiler_params=_params("arbitrary"),
    )(h, g.reshape(1, D), tgt)


def _swiglu(gate, up, name):
    S, F = gate.shape
    ts, tf = _tile(S, 512, 16), _tile(F, 1408)

    def body(g_ref, u_ref, o_ref):
        g = g_ref[...].astype(F32)
        sig = 1.0 / (1.0 + jnp.exp(-g))
        o_ref[...] = (g * sig * u_ref[...].astype(F32)).astype(BF16)

    blk = pl.BlockSpec((ts, tf), lambda i, j: (i, j))
    return pl.pallas_call(
        body, name=name, out_shape=jax.ShapeDtypeStruct((S, F), BF16),
        grid=(S // ts, F // tf), in_specs=[blk, blk], out_specs=blk,
        compiler_params=_params("parallel", "parallel"),
    )(gate, up)


def _swiglu_bwd(gate, up, da, name):
    S, F = gate.shape
    ts, tf = _tile(S, 512, 16), _tile(F, 1408)

    def body(g_ref, u_ref, d_ref, dg_ref, du_ref):
        g = g_ref[...].astype(F32)
        u = u_ref[...].astype(F32)
        d = d_ref[...].astype(F32)
        sig = 1.0 / (1.0 + jnp.exp(-g))
        silu = g * sig
        du_ref[...] = (d * silu).astype(BF16)
        dg_ref[...] = (d * u * (sig * (1.0 + g * (1.0 - sig)))).astype(BF16)

    blk = pl.BlockSpec((ts, tf), lambda i, j: (i, j))
    shp = jax.ShapeDtypeStruct((S, F), BF16)
    return pl.pallas_call(
        body, name=name, out_shape=(shp, shp),
        grid=(S // ts, F // tf), in_specs=[blk, blk, blk], out_specs=(blk, blk),
        compiler_params=_params("parallel", "parallel"),
    )(gate, up, da)


def _rope_tables(S):
    half = HEAD_DIM // 2
    inv_freq = ROPE_THETA ** (-jnp.arange(half, dtype=F32) / half)
    ang = jnp.arange(S).astype(F32)[:, None] * inv_freq[None, :]
    cos, sin = jnp.cos(ang), jnp.sin(ang)
    cos_t = jnp.tile(cos, (1, LANES // half))
    sin_t = jnp.tile(jnp.concatenate([-sin, sin], axis=1), (1, LANES // HEAD_DIM))
    return cos_t, sin_t


def _swap_halves(x):
    lane = lax.broadcasted_iota(jnp.int32, x.shape, 1)
    first = (lane % HEAD_DIM) < (HEAD_DIM // 2)
    return jnp.where(first, pltpu.roll(x, LANES - HEAD_DIM // 2, 1), pltpu.roll(x, HEAD_DIM // 2, 1))


def _rotary(x, cos_t, sin_t, n_rot, inverse, name):
    S, C = x.shape
    ts = _tile(S, 512, 16)
    ng = C // LANES

    def body(x_ref, c_ref, s_ref, o_ref):
        cs, sn = c_ref[...], s_ref[...]
        for gidx in range(ng):
            sl = slice(gidx * LANES, (gidx + 1) * LANES)
            v = x_ref[:, sl].astype(F32)
            if gidx < n_rot:
                if inverse:
                    v = v * cs + _swap_halves(v * sn)
                else:
                    v = v * cs + _swap_halves(v) * sn
            o_ref[:, sl] = v.astype(BF16)

    row = pl.BlockSpec((ts, C), lambda i: (i, 0))
    tab = pl.BlockSpec((ts, LANES), lambda i: (i, 0))
    return pl.pallas_call(
        body, name=name, out_shape=jax.ShapeDtypeStruct((S, C), BF16),
        grid=(S // ts,), in_specs=[row, tab, tab], out_specs=row,
        compiler_params=_params("parallel"),
    )(x, cos_t, sin_t)


def _head_masks():
    lane = lax.broadcasted_iota(jnp.int32, (BLK, LANES), 1)
    return lane < HEAD_DIM


def _split_bf16(x):
    hi = x.astype(BF16)
    lo = (x - hi.astype(F32)).astype(BF16)
    return hi, lo


def _sb_scores(qh, k2, carry_h, diag, tri_excl, strict):
    z = _dot(qh, k2, NT) * ATTN_SCALE
    a = jnp.minimum(z, 0.0) - jnp.log(1.0 + jnp.exp(-jnp.abs(z)))
    b = a - z
    if diag:
        b = jnp.where(strict, b, 0.0)
    bhi, blo = _split_bf16(b)
    suf = _dot(bhi, tri_excl, NN) + _dot(blo, tri_excl, NN)
    w = jnp.exp(a + suf + carry_h)
    if diag:
        w = jnp.where(strict, w, 0.0)
    return a, b, w


def _sb_fwd(qkv, name):
    S, D3 = qkv.shape
    D = D3 // 3
    npair, nb = D // LANES, S // BLK

    def body(q_ref, k_ref, v_ref, o_ref):
        i = pl.program_id(1)
        m0 = _head_masks()
        row = lax.broadcasted_iota(jnp.int32, (BLK, BLK), 0)
        col = lax.broadcasted_iota(jnp.int32, (BLK, BLK), 1)
        strict = col < row
        tri_excl = jnp.where(row > col, 1.0, 0.0).astype(BF16)
        q2 = q_ref[...]
        zq = jnp.zeros_like(q2)
        qh = (jnp.where(m0, q2, zq), jnp.where(m0, zq, q2))

        def block(j, carry, acc, diag):
            off = pl.multiple_of(j * BLK, BLK)
            k2 = k_ref[pl.ds(off, BLK), :]
            v2 = v_ref[pl.ds(off, BLK), :]
            vh = (jnp.where(m0, v2, zq), jnp.where(m0, zq, v2))
            new = []
            for h in range(2):
                _, b, w = _sb_scores(qh[h], k2, carry[h], diag, tri_excl, strict)
                acc = acc + _dot(w.astype(BF16), vh[h], NN)
                new.append(carry[h] + jnp.sum(b, axis=1, keepdims=True))
            return new, acc

        def alive(c0, c1):
            return (jnp.maximum(jnp.max(c0), jnp.max(c1)) > SB_LOG_FLOOR).astype(jnp.int32)

        c0 = jnp.zeros((BLK, 1), F32)
        carry, acc = block(i, (c0, c0), jnp.zeros((BLK, LANES), F32), True)

        def cond(st):
            return jnp.logical_and(st[0] >= 0, st[1] > 0)

        def step(st):
            j, _, ca, cb, acc = st
            (ca, cb), acc = block(j, (ca, cb), acc, False)
            return j - 1, alive(ca, cb), ca, cb, acc

        st = lax.while_loop(cond, step, (i - 1, alive(carry[0], carry[1]), carry[0], carry[1], acc))
        o_ref[...] = st[4]

    return pl.pallas_call(
        body, name=name, out_shape=jax.ShapeDtypeStruct((S, D), F32),
        grid=(npair, nb),
        in_specs=[pl.BlockSpec((BLK, LANES), lambda p, i: (i, p)),
                  pl.BlockSpec((S, LANES), lambda p, i: (0, npair + p)),
                  pl.BlockSpec((S, LANES), lambda p, i: (0, 2 * npair + p))],
        out_specs=pl.BlockSpec((BLK, LANES), lambda p, i: (i, p)),
        compiler_params=_params("arbitrary", "arbitrary"),
    )(qkv, qkv, qkv)


def _sb_bwd(qkv, o, do, name):
    S, D3 = qkv.shape
    D = D3 // 3
    npair, nb = D // LANES, S // BLK

    def body(q_ref, k_ref, v_ref, o_ref, do_ref, dq_ref, dk_ref, dv_ref):
        i = pl.program_id(1)
        m0 = _head_masks()
        row = lax.broadcasted_iota(jnp.int32, (BLK, BLK), 0)
        col = lax.broadcasted_iota(jnp.int32, (BLK, BLK), 1)
        strict = col < row
        tri_excl = jnp.where(row > col, 1.0, 0.0).astype(BF16)
        tri_incl = jnp.where(row >= col, 1.0, 0.0).astype(BF16)
        q2 = q_ref[...]
        do2 = do_ref[...]
        zq = jnp.zeros_like(q2)
        qh = (jnp.where(m0, q2, zq), jnp.where(m0, zq, q2))
        doh = (jnp.where(m0, do2, zq), jnp.where(m0, zq, do2))
        prod = do2.astype(F32) * o_ref[...]
        delta = (jnp.sum(jnp.where(m0, prod, 0.0), axis=1, keepdims=True),
                 jnp.sum(jnp.where(m0, 0.0, prod), axis=1, keepdims=True))

        @pl.when(i == 0)
        def _():
            dk_ref[...] = jnp.zeros_like(dk_ref)
            dv_ref[...] = jnp.zeros_like(dv_ref)

        def block(j, cb, cg, dq, diag):
            off = pl.multiple_of(j * BLK, BLK)
            k2 = k_ref[pl.ds(off, BLK), :]
            v2 = v_ref[pl.ds(off, BLK), :]
            kh = (jnp.where(m0, k2, zq), jnp.where(m0, zq, k2))
            dk_blk = jnp.zeros((BLK, LANES), F32)
            dv_blk = jnp.zeros((BLK, LANES), F32)
            ncb, ncg = [], []
            for h in range(2):
                a, b, w = _sb_scores(qh[h], k2, cb[h], diag, tri_excl, strict)
                wb = w.astype(BF16)
                g = _dot(doh[h], v2, NT) * wb.astype(F32)
                ghi, glo = _split_bf16(g)
                ginc = _dot(ghi, tri_incl, NN) + _dot(glo, tri_incl, NN) + cg[h]
                beta = jnp.exp(a)
                dz = g * (1.0 - beta) - beta * (delta[h] - ginc)
                if diag:
                    dz = jnp.where(strict, dz, 0.0)
                dzs = (dz * ATTN_SCALE).astype(BF16)
                dq = dq + _dot(dzs, kh[h], NN)
                dk_blk = dk_blk + _dot(dzs, qh[h], TN)
                dv_blk = dv_blk + _dot(wb, doh[h], TN)
                ncb.append(cb[h] + jnp.sum(b, axis=1, keepdims=True))
                ncg.append(cg[h] + jnp.sum(g, axis=1, keepdims=True))
            dk_ref[pl.ds(off, BLK), :] += dk_blk
            dv_ref[pl.ds(off, BLK), :] += dv_blk
            return ncb, ncg, dq

        def alive(c0, c1):
            return (jnp.maximum(jnp.max(c0), jnp.max(c1)) > SB_LOG_FLOOR).astype(jnp.int32)

        c0 = jnp.zeros((BLK, 1), F32)
        cb, cg, dq = block(i, (c0, c0), (c0, c0), jnp.zeros((BLK, LANES), F32), True)

        def cond(st):
            return jnp.logical_and(st[0] >= 0, st[1] > 0)

        def step(st):
            j, _, b0, b1, g0, g1, dq = st
            (b0, b1), (g0, g1), dq = block(j, (b0, b1), (g0, g1), dq, False)
            return j - 1, alive(b0, b1), b0, b1, g0, g1, dq

        st = lax.while_loop(cond, step, (i - 1, alive(cb[0], cb[1]), cb[0], cb[1], cg[0], cg[1], dq))
        dq_ref[...] = st[6].astype(BF16)

    blk = lambda c: pl.BlockSpec((BLK, LANES), lambda p, i: (i, c * npair + p))
    col_all = lambda c: pl.BlockSpec((S, LANES), lambda p, i: (0, c * npair + p))
    return pl.pallas_call(
        body, name=name,
        out_shape=(jax.ShapeDtypeStruct((S, D), BF16), jax.ShapeDtypeStruct((S, D), F32),
                   jax.ShapeDtypeStruct((S, D), F32)),
        grid=(npair, nb),
        in_specs=[blk(0), col_all(1), col_all(2), blk(0), blk(0)],
        out_specs=(blk(0), col_all(0), col_all(0)),
        compiler_params=_params("arbitrary", "arbitrary"),
    )(qkv, qkv, qkv, o, do)


SWA_Q_GROUPS = 4


def _roll_heads(x):
    return pltpu.roll(x.astype(F32), HEAD_DIM, 1).astype(BF16)


def _swa_valid(i):
    r = lax.broadcasted_iota(jnp.int32, (BLK, 2 * BLK), 0)
    c = lax.broadcasted_iota(jnp.int32, (BLK, 2 * BLK), 1)
    diff = r + BLK - c
    return (diff >= 0) & (diff < BLK) & ((i > 0) | (c >= BLK))


def _swa_probs(qm, ksel, valid, sink):
    z = _dot(qm, ksel, NT) * ATTN_SCALE
    z = jnp.where(valid, z, NEG_BIG)
    mx = jnp.maximum(jnp.max(z, axis=1, keepdims=True), sink)
    p = jnp.exp(z - mx)
    ps = jnp.exp(sink - mx)
    inv = 1.0 / (jnp.sum(p, axis=1, keepdims=True) + ps)
    return p * inv, ps * inv


def _swa_fwd(q, kv, sinks, name):
    S, D = q.shape
    nkvp = kv.shape[1] // (2 * LANES)
    nb = S // BLK
    qw = SWA_Q_GROUPS * LANES

    def body(q_ref, kc_ref, kp_ref, vc_ref, vp_ref, s_ref, o_ref):
        m, i = pl.program_id(0), pl.program_id(1)
        m0 = _head_masks()
        valid = _swa_valid(i)
        kk = jnp.concatenate([kp_ref[...], kc_ref[...]], axis=0)
        vv = jnp.concatenate([vp_ref[...], vc_ref[...]], axis=0)
        ksw, vsw = _roll_heads(kk), _roll_heads(vv)
        m0k = jnp.concatenate([m0, m0], axis=0)
        zv = jnp.zeros_like(vv)
        for c in range(SWA_Q_GROUPS):
            qc = q_ref[:, c * LANES:(c + 1) * LANES]
            zq = jnp.zeros_like(qc)
            w_half = c // 2
            acc = jnp.zeros((BLK, LANES), F32)
            for u in range(2):
                qm = jnp.where(m0, qc, zq) if u == 0 else jnp.where(m0, zq, qc)
                same = u == w_half
                ksel = kk if same else ksw
                vsel = vv if same else vsw
                vsel = jnp.where(m0k, vsel, zv) if u == 0 else jnp.where(m0k, zv, vsel)
                sink = s_ref[0, m * 2 * SWA_Q_GROUPS + 2 * c + u]
                p, _ = _swa_probs(qm, ksel, valid, sink)
                acc = acc + _dot(p.astype(BF16), vsel, NN)
            o_ref[:, c * LANES:(c + 1) * LANES] = acc

    prev = lambda i: jnp.maximum(i - 1, 0)
    return pl.pallas_call(
        body, name=name, out_shape=jax.ShapeDtypeStruct((S, D), F32),
        grid=(nkvp, nb),
        in_specs=[pl.BlockSpec((BLK, qw), lambda m, i: (i, m)),
                  pl.BlockSpec((BLK, LANES), lambda m, i: (i, m)),
                  pl.BlockSpec((BLK, LANES), lambda m, i: (prev(i), m)),
                  pl.BlockSpec((BLK, LANES), lambda m, i: (i, nkvp + m)),
                  pl.BlockSpec((BLK, LANES), lambda m, i: (prev(i), nkvp + m)),
                  pl.BlockSpec(memory_space=pltpu.SMEM)],
        out_specs=pl.BlockSpec((BLK, qw), lambda m, i: (i, m)),
        compiler_params=_params("arbitrary", "arbitrary"),
    )(q, kv, kv, kv, kv, sinks)


def _swa_bwd(q, kv, sinks, o, do, name):
    S, D = q.shape
    nkvp = kv.shape[1] // (2 * LANES)
    nb = S // BLK
    qw = SWA_Q_GROUPS * LANES
    nh = 2 * SWA_Q_GROUPS

    def body(q_ref, kc_ref, kp_ref, vc_ref, vp_ref, s_ref, o_ref, do_ref, dq_ref, dk_ref, dv_ref, ds_ref):
        m, i = pl.program_id(0), pl.program_id(1)
        m0 = _head_masks()
        valid = _swa_valid(i)
        kk = jnp.concatenate([kp_ref[...], kc_ref[...]], axis=0)
        vv = jnp.concatenate([vp_ref[...], vc_ref[...]], axis=0)
        ksw, vsw = _roll_heads(kk), _roll_heads(vv)
        m0k = jnp.concatenate([m0, m0], axis=0)
        zk = jnp.zeros_like(kk)

        @pl.when(i == 0)
        def _():
            dk_ref[...] = jnp.zeros_like(dk_ref)
            dv_ref[...] = jnp.zeros_like(dv_ref)
            ds_ref[...] = jnp.zeros_like(ds_ref)

        acc = {(t, s): jnp.zeros((2 * BLK, LANES), F32) for t in ("k", "v") for s in (True, False)}
        for c in range(SWA_Q_GROUPS):
            qc = q_ref[:, c * LANES:(c + 1) * LANES]
            doc = do_ref[:, c * LANES:(c + 1) * LANES]
            prod = doc.astype(F32) * o_ref[:, c * LANES:(c + 1) * LANES]
            zq = jnp.zeros_like(qc)
            w_half = c // 2
            dq = jnp.zeros((BLK, LANES), F32)
            for u in range(2):
                sel = (lambda x, z: jnp.where(m0, x, z)) if u == 0 else (lambda x, z: jnp.where(m0, z, x))
                selk = (lambda x, z: jnp.where(m0k, x, z)) if u == 0 else (lambda x, z: jnp.where(m0k, z, x))
                qm, dom = sel(qc, zq), sel(doc, zq)
                same = u == w_half
                ksel = kk if same else ksw
                vsel = vv if same else vsw
                hh = 2 * c + u
                sink = s_ref[0, m * nh + hh]
                p, ps = _swa_probs(qm, ksel, valid, sink)
                delta = jnp.sum(sel(prod, 0.0), axis=1, keepdims=True)
                dp = _dot(dom, vsel, NT)
                dsc = (p * (dp - delta) * ATTN_SCALE).astype(BF16)
                dq = dq + _dot(dsc, selk(ksel, zk), NN)
                acc["k", same] = acc["k", same] + _dot(dsc, qm, TN)
                acc["v", same] = acc["v", same] + _dot(p.astype(BF16), dom, TN)
                dsink = jnp.sum(jnp.broadcast_to(-(ps * delta), (BLK, LANES)), axis=0, keepdims=True)
                ds_ref[0, hh:hh + 1, :] += dsink
            dq_ref[:, c * LANES:(c + 1) * LANES] = dq
        dkk = acc["k", True] + pltpu.roll(acc["k", False], HEAD_DIM, 1)
        dvv = acc["v", True] + pltpu.roll(acc["v", False], HEAD_DIM, 1)
        poff = pl.multiple_of(jnp.maximum(i - 1, 0) * BLK, BLK)
        coff = pl.multiple_of(i * BLK, BLK)
        dk_ref[pl.ds(poff, BLK), :] += dkk[:BLK]
        dv_ref[pl.ds(poff, BLK), :] += dvv[:BLK]
        dk_ref[pl.ds(coff, BLK), :] += dkk[BLK:]
        dv_ref[pl.ds(coff, BLK), :] += dvv[BLK:]

    prev = lambda i: jnp.maximum(i - 1, 0)
    qblk = pl.BlockSpec((BLK, qw), lambda m, i: (i, m))
    col_all = pl.BlockSpec((S, LANES), lambda m, i: (0, m))
    return pl.pallas_call(
        body, name=name,
        out_shape=(jax.ShapeDtypeStruct((S, D), F32),
                   jax.ShapeDtypeStruct((S, nkvp * LANES), F32),
                   jax.ShapeDtypeStruct((S, nkvp * LANES), F32),
                   jax.ShapeDtypeStruct((nkvp, nh, LANES), F32)),
        grid=(nkvp, nb),
        in_specs=[qblk,
                  pl.BlockSpec((BLK, LANES), lambda m, i: (i, m)),
                  pl.BlockSpec((BLK, LANES), lambda m, i: (prev(i), m)),
                  pl.BlockSpec((BLK, LANES), lambda m, i: (i, nkvp + m)),
                  pl.BlockSpec((BLK, LANES), lambda m, i: (prev(i), nkvp + m)),
                  pl.BlockSpec(memory_space=pltpu.SMEM),
                  qblk, qblk],
        out_specs=(qblk, col_all, col_all, pl.BlockSpec((1, nh, LANES), lambda m, i: (m, 0, 0))),
        compiler_params=_params("arbitrary", "arbitrary"),
    )(q, kv, kv, kv, kv, sinks, o, do)


def _all_gather(x, name):
    R, C = x.shape

    def body(x_ref, out_ref, send_sems, recv_sems, local_sem):
        x_, y_, c_ = lax.axis_index("x"), lax.axis_index("y"), lax.axis_index("c")
        me, sibling = (x_, y_, c_), (x_, y_, 1 - c_)
        chips = [(1 - x_, y_), (x_, 1 - y_), (1 - x_, 1 - y_)]

        def rows(px, py, pc):
            return out_ref.at[4 * px + 2 * py + pc]

        def copy(k, block, to, src=None):
            return pltpu.make_async_remote_copy(
                src_ref=rows(*block) if src is None else src, dst_ref=rows(*block),
                send_sem=send_sems.at[k], recv_sem=recv_sems.at[k],
                device_id=to, device_id_type=MESH)

        mine = pltpu.make_async_copy(x_ref, rows(*me), local_sem)
        mine.start()
        first = [copy(0, me, sibling, src=x_ref)]
        first += [copy(1 + j, me, (*chip, c_), src=x_ref) for j, chip in enumerate(chips)]
        for cp in first:
            cp.start()
        passed = [copy(4 + j, (*chip, c_), sibling) for j, chip in enumerate(chips)]
        for j, chip in enumerate(chips):
            copy(1 + j, (*chip, c_), me).wait_recv()
            passed[j].start()
        copy(0, sibling, me).wait_recv()
        for j, chip in enumerate(chips):
            copy(4 + j, (*chip, 1 - c_), me).wait_recv()
        for cp in first + passed:
            cp.wait_send()
        mine.wait()

    return pl.pallas_call(
        body, name=name, out_shape=jax.ShapeDtypeStruct((N_DEV, R, C), x.dtype),
        in_specs=[pl.BlockSpec(memory_space=pl.ANY)], out_specs=pl.BlockSpec(memory_space=pl.ANY),
        scratch_shapes=[pltpu.SemaphoreType.DMA((7,)), pltpu.SemaphoreType.DMA((7,)), pltpu.SemaphoreType.DMA],
    )(x)


def _all_to_all(blocks, name):
    _, R, C = blocks.shape

    def body(b_ref, out_ref, send_sems, recv_sems, local_sem):
        x_, y_, c_ = lax.axis_index("x"), lax.axis_index("y"), lax.axis_index("c")
        my_idx = 4 * x_ + 2 * y_ + c_
        mine = pltpu.make_async_copy(b_ref.at[my_idx], out_ref.at[my_idx], local_sem)
        mine.start()
        copies = []
        for k in range(1, N_DEV):
            px = x_ ^ ((k >> 2) & 1)
            py = y_ ^ ((k >> 1) & 1)
            pc = c_ ^ (k & 1)
            copies.append(pltpu.make_async_remote_copy(
                src_ref=b_ref.at[4 * px + 2 * py + pc], dst_ref=out_ref.at[my_idx],
                send_sem=send_sems.at[k - 1], recv_sem=recv_sems.at[k - 1],
                device_id=(px, py, pc), device_id_type=MESH))
        for cp in copies:
            cp.start()
        for cp in copies:
            cp.wait_recv()
        for cp in copies:
            cp.wait_send()
        mine.wait()

    return pl.pallas_call(
        body, name=name, out_shape=jax.ShapeDtypeStruct((N_DEV, R, C), blocks.dtype),
        in_specs=[pl.BlockSpec(memory_space=pl.ANY)], out_specs=pl.BlockSpec(memory_space=pl.ANY),
        scratch_shapes=[pltpu.SemaphoreType.DMA((7,)), pltpu.SemaphoreType.DMA((7,)), pltpu.SemaphoreType.DMA],
    )(blocks)


def _sum_adamw(parts, w, m, v, name):
    _, R, C = parts.shape
    tr = _tile(R, 256, 16)
    c1 = 1.0 - ADAM_B1 ** ADAM_STEP
    c2 = 1.0 - ADAM_B2 ** ADAM_STEP

    def body(p_ref, w_ref, m_ref, v_ref, g_ref, d_ref, nm_ref, nv_ref):
        g = p_ref[0].astype(F32)
        for s in range(1, N_DEV):
            g = g + p_ref[s].astype(F32)
        nm = ADAM_B1 * m_ref[...] + (1.0 - ADAM_B1) * g
        nv = ADAM_B2 * v_ref[...] + (1.0 - ADAM_B2) * (g * g)
        m_hat = nm / c1
        v_hat = nv / c2
        g_ref[...] = g
        nm_ref[...] = nm
        nv_ref[...] = nv
        d_ref[...] = -ADAM_LR * (m_hat / (jnp.sqrt(v_hat) + ADAM_EPS) + ADAM_WD * w_ref[...])

    row = pl.BlockSpec((tr, C), lambda i: (i, 0))
    shp = jax.ShapeDtypeStruct((R, C), F32)
    return pl.pallas_call(
        body, name=name, out_shape=(shp, shp, shp, shp),
        grid=(R // tr,),
        in_specs=[pl.BlockSpec((N_DEV, tr, C), lambda i: (0, i, 0)), row, row, row],
        out_specs=(row, row, row, row),
        compiler_params=_params("parallel"),
    )(parts, w, m, v)


def _ffn_fwd(h, g, wg, wu, wo, tag):
    xn = _rmsnorm(h, g, f"{tag}_norm")
    gate = _mm(xn, wg, NN, BF16, f"{tag}_gate", tm=1024, tn=1408, tk=1024)
    up = _mm(xn, wu, NN, BF16, f"{tag}_up", tm=1024, tn=1408, tk=1024)
    act = _swiglu(gate, up, f"{tag}_act")
    out = _mm(act, wo, NN, F32, f"{tag}_down", scale=FFN_RES_SCALE, res=h, tm=512, tn=1024, tk=2816)
    return out, (xn, gate, up, act)


def _ffn_bwd(dh, h, g, wg, wu, wo, saved, tag):
    xn, gate, up, act = saved
    dact = _mm(dh, wo, NT, BF16, f"{tag}_dact", scale=FFN_RES_SCALE, tm=1024, tn=1408, tk=1024)
    dwo = _mm(act, dh, TN, BF16, f"{tag}_dwo", scale=FFN_RES_SCALE, tm=1408, tn=1024, tk=512)
    dgate, dup = _swiglu_bwd(gate, up, dact, f"{tag}_dswiglu")
    dwg = _mm(xn, dgate, TN, BF16, f"{tag}_dwg", tm=1024, tn=1408, tk=512)
    dwu = _mm(xn, dup, TN, BF16, f"{tag}_dwu", tm=1024, tn=1408, tk=512)
    dxn = _mm(dgate, wg, NT, F32, f"{tag}_dxn_g", tm=512, tn=1024, tk=2816)
    dxn = _mm(dup, wu, NT, F32, f"{tag}_dxn_u", res=dxn, tm=512, tn=1024, tk=2816)
    dh_in, dg = _rmsnorm_bwd(dxn, h, g, dh, f"{tag}_dnorm")
    return dh_in, dg, dwg, dwu, dwo


def _proj(a, w, out_dtype, name, res=None):
    return _mm(a, w, NN, out_dtype, name, res=res, tm=1024, tn=1024, tk=1024)


def _proj_bwd(x, dy, w, dx_dtype, tag):
    dx = _mm(dy, w, NT, dx_dtype, f"{tag}_dx", tm=1024, tn=1024, tk=1024)
    dw = _mm(x, dy, TN, BF16, f"{tag}_dw", tm=1024, tn=1024, tk=512)
    return dx, dw


def _col_shards_to_full(g, lead):
    nl = len(lead)
    perm = tuple(range(1, nl + 2)) + (0, nl + 2)
    t = jnp.transpose(g, perm)
    return t.reshape(t.shape[:nl + 1] + (t.shape[nl + 1] * t.shape[nl + 2],))


def _full_to_col_shards(w):
    nl = w.ndim - 2
    t = w.reshape(w.shape[:-1] + (N_DEV, w.shape[-1] // N_DEV))
    perm = (nl + 1,) + tuple(range(nl + 1)) + (nl + 2,)
    return jnp.transpose(t, perm)


def kernel(x, ffn1_norm, ffn1_w_in, ffn1_w_out, mix_norm, ffn2_norm, ffn2_w_in, ffn2_w_out, sb_w_qkv, sb_w_o, kv_norm, kv_w, swa_w_q, swa_sinks, swa_w_o, final_norm, loss_target, m_ffn1_norm, m_ffn1_w_in, m_ffn1_w_out, m_mix_norm, m_ffn2_norm, m_ffn2_w_in, m_ffn2_w_out, m_sb_w_qkv, m_sb_w_o, m_kv_norm, m_kv_w, m_swa_w_q, m_swa_sinks, m_swa_w_o, m_final_norm, v_ffn1_norm, v_ffn1_w_in, v_ffn1_w_out, v_mix_norm, v_ffn2_norm, v_ffn2_w_in, v_ffn2_w_out, v_sb_w_qkv, v_sb_w_o, v_kv_norm, v_kv_w, v_swa_w_q, v_swa_sinks, v_swa_w_o, v_final_norm):
    S, D = x.shape[1], x.shape[2]
    F = ffn1_w_out.shape[1] * N_DEV
    L = ffn1_w_in.shape[0]
    KV = kv_w.shape[1]
    assert L == 2 and D % PACK_W == 0 and swa_sinks.shape == (1, 2 * SWA_Q_GROUPS * KV // (2 * LANES))

    big_w = [ffn1_w_in, ffn1_w_out, ffn2_w_in, ffn2_w_out, sb_w_qkv, sb_w_o, kv_w, swa_w_q, swa_w_o]
    big_m = [m_ffn1_w_in, m_ffn1_w_out, m_ffn2_w_in, m_ffn2_w_out, m_sb_w_qkv, m_sb_w_o, m_kv_w, m_swa_w_q, m_swa_w_o]
    big_v = [v_ffn1_w_in, v_ffn1_w_out, v_ffn2_w_in, v_ffn2_w_out, v_sb_w_qkv, v_sb_w_o, v_kv_w, v_swa_w_q, v_swa_w_o]
    rows = [w.size // PACK_W for w in big_w]
    offs = [sum(rows[:i]) for i in range(len(rows))]

    def pack(ts, dtype):
        return jnp.concatenate([t.astype(dtype).reshape(-1, PACK_W) for t in ts], axis=0)

    gath = _all_gather(pack(big_w, BF16), "gather_weights")

    def part(i, shape):
        return gath[:, offs[i]:offs[i] + rows[i]].reshape((N_DEV,) + shape)

    w_in1 = _col_shards_to_full(part(0, ffn1_w_in.shape), (L,))
    w_out1 = jnp.transpose(part(1, ffn1_w_out.shape), (1, 0, 2, 3)).reshape(L, F, D)
    w_in2 = _col_shards_to_full(part(2, ffn2_w_in.shape), (L,))
    w_out2 = jnp.transpose(part(3, ffn2_w_out.shape), (1, 0, 2, 3)).reshape(L, F, D)
    w_qkv = _col_shards_to_full(part(4, sb_w_qkv.shape[1:]), ())
    w_sbo = part(5, sb_w_o.shape[1:]).reshape(D, D)
    w_kv = part(6, kv_w.shape).reshape(D, KV)
    w_q = part(7, swa_w_q.shape[1:]).reshape(D, D)
    w_swo = part(8, swa_w_o.shape[1:]).reshape(D, D)

    def ffn_weights(w_in, w_out, layer):
        return w_in[layer, :, :F], w_in[layer, :, F:], w_out[layer]

    cos_t, sin_t = _rope_tables(S)
    h0 = x.reshape(S, D)
    tgt = loss_target.reshape(S, D)

    fw = [ffn_weights(w_in1, w_out1, 0), ffn_weights(w_in2, w_out2, 0),
          ffn_weights(w_in1, w_out1, 1), ffn_weights(w_in2, w_out2, 1)]
    h1, sv_a1 = _ffn_fwd(h0, ffn1_norm[0], *fw[0], "ffn1a")
    hn_a = _rmsnorm(h1, mix_norm[0], "mix_a_norm")
    qkv = _proj(hn_a, w_qkv, BF16, "sb_qkv")
    o_sb = _sb_fwd(qkv, "sb_attn")
    h2 = _proj(o_sb, w_sbo, F32, "sb_out", res=h1)
    h3, sv_a2 = _ffn_fwd(h2, ffn2_norm[0], *fw[1], "ffn2a")
    kvn = _rmsnorm(h3, kv_norm, "kv_norm")
    kv_raw = _proj(kvn, w_kv, F32, "kv_proj")
    kv_rot = _rotary(kv_raw, cos_t, sin_t, KV // (2 * LANES), False, "kv_rope")
    h4, sv_b1 = _ffn_fwd(h3, ffn1_norm[1], *fw[2], "ffn1b")
    hn_b = _rmsnorm(h4, mix_norm[1], "mix_b_norm")
    q_raw = _proj(hn_b, w_q, F32, "swa_q")
    q_rot = _rotary(q_raw, cos_t, sin_t, D // LANES, False, "q_rope")
    o_sw = _swa_fwd(q_rot, kv_rot, swa_sinks, "swa_attn")
    h5 = _proj(o_sw, w_swo, F32, "swa_out", res=h4)
    h6, sv_b2 = _ffn_fwd(h5, ffn2_norm[1], *fw[3], "ffn2b")
    dh6, dg_final, sq_err = _final_loss(h6, final_norm, tgt, "final_loss")
    loss = lax.psum(0.5 * jnp.sum(sq_err) / D, ("x", "y", "c"))

    dh5, dg_f2b, dwg_f2b, dwu_f2b, dwo_f2b = _ffn_bwd(dh6, h5, ffn2_norm[1], *fw[3], sv_b2, "ffn2b")
    do_sw, dw_swo = _proj_bwd(o_sw, dh5, w_swo, BF16, "swa_out")
    dq_rot, dk_sw, dv_sw, dsink = _swa_bwd(q_rot, kv_rot, swa_sinks, o_sw, do_sw, "swa_attn_bwd")
    dq = _rotary(dq_rot, cos_t, sin_t, D // LANES, True, "q_rope_bwd")
    dhn_b, dw_q = _proj_bwd(hn_b, dq, w_q, F32, "swa_q")
    dh4, dg_mix_b = _rmsnorm_bwd(dhn_b, h4, mix_norm[1], dh5, "mix_b_dnorm")
    dh3, dg_f1b, dwg_f1b, dwu_f1b, dwo_f1b = _ffn_bwd(dh4, h3, ffn1_norm[1], *fw[2], sv_b1, "ffn1b")
    dkv = _rotary(jnp.concatenate([dk_sw, dv_sw], axis=1), cos_t, sin_t, KV // (2 * LANES), True, "kv_rope_bwd")
    dkvn, dw_kv = _proj_bwd(kvn, dkv, w_kv, F32, "kv_proj")
    dh3, dg_kv = _rmsnorm_bwd(dkvn, h3, kv_norm, dh3, "kv_dnorm")
    dh2, dg_f2a, dwg_f2a, dwu_f2a, dwo_f2a = _ffn_bwd(dh3, h2, ffn2_norm[0], *fw[1], sv_a2, "ffn2a")
    do_sb, dw_sbo = _proj_bwd(o_sb, dh2, w_sbo, BF16, "sb_out")
    dq_sb, dk_sb, dv_sb = _sb_bwd(qkv, o_sb, do_sb, "sb_attn_bwd")
    dqkv = jnp.concatenate([dq_sb, dk_sb.astype(BF16), dv_sb.astype(BF16)], axis=1)
    dhn_a, dw_qkv = _proj_bwd(hn_a, dqkv, w_qkv, F32, "sb_qkv")
    dh1, dg_mix_a = _rmsnorm_bwd(dhn_a, h1, mix_norm[0], dh2, "mix_a_dnorm")
    dx, dg_f1a, dwg_f1a, dwu_f1a, dwo_f1a = _ffn_bwd(dh1, h0, ffn1_norm[0], *fw[0], sv_a1, "ffn1a")

    def w_in_blocks(dwg_a, dwu_a, dwg_b, dwu_b):
        full = jnp.stack([jnp.concatenate([dwg_a, dwu_a], axis=1), jnp.concatenate([dwg_b, dwu_b], axis=1)])
        return _full_to_col_shards(full)

    def w_out_blocks(dwo_a, dwo_b):
        full = jnp.stack([dwo_a, dwo_b]).reshape(L, N_DEV, F // N_DEV, D)
        return jnp.transpose(full, (1, 0, 2, 3))

    blocks = [w_in_blocks(dwg_f1a, dwu_f1a, dwg_f1b, dwu_f1b), w_out_blocks(dwo_f1a, dwo_f1b),
              w_in_blocks(dwg_f2a, dwu_f2a, dwg_f2b, dwu_f2b), w_out_blocks(dwo_f2a, dwo_f2b),
              _full_to_col_shards(dw_qkv), dw_sbo, dw_kv, dw_q, dw_swo]
    gblocks = jnp.concatenate([b.reshape(N_DEV, -1, PACK_W) for b in blocks], axis=1)
    parts = _all_to_all(gblocks, "scatter_grads")
    g_big, d_big, nm_big, nv_big = _sum_adamw(parts, pack(big_w, F32), pack(big_m, F32), pack(big_v, F32), "adamw_big")

    def unpack(flat):
        return [flat[offs[i]:offs[i] + rows[i]].reshape(big_w[i].shape) for i in range(len(big_w))]

    small_w = [ffn1_norm, mix_norm, ffn2_norm, kv_norm, final_norm, swa_sinks]
    small_m = [m_ffn1_norm, m_mix_norm, m_ffn2_norm, m_kv_norm, m_final_norm, m_swa_sinks]
    small_v = [v_ffn1_norm, v_mix_norm, v_ffn2_norm, v_kv_norm, v_final_norm, v_swa_sinks]
    SMALL_ROWS = 16

    def pack_small(ts):
        rows_ = [t.reshape(-1, D) for t in ts[:-1]]
        sink_row = jnp.pad(ts[-1].reshape(1, -1), ((0, 0), (0, D - ts[-1].size)))
        flat = jnp.concatenate(rows_ + [sink_row], axis=0)
        return jnp.pad(flat, ((0, SMALL_ROWS - flat.shape[0]), (0, 0)))

    def gain(parts8):
        return jnp.sum(parts8, axis=0, keepdims=True)

    g_small_local = pack_small([
        jnp.concatenate([gain(dg_f1a), gain(dg_f1b)], axis=0),
        jnp.concatenate([gain(dg_mix_a), gain(dg_mix_b)], axis=0),
        jnp.concatenate([gain(dg_f2a), gain(dg_f2b)], axis=0),
        gain(dg_kv), gain(dg_final), dsink[:, :, 0].reshape(1, -1)])
    small_parts = _all_gather(g_small_local, "gather_small_grads")
    g_sm, d_sm, nm_sm, nv_sm = _sum_adamw(small_parts, pack_small(small_w), pack_small(small_m), pack_small(small_v), "adamw_small")

    def unpack_small(flat):
        out, r = [], 0
        for t in small_w[:-1]:
            n = t.size // D
            out.append(flat[r:r + n].reshape(t.shape))
            r += n
        out.append(flat[r, :swa_sinks.size].reshape(swa_sinks.shape))
        return out

    def ordered(big, small):
        f1w_in, f1w_out, f2w_in, f2w_out, qkv_, sbo_, kvw_, swq_, swo_ = big
        f1n, mixn, f2n, kvn_, finn, sinks_ = small
        return [f1n, f1w_in, f1w_out, mixn, f2n, f2w_in, f2w_out, qkv_, sbo_, kvn_, kvw_, swq_, sinks_, swo_, finn]

    outs = []
    for big_flat, small_flat in ((g_big, g_sm), (d_big, d_sm), (nm_big, nm_sm), (nv_big, nv_sm)):
        outs += ordered(unpack(big_flat), unpack_small(small_flat))
    return (loss, dx.reshape(x.shape), *outs)
```

```python
import jax
import jax.numpy as jnp
from jax import lax
from jax.experimental import pallas as pl
from jax.experimental.pallas import tpu as pltpu

F32 = jnp.float32
BF16 = jnp.bfloat16

N_DEV = 8
HEAD_DIM = 64
LANES = 128
BLK = 128
PACK_W = 1024
RMS_EPS = 1e-6
FFN_RES_SCALE = 0.5
ROPE_THETA = 10000.0
ATTN_SCALE = HEAD_DIM ** -0.5
SB_LOG_FLOOR = -110.0
NEG_BIG = -1e30
VMEM_LIMIT_V7X = 56 * 1024 * 1024

ADAM_LR = 0.001
ADAM_B1 = 0.9
ADAM_B2 = 0.999
ADAM_EPS = 1e-08
ADAM_WD = 0.01
ADAM_STEP = 10

NN = ((1,), (0,))
NT = ((1,), (1,))
TN = ((0,), (0,))
MESH = pl.DeviceIdType.MESH


def _dot(a, b, dims):
    return lax.dot_general(a, b, (dims, ((), ())), preferred_element_type=F32)


def _tile(n, pref, mult=LANES):
    if n <= pref:
        return n
    t = (pref // mult) * mult
    while t >= mult:
        if n % t == 0:
            return t
        t -= mult
    return n


def _params(*sem):
    return pltpu.CompilerParams(dimension_semantics=sem, vmem_limit_bytes=VMEM_LIMIT_V7X)


def _mm(a, b, dims, out_dtype, name, scale=1.0, res=None, tm=512, tn=512, tk=512):
    if dims == NN:
        (M, K), (_, N) = a.shape, b.shape
    elif dims == NT:
        (M, K), (N, _) = a.shape, b.shape
    else:
        (K, M), (_, N) = a.shape, b.shape
    tm, tn, tk = _tile(M, tm), _tile(N, tn), _tile(K, tk)
    nk = K // tk
    if dims == TN:
        a_spec = pl.BlockSpec((tk, tm), lambda i, j, k: (k, i))
    else:
        a_spec = pl.BlockSpec((tm, tk), lambda i, j, k: (i, k))
    if dims == NT:
        b_spec = pl.BlockSpec((tn, tk), lambda i, j, k: (j, k))
    else:
        b_spec = pl.BlockSpec((tk, tn), lambda i, j, k: (k, j))
    o_spec = pl.BlockSpec((tm, tn), lambda i, j, k: (i, j))
    has_res = res is not None

    def body(*refs):
        a_ref, b_ref = refs[0], refs[1]
        r_ref = refs[2] if has_res else None
        o_ref = refs[3] if has_res else refs[2]

        def finish(acc):
            r = acc * scale if scale != 1.0 else acc
            if has_res:
                r = r + r_ref[...]
            o_ref[...] = r.astype(out_dtype)

        p = _dot(a_ref[...].astype(BF16), b_ref[...].astype(BF16), dims)
        if nk == 1:
            finish(p)
        else:
            acc_ref = refs[-1]
            k = pl.program_id(2)

            @pl.when(k == 0)
            def _():
                acc_ref[...] = p

            @pl.when(k > 0)
            def _():
                acc_ref[...] += p

            @pl.when(k == nk - 1)
            def _():
                finish(acc_ref[...])

    in_specs = [a_spec, b_spec] + ([o_spec] if has_res else [])
    args = (a, b) + ((res,) if has_res else ())
    return pl.pallas_call(
        body, name=name,
        out_shape=jax.ShapeDtypeStruct((M, N), out_dtype),
        grid=(M // tm, N // tn, nk),
        in_specs=in_specs, out_specs=o_spec,
        scratch_shapes=[pltpu.VMEM((tm, tn), F32)] if nk > 1 else [],
        compiler_params=_params("parallel", "parallel", "arbitrary"),
    )(*args)


def _rows8(x):
    r, d = x.shape
    return jnp.sum(x.reshape(r // 8, 8, d), axis=0)


def _rmsnorm(h, g, name):
    S, D = h.shape
    ts = _tile(S, 512, 8)

    def body(h_ref, g_ref, o_ref):
        x = h_ref[...]
        r = lax.rsqrt(jnp.mean(x * x, axis=-1, keepdims=True) + RMS_EPS)
        o_ref[...] = ((x * r) * g_ref[...]).astype(BF16)

    return pl.pallas_call(
        body, name=name,
        out_shape=jax.ShapeDtypeStruct((S, D), BF16),
        grid=(S // ts,),
        in_specs=[pl.BlockSpec((ts, D), lambda i: (i, 0)), pl.BlockSpec((1, D), lambda i: (0, 0))],
        out_specs=pl.BlockSpec((ts, D), lambda i: (i, 0)),
        compiler_params=_params("parallel"),
    )(h, g.reshape(1, D))


def _final_loss(h, g, tgt, name):
    S, D = h.shape
    ts = _tile(S, 512, 8)

    def body(h_ref, g_ref, t_ref, dh_ref, dg_ref, l_ref):
        x = h_ref[...]
        r = lax.rsqrt(jnp.mean(x * x, axis=-1, keepdims=True) + RMS_EPS)
        xhat = x * r
        err = xhat * g_ref[...] - t_ref[...]
        d = err * (1.0 / D)
        dxh = d * g_ref[...]
        c = jnp.mean(dxh * xhat, axis=-1, keepdims=True)
        dh_ref[...] = r * (dxh - xhat * c)
        part = _rows8(d * xhat)
        lpart = _rows8(err * err)

        @pl.when(pl.program_id(0) == 0)
        def _():
            dg_ref[...] = part
            l_ref[...] = lpart

        @pl.when(pl.program_id(0) > 0)
        def _():
            dg_ref[...] += part
            l_ref[...] += lpart

    row = pl.BlockSpec((ts, D), lambda i: (i, 0))
    acc = pl.BlockSpec((8, D), lambda i: (0, 0))
    return pl.pallas_call(
        body, name=name,
        out_shape=(jax.ShapeDtypeStruct((S, D), F32), jax.ShapeDtypeStruct((8, D), F32),
                   jax.ShapeDtypeStruct((8, D), F32)),
        grid=(S // ts,),
        in_specs=[row, pl.BlockSpec((1, D), lambda i: (0, 0)), row],
        out_specs=(row, acc, acc),
        compiler_params=_params("arbitrary"),
    )(h, g.reshape(1, D), tgt)


def _ffn_up(h, g, wg, wu, name):
    S, D = h.shape
    F = wg.shape[1]
    tm, tn = _tile(S, 512, 16), _tile(F, 1408)

    def body(h_ref, g_ref, wg_ref, wu_ref, xn_ref, gate_ref, up_ref, act_ref):
        x = h_ref[...]
        r = lax.rsqrt(jnp.mean(x * x, axis=-1, keepdims=True) + RMS_EPS)
        xn = ((x * r) * g_ref[...]).astype(BF16)
        xn_ref[...] = xn
        gate = _dot(xn, wg_ref[...], NN)
        up = _dot(xn, wu_ref[...], NN)
        gate_ref[...] = gate.astype(BF16)
        up_ref[...] = up.astype(BF16)
        sig = 1.0 / (1.0 + jnp.exp(-gate))
        act_ref[...] = (gate * sig * up).astype(BF16)

    row = pl.BlockSpec((tm, D), lambda i, j: (i, 0))
    wcol = pl.BlockSpec((D, tn), lambda i, j: (0, j))
    blk = pl.BlockSpec((tm, tn), lambda i, j: (i, j))
    hid = jax.ShapeDtypeStruct((S, F), BF16)
    return pl.pallas_call(
        body, name=name, out_shape=(jax.ShapeDtypeStruct((S, D), BF16), hid, hid, hid),
        grid=(S // tm, F // tn),
        in_specs=[row, pl.BlockSpec((1, D), lambda i, j: (0, 0)), wcol, wcol],
        out_specs=(row, blk, blk, blk),
        compiler_params=_params("arbitrary", "arbitrary"),
    )(h, g.reshape(1, D), wg, wu)


def _ffn_dact(dh, wo, gate, up, name):
    S, D = dh.shape
    F = wo.shape[0]
    tm, tn = _tile(S, 512, 16), _tile(F, 1408)

    def body(dh_ref, wo_ref, g_ref, u_ref, dg_ref, du_ref):
        d = _dot(dh_ref[...].astype(BF16), wo_ref[...], NT) * FFN_RES_SCALE
        g = g_ref[...].astype(F32)
        u = u_ref[...].astype(F32)
        sig = 1.0 / (1.0 + jnp.exp(-g))
        du_ref[...] = (d * (g * sig)).astype(BF16)
        dg_ref[...] = (d * u * (sig * (1.0 + g * (1.0 - sig)))).astype(BF16)

    blk = pl.BlockSpec((tm, tn), lambda j, i: (i, j))
    hid = jax.ShapeDtypeStruct((S, F), BF16)
    return pl.pallas_call(
        body, name=name, out_shape=(hid, hid),
        grid=(F // tn, S // tm),
        in_specs=[pl.BlockSpec((tm, D), lambda j, i: (i, 0)), pl.BlockSpec((tn, D), lambda j, i: (j, 0)), blk, blk],
        out_specs=(blk, blk),
        compiler_params=_params("arbitrary", "arbitrary"),
    )(dh, wo, gate, up)


def _dx_norm_bwd(pairs, h, g, res, name):
    S, D = h.shape
    tm = _tile(S, 256, 16)
    n = len(pairs)

    def body(*refs):
        dy_refs, w_refs = refs[:n], refs[n:2 * n]
        h_ref, g_ref, r_ref, dh_ref, dg_ref = refs[2 * n:]
        d = _dot(dy_refs[0][...], w_refs[0][...], NT)
        for t in range(1, n):
            d = d + _dot(dy_refs[t][...], w_refs[t][...], NT)
        x = h_ref[...]
        r = lax.rsqrt(jnp.mean(x * x, axis=-1, keepdims=True) + RMS_EPS)
        xhat = x * r
        dxh = d * g_ref[...]
        c = jnp.mean(dxh * xhat, axis=-1, keepdims=True)
        dh_ref[...] = r * (dxh - xhat * c) + r_ref[...]
        part = _rows8(d * xhat)

        @pl.when(pl.program_id(0) == 0)
        def _():
            dg_ref[...] = part

        @pl.when(pl.program_id(0) > 0)
        def _():
            dg_ref[...] += part

    row = pl.BlockSpec((tm, D), lambda i: (i, 0))
    in_specs = [pl.BlockSpec((tm, dy.shape[1]), lambda i: (i, 0)) for dy, _ in pairs]
    in_specs += [pl.BlockSpec(w.shape, lambda i: (0, 0)) for _, w in pairs]
    in_specs += [row, pl.BlockSpec((1, D), lambda i: (0, 0)), row]
    return pl.pallas_call(
        body, name=name,
        out_shape=(jax.ShapeDtypeStruct((S, D), F32), jax.ShapeDtypeStruct((8, D), F32)),
        grid=(S // tm,),
        in_specs=in_specs,
        out_specs=(row, pl.BlockSpec((8, D), lambda i: (0, 0))),
        compiler_params=_params("arbitrary"),
    )(*[dy for dy, _ in pairs], *[w for _, w in pairs], h, g.reshape(1, D), res)


def _rope_tables(S):
    half = HEAD_DIM // 2
    inv_freq = ROPE_THETA ** (-jnp.arange(half, dtype=F32) / half)
    ang = jnp.arange(S).astype(F32)[:, None] * inv_freq[None, :]
    cos, sin = jnp.cos(ang), jnp.sin(ang)
    cos_t = jnp.tile(cos, (1, LANES // half))
    sin_t = jnp.tile(jnp.concatenate([-sin, sin], axis=1), (1, LANES // HEAD_DIM))
    return cos_t, sin_t


def _swap_halves(x):
    lane = lax.broadcasted_iota(jnp.int32, x.shape, 1)
    first = (lane % HEAD_DIM) < (HEAD_DIM // 2)
    return jnp.where(first, pltpu.roll(x, LANES - HEAD_DIM // 2, 1), pltpu.roll(x, HEAD_DIM // 2, 1))


def _rotary(x, cos_t, sin_t, n_rot, inverse, name):
    S, C = x.shape
    ts = _tile(S, 512, 16)
    ng = C // LANES

    def body(x_ref, c_ref, s_ref, o_ref):
        cs, sn = c_ref[...], s_ref[...]
        for gidx in range(ng):
            sl = slice(gidx * LANES, (gidx + 1) * LANES)
            v = x_ref[:, sl].astype(F32)
            if gidx < n_rot:
                if inverse:
                    v = v * cs + _swap_halves(v * sn)
                else:
                    v = v * cs + _swap_halves(v) * sn
            o_ref[:, sl] = v.astype(BF16)

    row = pl.BlockSpec((ts, C), lambda i: (i, 0))
    tab = pl.BlockSpec((ts, LANES), lambda i: (i, 0))
    return pl.pallas_call(
        body, name=name, out_shape=jax.ShapeDtypeStruct((S, C), BF16),
        grid=(S // ts,), in_specs=[row, tab, tab], out_specs=row,
        compiler_params=_params("parallel"),
    )(x, cos_t, sin_t)


def _head_masks():
    lane = lax.broadcasted_iota(jnp.int32, (BLK, LANES), 1)
    return lane < HEAD_DIM


def _split_bf16(x):
    hi = x.astype(BF16)
    lo = (x - hi.astype(F32)).astype(BF16)
    return hi, lo


def _sb_scores(qh, k2, carry_h, diag, tri_excl, strict):
    z = _dot(qh, k2, NT) * ATTN_SCALE
    a = jnp.minimum(z, 0.0) - jnp.log(1.0 + jnp.exp(-jnp.abs(z)))
    b = a - z
    if diag:
        b = jnp.where(strict, b, 0.0)
    bhi, blo = _split_bf16(b)
    suf = _dot(bhi, tri_excl, NN) + _dot(blo, tri_excl, NN)
    w = jnp.exp(a + suf + carry_h)
    if diag:
        w = jnp.where(strict, w, 0.0)
    return a, b, w


SB_FWD_PAIRS = 4
SB_BWD_PAIRS = 2


def _any_alive(carries):
    top = carries[0]
    for c in carries[1:]:
        top = jnp.maximum(top, c)
    return (jnp.max(top) > SB_LOG_FLOOR).astype(jnp.int32)


def _sb_fwd(qkv, name):
    S, D3 = qkv.shape
    D = D3 // 3
    npair, nb = D // LANES, S // BLK
    P = min(SB_FWD_PAIRS, npair)
    ngroup = npair // P
    W = P * LANES

    def body(q_ref, k_ref, v_ref, o_ref):
        i = pl.program_id(1)
        m0 = _head_masks()
        row = lax.broadcasted_iota(jnp.int32, (BLK, BLK), 0)
        col = lax.broadcasted_iota(jnp.int32, (BLK, BLK), 1)
        strict = col < row
        tri_excl = jnp.where(row > col, 1.0, 0.0).astype(BF16)
        zq = jnp.zeros((BLK, LANES), BF16)
        qh = []
        for p in range(P):
            q2 = q_ref[:, p * LANES:(p + 1) * LANES]
            qh += [jnp.where(m0, q2, zq), jnp.where(m0, zq, q2)]

        def block(j, carry, acc, diag):
            off = pl.multiple_of(j * BLK, BLK)
            new_carry, new_acc = [], []
            for p in range(P):
                k2 = k_ref[pl.ds(off, BLK), p * LANES:(p + 1) * LANES]
                v2 = v_ref[pl.ds(off, BLK), p * LANES:(p + 1) * LANES]
                vh = (jnp.where(m0, v2, zq), jnp.where(m0, zq, v2))
                a_p = acc[p]
                for h in range(2):
                    _, b, w = _sb_scores(qh[2 * p + h], k2, carry[2 * p + h], diag, tri_excl, strict)
                    a_p = a_p + _dot(w.astype(BF16), vh[h], NN)
                    new_carry.append(carry[2 * p + h] + jnp.sum(b, axis=1, keepdims=True))
                new_acc.append(a_p)
            return new_carry, new_acc

        c0 = jnp.zeros((BLK, 1), F32)
        carry, acc = block(i, [c0] * (2 * P), [jnp.zeros((BLK, LANES), F32)] * P, True)

        def cond(st):
            return jnp.logical_and(st[0] >= 0, st[1] > 0)

        def step(st):
            j, _, carry, acc = st
            carry, acc = block(j, carry, acc, False)
            return j - 1, _any_alive(carry), carry, acc

        st = lax.while_loop(cond, step, (i - 1, _any_alive(carry), carry, acc))
        for p in range(P):
            o_ref[:, p * LANES:(p + 1) * LANES] = st[3][p]

    return pl.pallas_call(
        body, name=name, out_shape=jax.ShapeDtypeStruct((S, D), F32),
        grid=(ngroup, nb),
        in_specs=[pl.BlockSpec((BLK, W), lambda g, i: (i, g)),
                  pl.BlockSpec((S, W), lambda g, i: (0, ngroup + g)),
                  pl.BlockSpec((S, W), lambda g, i: (0, 2 * ngroup + g))],
        out_specs=pl.BlockSpec((BLK, W), lambda g, i: (i, g)),
        compiler_params=_params("arbitrary", "arbitrary"),
    )(qkv, qkv, qkv)


def _sb_bwd(qkv, o, do, name):
    S, D3 = qkv.shape
    D = D3 // 3
    npair, nb = D // LANES, S // BLK
    P = min(SB_BWD_PAIRS, npair)
    ngroup = npair // P
    W = P * LANES

    def body(q_ref, k_ref, v_ref, o_ref, do_ref, dq_ref, dk_ref, dv_ref):
        i = pl.program_id(1)
        m0 = _head_masks()
        row = lax.broadcasted_iota(jnp.int32, (BLK, BLK), 0)
        col = lax.broadcasted_iota(jnp.int32, (BLK, BLK), 1)
        strict = col < row
        tri_excl = jnp.where(row > col, 1.0, 0.0).astype(BF16)
        tri_incl = jnp.where(row >= col, 1.0, 0.0).astype(BF16)
        zq = jnp.zeros((BLK, LANES), BF16)
        qh, doh, delta = [], [], []
        for p in range(P):
            sl = slice(p * LANES, (p + 1) * LANES)
            q2, do2 = q_ref[:, sl], do_ref[:, sl]
            qh += [jnp.where(m0, q2, zq), jnp.where(m0, zq, q2)]
            doh += [jnp.where(m0, do2, zq), jnp.where(m0, zq, do2)]
            prod = do2.astype(F32) * o_ref[:, sl]
            delta += [jnp.sum(jnp.where(m0, prod, 0.0), axis=1, keepdims=True),
                      jnp.sum(jnp.where(m0, 0.0, prod), axis=1, keepdims=True)]

        @pl.when(i == 0)
        def _():
            dk_ref[...] = jnp.zeros_like(dk_ref)
            dv_ref[...] = jnp.zeros_like(dv_ref)

        def block(j, cb, cg, dq, diag):
            off = pl.multiple_of(j * BLK, BLK)
            ncb, ncg, ndq = [], [], []
            for p in range(P):
                sl = slice(p * LANES, (p + 1) * LANES)
                k2 = k_ref[pl.ds(off, BLK), sl]
                v2 = v_ref[pl.ds(off, BLK), sl]
                kh = (jnp.where(m0, k2, zq), jnp.where(m0, zq, k2))
                dk_blk = jnp.zeros((BLK, LANES), F32)
                dv_blk = jnp.zeros((BLK, LANES), F32)
                dq_p = dq[p]
                for h in range(2):
                    n = 2 * p + h
                    a, b, w = _sb_scores(qh[n], k2, cb[n], diag, tri_excl, strict)
                    wb = w.astype(BF16)
                    g = _dot(doh[n], v2, NT) * wb.astype(F32)
                    ghi, glo = _split_bf16(g)
                    ginc = _dot(ghi, tri_incl, NN) + _dot(glo, tri_incl, NN) + cg[n]
                    beta = jnp.exp(a)
                    dz = g * (1.0 - beta) - beta * (delta[n] - ginc)
                    if diag:
                        dz = jnp.where(strict, dz, 0.0)
                    dzs = (dz * ATTN_SCALE).astype(BF16)
                    dq_p = dq_p + _dot(dzs, kh[h], NN)
                    dk_blk = dk_blk + _dot(dzs, qh[n], TN)
                    dv_blk = dv_blk + _dot(wb, doh[n], TN)
                    ncb.append(cb[n] + jnp.sum(b, axis=1, keepdims=True))
                    ncg.append(cg[n] + jnp.sum(g, axis=1, keepdims=True))
                dk_ref[pl.ds(off, BLK), sl] += dk_blk
                dv_ref[pl.ds(off, BLK), sl] += dv_blk
                ndq.append(dq_p)
            return ncb, ncg, ndq

        c0 = jnp.zeros((BLK, 1), F32)
        cb, cg, dq = block(i, [c0] * (2 * P), [c0] * (2 * P), [jnp.zeros((BLK, LANES), F32)] * P, True)

        def cond(st):
            return jnp.logical_and(st[0] >= 0, st[1] > 0)

        def step(st):
            j, _, cb, cg, dq = st
            cb, cg, dq = block(j, cb, cg, dq, False)
            return j - 1, _any_alive(cb), cb, cg, dq

        st = lax.while_loop(cond, step, (i - 1, _any_alive(cb), cb, cg, dq))
        for p in range(P):
            dq_ref[:, p * LANES:(p + 1) * LANES] = st[4][p].astype(BF16)

    blk = lambda c: pl.BlockSpec((BLK, W), lambda g, i: (i, c * ngroup + g))
    col_all = lambda c: pl.BlockSpec((S, W), lambda g, i: (0, c * ngroup + g))
    return pl.pallas_call(
        body, name=name,
        out_shape=(jax.ShapeDtypeStruct((S, D), BF16), jax.ShapeDtypeStruct((S, D), F32),
                   jax.ShapeDtypeStruct((S, D), F32)),
        grid=(ngroup, nb),
        in_specs=[blk(0), col_all(1), col_all(2), blk(0), blk(0)],
        out_specs=(blk(0), col_all(0), col_all(0)),
        compiler_params=_params("arbitrary", "arbitrary"),
    )(qkv, qkv, qkv, o, do)


SWA_Q_GROUPS = 4


def _roll_heads(x):
    return pltpu.roll(x.astype(F32), HEAD_DIM, 1).astype(BF16)


def _swa_valid(i):
    r = lax.broadcasted_iota(jnp.int32, (BLK, 2 * BLK), 0)
    c = lax.broadcasted_iota(jnp.int32, (BLK, 2 * BLK), 1)
    diff = r + BLK - c
    return (diff >= 0) & (diff < BLK) & ((i > 0) | (c >= BLK))


def _swa_probs(qm, ksel, valid, sink):
    z = _dot(qm, ksel, NT) * ATTN_SCALE
    z = jnp.where(valid, z, NEG_BIG)
    mx = jnp.maximum(jnp.max(z, axis=1, keepdims=True), sink)
    p = jnp.exp(z - mx)
    ps = jnp.exp(sink - mx)
    inv = 1.0 / (jnp.sum(p, axis=1, keepdims=True) + ps)
    return p * inv, ps * inv


def _swa_fwd(q, kv, sinks, name):
    S, D = q.shape
    nkvp = kv.shape[1] // (2 * LANES)
    nb = S // BLK
    qw = SWA_Q_GROUPS * LANES

    def body(q_ref, kc_ref, kp_ref, vc_ref, vp_ref, s_ref, o_ref):
        m, i = pl.program_id(0), pl.program_id(1)
        m0 = _head_masks()
        valid = _swa_valid(i)
        kk = jnp.concatenate([kp_ref[...], kc_ref[...]], axis=0)
        vv = jnp.concatenate([vp_ref[...], vc_ref[...]], axis=0)
        ksw, vsw = _roll_heads(kk), _roll_heads(vv)
        m0k = jnp.concatenate([m0, m0], axis=0)
        zv = jnp.zeros_like(vv)
        for c in range(SWA_Q_GROUPS):
            qc = q_ref[:, c * LANES:(c + 1) * LANES]
            zq = jnp.zeros_like(qc)
            w_half = c // 2
            acc = jnp.zeros((BLK, LANES), F32)
            for u in range(2):
                qm = jnp.where(m0, qc, zq) if u == 0 else jnp.where(m0, zq, qc)
                same = u == w_half
                ksel = kk if same else ksw
                vsel = vv if same else vsw
                vsel = jnp.where(m0k, vsel, zv) if u == 0 else jnp.where(m0k, zv, vsel)
                sink = s_ref[0, m * 2 * SWA_Q_GROUPS + 2 * c + u]
                p, _ = _swa_probs(qm, ksel, valid, sink)
                acc = acc + _dot(p.astype(BF16), vsel, NN)
            o_ref[:, c * LANES:(c + 1) * LANES] = acc

    prev = lambda i: jnp.maximum(i - 1, 0)
    return pl.pallas_call(
        body, name=name, out_shape=jax.ShapeDtypeStruct((S, D), F32),
        grid=(nkvp, nb),
        in_specs=[pl.BlockSpec((BLK, qw), lambda m, i: (i, m)),
                  pl.BlockSpec((BLK, LANES), lambda m, i: (i, m)),
                  pl.BlockSpec((BLK, LANES), lambda m, i: (prev(i), m)),
                  pl.BlockSpec((BLK, LANES), lambda m, i: (i, nkvp + m)),
                  pl.BlockSpec((BLK, LANES), lambda m, i: (prev(i), nkvp + m)),
                  pl.BlockSpec(memory_space=pltpu.SMEM)],
        out_specs=pl.BlockSpec((BLK, qw), lambda m, i: (i, m)),
        compiler_params=_params("arbitrary", "arbitrary"),
    )(q, kv, kv, kv, kv, sinks)


def _swa_bwd(q, kv, sinks, o, do, name):
    S, D = q.shape
    nkvp = kv.shape[1] // (2 * LANES)
    nb = S // BLK
    qw = SWA_Q_GROUPS * LANES
    nh = 2 * SWA_Q_GROUPS

    def body(q_ref, kc_ref, kp_ref, vc_ref, vp_ref, s_ref, o_ref, do_ref, dq_ref, dk_ref, dv_ref, ds_ref):
        m, i = pl.program_id(0), pl.program_id(1)
        m0 = _head_masks()
        valid = _swa_valid(i)
        kk = jnp.concatenate([kp_ref[...], kc_ref[...]], axis=0)
        vv = jnp.concatenate([vp_ref[...], vc_ref[...]], axis=0)
        ksw, vsw = _roll_heads(kk), _roll_heads(vv)
        m0k = jnp.concatenate([m0, m0], axis=0)
        zk = jnp.zeros_like(kk)

        @pl.when(i == 0)
        def _():
            dk_ref[...] = jnp.zeros_like(dk_ref)
            dv_ref[...] = jnp.zeros_like(dv_ref)
            ds_ref[...] = jnp.zeros_like(ds_ref)

        acc = {(t, s): jnp.zeros((2 * BLK, LANES), F32) for t in ("k", "v") for s in (True, False)}
        for c in range(SWA_Q_GROUPS):
            qc = q_ref[:, c * LANES:(c + 1) * LANES]
            doc = do_ref[:, c * LANES:(c + 1) * LANES]
            prod = doc.astype(F32) * o_ref[:, c * LANES:(c + 1) * LANES]
            zq = jnp.zeros_like(qc)
            w_half = c // 2
            dq = jnp.zeros((BLK, LANES), F32)
            for u in range(2):
                sel = (lambda x, z: jnp.where(m0, x, z)) if u == 0 else (lambda x, z: jnp.where(m0, z, x))
                selk = (lambda x, z: jnp.where(m0k, x, z)) if u == 0 else (lambda x, z: jnp.where(m0k, z, x))
                qm, dom = sel(qc, zq), sel(doc, zq)
                same = u == w_half
                ksel = kk if same else ksw
                vsel = vv if same else vsw
                hh = 2 * c + u
                sink = s_ref[0, m * nh + hh]
                p, ps = _swa_probs(qm, ksel, valid, sink)
                delta = jnp.sum(sel(prod, 0.0), axis=1, keepdims=True)
                dp = _dot(dom, vsel, NT)
                dsc = (p * (dp - delta) * ATTN_SCALE).astype(BF16)
                dq = dq + _dot(dsc, selk(ksel, zk), NN)
                acc["k", same] = acc["k", same] + _dot(dsc, qm, TN)
                acc["v", same] = acc["v", same] + _dot(p.astype(BF16), dom, TN)
                dsink = jnp.sum(jnp.broadcast_to(-(ps * delta), (BLK, LANES)), axis=0, keepdims=True)
                ds_ref[0, hh:hh + 1, :] += dsink
            dq_ref[:, c * LANES:(c + 1) * LANES] = dq
        dkk = acc["k", True] + pltpu.roll(acc["k", False], HEAD_DIM, 1)
        dvv = acc["v", True] + pltpu.roll(acc["v", False], HEAD_DIM, 1)
        poff = pl.multiple_of(jnp.maximum(i - 1, 0) * BLK, BLK)
        coff = pl.multiple_of(i * BLK, BLK)
        dk_ref[pl.ds(poff, BLK), :] += dkk[:BLK]
        dv_ref[pl.ds(poff, BLK), :] += dvv[:BLK]
        dk_ref[pl.ds(coff, BLK), :] += dkk[BLK:]
        dv_ref[pl.ds(coff, BLK), :] += dvv[BLK:]

    prev = lambda i: jnp.maximum(i - 1, 0)
    qblk = pl.BlockSpec((BLK, qw), lambda m, i: (i, m))
    col_all = pl.BlockSpec((S, LANES), lambda m, i: (0, m))
    return pl.pallas_call(
        body, name=name,
        out_shape=(jax.ShapeDtypeStruct((S, D), F32),
                   jax.ShapeDtypeStruct((S, nkvp * LANES), F32),
                   jax.ShapeDtypeStruct((S, nkvp * LANES), F32),
                   jax.ShapeDtypeStruct((nkvp, nh, LANES), F32)),
        grid=(nkvp, nb),
        in_specs=[qblk,
                  pl.BlockSpec((BLK, LANES), lambda m, i: (i, m)),
                  pl.BlockSpec((BLK, LANES), lambda m, i: (prev(i), m)),
                  pl.BlockSpec((BLK, LANES), lambda m, i: (i, nkvp + m)),
                  pl.BlockSpec((BLK, LANES), lambda m, i: (prev(i), nkvp + m)),
                  pl.BlockSpec(memory_space=pltpu.SMEM),
                  qblk, qblk],
        out_specs=(qblk, col_all, col_all, pl.BlockSpec((1, nh, LANES), lambda m, i: (m, 0, 0))),
        compiler_params=_params("arbitrary", "arbitrary"),
    )(q, kv, kv, kv, kv, sinks, o, do)


def _all_gather(x, name):
    R, C = x.shape

    def body(x_ref, out_ref, send_sems, recv_sems, local_sem):
        x_, y_, c_ = lax.axis_index("x"), lax.axis_index("y"), lax.axis_index("c")
        me, sibling = (x_, y_, c_), (x_, y_, 1 - c_)
        chips = [(1 - x_, y_), (x_, 1 - y_), (1 - x_, 1 - y_)]

        def rows(px, py, pc):
            return out_ref.at[4 * px + 2 * py + pc]

        def copy(k, block, to, src=None):
            return pltpu.make_async_remote_copy(
                src_ref=rows(*block) if src is None else src, dst_ref=rows(*block),
                send_sem=send_sems.at[k], recv_sem=recv_sems.at[k],
                device_id=to, device_id_type=MESH)

        mine = pltpu.make_async_copy(x_ref, rows(*me), local_sem)
        mine.start()
        first = [copy(0, me, sibling, src=x_ref)]
        first += [copy(1 + j, me, (*chip, c_), src=x_ref) for j, chip in enumerate(chips)]
        for cp in first:
            cp.start()
        passed = [copy(4 + j, (*chip, c_), sibling) for j, chip in enumerate(chips)]
        for j, chip in enumerate(chips):
            copy(1 + j, (*chip, c_), me).wait_recv()
            passed[j].start()
        copy(0, sibling, me).wait_recv()
        for j, chip in enumerate(chips):
            copy(4 + j, (*chip, 1 - c_), me).wait_recv()
        for cp in first + passed:
            cp.wait_send()
        mine.wait()

    return pl.pallas_call(
        body, name=name, out_shape=jax.ShapeDtypeStruct((N_DEV, R, C), x.dtype),
        in_specs=[pl.BlockSpec(memory_space=pl.ANY)], out_specs=pl.BlockSpec(memory_space=pl.ANY),
        scratch_shapes=[pltpu.SemaphoreType.DMA((7,)), pltpu.SemaphoreType.DMA((7,)), pltpu.SemaphoreType.DMA],
    )(x)


def _all_to_all(blocks, name):
    _, R, C = blocks.shape

    def body(b_ref, out_ref, send_sems, recv_sems, local_sem):
        x_, y_, c_ = lax.axis_index("x"), lax.axis_index("y"), lax.axis_index("c")
        my_idx = 4 * x_ + 2 * y_ + c_
        mine = pltpu.make_async_copy(b_ref.at[my_idx], out_ref.at[my_idx], local_sem)
        mine.start()
        copies = []
        for k in range(1, N_DEV):
            px = x_ ^ ((k >> 2) & 1)
            py = y_ ^ ((k >> 1) & 1)
            pc = c_ ^ (k & 1)
            copies.append(pltpu.make_async_remote_copy(
                src_ref=b_ref.at[4 * px + 2 * py + pc], dst_ref=out_ref.at[my_idx],
                send_sem=send_sems.at[k - 1], recv_sem=recv_sems.at[k - 1],
                device_id=(px, py, pc), device_id_type=MESH))
        for cp in copies:
            cp.start()
        for cp in copies:
            cp.wait_recv()
        for cp in copies:
            cp.wait_send()
        mine.wait()

    return pl.pallas_call(
        body, name=name, out_shape=jax.ShapeDtypeStruct((N_DEV, R, C), blocks.dtype),
        in_specs=[pl.BlockSpec(memory_space=pl.ANY)], out_specs=pl.BlockSpec(memory_space=pl.ANY),
        scratch_shapes=[pltpu.SemaphoreType.DMA((7,)), pltpu.SemaphoreType.DMA((7,)), pltpu.SemaphoreType.DMA],
    )(blocks)


def _sum_adamw(parts, w, m, v, name):
    _, R, C = parts.shape
    tr = _tile(R, 256, 16)
    c1 = 1.0 - ADAM_B1 ** ADAM_STEP
    c2 = 1.0 - ADAM_B2 ** ADAM_STEP

    def body(p_ref, w_ref, m_ref, v_ref, g_ref, d_ref, nm_ref, nv_ref):
        g = p_ref[0].astype(F32)
        for s in range(1, N_DEV):
            g = g + p_ref[s].astype(F32)
        nm = ADAM_B1 * m_ref[...] + (1.0 - ADAM_B1) * g
        nv = ADAM_B2 * v_ref[...] + (1.0 - ADAM_B2) * (g * g)
        m_hat = nm / c1
        v_hat = nv / c2
        g_ref[...] = g
        nm_ref[...] = nm
        nv_ref[...] = nv
        d_ref[...] = -ADAM_LR * (m_hat / (jnp.sqrt(v_hat) + ADAM_EPS) + ADAM_WD * w_ref[...])

    row = pl.BlockSpec((tr, C), lambda i: (i, 0))
    shp = jax.ShapeDtypeStruct((R, C), F32)
    return pl.pallas_call(
        body, name=name, out_shape=(shp, shp, shp, shp),
        grid=(R // tr,),
        in_specs=[pl.BlockSpec((N_DEV, tr, C), lambda i: (0, i, 0)), row, row, row],
        out_specs=(row, row, row, row),
        compiler_params=_params("parallel"),
    )(parts, w, m, v)


def _ffn_fwd(h, g, wg, wu, wo, tag):
    xn, gate, up, act = _ffn_up(h, g, wg, wu, f"{tag}_up")
    out = _mm(act, wo, NN, F32, f"{tag}_down", scale=FFN_RES_SCALE, res=h, tm=512, tn=1024, tk=2816)
    return out, (xn, gate, up, act)


def _ffn_bwd(dh, h, g, wg, wu, wo, saved, tag):
    xn, gate, up, act = saved
    dgate, dup = _ffn_dact(dh, wo, gate, up, f"{tag}_dact")
    dwo = _mm(act, dh, TN, BF16, f"{tag}_dwo", scale=FFN_RES_SCALE, tm=1408, tn=1024, tk=512)
    dwg = _mm(xn, dgate, TN, BF16, f"{tag}_dwg", tm=1024, tn=1408, tk=512)
    dwu = _mm(xn, dup, TN, BF16, f"{tag}_dwu", tm=1024, tn=1408, tk=512)
    dh_in, dg = _dx_norm_bwd([(dgate, wg), (dup, wu)], h, g, dh, f"{tag}_dx")
    return dh_in, dg, dwg, dwu, dwo


def _proj(a, w, out_dtype, name, res=None):
    return _mm(a, w, NN, out_dtype, name, res=res, tm=1024, tn=1024, tk=1024)


def _proj_dw(x, dy, name):
    return _mm(x, dy, TN, BF16, name, tm=1024, tn=1024, tk=512)


def _col_shards_to_full(g, lead):
    nl = len(lead)
    perm = tuple(range(1, nl + 2)) + (0, nl + 2)
    t = jnp.transpose(g, perm)
    return t.reshape(t.shape[:nl + 1] + (t.shape[nl + 1] * t.shape[nl + 2],))


def _full_to_col_shards(w):
    nl = w.ndim - 2
    t = w.reshape(w.shape[:-1] + (N_DEV, w.shape[-1] // N_DEV))
    perm = (nl + 1,) + tuple(range(nl + 1)) + (nl + 2,)
    return jnp.transpose(t, perm)


def kernel(x, ffn1_norm, ffn1_w_in, ffn1_w_out, mix_norm, ffn2_norm, ffn2_w_in, ffn2_w_out, sb_w_qkv, sb_w_o, kv_norm, kv_w, swa_w_q, swa_sinks, swa_w_o, final_norm, loss_target, m_ffn1_norm, m_ffn1_w_in, m_ffn1_w_out, m_mix_norm, m_ffn2_norm, m_ffn2_w_in, m_ffn2_w_out, m_sb_w_qkv, m_sb_w_o, m_kv_norm, m_kv_w, m_swa_w_q, m_swa_sinks, m_swa_w_o, m_final_norm, v_ffn1_norm, v_ffn1_w_in, v_ffn1_w_out, v_mix_norm, v_ffn2_norm, v_ffn2_w_in, v_ffn2_w_out, v_sb_w_qkv, v_sb_w_o, v_kv_norm, v_kv_w, v_swa_w_q, v_swa_sinks, v_swa_w_o, v_final_norm):
    S, D = x.shape[1], x.shape[2]
    F = ffn1_w_out.shape[1] * N_DEV
    L = ffn1_w_in.shape[0]
    KV = kv_w.shape[1]
    assert L == 2 and D % PACK_W == 0 and swa_sinks.shape == (1, 2 * SWA_Q_GROUPS * KV // (2 * LANES))

    big_w = [ffn1_w_in, ffn1_w_out, ffn2_w_in, ffn2_w_out, sb_w_qkv, sb_w_o, kv_w, swa_w_q, swa_w_o]
    big_m = [m_ffn1_w_in, m_ffn1_w_out, m_ffn2_w_in, m_ffn2_w_out, m_sb_w_qkv, m_sb_w_o, m_kv_w, m_swa_w_q, m_swa_w_o]
    big_v = [v_ffn1_w_in, v_ffn1_w_out, v_ffn2_w_in, v_ffn2_w_out, v_sb_w_qkv, v_sb_w_o, v_kv_w, v_swa_w_q, v_swa_w_o]
    rows = [w.size // PACK_W for w in big_w]
    offs = [sum(rows[:i]) for i in range(len(rows))]

    def pack(ts, dtype):
        return jnp.concatenate([t.astype(dtype).reshape(-1, PACK_W) for t in ts], axis=0)

    gath = _all_gather(pack(big_w, BF16), "gather_weights")

    def part(i, shape):
        return gath[:, offs[i]:offs[i] + rows[i]].reshape((N_DEV,) + shape)

    w_in1 = _col_shards_to_full(part(0, ffn1_w_in.shape), (L,))
    w_out1 = jnp.transpose(part(1, ffn1_w_out.shape), (1, 0, 2, 3)).reshape(L, F, D)
    w_in2 = _col_shards_to_full(part(2, ffn2_w_in.shape), (L,))
    w_out2 = jnp.transpose(part(3, ffn2_w_out.shape), (1, 0, 2, 3)).reshape(L, F, D)
    w_qkv = _col_shards_to_full(part(4, sb_w_qkv.shape[1:]), ())
    w_sbo = part(5, sb_w_o.shape[1:]).reshape(D, D)
    w_kv = part(6, kv_w.shape).reshape(D, KV)
    w_q = part(7, swa_w_q.shape[1:]).reshape(D, D)
    w_swo = part(8, swa_w_o.shape[1:]).reshape(D, D)

    def ffn_weights(w_in, w_out, layer):
        return w_in[layer, :, :F], w_in[layer, :, F:], w_out[layer]

    cos_t, sin_t = _rope_tables(S)
    h0 = x.reshape(S, D)
    tgt = loss_target.reshape(S, D)

    fw = [ffn_weights(w_in1, w_out1, 0), ffn_weights(w_in2, w_out2, 0),
          ffn_weights(w_in1, w_out1, 1), ffn_weights(w_in2, w_out2, 1)]
    h1, sv_a1 = _ffn_fwd(h0, ffn1_norm[0], *fw[0], "ffn1a")
    hn_a = _rmsnorm(h1, mix_norm[0], "mix_a_norm")
    qkv = _proj(hn_a, w_qkv, BF16, "sb_qkv")
    o_sb = _sb_fwd(qkv, "sb_attn")
    h2 = _proj(o_sb, w_sbo, F32, "sb_out", res=h1)
    h3, sv_a2 = _ffn_fwd(h2, ffn2_norm[0], *fw[1], "ffn2a")
    kvn = _rmsnorm(h3, kv_norm, "kv_norm")
    kv_raw = _proj(kvn, w_kv, F32, "kv_proj")
    kv_rot = _rotary(kv_raw, cos_t, sin_t, KV // (2 * LANES), False, "kv_rope")
    h4, sv_b1 = _ffn_fwd(h3, ffn1_norm[1], *fw[2], "ffn1b")
    hn_b = _rmsnorm(h4, mix_norm[1], "mix_b_norm")
    q_raw = _proj(hn_b, w_q, F32, "swa_q")
    q_rot = _rotary(q_raw, cos_t, sin_t, D // LANES, False, "q_rope")
    o_sw = _swa_fwd(q_rot, kv_rot, swa_sinks, "swa_attn")
    h5 = _proj(o_sw, w_swo, F32, "swa_out", res=h4)
    h6, sv_b2 = _ffn_fwd(h5, ffn2_norm[1], *fw[3], "ffn2b")
    dh6, dg_final, sq_err = _final_loss(h6, final_norm, tgt, "final_loss")
    loss = lax.psum(0.5 * jnp.sum(sq_err) / D, ("x", "y", "c"))

    dh5, dg_f2b, dwg_f2b, dwu_f2b, dwo_f2b = _ffn_bwd(dh6, h5, ffn2_norm[1], *fw[3], sv_b2, "ffn2b")
    do_sw = _mm(dh5, w_swo, NT, BF16, "swa_out_dx", tm=1024, tn=1024, tk=1024)
    dw_swo = _proj_dw(o_sw, dh5, "swa_out_dw")
    dq_rot, dk_sw, dv_sw, dsink = _swa_bwd(q_rot, kv_rot, swa_sinks, o_sw, do_sw, "swa_attn_bwd")
    dq = _rotary(dq_rot, cos_t, sin_t, D // LANES, True, "q_rope_bwd")
    dw_q = _proj_dw(hn_b, dq, "swa_q_dw")
    dh4, dg_mix_b = _dx_norm_bwd([(dq, w_q)], h4, mix_norm[1], dh5, "swa_q_dx")
    dh3, dg_f1b, dwg_f1b, dwu_f1b, dwo_f1b = _ffn_bwd(dh4, h3, ffn1_norm[1], *fw[2], sv_b1, "ffn1b")
    dkv = _rotary(jnp.concatenate([dk_sw, dv_sw], axis=1), cos_t, sin_t, KV // (2 * LANES), True, "kv_rope_bwd")
    dw_kv = _proj_dw(kvn, dkv, "kv_proj_dw")
    dh3, dg_kv = _dx_norm_bwd([(dkv, w_kv)], h3, kv_norm, dh3, "kv_proj_dx")
    dh2, dg_f2a, dwg_f2a, dwu_f2a, dwo_f2a = _ffn_bwd(dh3, h2, ffn2_norm[0], *fw[1], sv_a2, "ffn2a")
    do_sb = _mm(dh2, w_sbo, NT, BF16, "sb_out_dx", tm=1024, tn=1024, tk=1024)
    dw_sbo = _proj_dw(o_sb, dh2, "sb_out_dw")
    dq_sb, dk_sb, dv_sb = _sb_bwd(qkv, o_sb, do_sb, "sb_attn_bwd")
    dqkv = jnp.concatenate([dq_sb, dk_sb.astype(BF16), dv_sb.astype(BF16)], axis=1)
    dw_qkv = _proj_dw(hn_a, dqkv, "sb_qkv_dw")
    dh1, dg_mix_a = _dx_norm_bwd([(dqkv, w_qkv)], h1, mix_norm[0], dh2, "sb_qkv_dx")
    dx, dg_f1a, dwg_f1a, dwu_f1a, dwo_f1a = _ffn_bwd(dh1, h0, ffn1_norm[0], *fw[0], sv_a1, "ffn1a")

    def w_in_blocks(dwg_a, dwu_a, dwg_b, dwu_b):
        full = jnp.stack([jnp.concatenate([dwg_a, dwu_a], axis=1), jnp.concatenate([dwg_b, dwu_b], axis=1)])
        return _full_to_col_shards(full)

    def w_out_blocks(dwo_a, dwo_b):
        full = jnp.stack([dwo_a, dwo_b]).reshape(L, N_DEV, F // N_DEV, D)
        return jnp.transpose(full, (1, 0, 2, 3))

    blocks = [w_in_blocks(dwg_f1a, dwu_f1a, dwg_f1b, dwu_f1b), w_out_blocks(dwo_f1a, dwo_f1b),
              w_in_blocks(dwg_f2a, dwu_f2a, dwg_f2b, dwu_f2b), w_out_blocks(dwo_f2a, dwo_f2b),
              _full_to_col_shards(dw_qkv), dw_sbo, dw_kv, dw_q, dw_swo]
    gblocks = jnp.concatenate([b.reshape(N_DEV, -1, PACK_W) for b in blocks], axis=1)
    parts = _all_to_all(gblocks, "scatter_grads")
    g_big, d_big, nm_big, nv_big = _sum_adamw(parts, pack(big_w, F32), pack(big_m, F32), pack(big_v, F32), "adamw_big")

    def unpack(flat):
        return [flat[offs[i]:offs[i] + rows[i]].reshape(big_w[i].shape) for i in range(len(big_w))]

    small_w = [ffn1_norm, mix_norm, ffn2_norm, kv_norm, final_norm, swa_sinks]
    small_m = [m_ffn1_norm, m_mix_norm, m_ffn2_norm, m_kv_norm, m_final_norm, m_swa_sinks]
    small_v = [v_ffn1_norm, v_mix_norm, v_ffn2_norm, v_kv_norm, v_final_norm, v_swa_sinks]
    SMALL_ROWS = 16

    def pack_small(ts):
        rows_ = [t.reshape(-1, D) for t in ts[:-1]]
        sink_row = jnp.pad(ts[-1].reshape(1, -1), ((0, 0), (0, D - ts[-1].size)))
        flat = jnp.concatenate(rows_ + [sink_row], axis=0)
        return jnp.pad(flat, ((0, SMALL_ROWS - flat.shape[0]), (0, 0)))

    def gain(parts8):
        return jnp.sum(parts8, axis=0, keepdims=True)

    g_small_local = pack_small([
        jnp.concatenate([gain(dg_f1a), gain(dg_f1b)], axis=0),
        jnp.concatenate([gain(dg_mix_a), gain(dg_mix_b)], axis=0),
        jnp.concatenate([gain(dg_f2a), gain(dg_f2b)], axis=0),
        gain(dg_kv), gain(dg_final), dsink[:, :, 0].reshape(1, -1)])
    small_parts = _all_gather(g_small_local, "gather_small_grads")
    g_sm, d_sm, nm_sm, nv_sm = _sum_adamw(small_parts, pack_small(small_w), pack_small(small_m), pack_small(small_v), "adamw_small")

    def unpack_small(flat):
        out, r = [], 0
        for t in small_w[:-1]:
            n = t.size // D
            out.append(flat[r:r + n].reshape(t.shape))
            r += n
        out.append(flat[r, :swa_sinks.size].reshape(swa_sinks.shape))
        return out

    def ordered(big, small):
        f1w_in, f1w_out, f2w_in, f2w_out, qkv_, sbo_, kvw_, swq_, swo_ = big
        f1n, mixn, f2n, kvn_, finn, sinks_ = small
        return [f1n, f1w_in, f1w_out, mixn, f2n, f2w_in, f2w_out, qkv_, sbo_, kvn_, kvw_, swq_, sinks_, swo_, finn]

    outs = []
    for big_flat, small_flat in ((g_big, g_sm), (d_big, d_sm), (nm_big, nm_sm), (nv_big, nv_sm)):
        outs += ordered(unpack(big_flat), unpack_small(small_flat))
    return (loss, dx.reshape(x.shape), *outs)
```

```python
import jax
import jax.numpy as jnp
from jax import lax
from jax.experimental import pallas as pl
from jax.experimental.pallas import tpu as pltpu

F32 = jnp.float32
BF16 = jnp.bfloat16

N_DEV = 8
HEAD_DIM = 64
LANES = 128
BLK = 128
PACK_W = 1024
RMS_EPS = 1e-6
FFN_RES_SCALE = 0.5
ROPE_THETA = 10000.0
ATTN_SCALE = HEAD_DIM ** -0.5
SB_LOG_FLOOR = -110.0
NEG_BIG = -1e30
VMEM_LIMIT_V7X = 56 * 1024 * 1024

ADAM_LR = 0.001
ADAM_B1 = 0.9
ADAM_B2 = 0.999
ADAM_EPS = 1e-08
ADAM_WD = 0.01
ADAM_STEP = 10

NN = ((1,), (0,))
NT = ((1,), (1,))
TN = ((0,), (0,))
MESH = pl.DeviceIdType.MESH


def _dot(a, b, dims):
    return lax.dot_general(a, b, (dims, ((), ())), preferred_element_type=F32)


def _tile(n, pref, mult=LANES):
    if n <= pref:
        return n
    t = (pref // mult) * mult
    while t >= mult:
        if n % t == 0:
            return t
        t -= mult
    return n


def _params(*sem):
    return pltpu.CompilerParams(dimension_semantics=sem, vmem_limit_bytes=VMEM_LIMIT_V7X)


def _mm(a, b, dims, out_dtype, name, scale=1.0, res=None, tm=512, tn=512, tk=512):
    if dims == NN:
        (M, K), (_, N) = a.shape, b.shape
    elif dims == NT:
        (M, K), (N, _) = a.shape, b.shape
    else:
        (K, M), (_, N) = a.shape, b.shape
    tm, tn, tk = _tile(M, tm), _tile(N, tn), _tile(K, tk)
    nk = K // tk
    if dims == TN:
        a_spec = pl.BlockSpec((tk, tm), lambda i, j, k: (k, i))
    else:
        a_spec = pl.BlockSpec((tm, tk), lambda i, j, k: (i, k))
    if dims == NT:
        b_spec = pl.BlockSpec((tn, tk), lambda i, j, k: (j, k))
    else:
        b_spec = pl.BlockSpec((tk, tn), lambda i, j, k: (k, j))
    o_spec = pl.BlockSpec((tm, tn), lambda i, j, k: (i, j))
    has_res = res is not None

    def body(*refs):
        a_ref, b_ref = refs[0], refs[1]
        r_ref = refs[2] if has_res else None
        o_ref = refs[3] if has_res else refs[2]

        def finish(acc):
            r = acc * scale if scale != 1.0 else acc
            if has_res:
                r = r + r_ref[...]
            o_ref[...] = r.astype(out_dtype)

        p = _dot(a_ref[...].astype(BF16), b_ref[...].astype(BF16), dims)
        if nk == 1:
            finish(p)
        else:
            acc_ref = refs[-1]
            k = pl.program_id(2)

            @pl.when(k == 0)
            def _():
                acc_ref[...] = p

            @pl.when(k > 0)
            def _():
                acc_ref[...] += p

            @pl.when(k == nk - 1)
            def _():
                finish(acc_ref[...])

    in_specs = [a_spec, b_spec] + ([o_spec] if has_res else [])
    args = (a, b) + ((res,) if has_res else ())
    return pl.pallas_call(
        body, name=name,
        out_shape=jax.ShapeDtypeStruct((M, N), out_dtype),
        grid=(M // tm, N // tn, nk),
        in_specs=in_specs, out_specs=o_spec,
        scratch_shapes=[pltpu.VMEM((tm, tn), F32)] if nk > 1 else [],
        compiler_params=_params("parallel", "parallel", "arbitrary"),
    )(*args)


def _rows8(x):
    r, d = x.shape
    return jnp.sum(x.reshape(r // 8, 8, d), axis=0)


def _rmsnorm(h, g, name):
    S, D = h.shape
    ts = _tile(S, 512, 8)

    def body(h_ref, g_ref, o_ref):
        x = h_ref[...]
        r = lax.rsqrt(jnp.mean(x * x, axis=-1, keepdims=True) + RMS_EPS)
        o_ref[...] = ((x * r) * g_ref[...]).astype(BF16)

    return pl.pallas_call(
        body, name=name,
        out_shape=jax.ShapeDtypeStruct((S, D), BF16),
        grid=(S // ts,),
        in_specs=[pl.BlockSpec((ts, D), lambda i: (i, 0)), pl.BlockSpec((1, D), lambda i: (0, 0))],
        out_specs=pl.BlockSpec((ts, D), lambda i: (i, 0)),
        compiler_params=_params("parallel"),
    )(h, g.reshape(1, D))


def _final_loss(h, g, tgt, name):
    S, D = h.shape
    ts = _tile(S, 512, 8)

    def body(h_ref, g_ref, t_ref, dh_ref, dg_ref, l_ref):
        x = h_ref[...]
        r = lax.rsqrt(jnp.mean(x * x, axis=-1, keepdims=True) + RMS_EPS)
        xhat = x * r
        err = xhat * g_ref[...] - t_ref[...]
        d = err * (1.0 / D)
        dxh = d * g_ref[...]
        c = jnp.mean(dxh * xhat, axis=-1, keepdims=True)
        dh_ref[...] = r * (dxh - xhat * c)
        part = _rows8(d * xhat)
        lpart = _rows8(err * err)

        @pl.when(pl.program_id(0) == 0)
        def _():
            dg_ref[...] = part
            l_ref[...] = lpart

        @pl.when(pl.program_id(0) > 0)
        def _():
            dg_ref[...] += part
            l_ref[...] += lpart

    row = pl.BlockSpec((ts, D), lambda i: (i, 0))
    acc = pl.BlockSpec((8, D), lambda i: (0, 0))
    return pl.pallas_call(
        body, name=name,
        out_shape=(jax.ShapeDtypeStruct((S, D), F32), jax.ShapeDtypeStruct((8, D), F32),
                   jax.ShapeDtypeStruct((8, D), F32)),
        grid=(S // ts,),
        in_specs=[row, pl.BlockSpec((1, D), lambda i: (0, 0)), row],
        out_specs=(row, acc, acc),
        compiler_params=_params("arbitrary"),
    )(h, g.reshape(1, D), tgt)


def _ffn_up(h, g, wg, wu, name):
    S, D = h.shape
    F = wg.shape[1]
    tm, tn = _tile(S, 512, 16), _tile(F, 1408)

    def body(h_ref, g_ref, wg_ref, wu_ref, xn_ref, gate_ref, up_ref, act_ref):
        x = h_ref[...]
        r = lax.rsqrt(jnp.mean(x * x, axis=-1, keepdims=True) + RMS_EPS)
        xn = ((x * r) * g_ref[...]).astype(BF16)
        xn_ref[...] = xn
        gate = _dot(xn, wg_ref[...], NN)
        up = _dot(xn, wu_ref[...], NN)
        gate_ref[...] = gate.astype(BF16)
        up_ref[...] = up.astype(BF16)
        sig = 1.0 / (1.0 + jnp.exp(-gate))
        act_ref[...] = (gate * sig * up).astype(BF16)

    row = pl.BlockSpec((tm, D), lambda i, j: (i, 0))
    wcol = pl.BlockSpec((D, tn), lambda i, j: (0, j))
    blk = pl.BlockSpec((tm, tn), lambda i, j: (i, j))
    hid = jax.ShapeDtypeStruct((S, F), BF16)
    return pl.pallas_call(
        body, name=name, out_shape=(jax.ShapeDtypeStruct((S, D), BF16), hid, hid, hid),
        grid=(S // tm, F // tn),
        in_specs=[row, pl.BlockSpec((1, D), lambda i, j: (0, 0)), wcol, wcol],
        out_specs=(row, blk, blk, blk),
        compiler_params=_params("arbitrary", "arbitrary"),
    )(h, g.reshape(1, D), wg, wu)


def _ffn_dact(dh, wo, gate, up, name):
    S, D = dh.shape
    F = wo.shape[0]
    tm, tn = _tile(S, 512, 16), _tile(F, 1408)

    def body(dh_ref, wo_ref, g_ref, u_ref, dg_ref, du_ref):
        d = _dot(dh_ref[...].astype(BF16), wo_ref[...], NT) * FFN_RES_SCALE
        g = g_ref[...].astype(F32)
        u = u_ref[...].astype(F32)
        sig = 1.0 / (1.0 + jnp.exp(-g))
        du_ref[...] = (d * (g * sig)).astype(BF16)
        dg_ref[...] = (d * u * (sig * (1.0 + g * (1.0 - sig)))).astype(BF16)

    blk = pl.BlockSpec((tm, tn), lambda j, i: (i, j))
    hid = jax.ShapeDtypeStruct((S, F), BF16)
    return pl.pallas_call(
        body, name=name, out_shape=(hid, hid),
        grid=(F // tn, S // tm),
        in_specs=[pl.BlockSpec((tm, D), lambda j, i: (i, 0)), pl.BlockSpec((tn, D), lambda j, i: (j, 0)), blk, blk],
        out_specs=(blk, blk),
        compiler_params=_params("arbitrary", "arbitrary"),
    )(dh, wo, gate, up)


def _dx_norm_bwd(pairs, h, g, res, name):
    S, D = h.shape
    tm = _tile(S, 256, 16)
    n = len(pairs)

    def body(*refs):
        dy_refs, w_refs = refs[:n], refs[n:2 * n]
        h_ref, g_ref, r_ref, dh_ref, dg_ref = refs[2 * n:]
        d = _dot(dy_refs[0][...], w_refs[0][...], NT)
        for t in range(1, n):
            d = d + _dot(dy_refs[t][...], w_refs[t][...], NT)
        x = h_ref[...]
        r = lax.rsqrt(jnp.mean(x * x, axis=-1, keepdims=True) + RMS_EPS)
        xhat = x * r
        dxh = d * g_ref[...]
        c = jnp.mean(dxh * xhat, axis=-1, keepdims=True)
        dh_ref[...] = r * (dxh - xhat * c) + r_ref[...]
        part = _rows8(d * xhat)

        @pl.when(pl.program_id(0) == 0)
        def _():
            dg_ref[...] = part

        @pl.when(pl.program_id(0) > 0)
        def _():
            dg_ref[...] += part

    row = pl.BlockSpec((tm, D), lambda i: (i, 0))
    in_specs = [pl.BlockSpec((tm, dy.shape[1]), lambda i: (i, 0)) for dy, _ in pairs]
    in_specs += [pl.BlockSpec(w.shape, lambda i: (0, 0)) for _, w in pairs]
    in_specs += [row, pl.BlockSpec((1, D), lambda i: (0, 0)), row]
    return pl.pallas_call(
        body, name=name,
        out_shape=(jax.ShapeDtypeStruct((S, D), F32), jax.ShapeDtypeStruct((8, D), F32)),
        grid=(S // tm,),
        in_specs=in_specs,
        out_specs=(row, pl.BlockSpec((8, D), lambda i: (0, 0))),
        compiler_params=_params("arbitrary"),
    )(*[dy for dy, _ in pairs], *[w for _, w in pairs], h, g.reshape(1, D), res)


def _rope_tables(S):
    half = HEAD_DIM // 2
    inv_freq = ROPE_THETA ** (-jnp.arange(half, dtype=F32) / half)
    ang = jnp.arange(S).astype(F32)[:, None] * inv_freq[None, :]
    cos, sin = jnp.cos(ang), jnp.sin(ang)
    cos_t = jnp.tile(cos, (1, LANES // half))
    sin_t = jnp.tile(jnp.concatenate([-sin, sin], axis=1), (1, LANES // HEAD_DIM))
    return cos_t, sin_t


def _swap_halves(x):
    lane = lax.broadcasted_iota(jnp.int32, x.shape, 1)
    first = (lane % HEAD_DIM) < (HEAD_DIM // 2)
    return jnp.where(first, pltpu.roll(x, LANES - HEAD_DIM // 2, 1), pltpu.roll(x, HEAD_DIM // 2, 1))


def _rotary(x, cos_t, sin_t, n_rot, inverse, name):
    S, C = x.shape
    ts = _tile(S, 512, 16)
    ng = C // LANES

    def body(x_ref, c_ref, s_ref, o_ref):
        cs, sn = c_ref[...], s_ref[...]
        for gidx in range(ng):
            sl = slice(gidx * LANES, (gidx + 1) * LANES)
            v = x_ref[:, sl].astype(F32)
            if gidx < n_rot:
                if inverse:
                    v = v * cs + _swap_halves(v * sn)
                else:
                    v = v * cs + _swap_halves(v) * sn
            o_ref[:, sl] = v.astype(BF16)

    row = pl.BlockSpec((ts, C), lambda i: (i, 0))
    tab = pl.BlockSpec((ts, LANES), lambda i: (i, 0))
    return pl.pallas_call(
        body, name=name, out_shape=jax.ShapeDtypeStruct((S, C), BF16),
        grid=(S // ts,), in_specs=[row, tab, tab], out_specs=row,
        compiler_params=_params("parallel"),
    )(x, cos_t, sin_t)


def _head_masks():
    lane = lax.broadcasted_iota(jnp.int32, (BLK, LANES), 1)
    return lane < HEAD_DIM


def _split_bf16(x):
    hi = x.astype(BF16)
    lo = (x - hi.astype(F32)).astype(BF16)
    return hi, lo


def _sb_scores(qh, ks, carry, diag, tri_excl, strict):
    n_heads = len(qh)
    zs = [_dot(qh[n], ks[n // 2], NT) for n in range(n_heads)]
    a_l, b_l, split_l = [], [], []
    for z in zs:
        z = z * ATTN_SCALE
        a = jnp.minimum(z, 0.0) - jnp.log(1.0 + jnp.exp(-jnp.abs(z)))
        b = a - z
        if diag:
            b = jnp.where(strict, b, 0.0)
        a_l.append(a)
        b_l.append(b)
        split_l.append(_split_bf16(b))
    sufs = [_dot(hi, tri_excl, NN) + _dot(lo, tri_excl, NN) for hi, lo in split_l]
    w_l = []
    for n in range(n_heads):
        w = jnp.exp(a_l[n] + sufs[n] + carry[n])
        if diag:
            w = jnp.where(strict, w, 0.0)
        w_l.append(w)
    return a_l, b_l, w_l


SB_FWD_PAIRS = 4
SB_BWD_PAIRS = 2


def _any_alive(carries):
    top = carries[0]
    for c in carries[1:]:
        top = jnp.maximum(top, c)
    return (jnp.max(top) > SB_LOG_FLOOR).astype(jnp.int32)


def _sb_fwd(qkv, name):
    S, D3 = qkv.shape
    D = D3 // 3
    npair, nb = D // LANES, S // BLK
    P = min(SB_FWD_PAIRS, npair)
    ngroup = npair // P
    W = P * LANES

    def body(q_ref, k_ref, v_ref, o_ref):
        i = pl.program_id(1)
        m0 = _head_masks()
        row = lax.broadcasted_iota(jnp.int32, (BLK, BLK), 0)
        col = lax.broadcasted_iota(jnp.int32, (BLK, BLK), 1)
        strict = col < row
        tri_excl = jnp.where(row > col, 1.0, 0.0).astype(BF16)
        zq = jnp.zeros((BLK, LANES), BF16)
        qh = []
        for p in range(P):
            q2 = q_ref[:, p * LANES:(p + 1) * LANES]
            qh += [jnp.where(m0, q2, zq), jnp.where(m0, zq, q2)]

        def block(j, carry, acc, diag):
            off = pl.multiple_of(j * BLK, BLK)
            ks = [k_ref[pl.ds(off, BLK), p * LANES:(p + 1) * LANES] for p in range(P)]
            vh = []
            for p in range(P):
                v2 = v_ref[pl.ds(off, BLK), p * LANES:(p + 1) * LANES]
                vh += [jnp.where(m0, v2, zq), jnp.where(m0, zq, v2)]
            _, b_l, w_l = _sb_scores(qh, ks, carry, diag, tri_excl, strict)
            wb = [w.astype(BF16) for w in w_l]
            new_acc = [acc[p] + _dot(wb[2 * p], vh[2 * p], NN) + _dot(wb[2 * p + 1], vh[2 * p + 1], NN)
                       for p in range(P)]
            new_carry = [carry[n] + jnp.sum(b_l[n], axis=1, keepdims=True) for n in range(2 * P)]
            return new_carry, new_acc

        c0 = jnp.zeros((BLK, 1), F32)
        carry, acc = block(i, [c0] * (2 * P), [jnp.zeros((BLK, LANES), F32)] * P, True)

        def cond(st):
            return jnp.logical_and(st[0] >= 0, st[1] > 0)

        def step(st):
            j, _, carry, acc = st
            carry, acc = block(j, carry, acc, False)
            return j - 1, _any_alive(carry), carry, acc

        st = lax.while_loop(cond, step, (i - 1, _any_alive(carry), carry, acc))
        for p in range(P):
            o_ref[:, p * LANES:(p + 1) * LANES] = st[3][p]

    return pl.pallas_call(
        body, name=name, out_shape=jax.ShapeDtypeStruct((S, D), F32),
        grid=(ngroup, nb),
        in_specs=[pl.BlockSpec((BLK, W), lambda g, i: (i, g)),
                  pl.BlockSpec((S, W), lambda g, i: (0, ngroup + g)),
                  pl.BlockSpec((S, W), lambda g, i: (0, 2 * ngroup + g))],
        out_specs=pl.BlockSpec((BLK, W), lambda g, i: (i, g)),
        compiler_params=_params("arbitrary", "arbitrary"),
    )(qkv, qkv, qkv)


def _sb_bwd(qkv, o, do, name):
    S, D3 = qkv.shape
    D = D3 // 3
    npair, nb = D // LANES, S // BLK
    P = min(SB_BWD_PAIRS, npair)
    ngroup = npair // P
    W = P * LANES

    def body(q_ref, k_ref, v_ref, o_ref, do_ref, dq_ref, dk_ref, dv_ref):
        i = pl.program_id(1)
        m0 = _head_masks()
        row = lax.broadcasted_iota(jnp.int32, (BLK, BLK), 0)
        col = lax.broadcasted_iota(jnp.int32, (BLK, BLK), 1)
        strict = col < row
        tri_excl = jnp.where(row > col, 1.0, 0.0).astype(BF16)
        tri_incl = jnp.where(row >= col, 1.0, 0.0).astype(BF16)
        zq = jnp.zeros((BLK, LANES), BF16)
        qh, doh, delta = [], [], []
        for p in range(P):
            sl = slice(p * LANES, (p + 1) * LANES)
            q2, do2 = q_ref[:, sl], do_ref[:, sl]
            qh += [jnp.where(m0, q2, zq), jnp.where(m0, zq, q2)]
            doh += [jnp.where(m0, do2, zq), jnp.where(m0, zq, do2)]
            prod = do2.astype(F32) * o_ref[:, sl]
            delta += [jnp.sum(jnp.where(m0, prod, 0.0), axis=1, keepdims=True),
                      jnp.sum(jnp.where(m0, 0.0, prod), axis=1, keepdims=True)]

        @pl.when(i == 0)
        def _():
            dk_ref[...] = jnp.zeros_like(dk_ref)
            dv_ref[...] = jnp.zeros_like(dv_ref)

        def block(j, cb, cg, dq, diag):
            off = pl.multiple_of(j * BLK, BLK)
            nh = 2 * P
            ks = [k_ref[pl.ds(off, BLK), p * LANES:(p + 1) * LANES] for p in range(P)]
            vs = [v_ref[pl.ds(off, BLK), p * LANES:(p + 1) * LANES] for p in range(P)]
            kh = []
            for k2 in ks:
                kh += [jnp.where(m0, k2, zq), jnp.where(m0, zq, k2)]
            dws = [_dot(doh[n], vs[n // 2], NT) for n in range(nh)]
            a_l, b_l, w_l = _sb_scores(qh, ks, cb, diag, tri_excl, strict)
            wb = [w.astype(BF16) for w in w_l]
            g_l = [dws[n] * wb[n].astype(F32) for n in range(nh)]
            gsplit = [_split_bf16(g) for g in g_l]
            gincs = [_dot(hi, tri_incl, NN) + _dot(lo, tri_incl, NN) for hi, lo in gsplit]
            dzs = []
            for n in range(nh):
                beta = jnp.exp(a_l[n])
                dz = g_l[n] * (1.0 - beta) - beta * (delta[n] - (gincs[n] + cg[n]))
                if diag:
                    dz = jnp.where(strict, dz, 0.0)
                dzs.append((dz * ATTN_SCALE).astype(BF16))
            ndq = [dq[p] + _dot(dzs[2 * p], kh[2 * p], NN) + _dot(dzs[2 * p + 1], kh[2 * p + 1], NN)
                   for p in range(P)]
            for p in range(P):
                sl = slice(p * LANES, (p + 1) * LANES)
                dk_ref[pl.ds(off, BLK), sl] += (_dot(dzs[2 * p], qh[2 * p], TN)
                                                + _dot(dzs[2 * p + 1], qh[2 * p + 1], TN))
                dv_ref[pl.ds(off, BLK), sl] += (_dot(wb[2 * p], doh[2 * p], TN)
                                                + _dot(wb[2 * p + 1], doh[2 * p + 1], TN))
            ncb = [cb[n] + jnp.sum(b_l[n], axis=1, keepdims=True) for n in range(nh)]
            ncg = [cg[n] + jnp.sum(g_l[n], axis=1, keepdims=True) for n in range(nh)]
            return ncb, ncg, ndq

        c0 = jnp.zeros((BLK, 1), F32)
        cb, cg, dq = block(i, [c0] * (2 * P), [c0] * (2 * P), [jnp.zeros((BLK, LANES), F32)] * P, True)

        def cond(st):
            return jnp.logical_and(st[0] >= 0, st[1] > 0)

        def step(st):
            j, _, cb, cg, dq = st
            cb, cg, dq = block(j, cb, cg, dq, False)
            return j - 1, _any_alive(cb), cb, cg, dq

        st = lax.while_loop(cond, step, (i - 1, _any_alive(cb), cb, cg, dq))
        for p in range(P):
            dq_ref[:, p * LANES:(p + 1) * LANES] = st[4][p].astype(BF16)

    blk = lambda c: pl.BlockSpec((BLK, W), lambda g, i: (i, c * ngroup + g))
    col_all = lambda c: pl.BlockSpec((S, W), lambda g, i: (0, c * ngroup + g))
    return pl.pallas_call(
        body, name=name,
        out_shape=(jax.ShapeDtypeStruct((S, D), BF16), jax.ShapeDtypeStruct((S, D), F32),
                   jax.ShapeDtypeStruct((S, D), F32)),
        grid=(ngroup, nb),
        in_specs=[blk(0), col_all(1), col_all(2), blk(0), blk(0)],
        out_specs=(blk(0), col_all(0), col_all(0)),
        compiler_params=_params("arbitrary", "arbitrary"),
    )(qkv, qkv, qkv, o, do)


SWA_Q_GROUPS = 4


def _roll_heads(x):
    return pltpu.roll(x.astype(F32), HEAD_DIM, 1).astype(BF16)


def _swa_valid(i):
    r = lax.broadcasted_iota(jnp.int32, (BLK, 2 * BLK), 0)
    c = lax.broadcasted_iota(jnp.int32, (BLK, 2 * BLK), 1)
    diff = r + BLK - c
    return (diff >= 0) & (diff < BLK) & ((i > 0) | (c >= BLK))


def _swa_probs(qm, ksel, valid, sink):
    z = _dot(qm, ksel, NT) * ATTN_SCALE
    z = jnp.where(valid, z, NEG_BIG)
    mx = jnp.maximum(jnp.max(z, axis=1, keepdims=True), sink)
    p = jnp.exp(z - mx)
    ps = jnp.exp(sink - mx)
    inv = 1.0 / (jnp.sum(p, axis=1, keepdims=True) + ps)
    return p * inv, ps * inv


def _swa_fwd(q, kv, sinks, name):
    S, D = q.shape
    nkvp = kv.shape[1] // (2 * LANES)
    nb = S // BLK
    qw = SWA_Q_GROUPS * LANES

    def body(q_ref, kc_ref, kp_ref, vc_ref, vp_ref, s_ref, o_ref):
        m, i = pl.program_id(0), pl.program_id(1)
        m0 = _head_masks()
        valid = _swa_valid(i)
        kk = jnp.concatenate([kp_ref[...], kc_ref[...]], axis=0)
        vv = jnp.concatenate([vp_ref[...], vc_ref[...]], axis=0)
        ksw, vsw = _roll_heads(kk), _roll_heads(vv)
        m0k = jnp.concatenate([m0, m0], axis=0)
        zv = jnp.zeros_like(vv)
        for c in range(SWA_Q_GROUPS):
            qc = q_ref[:, c * LANES:(c + 1) * LANES]
            zq = jnp.zeros_like(qc)
            w_half = c // 2
            acc = jnp.zeros((BLK, LANES), F32)
            for u in range(2):
                qm = jnp.where(m0, qc, zq) if u == 0 else jnp.where(m0, zq, qc)
                same = u == w_half
                ksel = kk if same else ksw
                vsel = vv if same else vsw
                vsel = jnp.where(m0k, vsel, zv) if u == 0 else jnp.where(m0k, zv, vsel)
                sink = s_ref[0, m * 2 * SWA_Q_GROUPS + 2 * c + u]
                p, _ = _swa_probs(qm, ksel, valid, sink)
                acc = acc + _dot(p.astype(BF16), vsel, NN)
            o_ref[:, c * LANES:(c + 1) * LANES] = acc

    prev = lambda i: jnp.maximum(i - 1, 0)
    return pl.pallas_call(
        body, name=name, out_shape=jax.ShapeDtypeStruct((S, D), F32),
        grid=(nkvp, nb),
        in_specs=[pl.BlockSpec((BLK, qw), lambda m, i: (i, m)),
                  pl.BlockSpec((BLK, LANES), lambda m, i: (i, m)),
                  pl.BlockSpec((BLK, LANES), lambda m, i: (prev(i), m)),
                  pl.BlockSpec((BLK, LANES), lambda m, i: (i, nkvp + m)),
                  pl.BlockSpec((BLK, LANES), lambda m, i: (prev(i), nkvp + m)),
                  pl.BlockSpec(memory_space=pltpu.SMEM)],
        out_specs=pl.BlockSpec((BLK, qw), lambda m, i: (i, m)),
        compiler_params=_params("arbitrary", "arbitrary"),
    )(q, kv, kv, kv, kv, sinks)


def _swa_bwd(q, kv, sinks, o, do, name):
    S, D = q.shape
    nkvp = kv.shape[1] // (2 * LANES)
    nb = S // BLK
    qw = SWA_Q_GROUPS * LANES
    nh = 2 * SWA_Q_GROUPS

    def body(q_ref, kc_ref, kp_ref, vc_ref, vp_ref, s_ref, o_ref, do_ref, dq_ref, dk_ref, dv_ref, ds_ref):
        m, i = pl.program_id(0), pl.program_id(1)
        m0 = _head_masks()
        valid = _swa_valid(i)
        kk = jnp.concatenate([kp_ref[...], kc_ref[...]], axis=0)
        vv = jnp.concatenate([vp_ref[...], vc_ref[...]], axis=0)
        ksw, vsw = _roll_heads(kk), _roll_heads(vv)
        m0k = jnp.concatenate([m0, m0], axis=0)
        zk = jnp.zeros_like(kk)

        @pl.when(i == 0)
        def _():
            dk_ref[...] = jnp.zeros_like(dk_ref)
            dv_ref[...] = jnp.zeros_like(dv_ref)
            ds_ref[...] = jnp.zeros_like(ds_ref)

        acc = {(t, s): jnp.zeros((2 * BLK, LANES), F32) for t in ("k", "v") for s in (True, False)}
        for c in range(SWA_Q_GROUPS):
            qc = q_ref[:, c * LANES:(c + 1) * LANES]
            doc = do_ref[:, c * LANES:(c + 1) * LANES]
            prod = doc.astype(F32) * o_ref[:, c * LANES:(c + 1) * LANES]
            zq = jnp.zeros_like(qc)
            w_half = c // 2
            dq = jnp.zeros((BLK, LANES), F32)
            for u in range(2):
                sel = (lambda x, z: jnp.where(m0, x, z)) if u == 0 else (lambda x, z: jnp.where(m0, z, x))
                selk = (lambda x, z: jnp.where(m0k, x, z)) if u == 0 else (lambda x, z: jnp.where(m0k, z, x))
                qm, dom = sel(qc, zq), sel(doc, zq)
                same = u == w_half
                ksel = kk if same else ksw
                vsel = vv if same else vsw
                hh = 2 * c + u
                sink = s_ref[0, m * nh + hh]
                p, ps = _swa_probs(qm, ksel, valid, sink)
                delta = jnp.sum(sel(prod, 0.0), axis=1, keepdims=True)
                dp = _dot(dom, vsel, NT)
                dsc = (p * (dp - delta) * ATTN_SCALE).astype(BF16)
                dq = dq + _dot(dsc, selk(ksel, zk), NN)
                acc["k", same] = acc["k", same] + _dot(dsc, qm, TN)
                acc["v", same] = acc["v", same] + _dot(p.astype(BF16), dom, TN)
                dsink = jnp.sum(jnp.broadcast_to(-(ps * delta), (BLK, LANES)), axis=0, keepdims=True)
                ds_ref[0, hh:hh + 1, :] += dsink
            dq_ref[:, c * LANES:(c + 1) * LANES] = dq
        dkk = acc["k", True] + pltpu.roll(acc["k", False], HEAD_DIM, 1)
        dvv = acc["v", True] + pltpu.roll(acc["v", False], HEAD_DIM, 1)
        poff = pl.multiple_of(jnp.maximum(i - 1, 0) * BLK, BLK)
        coff = pl.multiple_of(i * BLK, BLK)
        dk_ref[pl.ds(poff, BLK), :] += dkk[:BLK]
        dv_ref[pl.ds(poff, BLK), :] += dvv[:BLK]
        dk_ref[pl.ds(coff, BLK), :] += dkk[BLK:]
        dv_ref[pl.ds(coff, BLK), :] += dvv[BLK:]

    prev = lambda i: jnp.maximum(i - 1, 0)
    qblk = pl.BlockSpec((BLK, qw), lambda m, i: (i, m))
    col_all = pl.BlockSpec((S, LANES), lambda m, i: (0, m))
    return pl.pallas_call(
        body, name=name,
        out_shape=(jax.ShapeDtypeStruct((S, D), F32),
                   jax.ShapeDtypeStruct((S, nkvp * LANES), F32),
                   jax.ShapeDtypeStruct((S, nkvp * LANES), F32),
                   jax.ShapeDtypeStruct((nkvp, nh, LANES), F32)),
        grid=(nkvp, nb),
        in_specs=[qblk,
                  pl.BlockSpec((BLK, LANES), lambda m, i: (i, m)),
                  pl.BlockSpec((BLK, LANES), lambda m, i: (prev(i), m)),
                  pl.BlockSpec((BLK, LANES), lambda m, i: (i, nkvp + m)),
                  pl.BlockSpec((BLK, LANES), lambda m, i: (prev(i), nkvp + m)),
                  pl.BlockSpec(memory_space=pltpu.SMEM),
                  qblk, qblk],
        out_specs=(qblk, col_all, col_all, pl.BlockSpec((1, nh, LANES), lambda m, i: (m, 0, 0))),
        compiler_params=_params("arbitrary", "arbitrary"),
    )(q, kv, kv, kv, kv, sinks, o, do)


def _all_gather(x, name):
    R, C = x.shape

    def body(x_ref, out_ref, send_sems, recv_sems, local_sem):
        x_, y_, c_ = lax.axis_index("x"), lax.axis_index("y"), lax.axis_index("c")
        me, sibling = (x_, y_, c_), (x_, y_, 1 - c_)
        chips = [(1 - x_, y_), (x_, 1 - y_), (1 - x_, 1 - y_)]

        def rows(px, py, pc):
            return out_ref.at[4 * px + 2 * py + pc]

        def copy(k, block, to, src=None):
            return pltpu.make_async_remote_copy(
                src_ref=rows(*block) if src is None else src, dst_ref=rows(*block),
                send_sem=send_sems.at[k], recv_sem=recv_sems.at[k],
                device_id=to, device_id_type=MESH)

        mine = pltpu.make_async_copy(x_ref, rows(*me), local_sem)
        mine.start()
        first = [copy(0, me, sibling, src=x_ref)]
        first += [copy(1 + j, me, (*chip, c_), src=x_ref) for j, chip in enumerate(chips)]
        for cp in first:
            cp.start()
        passed = [copy(4 + j, (*chip, c_), sibling) for j, chip in enumerate(chips)]
        for j, chip in enumerate(chips):
            copy(1 + j, (*chip, c_), me).wait_recv()
            passed[j].start()
        copy(0, sibling, me).wait_recv()
        for j, chip in enumerate(chips):
            copy(4 + j, (*chip, 1 - c_), me).wait_recv()
        for cp in first + passed:
            cp.wait_send()
        mine.wait()

    return pl.pallas_call(
        body, name=name, out_shape=jax.ShapeDtypeStruct((N_DEV, R, C), x.dtype),
        in_specs=[pl.BlockSpec(memory_space=pl.ANY)], out_specs=pl.BlockSpec(memory_space=pl.ANY),
        scratch_shapes=[pltpu.SemaphoreType.DMA((7,)), pltpu.SemaphoreType.DMA((7,)), pltpu.SemaphoreType.DMA],
    )(x)


def _all_to_all(blocks, name):
    _, R, C = blocks.shape

    def body(b_ref, out_ref, send_sems, recv_sems, local_sem):
        x_, y_, c_ = lax.axis_index("x"), lax.axis_index("y"), lax.axis_index("c")
        my_idx = 4 * x_ + 2 * y_ + c_
        mine = pltpu.make_async_copy(b_ref.at[my_idx], out_ref.at[my_idx], local_sem)
        mine.start()
        copies = []
        for k in range(1, N_DEV):
            px = x_ ^ ((k >> 2) & 1)
            py = y_ ^ ((k >> 1) & 1)
            pc = c_ ^ (k & 1)
            copies.append(pltpu.make_async_remote_copy(
                src_ref=b_ref.at[4 * px + 2 * py + pc], dst_ref=out_ref.at[my_idx],
                send_sem=send_sems.at[k - 1], recv_sem=recv_sems.at[k - 1],
                device_id=(px, py, pc), device_id_type=MESH))
        for cp in copies:
            cp.start()
        for cp in copies:
            cp.wait_recv()
        for cp in copies:
            cp.wait_send()
        mine.wait()

    return pl.pallas_call(
        body, name=name, out_shape=jax.ShapeDtypeStruct((N_DEV, R, C), blocks.dtype),
        in_specs=[pl.BlockSpec(memory_space=pl.ANY)], out_specs=pl.BlockSpec(memory_space=pl.ANY),
        scratch_shapes=[pltpu.SemaphoreType.DMA((7,)), pltpu.SemaphoreType.DMA((7,)), pltpu.SemaphoreType.DMA],
    )(blocks)


def _sum_adamw(parts, w, m, v, name):
    _, R, C = parts.shape
    tr = _tile(R, 256, 16)
    c1 = 1.0 - ADAM_B1 ** ADAM_STEP
    c2 = 1.0 - ADAM_B2 ** ADAM_STEP

    def body(p_ref, w_ref, m_ref, v_ref, g_ref, d_ref, nm_ref, nv_ref):
        g = p_ref[0].astype(F32)
        for s in range(1, N_DEV):
            g = g + p_ref[s].astype(F32)
        nm = ADAM_B1 * m_ref[...] + (1.0 - ADAM_B1) * g
        nv = ADAM_B2 * v_ref[...] + (1.0 - ADAM_B2) * (g * g)
        m_hat = nm / c1
        v_hat = nv / c2
        g_ref[...] = g
        nm_ref[...] = nm
        nv_ref[...] = nv
        d_ref[...] = -ADAM_LR * (m_hat / (jnp.sqrt(v_hat) + ADAM_EPS) + ADAM_WD * w_ref[...])

    row = pl.BlockSpec((tr, C), lambda i: (i, 0))
    shp = jax.ShapeDtypeStruct((R, C), F32)
    return pl.pallas_call(
        body, name=name, out_shape=(shp, shp, shp, shp),
        grid=(R // tr,),
        in_specs=[pl.BlockSpec((N_DEV, tr, C), lambda i: (0, i, 0)), row, row, row],
        out_specs=(row, row, row, row),
        compiler_params=_params("parallel"),
    )(parts, w, m, v)


def _ffn_fwd(h, g, wg, wu, wo, tag):
    xn, gate, up, act = _ffn_up(h, g, wg, wu, f"{tag}_up")
    out = _mm(act, wo, NN, F32, f"{tag}_down", scale=FFN_RES_SCALE, res=h, tm=512, tn=1024, tk=2816)
    return out, (xn, gate, up, act)


def _ffn_bwd(dh, h, g, wg, wu, wo, saved, tag):
    xn, gate, up, act = saved
    dgate, dup = _ffn_dact(dh, wo, gate, up, f"{tag}_dact")
    dwo = _mm(act, dh, TN, BF16, f"{tag}_dwo", scale=FFN_RES_SCALE, tm=1408, tn=1024, tk=512)
    dwg = _mm(xn, dgate, TN, BF16, f"{tag}_dwg", tm=1024, tn=1408, tk=512)
    dwu = _mm(xn, dup, TN, BF16, f"{tag}_dwu", tm=1024, tn=1408, tk=512)
    dh_in, dg = _dx_norm_bwd([(dgate, wg), (dup, wu)], h, g, dh, f"{tag}_dx")
    return dh_in, dg, dwg, dwu, dwo


def _proj(a, w, out_dtype, name, res=None):
    return _mm(a, w, NN, out_dtype, name, res=res, tm=1024, tn=1024, tk=1024)


def _proj_dw(x, dy, name):
    return _mm(x, dy, TN, BF16, name, tm=1024, tn=1024, tk=512)


def _col_shards_to_full(g, lead):
    nl = len(lead)
    perm = tuple(range(1, nl + 2)) + (0, nl + 2)
    t = jnp.transpose(g, perm)
    return t.reshape(t.shape[:nl + 1] + (t.shape[nl + 1] * t.shape[nl + 2],))


def _full_to_col_shards(w):
    nl = w.ndim - 2
    t = w.reshape(w.shape[:-1] + (N_DEV, w.shape[-1] // N_DEV))
    perm = (nl + 1,) + tuple(range(nl + 1)) + (nl + 2,)
    return jnp.transpose(t, perm)


def kernel(x, ffn1_norm, ffn1_w_in, ffn1_w_out, mix_norm, ffn2_norm, ffn2_w_in, ffn2_w_out, sb_w_qkv, sb_w_o, kv_norm, kv_w, swa_w_q, swa_sinks, swa_w_o, final_norm, loss_target, m_ffn1_norm, m_ffn1_w_in, m_ffn1_w_out, m_mix_norm, m_ffn2_norm, m_ffn2_w_in, m_ffn2_w_out, m_sb_w_qkv, m_sb_w_o, m_kv_norm, m_kv_w, m_swa_w_q, m_swa_sinks, m_swa_w_o, m_final_norm, v_ffn1_norm, v_ffn1_w_in, v_ffn1_w_out, v_mix_norm, v_ffn2_norm, v_ffn2_w_in, v_ffn2_w_out, v_sb_w_qkv, v_sb_w_o, v_kv_norm, v_kv_w, v_swa_w_q, v_swa_sinks, v_swa_w_o, v_final_norm):
    S, D = x.shape[1], x.shape[2]
    F = ffn1_w_out.shape[1] * N_DEV
    L = ffn1_w_in.shape[0]
    KV = kv_w.shape[1]
    assert L == 2 and D % PACK_W == 0 and swa_sinks.shape == (1, 2 * SWA_Q_GROUPS * KV // (2 * LANES))

    big_w = [ffn1_w_in, ffn1_w_out, ffn2_w_in, ffn2_w_out, sb_w_qkv, sb_w_o, kv_w, swa_w_q, swa_w_o]
    big_m = [m_ffn1_w_in, m_ffn1_w_out, m_ffn2_w_in, m_ffn2_w_out, m_sb_w_qkv, m_sb_w_o, m_kv_w, m_swa_w_q, m_swa_w_o]
    big_v = [v_ffn1_w_in, v_ffn1_w_out, v_ffn2_w_in, v_ffn2_w_out, v_sb_w_qkv, v_sb_w_o, v_kv_w, v_swa_w_q, v_swa_w_o]
    rows = [w.size // PACK_W for w in big_w]
    offs = [sum(rows[:i]) for i in range(len(rows))]

    def pack(ts, dtype):
        return jnp.concatenate([t.astype(dtype).reshape(-1, PACK_W) for t in ts], axis=0)

    gath = _all_gather(pack(big_w, BF16), "gather_weights")

    def part(i, shape):
        return gath[:, offs[i]:offs[i] + rows[i]].reshape((N_DEV,) + shape)

    w_in1 = _col_shards_to_full(part(0, ffn1_w_in.shape), (L,))
    w_out1 = jnp.transpose(part(1, ffn1_w_out.shape), (1, 0, 2, 3)).reshape(L, F, D)
    w_in2 = _col_shards_to_full(part(2, ffn2_w_in.shape), (L,))
    w_out2 = jnp.transpose(part(3, ffn2_w_out.shape), (1, 0, 2, 3)).reshape(L, F, D)
    w_qkv = _col_shards_to_full(part(4, sb_w_qkv.shape[1:]), ())
    w_sbo = part(5, sb_w_o.shape[1:]).reshape(D, D)
    w_kv = part(6, kv_w.shape).reshape(D, KV)
    w_q = part(7, swa_w_q.shape[1:]).reshape(D, D)
    w_swo = part(8, swa_w_o.shape[1:]).reshape(D, D)

    def ffn_weights(w_in, w_out, layer):
        return w_in[layer, :, :F], w_in[layer, :, F:], w_out[layer]

    cos_t, sin_t = _rope_tables(S)
    h0 = x.reshape(S, D)
    tgt = loss_target.reshape(S, D)

    fw = [ffn_weights(w_in1, w_out1, 0), ffn_weights(w_in2, w_out2, 0),
          ffn_weights(w_in1, w_out1, 1), ffn_weights(w_in2, w_out2, 1)]
    h1, sv_a1 = _ffn_fwd(h0, ffn1_norm[0], *fw[0], "ffn1a")
    hn_a = _rmsnorm(h1, mix_norm[0], "mix_a_norm")
    qkv = _proj(hn_a, w_qkv, BF16, "sb_qkv")
    o_sb = _sb_fwd(qkv, "sb_attn")
    h2 = _proj(o_sb, w_sbo, F32, "sb_out", res=h1)
    h3, sv_a2 = _ffn_fwd(h2, ffn2_norm[0], *fw[1], "ffn2a")
    kvn = _rmsnorm(h3, kv_norm, "kv_norm")
    kv_raw = _proj(kvn, w_kv, F32, "kv_proj")
    kv_rot = _rotary(kv_raw, cos_t, sin_t, KV // (2 * LANES), False, "kv_rope")
    h4, sv_b1 = _ffn_fwd(h3, ffn1_norm[1], *fw[2], "ffn1b")
    hn_b = _rmsnorm(h4, mix_norm[1], "mix_b_norm")
    q_raw = _proj(hn_b, w_q, F32, "swa_q")
    q_rot = _rotary(q_raw, cos_t, sin_t, D // LANES, False, "q_rope")
    o_sw = _swa_fwd(q_rot, kv_rot, swa_sinks, "swa_attn")
    h5 = _proj(o_sw, w_swo, F32, "swa_out", res=h4)
    h6, sv_b2 = _ffn_fwd(h5, ffn2_norm[1], *fw[3], "ffn2b")
    dh6, dg_final, sq_err = _final_loss(h6, final_norm, tgt, "final_loss")
    loss = lax.psum(0.5 * jnp.sum(sq_err) / D, ("x", "y", "c"))

    dh5, dg_f2b, dwg_f2b, dwu_f2b, dwo_f2b = _ffn_bwd(dh6, h5, ffn2_norm[1], *fw[3], sv_b2, "ffn2b")
    do_sw = _mm(dh5, w_swo, NT, BF16, "swa_out_dx", tm=1024, tn=1024, tk=1024)
    dw_swo = _proj_dw(o_sw, dh5, "swa_out_dw")
    dq_rot, dk_sw, dv_sw, dsink = _swa_bwd(q_rot, kv_rot, swa_sinks, o_sw, do_sw, "swa_attn_bwd")
    dq = _rotary(dq_rot, cos_t, sin_t, D // LANES, True, "q_rope_bwd")
    dw_q = _proj_dw(hn_b, dq, "swa_q_dw")
    dh4, dg_mix_b = _dx_norm_bwd([(dq, w_q)], h4, mix_norm[1], dh5, "swa_q_dx")
    dh3, dg_f1b, dwg_f1b, dwu_f1b, dwo_f1b = _ffn_bwd(dh4, h3, ffn1_norm[1], *fw[2], sv_b1, "ffn1b")
    dkv = _rotary(jnp.concatenate([dk_sw, dv_sw], axis=1), cos_t, sin_t, KV // (2 * LANES), True, "kv_rope_bwd")
    dw_kv = _proj_dw(kvn, dkv, "kv_proj_dw")
    dh3, dg_kv = _dx_norm_bwd([(dkv, w_kv)], h3, kv_norm, dh3, "kv_proj_dx")
    dh2, dg_f2a, dwg_f2a, dwu_f2a, dwo_f2a = _ffn_bwd(dh3, h2, ffn2_norm[0], *fw[1], sv_a2, "ffn2a")
    do_sb = _mm(dh2, w_sbo, NT, BF16, "sb_out_dx", tm=1024, tn=1024, tk=1024)
    dw_sbo = _proj_dw(o_sb, dh2, "sb_out_dw")
    dq_sb, dk_sb, dv_sb = _sb_bwd(qkv, o_sb, do_sb, "sb_attn_bwd")
    dqkv = jnp.concatenate([dq_sb, dk_sb.astype(BF16), dv_sb.astype(BF16)], axis=1)
    dw_qkv = _proj_dw(hn_a, dqkv, "sb_qkv_dw")
    dh1, dg_mix_a = _dx_norm_bwd([(dqkv, w_qkv)], h1, mix_norm[0], dh2, "sb_qkv_dx")
    dx, dg_f1a, dwg_f1a, dwu_f1a, dwo_f1a = _ffn_bwd(dh1, h0, ffn1_norm[0], *fw[0], sv_a1, "ffn1a")

    def w_in_blocks(dwg_a, dwu_a, dwg_b, dwu_b):
        full = jnp.stack([jnp.concatenate([dwg_a, dwu_a], axis=1), jnp.concatenate([dwg_b, dwu_b], axis=1)])
        return _full_to_col_shards(full)

    def w_out_blocks(dwo_a, dwo_b):
        full = jnp.stack([dwo_a, dwo_b]).reshape(L, N_DEV, F // N_DEV, D)
        return jnp.transpose(full, (1, 0, 2, 3))

    blocks = [w_in_blocks(dwg_f1a, dwu_f1a, dwg_f1b, dwu_f1b), w_out_blocks(dwo_f1a, dwo_f1b),
              w_in_blocks(dwg_f2a, dwu_f2a, dwg_f2b, dwu_f2b), w_out_blocks(dwo_f2a, dwo_f2b),
              _full_to_col_shards(dw_qkv), dw_sbo, dw_kv, dw_q, dw_swo]
    gblocks = jnp.concatenate([b.reshape(N_DEV, -1, PACK_W) for b in blocks], axis=1)
    parts = _all_to_all(gblocks, "scatter_grads")
    g_big, d_big, nm_big, nv_big = _sum_adamw(parts, pack(big_w, F32), pack(big_m, F32), pack(big_v, F32), "adamw_big")

    def unpack(flat):
        return [flat[offs[i]:offs[i] + rows[i]].reshape(big_w[i].shape) for i in range(len(big_w))]

    small_w = [ffn1_norm, mix_norm, ffn2_norm, kv_norm, final_norm, swa_sinks]
    small_m = [m_ffn1_norm, m_mix_norm, m_ffn2_norm, m_kv_norm, m_final_norm, m_swa_sinks]
    small_v = [v_ffn1_norm, v_mix_norm, v_ffn2_norm, v_kv_norm, v_final_norm, v_swa_sinks]
    SMALL_ROWS = 16

    def pack_small(ts):
        rows_ = [t.reshape(-1, D) for t in ts[:-1]]
        sink_row = jnp.pad(ts[-1].reshape(1, -1), ((0, 0), (0, D - ts[-1].size)))
        flat = jnp.concatenate(rows_ + [sink_row], axis=0)
        return jnp.pad(flat, ((0, SMALL_ROWS - flat.shape[0]), (0, 0)))

    def gain(parts8):
        return jnp.sum(parts8, axis=0, keepdims=True)

    g_small_local = pack_small([
        jnp.concatenate([gain(dg_f1a), gain(dg_f1b)], axis=0),
        jnp.concatenate([gain(dg_mix_a), gain(dg_mix_b)], axis=0),
        jnp.concatenate([gain(dg_f2a), gain(dg_f2b)], axis=0),
        gain(dg_kv), gain(dg_final), dsink[:, :, 0].reshape(1, -1)])
    small_parts = _all_gather(g_small_local, "gather_small_grads")
    g_sm, d_sm, nm_sm, nv_sm = _sum_adamw(small_parts, pack_small(small_w), pack_small(small_m), pack_small(small_v), "adamw_small")

    def unpack_small(flat):
        out, r = [], 0
        for t in small_w[:-1]:
            n = t.size // D
            out.append(flat[r:r + n].reshape(t.shape))
            r += n
        out.append(flat[r, :swa_sinks.size].reshape(swa_sinks.shape))
        return out

    def ordered(big, small):
        f1w_in, f1w_out, f2w_in, f2w_out, qkv_, sbo_, kvw_, swq_, swo_ = big
        f1n, mixn, f2n, kvn_, finn, sinks_ = small
        return [f1n, f1w_in, f1w_out, mixn, f2n, f2w_in, f2w_out, qkv_, sbo_, kvn_, kvw_, swq_, sinks_, swo_, finn]

    outs = []
    for big_flat, small_flat in ((g_big, g_sm), (d_big, d_sm), (nm_big, nm_sm), (nv_big, nv_sm)):
        outs += ordered(unpack(big_flat), unpack_small(small_flat))
    return (loss, dx.reshape(x.shape), *outs)
```

```python
import jax
import jax.numpy as jnp
from jax import lax
from jax.experimental import pallas as pl
from jax.experimental.pallas import tpu as pltpu

F32 = jnp.float32
BF16 = jnp.bfloat16

N_DEV = 8
HEAD_DIM = 64
LANES = 128
BLK = 128
RMS_EPS = 1e-6
FFN_RES_SCALE = 0.5
ROPE_THETA = 10000.0
ATTN_SCALE = HEAD_DIM ** -0.5
SB_LOG_FLOOR = -110.0
NEG_BIG = -1e30
VMEM_LIMIT_V7X = 56 * 1024 * 1024

ADAM_LR = 0.001
ADAM_B1 = 0.9
ADAM_B2 = 0.999
ADAM_EPS = 1e-08
ADAM_WD = 0.01
ADAM_STEP = 10

NN = ((1,), (0,))
NT = ((1,), (1,))
TN = ((0,), (0,))
MESH = pl.DeviceIdType.MESH


def _dot(a, b, dims):
    return lax.dot_general(a, b, (dims, ((), ())), preferred_element_type=F32)


def _tile(n, pref, mult=LANES):
    if n <= pref:
        return n
    t = (pref // mult) * mult
    while t >= mult:
        if n % t == 0:
            return t
        t -= mult
    return n


def _params(*sem):
    return pltpu.CompilerParams(dimension_semantics=sem, vmem_limit_bytes=VMEM_LIMIT_V7X)


def _mm(a, b, dims, out_dtype, name, scale=1.0, res=None, tm=512, tn=512, tk=512):
    if dims == NN:
        (M, K), (_, N) = a.shape, b.shape
    elif dims == NT:
        (M, K), (N, _) = a.shape, b.shape
    else:
        (K, M), (_, N) = a.shape, b.shape
    tm, tn, tk = _tile(M, tm), _tile(N, tn), _tile(K, tk)
    nk = K // tk
    if dims == TN:
        a_spec = pl.BlockSpec((tk, tm), lambda i, j, k: (k, i))
    else:
        a_spec = pl.BlockSpec((tm, tk), lambda i, j, k: (i, k))
    if dims == NT:
        b_spec = pl.BlockSpec((tn, tk), lambda i, j, k: (j, k))
    else:
        b_spec = pl.BlockSpec((tk, tn), lambda i, j, k: (k, j))
    o_spec = pl.BlockSpec((tm, tn), lambda i, j, k: (i, j))
    has_res = res is not None

    def body(*refs):
        a_ref, b_ref = refs[0], refs[1]
        r_ref = refs[2] if has_res else None
        o_ref = refs[3] if has_res else refs[2]

        def finish(acc):
            r = acc * scale if scale != 1.0 else acc
            if has_res:
                r = r + r_ref[...]
            o_ref[...] = r.astype(out_dtype)

        p = _dot(a_ref[...].astype(BF16), b_ref[...].astype(BF16), dims)
        if nk == 1:
            finish(p)
        else:
            acc_ref = refs[-1]
            k = pl.program_id(2)

            @pl.when(k == 0)
            def _():
                acc_ref[...] = p

            @pl.when(k > 0)
            def _():
                acc_ref[...] += p

            @pl.when(k == nk - 1)
            def _():
                finish(acc_ref[...])

    in_specs = [a_spec, b_spec] + ([o_spec] if has_res else [])
    args = (a, b) + ((res,) if has_res else ())
    return pl.pallas_call(
        body, name=name,
        out_shape=jax.ShapeDtypeStruct((M, N), out_dtype),
        grid=(M // tm, N // tn, nk),
        in_specs=in_specs, out_specs=o_spec,
        scratch_shapes=[pltpu.VMEM((tm, tn), F32)] if nk > 1 else [],
        compiler_params=_params("parallel", "parallel", "arbitrary"),
    )(*args)


def _rows8(x):
    r, d = x.shape
    return jnp.sum(x.reshape(r // 8, 8, d), axis=0)


def _rmsnorm(h, g, name):
    S, D = h.shape
    ts = _tile(S, 512, 8)

    def body(h_ref, g_ref, o_ref):
        x = h_ref[...]
        r = lax.rsqrt(jnp.mean(x * x, axis=-1, keepdims=True) + RMS_EPS)
        o_ref[...] = ((x * r) * g_ref[...]).astype(BF16)

    return pl.pallas_call(
        body, name=name,
        out_shape=jax.ShapeDtypeStruct((S, D), BF16),
        grid=(S // ts,),
        in_specs=[pl.BlockSpec((ts, D), lambda i: (i, 0)), pl.BlockSpec((1, D), lambda i: (0, 0))],
        out_specs=pl.BlockSpec((ts, D), lambda i: (i, 0)),
        compiler_params=_params("parallel"),
    )(h, g.reshape(1, D))


def _final_loss(h, g, tgt, name):
    S, D = h.shape
    ts = _tile(S, 512, 8)

    def body(h_ref, g_ref, t_ref, dh_ref, dg_ref, l_ref):
        x = h_ref[...]
        r = lax.rsqrt(jnp.mean(x * x, axis=-1, keepdims=True) + RMS_EPS)
        xhat = x * r
        err = xhat * g_ref[...] - t_ref[...]
        d = err * (1.0 / D)
        dxh = d * g_ref[...]
        c = jnp.mean(dxh * xhat, axis=-1, keepdims=True)
        dh_ref[...] = r * (dxh - xhat * c)
        part = _rows8(d * xhat)
        lpart = _rows8(err * err)

        @pl.when(pl.program_id(0) == 0)
        def _():
            dg_ref[...] = part
            l_ref[...] = lpart

        @pl.when(pl.program_id(0) > 0)
        def _():
            dg_ref[...] += part
            l_ref[...] += lpart

    row = pl.BlockSpec((ts, D), lambda i: (i, 0))
    acc = pl.BlockSpec((8, D), lambda i: (0, 0))
    return pl.pallas_call(
        body, name=name,
        out_shape=(jax.ShapeDtypeStruct((S, D), F32), jax.ShapeDtypeStruct((8, D), F32),
                   jax.ShapeDtypeStruct((8, D), F32)),
        grid=(S // ts,),
        in_specs=[row, pl.BlockSpec((1, D), lambda i: (0, 0)), row],
        out_specs=(row, acc, acc),
        compiler_params=_params("arbitrary"),
    )(h, g.reshape(1, D), tgt)


def _ffn_up(h, g, win_t, name):
    S, D = h.shape
    F = win_t.shape[0] // 2
    tm, tn = _tile(S, 512, 16), _tile(F, 1408)
    nf = F // tn

    def body(h_ref, g_ref, wg_ref, wu_ref, xn_ref, gate_ref, up_ref, act_ref):
        x = h_ref[...]
        r = lax.rsqrt(jnp.mean(x * x, axis=-1, keepdims=True) + RMS_EPS)
        xn = ((x * r) * g_ref[...]).astype(BF16)
        xn_ref[...] = xn
        gate = _dot(xn, wg_ref[...], NT)
        up = _dot(xn, wu_ref[...], NT)
        gate_ref[...] = gate.astype(BF16)
        up_ref[...] = up.astype(BF16)
        sig = 1.0 / (1.0 + jnp.exp(-gate))
        act_ref[...] = (gate * sig * up).astype(BF16)

    row = pl.BlockSpec((tm, D), lambda i, j: (i, 0))
    blk = pl.BlockSpec((tm, tn), lambda i, j: (i, j))
    hid = jax.ShapeDtypeStruct((S, F), BF16)
    return pl.pallas_call(
        body, name=name, out_shape=(jax.ShapeDtypeStruct((S, D), BF16), hid, hid, hid),
        grid=(S // tm, nf),
        in_specs=[row, pl.BlockSpec((1, D), lambda i, j: (0, 0)),
                  pl.BlockSpec((tn, D), lambda i, j: (j, 0)),
                  pl.BlockSpec((tn, D), lambda i, j: (j + nf, 0))],
        out_specs=(row, blk, blk, blk),
        compiler_params=_params("arbitrary", "arbitrary"),
    )(h, g.reshape(1, D), win_t, win_t)


def _ffn_dact(dh, wo, gate, up, name):
    S, D = dh.shape
    F = wo.shape[0]
    tm, tn = _tile(S, 512, 16), _tile(F, 1408)

    def body(dh_ref, wo_ref, g_ref, u_ref, dg_ref, du_ref):
        d = _dot(dh_ref[...].astype(BF16), wo_ref[...], NT) * FFN_RES_SCALE
        g = g_ref[...].astype(F32)
        u = u_ref[...].astype(F32)
        sig = 1.0 / (1.0 + jnp.exp(-g))
        du_ref[...] = (d * (g * sig)).astype(BF16)
        dg_ref[...] = (d * u * (sig * (1.0 + g * (1.0 - sig)))).astype(BF16)

    blk = pl.BlockSpec((tm, tn), lambda j, i: (i, j))
    hid = jax.ShapeDtypeStruct((S, F), BF16)
    return pl.pallas_call(
        body, name=name, out_shape=(hid, hid),
        grid=(F // tn, S // tm),
        in_specs=[pl.BlockSpec((tm, D), lambda j, i: (i, 0)), pl.BlockSpec((tn, D), lambda j, i: (j, 0)), blk, blk],
        out_specs=(blk, blk),
        compiler_params=_params("arbitrary", "arbitrary"),
    )(dh, wo, gate, up)


def _dwin(dgate, dup, xn, name):
    S, F = dgate.shape
    D = xn.shape[1]
    tr, tk = _tile(F, 1408), _tile(S, 512, 16)
    nf, nk = F // tr, S // tk

    def body(dg_ref, du_ref, x_ref, o_ref, acc_ref):
        r, k = pl.program_id(0), pl.program_id(1)

        def accumulate(a_ref):
            p = _dot(a_ref[...], x_ref[...], TN)

            @pl.when(k == 0)
            def _():
                acc_ref[...] = p

            @pl.when(k > 0)
            def _():
                acc_ref[...] += p

        @pl.when(r < nf)
        def _():
            accumulate(dg_ref)

        @pl.when(r >= nf)
        def _():
            accumulate(du_ref)

        @pl.when(k == nk - 1)
        def _():
            o_ref[...] = acc_ref[...].astype(BF16)

    return pl.pallas_call(
        body, name=name, out_shape=jax.ShapeDtypeStruct((2 * F, D), BF16),
        grid=(2 * nf, nk),
        in_specs=[pl.BlockSpec((tk, tr), lambda r, k: (jnp.where(r < nf, k, 0), jnp.minimum(r, nf - 1))),
                  pl.BlockSpec((tk, tr), lambda r, k: (jnp.where(r >= nf, k, 0), jnp.maximum(r - nf, 0))),
                  pl.BlockSpec((tk, D), lambda r, k: (k, 0))],
        out_specs=pl.BlockSpec((tr, D), lambda r, k: (r, 0)),
        scratch_shapes=[pltpu.VMEM((tr, D), F32)],
        compiler_params=_params("arbitrary", "arbitrary"),
    )(dgate, dup, xn)


def _dx_norm_bwd(terms, h, g, res, name):
    S, D = h.shape
    tm = _tile(S, 256, 16)
    n = len(terms)

    def body(*refs):
        dy_refs, w_refs = refs[:n], refs[n:2 * n]
        h_ref, g_ref, r_ref, dh_ref, dg_ref = refs[2 * n:]
        d = _dot(dy_refs[0][...], w_refs[0][...], terms[0][2])
        for t in range(1, n):
            d = d + _dot(dy_refs[t][...], w_refs[t][...], terms[t][2])
        x = h_ref[...]
        r = lax.rsqrt(jnp.mean(x * x, axis=-1, keepdims=True) + RMS_EPS)
        xhat = x * r
        dxh = d * g_ref[...]
        c = jnp.mean(dxh * xhat, axis=-1, keepdims=True)
        dh_ref[...] = r * (dxh - xhat * c) + r_ref[...]
        part = _rows8(d * xhat)

        @pl.when(pl.program_id(0) == 0)
        def _():
            dg_ref[...] = part

        @pl.when(pl.program_id(0) > 0)
        def _():
            dg_ref[...] += part

    def w_spec(w, nblk, blk):
        return pl.BlockSpec((w.shape[0] // nblk, w.shape[1]), lambda i: (blk, 0))

    row = pl.BlockSpec((tm, D), lambda i: (i, 0))
    in_specs = [pl.BlockSpec((tm, t[0].shape[1]), lambda i: (i, 0)) for t in terms]
    in_specs += [w_spec(t[1], t[3], t[4]) for t in terms]
    in_specs += [row, pl.BlockSpec((1, D), lambda i: (0, 0)), row]
    return pl.pallas_call(
        body, name=name,
        out_shape=(jax.ShapeDtypeStruct((S, D), F32), jax.ShapeDtypeStruct((8, D), F32)),
        grid=(S // tm,),
        in_specs=in_specs,
        out_specs=(row, pl.BlockSpec((8, D), lambda i: (0, 0))),
        compiler_params=_params("arbitrary"),
    )(*[t[0] for t in terms], *[t[1] for t in terms], h, g.reshape(1, D), res)


def _rope_tables(S):
    half = HEAD_DIM // 2
    inv_freq = ROPE_THETA ** (-jnp.arange(half, dtype=F32) / half)
    ang = jnp.arange(S).astype(F32)[:, None] * inv_freq[None, :]
    cos, sin = jnp.cos(ang), jnp.sin(ang)
    cos_t = jnp.tile(cos, (1, LANES // half))
    sin_t = jnp.tile(jnp.concatenate([-sin, sin], axis=1), (1, LANES // HEAD_DIM))
    return cos_t, sin_t


def _swap_halves(x):
    lane = lax.broadcasted_iota(jnp.int32, x.shape, 1)
    first = (lane % HEAD_DIM) < (HEAD_DIM // 2)
    return jnp.where(first, pltpu.roll(x, LANES - HEAD_DIM // 2, 1), pltpu.roll(x, HEAD_DIM // 2, 1))


def _rotary(x, cos_t, sin_t, n_rot, inverse, name):
    S, C = x.shape
    ts = _tile(S, 512, 16)
    ng = C // LANES

    def body(x_ref, c_ref, s_ref, o_ref):
        cs, sn = c_ref[...], s_ref[...]
        for gidx in range(ng):
            sl = slice(gidx * LANES, (gidx + 1) * LANES)
            v = x_ref[:, sl].astype(F32)
            if gidx < n_rot:
                if inverse:
                    v = v * cs + _swap_halves(v * sn)
                else:
                    v = v * cs + _swap_halves(v) * sn
            o_ref[:, sl] = v.astype(BF16)

    row = pl.BlockSpec((ts, C), lambda i: (i, 0))
    tab = pl.BlockSpec((ts, LANES), lambda i: (i, 0))
    return pl.pallas_call(
        body, name=name, out_shape=jax.ShapeDtypeStruct((S, C), BF16),
        grid=(S // ts,), in_specs=[row, tab, tab], out_specs=row,
        compiler_params=_params("parallel"),
    )(x, cos_t, sin_t)


def _head_masks():
    lane = lax.broadcasted_iota(jnp.int32, (BLK, LANES), 1)
    return lane < HEAD_DIM


def _split_bf16(x):
    hi = x.astype(BF16)
    lo = (x - hi.astype(F32)).astype(BF16)
    return hi, lo


def _sb_scores(qh, ks, carry, diag, tri_excl, strict):
    n_heads = len(qh)
    zs = [_dot(qh[n], ks[n // 2], NT) for n in range(n_heads)]
    a_l, b_l, split_l = [], [], []
    for z in zs:
        z = z * ATTN_SCALE
        a = jnp.minimum(z, 0.0) - jnp.log(1.0 + jnp.exp(-jnp.abs(z)))
        b = a - z
        if diag:
            b = jnp.where(strict, b, 0.0)
        a_l.append(a)
        b_l.append(b)
        split_l.append(_split_bf16(b))
    sufs = [_dot(hi, tri_excl, NN) + _dot(lo, tri_excl, NN) for hi, lo in split_l]
    w_l = []
    for n in range(n_heads):
        w = jnp.exp(a_l[n] + sufs[n] + carry[n])
        if diag:
            w = jnp.where(strict, w, 0.0)
        w_l.append(w)
    return a_l, b_l, w_l


SB_FWD_PAIRS = 4
SB_BWD_PAIRS = 2


def _any_alive(carries):
    top = carries[0]
    for c in carries[1:]:
        top = jnp.maximum(top, c)
    return (jnp.max(top) > SB_LOG_FLOOR).astype(jnp.int32)


def _sb_fwd(qkv, name):
    S, D3 = qkv.shape
    D = D3 // 3
    npair, nb = D // LANES, S // BLK
    P = min(SB_FWD_PAIRS, npair)
    ngroup = npair // P
    W = P * LANES

    def body(q_ref, k_ref, v_ref, o_ref):
        i = pl.program_id(1)
        m0 = _head_masks()
        row = lax.broadcasted_iota(jnp.int32, (BLK, BLK), 0)
        col = lax.broadcasted_iota(jnp.int32, (BLK, BLK), 1)
        strict = col < row
        tri_excl = jnp.where(row > col, 1.0, 0.0).astype(BF16)
        zq = jnp.zeros((BLK, LANES), BF16)
        qh = []
        for p in range(P):
            q2 = q_ref[:, p * LANES:(p + 1) * LANES]
            qh += [jnp.where(m0, q2, zq), jnp.where(m0, zq, q2)]

        def block(j, carry, acc, diag):
            off = pl.multiple_of(j * BLK, BLK)
            ks = [k_ref[pl.ds(off, BLK), p * LANES:(p + 1) * LANES] for p in range(P)]
            vh = []
            for p in range(P):
                v2 = v_ref[pl.ds(off, BLK), p * LANES:(p + 1) * LANES]
                vh += [jnp.where(m0, v2, zq), jnp.where(m0, zq, v2)]
            _, b_l, w_l = _sb_scores(qh, ks, carry, diag, tri_excl, strict)
            wb = [w.astype(BF16) for w in w_l]
            new_acc = [acc[p] + _dot(wb[2 * p], vh[2 * p], NN) + _dot(wb[2 * p + 1], vh[2 * p + 1], NN)
                       for p in range(P)]
            new_carry = [carry[n] + jnp.sum(b_l[n], axis=1, keepdims=True) for n in range(2 * P)]
            return new_carry, new_acc

        c0 = jnp.zeros((BLK, 1), F32)
        carry, acc = block(i, [c0] * (2 * P), [jnp.zeros((BLK, LANES), F32)] * P, True)

        def cond(st):
            return jnp.logical_and(st[0] >= 0, st[1] > 0)

        def step(st):
            j, _, carry, acc = st
            carry, acc = block(j, carry, acc, False)
            return j - 1, _any_alive(carry), carry, acc

        st = lax.while_loop(cond, step, (i - 1, _any_alive(carry), carry, acc))
        for p in range(P):
            o_ref[:, p * LANES:(p + 1) * LANES] = st[3][p]

    return pl.pallas_call(
        body, name=name, out_shape=jax.ShapeDtypeStruct((S, D), F32),
        grid=(ngroup, nb),
        in_specs=[pl.BlockSpec((BLK, W), lambda g, i: (i, g)),
                  pl.BlockSpec((S, W), lambda g, i: (0, ngroup + g)),
                  pl.BlockSpec((S, W), lambda g, i: (0, 2 * ngroup + g))],
        out_specs=pl.BlockSpec((BLK, W), lambda g, i: (i, g)),
        compiler_params=_params("arbitrary", "arbitrary"),
    )(qkv, qkv, qkv)


def _sb_bwd(qkv, o, do, name):
    S, D3 = qkv.shape
    D = D3 // 3
    npair, nb = D // LANES, S // BLK
    P = min(SB_BWD_PAIRS, npair)
    ngroup = npair // P
    W = P * LANES

    def body(q_ref, k_ref, v_ref, o_ref, do_ref, dq_ref, dk_ref, dv_ref):
        i = pl.program_id(1)
        m0 = _head_masks()
        row = lax.broadcasted_iota(jnp.int32, (BLK, BLK), 0)
        col = lax.broadcasted_iota(jnp.int32, (BLK, BLK), 1)
        strict = col < row
        tri_excl = jnp.where(row > col, 1.0, 0.0).astype(BF16)
        tri_incl = jnp.where(row >= col, 1.0, 0.0).astype(BF16)
        zq = jnp.zeros((BLK, LANES), BF16)
        qh, doh, delta = [], [], []
        for p in range(P):
            sl = slice(p * LANES, (p + 1) * LANES)
            q2, do2 = q_ref[:, sl], do_ref[:, sl]
            qh += [jnp.where(m0, q2, zq), jnp.where(m0, zq, q2)]
            doh += [jnp.where(m0, do2, zq), jnp.where(m0, zq, do2)]
            prod = do2.astype(F32) * o_ref[:, sl]
            delta += [jnp.sum(jnp.where(m0, prod, 0.0), axis=1, keepdims=True),
                      jnp.sum(jnp.where(m0, 0.0, prod), axis=1, keepdims=True)]

        @pl.when(i == 0)
        def _():
            dk_ref[...] = jnp.zeros_like(dk_ref)
            dv_ref[...] = jnp.zeros_like(dv_ref)

        def block(j, cb, cg, dq, diag):
            off = pl.multiple_of(j * BLK, BLK)
            nh = 2 * P
            ks = [k_ref[pl.ds(off, BLK), p * LANES:(p + 1) * LANES] for p in range(P)]
            vs = [v_ref[pl.ds(off, BLK), p * LANES:(p + 1) * LANES] for p in range(P)]
            kh = []
            for k2 in ks:
                kh += [jnp.where(m0, k2, zq), jnp.where(m0, zq, k2)]
            dws = [_dot(doh[n], vs[n // 2], NT) for n in range(nh)]
            a_l, b_l, w_l = _sb_scores(qh, ks, cb, diag, tri_excl, strict)
            wb = [w.astype(BF16) for w in w_l]
            g_l = [dws[n] * wb[n].astype(F32) for n in range(nh)]
            gsplit = [_split_bf16(g) for g in g_l]
            gincs = [_dot(hi, tri_incl, NN) + _dot(lo, tri_incl, NN) for hi, lo in gsplit]
            dzs = []
            for n in range(nh):
                beta = jnp.exp(a_l[n])
                dz = g_l[n] * (1.0 - beta) - beta * (delta[n] - (gincs[n] + cg[n]))
                if diag:
                    dz = jnp.where(strict, dz, 0.0)
                dzs.append((dz * ATTN_SCALE).astype(BF16))
            ndq = [dq[p] + _dot(dzs[2 * p], kh[2 * p], NN) + _dot(dzs[2 * p + 1], kh[2 * p + 1], NN)
                   for p in range(P)]
            for p in range(P):
                sl = slice(p * LANES, (p + 1) * LANES)
                dk_ref[pl.ds(off, BLK), sl] += (_dot(dzs[2 * p], qh[2 * p], TN)
                                                + _dot(dzs[2 * p + 1], qh[2 * p + 1], TN))
                dv_ref[pl.ds(off, BLK), sl] += (_dot(wb[2 * p], doh[2 * p], TN)
                                                + _dot(wb[2 * p + 1], doh[2 * p + 1], TN))
            ncb = [cb[n] + jnp.sum(b_l[n], axis=1, keepdims=True) for n in range(nh)]
            ncg = [cg[n] + jnp.sum(g_l[n], axis=1, keepdims=True) for n in range(nh)]
            return ncb, ncg, ndq

        c0 = jnp.zeros((BLK, 1), F32)
        cb, cg, dq = block(i, [c0] * (2 * P), [c0] * (2 * P), [jnp.zeros((BLK, LANES), F32)] * P, True)

        def cond(st):
            return jnp.logical_and(st[0] >= 0, st[1] > 0)

        def step(st):
            j, _, cb, cg, dq = st
            cb, cg, dq = block(j, cb, cg, dq, False)
            return j - 1, _any_alive(cb), cb, cg, dq

        st = lax.while_loop(cond, step, (i - 1, _any_alive(cb), cb, cg, dq))
        for p in range(P):
            dq_ref[:, p * LANES:(p + 1) * LANES] = st[4][p].astype(BF16)

    blk = lambda c: pl.BlockSpec((BLK, W), lambda g, i: (i, c * ngroup + g))
    col_all = lambda c: pl.BlockSpec((S, W), lambda g, i: (0, c * ngroup + g))
    return pl.pallas_call(
        body, name=name,
        out_shape=(jax.ShapeDtypeStruct((S, D), BF16), jax.ShapeDtypeStruct((S, D), F32),
                   jax.ShapeDtypeStruct((S, D), F32)),
        grid=(ngroup, nb),
        in_specs=[blk(0), col_all(1), col_all(2), blk(0), blk(0)],
        out_specs=(blk(0), col_all(0), col_all(0)),
        compiler_params=_params("arbitrary", "arbitrary"),
    )(qkv, qkv, qkv, o, do)


SWA_Q_GROUPS = 4


def _roll_heads(x):
    return pltpu.roll(x.astype(F32), HEAD_DIM, 1).astype(BF16)


def _swa_valid(i):
    r = lax.broadcasted_iota(jnp.int32, (BLK, 2 * BLK), 0)
    c = lax.broadcasted_iota(jnp.int32, (BLK, 2 * BLK), 1)
    diff = r + BLK - c
    return (diff >= 0) & (diff < BLK) & ((i > 0) | (c >= BLK))


def _swa_probs(z, valid, sink):
    z = jnp.where(valid, z * ATTN_SCALE, NEG_BIG)
    mx = jnp.maximum(jnp.max(z, axis=1, keepdims=True), sink)
    p = jnp.exp(z - mx)
    ps = jnp.exp(sink - mx)
    inv = 1.0 / (jnp.sum(p, axis=1, keepdims=True) + ps)
    return p * inv, ps * inv


def _swa_operands(q_ref, kc_ref, kp_ref, vc_ref, vp_ref, s_ref, m):
    m0 = _head_masks()
    m0k = jnp.concatenate([m0, m0], axis=0)
    kk = jnp.concatenate([kp_ref[...], kc_ref[...]], axis=0)
    vv = jnp.concatenate([vp_ref[...], vc_ref[...]], axis=0)
    ksw, vsw = _roll_heads(kk), _roll_heads(vv)
    zk = jnp.zeros_like(kk)
    heads = []
    for c in range(SWA_Q_GROUPS):
        qc = q_ref[:, c * LANES:(c + 1) * LANES]
        zq = jnp.zeros_like(qc)
        for u in range(2):
            same = u == c // 2
            sel = (lambda x, z, mk: jnp.where(mk, x, z)) if u == 0 else (lambda x, z, mk: jnp.where(mk, z, x))
            heads.append(dict(
                c=c, same=same, sel=sel,
                qm=sel(qc, zq, m0),
                k=kk if same else ksw, v=vv if same else vsw,
                km=sel(kk if same else ksw, zk, m0k), vm=sel(vv if same else vsw, zk, m0k),
                sink=s_ref[0, m * 2 * SWA_Q_GROUPS + 2 * c + u]))
    return heads, m0


def _swa_fwd(q, kv, sinks, name):
    S, D = q.shape
    nkvp = kv.shape[1] // (2 * LANES)
    nb = S // BLK
    qw = SWA_Q_GROUPS * LANES

    def body(q_ref, kc_ref, kp_ref, vc_ref, vp_ref, s_ref, o_ref):
        m, i = pl.program_id(0), pl.program_id(1)
        valid = _swa_valid(i)
        heads, _ = _swa_operands(q_ref, kc_ref, kp_ref, vc_ref, vp_ref, s_ref, m)
        zs = [_dot(hd["qm"], hd["k"], NT) for hd in heads]
        ps = [_swa_probs(z, valid, hd["sink"])[0].astype(BF16) for z, hd in zip(zs, heads)]
        for c in range(SWA_Q_GROUPS):
            o_ref[:, c * LANES:(c + 1) * LANES] = (_dot(ps[2 * c], heads[2 * c]["vm"], NN)
                                                   + _dot(ps[2 * c + 1], heads[2 * c + 1]["vm"], NN))

    prev = lambda i: jnp.maximum(i - 1, 0)
    return pl.pallas_call(
        body, name=name, out_shape=jax.ShapeDtypeStruct((S, D), F32),
        grid=(nkvp, nb),
        in_specs=[pl.BlockSpec((BLK, qw), lambda m, i: (i, m)),
                  pl.BlockSpec((BLK, LANES), lambda m, i: (i, m)),
                  pl.BlockSpec((BLK, LANES), lambda m, i: (prev(i), m)),
                  pl.BlockSpec((BLK, LANES), lambda m, i: (i, nkvp + m)),
                  pl.BlockSpec((BLK, LANES), lambda m, i: (prev(i), nkvp + m)),
                  pl.BlockSpec(memory_space=pltpu.SMEM)],
        out_specs=pl.BlockSpec((BLK, qw), lambda m, i: (i, m)),
        compiler_params=_params("arbitrary", "arbitrary"),
    )(q, kv, kv, kv, kv, sinks)


def _swa_bwd(q, kv, sinks, o, do, name):
    S, D = q.shape
    nkvp = kv.shape[1] // (2 * LANES)
    nb = S // BLK
    qw = SWA_Q_GROUPS * LANES
    nh = 2 * SWA_Q_GROUPS

    def body(q_ref, kc_ref, kp_ref, vc_ref, vp_ref, s_ref, o_ref, do_ref, dq_ref, dk_ref, dv_ref, ds_ref):
        m, i = pl.program_id(0), pl.program_id(1)
        valid = _swa_valid(i)
        heads, m0 = _swa_operands(q_ref, kc_ref, kp_ref, vc_ref, vp_ref, s_ref, m)

        @pl.when(i == 0)
        def _():
            dk_ref[...] = jnp.zeros_like(dk_ref)
            dv_ref[...] = jnp.zeros_like(dv_ref)
            ds_ref[...] = jnp.zeros_like(ds_ref)

        doms, deltas = [], []
        for hd in heads:
            c = hd["c"]
            doc = do_ref[:, c * LANES:(c + 1) * LANES]
            prod = doc.astype(F32) * o_ref[:, c * LANES:(c + 1) * LANES]
            doms.append(hd["sel"](doc, jnp.zeros_like(doc), m0))
            deltas.append(jnp.sum(hd["sel"](prod, 0.0, m0), axis=1, keepdims=True))
        zs = [_dot(hd["qm"], hd["k"], NT) for hd in heads]
        dps = [_dot(dom, hd["v"], NT) for dom, hd in zip(doms, heads)]
        pbs, dscs = [], []
        for n, hd in enumerate(heads):
            p, psink = _swa_probs(zs[n], valid, hd["sink"])
            pbs.append(p.astype(BF16))
            dscs.append((p * (dps[n] - deltas[n]) * ATTN_SCALE).astype(BF16))
            dsink = jnp.sum(jnp.broadcast_to(-(psink * deltas[n]), (BLK, LANES)), axis=0, keepdims=True)
            ds_ref[0, n:n + 1, :] += dsink
        for c in range(SWA_Q_GROUPS):
            dq_ref[:, c * LANES:(c + 1) * LANES] = (_dot(dscs[2 * c], heads[2 * c]["km"], NN)
                                                    + _dot(dscs[2 * c + 1], heads[2 * c + 1]["km"], NN))
        acc = {}
        for n, hd in enumerate(heads):
            dk_n = _dot(dscs[n], hd["qm"], TN)
            dv_n = _dot(pbs[n], doms[n], TN)
            for key, val in ((("k", hd["same"]), dk_n), (("v", hd["same"]), dv_n)):
                acc[key] = val if key not in acc else acc[key] + val
        dkk = acc["k", True] + pltpu.roll(acc["k", False], HEAD_DIM, 1)
        dvv = acc["v", True] + pltpu.roll(acc["v", False], HEAD_DIM, 1)
        poff = pl.multiple_of(jnp.maximum(i - 1, 0) * BLK, BLK)
        coff = pl.multiple_of(i * BLK, BLK)
        dk_ref[pl.ds(poff, BLK), :] += dkk[:BLK]
        dv_ref[pl.ds(poff, BLK), :] += dvv[:BLK]
        dk_ref[pl.ds(coff, BLK), :] += dkk[BLK:]
        dv_ref[pl.ds(coff, BLK), :] += dvv[BLK:]

    prev = lambda i: jnp.maximum(i - 1, 0)
    qblk = pl.BlockSpec((BLK, qw), lambda m, i: (i, m))
    col_all = pl.BlockSpec((S, LANES), lambda m, i: (0, m))
    return pl.pallas_call(
        body, name=name,
        out_shape=(jax.ShapeDtypeStruct((S, D), F32),
                   jax.ShapeDtypeStruct((S, nkvp * LANES), F32),
                   jax.ShapeDtypeStruct((S, nkvp * LANES), F32),
                   jax.ShapeDtypeStruct((nkvp, nh, LANES), F32)),
        grid=(nkvp, nb),
        in_specs=[qblk,
                  pl.BlockSpec((BLK, LANES), lambda m, i: (i, m)),
                  pl.BlockSpec((BLK, LANES), lambda m, i: (prev(i), m)),
                  pl.BlockSpec((BLK, LANES), lambda m, i: (i, nkvp + m)),
                  pl.BlockSpec((BLK, LANES), lambda m, i: (prev(i), nkvp + m)),
                  pl.BlockSpec(memory_space=pltpu.SMEM),
                  qblk, qblk],
        out_specs=(qblk, col_all, col_all, pl.BlockSpec((1, nh, LANES), lambda m, i: (m, 0, 0))),
        compiler_params=_params("arbitrary", "arbitrary"),
    )(q, kv, kv, kv, kv, sinks, o, do)


def _dev_index(p):
    return 4 * p[0] + 2 * p[1] + p[2]


def _all_gather_many(shards, name):
    n = len(shards)

    def body(*refs):
        x_refs, out_refs = refs[:n], refs[n:2 * n]
        send_sems, recv_sems, local_sems = refs[2 * n:]
        x_, y_, c_ = lax.axis_index("x"), lax.axis_index("y"), lax.axis_index("c")
        me, sibling = (x_, y_, c_), (x_, y_, 1 - c_)
        chips = [(1 - x_, y_), (x_, 1 - y_), (1 - x_, 1 - y_)]

        def copy(t, k, block, to, src=None):
            dst = out_refs[t].at[_dev_index(block)]
            return pltpu.make_async_remote_copy(
                src_ref=dst if src is None else src, dst_ref=dst,
                send_sem=send_sems.at[7 * t + k], recv_sem=recv_sems.at[7 * t + k],
                device_id=to, device_id_type=MESH)

        mine = [pltpu.make_async_copy(x_refs[t], out_refs[t].at[_dev_index(me)], local_sems.at[t]) for t in range(n)]
        for cp in mine:
            cp.start()
        first = []
        for t in range(n):
            first.append(copy(t, 0, me, sibling, src=x_refs[t]))
            first += [copy(t, 1 + j, me, (*chip, c_), src=x_refs[t]) for j, chip in enumerate(chips)]
        for cp in first:
            cp.start()
        passed = []
        for j, chip in enumerate(chips):
            for t in range(n):
                copy(t, 1 + j, (*chip, c_), me).wait_recv()
                fwd = copy(t, 4 + j, (*chip, c_), sibling)
                fwd.start()
                passed.append(fwd)
        for t in range(n):
            copy(t, 0, sibling, me).wait_recv()
        for j, chip in enumerate(chips):
            for t in range(n):
                copy(t, 4 + j, (*chip, 1 - c_), me).wait_recv()
        for cp in first + passed:
            cp.wait_send()
        for cp in mine:
            cp.wait()

    hbm = pl.BlockSpec(memory_space=pl.ANY)
    outs = pl.pallas_call(
        body, name=name,
        out_shape=[jax.ShapeDtypeStruct((N_DEV,) + s.shape, s.dtype) for s in shards],
        in_specs=[hbm] * n, out_specs=[hbm] * n,
        scratch_shapes=[pltpu.SemaphoreType.DMA((7 * n,)), pltpu.SemaphoreType.DMA((7 * n,)),
                        pltpu.SemaphoreType.DMA((n,))],
    )(*shards)
    return [o.reshape(N_DEV * s.shape[0], s.shape[1]) for o, s in zip(outs, shards)]


def _all_to_all_many(fulls, name):
    n = len(fulls)

    def body(*refs):
        b_refs, out_refs = refs[:n], refs[n:2 * n]
        send_sems, recv_sems, local_sems = refs[2 * n:]
        x_, y_, c_ = lax.axis_index("x"), lax.axis_index("y"), lax.axis_index("c")
        my_idx = _dev_index((x_, y_, c_))
        mine = [pltpu.make_async_copy(b_refs[t].at[my_idx], out_refs[t].at[my_idx], local_sems.at[t]) for t in range(n)]
        for cp in mine:
            cp.start()
        copies = []
        for t in range(n):
            for k in range(1, N_DEV):
                peer = (x_ ^ ((k >> 2) & 1), y_ ^ ((k >> 1) & 1), c_ ^ (k & 1))
                copies.append(pltpu.make_async_remote_copy(
                    src_ref=b_refs[t].at[_dev_index(peer)], dst_ref=out_refs[t].at[my_idx],
                    send_sem=send_sems.at[7 * t + k - 1], recv_sem=recv_sems.at[7 * t + k - 1],
                    device_id=peer, device_id_type=MESH))
        for cp in copies:
            cp.start()
        for cp in copies:
            cp.wait_recv()
        for cp in copies:
            cp.wait_send()
        for cp in mine:
            cp.wait()

    hbm = pl.BlockSpec(memory_space=pl.ANY)
    blocks = [f.reshape(N_DEV, f.shape[0] // N_DEV, f.shape[1]) for f in fulls]
    return pl.pallas_call(
        body, name=name,
        out_shape=[jax.ShapeDtypeStruct(b.shape, b.dtype) for b in blocks],
        in_specs=[hbm] * n, out_specs=[hbm] * n,
        scratch_shapes=[pltpu.SemaphoreType.DMA((7 * n,)), pltpu.SemaphoreType.DMA((7 * n,)),
                        pltpu.SemaphoreType.DMA((n,))],
    )(*blocks)


def _sum8(parts, name):
    _, R, C = parts.shape
    tr = _tile(R, 256, 16)

    def body(p_ref, g_ref):
        g = p_ref[0].astype(F32)
        for s in range(1, N_DEV):
            g = g + p_ref[s].astype(F32)
        g_ref[...] = g

    return pl.pallas_call(
        body, name=name, out_shape=jax.ShapeDtypeStruct((R, C), F32),
        grid=(R // tr,),
        in_specs=[pl.BlockSpec((N_DEV, tr, C), lambda i: (0, i, 0))],
        out_specs=pl.BlockSpec((tr, C), lambda i: (i, 0)),
        compiler_params=_params("parallel"),
    )(parts)


def _adamw(g, w, m, v, name):
    R, C = g.shape
    tr = _tile(R, 256, 8)
    c1 = 1.0 - ADAM_B1 ** ADAM_STEP
    c2 = 1.0 - ADAM_B2 ** ADAM_STEP

    def body(g_ref, w_ref, m_ref, v_ref, d_ref, nm_ref, nv_ref):
        gg = g_ref[...]
        nm = ADAM_B1 * m_ref[...] + (1.0 - ADAM_B1) * gg
        nv = ADAM_B2 * v_ref[...] + (1.0 - ADAM_B2) * (gg * gg)
        m_hat = nm / c1
        v_hat = nv / c2
        nm_ref[...] = nm
        nv_ref[...] = nv
        d_ref[...] = -ADAM_LR * (m_hat / (jnp.sqrt(v_hat) + ADAM_EPS) + ADAM_WD * w_ref[...])

    row = pl.BlockSpec((tr, C), lambda i: (i, 0))
    shp = jax.ShapeDtypeStruct((R, C), F32)
    return pl.pallas_call(
        body, name=name, out_shape=(shp, shp, shp),
        grid=(R // tr,), in_specs=[row, row, row, row], out_specs=(row, row, row),
        compiler_params=_params("parallel"),
    )(g, w, m, v)


def _ffn_fwd(h, g, win_t, wo, tag):
    xn, gate, up, act = _ffn_up(h, g, win_t, f"{tag}_up")
    out = _mm(act, wo, NN, F32, f"{tag}_down", scale=FFN_RES_SCALE, res=h, tm=512, tn=1024, tk=2816)
    return out, (xn, gate, up, act)


def _ffn_bwd(dh, h, g, win_t, wo, saved, tag):
    xn, gate, up, act = saved
    dgate, dup = _ffn_dact(dh, wo, gate, up, f"{tag}_dact")
    dwo = _mm(act, dh, TN, BF16, f"{tag}_dwo", scale=FFN_RES_SCALE, tm=1408, tn=1024, tk=512)
    dwin_t = _dwin(dgate, dup, xn, f"{tag}_dwin")
    dh_in, dg = _dx_norm_bwd([(dgate, win_t, NN, 2, 0), (dup, win_t, NN, 2, 1)], h, g, dh, f"{tag}_dx")
    return dh_in, dg, dwin_t, dwo


def _proj(a, w, dims, out_dtype, name, res=None):
    return _mm(a, w, dims, out_dtype, name, res=res, tm=1024, tn=1024, tk=1024)


def _proj_dw(x, dy, name):
    return _mm(x, dy, TN, BF16, name, tm=1024, tn=1024, tk=512)


def kernel(x, ffn1_norm, ffn1_w_in, ffn1_w_out, mix_norm, ffn2_norm, ffn2_w_in, ffn2_w_out, sb_w_qkv, sb_w_o, kv_norm, kv_w, swa_w_q, swa_sinks, swa_w_o, final_norm, loss_target, m_ffn1_norm, m_ffn1_w_in, m_ffn1_w_out, m_mix_norm, m_ffn2_norm, m_ffn2_w_in, m_ffn2_w_out, m_sb_w_qkv, m_sb_w_o, m_kv_norm, m_kv_w, m_swa_w_q, m_swa_sinks, m_swa_w_o, m_final_norm, v_ffn1_norm, v_ffn1_w_in, v_ffn1_w_out, v_mix_norm, v_ffn2_norm, v_ffn2_w_in, v_ffn2_w_out, v_sb_w_qkv, v_sb_w_o, v_kv_norm, v_kv_w, v_swa_w_q, v_swa_sinks, v_swa_w_o, v_final_norm):
    S, D = x.shape[1], x.shape[2]
    L = ffn1_w_in.shape[0]
    KV = kv_w.shape[1]
    assert L == 2 and swa_sinks.shape == (1, 2 * SWA_Q_GROUPS * KV // (2 * LANES))

    def bf(w):
        return w.astype(BF16)

    def bft(w):
        return jnp.transpose(w).astype(BF16)

    first = _all_gather_many([bft(ffn1_w_in[0]), bf(ffn1_w_out[0]), bft(sb_w_qkv[0]), bf(sb_w_o[0])], "gather_weights_a")
    win1a_t, wo1a, wqkv_t, w_sbo = first
    later = _all_gather_many(
        [bft(ffn2_w_in[0]), bf(ffn2_w_out[0]), bf(kv_w), bft(ffn1_w_in[1]), bf(ffn1_w_out[1]),
         bf(swa_w_q[0]), bf(swa_w_o[0]), bft(ffn2_w_in[1]), bf(ffn2_w_out[1])], "gather_weights_b")
    win2a_t, wo2a, w_kv, win1b_t, wo1b, w_q, w_swo, win2b_t, wo2b = later

    cos_t, sin_t = _rope_tables(S)
    h0 = x.reshape(S, D)
    tgt = loss_target.reshape(S, D)

    h1, sv_a1 = _ffn_fwd(h0, ffn1_norm[0], win1a_t, wo1a, "ffn1a")
    hn_a = _rmsnorm(h1, mix_norm[0], "mix_a_norm")
    qkv = _proj(hn_a, wqkv_t, NT, BF16, "sb_qkv")
    o_sb = _sb_fwd(qkv, "sb_attn")
    h2 = _proj(o_sb, w_sbo, NN, F32, "sb_out", res=h1)
    h3, sv_a2 = _ffn_fwd(h2, ffn2_norm[0], win2a_t, wo2a, "ffn2a")
    kvn = _rmsnorm(h3, kv_norm, "kv_norm")
    kv_raw = _proj(kvn, w_kv, NN, F32, "kv_proj")
    kv_rot = _rotary(kv_raw, cos_t, sin_t, KV // (2 * LANES), False, "kv_rope")
    h4, sv_b1 = _ffn_fwd(h3, ffn1_norm[1], win1b_t, wo1b, "ffn1b")
    hn_b = _rmsnorm(h4, mix_norm[1], "mix_b_norm")
    q_raw = _proj(hn_b, w_q, NN, F32, "swa_q")
    q_rot = _rotary(q_raw, cos_t, sin_t, D // LANES, False, "q_rope")
    o_sw = _swa_fwd(q_rot, kv_rot, swa_sinks, "swa_attn")
    h5 = _proj(o_sw, w_swo, NN, F32, "swa_out", res=h4)
    h6, sv_b2 = _ffn_fwd(h5, ffn2_norm[1], win2b_t, wo2b, "ffn2b")
    dh6, dg_final, sq_err = _final_loss(h6, final_norm, tgt, "final_loss")
    loss = lax.psum(0.5 * jnp.sum(sq_err) / D, ("x", "y", "c"))

    dh5, dg_f2b, dwin2b_t, dwo2b = _ffn_bwd(dh6, h5, ffn2_norm[1], win2b_t, wo2b, sv_b2, "ffn2b")
    do_sw = _proj(dh5, w_swo, NT, BF16, "swa_out_dx")
    dw_swo = _proj_dw(o_sw, dh5, "swa_out_dw")
    dq_rot, dk_sw, dv_sw, dsink = _swa_bwd(q_rot, kv_rot, swa_sinks, o_sw, do_sw, "swa_attn_bwd")
    dq = _rotary(dq_rot, cos_t, sin_t, D // LANES, True, "q_rope_bwd")
    dw_q = _proj_dw(hn_b, dq, "swa_q_dw")
    dh4, dg_mix_b = _dx_norm_bwd([(dq, w_q, NT, 1, 0)], h4, mix_norm[1], dh5, "swa_q_dx")
    dh3, dg_f1b, dwin1b_t, dwo1b = _ffn_bwd(dh4, h3, ffn1_norm[1], win1b_t, wo1b, sv_b1, "ffn1b")
    dkv = _rotary(jnp.concatenate([dk_sw, dv_sw], axis=1), cos_t, sin_t, KV // (2 * LANES), True, "kv_rope_bwd")
    dw_kv = _proj_dw(kvn, dkv, "kv_proj_dw")
    dh3, dg_kv = _dx_norm_bwd([(dkv, w_kv, NT, 1, 0)], h3, kv_norm, dh3, "kv_proj_dx")
    dh2, dg_f2a, dwin2a_t, dwo2a = _ffn_bwd(dh3, h2, ffn2_norm[0], win2a_t, wo2a, sv_a2, "ffn2a")
    do_sb = _proj(dh2, w_sbo, NT, BF16, "sb_out_dx")
    dw_sbo = _proj_dw(o_sb, dh2, "sb_out_dw")
    early = _all_to_all_many([dwin2b_t, dwo2b, dw_swo, dw_q, dwin1b_t, dwo1b, dw_kv, dwin2a_t, dwo2a, dw_sbo],
                             "scatter_grads_a")
    dq_sb, dk_sb, dv_sb = _sb_bwd(qkv, o_sb, do_sb, "sb_attn_bwd")
    dqkv = jnp.concatenate([dq_sb, dk_sb.astype(BF16), dv_sb.astype(BF16)], axis=1)
    dwqkv_t = _proj_dw(dqkv, hn_a, "sb_qkv_dw")
    dh1, dg_mix_a = _dx_norm_bwd([(dqkv, wqkv_t, NN, 1, 0)], h1, mix_norm[0], dh2, "sb_qkv_dx")
    dx, dg_f1a, dwin1a_t, dwo1a = _ffn_bwd(dh1, h0, ffn1_norm[0], win1a_t, wo1a, sv_a1, "ffn1a")
    late = _all_to_all_many([dwqkv_t, dwin1a_t, dwo1a], "scatter_grads_b")

    p_win2b, p_wo2b, p_swo, p_q, p_win1b, p_wo1b, p_kv, p_win2a, p_wo2a, p_sbo = early
    p_qkv, p_win1a, p_wo1a = late

    def natural(parts, tag):
        return _sum8(parts, f"sum_{tag}")

    def from_t(parts, tag):
        return jnp.transpose(_sum8(parts, f"sum_{tag}"))

    grads = {
        "ffn1_w_in": jnp.stack([from_t(p_win1a, "win1a"), from_t(p_win1b, "win1b")]),
        "ffn1_w_out": jnp.stack([natural(p_wo1a, "wo1a"), natural(p_wo1b, "wo1b")]),
        "ffn2_w_in": jnp.stack([from_t(p_win2a, "win2a"), from_t(p_win2b, "win2b")]),
        "ffn2_w_out": jnp.stack([natural(p_wo2a, "wo2a"), natural(p_wo2b, "wo2b")]),
        "sb_w_qkv": from_t(p_qkv, "qkv")[None],
        "sb_w_o": natural(p_sbo, "sbo")[None],
        "kv_w": natural(p_kv, "kv"),
        "swa_w_q": natural(p_q, "swq")[None],
        "swa_w_o": natural(p_swo, "swo")[None],
    }

    small_w = [ffn1_norm, mix_norm, ffn2_norm, kv_norm, final_norm, swa_sinks]
    small_m = [m_ffn1_norm, m_mix_norm, m_ffn2_norm, m_kv_norm, m_final_norm, m_swa_sinks]
    small_v = [v_ffn1_norm, v_mix_norm, v_ffn2_norm, v_kv_norm, v_final_norm, v_swa_sinks]
    SMALL_ROWS = 16

    def pack_small(ts):
        rows_ = [t.reshape(-1, D) for t in ts[:-1]]
        sink_row = jnp.pad(ts[-1].reshape(1, -1), ((0, 0), (0, D - ts[-1].size)))
        flat = jnp.concatenate(rows_ + [sink_row], axis=0)
        return jnp.pad(flat, ((0, SMALL_ROWS - flat.shape[0]), (0, 0)))

    def unpack_small(flat):
        out, r = [], 0
        for t in small_w[:-1]:
            n = t.size // D
            out.append(flat[r:r + n].reshape(t.shape))
            r += n
        out.append(flat[r, :swa_sinks.size].reshape(swa_sinks.shape))
        return out

    def gain(parts8):
        return jnp.sum(parts8, axis=0, keepdims=True)

    g_small_local = pack_small([
        jnp.concatenate([gain(dg_f1a), gain(dg_f1b)], axis=0),
        jnp.concatenate([gain(dg_mix_a), gain(dg_mix_b)], axis=0),
        jnp.concatenate([gain(dg_f2a), gain(dg_f2b)], axis=0),
        gain(dg_kv), gain(dg_final), dsink[:, :, 0].reshape(1, -1)])
    small_parts = _all_gather_many([g_small_local], "gather_small_grads")[0]
    g_small = _sum8(small_parts.reshape(N_DEV, SMALL_ROWS, D), "sum_small")
    d_small, nm_small, nv_small = _adamw(g_small, pack_small(small_w), pack_small(small_m), pack_small(small_v), "adamw_small")
    small_names = ["ffn1_norm", "mix_norm", "ffn2_norm", "kv_norm", "final_norm", "swa_sinks"]
    result = {"grad": dict(zip(small_names, unpack_small(g_small))),
              "delta": dict(zip(small_names, unpack_small(d_small))),
              "new_m": dict(zip(small_names, unpack_small(nm_small))),
              "new_v": dict(zip(small_names, unpack_small(nv_small)))}

    big = {"ffn1_w_in": (ffn1_w_in, m_ffn1_w_in, v_ffn1_w_in), "ffn1_w_out": (ffn1_w_out, m_ffn1_w_out, v_ffn1_w_out),
           "ffn2_w_in": (ffn2_w_in, m_ffn2_w_in, v_ffn2_w_in), "ffn2_w_out": (ffn2_w_out, m_ffn2_w_out, v_ffn2_w_out),
           "sb_w_qkv": (sb_w_qkv, m_sb_w_qkv, v_sb_w_qkv), "sb_w_o": (sb_w_o, m_sb_w_o, v_sb_w_o),
           "kv_w": (kv_w, m_kv_w, v_kv_w), "swa_w_q": (swa_w_q, m_swa_w_q, v_swa_w_q),
           "swa_w_o": (swa_w_o, m_swa_w_o, v_swa_w_o)}
    for nm, (w, m, v) in big.items():
        g = grads[nm]
        two_d = lambda t: t.reshape(-1, t.shape[-1])
        d, new_m, new_v = _adamw(two_d(g), two_d(w), two_d(m), two_d(v), f"adamw_{nm}")
        result["grad"][nm] = g
        result["delta"][nm] = d.reshape(w.shape)
        result["new_m"][nm] = new_m.reshape(w.shape)
        result["new_v"][nm] = new_v.reshape(w.shape)

    order = ["ffn1_norm", "ffn1_w_in", "ffn1_w_out", "mix_norm", "ffn2_norm", "ffn2_w_in", "ffn2_w_out",
             "sb_w_qkv", "sb_w_o", "kv_norm", "kv_w", "swa_w_q", "swa_sinks", "swa_w_o", "final_norm"]
    outs = [result[kind][nm] for kind in ("grad", "delta", "new_m", "new_v") for nm in order]
    return (loss, dx.reshape(x.shape), *outs)
```

```python
import jax
import jax.numpy as jnp
from jax import lax
from jax.experimental import pallas as pl
from jax.experimental.pallas import tpu as pltpu

F32 = jnp.float32
BF16 = jnp.bfloat16

N_DEV = 8
HEAD_DIM = 64
LANES = 128
BLK = 128
RMS_EPS = 1e-6
FFN_RES_SCALE = 0.5
ROPE_THETA = 10000.0
ATTN_SCALE = HEAD_DIM ** -0.5
SB_LOG_FLOOR = -110.0
NEG_BIG = -1e30
VMEM_LIMIT_V7X = 56 * 1024 * 1024

ADAM_LR = 0.001
ADAM_B1 = 0.9
ADAM_B2 = 0.999
ADAM_EPS = 1e-08
ADAM_WD = 0.01
ADAM_STEP = 10

NN = ((1,), (0,))
NT = ((1,), (1,))
TN = ((0,), (0,))
MESH = pl.DeviceIdType.MESH


def _dot(a, b, dims):
    return lax.dot_general(a, b, (dims, ((), ())), preferred_element_type=F32)


def _tile(n, pref, mult=LANES):
    if n <= pref:
        return n
    t = (pref // mult) * mult
    while t >= mult:
        if n % t == 0:
            return t
        t -= mult
    return n


def _params(*sem):
    return pltpu.CompilerParams(dimension_semantics=sem, vmem_limit_bytes=VMEM_LIMIT_V7X)


def _mm(a, b, dims, out_dtype, name, scale=1.0, res=None, tm=512, tn=512, tk=512):
    if dims == NN:
        (M, K), (_, N) = a.shape, b.shape
    elif dims == NT:
        (M, K), (N, _) = a.shape, b.shape
    else:
        (K, M), (_, N) = a.shape, b.shape
    tm, tn, tk = _tile(M, tm), _tile(N, tn), _tile(K, tk)
    nk = K // tk
    if dims == TN:
        a_spec = pl.BlockSpec((tk, tm), lambda i, j, k: (k, i))
    else:
        a_spec = pl.BlockSpec((tm, tk), lambda i, j, k: (i, k))
    if dims == NT:
        b_spec = pl.BlockSpec((tn, tk), lambda i, j, k: (j, k))
    else:
        b_spec = pl.BlockSpec((tk, tn), lambda i, j, k: (k, j))
    o_spec = pl.BlockSpec((tm, tn), lambda i, j, k: (i, j))
    has_res = res is not None

    def body(*refs):
        a_ref, b_ref = refs[0], refs[1]
        r_ref = refs[2] if has_res else None
        o_ref = refs[3] if has_res else refs[2]

        def finish(acc):
            r = acc * scale if scale != 1.0 else acc
            if has_res:
                r = r + r_ref[...]
            o_ref[...] = r.astype(out_dtype)

        p = _dot(a_ref[...].astype(BF16), b_ref[...].astype(BF16), dims)
        if nk == 1:
            finish(p)
        else:
            acc_ref = refs[-1]
            k = pl.program_id(2)

            @pl.when(k == 0)
            def _():
                acc_ref[...] = p

            @pl.when(k > 0)
            def _():
                acc_ref[...] += p

            @pl.when(k == nk - 1)
            def _():
                finish(acc_ref[...])

    in_specs = [a_spec, b_spec] + ([o_spec] if has_res else [])
    args = (a, b) + ((res,) if has_res else ())
    return pl.pallas_call(
        body, name=name,
        out_shape=jax.ShapeDtypeStruct((M, N), out_dtype),
        grid=(M // tm, N // tn, nk),
        in_specs=in_specs, out_specs=o_spec,
        scratch_shapes=[pltpu.VMEM((tm, tn), F32)] if nk > 1 else [],
        compiler_params=_params("parallel", "parallel", "arbitrary"),
    )(*args)


def _rows8(x):
    r, d = x.shape
    return jnp.sum(x.reshape(r // 8, 8, d), axis=0)


def _rmsnorm(h, g, name):
    S, D = h.shape
    ts = _tile(S, 512, 8)

    def body(h_ref, g_ref, o_ref):
        x = h_ref[...]
        r = lax.rsqrt(jnp.mean(x * x, axis=-1, keepdims=True) + RMS_EPS)
        o_ref[...] = ((x * r) * g_ref[...]).astype(BF16)

    return pl.pallas_call(
        body, name=name,
        out_shape=jax.ShapeDtypeStruct((S, D), BF16),
        grid=(S // ts,),
        in_specs=[pl.BlockSpec((ts, D), lambda i: (i, 0)), pl.BlockSpec((1, D), lambda i: (0, 0))],
        out_specs=pl.BlockSpec((ts, D), lambda i: (i, 0)),
        compiler_params=_params("parallel"),
    )(h, g.reshape(1, D))


def _final_loss(h, g, tgt, name):
    S, D = h.shape
    ts = _tile(S, 512, 8)

    def body(h_ref, g_ref, t_ref, dh_ref, dg_ref, l_ref):
        x = h_ref[...]
        r = lax.rsqrt(jnp.mean(x * x, axis=-1, keepdims=True) + RMS_EPS)
        xhat = x * r
        err = xhat * g_ref[...] - t_ref[...]
        d = err * (1.0 / D)
        dxh = d * g_ref[...]
        c = jnp.mean(dxh * xhat, axis=-1, keepdims=True)
        dh_ref[...] = r * (dxh - xhat * c)
        part = _rows8(d * xhat)
        lpart = _rows8(err * err)

        @pl.when(pl.program_id(0) == 0)
        def _():
            dg_ref[...] = part
            l_ref[...] = lpart

        @pl.when(pl.program_id(0) > 0)
        def _():
            dg_ref[...] += part
            l_ref[...] += lpart

    row = pl.BlockSpec((ts, D), lambda i: (i, 0))
    acc = pl.BlockSpec((8, D), lambda i: (0, 0))
    return pl.pallas_call(
        body, name=name,
        out_shape=(jax.ShapeDtypeStruct((S, D), F32), jax.ShapeDtypeStruct((8, D), F32),
                   jax.ShapeDtypeStruct((8, D), F32)),
        grid=(S // ts,),
        in_specs=[row, pl.BlockSpec((1, D), lambda i: (0, 0)), row],
        out_specs=(row, acc, acc),
        compiler_params=_params("arbitrary"),
    )(h, g.reshape(1, D), tgt)


def _ffn_up(h, g, win_t, name, carry=None):
    S, D = h.shape
    F = win_t.shape[0] // 2
    tm, tn = _tile(S, 512, 16), _tile(F, 1408)
    nf = F // tn

    def body(h_ref, g_ref, wg_ref, wu_ref, xn_ref, gate_ref, up_ref, act_ref):
        x = h_ref[...]
        r = lax.rsqrt(jnp.mean(x * x, axis=-1, keepdims=True) + RMS_EPS)
        xn = ((x * r) * g_ref[...]).astype(BF16)
        xn_ref[...] = xn
        gate = _dot(xn, wg_ref[...], NT)
        up = _dot(xn, wu_ref[...], NT)
        gate_ref[...] = gate.astype(BF16)
        up_ref[...] = up.astype(BF16)
        sig = 1.0 / (1.0 + jnp.exp(-gate))
        act_ref[...] = (gate * sig * up).astype(BF16)

    row = pl.BlockSpec((tm, D), lambda i, j: (i, 0))
    blk = pl.BlockSpec((tm, tn), lambda i, j: (i, j))
    hid = jax.ShapeDtypeStruct((S, F), BF16)
    return _pcall(
        body, (h, g.reshape(1, D), win_t, win_t), name=name,
        out_shape=(jax.ShapeDtypeStruct((S, D), BF16), hid, hid, hid),
        grid=(S // tm, nf),
        in_specs=[row, pl.BlockSpec((1, D), lambda i, j: (0, 0)),
                  pl.BlockSpec((tn, D), lambda i, j: (j, 0)),
                  pl.BlockSpec((tn, D), lambda i, j: (j + nf, 0))],
        out_specs=(row, blk, blk, blk),
        sem=("arbitrary", "arbitrary"), carry=carry)


def _ffn_dact(dh, wo, gate, up, name):
    S, D = dh.shape
    F = wo.shape[0]
    tm, tn = _tile(S, 512, 16), _tile(F, 1408)

    def body(dh_ref, wo_ref, g_ref, u_ref, dg_ref, du_ref):
        d = _dot(dh_ref[...].astype(BF16), wo_ref[...], NT) * FFN_RES_SCALE
        g = g_ref[...].astype(F32)
        u = u_ref[...].astype(F32)
        sig = 1.0 / (1.0 + jnp.exp(-g))
        du_ref[...] = (d * (g * sig)).astype(BF16)
        dg_ref[...] = (d * u * (sig * (1.0 + g * (1.0 - sig)))).astype(BF16)

    blk = pl.BlockSpec((tm, tn), lambda j, i: (i, j))
    hid = jax.ShapeDtypeStruct((S, F), BF16)
    return pl.pallas_call(
        body, name=name, out_shape=(hid, hid),
        grid=(F // tn, S // tm),
        in_specs=[pl.BlockSpec((tm, D), lambda j, i: (i, 0)), pl.BlockSpec((tn, D), lambda j, i: (j, 0)), blk, blk],
        out_specs=(blk, blk),
        compiler_params=_params("arbitrary", "arbitrary"),
    )(dh, wo, gate, up)


def _dwin(dgate, dup, xn, name, carry=None):
    S, F = dgate.shape
    D = xn.shape[1]
    tr, tk = _tile(F, 1408), _tile(S, 512, 16)
    nf, nk = F // tr, S // tk

    def body(dg_ref, du_ref, x_ref, o_ref, acc_ref):
        r, k = pl.program_id(0), pl.program_id(1)

        def accumulate(a_ref):
            p = _dot(a_ref[...], x_ref[...], TN)

            @pl.when(k == 0)
            def _():
                acc_ref[...] = p

            @pl.when(k > 0)
            def _():
                acc_ref[...] += p

        @pl.when(r < nf)
        def _():
            accumulate(dg_ref)

        @pl.when(r >= nf)
        def _():
            accumulate(du_ref)

        @pl.when(k == nk - 1)
        def _():
            o_ref[...] = acc_ref[...].astype(BF16)

    return _pcall(
        body, (dgate, dup, xn), name=name, out_shape=jax.ShapeDtypeStruct((2 * F, D), BF16),
        grid=(2 * nf, nk),
        in_specs=[pl.BlockSpec((tk, tr), lambda r, k: (jnp.where(r < nf, k, 0), jnp.minimum(r, nf - 1))),
                  pl.BlockSpec((tk, tr), lambda r, k: (jnp.where(r >= nf, k, 0), jnp.maximum(r - nf, 0))),
                  pl.BlockSpec((tk, D), lambda r, k: (k, 0))],
        out_specs=pl.BlockSpec((tr, D), lambda r, k: (r, 0)),
        scratch_shapes=[pltpu.VMEM((tr, D), F32)],
        sem=("arbitrary", "arbitrary"), carry=carry)


def _dx_norm_bwd(terms, h, g, res, name, carry=None):
    S, D = h.shape
    tm = _tile(S, 256, 16)
    n = len(terms)

    def body(*refs):
        dy_refs, w_refs = refs[:n], refs[n:2 * n]
        h_ref, g_ref, r_ref, dh_ref, dg_ref = refs[2 * n:]
        d = _dot(dy_refs[0][...], w_refs[0][...], terms[0][2])
        for t in range(1, n):
            d = d + _dot(dy_refs[t][...], w_refs[t][...], terms[t][2])
        x = h_ref[...]
        r = lax.rsqrt(jnp.mean(x * x, axis=-1, keepdims=True) + RMS_EPS)
        xhat = x * r
        dxh = d * g_ref[...]
        c = jnp.mean(dxh * xhat, axis=-1, keepdims=True)
        dh_ref[...] = r * (dxh - xhat * c) + r_ref[...]
        part = _rows8(d * xhat)

        @pl.when(pl.program_id(0) == 0)
        def _():
            dg_ref[...] = part

        @pl.when(pl.program_id(0) > 0)
        def _():
            dg_ref[...] += part

    def w_spec(w, nblk, blk):
        return pl.BlockSpec((w.shape[0] // nblk, w.shape[1]), lambda i: (blk, 0))

    row = pl.BlockSpec((tm, D), lambda i: (i, 0))
    in_specs = [pl.BlockSpec((tm, t[0].shape[1]), lambda i: (i, 0)) for t in terms]
    in_specs += [w_spec(t[1], t[3], t[4]) for t in terms]
    in_specs += [row, pl.BlockSpec((1, D), lambda i: (0, 0)), row]
    return _pcall(
        body, (*[t[0] for t in terms], *[t[1] for t in terms], h, g.reshape(1, D), res), name=name,
        out_shape=(jax.ShapeDtypeStruct((S, D), F32), jax.ShapeDtypeStruct((8, D), F32)),
        grid=(S // tm,),
        in_specs=in_specs,
        out_specs=(row, pl.BlockSpec((8, D), lambda i: (0, 0))),
        sem=("arbitrary",), carry=carry)


def _rope_tables(S):
    half = HEAD_DIM // 2
    inv_freq = ROPE_THETA ** (-jnp.arange(half, dtype=F32) / half)
    ang = jnp.arange(S).astype(F32)[:, None] * inv_freq[None, :]
    cos, sin = jnp.cos(ang), jnp.sin(ang)
    cos_t = jnp.tile(cos, (1, LANES // half))
    sin_t = jnp.tile(jnp.concatenate([-sin, sin], axis=1), (1, LANES // HEAD_DIM))
    return cos_t, sin_t


def _swap_halves(x):
    lane = lax.broadcasted_iota(jnp.int32, x.shape, 1)
    first = (lane % HEAD_DIM) < (HEAD_DIM // 2)
    return jnp.where(first, pltpu.roll(x, LANES - HEAD_DIM // 2, 1), pltpu.roll(x, HEAD_DIM // 2, 1))


def _rotary(x, cos_t, sin_t, n_rot, inverse, name):
    S, C = x.shape
    ts = _tile(S, 512, 16)
    ng = C // LANES

    def body(x_ref, c_ref, s_ref, o_ref):
        cs, sn = c_ref[...], s_ref[...]
        for gidx in range(ng):
            sl = slice(gidx * LANES, (gidx + 1) * LANES)
            v = x_ref[:, sl].astype(F32)
            if gidx < n_rot:
                if inverse:
                    v = v * cs + _swap_halves(v * sn)
                else:
                    v = v * cs + _swap_halves(v) * sn
            o_ref[:, sl] = v.astype(BF16)

    row = pl.BlockSpec((ts, C), lambda i: (i, 0))
    tab = pl.BlockSpec((ts, LANES), lambda i: (i, 0))
    return pl.pallas_call(
        body, name=name, out_shape=jax.ShapeDtypeStruct((S, C), BF16),
        grid=(S // ts,), in_specs=[row, tab, tab], out_specs=row,
        compiler_params=_params("parallel"),
    )(x, cos_t, sin_t)


def _head_masks():
    lane = lax.broadcasted_iota(jnp.int32, (BLK, LANES), 1)
    return lane < HEAD_DIM


def _split_bf16(x):
    hi = x.astype(BF16)
    lo = (x - hi.astype(F32)).astype(BF16)
    return hi, lo


def _sb_scores(qh, ks, carry, diag, tri_excl, strict):
    n_heads = len(qh)
    zs = [_dot(qh[n], ks[n // 2], NT) for n in range(n_heads)]
    a_l, b_l, split_l = [], [], []
    for z in zs:
        z = z * ATTN_SCALE
        a = jnp.minimum(z, 0.0) - jnp.log(1.0 + jnp.exp(-jnp.abs(z)))
        b = a - z
        if diag:
            b = jnp.where(strict, b, 0.0)
        a_l.append(a)
        b_l.append(b)
        split_l.append(_split_bf16(b))
    sufs = [_dot(hi, tri_excl, NN) + _dot(lo, tri_excl, NN) for hi, lo in split_l]
    w_l = []
    for n in range(n_heads):
        w = jnp.exp(a_l[n] + sufs[n] + carry[n])
        if diag:
            w = jnp.where(strict, w, 0.0)
        w_l.append(w)
    return a_l, b_l, w_l


SB_FWD_PAIRS = 4
SB_BWD_PAIRS = 2


def _any_alive(carries):
    top = carries[0]
    for c in carries[1:]:
        top = jnp.maximum(top, c)
    return (jnp.max(top) > SB_LOG_FLOOR).astype(jnp.int32)


def _sb_fwd(qkv, name, carry=None):
    S, D3 = qkv.shape
    D = D3 // 3
    npair, nb = D // LANES, S // BLK
    P = min(SB_FWD_PAIRS, npair)
    ngroup = npair // P
    W = P * LANES

    def body(q_ref, k_ref, v_ref, o_ref):
        i = pl.program_id(1)
        m0 = _head_masks()
        row = lax.broadcasted_iota(jnp.int32, (BLK, BLK), 0)
        col = lax.broadcasted_iota(jnp.int32, (BLK, BLK), 1)
        strict = col < row
        tri_excl = jnp.where(row > col, 1.0, 0.0).astype(BF16)
        zq = jnp.zeros((BLK, LANES), BF16)
        qh = []
        for p in range(P):
            q2 = q_ref[:, p * LANES:(p + 1) * LANES]
            qh += [jnp.where(m0, q2, zq), jnp.where(m0, zq, q2)]

        def block(j, carry, acc, diag):
            off = pl.multiple_of(j * BLK, BLK)
            ks = [k_ref[pl.ds(off, BLK), p * LANES:(p + 1) * LANES] for p in range(P)]
            vh = []
            for p in range(P):
                v2 = v_ref[pl.ds(off, BLK), p * LANES:(p + 1) * LANES]
                vh += [jnp.where(m0, v2, zq), jnp.where(m0, zq, v2)]
            _, b_l, w_l = _sb_scores(qh, ks, carry, diag, tri_excl, strict)
            wb = [w.astype(BF16) for w in w_l]
            new_acc = [acc[p] + _dot(wb[2 * p], vh[2 * p], NN) + _dot(wb[2 * p + 1], vh[2 * p + 1], NN)
                       for p in range(P)]
            new_carry = [carry[n] + jnp.sum(b_l[n], axis=1, keepdims=True) for n in range(2 * P)]
            return new_carry, new_acc

        c0 = jnp.zeros((BLK, 1), F32)
        carry, acc = block(i, [c0] * (2 * P), [jnp.zeros((BLK, LANES), F32)] * P, True)

        def cond(st):
            return jnp.logical_and(st[0] >= 0, st[1] > 0)

        def step(st):
            j, _, carry, acc = st
            carry, acc = block(j, carry, acc, False)
            return j - 1, _any_alive(carry), carry, acc

        st = lax.while_loop(cond, step, (i - 1, _any_alive(carry), carry, acc))
        for p in range(P):
            o_ref[:, p * LANES:(p + 1) * LANES] = st[3][p]

    return _pcall(
        body, (qkv, qkv, qkv), name=name, out_shape=jax.ShapeDtypeStruct((S, D), F32),
        grid=(ngroup, nb),
        in_specs=[pl.BlockSpec((BLK, W), lambda g, i: (i, g)),
                  pl.BlockSpec((S, W), lambda g, i: (0, ngroup + g)),
                  pl.BlockSpec((S, W), lambda g, i: (0, 2 * ngroup + g))],
        out_specs=pl.BlockSpec((BLK, W), lambda g, i: (i, g)),
        sem=("arbitrary", "arbitrary"), carry=carry)


def _sb_bwd(qkv, o, do, name, carry=None):
    S, D3 = qkv.shape
    D = D3 // 3
    npair, nb = D // LANES, S // BLK
    P = min(SB_BWD_PAIRS, npair)
    ngroup = npair // P
    W = P * LANES

    def body(q_ref, k_ref, v_ref, o_ref, do_ref, dq_ref, dk_ref, dv_ref):
        i = pl.program_id(1)
        m0 = _head_masks()
        row = lax.broadcasted_iota(jnp.int32, (BLK, BLK), 0)
        col = lax.broadcasted_iota(jnp.int32, (BLK, BLK), 1)
        strict = col < row
        tri_excl = jnp.where(row > col, 1.0, 0.0).astype(BF16)
        tri_incl = jnp.where(row >= col, 1.0, 0.0).astype(BF16)
        zq = jnp.zeros((BLK, LANES), BF16)
        qh, doh, delta = [], [], []
        for p in range(P):
            sl = slice(p * LANES, (p + 1) * LANES)
            q2, do2 = q_ref[:, sl], do_ref[:, sl]
            qh += [jnp.where(m0, q2, zq), jnp.where(m0, zq, q2)]
            doh += [jnp.where(m0, do2, zq), jnp.where(m0, zq, do2)]
            prod = do2.astype(F32) * o_ref[:, sl]
            delta += [jnp.sum(jnp.where(m0, prod, 0.0), axis=1, keepdims=True),
                      jnp.sum(jnp.where(m0, 0.0, prod), axis=1, keepdims=True)]

        @pl.when(i == 0)
        def _():
            dk_ref[...] = jnp.zeros_like(dk_ref)
            dv_ref[...] = jnp.zeros_like(dv_ref)

        def block(j, cb, cg, dq, diag):
            off = pl.multiple_of(j * BLK, BLK)
            nh = 2 * P
            ks = [k_ref[pl.ds(off, BLK), p * LANES:(p + 1) * LANES] for p in range(P)]
            vs = [v_ref[pl.ds(off, BLK), p * LANES:(p + 1) * LANES] for p in range(P)]
            kh = []
            for k2 in ks:
                kh += [jnp.where(m0, k2, zq), jnp.where(m0, zq, k2)]
            dws = [_dot(doh[n], vs[n // 2], NT) for n in range(nh)]
            a_l, b_l, w_l = _sb_scores(qh, ks, cb, diag, tri_excl, strict)
            wb = [w.astype(BF16) for w in w_l]
            g_l = [dws[n] * wb[n].astype(F32) for n in range(nh)]
            gsplit = [_split_bf16(g) for g in g_l]
            gincs = [_dot(hi, tri_incl, NN) + _dot(lo, tri_incl, NN) for hi, lo in gsplit]
            dzs = []
            for n in range(nh):
                beta = jnp.exp(a_l[n])
                dz = g_l[n] * (1.0 - beta) - beta * (delta[n] - (gincs[n] + cg[n]))
                if diag:
                    dz = jnp.where(strict, dz, 0.0)
                dzs.append((dz * ATTN_SCALE).astype(BF16))
            ndq = [dq[p] + _dot(dzs[2 * p], kh[2 * p], NN) + _dot(dzs[2 * p + 1], kh[2 * p + 1], NN)
                   for p in range(P)]
            for p in range(P):
                sl = slice(p * LANES, (p + 1) * LANES)
                dk_ref[pl.ds(off, BLK), sl] += (_dot(dzs[2 * p], qh[2 * p], TN)
                                                + _dot(dzs[2 * p + 1], qh[2 * p + 1], TN))
                dv_ref[pl.ds(off, BLK), sl] += (_dot(wb[2 * p], doh[2 * p], TN)
                                                + _dot(wb[2 * p + 1], doh[2 * p + 1], TN))
            ncb = [cb[n] + jnp.sum(b_l[n], axis=1, keepdims=True) for n in range(nh)]
            ncg = [cg[n] + jnp.sum(g_l[n], axis=1, keepdims=True) for n in range(nh)]
            return ncb, ncg, ndq

        c0 = jnp.zeros((BLK, 1), F32)
        cb, cg, dq = block(i, [c0] * (2 * P), [c0] * (2 * P), [jnp.zeros((BLK, LANES), F32)] * P, True)

        def cond(st):
            return jnp.logical_and(st[0] >= 0, st[1] > 0)

        def step(st):
            j, _, cb, cg, dq = st
            cb, cg, dq = block(j, cb, cg, dq, False)
            return j - 1, _any_alive(cb), cb, cg, dq

        st = lax.while_loop(cond, step, (i - 1, _any_alive(cb), cb, cg, dq))
        for p in range(P):
            dq_ref[:, p * LANES:(p + 1) * LANES] = st[4][p].astype(BF16)

    blk = lambda c: pl.BlockSpec((BLK, W), lambda g, i: (i, c * ngroup + g))
    col_all = lambda c: pl.BlockSpec((S, W), lambda g, i: (0, c * ngroup + g))
    return _pcall(
        body, (qkv, qkv, qkv, o, do), name=name,
        out_shape=(jax.ShapeDtypeStruct((S, D), BF16), jax.ShapeDtypeStruct((S, D), F32),
                   jax.ShapeDtypeStruct((S, D), F32)),
        grid=(ngroup, nb),
        in_specs=[blk(0), col_all(1), col_all(2), blk(0), blk(0)],
        out_specs=(blk(0), col_all(0), col_all(0)),
        sem=("arbitrary", "arbitrary"), carry=carry)


SWA_Q_GROUPS = 4


def _roll_heads(x):
    return pltpu.roll(x.astype(F32), HEAD_DIM, 1).astype(BF16)


def _swa_valid(i):
    r = lax.broadcasted_iota(jnp.int32, (BLK, 2 * BLK), 0)
    c = lax.broadcasted_iota(jnp.int32, (BLK, 2 * BLK), 1)
    diff = r + BLK - c
    return (diff >= 0) & (diff < BLK) & ((i > 0) | (c >= BLK))


def _swa_probs(z, valid, sink):
    z = jnp.where(valid, z * ATTN_SCALE, NEG_BIG)
    mx = jnp.maximum(jnp.max(z, axis=1, keepdims=True), sink)
    p = jnp.exp(z - mx)
    ps = jnp.exp(sink - mx)
    inv = 1.0 / (jnp.sum(p, axis=1, keepdims=True) + ps)
    return p * inv, ps * inv


def _swa_operands(q_ref, kc_ref, kp_ref, vc_ref, vp_ref, s_ref, m):
    m0 = _head_masks()
    m0k = jnp.concatenate([m0, m0], axis=0)
    kk = jnp.concatenate([kp_ref[...], kc_ref[...]], axis=0)
    vv = jnp.concatenate([vp_ref[...], vc_ref[...]], axis=0)
    ksw, vsw = _roll_heads(kk), _roll_heads(vv)
    zk = jnp.zeros_like(kk)
    heads = []
    for c in range(SWA_Q_GROUPS):
        qc = q_ref[:, c * LANES:(c + 1) * LANES]
        zq = jnp.zeros_like(qc)
        for u in range(2):
            same = u == c // 2
            sel = (lambda x, z, mk: jnp.where(mk, x, z)) if u == 0 else (lambda x, z, mk: jnp.where(mk, z, x))
            heads.append(dict(
                c=c, same=same, sel=sel,
                qm=sel(qc, zq, m0),
                k=kk if same else ksw, v=vv if same else vsw,
                km=sel(kk if same else ksw, zk, m0k), vm=sel(vv if same else vsw, zk, m0k),
                sink=s_ref[0, m * 2 * SWA_Q_GROUPS + 2 * c + u]))
    return heads, m0


def _swa_fwd(q, kv, sinks, name):
    S, D = q.shape
    nkvp = kv.shape[1] // (2 * LANES)
    nb = S // BLK
    qw = SWA_Q_GROUPS * LANES

    def body(q_ref, kc_ref, kp_ref, vc_ref, vp_ref, s_ref, o_ref):
        m, i = pl.program_id(0), pl.program_id(1)
        valid = _swa_valid(i)
        heads, _ = _swa_operands(q_ref, kc_ref, kp_ref, vc_ref, vp_ref, s_ref, m)
        zs = [_dot(hd["qm"], hd["k"], NT) for hd in heads]
        ps = [_swa_probs(z, valid, hd["sink"])[0].astype(BF16) for z, hd in zip(zs, heads)]
        for c in range(SWA_Q_GROUPS):
            o_ref[:, c * LANES:(c + 1) * LANES] = (_dot(ps[2 * c], heads[2 * c]["vm"], NN)
                                                   + _dot(ps[2 * c + 1], heads[2 * c + 1]["vm"], NN))

    prev = lambda i: jnp.maximum(i - 1, 0)
    return pl.pallas_call(
        body, name=name, out_shape=jax.ShapeDtypeStruct((S, D), F32),
        grid=(nkvp, nb),
        in_specs=[pl.BlockSpec((BLK, qw), lambda m, i: (i, m)),
                  pl.BlockSpec((BLK, LANES), lambda m, i: (i, m)),
                  pl.BlockSpec((BLK, LANES), lambda m, i: (prev(i), m)),
                  pl.BlockSpec((BLK, LANES), lambda m, i: (i, nkvp + m)),
                  pl.BlockSpec((BLK, LANES), lambda m, i: (prev(i), nkvp + m)),
                  pl.BlockSpec(memory_space=pltpu.SMEM)],
        out_specs=pl.BlockSpec((BLK, qw), lambda m, i: (i, m)),
        compiler_params=_params("arbitrary", "arbitrary"),
    )(q, kv, kv, kv, kv, sinks)


def _swa_bwd(q, kv, sinks, o, do, name):
    S, D = q.shape
    nkvp = kv.shape[1] // (2 * LANES)
    nb = S // BLK
    qw = SWA_Q_GROUPS * LANES
    nh = 2 * SWA_Q_GROUPS

    def body(q_ref, kc_ref, kp_ref, vc_ref, vp_ref, s_ref, o_ref, do_ref, dq_ref, dk_ref, dv_ref, ds_ref):
        m, i = pl.program_id(0), pl.program_id(1)
        valid = _swa_valid(i)
        heads, m0 = _swa_operands(q_ref, kc_ref, kp_ref, vc_ref, vp_ref, s_ref, m)

        @pl.when(i == 0)
        def _():
            dk_ref[...] = jnp.zeros_like(dk_ref)
            dv_ref[...] = jnp.zeros_like(dv_ref)
            ds_ref[...] = jnp.zeros_like(ds_ref)

        doms, deltas = [], []
        for hd in heads:
            c = hd["c"]
            doc = do_ref[:, c * LANES:(c + 1) * LANES]
            prod = doc.astype(F32) * o_ref[:, c * LANES:(c + 1) * LANES]
            doms.append(hd["sel"](doc, jnp.zeros_like(doc), m0))
            deltas.append(jnp.sum(hd["sel"](prod, 0.0, m0), axis=1, keepdims=True))
        zs = [_dot(hd["qm"], hd["k"], NT) for hd in heads]
        dps = [_dot(dom, hd["v"], NT) for dom, hd in zip(doms, heads)]
        pbs, dscs = [], []
        for n, hd in enumerate(heads):
            p, psink = _swa_probs(zs[n], valid, hd["sink"])
            pbs.append(p.astype(BF16))
            dscs.append((p * (dps[n] - deltas[n]) * ATTN_SCALE).astype(BF16))
            dsink = jnp.sum(jnp.broadcast_to(-(psink * deltas[n]), (BLK, LANES)), axis=0, keepdims=True)
            ds_ref[0, n:n + 1, :] += dsink
        for c in range(SWA_Q_GROUPS):
            dq_ref[:, c * LANES:(c + 1) * LANES] = (_dot(dscs[2 * c], heads[2 * c]["km"], NN)
                                                    + _dot(dscs[2 * c + 1], heads[2 * c + 1]["km"], NN))
        acc = {}
        for n, hd in enumerate(heads):
            dk_n = _dot(dscs[n], hd["qm"], TN)
            dv_n = _dot(pbs[n], doms[n], TN)
            for key, val in ((("k", hd["same"]), dk_n), (("v", hd["same"]), dv_n)):
                acc[key] = val if key not in acc else acc[key] + val
        dkk = acc["k", True] + pltpu.roll(acc["k", False], HEAD_DIM, 1)
        dvv = acc["v", True] + pltpu.roll(acc["v", False], HEAD_DIM, 1)
        poff = pl.multiple_of(jnp.maximum(i - 1, 0) * BLK, BLK)
        coff = pl.multiple_of(i * BLK, BLK)
        dk_ref[pl.ds(poff, BLK), :] += dkk[:BLK]
        dv_ref[pl.ds(poff, BLK), :] += dvv[:BLK]
        dk_ref[pl.ds(coff, BLK), :] += dkk[BLK:]
        dv_ref[pl.ds(coff, BLK), :] += dvv[BLK:]

    prev = lambda i: jnp.maximum(i - 1, 0)
    qblk = pl.BlockSpec((BLK, qw), lambda m, i: (i, m))
    col_all = pl.BlockSpec((S, LANES), lambda m, i: (0, m))
    return pl.pallas_call(
        body, name=name,
        out_shape=(jax.ShapeDtypeStruct((S, D), F32),
                   jax.ShapeDtypeStruct((S, nkvp * LANES), F32),
                   jax.ShapeDtypeStruct((S, nkvp * LANES), F32),
                   jax.ShapeDtypeStruct((nkvp, nh, LANES), F32)),
        grid=(nkvp, nb),
        in_specs=[qblk,
                  pl.BlockSpec((BLK, LANES), lambda m, i: (i, m)),
                  pl.BlockSpec((BLK, LANES), lambda m, i: (prev(i), m)),
                  pl.BlockSpec((BLK, LANES), lambda m, i: (i, nkvp + m)),
                  pl.BlockSpec((BLK, LANES), lambda m, i: (prev(i), nkvp + m)),
                  pl.BlockSpec(memory_space=pltpu.SMEM),
                  qblk, qblk],
        out_specs=(qblk, col_all, col_all, pl.BlockSpec((1, nh, LANES), lambda m, i: (m, 0, 0))),
        compiler_params=_params("arbitrary", "arbitrary"),
    )(q, kv, kv, kv, kv, sinks, o, do)


def _dev_index(p):
    return 4 * p[0] + 2 * p[1] + p[2]


def _gather_plan(x_refs, out_refs, send_sems, recv_sems, local_sems):
    n = len(x_refs)
    x_, y_, c_ = lax.axis_index("x"), lax.axis_index("y"), lax.axis_index("c")
    me, sibling = (x_, y_, c_), (x_, y_, 1 - c_)
    chips = [(1 - x_, y_), (x_, 1 - y_), (1 - x_, 1 - y_)]

    def copy(t, k, block, to, src=None):
        dst = out_refs[t].at[_dev_index(block)]
        return pltpu.make_async_remote_copy(
            src_ref=dst if src is None else src, dst_ref=dst,
            send_sem=send_sems.at[7 * t + k], recv_sem=recv_sems.at[7 * t + k],
            device_id=to, device_id_type=MESH)

    mine = [pltpu.make_async_copy(x_refs[t], out_refs[t].at[_dev_index(me)], local_sems.at[t]) for t in range(n)]
    first = []
    for t in range(n):
        first.append(copy(t, 0, me, sibling, src=x_refs[t]))
        first += [copy(t, 1 + j, me, (*chip, c_), src=x_refs[t]) for j, chip in enumerate(chips)]
    arrived = lambda t, j: copy(t, 1 + j, (*chips[j], c_), me)
    forward = lambda t, j: copy(t, 4 + j, (*chips[j], c_), sibling)
    from_sibling = lambda t: copy(t, 0, sibling, me)
    forwarded = lambda t, j: copy(t, 4 + j, (*chips[j], 1 - c_), me)
    return n, mine, first, arrived, forward, from_sibling, forwarded


def _gather_start(x_refs, out_refs, send_sems, recv_sems, local_sems):
    _, mine, first, *_ = _gather_plan(x_refs, out_refs, send_sems, recv_sems, local_sems)
    for cp in mine + first:
        cp.start()


def _gather_finish(x_refs, out_refs, send_sems, recv_sems, local_sems):
    n, mine, first, arrived, forward, from_sibling, forwarded = _gather_plan(
        x_refs, out_refs, send_sems, recv_sems, local_sems)
    passed = []
    for j in range(3):
        for t in range(n):
            arrived(t, j).wait_recv()
            fwd = forward(t, j)
            fwd.start()
            passed.append(fwd)
    for t in range(n):
        from_sibling(t).wait_recv()
    for j in range(3):
        for t in range(n):
            forwarded(t, j).wait_recv()
    for cp in first + passed:
        cp.wait_send()
    for cp in mine:
        cp.wait()


def _scatter_plan(b_refs, out_refs, send_sems, recv_sems, local_sems):
    n = len(b_refs)
    x_, y_, c_ = lax.axis_index("x"), lax.axis_index("y"), lax.axis_index("c")
    my_idx = _dev_index((x_, y_, c_))
    mine = [pltpu.make_async_copy(b_refs[t].at[my_idx], out_refs[t].at[my_idx], local_sems.at[t]) for t in range(n)]
    copies = []
    for t in range(n):
        for k in range(1, N_DEV):
            peer = (x_ ^ ((k >> 2) & 1), y_ ^ ((k >> 1) & 1), c_ ^ (k & 1))
            copies.append(pltpu.make_async_remote_copy(
                src_ref=b_refs[t].at[_dev_index(peer)], dst_ref=out_refs[t].at[my_idx],
                send_sem=send_sems.at[7 * t + k - 1], recv_sem=recv_sems.at[7 * t + k - 1],
                device_id=peer, device_id_type=MESH))
    return mine, copies


def _scatter_start(b_refs, out_refs, send_sems, recv_sems, local_sems):
    mine, copies = _scatter_plan(b_refs, out_refs, send_sems, recv_sems, local_sems)
    for cp in mine + copies:
        cp.start()


def _scatter_finish(b_refs, out_refs, send_sems, recv_sems, local_sems):
    mine, copies = _scatter_plan(b_refs, out_refs, send_sems, recv_sems, local_sems)
    for cp in copies:
        cp.wait_recv()
    for cp in copies:
        cp.wait_send()
    for cp in mine:
        cp.wait()


def _exchange_operands(kind, tensors):
    if kind == "gather":
        args = list(tensors)
        shapes = [jax.ShapeDtypeStruct((N_DEV,) + t.shape, t.dtype) for t in tensors]
        return args, shapes, _gather_start, _gather_finish
    args = [t.reshape(N_DEV, t.shape[0] // N_DEV, t.shape[1]) for t in tensors]
    shapes = [jax.ShapeDtypeStruct(a.shape, a.dtype) for a in args]
    return args, shapes, _scatter_start, _scatter_finish


def _exchange_results(kind, tensors, res):
    if kind == "gather":
        return [r.reshape(N_DEV * t.shape[0], t.shape[1]) for r, t in zip(res, tensors)]
    return list(res)


def _exchange_sems(n):
    return [pltpu.SemaphoreType.DMA((7 * n,)), pltpu.SemaphoreType.DMA((7 * n,)), pltpu.SemaphoreType.DMA((n,))]


def _exchange(kind, tensors, name):
    n = len(tensors)
    args, shapes, start, finish = _exchange_operands(kind, tensors)

    def body(*refs):
        start(refs[:n], refs[n:2 * n], *refs[2 * n:])
        finish(refs[:n], refs[n:2 * n], *refs[2 * n:])

    hbm = pl.BlockSpec(memory_space=pl.ANY)
    res = pl.pallas_call(body, name=name, out_shape=shapes, in_specs=[hbm] * n, out_specs=[hbm] * n,
                         scratch_shapes=_exchange_sems(n))(*args)
    return _exchange_results(kind, tensors, res)


def _pcall(body, args, *, name, out_shape, grid, in_specs, out_specs, sem, scratch_shapes=(), carry=None):
    if carry is None:
        out = pl.pallas_call(body, name=name, out_shape=out_shape, grid=grid, in_specs=list(in_specs),
                             out_specs=out_specs, scratch_shapes=list(scratch_shapes),
                             compiler_params=_params(*sem))(*args)
        return out, None
    kind, tensors = carry
    multi = isinstance(out_shape, (tuple, list))
    shapes = list(out_shape) if multi else [out_shape]
    ospecs = list(out_specs) if multi else [out_specs]
    n_in, n_out, n_scr, n_c = len(in_specs), len(shapes), len(scratch_shapes), len(tensors)
    c_args, c_shapes, start, finish = _exchange_operands(kind, tensors)

    def wrapped(*refs):
        ins, rest = refs[:n_in], refs[n_in:]
        c_in, rest = rest[:n_c], rest[n_c:]
        outs, rest = rest[:n_out], rest[n_out:]
        c_out, rest = rest[:n_c], rest[n_c:]
        scr, sems = rest[:n_scr], rest[n_scr:]
        ids = [pl.program_id(a) for a in range(len(grid))]
        first, last = ids[0] == 0, ids[0] == grid[0] - 1
        for a in range(1, len(grid)):
            first = jnp.logical_and(first, ids[a] == 0)
            last = jnp.logical_and(last, ids[a] == grid[a] - 1)

        @pl.when(first)
        def _():
            start(c_in, c_out, *sems)

        body(*ins, *outs, *scr)

        @pl.when(last)
        def _():
            finish(c_in, c_out, *sems)

    hbm = pl.BlockSpec(memory_space=pl.ANY)
    res = pl.pallas_call(
        wrapped, name=name, out_shape=shapes + c_shapes, grid=grid,
        in_specs=list(in_specs) + [hbm] * n_c, out_specs=ospecs + [hbm] * n_c,
        scratch_shapes=list(scratch_shapes) + _exchange_sems(n_c),
        compiler_params=_params(*sem))(*args, *c_args)
    outs = tuple(res[:n_out]) if multi else res[0]
    return outs, _exchange_results(kind, tensors, res[n_out:])


def _sum8(parts, name):
    _, R, C = parts.shape
    tr = _tile(R, 256, 16)

    def body(p_ref, g_ref):
        g = p_ref[0].astype(F32)
        for s in range(1, N_DEV):
            g = g + p_ref[s].astype(F32)
        g_ref[...] = g

    return pl.pallas_call(
        body, name=name, out_shape=jax.ShapeDtypeStruct((R, C), F32),
        grid=(R // tr,),
        in_specs=[pl.BlockSpec((N_DEV, tr, C), lambda i: (0, i, 0))],
        out_specs=pl.BlockSpec((tr, C), lambda i: (i, 0)),
        compiler_params=_params("parallel"),
    )(parts)


def _adamw(g, w, m, v, name):
    R, C = g.shape
    tr = _tile(R, 256, 8)
    c1 = 1.0 - ADAM_B1 ** ADAM_STEP
    c2 = 1.0 - ADAM_B2 ** ADAM_STEP

    def body(g_ref, w_ref, m_ref, v_ref, d_ref, nm_ref, nv_ref):
        gg = g_ref[...]
        nm = ADAM_B1 * m_ref[...] + (1.0 - ADAM_B1) * gg
        nv = ADAM_B2 * v_ref[...] + (1.0 - ADAM_B2) * (gg * gg)
        m_hat = nm / c1
        v_hat = nv / c2
        nm_ref[...] = nm
        nv_ref[...] = nv
        d_ref[...] = -ADAM_LR * (m_hat / (jnp.sqrt(v_hat) + ADAM_EPS) + ADAM_WD * w_ref[...])

    row = pl.BlockSpec((tr, C), lambda i: (i, 0))
    shp = jax.ShapeDtypeStruct((R, C), F32)
    return pl.pallas_call(
        body, name=name, out_shape=(shp, shp, shp),
        grid=(R // tr,), in_specs=[row, row, row, row], out_specs=(row, row, row),
        compiler_params=_params("parallel"),
    )(g, w, m, v)


def _ffn_down(act, wo, h, tag):
    return _mm(act, wo, NN, F32, f"{tag}_down", scale=FFN_RES_SCALE, res=h, tm=512, tn=1024, tk=2816)


def _ffn_fwd(h, g, win_t, wo, tag):
    saved, _ = _ffn_up(h, g, win_t, f"{tag}_up")
    return _ffn_down(saved[3], wo, h, tag), saved


def _ffn_bwd(dh, h, g, win_t, wo, saved, tag, scatter=False):
    xn, gate, up, act = saved
    dgate, dup = _ffn_dact(dh, wo, gate, up, f"{tag}_dact")
    dwo = _mm(act, dh, TN, BF16, f"{tag}_dwo", scale=FFN_RES_SCALE, tm=1408, tn=1024, tk=512)
    dwin_t, got_wo = _dwin(dgate, dup, xn, f"{tag}_dwin", carry=("scatter", [dwo]) if scatter else None)
    (dh_in, dg), got_win = _dx_norm_bwd([(dgate, win_t, NN, 2, 0), (dup, win_t, NN, 2, 1)], h, g, dh, f"{tag}_dx",
                                        carry=("scatter", [dwin_t]) if scatter else None)
    if scatter:
        return dh_in, dg, got_win[0], got_wo[0]
    return dh_in, dg, dwin_t, dwo


def _proj(a, w, dims, out_dtype, name, res=None):
    return _mm(a, w, dims, out_dtype, name, res=res, tm=1024, tn=1024, tk=1024)


def _proj_dw(x, dy, name):
    return _mm(x, dy, TN, BF16, name, tm=1024, tn=1024, tk=512)


def kernel(x, ffn1_norm, ffn1_w_in, ffn1_w_out, mix_norm, ffn2_norm, ffn2_w_in, ffn2_w_out, sb_w_qkv, sb_w_o, kv_norm, kv_w, swa_w_q, swa_sinks, swa_w_o, final_norm, loss_target, m_ffn1_norm, m_ffn1_w_in, m_ffn1_w_out, m_mix_norm, m_ffn2_norm, m_ffn2_w_in, m_ffn2_w_out, m_sb_w_qkv, m_sb_w_o, m_kv_norm, m_kv_w, m_swa_w_q, m_swa_sinks, m_swa_w_o, m_final_norm, v_ffn1_norm, v_ffn1_w_in, v_ffn1_w_out, v_mix_norm, v_ffn2_norm, v_ffn2_w_in, v_ffn2_w_out, v_sb_w_qkv, v_sb_w_o, v_kv_norm, v_kv_w, v_swa_w_q, v_swa_sinks, v_swa_w_o, v_final_norm):
    S, D = x.shape[1], x.shape[2]
    L = ffn1_w_in.shape[0]
    KV = kv_w.shape[1]
    assert L == 2 and swa_sinks.shape == (1, 2 * SWA_Q_GROUPS * KV // (2 * LANES))

    def bf(w):
        return w.astype(BF16)

    def bft(w):
        return jnp.transpose(w).astype(BF16)

    cos_t, sin_t = _rope_tables(S)
    h0 = x.reshape(S, D)
    tgt = loss_target.reshape(S, D)

    win1a_t, = _exchange("gather", [bft(ffn1_w_in[0])], "gather_first_weight")
    sv_a1, (wo1a, wqkv_t, w_sbo) = _ffn_up(
        h0, ffn1_norm[0], win1a_t, "ffn1a_up",
        carry=("gather", [bf(ffn1_w_out[0]), bft(sb_w_qkv[0]), bf(sb_w_o[0])]))
    h1 = _ffn_down(sv_a1[3], wo1a, h0, "ffn1a")
    hn_a = _rmsnorm(h1, mix_norm[0], "mix_a_norm")
    qkv = _proj(hn_a, wqkv_t, NT, BF16, "sb_qkv")
    o_sb, later = _sb_fwd(qkv, "sb_attn", carry=("gather", [
        bft(ffn2_w_in[0]), bf(ffn2_w_out[0]), bf(kv_w), bft(ffn1_w_in[1]), bf(ffn1_w_out[1]),
        bf(swa_w_q[0]), bf(swa_w_o[0]), bft(ffn2_w_in[1]), bf(ffn2_w_out[1])]))
    win2a_t, wo2a, w_kv, win1b_t, wo1b, w_q, w_swo, win2b_t, wo2b = later
    h2 = _proj(o_sb, w_sbo, NN, F32, "sb_out", res=h1)
    h3, sv_a2 = _ffn_fwd(h2, ffn2_norm[0], win2a_t, wo2a, "ffn2a")
    kvn = _rmsnorm(h3, kv_norm, "kv_norm")
    kv_raw = _proj(kvn, w_kv, NN, F32, "kv_proj")
    kv_rot = _rotary(kv_raw, cos_t, sin_t, KV // (2 * LANES), False, "kv_rope")
    h4, sv_b1 = _ffn_fwd(h3, ffn1_norm[1], win1b_t, wo1b, "ffn1b")
    hn_b = _rmsnorm(h4, mix_norm[1], "mix_b_norm")
    q_raw = _proj(hn_b, w_q, NN, F32, "swa_q")
    q_rot = _rotary(q_raw, cos_t, sin_t, D // LANES, False, "q_rope")
    o_sw = _swa_fwd(q_rot, kv_rot, swa_sinks, "swa_attn")
    h5 = _proj(o_sw, w_swo, NN, F32, "swa_out", res=h4)
    h6, sv_b2 = _ffn_fwd(h5, ffn2_norm[1], win2b_t, wo2b, "ffn2b")
    dh6, dg_final, sq_err = _final_loss(h6, final_norm, tgt, "final_loss")
    loss = lax.psum(0.5 * jnp.sum(sq_err) / D, ("x", "y", "c"))

    dh5, dg_f2b, dwin2b_t, dwo2b = _ffn_bwd(dh6, h5, ffn2_norm[1], win2b_t, wo2b, sv_b2, "ffn2b")
    do_sw = _proj(dh5, w_swo, NT, BF16, "swa_out_dx")
    dw_swo = _proj_dw(o_sw, dh5, "swa_out_dw")
    dq_rot, dk_sw, dv_sw, dsink = _swa_bwd(q_rot, kv_rot, swa_sinks, o_sw, do_sw, "swa_attn_bwd")
    dq = _rotary(dq_rot, cos_t, sin_t, D // LANES, True, "q_rope_bwd")
    dw_q = _proj_dw(hn_b, dq, "swa_q_dw")
    (dh4, dg_mix_b), _ = _dx_norm_bwd([(dq, w_q, NT, 1, 0)], h4, mix_norm[1], dh5, "swa_q_dx")
    dh3, dg_f1b, dwin1b_t, dwo1b = _ffn_bwd(dh4, h3, ffn1_norm[1], win1b_t, wo1b, sv_b1, "ffn1b")
    dkv = _rotary(jnp.concatenate([dk_sw, dv_sw], axis=1), cos_t, sin_t, KV // (2 * LANES), True, "kv_rope_bwd")
    dw_kv = _proj_dw(kvn, dkv, "kv_proj_dw")
    (dh3, dg_kv), _ = _dx_norm_bwd([(dkv, w_kv, NT, 1, 0)], h3, kv_norm, dh3, "kv_proj_dx")
    dh2, dg_f2a, dwin2a_t, dwo2a = _ffn_bwd(dh3, h2, ffn2_norm[0], win2a_t, wo2a, sv_a2, "ffn2a")
    do_sb = _proj(dh2, w_sbo, NT, BF16, "sb_out_dx")
    dw_sbo = _proj_dw(o_sb, dh2, "sb_out_dw")
    (dq_sb, dk_sb, dv_sb), early = _sb_bwd(qkv, o_sb, do_sb, "sb_attn_bwd", carry=("scatter", [
        dwin2b_t, dwo2b, dw_swo, dw_q, dwin1b_t, dwo1b, dw_kv, dwin2a_t, dwo2a, dw_sbo]))
    dqkv = jnp.concatenate([dq_sb, dk_sb.astype(BF16), dv_sb.astype(BF16)], axis=1)
    dwqkv_t = _proj_dw(dqkv, hn_a, "sb_qkv_dw")
    (dh1, dg_mix_a), (p_qkv,) = _dx_norm_bwd([(dqkv, wqkv_t, NN, 1, 0)], h1, mix_norm[0], dh2, "sb_qkv_dx",
                                             carry=("scatter", [dwqkv_t]))
    dx, dg_f1a, p_win1a, p_wo1a = _ffn_bwd(dh1, h0, ffn1_norm[0], win1a_t, wo1a, sv_a1, "ffn1a", scatter=True)

    p_win2b, p_wo2b, p_swo, p_q, p_win1b, p_wo1b, p_kv, p_win2a, p_wo2a, p_sbo = early

    def natural(parts, tag):
        return _sum8(parts, f"sum_{tag}")

    def from_t(parts, tag):
        return jnp.transpose(_sum8(parts, f"sum_{tag}"))

    grads = {
        "ffn1_w_in": jnp.stack([from_t(p_win1a, "win1a"), from_t(p_win1b, "win1b")]),
        "ffn1_w_out": jnp.stack([natural(p_wo1a, "wo1a"), natural(p_wo1b, "wo1b")]),
        "ffn2_w_in": jnp.stack([from_t(p_win2a, "win2a"), from_t(p_win2b, "win2b")]),
        "ffn2_w_out": jnp.stack([natural(p_wo2a, "wo2a"), natural(p_wo2b, "wo2b")]),
        "sb_w_qkv": from_t(p_qkv, "qkv")[None],
        "sb_w_o": natural(p_sbo, "sbo")[None],
        "kv_w": natural(p_kv, "kv"),
        "swa_w_q": natural(p_q, "swq")[None],
        "swa_w_o": natural(p_swo, "swo")[None],
    }

    small_w = [ffn1_norm, mix_norm, ffn2_norm, kv_norm, final_norm, swa_sinks]
    small_m = [m_ffn1_norm, m_mix_norm, m_ffn2_norm, m_kv_norm, m_final_norm, m_swa_sinks]
    small_v = [v_ffn1_norm, v_mix_norm, v_ffn2_norm, v_kv_norm, v_final_norm, v_swa_sinks]
    SMALL_ROWS = 16

    def pack_small(ts):
        rows_ = [t.reshape(-1, D) for t in ts[:-1]]
        sink_row = jnp.pad(ts[-1].reshape(1, -1), ((0, 0), (0, D - ts[-1].size)))
        flat = jnp.concatenate(rows_ + [sink_row], axis=0)
        return jnp.pad(flat, ((0, SMALL_ROWS - flat.shape[0]), (0, 0)))

    def unpack_small(flat):
        out, r = [], 0
        for t in small_w[:-1]:
            n = t.size // D
            out.append(flat[r:r + n].reshape(t.shape))
            r += n
        out.append(flat[r, :swa_sinks.size].reshape(swa_sinks.shape))
        return out

    def gain(parts8):
        return jnp.sum(parts8, axis=0, keepdims=True)

    g_small_local = pack_small([
        jnp.concatenate([gain(dg_f1a), gain(dg_f1b)], axis=0),
        jnp.concatenate([gain(dg_mix_a), gain(dg_mix_b)], axis=0),
        jnp.concatenate([gain(dg_f2a), gain(dg_f2b)], axis=0),
        gain(dg_kv), gain(dg_final), dsink[:, :, 0].reshape(1, -1)])
    small_parts = _exchange("gather", [g_small_local], "gather_small_grads")[0]
    g_small = _sum8(small_parts.reshape(N_DEV, SMALL_ROWS, D), "sum_small")
    d_small, nm_small, nv_small = _adamw(g_small, pack_small(small_w), pack_small(small_m), pack_small(small_v), "adamw_small")
    small_names = ["ffn1_norm", "mix_norm", "ffn2_norm", "kv_norm", "final_norm", "swa_sinks"]
    result = {"grad": dict(zip(small_names, unpack_small(g_small))),
              "delta": dict(zip(small_names, unpack_small(d_small))),
              "new_m": dict(zip(small_names, unpack_small(nm_small))),
              "new_v": dict(zip(small_names, unpack_small(nv_small)))}

    big = {"ffn1_w_in": (ffn1_w_in, m_ffn1_w_in, v_ffn1_w_in), "ffn1_w_out": (ffn1_w_out, m_ffn1_w_out, v_ffn1_w_out),
           "ffn2_w_in": (ffn2_w_in, m_ffn2_w_in, v_ffn2_w_in), "ffn2_w_out": (ffn2_w_out, m_ffn2_w_out, v_ffn2_w_out),
           "sb_w_qkv": (sb_w_qkv, m_sb_w_qkv, v_sb_w_qkv), "sb_w_o": (sb_w_o, m_sb_w_o, v_sb_w_o),
           "kv_w": (kv_w, m_kv_w, v_kv_w), "swa_w_q": (swa_w_q, m_swa_w_q, v_swa_w_q),
           "swa_w_o": (swa_w_o, m_swa_w_o, v_swa_w_o)}
    for nm, (w, m, v) in big.items():
        g = grads[nm]
        two_d = lambda t: t.reshape(-1, t.shape[-1])
        d, new_m, new_v = _adamw(two_d(g), two_d(w), two_d(m), two_d(v), f"adamw_{nm}")
        result["grad"][nm] = g
        result["delta"][nm] = d.reshape(w.shape)
        result["new_m"][nm] = new_m.reshape(w.shape)
        result["new_v"][nm] = new_v.reshape(w.shape)

    order = ["ffn1_norm", "ffn1_w_in", "ffn1_w_out", "mix_norm", "ffn2_norm", "ffn2_w_in", "ffn2_w_out",
             "sb_w_qkv", "sb_w_o", "kv_norm", "kv_w", "swa_w_q", "swa_sinks", "swa_w_o", "final_norm"]
    outs = [result[kind][nm] for kind in ("grad", "delta", "new_m", "new_v") for nm in order]
    return (loss, dx.reshape(x.shape), *outs)
```

```python
import jax
import jax.numpy as jnp
from jax import lax
from jax.experimental import pallas as pl
from jax.experimental.pallas import tpu as pltpu

F32 = jnp.float32
BF16 = jnp.bfloat16

N_DEV = 8
HEAD_DIM = 64
LANES = 128
BLK = 128
RMS_EPS = 1e-6
FFN_RES_SCALE = 0.5
ROPE_THETA = 10000.0
ATTN_SCALE = HEAD_DIM ** -0.5
SB_LOG_FLOOR = -110.0
NEG_BIG = -1e30
VMEM_LIMIT_V7X = 56 * 1024 * 1024

ADAM_LR = 0.001
ADAM_B1 = 0.9
ADAM_B2 = 0.999
ADAM_EPS = 1e-08
ADAM_WD = 0.01
ADAM_STEP = 10

NN = ((1,), (0,))
NT = ((1,), (1,))
TN = ((0,), (0,))
TN_CHUNK = 2048
MESH = pl.DeviceIdType.MESH


def _dot(a, b, dims):
    return lax.dot_general(a, b, (dims, ((), ())), preferred_element_type=F32)


def _tile(n, pref, mult=LANES):
    if n <= pref:
        return n
    t = (pref // mult) * mult
    while t >= mult:
        if n % t == 0:
            return t
        t -= mult
    return n


def _params(*sem):
    return pltpu.CompilerParams(dimension_semantics=sem, vmem_limit_bytes=VMEM_LIMIT_V7X)


def _mm(a, b, dims, out_dtype, name, scale=1.0, res=None, tm=512, tn=512, tk=512):
    if dims == NN:
        (M, K), (_, N) = a.shape, b.shape
    elif dims == NT:
        (M, K), (N, _) = a.shape, b.shape
    else:
        (K, M), (_, N) = a.shape, b.shape
    tm, tn, tk = _tile(M, tm), _tile(N, tn), _tile(K, tk)
    nk = K // tk
    if dims == TN:
        a_spec = pl.BlockSpec((tk, tm), lambda i, j, k: (k, i))
    else:
        a_spec = pl.BlockSpec((tm, tk), lambda i, j, k: (i, k))
    if dims == NT:
        b_spec = pl.BlockSpec((tn, tk), lambda i, j, k: (j, k))
    else:
        b_spec = pl.BlockSpec((tk, tn), lambda i, j, k: (k, j))
    o_spec = pl.BlockSpec((tm, tn), lambda i, j, k: (i, j))
    has_res = res is not None

    def body(*refs):
        a_ref, b_ref = refs[0], refs[1]
        r_ref = refs[2] if has_res else None
        o_ref = refs[3] if has_res else refs[2]

        def finish(acc):
            r = acc * scale if scale != 1.0 else acc
            if has_res:
                r = r + r_ref[...]
            o_ref[...] = r.astype(out_dtype)

        p = _dot(a_ref[...].astype(BF16), b_ref[...].astype(BF16), dims)
        if nk == 1:
            finish(p)
        else:
            acc_ref = refs[-1]
            k = pl.program_id(2)

            @pl.when(k == 0)
            def _():
                acc_ref[...] = p

            @pl.when(k > 0)
            def _():
                acc_ref[...] += p

            @pl.when(k == nk - 1)
            def _():
                finish(acc_ref[...])

    in_specs = [a_spec, b_spec] + ([o_spec] if has_res else [])
    args = (a, b) + ((res,) if has_res else ())
    return pl.pallas_call(
        body, name=name,
        out_shape=jax.ShapeDtypeStruct((M, N), out_dtype),
        grid=(M // tm, N // tn, nk),
        in_specs=in_specs, out_specs=o_spec,
        scratch_shapes=[pltpu.VMEM((tm, tn), F32)] if nk > 1 else [],
        compiler_params=_params("parallel", "parallel", "arbitrary"),
    )(*args)


def _rows8(x):
    r, d = x.shape
    return jnp.sum(x.reshape(r // 8, 8, d), axis=0)


def _rmsnorm(h, g, name):
    S, D = h.shape
    ts = _tile(S, 512, 8)

    def body(h_ref, g_ref, o_ref):
        x = h_ref[...]
        r = lax.rsqrt(jnp.mean(x * x, axis=-1, keepdims=True) + RMS_EPS)
        o_ref[...] = ((x * r) * g_ref[...]).astype(BF16)

    return pl.pallas_call(
        body, name=name,
        out_shape=jax.ShapeDtypeStruct((S, D), BF16),
        grid=(S // ts,),
        in_specs=[pl.BlockSpec((ts, D), lambda i: (i, 0)), pl.BlockSpec((1, D), lambda i: (0, 0))],
        out_specs=pl.BlockSpec((ts, D), lambda i: (i, 0)),
        compiler_params=_params("parallel"),
    )(h, g.reshape(1, D))


def _final_loss(h, g, tgt, name):
    S, D = h.shape
    ts = _tile(S, 512, 8)

    def body(h_ref, g_ref, t_ref, dh_ref, dg_ref, l_ref):
        x = h_ref[...]
        r = lax.rsqrt(jnp.mean(x * x, axis=-1, keepdims=True) + RMS_EPS)
        xhat = x * r
        err = xhat * g_ref[...] - t_ref[...]
        d = err * (1.0 / D)
        dxh = d * g_ref[...]
        c = jnp.mean(dxh * xhat, axis=-1, keepdims=True)
        dh_ref[...] = r * (dxh - xhat * c)
        part = _rows8(d * xhat)
        lpart = _rows8(err * err)

        @pl.when(pl.program_id(0) == 0)
        def _():
            dg_ref[...] = part
            l_ref[...] = lpart

        @pl.when(pl.program_id(0) > 0)
        def _():
            dg_ref[...] += part
            l_ref[...] += lpart

    row = pl.BlockSpec((ts, D), lambda i: (i, 0))
    acc = pl.BlockSpec((8, D), lambda i: (0, 0))
    return pl.pallas_call(
        body, name=name,
        out_shape=(jax.ShapeDtypeStruct((S, D), F32), jax.ShapeDtypeStruct((8, D), F32),
                   jax.ShapeDtypeStruct((8, D), F32)),
        grid=(S // ts,),
        in_specs=[row, pl.BlockSpec((1, D), lambda i: (0, 0)), row],
        out_specs=(row, acc, acc),
        compiler_params=_params("arbitrary"),
    )(h, g.reshape(1, D), tgt)


def _ffn_up(h, g, win_t, name, carry=None):
    S, D = h.shape
    F = win_t.shape[0] // 2
    tm, tn = _tile(S, 512, 16), _tile(F, 1408)
    nf = F // tn

    def body(h_ref, g_ref, wg_ref, wu_ref, xn_ref, gate_ref, up_ref, act_ref):
        x = h_ref[...]
        r = lax.rsqrt(jnp.mean(x * x, axis=-1, keepdims=True) + RMS_EPS)
        xn = ((x * r) * g_ref[...]).astype(BF16)
        xn_ref[...] = xn
        gate = _dot(xn, wg_ref[...], NT)
        up = _dot(xn, wu_ref[...], NT)
        gate_ref[...] = gate.astype(BF16)
        up_ref[...] = up.astype(BF16)
        sig = 1.0 / (1.0 + jnp.exp(-gate))
        act_ref[...] = (gate * sig * up).astype(BF16)

    row = pl.BlockSpec((tm, D), lambda i, j: (i, 0))
    blk = pl.BlockSpec((tm, tn), lambda i, j: (i, j))
    hid = jax.ShapeDtypeStruct((S, F), BF16)
    return _pcall(
        body, (h, g.reshape(1, D), win_t, win_t), name=name,
        out_shape=(jax.ShapeDtypeStruct((S, D), BF16), hid, hid, hid),
        grid=(S // tm, nf),
        in_specs=[row, pl.BlockSpec((1, D), lambda i, j: (0, 0)),
                  pl.BlockSpec((tn, D), lambda i, j: (j, 0)),
                  pl.BlockSpec((tn, D), lambda i, j: (j + nf, 0))],
        out_specs=(row, blk, blk, blk),
        sem=("arbitrary", "arbitrary"), carry=carry)


def _ffn_dact(dh, wo, gate, up, name):
    S, D = dh.shape
    F = wo.shape[0]
    tm, tn = _tile(S, 512, 16), _tile(F, 1408)

    def body(dh_ref, wo_ref, g_ref, u_ref, dg_ref, du_ref):
        d = _dot(dh_ref[...].astype(BF16), wo_ref[...], NT) * FFN_RES_SCALE
        g = g_ref[...].astype(F32)
        u = u_ref[...].astype(F32)
        sig = 1.0 / (1.0 + jnp.exp(-g))
        du_ref[...] = (d * (g * sig)).astype(BF16)
        dg_ref[...] = (d * u * (sig * (1.0 + g * (1.0 - sig)))).astype(BF16)

    blk = pl.BlockSpec((tm, tn), lambda j, i: (i, j))
    hid = jax.ShapeDtypeStruct((S, F), BF16)
    return pl.pallas_call(
        body, name=name, out_shape=(hid, hid),
        grid=(F // tn, S // tm),
        in_specs=[pl.BlockSpec((tm, D), lambda j, i: (i, 0)), pl.BlockSpec((tn, D), lambda j, i: (j, 0)), blk, blk],
        out_specs=(blk, blk),
        compiler_params=_params("arbitrary", "arbitrary"),
    )(dh, wo, gate, up)


def _dwin(dgate, dup, xn, name, carry=None):
    S, F = dgate.shape
    D = xn.shape[1]
    tr, tk = _tile(F, 1408), _tile(S, TN_CHUNK, 16)
    nf, nk = F // tr, S // tk

    def body(dg_ref, du_ref, x_ref, o_ref, acc_ref):
        r, k = pl.program_id(0), pl.program_id(1)

        def accumulate(a_ref):
            p = _dot(a_ref[...], x_ref[...], TN)

            @pl.when(k == 0)
            def _():
                acc_ref[...] = p

            @pl.when(k > 0)
            def _():
                acc_ref[...] += p

        @pl.when(r < nf)
        def _():
            accumulate(dg_ref)

        @pl.when(r >= nf)
        def _():
            accumulate(du_ref)

        @pl.when(k == nk - 1)
        def _():
            o_ref[...] = acc_ref[...].astype(BF16)

    return _pcall(
        body, (dgate, dup, xn), name=name, out_shape=jax.ShapeDtypeStruct((2 * F, D), BF16),
        grid=(2 * nf, nk),
        in_specs=[pl.BlockSpec((tk, tr), lambda r, k: (jnp.where(r < nf, k, 0), jnp.minimum(r, nf - 1))),
                  pl.BlockSpec((tk, tr), lambda r, k: (jnp.where(r >= nf, k, 0), jnp.maximum(r - nf, 0))),
                  pl.BlockSpec((tk, D), lambda r, k: (k, 0))],
        out_specs=pl.BlockSpec((tr, D), lambda r, k: (r, 0)),
        scratch_shapes=[pltpu.VMEM((tr, D), F32)],
        sem=("arbitrary", "arbitrary"), carry=carry)


def _dx_norm_bwd(terms, h, g, res, name, carry=None):
    S, D = h.shape
    tm = _tile(S, 256, 16)
    n = len(terms)

    def body(*refs):
        dy_refs, w_refs = refs[:n], refs[n:2 * n]
        h_ref, g_ref, r_ref, dh_ref, dg_ref = refs[2 * n:]
        d = _dot(dy_refs[0][...], w_refs[0][...], terms[0][2])
        for t in range(1, n):
            d = d + _dot(dy_refs[t][...], w_refs[t][...], terms[t][2])
        x = h_ref[...]
        r = lax.rsqrt(jnp.mean(x * x, axis=-1, keepdims=True) + RMS_EPS)
        xhat = x * r
        dxh = d * g_ref[...]
        c = jnp.mean(dxh * xhat, axis=-1, keepdims=True)
        dh_ref[...] = r * (dxh - xhat * c) + r_ref[...]
        part = _rows8(d * xhat)

        @pl.when(pl.program_id(0) == 0)
        def _():
            dg_ref[...] = part

        @pl.when(pl.program_id(0) > 0)
        def _():
            dg_ref[...] += part

    def w_spec(w, nblk, blk):
        return pl.BlockSpec((w.shape[0] // nblk, w.shape[1]), lambda i: (blk, 0))

    row = pl.BlockSpec((tm, D), lambda i: (i, 0))
    in_specs = [pl.BlockSpec((tm, t[0].shape[1]), lambda i: (i, 0)) for t in terms]
    in_specs += [w_spec(t[1], t[3], t[4]) for t in terms]
    in_specs += [row, pl.BlockSpec((1, D), lambda i: (0, 0)), row]
    return _pcall(
        body, (*[t[0] for t in terms], *[t[1] for t in terms], h, g.reshape(1, D), res), name=name,
        out_shape=(jax.ShapeDtypeStruct((S, D), F32), jax.ShapeDtypeStruct((8, D), F32)),
        grid=(S // tm,),
        in_specs=in_specs,
        out_specs=(row, pl.BlockSpec((8, D), lambda i: (0, 0))),
        sem=("arbitrary",), carry=carry)


def _rope_tables(S):
    half = HEAD_DIM // 2
    inv_freq = ROPE_THETA ** (-jnp.arange(half, dtype=F32) / half)
    ang = jnp.arange(S).astype(F32)[:, None] * inv_freq[None, :]
    cos, sin = jnp.cos(ang), jnp.sin(ang)
    cos_t = jnp.tile(cos, (1, LANES // half))
    sin_t = jnp.tile(jnp.concatenate([-sin, sin], axis=1), (1, LANES // HEAD_DIM))
    return cos_t, sin_t


def _swap_halves(x):
    lane = lax.broadcasted_iota(jnp.int32, x.shape, 1)
    first = (lane % HEAD_DIM) < (HEAD_DIM // 2)
    return jnp.where(first, pltpu.roll(x, LANES - HEAD_DIM // 2, 1), pltpu.roll(x, HEAD_DIM // 2, 1))


def _rotary(x, cos_t, sin_t, n_rot, inverse, name):
    S, C = x.shape
    ts = _tile(S, 512, 16)
    ng = C // LANES

    def body(x_ref, c_ref, s_ref, o_ref):
        cs, sn = c_ref[...], s_ref[...]
        for gidx in range(ng):
            sl = slice(gidx * LANES, (gidx + 1) * LANES)
            v = x_ref[:, sl].astype(F32)
            if gidx < n_rot:
                if inverse:
                    v = v * cs + _swap_halves(v * sn)
                else:
                    v = v * cs + _swap_halves(v) * sn
            o_ref[:, sl] = v.astype(BF16)

    row = pl.BlockSpec((ts, C), lambda i: (i, 0))
    tab = pl.BlockSpec((ts, LANES), lambda i: (i, 0))
    return pl.pallas_call(
        body, name=name, out_shape=jax.ShapeDtypeStruct((S, C), BF16),
        grid=(S // ts,), in_specs=[row, tab, tab], out_specs=row,
        compiler_params=_params("parallel"),
    )(x, cos_t, sin_t)


def _head_masks():
    lane = lax.broadcasted_iota(jnp.int32, (BLK, LANES), 1)
    return lane < HEAD_DIM


def _split_bf16(x):
    hi = x.astype(BF16)
    lo = (x - hi.astype(F32)).astype(BF16)
    return hi, lo


def _sb_scores(qh, ks, carry, diag, tri_excl, strict):
    n_heads = len(qh)
    zs = [_dot(qh[n], ks[n], NT) for n in range(n_heads)]
    a_l, b_l, split_l = [], [], []
    for z in zs:
        z = z * ATTN_SCALE
        a = jnp.minimum(z, 0.0) - jnp.log(1.0 + jnp.exp(-jnp.abs(z)))
        b = a - z
        if diag:
            b = jnp.where(strict, b, 0.0)
        a_l.append(a)
        b_l.append(b)
        split_l.append(_split_bf16(b))
    sufs = [_dot(hi, tri_excl, NN) + _dot(lo, tri_excl, NN) for hi, lo in split_l]
    w_l = []
    for n in range(n_heads):
        w = jnp.exp(a_l[n] + sufs[n] + carry[n])
        if diag:
            w = jnp.where(strict, w, 0.0)
        w_l.append(w)
    return a_l, b_l, w_l


SB_FWD_PAIRS = 4
SB_BWD_PAIRS = 2
SB_BWD_QBLOCKS = 2


def _any_alive(carries):
    top = carries[0]
    for c in carries[1:]:
        top = jnp.maximum(top, c)
    return (jnp.max(top) > SB_LOG_FLOOR).astype(jnp.int32)


def _sb_fwd(qkv, name, carry=None):
    S, D3 = qkv.shape
    D = D3 // 3
    npair, nb = D // LANES, S // BLK
    P = min(SB_FWD_PAIRS, npair)
    ngroup = npair // P
    W = P * LANES

    def body(q_ref, k_ref, v_ref, o_ref):
        i = pl.program_id(1)
        m0 = _head_masks()
        row = lax.broadcasted_iota(jnp.int32, (BLK, BLK), 0)
        col = lax.broadcasted_iota(jnp.int32, (BLK, BLK), 1)
        strict = col < row
        tri_excl = jnp.where(row > col, 1.0, 0.0).astype(BF16)
        zq = jnp.zeros((BLK, LANES), BF16)
        qh = []
        for p in range(P):
            q2 = q_ref[:, p * LANES:(p + 1) * LANES]
            qh += [jnp.where(m0, q2, zq), jnp.where(m0, zq, q2)]

        def block(j, carry, acc, diag):
            off = pl.multiple_of(j * BLK, BLK)
            ks = [k_ref[pl.ds(off, BLK), p * LANES:(p + 1) * LANES] for p in range(P)]
            vh = []
            for p in range(P):
                v2 = v_ref[pl.ds(off, BLK), p * LANES:(p + 1) * LANES]
                vh += [jnp.where(m0, v2, zq), jnp.where(m0, zq, v2)]
            _, b_l, w_l = _sb_scores(qh, [ks[n // 2] for n in range(2 * P)], carry, diag, tri_excl, strict)
            wb = [w.astype(BF16) for w in w_l]
            new_acc = [acc[p] + _dot(wb[2 * p], vh[2 * p], NN) + _dot(wb[2 * p + 1], vh[2 * p + 1], NN)
                       for p in range(P)]
            new_carry = [carry[n] + jnp.sum(b_l[n], axis=1, keepdims=True) for n in range(2 * P)]
            return new_carry, new_acc

        c0 = jnp.zeros((BLK, 1), F32)
        carry, acc = block(i, [c0] * (2 * P), [jnp.zeros((BLK, LANES), F32)] * P, True)

        def cond(st):
            return jnp.logical_and(st[0] >= 0, st[1] > 0)

        def step(st):
            j, _, carry, acc = st
            carry, acc = block(j, carry, acc, False)
            return j - 1, _any_alive(carry), carry, acc

        st = lax.while_loop(cond, step, (i - 1, _any_alive(carry), carry, acc))
        for p in range(P):
            o_ref[:, p * LANES:(p + 1) * LANES] = st[3][p]

    return _pcall(
        body, (qkv, qkv, qkv), name=name, out_shape=jax.ShapeDtypeStruct((S, D), F32),
        grid=(ngroup, nb),
        in_specs=[pl.BlockSpec((BLK, W), lambda g, i: (i, g)),
                  pl.BlockSpec((S, W), lambda g, i: (0, ngroup + g)),
                  pl.BlockSpec((S, W), lambda g, i: (0, 2 * ngroup + g))],
        out_specs=pl.BlockSpec((BLK, W), lambda g, i: (i, g)),
        sem=("arbitrary", "arbitrary"), carry=carry)


def _sb_bwd(qkv, o, do, name, carry=None):
    S, D3 = qkv.shape
    D = D3 // 3
    npair, nb = D // LANES, S // BLK
    P = min(SB_BWD_PAIRS, npair)
    ngroup = npair // P
    W = P * LANES

    QB = SB_BWD_QBLOCKS if nb % SB_BWD_QBLOCKS == 0 else 1
    nch = QB * 2 * P

    def body(q_ref, k_ref, v_ref, o_ref, do_ref, dq_ref, dk_ref, dv_ref):
        i_first = pl.program_id(1) * QB
        m0 = _head_masks()
        row = lax.broadcasted_iota(jnp.int32, (BLK, BLK), 0)
        col = lax.broadcasted_iota(jnp.int32, (BLK, BLK), 1)
        strict = col < row
        tri_excl = jnp.where(row > col, 1.0, 0.0).astype(BF16)
        tri_incl = jnp.where(row >= col, 1.0, 0.0).astype(BF16)
        zq = jnp.zeros((BLK, LANES), BF16)
        lanes = [slice(p * LANES, (p + 1) * LANES) for p in range(P)]
        qh, doh, delta = [], [], []
        for qb in range(QB):
            rs = slice(qb * BLK, (qb + 1) * BLK)
            for sl in lanes:
                q2, do2 = q_ref[rs, sl], do_ref[rs, sl]
                qh += [jnp.where(m0, q2, zq), jnp.where(m0, zq, q2)]
                doh += [jnp.where(m0, do2, zq), jnp.where(m0, zq, do2)]
                prod = do2.astype(F32) * o_ref[rs, sl]
                delta += [jnp.sum(jnp.where(m0, prod, 0.0), axis=1, keepdims=True),
                          jnp.sum(jnp.where(m0, 0.0, prod), axis=1, keepdims=True)]

        @pl.when(pl.program_id(1) == 0)
        def _():
            dk_ref[...] = jnp.zeros_like(dk_ref)
            dv_ref[...] = jnp.zeros_like(dv_ref)

        def block(js, valid, cb, cg, dq, diag):
            offs = [pl.multiple_of(j * BLK, BLK) for j in js]
            ks, vs, kh = [], [], []
            for qb in range(QB):
                for sl in lanes:
                    k2, v2 = k_ref[pl.ds(offs[qb], BLK), sl], v_ref[pl.ds(offs[qb], BLK), sl]
                    ks += [k2, k2]
                    vs += [v2, v2]
                    kh += [jnp.where(m0, k2, zq), jnp.where(m0, zq, k2)]
            dws = [_dot(doh[n], vs[n], NT) for n in range(nch)]
            a_l, b_l, w_l = _sb_scores(qh, ks, cb, diag, tri_excl, strict)
            wb = [w.astype(BF16) for w in w_l]
            g_l = [dws[n] * wb[n].astype(F32) for n in range(nch)]
            gsplit = [_split_bf16(g) for g in g_l]
            gincs = [_dot(hi, tri_incl, NN) + _dot(lo, tri_incl, NN) for hi, lo in gsplit]
            dzs = []
            for n in range(nch):
                beta = jnp.exp(a_l[n])
                dz = g_l[n] * (1.0 - beta) - beta * (delta[n] - (gincs[n] + cg[n]))
                if diag:
                    dz = jnp.where(strict, dz, 0.0)
                if valid[n // (2 * P)] is not None:
                    dz = jnp.where(valid[n // (2 * P)], dz, 0.0)
                dzs.append((dz * ATTN_SCALE).astype(BF16))
            ndq = []
            for qb in range(QB):
                for p, sl in enumerate(lanes):
                    n0 = qb * 2 * P + 2 * p
                    ndq.append(dq[qb * P + p] + _dot(dzs[n0], kh[n0], NN) + _dot(dzs[n0 + 1], kh[n0 + 1], NN))
                    dk_ref[pl.ds(offs[qb], BLK), sl] += _dot(dzs[n0], qh[n0], TN) + _dot(dzs[n0 + 1], qh[n0 + 1], TN)
                    dv_ref[pl.ds(offs[qb], BLK), sl] += _dot(wb[n0], doh[n0], TN) + _dot(wb[n0 + 1], doh[n0 + 1], TN)
            ncb = [cb[n] + jnp.sum(b_l[n], axis=1, keepdims=True) for n in range(nch)]
            ncg = [cg[n] + jnp.sum(g_l[n], axis=1, keepdims=True) for n in range(nch)]
            return ncb, ncg, ndq

        c0 = jnp.zeros((BLK, 1), F32)
        cb, cg, dq = block([i_first + qb for qb in range(QB)], [None] * QB, [c0] * nch, [c0] * nch,
                           [jnp.zeros((BLK, LANES), F32)] * (QB * P), True)

        def cond(st):
            return jnp.logical_and(i_first + QB - 1 - st[0] >= 0, st[1] > 0)

        def step(st):
            t, _, cb, cg, dq = st
            js = [i_first + qb - t for qb in range(QB)]
            valid = [js[qb] >= 0 for qb in range(QB - 1)] + [None]
            cb = [cb[n] if valid[n // (2 * P)] is None else jnp.where(valid[n // (2 * P)], cb[n], NEG_BIG)
                  for n in range(nch)]
            cb, cg, dq = block([jnp.maximum(j, 0) for j in js], valid, cb, cg, dq, False)
            return t + 1, _any_alive(cb), cb, cg, dq

        st = lax.while_loop(cond, step, (1, _any_alive(cb), cb, cg, dq))
        for qb in range(QB):
            for p, sl in enumerate(lanes):
                dq_ref[qb * BLK:(qb + 1) * BLK, sl] = st[4][qb * P + p].astype(BF16)

    blk = lambda c: pl.BlockSpec((QB * BLK, W), lambda g, i: (i, c * ngroup + g))
    col_all = lambda c: pl.BlockSpec((S, W), lambda g, i: (0, c * ngroup + g))
    return _pcall(
        body, (qkv, qkv, qkv, o, do), name=name,
        out_shape=(jax.ShapeDtypeStruct((S, D), BF16), jax.ShapeDtypeStruct((S, D), F32),
                   jax.ShapeDtypeStruct((S, D), F32)),
        grid=(ngroup, nb // QB),
        in_specs=[blk(0), col_all(1), col_all(2), blk(0), blk(0)],
        out_specs=(blk(0), col_all(0), col_all(0)),
        sem=("arbitrary", "arbitrary"), carry=carry)


SWA_Q_GROUPS = 4


def _roll_heads(x):
    return pltpu.roll(x.astype(F32), HEAD_DIM, 1).astype(BF16)


def _swa_valid(i):
    r = lax.broadcasted_iota(jnp.int32, (BLK, 2 * BLK), 0)
    c = lax.broadcasted_iota(jnp.int32, (BLK, 2 * BLK), 1)
    diff = r + BLK - c
    return (diff >= 0) & (diff < BLK) & ((i > 0) | (c >= BLK))


def _swa_probs(z, valid, sink):
    z = jnp.where(valid, z * ATTN_SCALE, NEG_BIG)
    mx = jnp.maximum(jnp.max(z, axis=1, keepdims=True), sink)
    p = jnp.exp(z - mx)
    ps = jnp.exp(sink - mx)
    inv = 1.0 / (jnp.sum(p, axis=1, keepdims=True) + ps)
    return p * inv, ps * inv


def _swa_operands(q_ref, kc_ref, kp_ref, vc_ref, vp_ref, s_ref, m):
    m0 = _head_masks()
    m0k = jnp.concatenate([m0, m0], axis=0)
    kk = jnp.concatenate([kp_ref[...], kc_ref[...]], axis=0)
    vv = jnp.concatenate([vp_ref[...], vc_ref[...]], axis=0)
    ksw, vsw = _roll_heads(kk), _roll_heads(vv)
    zk = jnp.zeros_like(kk)
    heads = []
    for c in range(SWA_Q_GROUPS):
        qc = q_ref[:, c * LANES:(c + 1) * LANES]
        zq = jnp.zeros_like(qc)
        for u in range(2):
            same = u == c // 2
            sel = (lambda x, z, mk: jnp.where(mk, x, z)) if u == 0 else (lambda x, z, mk: jnp.where(mk, z, x))
            heads.append(dict(
                c=c, same=same, sel=sel,
                qm=sel(qc, zq, m0),
                k=kk if same else ksw, v=vv if same else vsw,
                km=sel(kk if same else ksw, zk, m0k), vm=sel(vv if same else vsw, zk, m0k),
                sink=s_ref[0, m * 2 * SWA_Q_GROUPS + 2 * c + u]))
    return heads, m0


def _swa_fwd(q, kv, sinks, name):
    S, D = q.shape
    nkvp = kv.shape[1] // (2 * LANES)
    nb = S // BLK
    qw = SWA_Q_GROUPS * LANES

    def body(q_ref, kc_ref, kp_ref, vc_ref, vp_ref, s_ref, o_ref):
        m, i = pl.program_id(0), pl.program_id(1)
        valid = _swa_valid(i)
        heads, _ = _swa_operands(q_ref, kc_ref, kp_ref, vc_ref, vp_ref, s_ref, m)
        zs = [_dot(hd["qm"], hd["k"], NT) for hd in heads]
        ps = [_swa_probs(z, valid, hd["sink"])[0].astype(BF16) for z, hd in zip(zs, heads)]
        for c in range(SWA_Q_GROUPS):
            o_ref[:, c * LANES:(c + 1) * LANES] = (_dot(ps[2 * c], heads[2 * c]["vm"], NN)
                                                   + _dot(ps[2 * c + 1], heads[2 * c + 1]["vm"], NN))

    prev = lambda i: jnp.maximum(i - 1, 0)
    return pl.pallas_call(
        body, name=name, out_shape=jax.ShapeDtypeStruct((S, D), F32),
        grid=(nkvp, nb),
        in_specs=[pl.BlockSpec((BLK, qw), lambda m, i: (i, m)),
                  pl.BlockSpec((BLK, LANES), lambda m, i: (i, m)),
                  pl.BlockSpec((BLK, LANES), lambda m, i: (prev(i), m)),
                  pl.BlockSpec((BLK, LANES), lambda m, i: (i, nkvp + m)),
                  pl.BlockSpec((BLK, LANES), lambda m, i: (prev(i), nkvp + m)),
                  pl.BlockSpec(memory_space=pltpu.SMEM)],
        out_specs=pl.BlockSpec((BLK, qw), lambda m, i: (i, m)),
        compiler_params=_params("arbitrary", "arbitrary"),
    )(q, kv, kv, kv, kv, sinks)


def _swa_bwd(q, kv, sinks, o, do, name):
    S, D = q.shape
    nkvp = kv.shape[1] // (2 * LANES)
    nb = S // BLK
    qw = SWA_Q_GROUPS * LANES
    nh = 2 * SWA_Q_GROUPS

    def body(q_ref, kc_ref, kp_ref, vc_ref, vp_ref, s_ref, o_ref, do_ref, dq_ref, dk_ref, dv_ref, ds_ref):
        m, i = pl.program_id(0), pl.program_id(1)
        valid = _swa_valid(i)
        heads, m0 = _swa_operands(q_ref, kc_ref, kp_ref, vc_ref, vp_ref, s_ref, m)

        @pl.when(i == 0)
        def _():
            dk_ref[...] = jnp.zeros_like(dk_ref)
            dv_ref[...] = jnp.zeros_like(dv_ref)
            ds_ref[...] = jnp.zeros_like(ds_ref)

        doms, deltas = [], []
        for hd in heads:
            c = hd["c"]
            doc = do_ref[:, c * LANES:(c + 1) * LANES]
            prod = doc.astype(F32) * o_ref[:, c * LANES:(c + 1) * LANES]
            doms.append(hd["sel"](doc, jnp.zeros_like(doc), m0))
            deltas.append(jnp.sum(hd["sel"](prod, 0.0, m0), axis=1, keepdims=True))
        zs = [_dot(hd["qm"], hd["k"], NT) for hd in heads]
        dps = [_dot(dom, hd["v"], NT) for dom, hd in zip(doms, heads)]
        pbs, dscs = [], []
        for n, hd in enumerate(heads):
            p, psink = _swa_probs(zs[n], valid, hd["sink"])
            pbs.append(p.astype(BF16))
            dscs.append((p * (dps[n] - deltas[n]) * ATTN_SCALE).astype(BF16))
            dsink = jnp.sum(jnp.broadcast_to(-(psink * deltas[n]), (BLK, LANES)), axis=0, keepdims=True)
            ds_ref[0, n:n + 1, :] += dsink
        for c in range(SWA_Q_GROUPS):
            dq_ref[:, c * LANES:(c + 1) * LANES] = (_dot(dscs[2 * c], heads[2 * c]["km"], NN)
                                                    + _dot(dscs[2 * c + 1], heads[2 * c + 1]["km"], NN))
        acc = {}
        for n, hd in enumerate(heads):
            dk_n = _dot(dscs[n], hd["qm"], TN)
            dv_n = _dot(pbs[n], doms[n], TN)
            for key, val in ((("k", hd["same"]), dk_n), (("v", hd["same"]), dv_n)):
                acc[key] = val if key not in acc else acc[key] + val
        dkk = acc["k", True] + pltpu.roll(acc["k", False], HEAD_DIM, 1)
        dvv = acc["v", True] + pltpu.roll(acc["v", False], HEAD_DIM, 1)
        poff = pl.multiple_of(jnp.maximum(i - 1, 0) * BLK, BLK)
        coff = pl.multiple_of(i * BLK, BLK)
        dk_ref[pl.ds(poff, BLK), :] += dkk[:BLK]
        dv_ref[pl.ds(poff, BLK), :] += dvv[:BLK]
        dk_ref[pl.ds(coff, BLK), :] += dkk[BLK:]
        dv_ref[pl.ds(coff, BLK), :] += dvv[BLK:]

    prev = lambda i: jnp.maximum(i - 1, 0)
    qblk = pl.BlockSpec((BLK, qw), lambda m, i: (i, m))
    col_all = pl.BlockSpec((S, LANES), lambda m, i: (0, m))
    return pl.pallas_call(
        body, name=name,
        out_shape=(jax.ShapeDtypeStruct((S, D), F32),
                   jax.ShapeDtypeStruct((S, nkvp * LANES), F32),
                   jax.ShapeDtypeStruct((S, nkvp * LANES), F32),
                   jax.ShapeDtypeStruct((nkvp, nh, LANES), F32)),
        grid=(nkvp, nb),
        in_specs=[qblk,
                  pl.BlockSpec((BLK, LANES), lambda m, i: (i, m)),
                  pl.BlockSpec((BLK, LANES), lambda m, i: (prev(i), m)),
                  pl.BlockSpec((BLK, LANES), lambda m, i: (i, nkvp + m)),
                  pl.BlockSpec((BLK, LANES), lambda m, i: (prev(i), nkvp + m)),
                  pl.BlockSpec(memory_space=pltpu.SMEM),
                  qblk, qblk],
        out_specs=(qblk, col_all, col_all, pl.BlockSpec((1, nh, LANES), lambda m, i: (m, 0, 0))),
        compiler_params=_params("arbitrary", "arbitrary"),
    )(q, kv, kv, kv, kv, sinks, o, do)


def _dev_index(p):
    return 4 * p[0] + 2 * p[1] + p[2]


def _gather_plan(x_refs, out_refs, send_sems, recv_sems, local_sems):
    n = len(x_refs)
    x_, y_, c_ = lax.axis_index("x"), lax.axis_index("y"), lax.axis_index("c")
    me, sibling = (x_, y_, c_), (x_, y_, 1 - c_)
    chips = [(1 - x_, y_), (x_, 1 - y_), (1 - x_, 1 - y_)]

    def copy(t, k, block, to, src=None):
        dst = out_refs[t].at[_dev_index(block)]
        return pltpu.make_async_remote_copy(
            src_ref=dst if src is None else src, dst_ref=dst,
            send_sem=send_sems.at[7 * t + k], recv_sem=recv_sems.at[7 * t + k],
            device_id=to, device_id_type=MESH)

    mine = [pltpu.make_async_copy(x_refs[t], out_refs[t].at[_dev_index(me)], local_sems.at[t]) for t in range(n)]
    first = []
    for t in range(n):
        first.append(copy(t, 0, me, sibling, src=x_refs[t]))
        first += [copy(t, 1 + j, me, (*chip, c_), src=x_refs[t]) for j, chip in enumerate(chips)]
    arrived = lambda t, j: copy(t, 1 + j, (*chips[j], c_), me)
    forward = lambda t, j: copy(t, 4 + j, (*chips[j], c_), sibling)
    from_sibling = lambda t: copy(t, 0, sibling, me)
    forwarded = lambda t, j: copy(t, 4 + j, (*chips[j], 1 - c_), me)
    return n, mine, first, arrived, forward, from_sibling, forwarded


def _gather_start(x_refs, out_refs, send_sems, recv_sems, local_sems):
    _, mine, first, *_ = _gather_plan(x_refs, out_refs, send_sems, recv_sems, local_sems)
    for cp in mine + first:
        cp.start()


def _gather_finish(x_refs, out_refs, send_sems, recv_sems, local_sems):
    n, mine, first, arrived, forward, from_sibling, forwarded = _gather_plan(
        x_refs, out_refs, send_sems, recv_sems, local_sems)
    passed = []
    for j in range(3):
        for t in range(n):
            arrived(t, j).wait_recv()
            fwd = forward(t, j)
            fwd.start()
            passed.append(fwd)
    for t in range(n):
        from_sibling(t).wait_recv()
    for j in range(3):
        for t in range(n):
            forwarded(t, j).wait_recv()
    for cp in first + passed:
        cp.wait_send()
    for cp in mine:
        cp.wait()


def _scatter_plan(b_refs, out_refs, send_sems, recv_sems, local_sems):
    n = len(b_refs)
    x_, y_, c_ = lax.axis_index("x"), lax.axis_index("y"), lax.axis_index("c")
    my_idx = _dev_index((x_, y_, c_))
    mine = [pltpu.make_async_copy(b_refs[t].at[my_idx], out_refs[t].at[my_idx], local_sems.at[t]) for t in range(n)]
    copies = []
    for t in range(n):
        for k in range(1, N_DEV):
            peer = (x_ ^ ((k >> 2) & 1), y_ ^ ((k >> 1) & 1), c_ ^ (k & 1))
            copies.append(pltpu.make_async_remote_copy(
                src_ref=b_refs[t].at[_dev_index(peer)], dst_ref=out_refs[t].at[my_idx],
                send_sem=send_sems.at[7 * t + k - 1], recv_sem=recv_sems.at[7 * t + k - 1],
                device_id=peer, device_id_type=MESH))
    return mine, copies


def _scatter_start(b_refs, out_refs, send_sems, recv_sems, local_sems):
    mine, copies = _scatter_plan(b_refs, out_refs, send_sems, recv_sems, local_sems)
    for cp in mine + copies:
        cp.start()


def _scatter_finish(b_refs, out_refs, send_sems, recv_sems, local_sems):
    mine, copies = _scatter_plan(b_refs, out_refs, send_sems, recv_sems, local_sems)
    for cp in copies:
        cp.wait_recv()
    for cp in copies:
        cp.wait_send()
    for cp in mine:
        cp.wait()


def _exchange_operands(kind, tensors):
    if kind == "gather":
        args = list(tensors)
        shapes = [jax.ShapeDtypeStruct((N_DEV,) + t.shape, t.dtype) for t in tensors]
        return args, shapes, _gather_start, _gather_finish
    args = [t.reshape(N_DEV, t.shape[0] // N_DEV, t.shape[1]) for t in tensors]
    shapes = [jax.ShapeDtypeStruct(a.shape, a.dtype) for a in args]
    return args, shapes, _scatter_start, _scatter_finish


def _exchange_results(kind, tensors, res):
    if kind == "gather":
        return [r.reshape(N_DEV * t.shape[0], t.shape[1]) for r, t in zip(res, tensors)]
    return list(res)


def _exchange_sems(n):
    return [pltpu.SemaphoreType.DMA((7 * n,)), pltpu.SemaphoreType.DMA((7 * n,)), pltpu.SemaphoreType.DMA((n,))]


def _exchange(kind, tensors, name):
    n = len(tensors)
    args, shapes, start, finish = _exchange_operands(kind, tensors)

    def body(*refs):
        start(refs[:n], refs[n:2 * n], *refs[2 * n:])
        finish(refs[:n], refs[n:2 * n], *refs[2 * n:])

    hbm = pl.BlockSpec(memory_space=pl.ANY)
    res = pl.pallas_call(body, name=name, out_shape=shapes, in_specs=[hbm] * n, out_specs=[hbm] * n,
                         scratch_shapes=_exchange_sems(n))(*args)
    return _exchange_results(kind, tensors, res)


def _pcall(body, args, *, name, out_shape, grid, in_specs, out_specs, sem, scratch_shapes=(), carry=None):
    if carry is None:
        out = pl.pallas_call(body, name=name, out_shape=out_shape, grid=grid, in_specs=list(in_specs),
                             out_specs=out_specs, scratch_shapes=list(scratch_shapes),
                             compiler_params=_params(*sem))(*args)
        return out, None
    kind, tensors = carry
    multi = isinstance(out_shape, (tuple, list))
    shapes = list(out_shape) if multi else [out_shape]
    ospecs = list(out_specs) if multi else [out_specs]
    n_in, n_out, n_scr, n_c = len(in_specs), len(shapes), len(scratch_shapes), len(tensors)
    c_args, c_shapes, start, finish = _exchange_operands(kind, tensors)

    def wrapped(*refs):
        ins, rest = refs[:n_in], refs[n_in:]
        c_in, rest = rest[:n_c], rest[n_c:]
        outs, rest = rest[:n_out], rest[n_out:]
        c_out, rest = rest[:n_c], rest[n_c:]
        scr, sems = rest[:n_scr], rest[n_scr:]
        ids = [pl.program_id(a) for a in range(len(grid))]
        first, last = ids[0] == 0, ids[0] == grid[0] - 1
        for a in range(1, len(grid)):
            first = jnp.logical_and(first, ids[a] == 0)
            last = jnp.logical_and(last, ids[a] == grid[a] - 1)

        @pl.when(first)
        def _():
            start(c_in, c_out, *sems)

        body(*ins, *outs, *scr)

        @pl.when(last)
        def _():
            finish(c_in, c_out, *sems)

    hbm = pl.BlockSpec(memory_space=pl.ANY)
    res = pl.pallas_call(
        wrapped, name=name, out_shape=shapes + c_shapes, grid=grid,
        in_specs=list(in_specs) + [hbm] * n_c, out_specs=ospecs + [hbm] * n_c,
        scratch_shapes=list(scratch_shapes) + _exchange_sems(n_c),
        compiler_params=_params(*sem))(*args, *c_args)
    outs = tuple(res[:n_out]) if multi else res[0]
    return outs, _exchange_results(kind, tensors, res[n_out:])


def _sum8(parts, name):
    _, R, C = parts.shape
    tr = _tile(R, 256, 16)

    def body(p_ref, g_ref):
        g = p_ref[0].astype(F32)
        for s in range(1, N_DEV):
            g = g + p_ref[s].astype(F32)
        g_ref[...] = g

    return pl.pallas_call(
        body, name=name, out_shape=jax.ShapeDtypeStruct((R, C), F32),
        grid=(R // tr,),
        in_specs=[pl.BlockSpec((N_DEV, tr, C), lambda i: (0, i, 0))],
        out_specs=pl.BlockSpec((tr, C), lambda i: (i, 0)),
        compiler_params=_params("parallel"),
    )(parts)


def _adamw(g, w, m, v, name):
    R, C = g.shape
    tr = _tile(R, 256, 8)
    c1 = 1.0 - ADAM_B1 ** ADAM_STEP
    c2 = 1.0 - ADAM_B2 ** ADAM_STEP

    def body(g_ref, w_ref, m_ref, v_ref, d_ref, nm_ref, nv_ref):
        gg = g_ref[...]
        nm = ADAM_B1 * m_ref[...] + (1.0 - ADAM_B1) * gg
        nv = ADAM_B2 * v_ref[...] + (1.0 - ADAM_B2) * (gg * gg)
        m_hat = nm / c1
        v_hat = nv / c2
        nm_ref[...] = nm
        nv_ref[...] = nv
        d_ref[...] = -ADAM_LR * (m_hat / (jnp.sqrt(v_hat) + ADAM_EPS) + ADAM_WD * w_ref[...])

    row = pl.BlockSpec((tr, C), lambda i: (i, 0))
    shp = jax.ShapeDtypeStruct((R, C), F32)
    return pl.pallas_call(
        body, name=name, out_shape=(shp, shp, shp),
        grid=(R // tr,), in_specs=[row, row, row, row], out_specs=(row, row, row),
        compiler_params=_params("parallel"),
    )(g, w, m, v)


def _ffn_down(act, wo, h, tag):
    return _mm(act, wo, NN, F32, f"{tag}_down", scale=FFN_RES_SCALE, res=h, tm=512, tn=1024, tk=2816)


def _ffn_fwd(h, g, win_t, wo, tag):
    saved, _ = _ffn_up(h, g, win_t, f"{tag}_up")
    return _ffn_down(saved[3], wo, h, tag), saved


def _ffn_bwd(dh, h, g, win_t, wo, saved, tag, scatter=False):
    xn, gate, up, act = saved
    dgate, dup = _ffn_dact(dh, wo, gate, up, f"{tag}_dact")
    dwo = _mm(act, dh, TN, BF16, f"{tag}_dwo", scale=FFN_RES_SCALE, tm=1408, tn=1024, tk=TN_CHUNK)
    dwin_t, got_wo = _dwin(dgate, dup, xn, f"{tag}_dwin", carry=("scatter", [dwo]) if scatter else None)
    (dh_in, dg), got_win = _dx_norm_bwd([(dgate, win_t, NN, 2, 0), (dup, win_t, NN, 2, 1)], h, g, dh, f"{tag}_dx",
                                        carry=("scatter", [dwin_t]) if scatter else None)
    if scatter:
        return dh_in, dg, got_win[0], got_wo[0]
    return dh_in, dg, dwin_t, dwo


def _proj(a, w, dims, out_dtype, name, res=None):
    return _mm(a, w, dims, out_dtype, name, res=res, tm=1024, tn=1024, tk=1024)


def _proj_dw(x, dy, name):
    return _mm(x, dy, TN, BF16, name, tm=1024, tn=1024, tk=TN_CHUNK)


def kernel(x, ffn1_norm, ffn1_w_in, ffn1_w_out, mix_norm, ffn2_norm, ffn2_w_in, ffn2_w_out, sb_w_qkv, sb_w_o, kv_norm, kv_w, swa_w_q, swa_sinks, swa_w_o, final_norm, loss_target, m_ffn1_norm, m_ffn1_w_in, m_ffn1_w_out, m_mix_norm, m_ffn2_norm, m_ffn2_w_in, m_ffn2_w_out, m_sb_w_qkv, m_sb_w_o, m_kv_norm, m_kv_w, m_swa_w_q, m_swa_sinks, m_swa_w_o, m_final_norm, v_ffn1_norm, v_ffn1_w_in, v_ffn1_w_out, v_mix_norm, v_ffn2_norm, v_ffn2_w_in, v_ffn2_w_out, v_sb_w_qkv, v_sb_w_o, v_kv_norm, v_kv_w, v_swa_w_q, v_swa_sinks, v_swa_w_o, v_final_norm):
    S, D = x.shape[1], x.shape[2]
    L = ffn1_w_in.shape[0]
    KV = kv_w.shape[1]
    assert L == 2 and swa_sinks.shape == (1, 2 * SWA_Q_GROUPS * KV // (2 * LANES))

    def bf(w):
        return w.astype(BF16)

    def bft(w):
        return jnp.transpose(w).astype(BF16)

    cos_t, sin_t = _rope_tables(S)
    h0 = x.reshape(S, D)
    tgt = loss_target.reshape(S, D)

    win1a_t, = _exchange("gather", [bft(ffn1_w_in[0])], "gather_first_weight")
    sv_a1, (wo1a, wqkv_t, w_sbo) = _ffn_up(
        h0, ffn1_norm[0], win1a_t, "ffn1a_up",
        carry=("gather", [bf(ffn1_w_out[0]), bft(sb_w_qkv[0]), bf(sb_w_o[0])]))
    h1 = _ffn_down(sv_a1[3], wo1a, h0, "ffn1a")
    hn_a = _rmsnorm(h1, mix_norm[0], "mix_a_norm")
    qkv = _proj(hn_a, wqkv_t, NT, BF16, "sb_qkv")
    o_sb, later = _sb_fwd(qkv, "sb_attn", carry=("gather", [
        bft(ffn2_w_in[0]), bf(ffn2_w_out[0]), bf(kv_w), bft(ffn1_w_in[1]), bf(ffn1_w_out[1]),
        bf(swa_w_q[0]), bf(swa_w_o[0]), bft(ffn2_w_in[1]), bf(ffn2_w_out[1])]))
    win2a_t, wo2a, w_kv, win1b_t, wo1b, w_q, w_swo, win2b_t, wo2b = later
    h2 = _proj(o_sb, w_sbo, NN, F32, "sb_out", res=h1)
    h3, sv_a2 = _ffn_fwd(h2, ffn2_norm[0], win2a_t, wo2a, "ffn2a")
    kvn = _rmsnorm(h3, kv_norm, "kv_norm")
    kv_raw = _proj(kvn, w_kv, NN, F32, "kv_proj")
    kv_rot = _rotary(kv_raw, cos_t, sin_t, KV // (2 * LANES), False, "kv_rope")
    h4, sv_b1 = _ffn_fwd(h3, ffn1_norm[1], win1b_t, wo1b, "ffn1b")
    hn_b = _rmsnorm(h4, mix_norm[1], "mix_b_norm")
    q_raw = _proj(hn_b, w_q, NN, F32, "swa_q")
    q_rot = _rotary(q_raw, cos_t, sin_t, D // LANES, False, "q_rope")
    o_sw = _swa_fwd(q_rot, kv_rot, swa_sinks, "swa_attn")
    h5 = _proj(o_sw, w_swo, NN, F32, "swa_out", res=h4)
    h6, sv_b2 = _ffn_fwd(h5, ffn2_norm[1], win2b_t, wo2b, "ffn2b")
    dh6, dg_final, sq_err = _final_loss(h6, final_norm, tgt, "final_loss")
    loss = lax.psum(0.5 * jnp.sum(sq_err) / D, ("x", "y", "c"))

    dh5, dg_f2b, dwin2b_t, dwo2b = _ffn_bwd(dh6, h5, ffn2_norm[1], win2b_t, wo2b, sv_b2, "ffn2b")
    do_sw = _proj(dh5, w_swo, NT, BF16, "swa_out_dx")
    dw_swo = _proj_dw(o_sw, dh5, "swa_out_dw")
    dq_rot, dk_sw, dv_sw, dsink = _swa_bwd(q_rot, kv_rot, swa_sinks, o_sw, do_sw, "swa_attn_bwd")
    dq = _rotary(dq_rot, cos_t, sin_t, D // LANES, True, "q_rope_bwd")
    dw_q = _proj_dw(hn_b, dq, "swa_q_dw")
    (dh4, dg_mix_b), _ = _dx_norm_bwd([(dq, w_q, NT, 1, 0)], h4, mix_norm[1], dh5, "swa_q_dx")
    dh3, dg_f1b, dwin1b_t, dwo1b = _ffn_bwd(dh4, h3, ffn1_norm[1], win1b_t, wo1b, sv_b1, "ffn1b")
    dkv = _rotary(jnp.concatenate([dk_sw, dv_sw], axis=1), cos_t, sin_t, KV // (2 * LANES), True, "kv_rope_bwd")
    dw_kv = _proj_dw(kvn, dkv, "kv_proj_dw")
    (dh3, dg_kv), _ = _dx_norm_bwd([(dkv, w_kv, NT, 1, 0)], h3, kv_norm, dh3, "kv_proj_dx")
    dh2, dg_f2a, dwin2a_t, dwo2a = _ffn_bwd(dh3, h2, ffn2_norm[0], win2a_t, wo2a, sv_a2, "ffn2a")
    do_sb = _proj(dh2, w_sbo, NT, BF16, "sb_out_dx")
    dw_sbo = _proj_dw(o_sb, dh2, "sb_out_dw")
    (dq_sb, dk_sb, dv_sb), early = _sb_bwd(qkv, o_sb, do_sb, "sb_attn_bwd", carry=("scatter", [
        dwin2b_t, dwo2b, dw_swo, dw_q, dwin1b_t, dwo1b, dw_kv, dwin2a_t, dwo2a, dw_sbo]))
    dqkv = jnp.concatenate([dq_sb, dk_sb.astype(BF16), dv_sb.astype(BF16)], axis=1)
    dwqkv_t = _proj_dw(dqkv, hn_a, "sb_qkv_dw")
    (dh1, dg_mix_a), (p_qkv,) = _dx_norm_bwd([(dqkv, wqkv_t, NN, 1, 0)], h1, mix_norm[0], dh2, "sb_qkv_dx",
                                             carry=("scatter", [dwqkv_t]))
    dx, dg_f1a, p_win1a, p_wo1a = _ffn_bwd(dh1, h0, ffn1_norm[0], win1a_t, wo1a, sv_a1, "ffn1a", scatter=True)

    p_win2b, p_wo2b, p_swo, p_q, p_win1b, p_wo1b, p_kv, p_win2a, p_wo2a, p_sbo = early

    def natural(parts, tag):
        return _sum8(parts, f"sum_{tag}")

    def from_t(parts, tag):
        return jnp.transpose(_sum8(parts, f"sum_{tag}"))

    grads = {
        "ffn1_w_in": jnp.stack([from_t(p_win1a, "win1a"), from_t(p_win1b, "win1b")]),
        "ffn1_w_out": jnp.stack([natural(p_wo1a, "wo1a"), natural(p_wo1b, "wo1b")]),
        "ffn2_w_in": jnp.stack([from_t(p_win2a, "win2a"), from_t(p_win2b, "win2b")]),
        "ffn2_w_out": jnp.stack([natural(p_wo2a, "wo2a"), natural(p_wo2b, "wo2b")]),
        "sb_w_qkv": from_t(p_qkv, "qkv")[None],
        "sb_w_o": natural(p_sbo, "sbo")[None],
        "kv_w": natural(p_kv, "kv"),
        "swa_w_q": natural(p_q, "swq")[None],
        "swa_w_o": natural(p_swo, "swo")[None],
    }

    small_w = [ffn1_norm, mix_norm, ffn2_norm, kv_norm, final_norm, swa_sinks]
    small_m = [m_ffn1_norm, m_mix_norm, m_ffn2_norm, m_kv_norm, m_final_norm, m_swa_sinks]
    small_v = [v_ffn1_norm, v_mix_norm, v_ffn2_norm, v_kv_norm, v_final_norm, v_swa_sinks]
    SMALL_ROWS = 16

    def pack_small(ts):
        rows_ = [t.reshape(-1, D) for t in ts[:-1]]
        sink_row = jnp.pad(ts[-1].reshape(1, -1), ((0, 0), (0, D - ts[-1].size)))
        flat = jnp.concatenate(rows_ + [sink_row], axis=0)
        return jnp.pad(flat, ((0, SMALL_ROWS - flat.shape[0]), (0, 0)))

    def unpack_small(flat):
        out, r = [], 0
        for t in small_w[:-1]:
            n = t.size // D
            out.append(flat[r:r + n].reshape(t.shape))
            r += n
        out.append(flat[r, :swa_sinks.size].reshape(swa_sinks.shape))
        return out

    def gain(parts8):
        return jnp.sum(parts8, axis=0, keepdims=True)

    g_small_local = pack_small([
        jnp.concatenate([gain(dg_f1a), gain(dg_f1b)], axis=0),
        jnp.concatenate([gain(dg_mix_a), gain(dg_mix_b)], axis=0),
        jnp.concatenate([gain(dg_f2a), gain(dg_f2b)], axis=0),
        gain(dg_kv), gain(dg_final), dsink[:, :, 0].reshape(1, -1)])
    small_parts = _exchange("gather", [g_small_local], "gather_small_grads")[0]
    g_small = _sum8(small_parts.reshape(N_DEV, SMALL_ROWS, D), "sum_small")
    d_small, nm_small, nv_small = _adamw(g_small, pack_small(small_w), pack_small(small_m), pack_small(small_v), "adamw_small")
    small_names = ["ffn1_norm", "mix_norm", "ffn2_norm", "kv_norm", "final_norm", "swa_sinks"]
    result = {"grad": dict(zip(small_names, unpack_small(g_small))),
              "delta": dict(zip(small_names, unpack_small(d_small))),
              "new_m": dict(zip(small_names, unpack_small(nm_small))),
              "new_v": dict(zip(small_names, unpack_small(nv_small)))}

    big = {"ffn1_w_in": (ffn1_w_in, m_ffn1_w_in, v_ffn1_w_in), "ffn1_w_out": (ffn1_w_out, m_ffn1_w_out, v_ffn1_w_out),
           "ffn2_w_in": (ffn2_w_in, m_ffn2_w_in, v_ffn2_w_in), "ffn2_w_out": (ffn2_w_out, m_ffn2_w_out, v_ffn2_w_out),
           "sb_w_qkv": (sb_w_qkv, m_sb_w_qkv, v_sb_w_qkv), "sb_w_o": (sb_w_o, m_sb_w_o, v_sb_w_o),
           "kv_w": (kv_w, m_kv_w, v_kv_w), "swa_w_q": (swa_w_q, m_swa_w_q, v_swa_w_q),
           "swa_w_o": (swa_w_o, m_swa_w_o, v_swa_w_o)}
    for nm, (w, m, v) in big.items():
        g = grads[nm]
        two_d = lambda t: t.reshape(-1, t.shape[-1])
        d, new_m, new_v = _adamw(two_d(g), two_d(w), two_d(m), two_d(v), f"adamw_{nm}")
        result["grad"][nm] = g
        result["delta"][nm] = d.reshape(w.shape)
        result["new_m"][nm] = new_m.reshape(w.shape)
        result["new_v"][nm] = new_v.reshape(w.shape)

    order = ["ffn1_norm", "ffn1_w_in", "ffn1_w_out", "mix_norm", "ffn2_norm", "ffn2_w_in", "ffn2_w_out",
             "sb_w_qkv", "sb_w_o", "kv_norm", "kv_w", "swa_w_q", "swa_sinks", "swa_w_o", "final_norm"]
    outs = [result[kind][nm] for kind in ("grad", "delta", "new_m", "new_v") for nm in order]
    return (loss, dx.reshape(x.shape), *outs)
```

```python
import jax
import jax.numpy as jnp
from jax import lax
from jax.experimental import pallas as pl
from jax.experimental.pallas import tpu as pltpu

F32 = jnp.float32
BF16 = jnp.bfloat16

N_DEV = 8
HEAD_DIM = 64
LANES = 128
BLK = 128
RMS_EPS = 1e-6
FFN_RES_SCALE = 0.5
ROPE_THETA = 10000.0
ATTN_SCALE = HEAD_DIM ** -0.5
SB_LOG_FLOOR = -88.0
NEG_BIG = -1e30
VMEM_LIMIT_V7X = 56 * 1024 * 1024

ADAM_LR = 0.001
ADAM_B1 = 0.9
ADAM_B2 = 0.999
ADAM_EPS = 1e-08
ADAM_WD = 0.01
ADAM_STEP = 10

NN = ((1,), (0,))
NT = ((1,), (1,))
TN = ((0,), (0,))
TN_CHUNK = 2048
MESH = pl.DeviceIdType.MESH


def _dot(a, b, dims):
    return lax.dot_general(a, b, (dims, ((), ())), preferred_element_type=F32)


def _tile(n, pref, mult=LANES):
    if n <= pref:
        return n
    t = (pref // mult) * mult
    while t >= mult:
        if n % t == 0:
            return t
        t -= mult
    return n


def _params(*sem):
    return pltpu.CompilerParams(dimension_semantics=sem, vmem_limit_bytes=VMEM_LIMIT_V7X)


def _mm(a, b, dims, out_dtype, name, scale=1.0, res=None, tm=512, tn=512, tk=512):
    if dims == NN:
        (M, K), (_, N) = a.shape, b.shape
    elif dims == NT:
        (M, K), (N, _) = a.shape, b.shape
    else:
        (K, M), (_, N) = a.shape, b.shape
    tm, tn, tk = _tile(M, tm), _tile(N, tn), _tile(K, tk)
    nk = K // tk
    if dims == TN:
        a_spec = pl.BlockSpec((tk, tm), lambda i, j, k: (k, i))
    else:
        a_spec = pl.BlockSpec((tm, tk), lambda i, j, k: (i, k))
    if dims == NT:
        b_spec = pl.BlockSpec((tn, tk), lambda i, j, k: (j, k))
    else:
        b_spec = pl.BlockSpec((tk, tn), lambda i, j, k: (k, j))
    o_spec = pl.BlockSpec((tm, tn), lambda i, j, k: (i, j))
    has_res = res is not None

    def body(*refs):
        a_ref, b_ref = refs[0], refs[1]
        r_ref = refs[2] if has_res else None
        o_ref = refs[3] if has_res else refs[2]

        def finish(acc):
            r = acc * scale if scale != 1.0 else acc
            if has_res:
                r = r + r_ref[...]
            o_ref[...] = r.astype(out_dtype)

        p = _dot(a_ref[...].astype(BF16), b_ref[...].astype(BF16), dims)
        if nk == 1:
            finish(p)
        else:
            acc_ref = refs[-1]
            k = pl.program_id(2)

            @pl.when(k == 0)
            def _():
                acc_ref[...] = p

            @pl.when(k > 0)
            def _():
                acc_ref[...] += p

            @pl.when(k == nk - 1)
            def _():
                finish(acc_ref[...])

    in_specs = [a_spec, b_spec] + ([o_spec] if has_res else [])
    args = (a, b) + ((res,) if has_res else ())
    return pl.pallas_call(
        body, name=name,
        out_shape=jax.ShapeDtypeStruct((M, N), out_dtype),
        grid=(M // tm, N // tn, nk),
        in_specs=in_specs, out_specs=o_spec,
        scratch_shapes=[pltpu.VMEM((tm, tn), F32)] if nk > 1 else [],
        compiler_params=_params("parallel", "parallel", "arbitrary"),
    )(*args)


def _rows8(x):
    r, d = x.shape
    return jnp.sum(x.reshape(r // 8, 8, d), axis=0)


def _rmsnorm(h, g, name):
    S, D = h.shape
    ts = _tile(S, 512, 8)

    def body(h_ref, g_ref, o_ref):
        x = h_ref[...]
        r = lax.rsqrt(jnp.mean(x * x, axis=-1, keepdims=True) + RMS_EPS)
        o_ref[...] = ((x * r) * g_ref[...]).astype(BF16)

    return pl.pallas_call(
        body, name=name,
        out_shape=jax.ShapeDtypeStruct((S, D), BF16),
        grid=(S // ts,),
        in_specs=[pl.BlockSpec((ts, D), lambda i: (i, 0)), pl.BlockSpec((1, D), lambda i: (0, 0))],
        out_specs=pl.BlockSpec((ts, D), lambda i: (i, 0)),
        compiler_params=_params("parallel"),
    )(h, g.reshape(1, D))


def _final_loss(h, g, tgt, name):
    S, D = h.shape
    ts = _tile(S, 512, 8)

    def body(h_ref, g_ref, t_ref, dh_ref, dg_ref, l_ref):
        x = h_ref[...]
        r = lax.rsqrt(jnp.mean(x * x, axis=-1, keepdims=True) + RMS_EPS)
        xhat = x * r
        err = xhat * g_ref[...] - t_ref[...]
        d = err * (1.0 / D)
        dxh = d * g_ref[...]
        c = jnp.mean(dxh * xhat, axis=-1, keepdims=True)
        dh_ref[...] = r * (dxh - xhat * c)
        part = _rows8(d * xhat)
        lpart = _rows8(err * err)

        @pl.when(pl.program_id(0) == 0)
        def _():
            dg_ref[...] = part
            l_ref[...] = lpart

        @pl.when(pl.program_id(0) > 0)
        def _():
            dg_ref[...] += part
            l_ref[...] += lpart

    row = pl.BlockSpec((ts, D), lambda i: (i, 0))
    acc = pl.BlockSpec((8, D), lambda i: (0, 0))
    return pl.pallas_call(
        body, name=name,
        out_shape=(jax.ShapeDtypeStruct((S, D), F32), jax.ShapeDtypeStruct((8, D), F32),
                   jax.ShapeDtypeStruct((8, D), F32)),
        grid=(S // ts,),
        in_specs=[row, pl.BlockSpec((1, D), lambda i: (0, 0)), row],
        out_specs=(row, acc, acc),
        compiler_params=_params("arbitrary"),
    )(h, g.reshape(1, D), tgt)


def _ffn_up(h, g, win_t, name, carry=None):
    S, D = h.shape
    F = win_t.shape[0] // 2
    tm, tn = _tile(S, 512, 16), _tile(F, 1408)
    nf = F // tn

    def body(h_ref, g_ref, wg_ref, wu_ref, xn_ref, silu_ref, dsilu_ref, up_ref, act_ref):
        x = h_ref[...]
        r = lax.rsqrt(jnp.mean(x * x, axis=-1, keepdims=True) + RMS_EPS)
        xn = ((x * r) * g_ref[...]).astype(BF16)
        xn_ref[...] = xn
        gate = _dot(xn, wg_ref[...], NT)
        up = _dot(xn, wu_ref[...], NT)
        sig = 1.0 / (1.0 + jnp.exp(-gate))
        silu = gate * sig
        up_ref[...] = up.astype(BF16)
        silu_ref[...] = silu.astype(BF16)
        dsilu_ref[...] = (sig + silu * (1.0 - sig)).astype(BF16)
        act_ref[...] = (silu * up).astype(BF16)

    row = pl.BlockSpec((tm, D), lambda i, j: (i, 0))
    blk = pl.BlockSpec((tm, tn), lambda i, j: (i, j))
    hid = jax.ShapeDtypeStruct((S, F), BF16)
    return _pcall(
        body, (h, g.reshape(1, D), win_t, win_t), name=name,
        out_shape=(jax.ShapeDtypeStruct((S, D), BF16), hid, hid, hid, hid),
        grid=(S // tm, nf),
        in_specs=[row, pl.BlockSpec((1, D), lambda i, j: (0, 0)),
                  pl.BlockSpec((tn, D), lambda i, j: (j, 0)),
                  pl.BlockSpec((tn, D), lambda i, j: (j + nf, 0))],
        out_specs=(row, blk, blk, blk, blk),
        sem=("arbitrary", "arbitrary"), carry=carry)


def _ffn_dact(dh, wo, silu, dsilu, up, name):
    S, D = dh.shape
    F = wo.shape[0]
    tm, tn = _tile(S, 512, 16), _tile(F, 1408)

    def body(dh_ref, wo_ref, s_ref, ds_ref, u_ref, dg_ref, du_ref):
        d = _dot(dh_ref[...].astype(BF16), wo_ref[...], NT) * FFN_RES_SCALE
        du_ref[...] = (d * s_ref[...].astype(F32)).astype(BF16)
        dg_ref[...] = (d * u_ref[...].astype(F32) * ds_ref[...].astype(F32)).astype(BF16)

    blk = pl.BlockSpec((tm, tn), lambda j, i: (i, j))
    hid = jax.ShapeDtypeStruct((S, F), BF16)
    return pl.pallas_call(
        body, name=name, out_shape=(hid, hid),
        grid=(F // tn, S // tm),
        in_specs=[pl.BlockSpec((tm, D), lambda j, i: (i, 0)), pl.BlockSpec((tn, D), lambda j, i: (j, 0)),
                  blk, blk, blk],
        out_specs=(blk, blk),
        compiler_params=_params("arbitrary", "arbitrary"),
    )(dh, wo, silu, dsilu, up)


def _dwin(dgate, dup, xn, name, carry=None):
    S, F = dgate.shape
    D = xn.shape[1]
    tr, tk = _tile(F, 1408), _tile(S, TN_CHUNK, 16)
    nf, nk = F // tr, S // tk

    def body(dg_ref, du_ref, x_ref, o_ref, acc_ref):
        r, k = pl.program_id(0), pl.program_id(1)

        def accumulate(a_ref):
            p = _dot(a_ref[...], x_ref[...], TN)

            @pl.when(k == 0)
            def _():
                acc_ref[...] = p

            @pl.when(k > 0)
            def _():
                acc_ref[...] += p

        @pl.when(r < nf)
        def _():
            accumulate(dg_ref)

        @pl.when(r >= nf)
        def _():
            accumulate(du_ref)

        @pl.when(k == nk - 1)
        def _():
            o_ref[...] = acc_ref[...].astype(BF16)

    return _pcall(
        body, (dgate, dup, xn), name=name, out_shape=jax.ShapeDtypeStruct((2 * F, D), BF16),
        grid=(2 * nf, nk),
        in_specs=[pl.BlockSpec((tk, tr), lambda r, k: (jnp.where(r < nf, k, 0), jnp.minimum(r, nf - 1))),
                  pl.BlockSpec((tk, tr), lambda r, k: (jnp.where(r >= nf, k, 0), jnp.maximum(r - nf, 0))),
                  pl.BlockSpec((tk, D), lambda r, k: (k, 0))],
        out_specs=pl.BlockSpec((tr, D), lambda r, k: (r, 0)),
        scratch_shapes=[pltpu.VMEM((tr, D), F32)],
        sem=("arbitrary", "arbitrary"), carry=carry)


def _dx_norm_bwd(terms, h, g, res, name, carry=None):
    S, D = h.shape
    tm = _tile(S, 256, 16)
    n = len(terms)

    def body(*refs):
        dy_refs, w_refs = refs[:n], refs[n:2 * n]
        h_ref, g_ref, r_ref, dh_ref, dg_ref = refs[2 * n:]
        d = _dot(dy_refs[0][...], w_refs[0][...], terms[0][2])
        for t in range(1, n):
            d = d + _dot(dy_refs[t][...], w_refs[t][...], terms[t][2])
        x = h_ref[...]
        r = lax.rsqrt(jnp.mean(x * x, axis=-1, keepdims=True) + RMS_EPS)
        xhat = x * r
        dxh = d * g_ref[...]
        c = jnp.mean(dxh * xhat, axis=-1, keepdims=True)
        dh_ref[...] = r * (dxh - xhat * c) + r_ref[...]
        part = _rows8(d * xhat)

        @pl.when(pl.program_id(0) == 0)
        def _():
            dg_ref[...] = part

        @pl.when(pl.program_id(0) > 0)
        def _():
            dg_ref[...] += part

    def w_spec(w, nblk, blk):
        return pl.BlockSpec((w.shape[0] // nblk, w.shape[1]), lambda i: (blk, 0))

    row = pl.BlockSpec((tm, D), lambda i: (i, 0))
    in_specs = [pl.BlockSpec((tm, t[0].shape[1]), lambda i: (i, 0)) for t in terms]
    in_specs += [w_spec(t[1], t[3], t[4]) for t in terms]
    in_specs += [row, pl.BlockSpec((1, D), lambda i: (0, 0)), row]
    return _pcall(
        body, (*[t[0] for t in terms], *[t[1] for t in terms], h, g.reshape(1, D), res), name=name,
        out_shape=(jax.ShapeDtypeStruct((S, D), F32), jax.ShapeDtypeStruct((8, D), F32)),
        grid=(S // tm,),
        in_specs=in_specs,
        out_specs=(row, pl.BlockSpec((8, D), lambda i: (0, 0))),
        sem=("arbitrary",), carry=carry)


def _rope_tables(S):
    half = HEAD_DIM // 2
    inv_freq = ROPE_THETA ** (-jnp.arange(half, dtype=F32) / half)
    ang = jnp.arange(S).astype(F32)[:, None] * inv_freq[None, :]
    cos, sin = jnp.cos(ang), jnp.sin(ang)
    cos_t = jnp.tile(cos, (1, LANES // half))
    sin_t = jnp.tile(jnp.concatenate([-sin, sin], axis=1), (1, LANES // HEAD_DIM))
    return cos_t, sin_t


def _swap_halves(x):
    lane = lax.broadcasted_iota(jnp.int32, x.shape, 1)
    first = (lane % HEAD_DIM) < (HEAD_DIM // 2)
    return jnp.where(first, pltpu.roll(x, LANES - HEAD_DIM // 2, 1), pltpu.roll(x, HEAD_DIM // 2, 1))


def _rotary(x, cos_t, sin_t, n_rot, inverse, name):
    S, C = x.shape
    ts = _tile(S, 512, 16)
    ng = C // LANES

    def body(x_ref, c_ref, s_ref, o_ref):
        cs, sn = c_ref[...], s_ref[...]
        for gidx in range(ng):
            sl = slice(gidx * LANES, (gidx + 1) * LANES)
            v = x_ref[:, sl].astype(F32)
            if gidx < n_rot:
                if inverse:
                    v = v * cs + _swap_halves(v * sn)
                else:
                    v = v * cs + _swap_halves(v) * sn
            o_ref[:, sl] = v.astype(BF16)

    row = pl.BlockSpec((ts, C), lambda i: (i, 0))
    tab = pl.BlockSpec((ts, LANES), lambda i: (i, 0))
    return pl.pallas_call(
        body, name=name, out_shape=jax.ShapeDtypeStruct((S, C), BF16),
        grid=(S // ts,), in_specs=[row, tab, tab], out_specs=row,
        compiler_params=_params("parallel"),
    )(x, cos_t, sin_t)


def _head_masks():
    lane = lax.broadcasted_iota(jnp.int32, (BLK, LANES), 1)
    return lane < HEAD_DIM


def _split_bf16(x):
    hi = x.astype(BF16)
    lo = (x - hi.astype(F32)).astype(BF16)
    return hi, lo


def _sb_scores(qh, ks, carry, diag, tri_excl, strict):
    n_heads = len(qh)
    zs = [_dot(qh[n], ks[n], NT) for n in range(n_heads)]
    a_l, b_l, split_l = [], [], []
    for z in zs:
        a = jnp.minimum(z, 0.0) - jnp.log(1.0 + jnp.exp(-jnp.abs(z)))
        b = a - z
        if diag:
            b = jnp.where(strict, b, 0.0)
        a_l.append(a)
        b_l.append(b)
        split_l.append(_split_bf16(b))
    sufs = [_dot(hi, tri_excl, NN) + _dot(lo, tri_excl, NN) for hi, lo in split_l]
    w_l = []
    for n in range(n_heads):
        w = jnp.exp(a_l[n] + sufs[n] + carry[n])
        if diag:
            w = jnp.where(strict, w, 0.0)
        w_l.append(w)
    return a_l, b_l, w_l


SB_FWD_PAIRS = 4
SB_BWD_PAIRS = 2
SB_BWD_QBLOCKS = 2


def _any_alive(carries):
    top = carries[0]
    for c in carries[1:]:
        top = jnp.maximum(top, c)
    return (jnp.max(top) > SB_LOG_FLOOR).astype(jnp.int32)


def _sb_fwd(qkv, name, carry=None):
    S, D3 = qkv.shape
    D = D3 // 3
    npair, nb = D // LANES, S // BLK
    P = min(SB_FWD_PAIRS, npair)
    ngroup = npair // P
    W = P * LANES

    def body(q_ref, k_ref, v_ref, o_ref):
        i = pl.program_id(1)
        m0 = _head_masks()
        row = lax.broadcasted_iota(jnp.int32, (BLK, BLK), 0)
        col = lax.broadcasted_iota(jnp.int32, (BLK, BLK), 1)
        strict = col < row
        tri_excl = jnp.where(row > col, 1.0, 0.0).astype(BF16)
        zq = jnp.zeros((BLK, LANES), BF16)
        qh = []
        for p in range(P):
            q2 = q_ref[:, p * LANES:(p + 1) * LANES] * ATTN_SCALE
            qh += [jnp.where(m0, q2, zq), jnp.where(m0, zq, q2)]

        def block(j, carry, acc, diag):
            off = pl.multiple_of(j * BLK, BLK)
            ks = [k_ref[pl.ds(off, BLK), p * LANES:(p + 1) * LANES] for p in range(P)]
            vh = []
            for p in range(P):
                v2 = v_ref[pl.ds(off, BLK), p * LANES:(p + 1) * LANES]
                vh += [jnp.where(m0, v2, zq), jnp.where(m0, zq, v2)]
            _, b_l, w_l = _sb_scores(qh, [ks[n // 2] for n in range(2 * P)], carry, diag, tri_excl, strict)
            wb = [w.astype(BF16) for w in w_l]
            new_acc = [acc[p] + _dot(wb[2 * p], vh[2 * p], NN) + _dot(wb[2 * p + 1], vh[2 * p + 1], NN)
                       for p in range(P)]
            new_carry = [carry[n] + jnp.sum(b_l[n], axis=1, keepdims=True) for n in range(2 * P)]
            return new_carry, new_acc

        c0 = jnp.zeros((BLK, 1), F32)
        carry, acc = block(i, [c0] * (2 * P), [jnp.zeros((BLK, LANES), F32)] * P, True)

        def cond(st):
            return jnp.logical_and(st[0] >= 0, st[1] > 0)

        def step(st):
            j, _, carry, acc = st
            carry, acc = block(j, carry, acc, False)
            return j - 1, _any_alive(carry), carry, acc

        st = lax.while_loop(cond, step, (i - 1, _any_alive(carry), carry, acc))
        for p in range(P):
            o_ref[:, p * LANES:(p + 1) * LANES] = st[3][p]

    return _pcall(
        body, (qkv, qkv, qkv), name=name, out_shape=jax.ShapeDtypeStruct((S, D), F32),
        grid=(ngroup, nb),
        in_specs=[pl.BlockSpec((BLK, W), lambda g, i: (i, g)),
                  pl.BlockSpec((S, W), lambda g, i: (0, ngroup + g)),
                  pl.BlockSpec((S, W), lambda g, i: (0, 2 * ngroup + g))],
        out_specs=pl.BlockSpec((BLK, W), lambda g, i: (i, g)),
        sem=("arbitrary", "arbitrary"), carry=carry)


def _sb_bwd(qkv, o, do, name, carry=None):
    S, D3 = qkv.shape
    D = D3 // 3
    npair, nb = D // LANES, S // BLK
    P = min(SB_BWD_PAIRS, npair)
    ngroup = npair // P
    W = P * LANES

    QB = SB_BWD_QBLOCKS if nb % SB_BWD_QBLOCKS == 0 else 1
    nch = QB * 2 * P

    def body(q_ref, k_ref, v_ref, o_ref, do_ref, dq_ref, dk_ref, dv_ref):
        i_first = pl.program_id(1) * QB
        m0 = _head_masks()
        row = lax.broadcasted_iota(jnp.int32, (BLK, BLK), 0)
        col = lax.broadcasted_iota(jnp.int32, (BLK, BLK), 1)
        strict = col < row
        tri_excl = jnp.where(row > col, 1.0, 0.0).astype(BF16)
        tri_incl = jnp.where(row >= col, 1.0, 0.0).astype(BF16)
        zq = jnp.zeros((BLK, LANES), BF16)
        lanes = [slice(p * LANES, (p + 1) * LANES) for p in range(P)]
        qh, doh, delta = [], [], []
        for qb in range(QB):
            rs = slice(qb * BLK, (qb + 1) * BLK)
            for sl in lanes:
                q2, do2 = q_ref[rs, sl] * ATTN_SCALE, do_ref[rs, sl]
                qh += [jnp.where(m0, q2, zq), jnp.where(m0, zq, q2)]
                doh += [jnp.where(m0, do2, zq), jnp.where(m0, zq, do2)]
                prod = do2.astype(F32) * o_ref[rs, sl]
                delta += [jnp.sum(jnp.where(m0, prod, 0.0), axis=1, keepdims=True),
                          jnp.sum(jnp.where(m0, 0.0, prod), axis=1, keepdims=True)]

        @pl.when(pl.program_id(1) == 0)
        def _():
            dk_ref[...] = jnp.zeros_like(dk_ref)
            dv_ref[...] = jnp.zeros_like(dv_ref)

        def block(js, valid, cb, cg, dq, diag):
            offs = [pl.multiple_of(j * BLK, BLK) for j in js]
            ks, vs, kh = [], [], []
            for qb in range(QB):
                for sl in lanes:
                    k2, v2 = k_ref[pl.ds(offs[qb], BLK), sl], v_ref[pl.ds(offs[qb], BLK), sl]
                    ks += [k2, k2]
                    vs += [v2, v2]
                    k2s = k2 * ATTN_SCALE
                    kh += [jnp.where(m0, k2s, zq), jnp.where(m0, zq, k2s)]
            dws = [_dot(doh[n], vs[n], NT) for n in range(nch)]
            a_l, b_l, w_l = _sb_scores(qh, ks, cb, diag, tri_excl, strict)
            wb = [w.astype(BF16) for w in w_l]
            g_l = [dws[n] * wb[n].astype(F32) for n in range(nch)]
            gsplit = [_split_bf16(g) for g in g_l]
            gincs = [_dot(hi, tri_incl, NN) + _dot(lo, tri_incl, NN) for hi, lo in gsplit]
            dzs = []
            for n in range(nch):
                beta = jnp.exp(a_l[n])
                dz = g_l[n] - beta * (g_l[n] + ((delta[n] - cg[n]) - gincs[n]))
                if diag:
                    dz = jnp.where(strict, dz, 0.0)
                if valid[n // (2 * P)] is not None:
                    dz = jnp.where(valid[n // (2 * P)], dz, 0.0)
                dzs.append(dz.astype(BF16))
            ndq = []
            for qb in range(QB):
                for p, sl in enumerate(lanes):
                    n0 = qb * 2 * P + 2 * p
                    ndq.append(dq[qb * P + p] + _dot(dzs[n0], kh[n0], NN) + _dot(dzs[n0 + 1], kh[n0 + 1], NN))
                    dk_ref[pl.ds(offs[qb], BLK), sl] += _dot(dzs[n0], qh[n0], TN) + _dot(dzs[n0 + 1], qh[n0 + 1], TN)
                    dv_ref[pl.ds(offs[qb], BLK), sl] += _dot(wb[n0], doh[n0], TN) + _dot(wb[n0 + 1], doh[n0 + 1], TN)
            ncb = [cb[n] + jnp.sum(b_l[n], axis=1, keepdims=True) for n in range(nch)]
            ncg = [cg[n] + jnp.sum(g_l[n], axis=1, keepdims=True) for n in range(nch)]
            return ncb, ncg, ndq

        c0 = jnp.zeros((BLK, 1), F32)
        cb, cg, dq = block([i_first + qb for qb in range(QB)], [None] * QB, [c0] * nch, [c0] * nch,
                           [jnp.zeros((BLK, LANES), F32)] * (QB * P), True)

        def cond(st):
            return jnp.logical_and(i_first + QB - 1 - st[0] >= 0, st[1] > 0)

        def step(st):
            t, _, cb, cg, dq = st
            js = [i_first + qb - t for qb in range(QB)]
            valid = [js[qb] >= 0 for qb in range(QB - 1)] + [None]
            cb = [cb[n] if valid[n // (2 * P)] is None else jnp.where(valid[n // (2 * P)], cb[n], NEG_BIG)
                  for n in range(nch)]
            cb, cg, dq = block([jnp.maximum(j, 0) for j in js], valid, cb, cg, dq, False)
            return t + 1, _any_alive(cb), cb, cg, dq

        st = lax.while_loop(cond, step, (1, _any_alive(cb), cb, cg, dq))
        for qb in range(QB):
            for p, sl in enumerate(lanes):
                dq_ref[qb * BLK:(qb + 1) * BLK, sl] = st[4][qb * P + p].astype(BF16)

    blk = lambda c: pl.BlockSpec((QB * BLK, W), lambda g, i: (i, c * ngroup + g))
    col_all = lambda c: pl.BlockSpec((S, W), lambda g, i: (0, c * ngroup + g))
    return _pcall(
        body, (qkv, qkv, qkv, o, do), name=name,
        out_shape=(jax.ShapeDtypeStruct((S, D), BF16), jax.ShapeDtypeStruct((S, D), F32),
                   jax.ShapeDtypeStruct((S, D), F32)),
        grid=(ngroup, nb // QB),
        in_specs=[blk(0), col_all(1), col_all(2), blk(0), blk(0)],
        out_specs=(blk(0), col_all(0), col_all(0)),
        sem=("arbitrary", "arbitrary"), carry=carry)


SWA_Q_GROUPS = 4


def _roll_heads(x):
    return pltpu.roll(x.astype(F32), HEAD_DIM, 1).astype(BF16)


def _swa_valid(i):
    r = lax.broadcasted_iota(jnp.int32, (BLK, 2 * BLK), 0)
    c = lax.broadcasted_iota(jnp.int32, (BLK, 2 * BLK), 1)
    diff = r + BLK - c
    return (diff >= 0) & (diff < BLK) & ((i > 0) | (c >= BLK))


def _swa_probs(z, valid, sink):
    z = jnp.where(valid, z * ATTN_SCALE, NEG_BIG)
    mx = jnp.maximum(jnp.max(z, axis=1, keepdims=True), sink)
    p = jnp.exp(z - mx)
    ps = jnp.exp(sink - mx)
    inv = 1.0 / (jnp.sum(p, axis=1, keepdims=True) + ps)
    return p * inv, ps * inv


def _swa_operands(q_ref, kc_ref, kp_ref, vc_ref, vp_ref, s_ref, m):
    m0 = _head_masks()
    m0k = jnp.concatenate([m0, m0], axis=0)
    kk = jnp.concatenate([kp_ref[...], kc_ref[...]], axis=0)
    vv = jnp.concatenate([vp_ref[...], vc_ref[...]], axis=0)
    ksw, vsw = _roll_heads(kk), _roll_heads(vv)
    zk = jnp.zeros_like(kk)
    heads = []
    for c in range(SWA_Q_GROUPS):
        qc = q_ref[:, c * LANES:(c + 1) * LANES]
        zq = jnp.zeros_like(qc)
        for u in range(2):
            same = u == c // 2
            sel = (lambda x, z, mk: jnp.where(mk, x, z)) if u == 0 else (lambda x, z, mk: jnp.where(mk, z, x))
            heads.append(dict(
                c=c, same=same, sel=sel,
                qm=sel(qc, zq, m0),
                k=kk if same else ksw, v=vv if same else vsw,
                km=sel(kk if same else ksw, zk, m0k), vm=sel(vv if same else vsw, zk, m0k),
                sink=s_ref[0, m * 2 * SWA_Q_GROUPS + 2 * c + u]))
    return heads, m0


def _swa_fwd(q, kv, sinks, name):
    S, D = q.shape
    nkvp = kv.shape[1] // (2 * LANES)
    nb = S // BLK
    qw = SWA_Q_GROUPS * LANES

    def body(q_ref, kc_ref, kp_ref, vc_ref, vp_ref, s_ref, o_ref):
        m, i = pl.program_id(0), pl.program_id(1)
        valid = _swa_valid(i)
        heads, _ = _swa_operands(q_ref, kc_ref, kp_ref, vc_ref, vp_ref, s_ref, m)
        zs = [_dot(hd["qm"], hd["k"], NT) for hd in heads]
        ps = [_swa_probs(z, valid, hd["sink"])[0].astype(BF16) for z, hd in zip(zs, heads)]
        for c in range(SWA_Q_GROUPS):
            o_ref[:, c * LANES:(c + 1) * LANES] = (_dot(ps[2 * c], heads[2 * c]["vm"], NN)
                                                   + _dot(ps[2 * c + 1], heads[2 * c + 1]["vm"], NN))

    prev = lambda i: jnp.maximum(i - 1, 0)
    return pl.pallas_call(
        body, name=name, out_shape=jax.ShapeDtypeStruct((S, D), F32),
        grid=(nkvp, nb),
        in_specs=[pl.BlockSpec((BLK, qw), lambda m, i: (i, m)),
                  pl.BlockSpec((BLK, LANES), lambda m, i: (i, m)),
                  pl.BlockSpec((BLK, LANES), lambda m, i: (prev(i), m)),
                  pl.BlockSpec((BLK, LANES), lambda m, i: (i, nkvp + m)),
                  pl.BlockSpec((BLK, LANES), lambda m, i: (prev(i), nkvp + m)),
                  pl.BlockSpec(memory_space=pltpu.SMEM)],
        out_specs=pl.BlockSpec((BLK, qw), lambda m, i: (i, m)),
        compiler_params=_params("arbitrary", "arbitrary"),
    )(q, kv, kv, kv, kv, sinks)


def _swa_bwd(q, kv, sinks, o, do, name):
    S, D = q.shape
    nkvp = kv.shape[1] // (2 * LANES)
    nb = S // BLK
    qw = SWA_Q_GROUPS * LANES
    nh = 2 * SWA_Q_GROUPS

    def body(q_ref, kc_ref, kp_ref, vc_ref, vp_ref, s_ref, o_ref, do_ref, dq_ref, dk_ref, dv_ref, ds_ref):
        m, i = pl.program_id(0), pl.program_id(1)
        valid = _swa_valid(i)
        heads, m0 = _swa_operands(q_ref, kc_ref, kp_ref, vc_ref, vp_ref, s_ref, m)

        @pl.when(i == 0)
        def _():
            dk_ref[...] = jnp.zeros_like(dk_ref)
            dv_ref[...] = jnp.zeros_like(dv_ref)
            ds_ref[...] = jnp.zeros_like(ds_ref)

        doms, deltas = [], []
        for hd in heads:
            c = hd["c"]
            doc = do_ref[:, c * LANES:(c + 1) * LANES]
            prod = doc.astype(F32) * o_ref[:, c * LANES:(c + 1) * LANES]
            doms.append(hd["sel"](doc, jnp.zeros_like(doc), m0))
            deltas.append(jnp.sum(hd["sel"](prod, 0.0, m0), axis=1, keepdims=True))
        zs = [_dot(hd["qm"], hd["k"], NT) for hd in heads]
        dps = [_dot(dom, hd["v"], NT) for dom, hd in zip(doms, heads)]
        pbs, dscs = [], []
        for n, hd in enumerate(heads):
            p, psink = _swa_probs(zs[n], valid, hd["sink"])
            pbs.append(p.astype(BF16))
            dscs.append((p * (dps[n] - deltas[n]) * ATTN_SCALE).astype(BF16))
            dsink = jnp.sum(jnp.broadcast_to(-(psink * deltas[n]), (BLK, LANES)), axis=0, keepdims=True)
            ds_ref[0, n:n + 1, :] += dsink
        for c in range(SWA_Q_GROUPS):
            dq_ref[:, c * LANES:(c + 1) * LANES] = (_dot(dscs[2 * c], heads[2 * c]["km"], NN)
                                                    + _dot(dscs[2 * c + 1], heads[2 * c + 1]["km"], NN))
        acc = {}
        for n, hd in enumerate(heads):
            dk_n = _dot(dscs[n], hd["qm"], TN)
            dv_n = _dot(pbs[n], doms[n], TN)
            for key, val in ((("k", hd["same"]), dk_n), (("v", hd["same"]), dv_n)):
                acc[key] = val if key not in acc else acc[key] + val
        dkk = acc["k", True] + pltpu.roll(acc["k", False], HEAD_DIM, 1)
        dvv = acc["v", True] + pltpu.roll(acc["v", False], HEAD_DIM, 1)
        poff = pl.multiple_of(jnp.maximum(i - 1, 0) * BLK, BLK)
        coff = pl.multiple_of(i * BLK, BLK)
        dk_ref[pl.ds(poff, BLK), :] += dkk[:BLK]
        dv_ref[pl.ds(poff, BLK), :] += dvv[:BLK]
        dk_ref[pl.ds(coff, BLK), :] += dkk[BLK:]
        dv_ref[pl.ds(coff, BLK), :] += dvv[BLK:]

    prev = lambda i: jnp.maximum(i - 1, 0)
    qblk = pl.BlockSpec((BLK, qw), lambda m, i: (i, m))
    col_all = pl.BlockSpec((S, LANES), lambda m, i: (0, m))
    return pl.pallas_call(
        body, name=name,
        out_shape=(jax.ShapeDtypeStruct((S, D), F32),
                   jax.ShapeDtypeStruct((S, nkvp * LANES), F32),
                   jax.ShapeDtypeStruct((S, nkvp * LANES), F32),
                   jax.ShapeDtypeStruct((nkvp, nh, LANES), F32)),
        grid=(nkvp, nb),
        in_specs=[qblk,
                  pl.BlockSpec((BLK, LANES), lambda m, i: (i, m)),
                  pl.BlockSpec((BLK, LANES), lambda m, i: (prev(i), m)),
                  pl.BlockSpec((BLK, LANES), lambda m, i: (i, nkvp + m)),
                  pl.BlockSpec((BLK, LANES), lambda m, i: (prev(i), nkvp + m)),
                  pl.BlockSpec(memory_space=pltpu.SMEM),
                  qblk, qblk],
        out_specs=(qblk, col_all, col_all, pl.BlockSpec((1, nh, LANES), lambda m, i: (m, 0, 0))),
        compiler_params=_params("arbitrary", "arbitrary"),
    )(q, kv, kv, kv, kv, sinks, o, do)


def _dev_index(p):
    return 4 * p[0] + 2 * p[1] + p[2]


def _gather_plan(x_refs, out_refs, send_sems, recv_sems, local_sems):
    n = len(x_refs)
    x_, y_, c_ = lax.axis_index("x"), lax.axis_index("y"), lax.axis_index("c")
    me, sibling = (x_, y_, c_), (x_, y_, 1 - c_)
    chips = [(1 - x_, y_), (x_, 1 - y_), (1 - x_, 1 - y_)]

    def copy(t, k, block, to, src=None):
        dst = out_refs[t].at[_dev_index(block)]
        return pltpu.make_async_remote_copy(
            src_ref=dst if src is None else src, dst_ref=dst,
            send_sem=send_sems.at[7 * t + k], recv_sem=recv_sems.at[7 * t + k],
            device_id=to, device_id_type=MESH)

    mine = [pltpu.make_async_copy(x_refs[t], out_refs[t].at[_dev_index(me)], local_sems.at[t]) for t in range(n)]
    first = []
    for t in range(n):
        first.append(copy(t, 0, me, sibling, src=x_refs[t]))
        first += [copy(t, 1 + j, me, (*chip, c_), src=x_refs[t]) for j, chip in enumerate(chips)]
    arrived = lambda t, j: copy(t, 1 + j, (*chips[j], c_), me)
    forward = lambda t, j: copy(t, 4 + j, (*chips[j], c_), sibling)
    from_sibling = lambda t: copy(t, 0, sibling, me)
    forwarded = lambda t, j: copy(t, 4 + j, (*chips[j], 1 - c_), me)
    return n, mine, first, arrived, forward, from_sibling, forwarded


def _gather_start(x_refs, out_refs, send_sems, recv_sems, local_sems):
    _, mine, first, *_ = _gather_plan(x_refs, out_refs, send_sems, recv_sems, local_sems)
    for cp in mine + first:
        cp.start()


def _gather_finish(x_refs, out_refs, send_sems, recv_sems, local_sems):
    n, mine, first, arrived, forward, from_sibling, forwarded = _gather_plan(
        x_refs, out_refs, send_sems, recv_sems, local_sems)
    passed = []
    for j in range(3):
        for t in range(n):
            arrived(t, j).wait_recv()
            fwd = forward(t, j)
            fwd.start()
            passed.append(fwd)
    for t in range(n):
        from_sibling(t).wait_recv()
    for j in range(3):
        for t in range(n):
            forwarded(t, j).wait_recv()
    for cp in first + passed:
        cp.wait_send()
    for cp in mine:
        cp.wait()


def _scatter_plan(b_refs, out_refs, send_sems, recv_sems, local_sems):
    n = len(b_refs)
    x_, y_, c_ = lax.axis_index("x"), lax.axis_index("y"), lax.axis_index("c")
    my_idx = _dev_index((x_, y_, c_))
    mine = [pltpu.make_async_copy(b_refs[t].at[my_idx], out_refs[t].at[my_idx], local_sems.at[t]) for t in range(n)]
    copies = []
    for t in range(n):
        for k in range(1, N_DEV):
            peer = (x_ ^ ((k >> 2) & 1), y_ ^ ((k >> 1) & 1), c_ ^ (k & 1))
            copies.append(pltpu.make_async_remote_copy(
                src_ref=b_refs[t].at[_dev_index(peer)], dst_ref=out_refs[t].at[my_idx],
                send_sem=send_sems.at[7 * t + k - 1], recv_sem=recv_sems.at[7 * t + k - 1],
                device_id=peer, device_id_type=MESH))
    return mine, copies


def _scatter_start(b_refs, out_refs, send_sems, recv_sems, local_sems):
    mine, copies = _scatter_plan(b_refs, out_refs, send_sems, recv_sems, local_sems)
    for cp in mine + copies:
        cp.start()


def _scatter_finish(b_refs, out_refs, send_sems, recv_sems, local_sems):
    mine, copies = _scatter_plan(b_refs, out_refs, send_sems, recv_sems, local_sems)
    for cp in copies:
        cp.wait_recv()
    for cp in copies:
        cp.wait_send()
    for cp in mine:
        cp.wait()


def _exchange_operands(kind, tensors):
    if kind == "gather":
        args = list(tensors)
        shapes = [jax.ShapeDtypeStruct((N_DEV,) + t.shape, t.dtype) for t in tensors]
        return args, shapes, _gather_start, _gather_finish
    args = [t.reshape(N_DEV, t.shape[0] // N_DEV, t.shape[1]) for t in tensors]
    shapes = [jax.ShapeDtypeStruct(a.shape, a.dtype) for a in args]
    return args, shapes, _scatter_start, _scatter_finish


def _exchange_results(kind, tensors, res):
    if kind == "gather":
        return [r.reshape(N_DEV * t.shape[0], t.shape[1]) for r, t in zip(res, tensors)]
    return list(res)


def _exchange_sems(n):
    return [pltpu.SemaphoreType.DMA((7 * n,)), pltpu.SemaphoreType.DMA((7 * n,)), pltpu.SemaphoreType.DMA((n,))]


def _exchange(kind, tensors, name):
    n = len(tensors)
    args, shapes, start, finish = _exchange_operands(kind, tensors)

    def body(*refs):
        start(refs[:n], refs[n:2 * n], *refs[2 * n:])
        finish(refs[:n], refs[n:2 * n], *refs[2 * n:])

    hbm = pl.BlockSpec(memory_space=pl.ANY)
    res = pl.pallas_call(body, name=name, out_shape=shapes, in_specs=[hbm] * n, out_specs=[hbm] * n,
                         scratch_shapes=_exchange_sems(n))(*args)
    return _exchange_results(kind, tensors, res)


def _pcall(body, args, *, name, out_shape, grid, in_specs, out_specs, sem, scratch_shapes=(), carry=None):
    if carry is None:
        out = pl.pallas_call(body, name=name, out_shape=out_shape, grid=grid, in_specs=list(in_specs),
                             out_specs=out_specs, scratch_shapes=list(scratch_shapes),
                             compiler_params=_params(*sem))(*args)
        return out, None
    kind, tensors = carry
    multi = isinstance(out_shape, (tuple, list))
    shapes = list(out_shape) if multi else [out_shape]
    ospecs = list(out_specs) if multi else [out_specs]
    n_in, n_out, n_scr, n_c = len(in_specs), len(shapes), len(scratch_shapes), len(tensors)
    c_args, c_shapes, start, finish = _exchange_operands(kind, tensors)

    def wrapped(*refs):
        ins, rest = refs[:n_in], refs[n_in:]
        c_in, rest = rest[:n_c], rest[n_c:]
        outs, rest = rest[:n_out], rest[n_out:]
        c_out, rest = rest[:n_c], rest[n_c:]
        scr, sems = rest[:n_scr], rest[n_scr:]
        ids = [pl.program_id(a) for a in range(len(grid))]
        first, last = ids[0] == 0, ids[0] == grid[0] - 1
        for a in range(1, len(grid)):
            first = jnp.logical_and(first, ids[a] == 0)
            last = jnp.logical_and(last, ids[a] == grid[a] - 1)

        @pl.when(first)
        def _():
            start(c_in, c_out, *sems)

        body(*ins, *outs, *scr)

        @pl.when(last)
        def _():
            finish(c_in, c_out, *sems)

    hbm = pl.BlockSpec(memory_space=pl.ANY)
    res = pl.pallas_call(
        wrapped, name=name, out_shape=shapes + c_shapes, grid=grid,
        in_specs=list(in_specs) + [hbm] * n_c, out_specs=ospecs + [hbm] * n_c,
        scratch_shapes=list(scratch_shapes) + _exchange_sems(n_c),
        compiler_params=_params(*sem))(*args, *c_args)
    outs = tuple(res[:n_out]) if multi else res[0]
    return outs, _exchange_results(kind, tensors, res[n_out:])


def _sum8(parts, name):
    _, R, C = parts.shape
    tr = _tile(R, 256, 16)

    def body(p_ref, g_ref):
        g = p_ref[0].astype(F32)
        for s in range(1, N_DEV):
            g = g + p_ref[s].astype(F32)
        g_ref[...] = g

    return pl.pallas_call(
        body, name=name, out_shape=jax.ShapeDtypeStruct((R, C), F32),
        grid=(R // tr,),
        in_specs=[pl.BlockSpec((N_DEV, tr, C), lambda i: (0, i, 0))],
        out_specs=pl.BlockSpec((tr, C), lambda i: (i, 0)),
        compiler_params=_params("parallel"),
    )(parts)


def _adamw(g, w, m, v, name):
    R, C = g.shape
    tr = _tile(R, 256, 8)
    c1 = 1.0 - ADAM_B1 ** ADAM_STEP
    c2 = 1.0 - ADAM_B2 ** ADAM_STEP

    def body(g_ref, w_ref, m_ref, v_ref, d_ref, nm_ref, nv_ref):
        gg = g_ref[...]
        nm = ADAM_B1 * m_ref[...] + (1.0 - ADAM_B1) * gg
        nv = ADAM_B2 * v_ref[...] + (1.0 - ADAM_B2) * (gg * gg)
        m_hat = nm / c1
        v_hat = nv / c2
        nm_ref[...] = nm
        nv_ref[...] = nv
        d_ref[...] = -ADAM_LR * (m_hat / (jnp.sqrt(v_hat) + ADAM_EPS) + ADAM_WD * w_ref[...])

    row = pl.BlockSpec((tr, C), lambda i: (i, 0))
    shp = jax.ShapeDtypeStruct((R, C), F32)
    return pl.pallas_call(
        body, name=name, out_shape=(shp, shp, shp),
        grid=(R // tr,), in_specs=[row, row, row, row], out_specs=(row, row, row),
        compiler_params=_params("parallel"),
    )(g, w, m, v)


def _ffn_down(act, wo, h, tag):
    return _mm(act, wo, NN, F32, f"{tag}_down", scale=FFN_RES_SCALE, res=h, tm=512, tn=1024, tk=2816)


def _ffn_fwd(h, g, win_t, wo, tag):
    saved, _ = _ffn_up(h, g, win_t, f"{tag}_up")
    return _ffn_down(saved[-1], wo, h, tag), saved


def _ffn_bwd(dh, h, g, win_t, wo, saved, tag, scatter=False):
    xn, silu, dsilu, up, act = saved
    dgate, dup = _ffn_dact(dh, wo, silu, dsilu, up, f"{tag}_dact")
    dwo = _mm(act, dh, TN, BF16, f"{tag}_dwo", scale=FFN_RES_SCALE, tm=1408, tn=1024, tk=TN_CHUNK)
    dwin_t, got_wo = _dwin(dgate, dup, xn, f"{tag}_dwin", carry=("scatter", [dwo]) if scatter else None)
    (dh_in, dg), got_win = _dx_norm_bwd([(dgate, win_t, NN, 2, 0), (dup, win_t, NN, 2, 1)], h, g, dh, f"{tag}_dx",
                                        carry=("scatter", [dwin_t]) if scatter else None)
    if scatter:
        return dh_in, dg, got_win[0], got_wo[0]
    return dh_in, dg, dwin_t, dwo


def _proj(a, w, dims, out_dtype, name, res=None):
    return _mm(a, w, dims, out_dtype, name, res=res, tm=1024, tn=1024, tk=1024)


def _proj_dw(x, dy, name):
    return _mm(x, dy, TN, BF16, name, tm=1024, tn=1024, tk=TN_CHUNK)


def kernel(x, ffn1_norm, ffn1_w_in, ffn1_w_out, mix_norm, ffn2_norm, ffn2_w_in, ffn2_w_out, sb_w_qkv, sb_w_o, kv_norm, kv_w, swa_w_q, swa_sinks, swa_w_o, final_norm, loss_target, m_ffn1_norm, m_ffn1_w_in, m_ffn1_w_out, m_mix_norm, m_ffn2_norm, m_ffn2_w_in, m_ffn2_w_out, m_sb_w_qkv, m_sb_w_o, m_kv_norm, m_kv_w, m_swa_w_q, m_swa_sinks, m_swa_w_o, m_final_norm, v_ffn1_norm, v_ffn1_w_in, v_ffn1_w_out, v_mix_norm, v_ffn2_norm, v_ffn2_w_in, v_ffn2_w_out, v_sb_w_qkv, v_sb_w_o, v_kv_norm, v_kv_w, v_swa_w_q, v_swa_sinks, v_swa_w_o, v_final_norm):
    S, D = x.shape[1], x.shape[2]
    L = ffn1_w_in.shape[0]
    KV = kv_w.shape[1]
    assert L == 2 and swa_sinks.shape == (1, 2 * SWA_Q_GROUPS * KV // (2 * LANES))

    def bf(w):
        return w.astype(BF16)

    def bft(w):
        return jnp.transpose(w).astype(BF16)

    cos_t, sin_t = _rope_tables(S)
    h0 = x.reshape(S, D)
    tgt = loss_target.reshape(S, D)

    win1a_t, = _exchange("gather", [bft(ffn1_w_in[0])], "gather_first_weight")
    sv_a1, (wo1a, wqkv_t, w_sbo) = _ffn_up(
        h0, ffn1_norm[0], win1a_t, "ffn1a_up",
        carry=("gather", [bf(ffn1_w_out[0]), bft(sb_w_qkv[0]), bf(sb_w_o[0])]))
    h1 = _ffn_down(sv_a1[-1], wo1a, h0, "ffn1a")
    hn_a = _rmsnorm(h1, mix_norm[0], "mix_a_norm")
    qkv = _proj(hn_a, wqkv_t, NT, BF16, "sb_qkv")
    o_sb, later = _sb_fwd(qkv, "sb_attn", carry=("gather", [
        bft(ffn2_w_in[0]), bf(ffn2_w_out[0]), bf(kv_w), bft(ffn1_w_in[1]), bf(ffn1_w_out[1]),
        bf(swa_w_q[0]), bf(swa_w_o[0]), bft(ffn2_w_in[1]), bf(ffn2_w_out[1])]))
    win2a_t, wo2a, w_kv, win1b_t, wo1b, w_q, w_swo, win2b_t, wo2b = later
    h2 = _proj(o_sb, w_sbo, NN, F32, "sb_out", res=h1)
    h3, sv_a2 = _ffn_fwd(h2, ffn2_norm[0], win2a_t, wo2a, "ffn2a")
    kvn = _rmsnorm(h3, kv_norm, "kv_norm")
    kv_raw = _proj(kvn, w_kv, NN, F32, "kv_proj")
    kv_rot = _rotary(kv_raw, cos_t, sin_t, KV // (2 * LANES), False, "kv_rope")
    h4, sv_b1 = _ffn_fwd(h3, ffn1_norm[1], win1b_t, wo1b, "ffn1b")
    hn_b = _rmsnorm(h4, mix_norm[1], "mix_b_norm")
    q_raw = _proj(hn_b, w_q, NN, F32, "swa_q")
    q_rot = _rotary(q_raw, cos_t, sin_t, D // LANES, False, "q_rope")
    o_sw = _swa_fwd(q_rot, kv_rot, swa_sinks, "swa_attn")
    h5 = _proj(o_sw, w_swo, NN, F32, "swa_out", res=h4)
    h6, sv_b2 = _ffn_fwd(h5, ffn2_norm[1], win2b_t, wo2b, "ffn2b")
    dh6, dg_final, sq_err = _final_loss(h6, final_norm, tgt, "final_loss")
    loss = lax.psum(0.5 * jnp.sum(sq_err) / D, ("x", "y", "c"))

    dh5, dg_f2b, dwin2b_t, dwo2b = _ffn_bwd(dh6, h5, ffn2_norm[1], win2b_t, wo2b, sv_b2, "ffn2b")
    do_sw = _proj(dh5, w_swo, NT, BF16, "swa_out_dx")
    dw_swo = _proj_dw(o_sw, dh5, "swa_out_dw")
    dq_rot, dk_sw, dv_sw, dsink = _swa_bwd(q_rot, kv_rot, swa_sinks, o_sw, do_sw, "swa_attn_bwd")
    dq = _rotary(dq_rot, cos_t, sin_t, D // LANES, True, "q_rope_bwd")
    dw_q = _proj_dw(hn_b, dq, "swa_q_dw")
    (dh4, dg_mix_b), _ = _dx_norm_bwd([(dq, w_q, NT, 1, 0)], h4, mix_norm[1], dh5, "swa_q_dx")
    dh3, dg_f1b, dwin1b_t, dwo1b = _ffn_bwd(dh4, h3, ffn1_norm[1], win1b_t, wo1b, sv_b1, "ffn1b")
    dkv = _rotary(jnp.concatenate([dk_sw, dv_sw], axis=1), cos_t, sin_t, KV // (2 * LANES), True, "kv_rope_bwd")
    dw_kv = _proj_dw(kvn, dkv, "kv_proj_dw")
    (dh3, dg_kv), _ = _dx_norm_bwd([(dkv, w_kv, NT, 1, 0)], h3, kv_norm, dh3, "kv_proj_dx")
    dh2, dg_f2a, dwin2a_t, dwo2a = _ffn_bwd(dh3, h2, ffn2_norm[0], win2a_t, wo2a, sv_a2, "ffn2a")
    do_sb = _proj(dh2, w_sbo, NT, BF16, "sb_out_dx")
    dw_sbo = _proj_dw(o_sb, dh2, "sb_out_dw")
    (dq_sb, dk_sb, dv_sb), early = _sb_bwd(qkv, o_sb, do_sb, "sb_attn_bwd", carry=("scatter", [
        dwin2b_t, dwo2b, dw_swo, dw_q, dwin1b_t, dwo1b, dw_kv, dwin2a_t, dwo2a, dw_sbo]))
    dqkv = jnp.concatenate([dq_sb, dk_sb.astype(BF16), dv_sb.astype(BF16)], axis=1)
    dwqkv_t = _proj_dw(dqkv, hn_a, "sb_qkv_dw")
    (dh1, dg_mix_a), (p_qkv,) = _dx_norm_bwd([(dqkv, wqkv_t, NN, 1, 0)], h1, mix_norm[0], dh2, "sb_qkv_dx",
                                             carry=("scatter", [dwqkv_t]))
    dx, dg_f1a, p_win1a, p_wo1a = _ffn_bwd(dh1, h0, ffn1_norm[0], win1a_t, wo1a, sv_a1, "ffn1a", scatter=True)

    p_win2b, p_wo2b, p_swo, p_q, p_win1b, p_wo1b, p_kv, p_win2a, p_wo2a, p_sbo = early

    def natural(parts, tag):
        return _sum8(parts, f"sum_{tag}")

    def from_t(parts, tag):
        return jnp.transpose(_sum8(parts, f"sum_{tag}"))

    grads = {
        "ffn1_w_in": jnp.stack([from_t(p_win1a, "win1a"), from_t(p_win1b, "win1b")]),
        "ffn1_w_out": jnp.stack([natural(p_wo1a, "wo1a"), natural(p_wo1b, "wo1b")]),
        "ffn2_w_in": jnp.stack([from_t(p_win2a, "win2a"), from_t(p_win2b, "win2b")]),
        "ffn2_w_out": jnp.stack([natural(p_wo2a, "wo2a"), natural(p_wo2b, "wo2b")]),
        "sb_w_qkv": from_t(p_qkv, "qkv")[None],
        "sb_w_o": natural(p_sbo, "sbo")[None],
        "kv_w": natural(p_kv, "kv"),
        "swa_w_q": natural(p_q, "swq")[None],
        "swa_w_o": natural(p_swo, "swo")[None],
    }

    small_w = [ffn1_norm, mix_norm, ffn2_norm, kv_norm, final_norm, swa_sinks]
    small_m = [m_ffn1_norm, m_mix_norm, m_ffn2_norm, m_kv_norm, m_final_norm, m_swa_sinks]
    small_v = [v_ffn1_norm, v_mix_norm, v_ffn2_norm, v_kv_norm, v_final_norm, v_swa_sinks]
    SMALL_ROWS = 16

    def pack_small(ts):
        rows_ = [t.reshape(-1, D) for t in ts[:-1]]
        sink_row = jnp.pad(ts[-1].reshape(1, -1), ((0, 0), (0, D - ts[-1].size)))
        flat = jnp.concatenate(rows_ + [sink_row], axis=0)
        return jnp.pad(flat, ((0, SMALL_ROWS - flat.shape[0]), (0, 0)))

    def unpack_small(flat):
        out, r = [], 0
        for t in small_w[:-1]:
            n = t.size // D
            out.append(flat[r:r + n].reshape(t.shape))
            r += n
        out.append(flat[r, :swa_sinks.size].reshape(swa_sinks.shape))
        return out

    def gain(parts8):
        return jnp.sum(parts8, axis=0, keepdims=True)

    g_small_local = pack_small([
        jnp.concatenate([gain(dg_f1a), gain(dg_f1b)], axis=0),
        jnp.concatenate([gain(dg_mix_a), gain(dg_mix_b)], axis=0),
        jnp.concatenate([gain(dg_f2a), gain(dg_f2b)], axis=0),
        gain(dg_kv), gain(dg_final), dsink[:, :, 0].reshape(1, -1)])
    small_parts = _exchange("gather", [g_small_local], "gather_small_grads")[0]
    g_small = _sum8(small_parts.reshape(N_DEV, SMALL_ROWS, D), "sum_small")
    d_small, nm_small, nv_small = _adamw(g_small, pack_small(small_w), pack_small(small_m), pack_small(small_v), "adamw_small")
    small_names = ["ffn1_norm", "mix_norm", "ffn2_norm", "kv_norm", "final_norm", "swa_sinks"]
    result = {"grad": dict(zip(small_names, unpack_small(g_small))),
              "delta": dict(zip(small_names, unpack_small(d_small))),
              "new_m": dict(zip(small_names, unpack_small(nm_small))),
              "new_v": dict(zip(small_names, unpack_small(nv_small)))}

    big = {"ffn1_w_in": (ffn1_w_in, m_ffn1_w_in, v_ffn1_w_in), "ffn1_w_out": (ffn1_w_out, m_ffn1_w_out, v_ffn1_w_out),
           "ffn2_w_in": (ffn2_w_in, m_ffn2_w_in, v_ffn2_w_in), "ffn2_w_out": (ffn2_w_out, m_ffn2_w_out, v_ffn2_w_out),
           "sb_w_qkv": (sb_w_qkv, m_sb_w_qkv, v_sb_w_qkv), "sb_w_o": (sb_w_o, m_sb_w_o, v_sb_w_o),
           "kv_w": (kv_w, m_kv_w, v_kv_w), "swa_w_q": (swa_w_q, m_swa_w_q, v_swa_w_q),
           "swa_w_o": (swa_w_o, m_swa_w_o, v_swa_w_o)}
    for nm, (w, m, v) in big.items():
        g = grads[nm]
        two_d = lambda t: t.reshape(-1, t.shape[-1])
        d, new_m, new_v = _adamw(two_d(g), two_d(w), two_d(m), two_d(v), f"adamw_{nm}")
        result["grad"][nm] = g
        result["delta"][nm] = d.reshape(w.shape)
        result["new_m"][nm] = new_m.reshape(w.shape)
        result["new_v"][nm] = new_v.reshape(w.shape)

    order = ["ffn1_norm", "ffn1_w_in", "ffn1_w_out", "mix_norm", "ffn2_norm", "ffn2_w_in", "ffn2_w_out",
             "sb_w_qkv", "sb_w_o", "kv_norm", "kv_w", "swa_w_q", "swa_sinks", "swa_w_o", "final_norm"]
    outs = [result[kind][nm] for kind in ("grad", "delta", "new_m", "new_v") for nm in order]
    return (loss, dx.reshape(x.shape), *outs)
```

```python
import jax
import jax.numpy as jnp
from jax import lax
from jax.experimental import pallas as pl
from jax.experimental.pallas import tpu as pltpu

F32 = jnp.float32
BF16 = jnp.bfloat16

N_DEV = 8
HEAD_DIM = 64
LANES = 128
BLK = 128
RMS_EPS = 1e-6
FFN_RES_SCALE = 0.5
ROPE_THETA = 10000.0
ATTN_SCALE = HEAD_DIM ** -0.5
SB_LOG_FLOOR = -88.0
NEG_BIG = -1e30
VMEM_LIMIT_V7X = 56 * 1024 * 1024

ADAM_LR = 0.001
ADAM_B1 = 0.9
ADAM_B2 = 0.999
ADAM_EPS = 1e-08
ADAM_WD = 0.01
ADAM_STEP = 10

NN = ((1,), (0,))
NT = ((1,), (1,))
TN = ((0,), (0,))
TN_CHUNK = 2048
MESH = pl.DeviceIdType.MESH


def _dot(a, b, dims):
    return lax.dot_general(a, b, (dims, ((), ())), preferred_element_type=F32)


def _tile(n, pref, mult=LANES):
    if n <= pref:
        return n
    t = (pref // mult) * mult
    while t >= mult:
        if n % t == 0:
            return t
        t -= mult
    return n


def _params(*sem):
    return pltpu.CompilerParams(dimension_semantics=sem, vmem_limit_bytes=VMEM_LIMIT_V7X)


def _mm(a, b, dims, out_dtype, name, scale=1.0, res=None, tm=512, tn=512, tk=512):
    if dims == NN:
        (M, K), (_, N) = a.shape, b.shape
    elif dims == NT:
        (M, K), (N, _) = a.shape, b.shape
    else:
        (K, M), (_, N) = a.shape, b.shape
    tm, tn, tk = _tile(M, tm), _tile(N, tn), _tile(K, tk)
    nk = K // tk
    if dims == TN:
        a_spec = pl.BlockSpec((tk, tm), lambda i, j, k: (k, i))
    else:
        a_spec = pl.BlockSpec((tm, tk), lambda i, j, k: (i, k))
    if dims == NT:
        b_spec = pl.BlockSpec((tn, tk), lambda i, j, k: (j, k))
    else:
        b_spec = pl.BlockSpec((tk, tn), lambda i, j, k: (k, j))
    o_spec = pl.BlockSpec((tm, tn), lambda i, j, k: (i, j))
    has_res = res is not None

    def body(*refs):
        a_ref, b_ref = refs[0], refs[1]
        r_ref = refs[2] if has_res else None
        o_ref = refs[3] if has_res else refs[2]

        def finish(acc):
            r = acc * scale if scale != 1.0 else acc
            if has_res:
                r = r + r_ref[...]
            o_ref[...] = r.astype(out_dtype)

        p = _dot(a_ref[...].astype(BF16), b_ref[...].astype(BF16), dims)
        if nk == 1:
            finish(p)
        else:
            acc_ref = refs[-1]
            k = pl.program_id(2)

            @pl.when(k == 0)
            def _():
                acc_ref[...] = p

            @pl.when(k > 0)
            def _():
                acc_ref[...] += p

            @pl.when(k == nk - 1)
            def _():
                finish(acc_ref[...])

    in_specs = [a_spec, b_spec] + ([o_spec] if has_res else [])
    args = (a, b) + ((res,) if has_res else ())
    return pl.pallas_call(
        body, name=name,
        out_shape=jax.ShapeDtypeStruct((M, N), out_dtype),
        grid=(M // tm, N // tn, nk),
        in_specs=in_specs, out_specs=o_spec,
        scratch_shapes=[pltpu.VMEM((tm, tn), F32)] if nk > 1 else [],
        compiler_params=_params("parallel", "parallel", "arbitrary"),
    )(*args)


def _rows8(x):
    r, d = x.shape
    return jnp.sum(x.reshape(r // 8, 8, d), axis=0)


def _rmsnorm(h, g, name):
    S, D = h.shape
    ts = _tile(S, 512, 8)

    def body(h_ref, g_ref, o_ref):
        x = h_ref[...]
        r = lax.rsqrt(jnp.mean(x * x, axis=-1, keepdims=True) + RMS_EPS)
        o_ref[...] = ((x * r) * g_ref[...]).astype(BF16)

    return pl.pallas_call(
        body, name=name,
        out_shape=jax.ShapeDtypeStruct((S, D), BF16),
        grid=(S // ts,),
        in_specs=[pl.BlockSpec((ts, D), lambda i: (i, 0)), pl.BlockSpec((1, D), lambda i: (0, 0))],
        out_specs=pl.BlockSpec((ts, D), lambda i: (i, 0)),
        compiler_params=_params("parallel"),
    )(h, g.reshape(1, D))


def _final_loss(h, g, tgt, name):
    S, D = h.shape
    ts = _tile(S, 512, 8)

    def body(h_ref, g_ref, t_ref, dh_ref, dg_ref, l_ref):
        x = h_ref[...]
        r = lax.rsqrt(jnp.mean(x * x, axis=-1, keepdims=True) + RMS_EPS)
        xhat = x * r
        err = xhat * g_ref[...] - t_ref[...]
        d = err * (1.0 / D)
        dxh = d * g_ref[...]
        c = jnp.mean(dxh * xhat, axis=-1, keepdims=True)
        dh_ref[...] = r * (dxh - xhat * c)
        part = _rows8(d * xhat)
        lpart = _rows8(err * err)

        @pl.when(pl.program_id(0) == 0)
        def _():
            dg_ref[...] = part
            l_ref[...] = lpart

        @pl.when(pl.program_id(0) > 0)
        def _():
            dg_ref[...] += part
            l_ref[...] += lpart

    row = pl.BlockSpec((ts, D), lambda i: (i, 0))
    acc = pl.BlockSpec((8, D), lambda i: (0, 0))
    return pl.pallas_call(
        body, name=name,
        out_shape=(jax.ShapeDtypeStruct((S, D), F32), jax.ShapeDtypeStruct((8, D), F32),
                   jax.ShapeDtypeStruct((8, D), F32)),
        grid=(S // ts,),
        in_specs=[row, pl.BlockSpec((1, D), lambda i: (0, 0)), row],
        out_specs=(row, acc, acc),
        compiler_params=_params("arbitrary"),
    )(h, g.reshape(1, D), tgt)


def _ffn_up(h, g, win_t, name, carry=None):
    S, D = h.shape
    F = win_t.shape[0] // 2
    tm, tn = _tile(S, 512, 16), _tile(F, 1408)
    nf = F // tn

    def body(h_ref, g_ref, wg_ref, wu_ref, xn_ref, silu_ref, dsilu_ref, up_ref, act_ref):
        x = h_ref[...]
        r = lax.rsqrt(jnp.mean(x * x, axis=-1, keepdims=True) + RMS_EPS)
        xn = ((x * r) * g_ref[...]).astype(BF16)
        xn_ref[...] = xn
        gate = _dot(xn, wg_ref[...], NT)
        up = _dot(xn, wu_ref[...], NT)
        sig = 1.0 / (1.0 + jnp.exp(-gate))
        silu = gate * sig
        up_ref[...] = up.astype(BF16)
        silu_ref[...] = silu.astype(BF16)
        dsilu_ref[...] = (sig + silu * (1.0 - sig)).astype(BF16)
        act_ref[...] = (silu * up).astype(BF16)

    row = pl.BlockSpec((tm, D), lambda i, j: (i, 0))
    blk = pl.BlockSpec((tm, tn), lambda i, j: (i, j))
    hid = jax.ShapeDtypeStruct((S, F), BF16)
    return _pcall(
        body, (h, g.reshape(1, D), win_t, win_t), name=name,
        out_shape=(jax.ShapeDtypeStruct((S, D), BF16), hid, hid, hid, hid),
        grid=(S // tm, nf),
        in_specs=[row, pl.BlockSpec((1, D), lambda i, j: (0, 0)),
                  pl.BlockSpec((tn, D), lambda i, j: (j, 0)),
                  pl.BlockSpec((tn, D), lambda i, j: (j + nf, 0))],
        out_specs=(row, blk, blk, blk, blk),
        sem=("arbitrary", "arbitrary"), carry=carry)


def _ffn_dact(dh, wo, silu, dsilu, up, name):
    S, D = dh.shape
    F = wo.shape[0]
    tm, tn = _tile(S, 512, 16), _tile(F, 1408)

    def body(dh_ref, wo_ref, s_ref, ds_ref, u_ref, dg_ref, du_ref):
        d = _dot(dh_ref[...].astype(BF16), wo_ref[...], NT) * FFN_RES_SCALE
        du_ref[...] = (d * s_ref[...].astype(F32)).astype(BF16)
        dg_ref[...] = (d * u_ref[...].astype(F32) * ds_ref[...].astype(F32)).astype(BF16)

    blk = pl.BlockSpec((tm, tn), lambda j, i: (i, j))
    hid = jax.ShapeDtypeStruct((S, F), BF16)
    return pl.pallas_call(
        body, name=name, out_shape=(hid, hid),
        grid=(F // tn, S // tm),
        in_specs=[pl.BlockSpec((tm, D), lambda j, i: (i, 0)), pl.BlockSpec((tn, D), lambda j, i: (j, 0)),
                  blk, blk, blk],
        out_specs=(blk, blk),
        compiler_params=_params("arbitrary", "arbitrary"),
    )(dh, wo, silu, dsilu, up)


def _dwin(dgate, dup, xn, name, carry=None):
    S, F = dgate.shape
    D = xn.shape[1]
    tr, tk = _tile(F, 1408), _tile(S, TN_CHUNK, 16)
    nf, nk = F // tr, S // tk

    def body(dg_ref, du_ref, x_ref, o_ref, acc_ref):
        r, k = pl.program_id(0), pl.program_id(1)

        def accumulate(a_ref):
            p = _dot(a_ref[...], x_ref[...], TN)

            @pl.when(k == 0)
            def _():
                acc_ref[...] = p

            @pl.when(k > 0)
            def _():
                acc_ref[...] += p

        @pl.when(r < nf)
        def _():
            accumulate(dg_ref)

        @pl.when(r >= nf)
        def _():
            accumulate(du_ref)

        @pl.when(k == nk - 1)
        def _():
            o_ref[...] = acc_ref[...].astype(BF16)

    return _pcall(
        body, (dgate, dup, xn), name=name, out_shape=jax.ShapeDtypeStruct((2 * F, D), BF16),
        grid=(2 * nf, nk),
        in_specs=[pl.BlockSpec((tk, tr), lambda r, k: (jnp.where(r < nf, k, 0), jnp.minimum(r, nf - 1))),
                  pl.BlockSpec((tk, tr), lambda r, k: (jnp.where(r >= nf, k, 0), jnp.maximum(r - nf, 0))),
                  pl.BlockSpec((tk, D), lambda r, k: (k, 0))],
        out_specs=pl.BlockSpec((tr, D), lambda r, k: (r, 0)),
        scratch_shapes=[pltpu.VMEM((tr, D), F32)],
        sem=("arbitrary", "arbitrary"), carry=carry)


def _dx_norm_bwd(terms, h, g, res, name, carry=None):
    S, D = h.shape
    tm = _tile(S, 256, 16)
    n = len(terms)

    def body(*refs):
        dy_refs, w_refs = refs[:n], refs[n:2 * n]
        h_ref, g_ref, r_ref, dh_ref, dg_ref = refs[2 * n:]
        d = _dot(dy_refs[0][...], w_refs[0][...], terms[0][2])
        for t in range(1, n):
            d = d + _dot(dy_refs[t][...], w_refs[t][...], terms[t][2])
        x = h_ref[...]
        r = lax.rsqrt(jnp.mean(x * x, axis=-1, keepdims=True) + RMS_EPS)
        xhat = x * r
        dxh = d * g_ref[...]
        c = jnp.mean(dxh * xhat, axis=-1, keepdims=True)
        dh_ref[...] = r * (dxh - xhat * c) + r_ref[...]
        part = _rows8(d * xhat)

        @pl.when(pl.program_id(0) == 0)
        def _():
            dg_ref[...] = part

        @pl.when(pl.program_id(0) > 0)
        def _():
            dg_ref[...] += part

    def w_spec(w, nblk, blk):
        return pl.BlockSpec((w.shape[0] // nblk, w.shape[1]), lambda i: (blk, 0))

    row = pl.BlockSpec((tm, D), lambda i: (i, 0))
    in_specs = [pl.BlockSpec((tm, t[0].shape[1]), lambda i: (i, 0)) for t in terms]
    in_specs += [w_spec(t[1], t[3], t[4]) for t in terms]
    in_specs += [row, pl.BlockSpec((1, D), lambda i: (0, 0)), row]
    return _pcall(
        body, (*[t[0] for t in terms], *[t[1] for t in terms], h, g.reshape(1, D), res), name=name,
        out_shape=(jax.ShapeDtypeStruct((S, D), F32), jax.ShapeDtypeStruct((8, D), F32)),
        grid=(S // tm,),
        in_specs=in_specs,
        out_specs=(row, pl.BlockSpec((8, D), lambda i: (0, 0))),
        sem=("arbitrary",), carry=carry)


def _load_resident(pairs, sems):
    @pl.when(pl.program_id(0) == 0)
    def _():
        copies = [pltpu.make_async_copy(src, dst, sems.at[n]) for n, (src, dst) in enumerate(pairs)]
        for cp in copies:
            cp.start()
        for cp in copies:
            cp.wait()


def _ffn_fwd_fused(h, g, win_t, wo, name):
    S, D = h.shape
    F = wo.shape[0]
    tm = _tile(S, 256, 16)

    def body(h_ref, g_ref, win_hbm, wo_hbm, out_ref, xn_ref, silu_ref, dsilu_ref, up_ref, act_ref, win_v, wo_v, sems):
        _load_resident([(win_hbm, win_v), (wo_hbm, wo_v)], sems)
        x = h_ref[...]
        r = lax.rsqrt(jnp.mean(x * x, axis=-1, keepdims=True) + RMS_EPS)
        xn = ((x * r) * g_ref[...]).astype(BF16)
        xn_ref[...] = xn
        gate = _dot(xn, win_v[:F, :], NT)
        up = _dot(xn, win_v[F:, :], NT)
        sig = 1.0 / (1.0 + jnp.exp(-gate))
        silu = gate * sig
        act = (silu * up).astype(BF16)
        up_ref[...] = up.astype(BF16)
        silu_ref[...] = silu.astype(BF16)
        dsilu_ref[...] = (sig + silu * (1.0 - sig)).astype(BF16)
        act_ref[...] = act
        out_ref[...] = x + FFN_RES_SCALE * _dot(act, wo_v[...], NN)

    row = pl.BlockSpec((tm, D), lambda i: (i, 0))
    wide = pl.BlockSpec((tm, F), lambda i: (i, 0))
    hbm = pl.BlockSpec(memory_space=pl.ANY)
    hid = jax.ShapeDtypeStruct((S, F), BF16)
    res = pl.pallas_call(
        body, name=name,
        out_shape=(jax.ShapeDtypeStruct((S, D), F32), jax.ShapeDtypeStruct((S, D), BF16), hid, hid, hid, hid),
        grid=(S // tm,),
        in_specs=[row, pl.BlockSpec((1, D), lambda i: (0, 0)), hbm, hbm],
        out_specs=(row, row, wide, wide, wide, wide),
        scratch_shapes=[pltpu.VMEM(win_t.shape, BF16), pltpu.VMEM(wo.shape, BF16), pltpu.SemaphoreType.DMA((2,))],
        compiler_params=_params("arbitrary"),
    )(h, g.reshape(1, D), win_t, wo)
    return res[0], tuple(res[1:])


def _ffn_bwd_fused(dh, h, g, win_t, wo, silu, dsilu, up, name):
    S, D = h.shape
    F = wo.shape[0]
    tm = _tile(S, 256, 16)

    def body(dh_ref, h_ref, g_ref, s_ref, ds_ref, u_ref, win_hbm, wo_hbm,
             dhin_ref, dgain_ref, dgate_ref, dup_ref, win_v, wo_v, sems):
        _load_resident([(win_hbm, win_v), (wo_hbm, wo_v)], sems)
        dhv = dh_ref[...]
        d = _dot(dhv.astype(BF16), wo_v[...], NT) * FFN_RES_SCALE
        dup = (d * s_ref[...].astype(F32)).astype(BF16)
        dgate = (d * u_ref[...].astype(F32) * ds_ref[...].astype(F32)).astype(BF16)
        dup_ref[...] = dup
        dgate_ref[...] = dgate
        dxn = _dot(dgate, win_v[:F, :], NN) + _dot(dup, win_v[F:, :], NN)
        x = h_ref[...]
        r = lax.rsqrt(jnp.mean(x * x, axis=-1, keepdims=True) + RMS_EPS)
        xhat = x * r
        dxh = dxn * g_ref[...]
        c = jnp.mean(dxh * xhat, axis=-1, keepdims=True)
        dhin_ref[...] = r * (dxh - xhat * c) + dhv
        part = _rows8(dxn * xhat)

        @pl.when(pl.program_id(0) == 0)
        def _():
            dgain_ref[...] = part

        @pl.when(pl.program_id(0) > 0)
        def _():
            dgain_ref[...] += part

    row = pl.BlockSpec((tm, D), lambda i: (i, 0))
    wide = pl.BlockSpec((tm, F), lambda i: (i, 0))
    hbm = pl.BlockSpec(memory_space=pl.ANY)
    hid = jax.ShapeDtypeStruct((S, F), BF16)
    return pl.pallas_call(
        body, name=name,
        out_shape=(jax.ShapeDtypeStruct((S, D), F32), jax.ShapeDtypeStruct((8, D), F32), hid, hid),
        grid=(S // tm,),
        in_specs=[row, row, pl.BlockSpec((1, D), lambda i: (0, 0)), wide, wide, wide, hbm, hbm],
        out_specs=(row, pl.BlockSpec((8, D), lambda i: (0, 0)), wide, wide),
        scratch_shapes=[pltpu.VMEM(win_t.shape, BF16), pltpu.VMEM(wo.shape, BF16), pltpu.SemaphoreType.DMA((2,))],
        compiler_params=_params("arbitrary"),
    )(dh, h, g.reshape(1, D), silu, dsilu, up, win_t, wo)


def _rope_tables(S):
    half = HEAD_DIM // 2
    inv_freq = ROPE_THETA ** (-jnp.arange(half, dtype=F32) / half)
    ang = jnp.arange(S).astype(F32)[:, None] * inv_freq[None, :]
    cos, sin = jnp.cos(ang), jnp.sin(ang)
    cos_t = jnp.tile(cos, (1, LANES // half))
    sin_t = jnp.tile(jnp.concatenate([-sin, sin], axis=1), (1, LANES // HEAD_DIM))
    return cos_t, sin_t


def _swap_halves(x):
    lane = lax.broadcasted_iota(jnp.int32, x.shape, 1)
    first = (lane % HEAD_DIM) < (HEAD_DIM // 2)
    return jnp.where(first, pltpu.roll(x, LANES - HEAD_DIM // 2, 1), pltpu.roll(x, HEAD_DIM // 2, 1))


def _rotary(x, cos_t, sin_t, n_rot, inverse, name):
    S, C = x.shape
    ts = _tile(S, 512, 16)
    ng = C // LANES

    def body(x_ref, c_ref, s_ref, o_ref):
        cs, sn = c_ref[...], s_ref[...]
        for gidx in range(ng):
            sl = slice(gidx * LANES, (gidx + 1) * LANES)
            v = x_ref[:, sl].astype(F32)
            if gidx < n_rot:
                if inverse:
                    v = v * cs + _swap_halves(v * sn)
                else:
                    v = v * cs + _swap_halves(v) * sn
            o_ref[:, sl] = v.astype(BF16)

    row = pl.BlockSpec((ts, C), lambda i: (i, 0))
    tab = pl.BlockSpec((ts, LANES), lambda i: (i, 0))
    return pl.pallas_call(
        body, name=name, out_shape=jax.ShapeDtypeStruct((S, C), BF16),
        grid=(S // ts,), in_specs=[row, tab, tab], out_specs=row,
        compiler_params=_params("parallel"),
    )(x, cos_t, sin_t)


def _head_masks():
    lane = lax.broadcasted_iota(jnp.int32, (BLK, LANES), 1)
    return lane < HEAD_DIM


def _split_bf16(x):
    hi = x.astype(BF16)
    lo = (x - hi.astype(F32)).astype(BF16)
    return hi, lo


def _sb_scores(qh, ks, carry, diag, tri_excl, strict):
    n_heads = len(qh)
    zs = [_dot(qh[n], ks[n], NT) for n in range(n_heads)]
    a_l, b_l, split_l = [], [], []
    for z in zs:
        a = jnp.minimum(z, 0.0) - jnp.log(1.0 + jnp.exp(-jnp.abs(z)))
        b = a - z
        if diag:
            b = jnp.where(strict, b, 0.0)
        a_l.append(a)
        b_l.append(b)
        split_l.append(_split_bf16(b))
    sufs = [_dot(hi, tri_excl, NN) + _dot(lo, tri_excl, NN) for hi, lo in split_l]
    w_l = []
    for n in range(n_heads):
        w = jnp.exp(a_l[n] + sufs[n] + carry[n])
        if diag:
            w = jnp.where(strict, w, 0.0)
        w_l.append(w)
    return a_l, b_l, w_l


SB_FWD_PAIRS = 4
SB_BWD_PAIRS = 2
SB_BWD_QBLOCKS = 2


def _any_alive(carries):
    top = carries[0]
    for c in carries[1:]:
        top = jnp.maximum(top, c)
    return (jnp.max(top) > SB_LOG_FLOOR).astype(jnp.int32)


def _sb_fwd(qkv, name, carry=None):
    S, D3 = qkv.shape
    D = D3 // 3
    npair, nb = D // LANES, S // BLK
    P = min(SB_FWD_PAIRS, npair)
    ngroup = npair // P
    W = P * LANES

    def body(q_ref, k_ref, v_ref, o_ref):
        i = pl.program_id(1)
        m0 = _head_masks()
        row = lax.broadcasted_iota(jnp.int32, (BLK, BLK), 0)
        col = lax.broadcasted_iota(jnp.int32, (BLK, BLK), 1)
        strict = col < row
        tri_excl = jnp.where(row > col, 1.0, 0.0).astype(BF16)
        zq = jnp.zeros((BLK, LANES), BF16)
        qh = []
        for p in range(P):
            q2 = q_ref[:, p * LANES:(p + 1) * LANES] * ATTN_SCALE
            qh += [jnp.where(m0, q2, zq), jnp.where(m0, zq, q2)]

        def block(j, carry, acc, diag):
            off = pl.multiple_of(j * BLK, BLK)
            ks = [k_ref[pl.ds(off, BLK), p * LANES:(p + 1) * LANES] for p in range(P)]
            vh = []
            for p in range(P):
                v2 = v_ref[pl.ds(off, BLK), p * LANES:(p + 1) * LANES]
                vh += [jnp.where(m0, v2, zq), jnp.where(m0, zq, v2)]
            _, b_l, w_l = _sb_scores(qh, [ks[n // 2] for n in range(2 * P)], carry, diag, tri_excl, strict)
            wb = [w.astype(BF16) for w in w_l]
            new_acc = [acc[p] + _dot(wb[2 * p], vh[2 * p], NN) + _dot(wb[2 * p + 1], vh[2 * p + 1], NN)
                       for p in range(P)]
            new_carry = [carry[n] + jnp.sum(b_l[n], axis=1, keepdims=True) for n in range(2 * P)]
            return new_carry, new_acc

        c0 = jnp.zeros((BLK, 1), F32)
        carry, acc = block(i, [c0] * (2 * P), [jnp.zeros((BLK, LANES), F32)] * P, True)

        def cond(st):
            return jnp.logical_and(st[0] >= 0, st[1] > 0)

        def step(st):
            j, _, carry, acc = st
            carry, acc = block(j, carry, acc, False)
            return j - 1, _any_alive(carry), carry, acc

        st = lax.while_loop(cond, step, (i - 1, _any_alive(carry), carry, acc))
        for p in range(P):
            o_ref[:, p * LANES:(p + 1) * LANES] = st[3][p]

    return _pcall(
        body, (qkv, qkv, qkv), name=name, out_shape=jax.ShapeDtypeStruct((S, D), F32),
        grid=(ngroup, nb),
        in_specs=[pl.BlockSpec((BLK, W), lambda g, i: (i, g)),
                  pl.BlockSpec((S, W), lambda g, i: (0, ngroup + g)),
                  pl.BlockSpec((S, W), lambda g, i: (0, 2 * ngroup + g))],
        out_specs=pl.BlockSpec((BLK, W), lambda g, i: (i, g)),
        sem=("arbitrary", "arbitrary"), carry=carry)


def _sb_bwd(qkv, o, do, name, carry=None):
    S, D3 = qkv.shape
    D = D3 // 3
    npair, nb = D // LANES, S // BLK
    P = min(SB_BWD_PAIRS, npair)
    ngroup = npair // P
    W = P * LANES

    QB = SB_BWD_QBLOCKS if nb % SB_BWD_QBLOCKS == 0 else 1
    nch = QB * 2 * P

    def body(q_ref, k_ref, v_ref, o_ref, do_ref, dq_ref, dk_ref, dv_ref):
        i_first = pl.program_id(1) * QB
        m0 = _head_masks()
        row = lax.broadcasted_iota(jnp.int32, (BLK, BLK), 0)
        col = lax.broadcasted_iota(jnp.int32, (BLK, BLK), 1)
        strict = col < row
        tri_excl = jnp.where(row > col, 1.0, 0.0).astype(BF16)
        tri_incl = jnp.where(row >= col, 1.0, 0.0).astype(BF16)
        zq = jnp.zeros((BLK, LANES), BF16)
        lanes = [slice(p * LANES, (p + 1) * LANES) for p in range(P)]
        qh, doh, delta = [], [], []
        for qb in range(QB):
            rs = slice(qb * BLK, (qb + 1) * BLK)
            for sl in lanes:
                q2, do2 = q_ref[rs, sl] * ATTN_SCALE, do_ref[rs, sl]
                qh += [jnp.where(m0, q2, zq), jnp.where(m0, zq, q2)]
                doh += [jnp.where(m0, do2, zq), jnp.where(m0, zq, do2)]
                prod = do2.astype(F32) * o_ref[rs, sl]
                delta += [jnp.sum(jnp.where(m0, prod, 0.0), axis=1, keepdims=True),
                          jnp.sum(jnp.where(m0, 0.0, prod), axis=1, keepdims=True)]

        @pl.when(pl.program_id(1) == 0)
        def _():
            dk_ref[...] = jnp.zeros_like(dk_ref)
            dv_ref[...] = jnp.zeros_like(dv_ref)

        def block(js, valid, cb, cg, dq, diag):
            offs = [pl.multiple_of(j * BLK, BLK) for j in js]
            ks, vs, kh = [], [], []
            for qb in range(QB):
                for sl in lanes:
                    k2, v2 = k_ref[pl.ds(offs[qb], BLK), sl], v_ref[pl.ds(offs[qb], BLK), sl]
                    ks += [k2, k2]
                    vs += [v2, v2]
                    k2s = k2 * ATTN_SCALE
                    kh += [jnp.where(m0, k2s, zq), jnp.where(m0, zq, k2s)]
            dws = [_dot(doh[n], vs[n], NT) for n in range(nch)]
            a_l, b_l, w_l = _sb_scores(qh, ks, cb, diag, tri_excl, strict)
            wb = [w.astype(BF16) for w in w_l]
            g_l = [dws[n] * wb[n].astype(F32) for n in range(nch)]
            gsplit = [_split_bf16(g) for g in g_l]
            gincs = [_dot(hi, tri_incl, NN) + _dot(lo, tri_incl, NN) for hi, lo in gsplit]
            dzs = []
            for n in range(nch):
                beta = jnp.exp(a_l[n])
                dz = g_l[n] - beta * (g_l[n] + ((delta[n] - cg[n]) - gincs[n]))
                if diag:
                    dz = jnp.where(strict, dz, 0.0)
                if valid[n // (2 * P)] is not None:
                    dz = jnp.where(valid[n // (2 * P)], dz, 0.0)
                dzs.append(dz.astype(BF16))
            ndq = []
            for qb in range(QB):
                for p, sl in enumerate(lanes):
                    n0 = qb * 2 * P + 2 * p
                    ndq.append(dq[qb * P + p] + _dot(dzs[n0], kh[n0], NN) + _dot(dzs[n0 + 1], kh[n0 + 1], NN))
                    dk_ref[pl.ds(offs[qb], BLK), sl] += _dot(dzs[n0], qh[n0], TN) + _dot(dzs[n0 + 1], qh[n0 + 1], TN)
                    dv_ref[pl.ds(offs[qb], BLK), sl] += _dot(wb[n0], doh[n0], TN) + _dot(wb[n0 + 1], doh[n0 + 1], TN)
            ncb = [cb[n] + jnp.sum(b_l[n], axis=1, keepdims=True) for n in range(nch)]
            ncg = [cg[n] + jnp.sum(g_l[n], axis=1, keepdims=True) for n in range(nch)]
            return ncb, ncg, ndq

        c0 = jnp.zeros((BLK, 1), F32)
        cb, cg, dq = block([i_first + qb for qb in range(QB)], [None] * QB, [c0] * nch, [c0] * nch,
                           [jnp.zeros((BLK, LANES), F32)] * (QB * P), True)

        def cond(st):
            return jnp.logical_and(i_first + QB - 1 - st[0] >= 0, st[1] > 0)

        def step(st):
            t, _, cb, cg, dq = st
            js = [i_first + qb - t for qb in range(QB)]
            valid = [js[qb] >= 0 for qb in range(QB - 1)] + [None]
            cb = [cb[n] if valid[n // (2 * P)] is None else jnp.where(valid[n // (2 * P)], cb[n], NEG_BIG)
                  for n in range(nch)]
            cb, cg, dq = block([jnp.maximum(j, 0) for j in js], valid, cb, cg, dq, False)
            return t + 1, _any_alive(cb), cb, cg, dq

        st = lax.while_loop(cond, step, (1, _any_alive(cb), cb, cg, dq))
        for qb in range(QB):
            for p, sl in enumerate(lanes):
                dq_ref[qb * BLK:(qb + 1) * BLK, sl] = st[4][qb * P + p].astype(BF16)

    blk = lambda c: pl.BlockSpec((QB * BLK, W), lambda g, i: (i, c * ngroup + g))
    col_all = lambda c: pl.BlockSpec((S, W), lambda g, i: (0, c * ngroup + g))
    return _pcall(
        body, (qkv, qkv, qkv, o, do), name=name,
        out_shape=(jax.ShapeDtypeStruct((S, D), BF16), jax.ShapeDtypeStruct((S, D), F32),
                   jax.ShapeDtypeStruct((S, D), F32)),
        grid=(ngroup, nb // QB),
        in_specs=[blk(0), col_all(1), col_all(2), blk(0), blk(0)],
        out_specs=(blk(0), col_all(0), col_all(0)),
        sem=("arbitrary", "arbitrary"), carry=carry)


SWA_Q_GROUPS = 4


def _roll_heads(x):
    return pltpu.roll(x.astype(F32), HEAD_DIM, 1).astype(BF16)


def _swa_valid(i):
    r = lax.broadcasted_iota(jnp.int32, (BLK, 2 * BLK), 0)
    c = lax.broadcasted_iota(jnp.int32, (BLK, 2 * BLK), 1)
    diff = r + BLK - c
    return (diff >= 0) & (diff < BLK) & ((i > 0) | (c >= BLK))


def _swa_probs(z, valid, sink):
    z = jnp.where(valid, z * ATTN_SCALE, NEG_BIG)
    mx = jnp.maximum(jnp.max(z, axis=1, keepdims=True), sink)
    p = jnp.exp(z - mx)
    ps = jnp.exp(sink - mx)
    inv = 1.0 / (jnp.sum(p, axis=1, keepdims=True) + ps)
    return p * inv, ps * inv


def _swa_operands(q_ref, kc_ref, kp_ref, vc_ref, vp_ref, s_ref, m):
    m0 = _head_masks()
    m0k = jnp.concatenate([m0, m0], axis=0)
    kk = jnp.concatenate([kp_ref[...], kc_ref[...]], axis=0)
    vv = jnp.concatenate([vp_ref[...], vc_ref[...]], axis=0)
    ksw, vsw = _roll_heads(kk), _roll_heads(vv)
    zk = jnp.zeros_like(kk)
    heads = []
    for c in range(SWA_Q_GROUPS):
        qc = q_ref[:, c * LANES:(c + 1) * LANES]
        zq = jnp.zeros_like(qc)
        for u in range(2):
            same = u == c // 2
            sel = (lambda x, z, mk: jnp.where(mk, x, z)) if u == 0 else (lambda x, z, mk: jnp.where(mk, z, x))
            heads.append(dict(
                c=c, same=same, sel=sel,
                qm=sel(qc, zq, m0),
                k=kk if same else ksw, v=vv if same else vsw,
                km=sel(kk if same else ksw, zk, m0k), vm=sel(vv if same else vsw, zk, m0k),
                sink=s_ref[0, m * 2 * SWA_Q_GROUPS + 2 * c + u]))
    return heads, m0


def _swa_fwd(q, kv, sinks, name):
    S, D = q.shape
    nkvp = kv.shape[1] // (2 * LANES)
    nb = S // BLK
    qw = SWA_Q_GROUPS * LANES

    def body(q_ref, kc_ref, kp_ref, vc_ref, vp_ref, s_ref, o_ref):
        m, i = pl.program_id(0), pl.program_id(1)
        valid = _swa_valid(i)
        heads, _ = _swa_operands(q_ref, kc_ref, kp_ref, vc_ref, vp_ref, s_ref, m)
        zs = [_dot(hd["qm"], hd["k"], NT) for hd in heads]
        ps = [_swa_probs(z, valid, hd["sink"])[0].astype(BF16) for z, hd in zip(zs, heads)]
        for c in range(SWA_Q_GROUPS):
            o_ref[:, c * LANES:(c + 1) * LANES] = (_dot(ps[2 * c], heads[2 * c]["vm"], NN)
                                                   + _dot(ps[2 * c + 1], heads[2 * c + 1]["vm"], NN))

    prev = lambda i: jnp.maximum(i - 1, 0)
    return pl.pallas_call(
        body, name=name, out_shape=jax.ShapeDtypeStruct((S, D), F32),
        grid=(nkvp, nb),
        in_specs=[pl.BlockSpec((BLK, qw), lambda m, i: (i, m)),
                  pl.BlockSpec((BLK, LANES), lambda m, i: (i, m)),
                  pl.BlockSpec((BLK, LANES), lambda m, i: (prev(i), m)),
                  pl.BlockSpec((BLK, LANES), lambda m, i: (i, nkvp + m)),
                  pl.BlockSpec((BLK, LANES), lambda m, i: (prev(i), nkvp + m)),
                  pl.BlockSpec(memory_space=pltpu.SMEM)],
        out_specs=pl.BlockSpec((BLK, qw), lambda m, i: (i, m)),
        compiler_params=_params("arbitrary", "arbitrary"),
    )(q, kv, kv, kv, kv, sinks)


def _swa_bwd(q, kv, sinks, o, do, name):
    S, D = q.shape
    nkvp = kv.shape[1] // (2 * LANES)
    nb = S // BLK
    qw = SWA_Q_GROUPS * LANES
    nh = 2 * SWA_Q_GROUPS

    def body(q_ref, kc_ref, kp_ref, vc_ref, vp_ref, s_ref, o_ref, do_ref, dq_ref, dk_ref, dv_ref, ds_ref):
        m, i = pl.program_id(0), pl.program_id(1)
        valid = _swa_valid(i)
        heads, m0 = _swa_operands(q_ref, kc_ref, kp_ref, vc_ref, vp_ref, s_ref, m)

        @pl.when(i == 0)
        def _():
            dk_ref[...] = jnp.zeros_like(dk_ref)
            dv_ref[...] = jnp.zeros_like(dv_ref)
            ds_ref[...] = jnp.zeros_like(ds_ref)

        doms, deltas = [], []
        for hd in heads:
            c = hd["c"]
            doc = do_ref[:, c * LANES:(c + 1) * LANES]
            prod = doc.astype(F32) * o_ref[:, c * LANES:(c + 1) * LANES]
            doms.append(hd["sel"](doc, jnp.zeros_like(doc), m0))
            deltas.append(jnp.sum(hd["sel"](prod, 0.0, m0), axis=1, keepdims=True))
        zs = [_dot(hd["qm"], hd["k"], NT) for hd in heads]
        dps = [_dot(dom, hd["v"], NT) for dom, hd in zip(doms, heads)]
        pbs, dscs = [], []
        for n, hd in enumerate(heads):
            p, psink = _swa_probs(zs[n], valid, hd["sink"])
            pbs.append(p.astype(BF16))
            dscs.append((p * (dps[n] - deltas[n]) * ATTN_SCALE).astype(BF16))
            dsink = jnp.sum(jnp.broadcast_to(-(psink * deltas[n]), (BLK, LANES)), axis=0, keepdims=True)
            ds_ref[0, n:n + 1, :] += dsink
        for c in range(SWA_Q_GROUPS):
            dq_ref[:, c * LANES:(c + 1) * LANES] = (_dot(dscs[2 * c], heads[2 * c]["km"], NN)
                                                    + _dot(dscs[2 * c + 1], heads[2 * c + 1]["km"], NN))
        acc = {}
        for n, hd in enumerate(heads):
            dk_n = _dot(dscs[n], hd["qm"], TN)
            dv_n = _dot(pbs[n], doms[n], TN)
            for key, val in ((("k", hd["same"]), dk_n), (("v", hd["same"]), dv_n)):
                acc[key] = val if key not in acc else acc[key] + val
        dkk = acc["k", True] + pltpu.roll(acc["k", False], HEAD_DIM, 1)
        dvv = acc["v", True] + pltpu.roll(acc["v", False], HEAD_DIM, 1)
        poff = pl.multiple_of(jnp.maximum(i - 1, 0) * BLK, BLK)
        coff = pl.multiple_of(i * BLK, BLK)
        dk_ref[pl.ds(poff, BLK), :] += dkk[:BLK]
        dv_ref[pl.ds(poff, BLK), :] += dvv[:BLK]
        dk_ref[pl.ds(coff, BLK), :] += dkk[BLK:]
        dv_ref[pl.ds(coff, BLK), :] += dvv[BLK:]

    prev = lambda i: jnp.maximum(i - 1, 0)
    qblk = pl.BlockSpec((BLK, qw), lambda m, i: (i, m))
    col_all = pl.BlockSpec((S, LANES), lambda m, i: (0, m))
    return pl.pallas_call(
        body, name=name,
        out_shape=(jax.ShapeDtypeStruct((S, D), F32),
                   jax.ShapeDtypeStruct((S, nkvp * LANES), F32),
                   jax.ShapeDtypeStruct((S, nkvp * LANES), F32),
                   jax.ShapeDtypeStruct((nkvp, nh, LANES), F32)),
        grid=(nkvp, nb),
        in_specs=[qblk,
                  pl.BlockSpec((BLK, LANES), lambda m, i: (i, m)),
                  pl.BlockSpec((BLK, LANES), lambda m, i: (prev(i), m)),
                  pl.BlockSpec((BLK, LANES), lambda m, i: (i, nkvp + m)),
                  pl.BlockSpec((BLK, LANES), lambda m, i: (prev(i), nkvp + m)),
                  pl.BlockSpec(memory_space=pltpu.SMEM),
                  qblk, qblk],
        out_specs=(qblk, col_all, col_all, pl.BlockSpec((1, nh, LANES), lambda m, i: (m, 0, 0))),
        compiler_params=_params("arbitrary", "arbitrary"),
    )(q, kv, kv, kv, kv, sinks, o, do)


def _dev_index(p):
    return 4 * p[0] + 2 * p[1] + p[2]


def _gather_plan(x_refs, out_refs, send_sems, recv_sems, local_sems):
    n = len(x_refs)
    x_, y_, c_ = lax.axis_index("x"), lax.axis_index("y"), lax.axis_index("c")
    me, sibling = (x_, y_, c_), (x_, y_, 1 - c_)
    chips = [(1 - x_, y_), (x_, 1 - y_), (1 - x_, 1 - y_)]

    def copy(t, k, block, to, src=None):
        dst = out_refs[t].at[_dev_index(block)]
        return pltpu.make_async_remote_copy(
            src_ref=dst if src is None else src, dst_ref=dst,
            send_sem=send_sems.at[7 * t + k], recv_sem=recv_sems.at[7 * t + k],
            device_id=to, device_id_type=MESH)

    mine = [pltpu.make_async_copy(x_refs[t], out_refs[t].at[_dev_index(me)], local_sems.at[t]) for t in range(n)]
    first = []
    for t in range(n):
        first.append(copy(t, 0, me, sibling, src=x_refs[t]))
        first += [copy(t, 1 + j, me, (*chip, c_), src=x_refs[t]) for j, chip in enumerate(chips)]
    arrived = lambda t, j: copy(t, 1 + j, (*chips[j], c_), me)
    forward = lambda t, j: copy(t, 4 + j, (*chips[j], c_), sibling)
    from_sibling = lambda t: copy(t, 0, sibling, me)
    forwarded = lambda t, j: copy(t, 4 + j, (*chips[j], 1 - c_), me)
    return n, mine, first, arrived, forward, from_sibling, forwarded


def _gather_start(x_refs, out_refs, send_sems, recv_sems, local_sems):
    _, mine, first, *_ = _gather_plan(x_refs, out_refs, send_sems, recv_sems, local_sems)
    for cp in mine + first:
        cp.start()


def _gather_finish(x_refs, out_refs, send_sems, recv_sems, local_sems):
    n, mine, first, arrived, forward, from_sibling, forwarded = _gather_plan(
        x_refs, out_refs, send_sems, recv_sems, local_sems)
    passed = []
    for j in range(3):
        for t in range(n):
            arrived(t, j).wait_recv()
            fwd = forward(t, j)
            fwd.start()
            passed.append(fwd)
    for t in range(n):
        from_sibling(t).wait_recv()
    for j in range(3):
        for t in range(n):
            forwarded(t, j).wait_recv()
    for cp in first + passed:
        cp.wait_send()
    for cp in mine:
        cp.wait()


def _scatter_plan(b_refs, out_refs, send_sems, recv_sems, local_sems):
    n = len(b_refs)
    x_, y_, c_ = lax.axis_index("x"), lax.axis_index("y"), lax.axis_index("c")
    my_idx = _dev_index((x_, y_, c_))
    mine = [pltpu.make_async_copy(b_refs[t].at[my_idx], out_refs[t].at[my_idx], local_sems.at[t]) for t in range(n)]
    copies = []
    for t in range(n):
        for k in range(1, N_DEV):
            peer = (x_ ^ ((k >> 2) & 1), y_ ^ ((k >> 1) & 1), c_ ^ (k & 1))
            copies.append(pltpu.make_async_remote_copy(
                src_ref=b_refs[t].at[_dev_index(peer)], dst_ref=out_refs[t].at[my_idx],
                send_sem=send_sems.at[7 * t + k - 1], recv_sem=recv_sems.at[7 * t + k - 1],
                device_id=peer, device_id_type=MESH))
    return mine, copies


def _scatter_start(b_refs, out_refs, send_sems, recv_sems, local_sems):
    mine, copies = _scatter_plan(b_refs, out_refs, send_sems, recv_sems, local_sems)
    for cp in mine + copies:
        cp.start()


def _scatter_finish(b_refs, out_refs, send_sems, recv_sems, local_sems):
    mine, copies = _scatter_plan(b_refs, out_refs, send_sems, recv_sems, local_sems)
    for cp in copies:
        cp.wait_recv()
    for cp in copies:
        cp.wait_send()
    for cp in mine:
        cp.wait()


def _exchange_operands(kind, tensors):
    if kind == "gather":
        args = list(tensors)
        shapes = [jax.ShapeDtypeStruct((N_DEV,) + t.shape, t.dtype) for t in tensors]
        return args, shapes, _gather_start, _gather_finish
    args = [t.reshape(N_DEV, t.shape[0] // N_DEV, t.shape[1]) for t in tensors]
    shapes = [jax.ShapeDtypeStruct(a.shape, a.dtype) for a in args]
    return args, shapes, _scatter_start, _scatter_finish


def _exchange_results(kind, tensors, res):
    if kind == "gather":
        return [r.reshape(N_DEV * t.shape[0], t.shape[1]) for r, t in zip(res, tensors)]
    return list(res)


def _exchange_sems(n):
    return [pltpu.SemaphoreType.DMA((7 * n,)), pltpu.SemaphoreType.DMA((7 * n,)), pltpu.SemaphoreType.DMA((n,))]


def _exchange(kind, tensors, name):
    n = len(tensors)
    args, shapes, start, finish = _exchange_operands(kind, tensors)

    def body(*refs):
        start(refs[:n], refs[n:2 * n], *refs[2 * n:])
        finish(refs[:n], refs[n:2 * n], *refs[2 * n:])

    hbm = pl.BlockSpec(memory_space=pl.ANY)
    res = pl.pallas_call(body, name=name, out_shape=shapes, in_specs=[hbm] * n, out_specs=[hbm] * n,
                         scratch_shapes=_exchange_sems(n))(*args)
    return _exchange_results(kind, tensors, res)


def _pcall(body, args, *, name, out_shape, grid, in_specs, out_specs, sem, scratch_shapes=(), carry=None):
    if carry is None:
        out = pl.pallas_call(body, name=name, out_shape=out_shape, grid=grid, in_specs=list(in_specs),
                             out_specs=out_specs, scratch_shapes=list(scratch_shapes),
                             compiler_params=_params(*sem))(*args)
        return out, None
    kind, tensors = carry
    multi = isinstance(out_shape, (tuple, list))
    shapes = list(out_shape) if multi else [out_shape]
    ospecs = list(out_specs) if multi else [out_specs]
    n_in, n_out, n_scr, n_c = len(in_specs), len(shapes), len(scratch_shapes), len(tensors)
    c_args, c_shapes, start, finish = _exchange_operands(kind, tensors)

    def wrapped(*refs):
        ins, rest = refs[:n_in], refs[n_in:]
        c_in, rest = rest[:n_c], rest[n_c:]
        outs, rest = rest[:n_out], rest[n_out:]
        c_out, rest = rest[:n_c], rest[n_c:]
        scr, sems = rest[:n_scr], rest[n_scr:]
        ids = [pl.program_id(a) for a in range(len(grid))]
        first, last = ids[0] == 0, ids[0] == grid[0] - 1
        for a in range(1, len(grid)):
            first = jnp.logical_and(first, ids[a] == 0)
            last = jnp.logical_and(last, ids[a] == grid[a] - 1)

        @pl.when(first)
        def _():
            start(c_in, c_out, *sems)

        body(*ins, *outs, *scr)

        @pl.when(last)
        def _():
            finish(c_in, c_out, *sems)

    hbm = pl.BlockSpec(memory_space=pl.ANY)
    res = pl.pallas_call(
        wrapped, name=name, out_shape=shapes + c_shapes, grid=grid,
        in_specs=list(in_specs) + [hbm] * n_c, out_specs=ospecs + [hbm] * n_c,
        scratch_shapes=list(scratch_shapes) + _exchange_sems(n_c),
        compiler_params=_params(*sem))(*args, *c_args)
    outs = tuple(res[:n_out]) if multi else res[0]
    return outs, _exchange_results(kind, tensors, res[n_out:])


def _sum8(parts, name):
    _, R, C = parts.shape
    tr = _tile(R, 256, 16)

    def body(p_ref, g_ref):
        g = p_ref[0].astype(F32)
        for s in range(1, N_DEV):
            g = g + p_ref[s].astype(F32)
        g_ref[...] = g

    return pl.pallas_call(
        body, name=name, out_shape=jax.ShapeDtypeStruct((R, C), F32),
        grid=(R // tr,),
        in_specs=[pl.BlockSpec((N_DEV, tr, C), lambda i: (0, i, 0))],
        out_specs=pl.BlockSpec((tr, C), lambda i: (i, 0)),
        compiler_params=_params("parallel"),
    )(parts)


def _adamw(g, w, m, v, name):
    R, C = g.shape
    tr = _tile(R, 256, 8)
    c1 = 1.0 - ADAM_B1 ** ADAM_STEP
    c2 = 1.0 - ADAM_B2 ** ADAM_STEP

    def body(g_ref, w_ref, m_ref, v_ref, d_ref, nm_ref, nv_ref):
        gg = g_ref[...]
        nm = ADAM_B1 * m_ref[...] + (1.0 - ADAM_B1) * gg
        nv = ADAM_B2 * v_ref[...] + (1.0 - ADAM_B2) * (gg * gg)
        m_hat = nm / c1
        v_hat = nv / c2
        nm_ref[...] = nm
        nv_ref[...] = nv
        d_ref[...] = -ADAM_LR * (m_hat / (jnp.sqrt(v_hat) + ADAM_EPS) + ADAM_WD * w_ref[...])

    row = pl.BlockSpec((tr, C), lambda i: (i, 0))
    shp = jax.ShapeDtypeStruct((R, C), F32)
    return pl.pallas_call(
        body, name=name, out_shape=(shp, shp, shp),
        grid=(R // tr,), in_specs=[row, row, row, row], out_specs=(row, row, row),
        compiler_params=_params("parallel"),
    )(g, w, m, v)


def _ffn_down(act, wo, h, tag):
    return _mm(act, wo, NN, F32, f"{tag}_down", scale=FFN_RES_SCALE, res=h, tm=512, tn=1024, tk=2816)


def _ffn_fwd(h, g, win_t, wo, tag):
    return _ffn_fwd_fused(h, g, win_t, wo, f"{tag}_fwd")


def _ffn_bwd(dh, h, g, win_t, wo, saved, tag, scatter=False):
    xn, silu, dsilu, up, act = saved
    dwo = _mm(act, dh, TN, BF16, f"{tag}_dwo", scale=FFN_RES_SCALE, tm=1408, tn=1024, tk=TN_CHUNK)
    if not scatter:
        dh_in, dg, dgate, dup = _ffn_bwd_fused(dh, h, g, win_t, wo, silu, dsilu, up, f"{tag}_bwd")
        dwin_t, _ = _dwin(dgate, dup, xn, f"{tag}_dwin")
        return dh_in, dg, dwin_t, dwo
    dgate, dup = _ffn_dact(dh, wo, silu, dsilu, up, f"{tag}_dact")
    dwin_t, got_wo = _dwin(dgate, dup, xn, f"{tag}_dwin", carry=("scatter", [dwo]))
    (dh_in, dg), got_win = _dx_norm_bwd([(dgate, win_t, NN, 2, 0), (dup, win_t, NN, 2, 1)], h, g, dh, f"{tag}_dx",
                                        carry=("scatter", [dwin_t]))
    return dh_in, dg, got_win[0], got_wo[0]


def _proj(a, w, dims, out_dtype, name, res=None):
    return _mm(a, w, dims, out_dtype, name, res=res, tm=1024, tn=1024, tk=1024)


def _proj_dw(x, dy, name):
    return _mm(x, dy, TN, BF16, name, tm=1024, tn=1024, tk=TN_CHUNK)


def kernel(x, ffn1_norm, ffn1_w_in, ffn1_w_out, mix_norm, ffn2_norm, ffn2_w_in, ffn2_w_out, sb_w_qkv, sb_w_o, kv_norm, kv_w, swa_w_q, swa_sinks, swa_w_o, final_norm, loss_target, m_ffn1_norm, m_ffn1_w_in, m_ffn1_w_out, m_mix_norm, m_ffn2_norm, m_ffn2_w_in, m_ffn2_w_out, m_sb_w_qkv, m_sb_w_o, m_kv_norm, m_kv_w, m_swa_w_q, m_swa_sinks, m_swa_w_o, m_final_norm, v_ffn1_norm, v_ffn1_w_in, v_ffn1_w_out, v_mix_norm, v_ffn2_norm, v_ffn2_w_in, v_ffn2_w_out, v_sb_w_qkv, v_sb_w_o, v_kv_norm, v_kv_w, v_swa_w_q, v_swa_sinks, v_swa_w_o, v_final_norm):
    S, D = x.shape[1], x.shape[2]
    L = ffn1_w_in.shape[0]
    KV = kv_w.shape[1]
    assert L == 2 and swa_sinks.shape == (1, 2 * SWA_Q_GROUPS * KV // (2 * LANES))

    def bf(w):
        return w.astype(BF16)

    def bft(w):
        return jnp.transpose(w).astype(BF16)

    cos_t, sin_t = _rope_tables(S)
    h0 = x.reshape(S, D)
    tgt = loss_target.reshape(S, D)

    win1a_t, = _exchange("gather", [bft(ffn1_w_in[0])], "gather_first_weight")
    sv_a1, (wo1a, wqkv_t, w_sbo) = _ffn_up(
        h0, ffn1_norm[0], win1a_t, "ffn1a_up",
        carry=("gather", [bf(ffn1_w_out[0]), bft(sb_w_qkv[0]), bf(sb_w_o[0])]))
    h1 = _ffn_down(sv_a1[-1], wo1a, h0, "ffn1a")
    hn_a = _rmsnorm(h1, mix_norm[0], "mix_a_norm")
    qkv = _proj(hn_a, wqkv_t, NT, BF16, "sb_qkv")
    o_sb, later = _sb_fwd(qkv, "sb_attn", carry=("gather", [
        bft(ffn2_w_in[0]), bf(ffn2_w_out[0]), bf(kv_w), bft(ffn1_w_in[1]), bf(ffn1_w_out[1]),
        bf(swa_w_q[0]), bf(swa_w_o[0]), bft(ffn2_w_in[1]), bf(ffn2_w_out[1])]))
    win2a_t, wo2a, w_kv, win1b_t, wo1b, w_q, w_swo, win2b_t, wo2b = later
    h2 = _proj(o_sb, w_sbo, NN, F32, "sb_out", res=h1)
    h3, sv_a2 = _ffn_fwd(h2, ffn2_norm[0], win2a_t, wo2a, "ffn2a")
    kvn = _rmsnorm(h3, kv_norm, "kv_norm")
    kv_raw = _proj(kvn, w_kv, NN, F32, "kv_proj")
    kv_rot = _rotary(kv_raw, cos_t, sin_t, KV // (2 * LANES), False, "kv_rope")
    h4, sv_b1 = _ffn_fwd(h3, ffn1_norm[1], win1b_t, wo1b, "ffn1b")
    hn_b = _rmsnorm(h4, mix_norm[1], "mix_b_norm")
    q_raw = _proj(hn_b, w_q, NN, F32, "swa_q")
    q_rot = _rotary(q_raw, cos_t, sin_t, D // LANES, False, "q_rope")
    o_sw = _swa_fwd(q_rot, kv_rot, swa_sinks, "swa_attn")
    h5 = _proj(o_sw, w_swo, NN, F32, "swa_out", res=h4)
    h6, sv_b2 = _ffn_fwd(h5, ffn2_norm[1], win2b_t, wo2b, "ffn2b")
    dh6, dg_final, sq_err = _final_loss(h6, final_norm, tgt, "final_loss")
    loss = lax.psum(0.5 * jnp.sum(sq_err) / D, ("x", "y", "c"))

    dh5, dg_f2b, dwin2b_t, dwo2b = _ffn_bwd(dh6, h5, ffn2_norm[1], win2b_t, wo2b, sv_b2, "ffn2b")
    do_sw = _proj(dh5, w_swo, NT, BF16, "swa_out_dx")
    dw_swo = _proj_dw(o_sw, dh5, "swa_out_dw")
    dq_rot, dk_sw, dv_sw, dsink = _swa_bwd(q_rot, kv_rot, swa_sinks, o_sw, do_sw, "swa_attn_bwd")
    dq = _rotary(dq_rot, cos_t, sin_t, D // LANES, True, "q_rope_bwd")
    dw_q = _proj_dw(hn_b, dq, "swa_q_dw")
    (dh4, dg_mix_b), _ = _dx_norm_bwd([(dq, w_q, NT, 1, 0)], h4, mix_norm[1], dh5, "swa_q_dx")
    dh3, dg_f1b, dwin1b_t, dwo1b = _ffn_bwd(dh4, h3, ffn1_norm[1], win1b_t, wo1b, sv_b1, "ffn1b")
    dkv = _rotary(jnp.concatenate([dk_sw, dv_sw], axis=1), cos_t, sin_t, KV // (2 * LANES), True, "kv_rope_bwd")
    dw_kv = _proj_dw(kvn, dkv, "kv_proj_dw")
    (dh3, dg_kv), _ = _dx_norm_bwd([(dkv, w_kv, NT, 1, 0)], h3, kv_norm, dh3, "kv_proj_dx")
    dh2, dg_f2a, dwin2a_t, dwo2a = _ffn_bwd(dh3, h2, ffn2_norm[0], win2a_t, wo2a, sv_a2, "ffn2a")
    do_sb = _proj(dh2, w_sbo, NT, BF16, "sb_out_dx")
    dw_sbo = _proj_dw(o_sb, dh2, "sb_out_dw")
    (dq_sb, dk_sb, dv_sb), early = _sb_bwd(qkv, o_sb, do_sb, "sb_attn_bwd", carry=("scatter", [
        dwin2b_t, dwo2b, dw_swo, dw_q, dwin1b_t, dwo1b, dw_kv, dwin2a_t, dwo2a, dw_sbo]))
    dqkv = jnp.concatenate([dq_sb, dk_sb.astype(BF16), dv_sb.astype(BF16)], axis=1)
    dwqkv_t = _proj_dw(dqkv, hn_a, "sb_qkv_dw")
    (dh1, dg_mix_a), (p_qkv,) = _dx_norm_bwd([(dqkv, wqkv_t, NN, 1, 0)], h1, mix_norm[0], dh2, "sb_qkv_dx",
                                             carry=("scatter", [dwqkv_t]))
    dx, dg_f1a, p_win1a, p_wo1a = _ffn_bwd(dh1, h0, ffn1_norm[0], win1a_t, wo1a, sv_a1, "ffn1a", scatter=True)

    p_win2b, p_wo2b, p_swo, p_q, p_win1b, p_wo1b, p_kv, p_win2a, p_wo2a, p_sbo = early

    def natural(parts, tag):
        return _sum8(parts, f"sum_{tag}")

    def from_t(parts, tag):
        return jnp.transpose(_sum8(parts, f"sum_{tag}"))

    grads = {
        "ffn1_w_in": jnp.stack([from_t(p_win1a, "win1a"), from_t(p_win1b, "win1b")]),
        "ffn1_w_out": jnp.stack([natural(p_wo1a, "wo1a"), natural(p_wo1b, "wo1b")]),
        "ffn2_w_in": jnp.stack([from_t(p_win2a, "win2a"), from_t(p_win2b, "win2b")]),
        "ffn2_w_out": jnp.stack([natural(p_wo2a, "wo2a"), natural(p_wo2b, "wo2b")]),
        "sb_w_qkv": from_t(p_qkv, "qkv")[None],
        "sb_w_o": natural(p_sbo, "sbo")[None],
        "kv_w": natural(p_kv, "kv"),
        "swa_w_q": natural(p_q, "swq")[None],
        "swa_w_o": natural(p_swo, "swo")[None],
    }

    small_w = [ffn1_norm, mix_norm, ffn2_norm, kv_norm, final_norm, swa_sinks]
    small_m = [m_ffn1_norm, m_mix_norm, m_ffn2_norm, m_kv_norm, m_final_norm, m_swa_sinks]
    small_v = [v_ffn1_norm, v_mix_norm, v_ffn2_norm, v_kv_norm, v_final_norm, v_swa_sinks]
    SMALL_ROWS = 16

    def pack_small(ts):
        rows_ = [t.reshape(-1, D) for t in ts[:-1]]
        sink_row = jnp.pad(ts[-1].reshape(1, -1), ((0, 0), (0, D - ts[-1].size)))
        flat = jnp.concatenate(rows_ + [sink_row], axis=0)
        return jnp.pad(flat, ((0, SMALL_ROWS - flat.shape[0]), (0, 0)))

    def unpack_small(flat):
        out, r = [], 0
        for t in small_w[:-1]:
            n = t.size // D
            out.append(flat[r:r + n].reshape(t.shape))
            r += n
        out.append(flat[r, :swa_sinks.size].reshape(swa_sinks.shape))
        return out

    def gain(parts8):
        return jnp.sum(parts8, axis=0, keepdims=True)

    g_small_local = pack_small([
        jnp.concatenate([gain(dg_f1a), gain(dg_f1b)], axis=0),
        jnp.concatenate([gain(dg_mix_a), gain(dg_mix_b)], axis=0),
        jnp.concatenate([gain(dg_f2a), gain(dg_f2b)], axis=0),
        gain(dg_kv), gain(dg_final), dsink[:, :, 0].reshape(1, -1)])
    small_parts = _exchange("gather", [g_small_local], "gather_small_grads")[0]
    g_small = _sum8(small_parts.reshape(N_DEV, SMALL_ROWS, D), "sum_small")
    d_small, nm_small, nv_small = _adamw(g_small, pack_small(small_w), pack_small(small_m), pack_small(small_v), "adamw_small")
    small_names = ["ffn1_norm", "mix_norm", "ffn2_norm", "kv_norm", "final_norm", "swa_sinks"]
    result = {"grad": dict(zip(small_names, unpack_small(g_small))),
              "delta": dict(zip(small_names, unpack_small(d_small))),
              "new_m": dict(zip(small_names, unpack_small(nm_small))),
              "new_v": dict(zip(small_names, unpack_small(nv_small)))}

    big = {"ffn1_w_in": (ffn1_w_in, m_ffn1_w_in, v_ffn1_w_in), "ffn1_w_out": (ffn1_w_out, m_ffn1_w_out, v_ffn1_w_out),
           "ffn2_w_in": (ffn2_w_in, m_ffn2_w_in, v_ffn2_w_in), "ffn2_w_out": (ffn2_w_out, m_ffn2_w_out, v_ffn2_w_out),
           "sb_w_qkv": (sb_w_qkv, m_sb_w_qkv, v_sb_w_qkv), "sb_w_o": (sb_w_o, m_sb_w_o, v_sb_w_o),
           "kv_w": (kv_w, m_kv_w, v_kv_w), "swa_w_q": (swa_w_q, m_swa_w_q, v_swa_w_q),
           "swa_w_o": (swa_w_o, m_swa_w_o, v_swa_w_o)}
    for nm, (w, m, v) in big.items():
        g = grads[nm]
        two_d = lambda t: t.reshape(-1, t.shape[-1])
        d, new_m, new_v = _adamw(two_d(g), two_d(w), two_d(m), two_d(v), f"adamw_{nm}")
        result["grad"][nm] = g
        result["delta"][nm] = d.reshape(w.shape)
        result["new_m"][nm] = new_m.reshape(w.shape)
        result["new_v"][nm] = new_v.reshape(w.shape)

    order = ["ffn1_norm", "ffn1_w_in", "ffn1_w_out", "mix_norm", "ffn2_norm", "ffn2_w_in", "ffn2_w_out",
             "sb_w_qkv", "sb_w_o", "kv_norm", "kv_w", "swa_w_q", "swa_sinks", "swa_w_o", "final_norm"]
    outs = [result[kind][nm] for kind in ("grad", "delta", "new_m", "new_v") for nm in order]
    return (loss, dx.reshape(x.shape), *outs)
```

```python
import jax
import jax.numpy as jnp
from jax import lax
from jax.experimental import pallas as pl
from jax.experimental.pallas import tpu as pltpu

F32 = jnp.float32
BF16 = jnp.bfloat16

N_DEV = 8
HEAD_DIM = 64
LANES = 128
BLK = 128
RMS_EPS = 1e-6
FFN_RES_SCALE = 0.5
ROPE_THETA = 10000.0
ATTN_SCALE = HEAD_DIM ** -0.5
SB_LOG_FLOOR = -88.0
NEG_BIG = -1e30
VMEM_LIMIT_V7X = 56 * 1024 * 1024

ADAM_LR = 0.001
ADAM_B1 = 0.9
ADAM_B2 = 0.999
ADAM_EPS = 1e-08
ADAM_WD = 0.01
ADAM_STEP = 10

NN = ((1,), (0,))
NT = ((1,), (1,))
TN = ((0,), (0,))
TN_CHUNK = 2048
MESH = pl.DeviceIdType.MESH


def _dot(a, b, dims):
    return lax.dot_general(a, b, (dims, ((), ())), preferred_element_type=F32)


def _tile(n, pref, mult=LANES):
    if n <= pref:
        return n
    t = (pref // mult) * mult
    while t >= mult:
        if n % t == 0:
            return t
        t -= mult
    return n


def _params(*sem):
    return pltpu.CompilerParams(dimension_semantics=sem, vmem_limit_bytes=VMEM_LIMIT_V7X)


def _mm(a, b, dims, out_dtype, name, scale=1.0, res=None, tm=512, tn=512, tk=512):
    if dims == NN:
        (M, K), (_, N) = a.shape, b.shape
    elif dims == NT:
        (M, K), (N, _) = a.shape, b.shape
    else:
        (K, M), (_, N) = a.shape, b.shape
    tm, tn, tk = _tile(M, tm), _tile(N, tn), _tile(K, tk)
    nk = K // tk
    if dims == TN:
        a_spec = pl.BlockSpec((tk, tm), lambda i, j, k: (k, i))
    else:
        a_spec = pl.BlockSpec((tm, tk), lambda i, j, k: (i, k))
    if dims == NT:
        b_spec = pl.BlockSpec((tn, tk), lambda i, j, k: (j, k))
    else:
        b_spec = pl.BlockSpec((tk, tn), lambda i, j, k: (k, j))
    o_spec = pl.BlockSpec((tm, tn), lambda i, j, k: (i, j))
    has_res = res is not None

    def body(*refs):
        a_ref, b_ref = refs[0], refs[1]
        r_ref = refs[2] if has_res else None
        o_ref = refs[3] if has_res else refs[2]

        def finish(acc):
            r = acc * scale if scale != 1.0 else acc
            if has_res:
                r = r + r_ref[...]
            o_ref[...] = r.astype(out_dtype)

        p = _dot(a_ref[...].astype(BF16), b_ref[...].astype(BF16), dims)
        if nk == 1:
            finish(p)
        else:
            acc_ref = refs[-1]
            k = pl.program_id(2)

            @pl.when(k == 0)
            def _():
                acc_ref[...] = p

            @pl.when(k > 0)
            def _():
                acc_ref[...] += p

            @pl.when(k == nk - 1)
            def _():
                finish(acc_ref[...])

    in_specs = [a_spec, b_spec] + ([o_spec] if has_res else [])
    args = (a, b) + ((res,) if has_res else ())
    return pl.pallas_call(
        body, name=name,
        out_shape=jax.ShapeDtypeStruct((M, N), out_dtype),
        grid=(M // tm, N // tn, nk),
        in_specs=in_specs, out_specs=o_spec,
        scratch_shapes=[pltpu.VMEM((tm, tn), F32)] if nk > 1 else [],
        compiler_params=_params("parallel", "parallel", "arbitrary"),
    )(*args)


def _rows8(x):
    r, d = x.shape
    return jnp.sum(x.reshape(r // 8, 8, d), axis=0)


def _rmsnorm(h, g, name):
    S, D = h.shape
    ts = _tile(S, 512, 8)

    def body(h_ref, g_ref, o_ref):
        x = h_ref[...]
        r = lax.rsqrt(jnp.mean(x * x, axis=-1, keepdims=True) + RMS_EPS)
        o_ref[...] = ((x * r) * g_ref[...]).astype(BF16)

    return pl.pallas_call(
        body, name=name,
        out_shape=jax.ShapeDtypeStruct((S, D), BF16),
        grid=(S // ts,),
        in_specs=[pl.BlockSpec((ts, D), lambda i: (i, 0)), pl.BlockSpec((1, D), lambda i: (0, 0))],
        out_specs=pl.BlockSpec((ts, D), lambda i: (i, 0)),
        compiler_params=_params("parallel"),
    )(h, g.reshape(1, D))


def _final_loss(h, g, tgt, name):
    S, D = h.shape
    ts = _tile(S, 512, 8)

    def body(h_ref, g_ref, t_ref, dh_ref, dg_ref, l_ref):
        x = h_ref[...]
        r = lax.rsqrt(jnp.mean(x * x, axis=-1, keepdims=True) + RMS_EPS)
        xhat = x * r
        err = xhat * g_ref[...] - t_ref[...]
        d = err * (1.0 / D)
        dxh = d * g_ref[...]
        c = jnp.mean(dxh * xhat, axis=-1, keepdims=True)
        dh_ref[...] = r * (dxh - xhat * c)
        part = _rows8(d * xhat)
        lpart = _rows8(err * err)

        @pl.when(pl.program_id(0) == 0)
        def _():
            dg_ref[...] = part
            l_ref[...] = lpart

        @pl.when(pl.program_id(0) > 0)
        def _():
            dg_ref[...] += part
            l_ref[...] += lpart

    row = pl.BlockSpec((ts, D), lambda i: (i, 0))
    acc = pl.BlockSpec((8, D), lambda i: (0, 0))
    return pl.pallas_call(
        body, name=name,
        out_shape=(jax.ShapeDtypeStruct((S, D), F32), jax.ShapeDtypeStruct((8, D), F32),
                   jax.ShapeDtypeStruct((8, D), F32)),
        grid=(S // ts,),
        in_specs=[row, pl.BlockSpec((1, D), lambda i: (0, 0)), row],
        out_specs=(row, acc, acc),
        compiler_params=_params("arbitrary"),
    )(h, g.reshape(1, D), tgt)


def _ffn_up(h, g, win_t, name, carry=None):
    S, D = h.shape
    F = win_t.shape[0] // 2
    tm, tn = _tile(S, 512, 16), _tile(F, 1408)
    nf = F // tn

    def body(h_ref, g_ref, wg_ref, wu_ref, xn_ref, silu_ref, dsilu_ref, up_ref, act_ref):
        x = h_ref[...]
        r = lax.rsqrt(jnp.mean(x * x, axis=-1, keepdims=True) + RMS_EPS)
        xn = ((x * r) * g_ref[...]).astype(BF16)
        xn_ref[...] = xn
        gate = _dot(xn, wg_ref[...], NT)
        up = _dot(xn, wu_ref[...], NT)
        sig = 1.0 / (1.0 + jnp.exp(-gate))
        silu = gate * sig
        up_ref[...] = up.astype(BF16)
        silu_ref[...] = silu.astype(BF16)
        dsilu_ref[...] = (sig + silu * (1.0 - sig)).astype(BF16)
        act_ref[...] = (silu * up).astype(BF16)

    row = pl.BlockSpec((tm, D), lambda i, j: (i, 0))
    blk = pl.BlockSpec((tm, tn), lambda i, j: (i, j))
    hid = jax.ShapeDtypeStruct((S, F), BF16)
    return _pcall(
        body, (h, g.reshape(1, D), win_t, win_t), name=name,
        out_shape=(jax.ShapeDtypeStruct((S, D), BF16), hid, hid, hid, hid),
        grid=(S // tm, nf),
        in_specs=[row, pl.BlockSpec((1, D), lambda i, j: (0, 0)),
                  pl.BlockSpec((tn, D), lambda i, j: (j, 0)),
                  pl.BlockSpec((tn, D), lambda i, j: (j + nf, 0))],
        out_specs=(row, blk, blk, blk, blk),
        sem=("arbitrary", "arbitrary"), carry=carry)


def _ffn_dact(dh, wo, silu, dsilu, up, name):
    S, D = dh.shape
    F = wo.shape[0]
    tm, tn = _tile(S, 512, 16), _tile(F, 1408)

    def body(dh_ref, wo_ref, s_ref, ds_ref, u_ref, dg_ref, du_ref):
        d = _dot(dh_ref[...].astype(BF16), wo_ref[...], NT) * FFN_RES_SCALE
        du_ref[...] = (d * s_ref[...].astype(F32)).astype(BF16)
        dg_ref[...] = (d * u_ref[...].astype(F32) * ds_ref[...].astype(F32)).astype(BF16)

    blk = pl.BlockSpec((tm, tn), lambda j, i: (i, j))
    hid = jax.ShapeDtypeStruct((S, F), BF16)
    return pl.pallas_call(
        body, name=name, out_shape=(hid, hid),
        grid=(F // tn, S // tm),
        in_specs=[pl.BlockSpec((tm, D), lambda j, i: (i, 0)), pl.BlockSpec((tn, D), lambda j, i: (j, 0)),
                  blk, blk, blk],
        out_specs=(blk, blk),
        compiler_params=_params("arbitrary", "arbitrary"),
    )(dh, wo, silu, dsilu, up)


def _dwin(dgate, dup, xn, name, carry=None):
    S, F = dgate.shape
    D = xn.shape[1]
    tr, tk = _tile(F, 1408), _tile(S, TN_CHUNK, 16)
    nf, nk = F // tr, S // tk

    def body(dg_ref, du_ref, x_ref, o_ref, acc_ref):
        r, k = pl.program_id(0), pl.program_id(1)

        def accumulate(a_ref):
            p = _dot(a_ref[...], x_ref[...], TN)

            @pl.when(k == 0)
            def _():
                acc_ref[...] = p

            @pl.when(k > 0)
            def _():
                acc_ref[...] += p

        @pl.when(r < nf)
        def _():
            accumulate(dg_ref)

        @pl.when(r >= nf)
        def _():
            accumulate(du_ref)

        @pl.when(k == nk - 1)
        def _():
            o_ref[...] = acc_ref[...].astype(BF16)

    return _pcall(
        body, (dgate, dup, xn), name=name, out_shape=jax.ShapeDtypeStruct((2 * F, D), BF16),
        grid=(2 * nf, nk),
        in_specs=[pl.BlockSpec((tk, tr), lambda r, k: (jnp.where(r < nf, k, 0), jnp.minimum(r, nf - 1))),
                  pl.BlockSpec((tk, tr), lambda r, k: (jnp.where(r >= nf, k, 0), jnp.maximum(r - nf, 0))),
                  pl.BlockSpec((tk, D), lambda r, k: (k, 0))],
        out_specs=pl.BlockSpec((tr, D), lambda r, k: (r, 0)),
        scratch_shapes=[pltpu.VMEM((tr, D), F32)],
        sem=("arbitrary", "arbitrary"), carry=carry)


def _dx_norm_bwd(terms, h, g, res, name, carry=None):
    S, D = h.shape
    tm = _tile(S, 256, 16)
    n = len(terms)

    def body(*refs):
        dy_refs, w_refs = refs[:n], refs[n:2 * n]
        h_ref, g_ref, r_ref, dh_ref, dg_ref = refs[2 * n:]
        d = _dot(dy_refs[0][...], w_refs[0][...], terms[0][2])
        for t in range(1, n):
            d = d + _dot(dy_refs[t][...], w_refs[t][...], terms[t][2])
        x = h_ref[...]
        r = lax.rsqrt(jnp.mean(x * x, axis=-1, keepdims=True) + RMS_EPS)
        xhat = x * r
        dxh = d * g_ref[...]
        c = jnp.mean(dxh * xhat, axis=-1, keepdims=True)
        dh_ref[...] = r * (dxh - xhat * c) + r_ref[...]
        part = _rows8(d * xhat)

        @pl.when(pl.program_id(0) == 0)
        def _():
            dg_ref[...] = part

        @pl.when(pl.program_id(0) > 0)
        def _():
            dg_ref[...] += part

    def w_spec(w, nblk, blk):
        return pl.BlockSpec((w.shape[0] // nblk, w.shape[1]), lambda i: (blk, 0))

    row = pl.BlockSpec((tm, D), lambda i: (i, 0))
    in_specs = [pl.BlockSpec((tm, t[0].shape[1]), lambda i: (i, 0)) for t in terms]
    in_specs += [w_spec(t[1], t[3], t[4]) for t in terms]
    in_specs += [row, pl.BlockSpec((1, D), lambda i: (0, 0)), row]
    return _pcall(
        body, (*[t[0] for t in terms], *[t[1] for t in terms], h, g.reshape(1, D), res), name=name,
        out_shape=(jax.ShapeDtypeStruct((S, D), F32), jax.ShapeDtypeStruct((8, D), F32)),
        grid=(S // tm,),
        in_specs=in_specs,
        out_specs=(row, pl.BlockSpec((8, D), lambda i: (0, 0))),
        sem=("arbitrary",), carry=carry)


def _load_resident(pairs, sems):
    @pl.when(pl.program_id(0) == 0)
    def _():
        copies = [pltpu.make_async_copy(src, dst, sems.at[n]) for n, (src, dst) in enumerate(pairs)]
        for cp in copies:
            cp.start()
        for cp in copies:
            cp.wait()


def _ffn_fwd_fused(h, g, win_t, wo, name):
    S, D = h.shape
    F = wo.shape[0]
    tm = _tile(S, 256, 16)

    def body(h_ref, g_ref, win_hbm, wo_hbm, out_ref, xn_ref, silu_ref, dsilu_ref, up_ref, act_ref, win_v, wo_v, sems):
        _load_resident([(win_hbm, win_v), (wo_hbm, wo_v)], sems)
        x = h_ref[...]
        r = lax.rsqrt(jnp.mean(x * x, axis=-1, keepdims=True) + RMS_EPS)
        xn = ((x * r) * g_ref[...]).astype(BF16)
        xn_ref[...] = xn
        gate = _dot(xn, win_v[:F, :], NT)
        up = _dot(xn, win_v[F:, :], NT)
        sig = 1.0 / (1.0 + jnp.exp(-gate))
        silu = gate * sig
        act = (silu * up).astype(BF16)
        up_ref[...] = up.astype(BF16)
        silu_ref[...] = silu.astype(BF16)
        dsilu_ref[...] = (sig + silu * (1.0 - sig)).astype(BF16)
        act_ref[...] = act
        out_ref[...] = x + FFN_RES_SCALE * _dot(act, wo_v[...], NN)

    row = pl.BlockSpec((tm, D), lambda i: (i, 0))
    wide = pl.BlockSpec((tm, F), lambda i: (i, 0))
    hbm = pl.BlockSpec(memory_space=pl.ANY)
    hid = jax.ShapeDtypeStruct((S, F), BF16)
    res = pl.pallas_call(
        body, name=name,
        out_shape=(jax.ShapeDtypeStruct((S, D), F32), jax.ShapeDtypeStruct((S, D), BF16), hid, hid, hid, hid),
        grid=(S // tm,),
        in_specs=[row, pl.BlockSpec((1, D), lambda i: (0, 0)), hbm, hbm],
        out_specs=(row, row, wide, wide, wide, wide),
        scratch_shapes=[pltpu.VMEM(win_t.shape, BF16), pltpu.VMEM(wo.shape, BF16), pltpu.SemaphoreType.DMA((2,))],
        compiler_params=_params("arbitrary"),
    )(h, g.reshape(1, D), win_t, wo)
    return res[0], tuple(res[1:])


def _ffn_bwd_fused(dh, h, g, win_t, wo, silu, dsilu, up, name):
    S, D = h.shape
    F = wo.shape[0]
    tm = _tile(S, 256, 16)

    def body(dh_ref, h_ref, g_ref, s_ref, ds_ref, u_ref, win_hbm, wo_hbm,
             dhin_ref, dgain_ref, dgate_ref, dup_ref, win_v, wo_v, sems):
        _load_resident([(win_hbm, win_v), (wo_hbm, wo_v)], sems)
        dhv = dh_ref[...]
        d = _dot(dhv.astype(BF16), wo_v[...], NT) * FFN_RES_SCALE
        dup = (d * s_ref[...].astype(F32)).astype(BF16)
        dgate = (d * u_ref[...].astype(F32) * ds_ref[...].astype(F32)).astype(BF16)
        dup_ref[...] = dup
        dgate_ref[...] = dgate
        dxn = _dot(dgate, win_v[:F, :], NN) + _dot(dup, win_v[F:, :], NN)
        x = h_ref[...]
        r = lax.rsqrt(jnp.mean(x * x, axis=-1, keepdims=True) + RMS_EPS)
        xhat = x * r
        dxh = dxn * g_ref[...]
        c = jnp.mean(dxh * xhat, axis=-1, keepdims=True)
        dhin_ref[...] = r * (dxh - xhat * c) + dhv
        part = _rows8(dxn * xhat)

        @pl.when(pl.program_id(0) == 0)
        def _():
            dgain_ref[...] = part

        @pl.when(pl.program_id(0) > 0)
        def _():
            dgain_ref[...] += part

    row = pl.BlockSpec((tm, D), lambda i: (i, 0))
    wide = pl.BlockSpec((tm, F), lambda i: (i, 0))
    hbm = pl.BlockSpec(memory_space=pl.ANY)
    hid = jax.ShapeDtypeStruct((S, F), BF16)
    return pl.pallas_call(
        body, name=name,
        out_shape=(jax.ShapeDtypeStruct((S, D), F32), jax.ShapeDtypeStruct((8, D), F32), hid, hid),
        grid=(S // tm,),
        in_specs=[row, row, pl.BlockSpec((1, D), lambda i: (0, 0)), wide, wide, wide, hbm, hbm],
        out_specs=(row, pl.BlockSpec((8, D), lambda i: (0, 0)), wide, wide),
        scratch_shapes=[pltpu.VMEM(win_t.shape, BF16), pltpu.VMEM(wo.shape, BF16), pltpu.SemaphoreType.DMA((2,))],
        compiler_params=_params("arbitrary"),
    )(dh, h, g.reshape(1, D), silu, dsilu, up, win_t, wo)


def _rope_tables(S):
    half = HEAD_DIM // 2
    inv_freq = ROPE_THETA ** (-jnp.arange(half, dtype=F32) / half)
    ang = jnp.arange(S).astype(F32)[:, None] * inv_freq[None, :]
    cos, sin = jnp.cos(ang), jnp.sin(ang)
    cos_t = jnp.tile(cos, (1, LANES // half))
    sin_t = jnp.tile(jnp.concatenate([-sin, sin], axis=1), (1, LANES // HEAD_DIM))
    return cos_t, sin_t


def _swap_halves(x):
    lane = lax.broadcasted_iota(jnp.int32, x.shape, 1)
    first = (lane % HEAD_DIM) < (HEAD_DIM // 2)
    return jnp.where(first, pltpu.roll(x, LANES - HEAD_DIM // 2, 1), pltpu.roll(x, HEAD_DIM // 2, 1))


def _rotary(x, cos_t, sin_t, n_rot, inverse, name):
    S, C = x.shape
    ts = _tile(S, 512, 16)
    ng = C // LANES

    def body(x_ref, c_ref, s_ref, o_ref):
        cs, sn = c_ref[...], s_ref[...]
        for gidx in range(ng):
            sl = slice(gidx * LANES, (gidx + 1) * LANES)
            v = x_ref[:, sl].astype(F32)
            if gidx < n_rot:
                if inverse:
                    v = v * cs + _swap_halves(v * sn)
                else:
                    v = v * cs + _swap_halves(v) * sn
            o_ref[:, sl] = v.astype(BF16)

    row = pl.BlockSpec((ts, C), lambda i: (i, 0))
    tab = pl.BlockSpec((ts, LANES), lambda i: (i, 0))
    return pl.pallas_call(
        body, name=name, out_shape=jax.ShapeDtypeStruct((S, C), BF16),
        grid=(S // ts,), in_specs=[row, tab, tab], out_specs=row,
        compiler_params=_params("parallel"),
    )(x, cos_t, sin_t)


def _head_masks():
    lane = lax.broadcasted_iota(jnp.int32, (BLK, LANES), 1)
    return lane < HEAD_DIM


def _split_bf16(x):
    hi = x.astype(BF16)
    lo = (x - hi.astype(F32)).astype(BF16)
    return hi, lo


def _sb_scores(qh, ks, carry, diag, tri_excl, strict):
    n_heads = len(qh)
    zs = [_dot(ks[n], qh[n], NT) for n in range(n_heads)]
    a_l, b_l, split_l = [], [], []
    for z in zs:
        a = jnp.minimum(z, 0.0) - jnp.log(1.0 + jnp.exp(-jnp.abs(z)))
        b = a - z
        if diag:
            b = jnp.where(strict, b, 0.0)
        a_l.append(a)
        b_l.append(b)
        split_l.append(_split_bf16(b))
    sufs = [_dot(tri_excl, hi, NN) + _dot(tri_excl, lo, NN) for hi, lo in split_l]
    w_l = []
    for n in range(n_heads):
        w = jnp.exp(a_l[n] + sufs[n] + carry[n])
        if diag:
            w = jnp.where(strict, w, 0.0)
        w_l.append(w)
    return a_l, b_l, w_l


SB_FWD_PAIRS = 4
SB_BWD_PAIRS = 2
SB_BWD_QBLOCKS = 2


def _any_alive(carries):
    top = carries[0]
    for c in carries[1:]:
        top = jnp.maximum(top, c)
    return (jnp.max(top) > SB_LOG_FLOOR).astype(jnp.int32)


def _sb_masks():
    row = lax.broadcasted_iota(jnp.int32, (BLK, BLK), 0)
    col = lax.broadcasted_iota(jnp.int32, (BLK, BLK), 1)
    tri_excl = jnp.where(col > row, 1.0, 0.0).astype(BF16)
    tri_incl = jnp.where(col >= row, 1.0, 0.0).astype(BF16)
    return row < HEAD_DIM, row < col, tri_excl, tri_incl


def _sb_fwd(qkv, v_t, name, carry=None):
    S, D3 = qkv.shape
    D = D3 // 3
    npair, nb = D // LANES, S // BLK
    P = min(SB_FWD_PAIRS, npair)
    ngroup = npair // P
    W = P * LANES

    def body(q_ref, k_ref, vt_ref, o_ref):
        i = pl.program_id(1)
        m0 = _head_masks()
        top, strict, tri_excl, _ = _sb_masks()
        zq = jnp.zeros((BLK, LANES), BF16)
        lanes = [slice(p * LANES, (p + 1) * LANES) for p in range(P)]
        qh = []
        for sl in lanes:
            q2 = q_ref[:, sl] * ATTN_SCALE
            qh += [jnp.where(m0, q2, zq), jnp.where(m0, zq, q2)]

        def block(j, carry, acc, diag):
            off = pl.multiple_of(j * BLK, BLK)
            ks, vth = [], []
            for sl in lanes:
                k2 = k_ref[pl.ds(off, BLK), sl]
                vt = vt_ref[sl, pl.ds(off, BLK)]
                ks += [k2, k2]
                vth += [jnp.where(top, vt, zq), jnp.where(top, zq, vt)]
            _, b_l, w_l = _sb_scores(qh, ks, carry, diag, tri_excl, strict)
            wb = [w.astype(BF16) for w in w_l]
            new_acc = [acc[p] + _dot(vth[2 * p], wb[2 * p], NN) + _dot(vth[2 * p + 1], wb[2 * p + 1], NN)
                       for p in range(P)]
            new_carry = [carry[n] + jnp.sum(b_l[n], axis=0, keepdims=True) for n in range(2 * P)]
            return new_carry, new_acc

        c0 = jnp.zeros((1, BLK), F32)
        carry, acc = block(i, [c0] * (2 * P), [jnp.zeros((LANES, BLK), F32)] * P, True)

        def cond(st):
            return jnp.logical_and(st[0] >= 0, st[1] > 0)

        def step(st):
            j, _, carry, acc = st
            carry, acc = block(j, carry, acc, False)
            return j - 1, _any_alive(carry), carry, acc

        st = lax.while_loop(cond, step, (i - 1, _any_alive(carry), carry, acc))
        for p, sl in enumerate(lanes):
            o_ref[:, sl] = jnp.transpose(st[3][p])

    return _pcall(
        body, (qkv, qkv, v_t), name=name, out_shape=jax.ShapeDtypeStruct((S, D), F32),
        grid=(ngroup, nb),
        in_specs=[pl.BlockSpec((BLK, W), lambda g, i: (i, g)),
                  pl.BlockSpec((S, W), lambda g, i: (0, ngroup + g)),
                  pl.BlockSpec((W, S), lambda g, i: (g, 0))],
        out_specs=pl.BlockSpec((BLK, W), lambda g, i: (i, g)),
        sem=("arbitrary", "arbitrary"), carry=carry)


def _sb_bwd(qkv, k_t, o, do, name, carry=None):
    S, D3 = qkv.shape
    D = D3 // 3
    npair, nb = D // LANES, S // BLK
    P = min(SB_BWD_PAIRS, npair)
    ngroup = npair // P
    W = P * LANES

    QB = SB_BWD_QBLOCKS if nb % SB_BWD_QBLOCKS == 0 else 1
    nch = QB * 2 * P

    def body(q_ref, o_ref, do_ref, qkv_hbm, kt_hbm, dq_ref, dk_ref, dv_ref, k_ref, v_ref, kt_ref, sems):
        grp = pl.program_id(0)
        i_first = pl.program_id(1) * QB
        m0 = _head_masks()
        top, strict, tri_excl, tri_incl = _sb_masks()
        zq = jnp.zeros((BLK, LANES), BF16)
        lanes = [slice(p * LANES, (p + 1) * LANES) for p in range(P)]

        @pl.when(pl.program_id(1) == 0)
        def _():
            copies = [pltpu.make_async_copy(qkv_hbm.at[:, pl.ds(pl.multiple_of((c * ngroup + grp) * W, LANES), W)],
                                            ref, sems.at[c - 1]) for c, ref in ((1, k_ref), (2, v_ref))]
            copies.append(pltpu.make_async_copy(kt_hbm.at[pl.ds(pl.multiple_of(grp * W, LANES), W), :],
                                                kt_ref, sems.at[2]))
            for cp in copies:
                cp.start()
            dk_ref[...] = jnp.zeros_like(dk_ref)
            dv_ref[...] = jnp.zeros_like(dv_ref)
            for cp in copies:
                cp.wait()

        qh, doh, delta = [], [], []
        for qb in range(QB):
            rs = slice(qb * BLK, (qb + 1) * BLK)
            for sl in lanes:
                q2, do2 = q_ref[rs, sl] * ATTN_SCALE, do_ref[rs, sl]
                qh += [jnp.where(m0, q2, zq), jnp.where(m0, zq, q2)]
                doh += [jnp.where(m0, do2, zq), jnp.where(m0, zq, do2)]
                prod_t = jnp.transpose(do2.astype(F32) * o_ref[rs, sl])
                delta += [jnp.sum(jnp.where(top, prod_t, 0.0), axis=0, keepdims=True),
                          jnp.sum(jnp.where(top, 0.0, prod_t), axis=0, keepdims=True)]

        def block(js, valid, cb, cg, dq, diag):
            offs = [pl.multiple_of(j * BLK, BLK) for j in js]
            ks, vs, kth = [], [], []
            for qb in range(QB):
                for sl in lanes:
                    k2, v2 = k_ref[pl.ds(offs[qb], BLK), sl], v_ref[pl.ds(offs[qb], BLK), sl]
                    ks += [k2, k2]
                    vs += [v2, v2]
                    kt = kt_ref[sl, pl.ds(offs[qb], BLK)] * ATTN_SCALE
                    kth += [jnp.where(top, kt, zq), jnp.where(top, zq, kt)]
            dws = [_dot(vs[n], doh[n], NT) for n in range(nch)]
            a_l, b_l, w_l = _sb_scores(qh, ks, cb, diag, tri_excl, strict)
            wb = [w.astype(BF16) for w in w_l]
            g_l = [dws[n] * wb[n].astype(F32) for n in range(nch)]
            gsplit = [_split_bf16(g) for g in g_l]
            gincs = [_dot(tri_incl, hi, NN) + _dot(tri_incl, lo, NN) for hi, lo in gsplit]
            dzs = []
            for n in range(nch):
                beta = jnp.exp(a_l[n])
                dz = g_l[n] - beta * (g_l[n] + ((delta[n] - cg[n]) - gincs[n]))
                if diag:
                    dz = jnp.where(strict, dz, 0.0)
                if valid[n // (2 * P)] is not None:
                    dz = jnp.where(valid[n // (2 * P)], dz, 0.0)
                dzs.append(dz.astype(BF16))
            ndq = []
            for qb in range(QB):
                for p, sl in enumerate(lanes):
                    n0 = qb * 2 * P + 2 * p
                    ndq.append(dq[qb * P + p] + _dot(kth[n0], dzs[n0], NN) + _dot(kth[n0 + 1], dzs[n0 + 1], NN))
                    dk_ref[pl.ds(offs[qb], BLK), sl] += _dot(dzs[n0], qh[n0], NN) + _dot(dzs[n0 + 1], qh[n0 + 1], NN)
                    dv_ref[pl.ds(offs[qb], BLK), sl] += _dot(wb[n0], doh[n0], NN) + _dot(wb[n0 + 1], doh[n0 + 1], NN)
            ncb = [cb[n] + jnp.sum(b_l[n], axis=0, keepdims=True) for n in range(nch)]
            ncg = [cg[n] + jnp.sum(g_l[n], axis=0, keepdims=True) for n in range(nch)]
            return ncb, ncg, ndq

        c0 = jnp.zeros((1, BLK), F32)
        cb, cg, dq = block([i_first + qb for qb in range(QB)], [None] * QB, [c0] * nch, [c0] * nch,
                           [jnp.zeros((LANES, BLK), F32)] * (QB * P), True)

        def cond(st):
            return jnp.logical_and(i_first + QB - 1 - st[0] >= 0, st[1] > 0)

        def step(st):
            t, _, cb, cg, dq = st
            js = [i_first + qb - t for qb in range(QB)]
            valid = [js[qb] >= 0 for qb in range(QB - 1)] + [None]
            cb = [cb[n] if valid[n // (2 * P)] is None else jnp.where(valid[n // (2 * P)], cb[n], NEG_BIG)
                  for n in range(nch)]
            cb, cg, dq = block([jnp.maximum(j, 0) for j in js], valid, cb, cg, dq, False)
            return t + 1, _any_alive(cb), cb, cg, dq

        st = lax.while_loop(cond, step, (1, _any_alive(cb), cb, cg, dq))
        for qb in range(QB):
            for p, sl in enumerate(lanes):
                dq_ref[qb * BLK:(qb + 1) * BLK, sl] = jnp.transpose(st[4][qb * P + p]).astype(BF16)

    blk = pl.BlockSpec((QB * BLK, W), lambda g, i: (i, g))
    col_all = pl.BlockSpec((S, W), lambda g, i: (0, g))
    hbm = pl.BlockSpec(memory_space=pl.ANY)
    return _pcall(
        body, (qkv, o, do, qkv, k_t), name=name,
        out_shape=(jax.ShapeDtypeStruct((S, D), BF16), jax.ShapeDtypeStruct((S, D), F32),
                   jax.ShapeDtypeStruct((S, D), F32)),
        grid=(ngroup, nb // QB),
        in_specs=[blk, blk, blk, hbm, hbm],
        out_specs=(blk, col_all, col_all),
        scratch_shapes=[pltpu.VMEM((S, W), BF16), pltpu.VMEM((S, W), BF16), pltpu.VMEM((W, S), BF16),
                        pltpu.SemaphoreType.DMA((3,))],
        sem=("arbitrary", "arbitrary"), carry=carry)


SWA_Q_GROUPS = 4


def _roll_heads(x):
    return pltpu.roll(x.astype(F32), HEAD_DIM, 1).astype(BF16)


def _swa_valid(i):
    r = lax.broadcasted_iota(jnp.int32, (BLK, 2 * BLK), 0)
    c = lax.broadcasted_iota(jnp.int32, (BLK, 2 * BLK), 1)
    diff = r + BLK - c
    return (diff >= 0) & (diff < BLK) & ((i > 0) | (c >= BLK))


def _swa_probs(z, valid, sink):
    z = jnp.where(valid, z * ATTN_SCALE, NEG_BIG)
    mx = jnp.maximum(jnp.max(z, axis=1, keepdims=True), sink)
    p = jnp.exp(z - mx)
    ps = jnp.exp(sink - mx)
    inv = 1.0 / (jnp.sum(p, axis=1, keepdims=True) + ps)
    return p * inv, ps * inv


def _swa_operands(q_ref, kc_ref, kp_ref, vc_ref, vp_ref, s_ref, m):
    m0 = _head_masks()
    m0k = jnp.concatenate([m0, m0], axis=0)
    kk = jnp.concatenate([kp_ref[...], kc_ref[...]], axis=0)
    vv = jnp.concatenate([vp_ref[...], vc_ref[...]], axis=0)
    ksw, vsw = _roll_heads(kk), _roll_heads(vv)
    zk = jnp.zeros_like(kk)
    heads = []
    for c in range(SWA_Q_GROUPS):
        qc = q_ref[:, c * LANES:(c + 1) * LANES]
        zq = jnp.zeros_like(qc)
        for u in range(2):
            same = u == c // 2
            sel = (lambda x, z, mk: jnp.where(mk, x, z)) if u == 0 else (lambda x, z, mk: jnp.where(mk, z, x))
            heads.append(dict(
                c=c, same=same, sel=sel,
                qm=sel(qc, zq, m0),
                k=kk if same else ksw, v=vv if same else vsw,
                km=sel(kk if same else ksw, zk, m0k), vm=sel(vv if same else vsw, zk, m0k),
                sink=s_ref[0, m * 2 * SWA_Q_GROUPS + 2 * c + u]))
    return heads, m0


def _swa_fwd(q, kv, sinks, name):
    S, D = q.shape
    nkvp = kv.shape[1] // (2 * LANES)
    nb = S // BLK
    qw = SWA_Q_GROUPS * LANES

    def body(q_ref, kc_ref, kp_ref, vc_ref, vp_ref, s_ref, o_ref):
        m, i = pl.program_id(0), pl.program_id(1)
        valid = _swa_valid(i)
        heads, _ = _swa_operands(q_ref, kc_ref, kp_ref, vc_ref, vp_ref, s_ref, m)
        zs = [_dot(hd["qm"], hd["k"], NT) for hd in heads]
        ps = [_swa_probs(z, valid, hd["sink"])[0].astype(BF16) for z, hd in zip(zs, heads)]
        for c in range(SWA_Q_GROUPS):
            o_ref[:, c * LANES:(c + 1) * LANES] = (_dot(ps[2 * c], heads[2 * c]["vm"], NN)
                                                   + _dot(ps[2 * c + 1], heads[2 * c + 1]["vm"], NN))

    prev = lambda i: jnp.maximum(i - 1, 0)
    return pl.pallas_call(
        body, name=name, out_shape=jax.ShapeDtypeStruct((S, D), F32),
        grid=(nkvp, nb),
        in_specs=[pl.BlockSpec((BLK, qw), lambda m, i: (i, m)),
                  pl.BlockSpec((BLK, LANES), lambda m, i: (i, m)),
                  pl.BlockSpec((BLK, LANES), lambda m, i: (prev(i), m)),
                  pl.BlockSpec((BLK, LANES), lambda m, i: (i, nkvp + m)),
                  pl.BlockSpec((BLK, LANES), lambda m, i: (prev(i), nkvp + m)),
                  pl.BlockSpec(memory_space=pltpu.SMEM)],
        out_specs=pl.BlockSpec((BLK, qw), lambda m, i: (i, m)),
        compiler_params=_params("arbitrary", "arbitrary"),
    )(q, kv, kv, kv, kv, sinks)


def _swa_bwd(q, kv, sinks, o, do, name):
    S, D = q.shape
    nkvp = kv.shape[1] // (2 * LANES)
    nb = S // BLK
    qw = SWA_Q_GROUPS * LANES
    nh = 2 * SWA_Q_GROUPS

    def body(q_ref, kc_ref, kp_ref, vc_ref, vp_ref, s_ref, o_ref, do_ref, dq_ref, dk_ref, dv_ref, ds_ref):
        m, i = pl.program_id(0), pl.program_id(1)
        valid = _swa_valid(i)
        heads, m0 = _swa_operands(q_ref, kc_ref, kp_ref, vc_ref, vp_ref, s_ref, m)

        @pl.when(i == 0)
        def _():
            dk_ref[...] = jnp.zeros_like(dk_ref)
            dv_ref[...] = jnp.zeros_like(dv_ref)
            ds_ref[...] = jnp.zeros_like(ds_ref)

        doms, deltas = [], []
        for hd in heads:
            c = hd["c"]
            doc = do_ref[:, c * LANES:(c + 1) * LANES]
            prod = doc.astype(F32) * o_ref[:, c * LANES:(c + 1) * LANES]
            doms.append(hd["sel"](doc, jnp.zeros_like(doc), m0))
            deltas.append(jnp.sum(hd["sel"](prod, 0.0, m0), axis=1, keepdims=True))
        zs = [_dot(hd["qm"], hd["k"], NT) for hd in heads]
        dps = [_dot(dom, hd["v"], NT) for dom, hd in zip(doms, heads)]
        pbs, dscs = [], []
        for n, hd in enumerate(heads):
            p, psink = _swa_probs(zs[n], valid, hd["sink"])
            pbs.append(p.astype(BF16))
            dscs.append((p * (dps[n] - deltas[n]) * ATTN_SCALE).astype(BF16))
            dsink = jnp.sum(jnp.broadcast_to(-(psink * deltas[n]), (BLK, LANES)), axis=0, keepdims=True)
            ds_ref[0, n:n + 1, :] += dsink
        for c in range(SWA_Q_GROUPS):
            dq_ref[:, c * LANES:(c + 1) * LANES] = (_dot(dscs[2 * c], heads[2 * c]["km"], NN)
                                                    + _dot(dscs[2 * c + 1], heads[2 * c + 1]["km"], NN))
        acc = {}
        for n, hd in enumerate(heads):
            dk_n = _dot(dscs[n], hd["qm"], TN)
            dv_n = _dot(pbs[n], doms[n], TN)
            for key, val in ((("k", hd["same"]), dk_n), (("v", hd["same"]), dv_n)):
                acc[key] = val if key not in acc else acc[key] + val
        dkk = acc["k", True] + pltpu.roll(acc["k", False], HEAD_DIM, 1)
        dvv = acc["v", True] + pltpu.roll(acc["v", False], HEAD_DIM, 1)
        poff = pl.multiple_of(jnp.maximum(i - 1, 0) * BLK, BLK)
        coff = pl.multiple_of(i * BLK, BLK)
        dk_ref[pl.ds(poff, BLK), :] += dkk[:BLK]
        dv_ref[pl.ds(poff, BLK), :] += dvv[:BLK]
        dk_ref[pl.ds(coff, BLK), :] += dkk[BLK:]
        dv_ref[pl.ds(coff, BLK), :] += dvv[BLK:]

    prev = lambda i: jnp.maximum(i - 1, 0)
    qblk = pl.BlockSpec((BLK, qw), lambda m, i: (i, m))
    col_all = pl.BlockSpec((S, LANES), lambda m, i: (0, m))
    return pl.pallas_call(
        body, name=name,
        out_shape=(jax.ShapeDtypeStruct((S, D), F32),
                   jax.ShapeDtypeStruct((S, nkvp * LANES), F32),
                   jax.ShapeDtypeStruct((S, nkvp * LANES), F32),
                   jax.ShapeDtypeStruct((nkvp, nh, LANES), F32)),
        grid=(nkvp, nb),
        in_specs=[qblk,
                  pl.BlockSpec((BLK, LANES), lambda m, i: (i, m)),
                  pl.BlockSpec((BLK, LANES), lambda m, i: (prev(i), m)),
                  pl.BlockSpec((BLK, LANES), lambda m, i: (i, nkvp + m)),
                  pl.BlockSpec((BLK, LANES), lambda m, i: (prev(i), nkvp + m)),
                  pl.BlockSpec(memory_space=pltpu.SMEM),
                  qblk, qblk],
        out_specs=(qblk, col_all, col_all, pl.BlockSpec((1, nh, LANES), lambda m, i: (m, 0, 0))),
        compiler_params=_params("arbitrary", "arbitrary"),
    )(q, kv, kv, kv, kv, sinks, o, do)


def _dev_index(p):
    return 4 * p[0] + 2 * p[1] + p[2]


def _gather_plan(x_refs, out_refs, send_sems, recv_sems, local_sems):
    n = len(x_refs)
    x_, y_, c_ = lax.axis_index("x"), lax.axis_index("y"), lax.axis_index("c")
    me, sibling = (x_, y_, c_), (x_, y_, 1 - c_)
    chips = [(1 - x_, y_), (x_, 1 - y_), (1 - x_, 1 - y_)]

    def copy(t, k, block, to, src=None):
        dst = out_refs[t].at[_dev_index(block)]
        return pltpu.make_async_remote_copy(
            src_ref=dst if src is None else src, dst_ref=dst,
            send_sem=send_sems.at[7 * t + k], recv_sem=recv_sems.at[7 * t + k],
            device_id=to, device_id_type=MESH)

    mine = [pltpu.make_async_copy(x_refs[t], out_refs[t].at[_dev_index(me)], local_sems.at[t]) for t in range(n)]
    first = []
    for t in range(n):
        first.append(copy(t, 0, me, sibling, src=x_refs[t]))
        first += [copy(t, 1 + j, me, (*chip, c_), src=x_refs[t]) for j, chip in enumerate(chips)]
    arrived = lambda t, j: copy(t, 1 + j, (*chips[j], c_), me)
    forward = lambda t, j: copy(t, 4 + j, (*chips[j], c_), sibling)
    from_sibling = lambda t: copy(t, 0, sibling, me)
    forwarded = lambda t, j: copy(t, 4 + j, (*chips[j], 1 - c_), me)
    return n, mine, first, arrived, forward, from_sibling, forwarded


def _gather_start(x_refs, out_refs, send_sems, recv_sems, local_sems):
    _, mine, first, *_ = _gather_plan(x_refs, out_refs, send_sems, recv_sems, local_sems)
    for cp in mine + first:
        cp.start()


def _gather_finish(x_refs, out_refs, send_sems, recv_sems, local_sems):
    n, mine, first, arrived, forward, from_sibling, forwarded = _gather_plan(
        x_refs, out_refs, send_sems, recv_sems, local_sems)
    passed = []
    for j in range(3):
        for t in range(n):
            arrived(t, j).wait_recv()
            fwd = forward(t, j)
            fwd.start()
            passed.append(fwd)
    for t in range(n):
        from_sibling(t).wait_recv()
    for j in range(3):
        for t in range(n):
            forwarded(t, j).wait_recv()
    for cp in first + passed:
        cp.wait_send()
    for cp in mine:
        cp.wait()


def _scatter_plan(b_refs, out_refs, send_sems, recv_sems, local_sems):
    n = len(b_refs)
    x_, y_, c_ = lax.axis_index("x"), lax.axis_index("y"), lax.axis_index("c")
    my_idx = _dev_index((x_, y_, c_))
    mine = [pltpu.make_async_copy(b_refs[t].at[my_idx], out_refs[t].at[my_idx], local_sems.at[t]) for t in range(n)]
    copies = []
    for t in range(n):
        for k in range(1, N_DEV):
            peer = (x_ ^ ((k >> 2) & 1), y_ ^ ((k >> 1) & 1), c_ ^ (k & 1))
            copies.append(pltpu.make_async_remote_copy(
                src_ref=b_refs[t].at[_dev_index(peer)], dst_ref=out_refs[t].at[my_idx],
                send_sem=send_sems.at[7 * t + k - 1], recv_sem=recv_sems.at[7 * t + k - 1],
                device_id=peer, device_id_type=MESH))
    return mine, copies


def _scatter_start(b_refs, out_refs, send_sems, recv_sems, local_sems):
    mine, copies = _scatter_plan(b_refs, out_refs, send_sems, recv_sems, local_sems)
    for cp in mine + copies:
        cp.start()


def _scatter_finish(b_refs, out_refs, send_sems, recv_sems, local_sems):
    mine, copies = _scatter_plan(b_refs, out_refs, send_sems, recv_sems, local_sems)
    for cp in copies:
        cp.wait_recv()
    for cp in copies:
        cp.wait_send()
    for cp in mine:
        cp.wait()


def _exchange_operands(kind, tensors):
    if kind == "gather":
        args = list(tensors)
        shapes = [jax.ShapeDtypeStruct((N_DEV,) + t.shape, t.dtype) for t in tensors]
        return args, shapes, _gather_start, _gather_finish
    args = [t.reshape(N_DEV, t.shape[0] // N_DEV, t.shape[1]) for t in tensors]
    shapes = [jax.ShapeDtypeStruct(a.shape, a.dtype) for a in args]
    return args, shapes, _scatter_start, _scatter_finish


def _exchange_results(kind, tensors, res):
    if kind == "gather":
        return [r.reshape(N_DEV * t.shape[0], t.shape[1]) for r, t in zip(res, tensors)]
    return list(res)


def _exchange_sems(n):
    return [pltpu.SemaphoreType.DMA((7 * n,)), pltpu.SemaphoreType.DMA((7 * n,)), pltpu.SemaphoreType.DMA((n,))]


def _exchange(kind, tensors, name):
    n = len(tensors)
    args, shapes, start, finish = _exchange_operands(kind, tensors)

    def body(*refs):
        start(refs[:n], refs[n:2 * n], *refs[2 * n:])
        finish(refs[:n], refs[n:2 * n], *refs[2 * n:])

    hbm = pl.BlockSpec(memory_space=pl.ANY)
    res = pl.pallas_call(body, name=name, out_shape=shapes, in_specs=[hbm] * n, out_specs=[hbm] * n,
                         scratch_shapes=_exchange_sems(n))(*args)
    return _exchange_results(kind, tensors, res)


def _pcall(body, args, *, name, out_shape, grid, in_specs, out_specs, sem, scratch_shapes=(), carry=None):
    if carry is None:
        out = pl.pallas_call(body, name=name, out_shape=out_shape, grid=grid, in_specs=list(in_specs),
                             out_specs=out_specs, scratch_shapes=list(scratch_shapes),
                             compiler_params=_params(*sem))(*args)
        return out, None
    kind, tensors = carry
    multi = isinstance(out_shape, (tuple, list))
    shapes = list(out_shape) if multi else [out_shape]
    ospecs = list(out_specs) if multi else [out_specs]
    n_in, n_out, n_scr, n_c = len(in_specs), len(shapes), len(scratch_shapes), len(tensors)
    c_args, c_shapes, start, finish = _exchange_operands(kind, tensors)

    def wrapped(*refs):
        ins, rest = refs[:n_in], refs[n_in:]
        c_in, rest = rest[:n_c], rest[n_c:]
        outs, rest = rest[:n_out], rest[n_out:]
        c_out, rest = rest[:n_c], rest[n_c:]
        scr, sems = rest[:n_scr], rest[n_scr:]
        ids = [pl.program_id(a) for a in range(len(grid))]
        first, last = ids[0] == 0, ids[0] == grid[0] - 1
        for a in range(1, len(grid)):
            first = jnp.logical_and(first, ids[a] == 0)
            last = jnp.logical_and(last, ids[a] == grid[a] - 1)

        @pl.when(first)
        def _():
            start(c_in, c_out, *sems)

        body(*ins, *outs, *scr)

        @pl.when(last)
        def _():
            finish(c_in, c_out, *sems)

    hbm = pl.BlockSpec(memory_space=pl.ANY)
    res = pl.pallas_call(
        wrapped, name=name, out_shape=shapes + c_shapes, grid=grid,
        in_specs=list(in_specs) + [hbm] * n_c, out_specs=ospecs + [hbm] * n_c,
        scratch_shapes=list(scratch_shapes) + _exchange_sems(n_c),
        compiler_params=_params(*sem))(*args, *c_args)
    outs = tuple(res[:n_out]) if multi else res[0]
    return outs, _exchange_results(kind, tensors, res[n_out:])


def _sum8(parts, name):
    _, R, C = parts.shape
    tr = _tile(R, 256, 16)

    def body(p_ref, g_ref):
        g = p_ref[0].astype(F32)
        for s in range(1, N_DEV):
            g = g + p_ref[s].astype(F32)
        g_ref[...] = g

    return pl.pallas_call(
        body, name=name, out_shape=jax.ShapeDtypeStruct((R, C), F32),
        grid=(R // tr,),
        in_specs=[pl.BlockSpec((N_DEV, tr, C), lambda i: (0, i, 0))],
        out_specs=pl.BlockSpec((tr, C), lambda i: (i, 0)),
        compiler_params=_params("parallel"),
    )(parts)


def _adamw(g, w, m, v, name):
    R, C = g.shape
    tr = _tile(R, 256, 8)
    c1 = 1.0 - ADAM_B1 ** ADAM_STEP
    c2 = 1.0 - ADAM_B2 ** ADAM_STEP

    def body(g_ref, w_ref, m_ref, v_ref, d_ref, nm_ref, nv_ref):
        gg = g_ref[...]
        nm = ADAM_B1 * m_ref[...] + (1.0 - ADAM_B1) * gg
        nv = ADAM_B2 * v_ref[...] + (1.0 - ADAM_B2) * (gg * gg)
        m_hat = nm / c1
        v_hat = nv / c2
        nm_ref[...] = nm
        nv_ref[...] = nv
        d_ref[...] = -ADAM_LR * (m_hat / (jnp.sqrt(v_hat) + ADAM_EPS) + ADAM_WD * w_ref[...])

    row = pl.BlockSpec((tr, C), lambda i: (i, 0))
    shp = jax.ShapeDtypeStruct((R, C), F32)
    return pl.pallas_call(
        body, name=name, out_shape=(shp, shp, shp),
        grid=(R // tr,), in_specs=[row, row, row, row], out_specs=(row, row, row),
        compiler_params=_params("parallel"),
    )(g, w, m, v)


def _ffn_down(act, wo, h, tag):
    return _mm(act, wo, NN, F32, f"{tag}_down", scale=FFN_RES_SCALE, res=h, tm=512, tn=1024, tk=2816)


def _ffn_fwd(h, g, win_t, wo, tag):
    return _ffn_fwd_fused(h, g, win_t, wo, f"{tag}_fwd")


def _ffn_bwd(dh, h, g, win_t, wo, saved, tag, scatter=False):
    xn, silu, dsilu, up, act = saved
    dwo = _mm(act, dh, TN, BF16, f"{tag}_dwo", scale=FFN_RES_SCALE, tm=1408, tn=1024, tk=TN_CHUNK)
    if not scatter:
        dh_in, dg, dgate, dup = _ffn_bwd_fused(dh, h, g, win_t, wo, silu, dsilu, up, f"{tag}_bwd")
        dwin_t, _ = _dwin(dgate, dup, xn, f"{tag}_dwin")
        return dh_in, dg, dwin_t, dwo
    dgate, dup = _ffn_dact(dh, wo, silu, dsilu, up, f"{tag}_dact")
    dwin_t, got_wo = _dwin(dgate, dup, xn, f"{tag}_dwin", carry=("scatter", [dwo]))
    (dh_in, dg), got_win = _dx_norm_bwd([(dgate, win_t, NN, 2, 0), (dup, win_t, NN, 2, 1)], h, g, dh, f"{tag}_dx",
                                        carry=("scatter", [dwin_t]))
    return dh_in, dg, got_win[0], got_wo[0]


def _proj(a, w, dims, out_dtype, name, res=None):
    return _mm(a, w, dims, out_dtype, name, res=res, tm=1024, tn=1024, tk=1024)


def _proj_dw(x, dy, name):
    return _mm(x, dy, TN, BF16, name, tm=1024, tn=1024, tk=TN_CHUNK)


def kernel(x, ffn1_norm, ffn1_w_in, ffn1_w_out, mix_norm, ffn2_norm, ffn2_w_in, ffn2_w_out, sb_w_qkv, sb_w_o, kv_norm, kv_w, swa_w_q, swa_sinks, swa_w_o, final_norm, loss_target, m_ffn1_norm, m_ffn1_w_in, m_ffn1_w_out, m_mix_norm, m_ffn2_norm, m_ffn2_w_in, m_ffn2_w_out, m_sb_w_qkv, m_sb_w_o, m_kv_norm, m_kv_w, m_swa_w_q, m_swa_sinks, m_swa_w_o, m_final_norm, v_ffn1_norm, v_ffn1_w_in, v_ffn1_w_out, v_mix_norm, v_ffn2_norm, v_ffn2_w_in, v_ffn2_w_out, v_sb_w_qkv, v_sb_w_o, v_kv_norm, v_kv_w, v_swa_w_q, v_swa_sinks, v_swa_w_o, v_final_norm):
    S, D = x.shape[1], x.shape[2]
    L = ffn1_w_in.shape[0]
    KV = kv_w.shape[1]
    assert L == 2 and swa_sinks.shape == (1, 2 * SWA_Q_GROUPS * KV // (2 * LANES))

    def bf(w):
        return w.astype(BF16)

    def bft(w):
        return jnp.transpose(w).astype(BF16)

    cos_t, sin_t = _rope_tables(S)
    h0 = x.reshape(S, D)
    tgt = loss_target.reshape(S, D)

    win1a_t, = _exchange("gather", [bft(ffn1_w_in[0])], "gather_first_weight")
    sv_a1, (wo1a, wqkv_t, w_sbo) = _ffn_up(
        h0, ffn1_norm[0], win1a_t, "ffn1a_up",
        carry=("gather", [bf(ffn1_w_out[0]), bft(sb_w_qkv[0]), bf(sb_w_o[0])]))
    h1 = _ffn_down(sv_a1[-1], wo1a, h0, "ffn1a")
    hn_a = _rmsnorm(h1, mix_norm[0], "mix_a_norm")
    qkv = _proj(hn_a, wqkv_t, NT, BF16, "sb_qkv")
    k_t, v_t = jnp.transpose(qkv[:, D:2 * D]), jnp.transpose(qkv[:, 2 * D:])
    o_sb, later = _sb_fwd(qkv, v_t, "sb_attn", carry=("gather", [
        bft(ffn2_w_in[0]), bf(ffn2_w_out[0]), bf(kv_w), bft(ffn1_w_in[1]), bf(ffn1_w_out[1]),
        bf(swa_w_q[0]), bf(swa_w_o[0]), bft(ffn2_w_in[1]), bf(ffn2_w_out[1])]))
    win2a_t, wo2a, w_kv, win1b_t, wo1b, w_q, w_swo, win2b_t, wo2b = later
    h2 = _proj(o_sb, w_sbo, NN, F32, "sb_out", res=h1)
    h3, sv_a2 = _ffn_fwd(h2, ffn2_norm[0], win2a_t, wo2a, "ffn2a")
    kvn = _rmsnorm(h3, kv_norm, "kv_norm")
    kv_raw = _proj(kvn, w_kv, NN, F32, "kv_proj")
    kv_rot = _rotary(kv_raw, cos_t, sin_t, KV // (2 * LANES), False, "kv_rope")
    h4, sv_b1 = _ffn_fwd(h3, ffn1_norm[1], win1b_t, wo1b, "ffn1b")
    hn_b = _rmsnorm(h4, mix_norm[1], "mix_b_norm")
    q_raw = _proj(hn_b, w_q, NN, F32, "swa_q")
    q_rot = _rotary(q_raw, cos_t, sin_t, D // LANES, False, "q_rope")
    o_sw = _swa_fwd(q_rot, kv_rot, swa_sinks, "swa_attn")
    h5 = _proj(o_sw, w_swo, NN, F32, "swa_out", res=h4)
    h6, sv_b2 = _ffn_fwd(h5, ffn2_norm[1], win2b_t, wo2b, "ffn2b")
    dh6, dg_final, sq_err = _final_loss(h6, final_norm, tgt, "final_loss")
    loss = lax.psum(0.5 * jnp.sum(sq_err) / D, ("x", "y", "c"))

    dh5, dg_f2b, dwin2b_t, dwo2b = _ffn_bwd(dh6, h5, ffn2_norm[1], win2b_t, wo2b, sv_b2, "ffn2b")
    do_sw = _proj(dh5, w_swo, NT, BF16, "swa_out_dx")
    dw_swo = _proj_dw(o_sw, dh5, "swa_out_dw")
    dq_rot, dk_sw, dv_sw, dsink = _swa_bwd(q_rot, kv_rot, swa_sinks, o_sw, do_sw, "swa_attn_bwd")
    dq = _rotary(dq_rot, cos_t, sin_t, D // LANES, True, "q_rope_bwd")
    dw_q = _proj_dw(hn_b, dq, "swa_q_dw")
    (dh4, dg_mix_b), _ = _dx_norm_bwd([(dq, w_q, NT, 1, 0)], h4, mix_norm[1], dh5, "swa_q_dx")
    dh3, dg_f1b, dwin1b_t, dwo1b = _ffn_bwd(dh4, h3, ffn1_norm[1], win1b_t, wo1b, sv_b1, "ffn1b")
    dkv = _rotary(jnp.concatenate([dk_sw, dv_sw], axis=1), cos_t, sin_t, KV // (2 * LANES), True, "kv_rope_bwd")
    dw_kv = _proj_dw(kvn, dkv, "kv_proj_dw")
    (dh3, dg_kv), _ = _dx_norm_bwd([(dkv, w_kv, NT, 1, 0)], h3, kv_norm, dh3, "kv_proj_dx")
    dh2, dg_f2a, dwin2a_t, dwo2a = _ffn_bwd(dh3, h2, ffn2_norm[0], win2a_t, wo2a, sv_a2, "ffn2a")
    do_sb = _proj(dh2, w_sbo, NT, BF16, "sb_out_dx")
    dw_sbo = _proj_dw(o_sb, dh2, "sb_out_dw")
    (dq_sb, dk_sb, dv_sb), early = _sb_bwd(qkv, k_t, o_sb, do_sb, "sb_attn_bwd", carry=("scatter", [
        dwin2b_t, dwo2b, dw_swo, dw_q, dwin1b_t, dwo1b, dw_kv, dwin2a_t, dwo2a, dw_sbo]))
    dqkv = jnp.concatenate([dq_sb, dk_sb.astype(BF16), dv_sb.astype(BF16)], axis=1)
    dwqkv_t = _proj_dw(dqkv, hn_a, "sb_qkv_dw")
    (dh1, dg_mix_a), (p_qkv,) = _dx_norm_bwd([(dqkv, wqkv_t, NN, 1, 0)], h1, mix_norm[0], dh2, "sb_qkv_dx",
                                             carry=("scatter", [dwqkv_t]))
    dx, dg_f1a, p_win1a, p_wo1a = _ffn_bwd(dh1, h0, ffn1_norm[0], win1a_t, wo1a, sv_a1, "ffn1a", scatter=True)

    p_win2b, p_wo2b, p_swo, p_q, p_win1b, p_wo1b, p_kv, p_win2a, p_wo2a, p_sbo = early

    def natural(parts, tag):
        return _sum8(parts, f"sum_{tag}")

    def from_t(parts, tag):
        return jnp.transpose(_sum8(parts, f"sum_{tag}"))

    grads = {
        "ffn1_w_in": jnp.stack([from_t(p_win1a, "win1a"), from_t(p_win1b, "win1b")]),
        "ffn1_w_out": jnp.stack([natural(p_wo1a, "wo1a"), natural(p_wo1b, "wo1b")]),
        "ffn2_w_in": jnp.stack([from_t(p_win2a, "win2a"), from_t(p_win2b, "win2b")]),
        "ffn2_w_out": jnp.stack([natural(p_wo2a, "wo2a"), natural(p_wo2b, "wo2b")]),
        "sb_w_qkv": from_t(p_qkv, "qkv")[None],
        "sb_w_o": natural(p_sbo, "sbo")[None],
        "kv_w": natural(p_kv, "kv"),
        "swa_w_q": natural(p_q, "swq")[None],
        "swa_w_o": natural(p_swo, "swo")[None],
    }

    small_w = [ffn1_norm, mix_norm, ffn2_norm, kv_norm, final_norm, swa_sinks]
    small_m = [m_ffn1_norm, m_mix_norm, m_ffn2_norm, m_kv_norm, m_final_norm, m_swa_sinks]
    small_v = [v_ffn1_norm, v_mix_norm, v_ffn2_norm, v_kv_norm, v_final_norm, v_swa_sinks]
    SMALL_ROWS = 16

    def pack_small(ts):
        rows_ = [t.reshape(-1, D) for t in ts[:-1]]
        sink_row = jnp.pad(ts[-1].reshape(1, -1), ((0, 0), (0, D - ts[-1].size)))
        flat = jnp.concatenate(rows_ + [sink_row], axis=0)
        return jnp.pad(flat, ((0, SMALL_ROWS - flat.shape[0]), (0, 0)))

    def unpack_small(flat):
        out, r = [], 0
        for t in small_w[:-1]:
            n = t.size // D
            out.append(flat[r:r + n].reshape(t.shape))
            r += n
        out.append(flat[r, :swa_sinks.size].reshape(swa_sinks.shape))
        return out

    def gain(parts8):
        return jnp.sum(parts8, axis=0, keepdims=True)

    g_small_local = pack_small([
        jnp.concatenate([gain(dg_f1a), gain(dg_f1b)], axis=0),
        jnp.concatenate([gain(dg_mix_a), gain(dg_mix_b)], axis=0),
        jnp.concatenate([gain(dg_f2a), gain(dg_f2b)], axis=0),
        gain(dg_kv), gain(dg_final), dsink[:, :, 0].reshape(1, -1)])
    small_parts = _exchange("gather", [g_small_local], "gather_small_grads")[0]
    g_small = _sum8(small_parts.reshape(N_DEV, SMALL_ROWS, D), "sum_small")
    d_small, nm_small, nv_small = _adamw(g_small, pack_small(small_w), pack_small(small_m), pack_small(small_v), "adamw_small")
    small_names = ["ffn1_norm", "mix_norm", "ffn2_norm", "kv_norm", "final_norm", "swa_sinks"]
    result = {"grad": dict(zip(small_names, unpack_small(g_small))),
              "delta": dict(zip(small_names, unpack_small(d_small))),
              "new_m": dict(zip(small_names, unpack_small(nm_small))),
              "new_v": dict(zip(small_names, unpack_small(nv_small)))}

    big = {"ffn1_w_in": (ffn1_w_in, m_ffn1_w_in, v_ffn1_w_in), "ffn1_w_out": (ffn1_w_out, m_ffn1_w_out, v_ffn1_w_out),
           "ffn2_w_in": (ffn2_w_in, m_ffn2_w_in, v_ffn2_w_in), "ffn2_w_out": (ffn2_w_out, m_ffn2_w_out, v_ffn2_w_out),
           "sb_w_qkv": (sb_w_qkv, m_sb_w_qkv, v_sb_w_qkv), "sb_w_o": (sb_w_o, m_sb_w_o, v_sb_w_o),
           "kv_w": (kv_w, m_kv_w, v_kv_w), "swa_w_q": (swa_w_q, m_swa_w_q, v_swa_w_q),
           "swa_w_o": (swa_w_o, m_swa_w_o, v_swa_w_o)}
    for nm, (w, m, v) in big.items():
        g = grads[nm]
        two_d = lambda t: t.reshape(-1, t.shape[-1])
        d, new_m, new_v = _adamw(two_d(g), two_d(w), two_d(m), two_d(v), f"adamw_{nm}")
        result["grad"][nm] = g
        result["delta"][nm] = d.reshape(w.shape)
        result["new_m"][nm] = new_m.reshape(w.shape)
        result["new_v"][nm] = new_v.reshape(w.shape)

    order = ["ffn1_norm", "ffn1_w_in", "ffn1_w_out", "mix_norm", "ffn2_norm", "ffn2_w_in", "ffn2_w_out",
             "sb_w_qkv", "sb_w_o", "kv_norm", "kv_w", "swa_w_q", "swa_sinks", "swa_w_o", "final_norm"]
    outs = [result[kind][nm] for kind in ("grad", "delta", "new_m", "new_v") for nm in order]
    return (loss, dx.reshape(x.shape), *outs)
```

```python
import jax
import jax.numpy as jnp
from jax import lax
from jax.experimental import pallas as pl
from jax.experimental.pallas import tpu as pltpu

F32 = jnp.float32
BF16 = jnp.bfloat16

N_DEV = 8
HEAD_DIM = 64
LANES = 128
BLK = 128
RMS_EPS = 1e-6
FFN_RES_SCALE = 0.5
ROPE_THETA = 10000.0
ATTN_SCALE = HEAD_DIM ** -0.5
SB_LOG_FLOOR = -88.0
NEG_BIG = -1e30
VMEM_LIMIT_V7X = 56 * 1024 * 1024

ADAM_LR = 0.001
ADAM_B1 = 0.9
ADAM_B2 = 0.999
ADAM_EPS = 1e-08
ADAM_WD = 0.01
ADAM_STEP = 10

NN = ((1,), (0,))
NT = ((1,), (1,))
TN = ((0,), (0,))
TN_CHUNK = 2048
MESH = pl.DeviceIdType.MESH


def _dot(a, b, dims):
    return lax.dot_general(a, b, (dims, ((), ())), preferred_element_type=F32)


def _tile(n, pref, mult=LANES):
    if n <= pref:
        return n
    t = (pref // mult) * mult
    while t >= mult:
        if n % t == 0:
            return t
        t -= mult
    return n


def _params(*sem):
    return pltpu.CompilerParams(dimension_semantics=sem, vmem_limit_bytes=VMEM_LIMIT_V7X)


def _mm(a, b, dims, out_dtype, name, scale=1.0, res=None, tm=512, tn=512, tk=512):
    if dims == NN:
        (M, K), (_, N) = a.shape, b.shape
    elif dims == NT:
        (M, K), (N, _) = a.shape, b.shape
    else:
        (K, M), (_, N) = a.shape, b.shape
    tm, tn, tk = _tile(M, tm), _tile(N, tn), _tile(K, tk)
    nk = K // tk
    if dims == TN:
        a_spec = pl.BlockSpec((tk, tm), lambda i, j, k: (k, i))
    else:
        a_spec = pl.BlockSpec((tm, tk), lambda i, j, k: (i, k))
    if dims == NT:
        b_spec = pl.BlockSpec((tn, tk), lambda i, j, k: (j, k))
    else:
        b_spec = pl.BlockSpec((tk, tn), lambda i, j, k: (k, j))
    o_spec = pl.BlockSpec((tm, tn), lambda i, j, k: (i, j))
    has_res = res is not None

    def body(*refs):
        a_ref, b_ref = refs[0], refs[1]
        r_ref = refs[2] if has_res else None
        o_ref = refs[3] if has_res else refs[2]

        def finish(acc):
            r = acc * scale if scale != 1.0 else acc
            if has_res:
                r = r + r_ref[...]
            o_ref[...] = r.astype(out_dtype)

        p = _dot(a_ref[...].astype(BF16), b_ref[...].astype(BF16), dims)
        if nk == 1:
            finish(p)
        else:
            acc_ref = refs[-1]
            k = pl.program_id(2)

            @pl.when(k == 0)
            def _():
                acc_ref[...] = p

            @pl.when(k > 0)
            def _():
                acc_ref[...] += p

            @pl.when(k == nk - 1)
            def _():
                finish(acc_ref[...])

    in_specs = [a_spec, b_spec] + ([o_spec] if has_res else [])
    args = (a, b) + ((res,) if has_res else ())
    return pl.pallas_call(
        body, name=name,
        out_shape=jax.ShapeDtypeStruct((M, N), out_dtype),
        grid=(M // tm, N // tn, nk),
        in_specs=in_specs, out_specs=o_spec,
        scratch_shapes=[pltpu.VMEM((tm, tn), F32)] if nk > 1 else [],
        compiler_params=_params("parallel", "parallel", "arbitrary"),
    )(*args)


def _rows8(x):
    r, d = x.shape
    return jnp.sum(x.reshape(r // 8, 8, d), axis=0)


def _rmsnorm(h, g, name):
    S, D = h.shape
    ts = _tile(S, 512, 8)

    def body(h_ref, g_ref, o_ref):
        x = h_ref[...]
        r = lax.rsqrt(jnp.mean(x * x, axis=-1, keepdims=True) + RMS_EPS)
        o_ref[...] = ((x * r) * g_ref[...]).astype(BF16)

    return pl.pallas_call(
        body, name=name,
        out_shape=jax.ShapeDtypeStruct((S, D), BF16),
        grid=(S // ts,),
        in_specs=[pl.BlockSpec((ts, D), lambda i: (i, 0)), pl.BlockSpec((1, D), lambda i: (0, 0))],
        out_specs=pl.BlockSpec((ts, D), lambda i: (i, 0)),
        compiler_params=_params("parallel"),
    )(h, g.reshape(1, D))


def _final_loss(h, g, tgt, name):
    S, D = h.shape
    ts = _tile(S, 512, 8)

    def body(h_ref, g_ref, t_ref, dh_ref, dg_ref, l_ref):
        x = h_ref[...]
        r = lax.rsqrt(jnp.mean(x * x, axis=-1, keepdims=True) + RMS_EPS)
        xhat = x * r
        err = xhat * g_ref[...] - t_ref[...]
        d = err * (1.0 / D)
        dxh = d * g_ref[...]
        c = jnp.mean(dxh * xhat, axis=-1, keepdims=True)
        dh_ref[...] = r * (dxh - xhat * c)
        part = _rows8(d * xhat)
        lpart = _rows8(err * err)

        @pl.when(pl.program_id(0) == 0)
        def _():
            dg_ref[...] = part
            l_ref[...] = lpart

        @pl.when(pl.program_id(0) > 0)
        def _():
            dg_ref[...] += part
            l_ref[...] += lpart

    row = pl.BlockSpec((ts, D), lambda i: (i, 0))
    acc = pl.BlockSpec((8, D), lambda i: (0, 0))
    return pl.pallas_call(
        body, name=name,
        out_shape=(jax.ShapeDtypeStruct((S, D), F32), jax.ShapeDtypeStruct((8, D), F32),
                   jax.ShapeDtypeStruct((8, D), F32)),
        grid=(S // ts,),
        in_specs=[row, pl.BlockSpec((1, D), lambda i: (0, 0)), row],
        out_specs=(row, acc, acc),
        compiler_params=_params("arbitrary"),
    )(h, g.reshape(1, D), tgt)


def _ffn_up(h, g, win_t, name, carry=None):
    S, D = h.shape
    F = win_t.shape[0] // 2
    tm, tn = _tile(S, 512, 16), _tile(F, 1408)
    nf = F // tn

    def body(h_ref, g_ref, wg_ref, wu_ref, xn_ref, silu_ref, dsilu_ref, up_ref, act_ref):
        x = h_ref[...]
        r = lax.rsqrt(jnp.mean(x * x, axis=-1, keepdims=True) + RMS_EPS)
        xn = ((x * r) * g_ref[...]).astype(BF16)
        xn_ref[...] = xn
        gate = _dot(xn, wg_ref[...], NT)
        up = _dot(xn, wu_ref[...], NT)
        sig = 1.0 / (1.0 + jnp.exp(-gate))
        silu = gate * sig
        up_ref[...] = up.astype(BF16)
        silu_ref[...] = silu.astype(BF16)
        dsilu_ref[...] = (sig + silu * (1.0 - sig)).astype(BF16)
        act_ref[...] = (silu * up).astype(BF16)

    row = pl.BlockSpec((tm, D), lambda i, j: (i, 0))
    blk = pl.BlockSpec((tm, tn), lambda i, j: (i, j))
    hid = jax.ShapeDtypeStruct((S, F), BF16)
    return _pcall(
        body, (h, g.reshape(1, D), win_t, win_t), name=name,
        out_shape=(jax.ShapeDtypeStruct((S, D), BF16), hid, hid, hid, hid),
        grid=(S // tm, nf),
        in_specs=[row, pl.BlockSpec((1, D), lambda i, j: (0, 0)),
                  pl.BlockSpec((tn, D), lambda i, j: (j, 0)),
                  pl.BlockSpec((tn, D), lambda i, j: (j + nf, 0))],
        out_specs=(row, blk, blk, blk, blk),
        sem=("arbitrary", "arbitrary"), carry=carry)


def _ffn_dact(dh, wo, silu, dsilu, up, name):
    S, D = dh.shape
    F = wo.shape[0]
    tm, tn = _tile(S, 512, 16), _tile(F, 1408)

    def body(dh_ref, wo_ref, s_ref, ds_ref, u_ref, dg_ref, du_ref):
        d = _dot(dh_ref[...].astype(BF16), wo_ref[...], NT) * FFN_RES_SCALE
        du_ref[...] = (d * s_ref[...].astype(F32)).astype(BF16)
        dg_ref[...] = (d * u_ref[...].astype(F32) * ds_ref[...].astype(F32)).astype(BF16)

    blk = pl.BlockSpec((tm, tn), lambda j, i: (i, j))
    hid = jax.ShapeDtypeStruct((S, F), BF16)
    return pl.pallas_call(
        body, name=name, out_shape=(hid, hid),
        grid=(F // tn, S // tm),
        in_specs=[pl.BlockSpec((tm, D), lambda j, i: (i, 0)), pl.BlockSpec((tn, D), lambda j, i: (j, 0)),
                  blk, blk, blk],
        out_specs=(blk, blk),
        compiler_params=_params("arbitrary", "arbitrary"),
    )(dh, wo, silu, dsilu, up)


def _dwin(dgate, dup, xn, name, carry=None):
    S, F = dgate.shape
    D = xn.shape[1]
    tr, tk = _tile(F, 1408), _tile(S, TN_CHUNK, 16)
    nf, nk = F // tr, S // tk

    def body(dg_ref, du_ref, x_ref, o_ref, acc_ref):
        r, k = pl.program_id(0), pl.program_id(1)

        def accumulate(a_ref):
            p = _dot(a_ref[...], x_ref[...], TN)

            @pl.when(k == 0)
            def _():
                acc_ref[...] = p

            @pl.when(k > 0)
            def _():
                acc_ref[...] += p

        @pl.when(r < nf)
        def _():
            accumulate(dg_ref)

        @pl.when(r >= nf)
        def _():
            accumulate(du_ref)

        @pl.when(k == nk - 1)
        def _():
            o_ref[...] = acc_ref[...].astype(BF16)

    return _pcall(
        body, (dgate, dup, xn), name=name, out_shape=jax.ShapeDtypeStruct((2 * F, D), BF16),
        grid=(2 * nf, nk),
        in_specs=[pl.BlockSpec((tk, tr), lambda r, k: (jnp.where(r < nf, k, 0), jnp.minimum(r, nf - 1))),
                  pl.BlockSpec((tk, tr), lambda r, k: (jnp.where(r >= nf, k, 0), jnp.maximum(r - nf, 0))),
                  pl.BlockSpec((tk, D), lambda r, k: (k, 0))],
        out_specs=pl.BlockSpec((tr, D), lambda r, k: (r, 0)),
        scratch_shapes=[pltpu.VMEM((tr, D), F32)],
        sem=("arbitrary", "arbitrary"), carry=carry)


def _dx_norm_bwd(terms, h, g, res, name, carry=None):
    S, D = h.shape
    tm = _tile(S, 256, 16)
    n = len(terms)

    def body(*refs):
        dy_refs, w_refs = refs[:n], refs[n:2 * n]
        h_ref, g_ref, r_ref, dh_ref, dg_ref = refs[2 * n:]
        d = _dot(dy_refs[0][...], w_refs[0][...], terms[0][2])
        for t in range(1, n):
            d = d + _dot(dy_refs[t][...], w_refs[t][...], terms[t][2])
        x = h_ref[...]
        r = lax.rsqrt(jnp.mean(x * x, axis=-1, keepdims=True) + RMS_EPS)
        xhat = x * r
        dxh = d * g_ref[...]
        c = jnp.mean(dxh * xhat, axis=-1, keepdims=True)
        dh_ref[...] = r * (dxh - xhat * c) + r_ref[...]
        part = _rows8(d * xhat)

        @pl.when(pl.program_id(0) == 0)
        def _():
            dg_ref[...] = part

        @pl.when(pl.program_id(0) > 0)
        def _():
            dg_ref[...] += part

    def w_spec(w, nblk, blk):
        return pl.BlockSpec((w.shape[0] // nblk, w.shape[1]), lambda i: (blk, 0))

    row = pl.BlockSpec((tm, D), lambda i: (i, 0))
    in_specs = [pl.BlockSpec((tm, t[0].shape[1]), lambda i: (i, 0)) for t in terms]
    in_specs += [w_spec(t[1], t[3], t[4]) for t in terms]
    in_specs += [row, pl.BlockSpec((1, D), lambda i: (0, 0)), row]
    return _pcall(
        body, (*[t[0] for t in terms], *[t[1] for t in terms], h, g.reshape(1, D), res), name=name,
        out_shape=(jax.ShapeDtypeStruct((S, D), F32), jax.ShapeDtypeStruct((8, D), F32)),
        grid=(S // tm,),
        in_specs=in_specs,
        out_specs=(row, pl.BlockSpec((8, D), lambda i: (0, 0))),
        sem=("arbitrary",), carry=carry)


def _load_resident(pairs, sems):
    @pl.when(pl.program_id(0) == 0)
    def _():
        copies = [pltpu.make_async_copy(src, dst, sems.at[n]) for n, (src, dst) in enumerate(pairs)]
        for cp in copies:
            cp.start()
        for cp in copies:
            cp.wait()


def _ffn_fwd_fused(h, g, win_t, wo, name, carry=None):
    S, D = h.shape
    F = wo.shape[0]
    tm = _tile(S, 256, 16)

    def body(h_ref, g_ref, win_hbm, wo_hbm, out_ref, xn_ref, silu_ref, dsilu_ref, up_ref, act_ref, win_v, wo_v, sems):
        _load_resident([(win_hbm, win_v), (wo_hbm, wo_v)], sems)
        x = h_ref[...]
        r = lax.rsqrt(jnp.mean(x * x, axis=-1, keepdims=True) + RMS_EPS)
        xn = ((x * r) * g_ref[...]).astype(BF16)
        xn_ref[...] = xn
        gate = _dot(xn, win_v[:F, :], NT)
        up = _dot(xn, win_v[F:, :], NT)
        sig = 1.0 / (1.0 + jnp.exp(-gate))
        silu = gate * sig
        act = (silu * up).astype(BF16)
        up_ref[...] = up.astype(BF16)
        silu_ref[...] = silu.astype(BF16)
        dsilu_ref[...] = (sig + silu * (1.0 - sig)).astype(BF16)
        act_ref[...] = act
        out_ref[...] = x + FFN_RES_SCALE * _dot(act, wo_v[...], NN)

    row = pl.BlockSpec((tm, D), lambda i: (i, 0))
    wide = pl.BlockSpec((tm, F), lambda i: (i, 0))
    hbm = pl.BlockSpec(memory_space=pl.ANY)
    hid = jax.ShapeDtypeStruct((S, F), BF16)
    res, got = _pcall(
        body, (h, g.reshape(1, D), win_t, wo), name=name,
        out_shape=(jax.ShapeDtypeStruct((S, D), F32), jax.ShapeDtypeStruct((S, D), BF16), hid, hid, hid, hid),
        grid=(S // tm,),
        in_specs=[row, pl.BlockSpec((1, D), lambda i: (0, 0)), hbm, hbm],
        out_specs=(row, row, wide, wide, wide, wide),
        scratch_shapes=[pltpu.VMEM(win_t.shape, BF16), pltpu.VMEM(wo.shape, BF16), pltpu.SemaphoreType.DMA((2,))],
        sem=("arbitrary",), carry=carry)
    return res[0], tuple(res[1:]), got


def _ffn_bwd_fused(dh, h, g, win_t, wo, silu, dsilu, up, name, carry=None):
    S, D = h.shape
    F = wo.shape[0]
    tm = _tile(S, 256, 16)

    def body(dh_ref, h_ref, g_ref, s_ref, ds_ref, u_ref, win_hbm, wo_hbm,
             dhin_ref, dgain_ref, dgate_ref, dup_ref, win_v, wo_v, sems):
        _load_resident([(win_hbm, win_v), (wo_hbm, wo_v)], sems)
        dhv = dh_ref[...]
        d = _dot(dhv.astype(BF16), wo_v[...], NT) * FFN_RES_SCALE
        dup = (d * s_ref[...].astype(F32)).astype(BF16)
        dgate = (d * u_ref[...].astype(F32) * ds_ref[...].astype(F32)).astype(BF16)
        dup_ref[...] = dup
        dgate_ref[...] = dgate
        dxn = _dot(dgate, win_v[:F, :], NN) + _dot(dup, win_v[F:, :], NN)
        x = h_ref[...]
        r = lax.rsqrt(jnp.mean(x * x, axis=-1, keepdims=True) + RMS_EPS)
        xhat = x * r
        dxh = dxn * g_ref[...]
        c = jnp.mean(dxh * xhat, axis=-1, keepdims=True)
        dhin_ref[...] = r * (dxh - xhat * c) + dhv
        part = _rows8(dxn * xhat)

        @pl.when(pl.program_id(0) == 0)
        def _():
            dgain_ref[...] = part

        @pl.when(pl.program_id(0) > 0)
        def _():
            dgain_ref[...] += part

    row = pl.BlockSpec((tm, D), lambda i: (i, 0))
    wide = pl.BlockSpec((tm, F), lambda i: (i, 0))
    hbm = pl.BlockSpec(memory_space=pl.ANY)
    hid = jax.ShapeDtypeStruct((S, F), BF16)
    return _pcall(
        body, (dh, h, g.reshape(1, D), silu, dsilu, up, win_t, wo), name=name,
        out_shape=(jax.ShapeDtypeStruct((S, D), F32), jax.ShapeDtypeStruct((8, D), F32), hid, hid),
        grid=(S // tm,),
        in_specs=[row, row, pl.BlockSpec((1, D), lambda i: (0, 0)), wide, wide, wide, hbm, hbm],
        out_specs=(row, pl.BlockSpec((8, D), lambda i: (0, 0)), wide, wide),
        scratch_shapes=[pltpu.VMEM(win_t.shape, BF16), pltpu.VMEM(wo.shape, BF16), pltpu.SemaphoreType.DMA((2,))],
        sem=("arbitrary",), carry=carry)


def _rope_tables(S):
    half = HEAD_DIM // 2
    inv_freq = ROPE_THETA ** (-jnp.arange(half, dtype=F32) / half)
    ang = jnp.arange(S).astype(F32)[:, None] * inv_freq[None, :]
    cos, sin = jnp.cos(ang), jnp.sin(ang)
    cos_t = jnp.tile(cos, (1, LANES // half))
    sin_t = jnp.tile(jnp.concatenate([-sin, sin], axis=1), (1, LANES // HEAD_DIM))
    return cos_t, sin_t


def _swap_halves(x):
    lane = lax.broadcasted_iota(jnp.int32, x.shape, 1)
    first = (lane % HEAD_DIM) < (HEAD_DIM // 2)
    return jnp.where(first, pltpu.roll(x, LANES - HEAD_DIM // 2, 1), pltpu.roll(x, HEAD_DIM // 2, 1))


def _rotary(x, cos_t, sin_t, n_rot, inverse, name):
    S, C = x.shape
    ts = _tile(S, 512, 16)
    ng = C // LANES

    def body(x_ref, c_ref, s_ref, o_ref):
        cs, sn = c_ref[...], s_ref[...]
        for gidx in range(ng):
            sl = slice(gidx * LANES, (gidx + 1) * LANES)
            v = x_ref[:, sl].astype(F32)
            if gidx < n_rot:
                if inverse:
                    v = v * cs + _swap_halves(v * sn)
                else:
                    v = v * cs + _swap_halves(v) * sn
            o_ref[:, sl] = v.astype(BF16)

    row = pl.BlockSpec((ts, C), lambda i: (i, 0))
    tab = pl.BlockSpec((ts, LANES), lambda i: (i, 0))
    return pl.pallas_call(
        body, name=name, out_shape=jax.ShapeDtypeStruct((S, C), BF16),
        grid=(S // ts,), in_specs=[row, tab, tab], out_specs=row,
        compiler_params=_params("parallel"),
    )(x, cos_t, sin_t)


def _head_masks():
    lane = lax.broadcasted_iota(jnp.int32, (BLK, LANES), 1)
    return lane < HEAD_DIM


def _split_bf16(x):
    hi = x.astype(BF16)
    lo = (x - hi.astype(F32)).astype(BF16)
    return hi, lo


def _sb_scores(qh, ks, carry, diag, tri_excl, strict):
    n_heads = len(qh)
    zs = [_dot(ks[n], qh[n], NT) for n in range(n_heads)]
    a_l, b_l, split_l = [], [], []
    for z in zs:
        a = jnp.minimum(z, 0.0) - jnp.log(1.0 + jnp.exp(-jnp.abs(z)))
        b = a - z
        if diag:
            b = jnp.where(strict, b, 0.0)
        a_l.append(a)
        b_l.append(b)
        split_l.append(_split_bf16(b))
    sufs = [_dot(tri_excl, hi, NN) + _dot(tri_excl, lo, NN) for hi, lo in split_l]
    w_l = []
    for n in range(n_heads):
        w = jnp.exp(a_l[n] + sufs[n] + carry[n])
        if diag:
            w = jnp.where(strict, w, 0.0)
        w_l.append(w)
    return a_l, b_l, w_l


SB_FWD_PAIRS = 4
SB_BWD_PAIRS = 2
SB_BWD_QBLOCKS = 2


def _any_alive(carries):
    top = carries[0]
    for c in carries[1:]:
        top = jnp.maximum(top, c)
    return (jnp.max(top) > SB_LOG_FLOOR).astype(jnp.int32)


def _sb_masks():
    row = lax.broadcasted_iota(jnp.int32, (BLK, BLK), 0)
    col = lax.broadcasted_iota(jnp.int32, (BLK, BLK), 1)
    tri_excl = jnp.where(col > row, 1.0, 0.0).astype(BF16)
    tri_incl = jnp.where(col >= row, 1.0, 0.0).astype(BF16)
    return row < HEAD_DIM, row < col, tri_excl, tri_incl


def _sb_fwd(qkv, v_t, name, carry=None):
    S, D3 = qkv.shape
    D = D3 // 3
    npair, nb = D // LANES, S // BLK
    P = min(SB_FWD_PAIRS, npair)
    ngroup = npair // P
    W = P * LANES

    def body(q_ref, k_ref, vt_ref, o_ref):
        i = pl.program_id(1)
        m0 = _head_masks()
        top, strict, tri_excl, _ = _sb_masks()
        zq = jnp.zeros((BLK, LANES), BF16)
        lanes = [slice(p * LANES, (p + 1) * LANES) for p in range(P)]
        qh = []
        for sl in lanes:
            q2 = q_ref[:, sl] * ATTN_SCALE
            qh += [jnp.where(m0, q2, zq), jnp.where(m0, zq, q2)]

        def block(j, carry, acc, diag):
            off = pl.multiple_of(j * BLK, BLK)
            ks, vth = [], []
            for sl in lanes:
                k2 = k_ref[pl.ds(off, BLK), sl]
                vt = vt_ref[sl, pl.ds(off, BLK)]
                ks += [k2, k2]
                vth += [jnp.where(top, vt, zq), jnp.where(top, zq, vt)]
            _, b_l, w_l = _sb_scores(qh, ks, carry, diag, tri_excl, strict)
            wb = [w.astype(BF16) for w in w_l]
            new_acc = [acc[p] + _dot(vth[2 * p], wb[2 * p], NN) + _dot(vth[2 * p + 1], wb[2 * p + 1], NN)
                       for p in range(P)]
            new_carry = [carry[n] + jnp.sum(b_l[n], axis=0, keepdims=True) for n in range(2 * P)]
            return new_carry, new_acc

        c0 = jnp.zeros((1, BLK), F32)
        carry, acc = block(i, [c0] * (2 * P), [jnp.zeros((LANES, BLK), F32)] * P, True)

        def cond(st):
            return jnp.logical_and(st[0] >= 0, st[1] > 0)

        def step(st):
            j, _, carry, acc = st
            carry, acc = block(j, carry, acc, False)
            return j - 1, _any_alive(carry), carry, acc

        st = lax.while_loop(cond, step, (i - 1, _any_alive(carry), carry, acc))
        for p, sl in enumerate(lanes):
            o_ref[:, sl] = jnp.transpose(st[3][p])

    return _pcall(
        body, (qkv, qkv, v_t), name=name, out_shape=jax.ShapeDtypeStruct((S, D), F32),
        grid=(ngroup, nb),
        in_specs=[pl.BlockSpec((BLK, W), lambda g, i: (i, g)),
                  pl.BlockSpec((S, W), lambda g, i: (0, ngroup + g)),
                  pl.BlockSpec((W, S), lambda g, i: (g, 0))],
        out_specs=pl.BlockSpec((BLK, W), lambda g, i: (i, g)),
        sem=("arbitrary", "arbitrary"), carry=carry)


def _sb_bwd(qkv, k_t, o, do, name, carry=None):
    S, D3 = qkv.shape
    D = D3 // 3
    npair, nb = D // LANES, S // BLK
    P = min(SB_BWD_PAIRS, npair)
    ngroup = npair // P
    W = P * LANES

    QB = SB_BWD_QBLOCKS if nb % SB_BWD_QBLOCKS == 0 else 1
    nch = QB * 2 * P

    def body(q_ref, o_ref, do_ref, qkv_hbm, kt_hbm, dq_ref, dk_ref, dv_ref, k_ref, v_ref, kt_ref, sems):
        grp = pl.program_id(0)
        i_first = pl.program_id(1) * QB
        m0 = _head_masks()
        top, strict, tri_excl, tri_incl = _sb_masks()
        zq = jnp.zeros((BLK, LANES), BF16)
        lanes = [slice(p * LANES, (p + 1) * LANES) for p in range(P)]

        @pl.when(pl.program_id(1) == 0)
        def _():
            copies = [pltpu.make_async_copy(qkv_hbm.at[:, pl.ds(pl.multiple_of((c * ngroup + grp) * W, LANES), W)],
                                            ref, sems.at[c - 1]) for c, ref in ((1, k_ref), (2, v_ref))]
            copies.append(pltpu.make_async_copy(kt_hbm.at[pl.ds(pl.multiple_of(grp * W, LANES), W), :],
                                                kt_ref, sems.at[2]))
            for cp in copies:
                cp.start()
            dk_ref[...] = jnp.zeros_like(dk_ref)
            dv_ref[...] = jnp.zeros_like(dv_ref)
            for cp in copies:
                cp.wait()

        qh, doh, delta = [], [], []
        for qb in range(QB):
            rs = slice(qb * BLK, (qb + 1) * BLK)
            for sl in lanes:
                q2, do2 = q_ref[rs, sl] * ATTN_SCALE, do_ref[rs, sl]
                qh += [jnp.where(m0, q2, zq), jnp.where(m0, zq, q2)]
                doh += [jnp.where(m0, do2, zq), jnp.where(m0, zq, do2)]
                prod_t = jnp.transpose(do2.astype(F32) * o_ref[rs, sl])
                delta += [jnp.sum(jnp.where(top, prod_t, 0.0), axis=0, keepdims=True),
                          jnp.sum(jnp.where(top, 0.0, prod_t), axis=0, keepdims=True)]

        def block(js, valid, cb, cg, dq, diag):
            offs = [pl.multiple_of(j * BLK, BLK) for j in js]
            ks, vs, kth = [], [], []
            for qb in range(QB):
                for sl in lanes:
                    k2, v2 = k_ref[pl.ds(offs[qb], BLK), sl], v_ref[pl.ds(offs[qb], BLK), sl]
                    ks += [k2, k2]
                    vs += [v2, v2]
                    kt = kt_ref[sl, pl.ds(offs[qb], BLK)] * ATTN_SCALE
                    kth += [jnp.where(top, kt, zq), jnp.where(top, zq, kt)]
            dws = [_dot(vs[n], doh[n], NT) for n in range(nch)]
            a_l, b_l, w_l = _sb_scores(qh, ks, cb, diag, tri_excl, strict)
            wb = [w.astype(BF16) for w in w_l]
            g_l = [dws[n] * wb[n].astype(F32) for n in range(nch)]
            gsplit = [_split_bf16(g) for g in g_l]
            gincs = [_dot(tri_incl, hi, NN) + _dot(tri_incl, lo, NN) for hi, lo in gsplit]
            dzs = []
            for n in range(nch):
                beta = jnp.exp(a_l[n])
                dz = g_l[n] - beta * (g_l[n] + ((delta[n] - cg[n]) - gincs[n]))
                if diag:
                    dz = jnp.where(strict, dz, 0.0)
                if valid[n // (2 * P)] is not None:
                    dz = jnp.where(valid[n // (2 * P)], dz, 0.0)
                dzs.append(dz.astype(BF16))
            ndq = []
            for qb in range(QB):
                for p, sl in enumerate(lanes):
                    n0 = qb * 2 * P + 2 * p
                    ndq.append(dq[qb * P + p] + _dot(kth[n0], dzs[n0], NN) + _dot(kth[n0 + 1], dzs[n0 + 1], NN))
                    dk_ref[pl.ds(offs[qb], BLK), sl] += _dot(dzs[n0], qh[n0], NN) + _dot(dzs[n0 + 1], qh[n0 + 1], NN)
                    dv_ref[pl.ds(offs[qb], BLK), sl] += _dot(wb[n0], doh[n0], NN) + _dot(wb[n0 + 1], doh[n0 + 1], NN)
            ncb = [cb[n] + jnp.sum(b_l[n], axis=0, keepdims=True) for n in range(nch)]
            ncg = [cg[n] + jnp.sum(g_l[n], axis=0, keepdims=True) for n in range(nch)]
            return ncb, ncg, ndq

        c0 = jnp.zeros((1, BLK), F32)
        cb, cg, dq = block([i_first + qb for qb in range(QB)], [None] * QB, [c0] * nch, [c0] * nch,
                           [jnp.zeros((LANES, BLK), F32)] * (QB * P), True)

        def cond(st):
            return jnp.logical_and(i_first + QB - 1 - st[0] >= 0, st[1] > 0)

        def step(st):
            t, _, cb, cg, dq = st
            js = [i_first + qb - t for qb in range(QB)]
            valid = [js[qb] >= 0 for qb in range(QB - 1)] + [None]
            cb = [cb[n] if valid[n // (2 * P)] is None else jnp.where(valid[n // (2 * P)], cb[n], NEG_BIG)
                  for n in range(nch)]
            cb, cg, dq = block([jnp.maximum(j, 0) for j in js], valid, cb, cg, dq, False)
            return t + 1, _any_alive(cb), cb, cg, dq

        st = lax.while_loop(cond, step, (1, _any_alive(cb), cb, cg, dq))
        for qb in range(QB):
            for p, sl in enumerate(lanes):
                dq_ref[qb * BLK:(qb + 1) * BLK, sl] = jnp.transpose(st[4][qb * P + p]).astype(BF16)

    blk = pl.BlockSpec((QB * BLK, W), lambda g, i: (i, g))
    col_all = pl.BlockSpec((S, W), lambda g, i: (0, g))
    hbm = pl.BlockSpec(memory_space=pl.ANY)
    return _pcall(
        body, (qkv, o, do, qkv, k_t), name=name,
        out_shape=(jax.ShapeDtypeStruct((S, D), BF16), jax.ShapeDtypeStruct((S, D), F32),
                   jax.ShapeDtypeStruct((S, D), F32)),
        grid=(ngroup, nb // QB),
        in_specs=[blk, blk, blk, hbm, hbm],
        out_specs=(blk, col_all, col_all),
        scratch_shapes=[pltpu.VMEM((S, W), BF16), pltpu.VMEM((S, W), BF16), pltpu.VMEM((W, S), BF16),
                        pltpu.SemaphoreType.DMA((3,))],
        sem=("arbitrary", "arbitrary"), carry=carry)


SWA_Q_GROUPS = 4


def _roll_heads(x):
    return pltpu.roll(x.astype(F32), HEAD_DIM, 1).astype(BF16)


def _swa_valid(i):
    r = lax.broadcasted_iota(jnp.int32, (BLK, 2 * BLK), 0)
    c = lax.broadcasted_iota(jnp.int32, (BLK, 2 * BLK), 1)
    diff = r + BLK - c
    return (diff >= 0) & (diff < BLK) & ((i > 0) | (c >= BLK))


def _swa_probs(z, valid, sink):
    z = jnp.where(valid, z * ATTN_SCALE, NEG_BIG)
    mx = jnp.maximum(jnp.max(z, axis=1, keepdims=True), sink)
    p = jnp.exp(z - mx)
    ps = jnp.exp(sink - mx)
    inv = 1.0 / (jnp.sum(p, axis=1, keepdims=True) + ps)
    return p * inv, ps * inv


def _swa_operands(q_ref, kc_ref, kp_ref, vc_ref, vp_ref, s_ref, m):
    m0 = _head_masks()
    m0k = jnp.concatenate([m0, m0], axis=0)
    kk = jnp.concatenate([kp_ref[...], kc_ref[...]], axis=0)
    vv = jnp.concatenate([vp_ref[...], vc_ref[...]], axis=0)
    ksw, vsw = _roll_heads(kk), _roll_heads(vv)
    zk = jnp.zeros_like(kk)
    heads = []
    for c in range(SWA_Q_GROUPS):
        qc = q_ref[:, c * LANES:(c + 1) * LANES]
        zq = jnp.zeros_like(qc)
        for u in range(2):
            same = u == c // 2
            sel = (lambda x, z, mk: jnp.where(mk, x, z)) if u == 0 else (lambda x, z, mk: jnp.where(mk, z, x))
            heads.append(dict(
                c=c, same=same, sel=sel,
                qm=sel(qc, zq, m0),
                k=kk if same else ksw, v=vv if same else vsw,
                km=sel(kk if same else ksw, zk, m0k), vm=sel(vv if same else vsw, zk, m0k),
                sink=s_ref[0, m * 2 * SWA_Q_GROUPS + 2 * c + u]))
    return heads, m0


def _swa_fwd(q, kv, sinks, name):
    S, D = q.shape
    nkvp = kv.shape[1] // (2 * LANES)
    nb = S // BLK
    qw = SWA_Q_GROUPS * LANES

    def body(q_ref, kc_ref, kp_ref, vc_ref, vp_ref, s_ref, o_ref):
        m, i = pl.program_id(0), pl.program_id(1)
        valid = _swa_valid(i)
        heads, _ = _swa_operands(q_ref, kc_ref, kp_ref, vc_ref, vp_ref, s_ref, m)
        zs = [_dot(hd["qm"], hd["k"], NT) for hd in heads]
        ps = [_swa_probs(z, valid, hd["sink"])[0].astype(BF16) for z, hd in zip(zs, heads)]
        for c in range(SWA_Q_GROUPS):
            o_ref[:, c * LANES:(c + 1) * LANES] = (_dot(ps[2 * c], heads[2 * c]["vm"], NN)
                                                   + _dot(ps[2 * c + 1], heads[2 * c + 1]["vm"], NN))

    prev = lambda i: jnp.maximum(i - 1, 0)
    return pl.pallas_call(
        body, name=name, out_shape=jax.ShapeDtypeStruct((S, D), F32),
        grid=(nkvp, nb),
        in_specs=[pl.BlockSpec((BLK, qw), lambda m, i: (i, m)),
                  pl.BlockSpec((BLK, LANES), lambda m, i: (i, m)),
                  pl.BlockSpec((BLK, LANES), lambda m, i: (prev(i), m)),
                  pl.BlockSpec((BLK, LANES), lambda m, i: (i, nkvp + m)),
                  pl.BlockSpec((BLK, LANES), lambda m, i: (prev(i), nkvp + m)),
                  pl.BlockSpec(memory_space=pltpu.SMEM)],
        out_specs=pl.BlockSpec((BLK, qw), lambda m, i: (i, m)),
        compiler_params=_params("arbitrary", "arbitrary"),
    )(q, kv, kv, kv, kv, sinks)


def _swa_bwd(q, kv, sinks, o, do, name, carry=None):
    S, D = q.shape
    nkvp = kv.shape[1] // (2 * LANES)
    nb = S // BLK
    qw = SWA_Q_GROUPS * LANES
    nh = 2 * SWA_Q_GROUPS

    def body(q_ref, kc_ref, kp_ref, vc_ref, vp_ref, s_ref, o_ref, do_ref, dq_ref, dk_ref, dv_ref, ds_ref):
        m, i = pl.program_id(0), pl.program_id(1)
        valid = _swa_valid(i)
        heads, m0 = _swa_operands(q_ref, kc_ref, kp_ref, vc_ref, vp_ref, s_ref, m)

        @pl.when(i == 0)
        def _():
            dk_ref[...] = jnp.zeros_like(dk_ref)
            dv_ref[...] = jnp.zeros_like(dv_ref)
            ds_ref[...] = jnp.zeros_like(ds_ref)

        doms, deltas = [], []
        for hd in heads:
            c = hd["c"]
            doc = do_ref[:, c * LANES:(c + 1) * LANES]
            prod = doc.astype(F32) * o_ref[:, c * LANES:(c + 1) * LANES]
            doms.append(hd["sel"](doc, jnp.zeros_like(doc), m0))
            deltas.append(jnp.sum(hd["sel"](prod, 0.0, m0), axis=1, keepdims=True))
        zs = [_dot(hd["qm"], hd["k"], NT) for hd in heads]
        dps = [_dot(dom, hd["v"], NT) for dom, hd in zip(doms, heads)]
        pbs, dscs = [], []
        for n, hd in enumerate(heads):
            p, psink = _swa_probs(zs[n], valid, hd["sink"])
            pbs.append(p.astype(BF16))
            dscs.append((p * (dps[n] - deltas[n]) * ATTN_SCALE).astype(BF16))
            dsink = jnp.sum(jnp.broadcast_to(-(psink * deltas[n]), (BLK, LANES)), axis=0, keepdims=True)
            ds_ref[0, n:n + 1, :] += dsink
        for c in range(SWA_Q_GROUPS):
            dq_ref[:, c * LANES:(c + 1) * LANES] = (_dot(dscs[2 * c], heads[2 * c]["km"], NN)
                                                    + _dot(dscs[2 * c + 1], heads[2 * c + 1]["km"], NN))
        acc = {}
        for n, hd in enumerate(heads):
            dk_n = _dot(dscs[n], hd["qm"], TN)
            dv_n = _dot(pbs[n], doms[n], TN)
            for key, val in ((("k", hd["same"]), dk_n), (("v", hd["same"]), dv_n)):
                acc[key] = val if key not in acc else acc[key] + val
        dkk = acc["k", True] + pltpu.roll(acc["k", False], HEAD_DIM, 1)
        dvv = acc["v", True] + pltpu.roll(acc["v", False], HEAD_DIM, 1)
        poff = pl.multiple_of(jnp.maximum(i - 1, 0) * BLK, BLK)
        coff = pl.multiple_of(i * BLK, BLK)
        dk_ref[pl.ds(poff, BLK), :] += dkk[:BLK]
        dv_ref[pl.ds(poff, BLK), :] += dvv[:BLK]
        dk_ref[pl.ds(coff, BLK), :] += dkk[BLK:]
        dv_ref[pl.ds(coff, BLK), :] += dvv[BLK:]

    prev = lambda i: jnp.maximum(i - 1, 0)
    qblk = pl.BlockSpec((BLK, qw), lambda m, i: (i, m))
    col_all = pl.BlockSpec((S, LANES), lambda m, i: (0, m))
    return _pcall(
        body, (q, kv, kv, kv, kv, sinks, o, do), name=name,
        out_shape=(jax.ShapeDtypeStruct((S, D), F32),
                   jax.ShapeDtypeStruct((S, nkvp * LANES), F32),
                   jax.ShapeDtypeStruct((S, nkvp * LANES), F32),
                   jax.ShapeDtypeStruct((nkvp, nh, LANES), F32)),
        grid=(nkvp, nb),
        in_specs=[qblk,
                  pl.BlockSpec((BLK, LANES), lambda m, i: (i, m)),
                  pl.BlockSpec((BLK, LANES), lambda m, i: (prev(i), m)),
                  pl.BlockSpec((BLK, LANES), lambda m, i: (i, nkvp + m)),
                  pl.BlockSpec((BLK, LANES), lambda m, i: (prev(i), nkvp + m)),
                  pl.BlockSpec(memory_space=pltpu.SMEM),
                  qblk, qblk],
        out_specs=(qblk, col_all, col_all, pl.BlockSpec((1, nh, LANES), lambda m, i: (m, 0, 0))),
        sem=("arbitrary", "arbitrary"), carry=carry)


def _dev_index(p):
    return 4 * p[0] + 2 * p[1] + p[2]


def _gather_plan(x_refs, out_refs, send_sems, recv_sems, local_sems):
    n = len(x_refs)
    x_, y_, c_ = lax.axis_index("x"), lax.axis_index("y"), lax.axis_index("c")
    me, sibling = (x_, y_, c_), (x_, y_, 1 - c_)
    chips = [(1 - x_, y_), (x_, 1 - y_), (1 - x_, 1 - y_)]

    def copy(t, k, block, to, src=None):
        dst = out_refs[t].at[_dev_index(block)]
        return pltpu.make_async_remote_copy(
            src_ref=dst if src is None else src, dst_ref=dst,
            send_sem=send_sems.at[7 * t + k], recv_sem=recv_sems.at[7 * t + k],
            device_id=to, device_id_type=MESH)

    mine = [pltpu.make_async_copy(x_refs[t], out_refs[t].at[_dev_index(me)], local_sems.at[t]) for t in range(n)]
    first = []
    for t in range(n):
        first.append(copy(t, 0, me, sibling, src=x_refs[t]))
        first += [copy(t, 1 + j, me, (*chip, c_), src=x_refs[t]) for j, chip in enumerate(chips)]
    arrived = lambda t, j: copy(t, 1 + j, (*chips[j], c_), me)
    forward = lambda t, j: copy(t, 4 + j, (*chips[j], c_), sibling)
    from_sibling = lambda t: copy(t, 0, sibling, me)
    forwarded = lambda t, j: copy(t, 4 + j, (*chips[j], 1 - c_), me)
    return n, mine, first, arrived, forward, from_sibling, forwarded


def _gather_start(x_refs, out_refs, send_sems, recv_sems, local_sems):
    _, mine, first, *_ = _gather_plan(x_refs, out_refs, send_sems, recv_sems, local_sems)
    for cp in mine + first:
        cp.start()


def _gather_finish(x_refs, out_refs, send_sems, recv_sems, local_sems):
    n, mine, first, arrived, forward, from_sibling, forwarded = _gather_plan(
        x_refs, out_refs, send_sems, recv_sems, local_sems)
    passed = []
    for j in range(3):
        for t in range(n):
            arrived(t, j).wait_recv()
            fwd = forward(t, j)
            fwd.start()
            passed.append(fwd)
    for t in range(n):
        from_sibling(t).wait_recv()
    for j in range(3):
        for t in range(n):
            forwarded(t, j).wait_recv()
    for cp in first + passed:
        cp.wait_send()
    for cp in mine:
        cp.wait()


def _scatter_plan(b_refs, out_refs, send_sems, recv_sems, local_sems):
    n = len(b_refs)
    x_, y_, c_ = lax.axis_index("x"), lax.axis_index("y"), lax.axis_index("c")
    my_idx = _dev_index((x_, y_, c_))
    mine = [pltpu.make_async_copy(b_refs[t].at[my_idx], out_refs[t].at[my_idx], local_sems.at[t]) for t in range(n)]
    copies = []
    for t in range(n):
        for k in range(1, N_DEV):
            peer = (x_ ^ ((k >> 2) & 1), y_ ^ ((k >> 1) & 1), c_ ^ (k & 1))
            copies.append(pltpu.make_async_remote_copy(
                src_ref=b_refs[t].at[_dev_index(peer)], dst_ref=out_refs[t].at[my_idx],
                send_sem=send_sems.at[7 * t + k - 1], recv_sem=recv_sems.at[7 * t + k - 1],
                device_id=peer, device_id_type=MESH))
    return mine, copies


def _scatter_start(b_refs, out_refs, send_sems, recv_sems, local_sems):
    mine, copies = _scatter_plan(b_refs, out_refs, send_sems, recv_sems, local_sems)
    for cp in mine + copies:
        cp.start()


def _scatter_finish(b_refs, out_refs, send_sems, recv_sems, local_sems):
    mine, copies = _scatter_plan(b_refs, out_refs, send_sems, recv_sems, local_sems)
    for cp in copies:
        cp.wait_recv()
    for cp in copies:
        cp.wait_send()
    for cp in mine:
        cp.wait()


def _exchange_operands(kind, tensors):
    if kind == "gather":
        args = list(tensors)
        shapes = [jax.ShapeDtypeStruct((N_DEV,) + t.shape, t.dtype) for t in tensors]
        return args, shapes, _gather_start, _gather_finish
    args = [t.reshape(N_DEV, t.shape[0] // N_DEV, t.shape[1]) for t in tensors]
    shapes = [jax.ShapeDtypeStruct(a.shape, a.dtype) for a in args]
    return args, shapes, _scatter_start, _scatter_finish


def _exchange_results(kind, tensors, res):
    if kind == "gather":
        return [r.reshape(N_DEV * t.shape[0], t.shape[1]) for r, t in zip(res, tensors)]
    return list(res)


def _exchange_sems(n):
    return [pltpu.SemaphoreType.DMA((7 * n,)), pltpu.SemaphoreType.DMA((7 * n,)), pltpu.SemaphoreType.DMA((n,))]


def _exchange(kind, tensors, name):
    n = len(tensors)
    args, shapes, start, finish = _exchange_operands(kind, tensors)

    def body(*refs):
        start(refs[:n], refs[n:2 * n], *refs[2 * n:])
        finish(refs[:n], refs[n:2 * n], *refs[2 * n:])

    hbm = pl.BlockSpec(memory_space=pl.ANY)
    res = pl.pallas_call(body, name=name, out_shape=shapes, in_specs=[hbm] * n, out_specs=[hbm] * n,
                         scratch_shapes=_exchange_sems(n))(*args)
    return _exchange_results(kind, tensors, res)


def _pcall(body, args, *, name, out_shape, grid, in_specs, out_specs, sem, scratch_shapes=(), carry=None):
    if carry is None:
        out = pl.pallas_call(body, name=name, out_shape=out_shape, grid=grid, in_specs=list(in_specs),
                             out_specs=out_specs, scratch_shapes=list(scratch_shapes),
                             compiler_params=_params(*sem))(*args)
        return out, None
    kind, tensors = carry
    multi = isinstance(out_shape, (tuple, list))
    shapes = list(out_shape) if multi else [out_shape]
    ospecs = list(out_specs) if multi else [out_specs]
    n_in, n_out, n_scr, n_c = len(in_specs), len(shapes), len(scratch_shapes), len(tensors)
    c_args, c_shapes, start, finish = _exchange_operands(kind, tensors)

    def wrapped(*refs):
        ins, rest = refs[:n_in], refs[n_in:]
        c_in, rest = rest[:n_c], rest[n_c:]
        outs, rest = rest[:n_out], rest[n_out:]
        c_out, rest = rest[:n_c], rest[n_c:]
        scr, sems = rest[:n_scr], rest[n_scr:]
        ids = [pl.program_id(a) for a in range(len(grid))]
        first, last = ids[0] == 0, ids[0] == grid[0] - 1
        for a in range(1, len(grid)):
            first = jnp.logical_and(first, ids[a] == 0)
            last = jnp.logical_and(last, ids[a] == grid[a] - 1)

        @pl.when(first)
        def _():
            start(c_in, c_out, *sems)

        body(*ins, *outs, *scr)

        @pl.when(last)
        def _():
            finish(c_in, c_out, *sems)

    hbm = pl.BlockSpec(memory_space=pl.ANY)
    res = pl.pallas_call(
        wrapped, name=name, out_shape=shapes + c_shapes, grid=grid,
        in_specs=list(in_specs) + [hbm] * n_c, out_specs=ospecs + [hbm] * n_c,
        scratch_shapes=list(scratch_shapes) + _exchange_sems(n_c),
        compiler_params=_params(*sem))(*args, *c_args)
    outs = tuple(res[:n_out]) if multi else res[0]
    return outs, _exchange_results(kind, tensors, res[n_out:])


def _sum8(parts, name):
    _, R, C = parts.shape
    tr = _tile(R, 256, 16)

    def body(p_ref, g_ref):
        g = p_ref[0].astype(F32)
        for s in range(1, N_DEV):
            g = g + p_ref[s].astype(F32)
        g_ref[...] = g

    return pl.pallas_call(
        body, name=name, out_shape=jax.ShapeDtypeStruct((R, C), F32),
        grid=(R // tr,),
        in_specs=[pl.BlockSpec((N_DEV, tr, C), lambda i: (0, i, 0))],
        out_specs=pl.BlockSpec((tr, C), lambda i: (i, 0)),
        compiler_params=_params("parallel"),
    )(parts)


def _adamw(g, w, m, v, name):
    R, C = g.shape
    tr = _tile(R, 256, 8)
    c1 = 1.0 - ADAM_B1 ** ADAM_STEP
    c2 = 1.0 - ADAM_B2 ** ADAM_STEP

    def body(g_ref, w_ref, m_ref, v_ref, d_ref, nm_ref, nv_ref):
        gg = g_ref[...]
        nm = ADAM_B1 * m_ref[...] + (1.0 - ADAM_B1) * gg
        nv = ADAM_B2 * v_ref[...] + (1.0 - ADAM_B2) * (gg * gg)
        m_hat = nm / c1
        v_hat = nv / c2
        nm_ref[...] = nm
        nv_ref[...] = nv
        d_ref[...] = -ADAM_LR * (m_hat / (jnp.sqrt(v_hat) + ADAM_EPS) + ADAM_WD * w_ref[...])

    row = pl.BlockSpec((tr, C), lambda i: (i, 0))
    shp = jax.ShapeDtypeStruct((R, C), F32)
    return pl.pallas_call(
        body, name=name, out_shape=(shp, shp, shp),
        grid=(R // tr,), in_specs=[row, row, row, row], out_specs=(row, row, row),
        compiler_params=_params("parallel"),
    )(g, w, m, v)


def _ffn_down(act, wo, h, tag):
    return _mm(act, wo, NN, F32, f"{tag}_down", scale=FFN_RES_SCALE, res=h, tm=512, tn=1024, tk=2816)


def _ffn_fwd(h, g, win_t, wo, tag, carry=None):
    return _ffn_fwd_fused(h, g, win_t, wo, f"{tag}_fwd", carry=carry)


def _ffn_bwd(dh, h, g, win_t, wo, saved, tag, scatter=False, carry=None):
    xn, silu, dsilu, up, act = saved
    dwo = _mm(act, dh, TN, BF16, f"{tag}_dwo", scale=FFN_RES_SCALE, tm=1408, tn=1024, tk=TN_CHUNK)
    if not scatter:
        (dh_in, dg, dgate, dup), got = _ffn_bwd_fused(dh, h, g, win_t, wo, silu, dsilu, up, f"{tag}_bwd", carry=carry)
        dwin_t, _ = _dwin(dgate, dup, xn, f"{tag}_dwin")
        return dh_in, dg, dwin_t, dwo, got
    dgate, dup = _ffn_dact(dh, wo, silu, dsilu, up, f"{tag}_dact")
    dwin_t, got_wo = _dwin(dgate, dup, xn, f"{tag}_dwin", carry=("scatter", [dwo]))
    (dh_in, dg), got_win = _dx_norm_bwd([(dgate, win_t, NN, 2, 0), (dup, win_t, NN, 2, 1)], h, g, dh, f"{tag}_dx",
                                        carry=("scatter", [dwin_t]))
    return dh_in, dg, got_win[0], got_wo[0]


def _proj(a, w, dims, out_dtype, name, res=None):
    return _mm(a, w, dims, out_dtype, name, res=res, tm=1024, tn=1024, tk=1024)


def _proj_dw(x, dy, name):
    return _mm(x, dy, TN, BF16, name, tm=1024, tn=1024, tk=TN_CHUNK)


def kernel(x, ffn1_norm, ffn1_w_in, ffn1_w_out, mix_norm, ffn2_norm, ffn2_w_in, ffn2_w_out, sb_w_qkv, sb_w_o, kv_norm, kv_w, swa_w_q, swa_sinks, swa_w_o, final_norm, loss_target, m_ffn1_norm, m_ffn1_w_in, m_ffn1_w_out, m_mix_norm, m_ffn2_norm, m_ffn2_w_in, m_ffn2_w_out, m_sb_w_qkv, m_sb_w_o, m_kv_norm, m_kv_w, m_swa_w_q, m_swa_sinks, m_swa_w_o, m_final_norm, v_ffn1_norm, v_ffn1_w_in, v_ffn1_w_out, v_mix_norm, v_ffn2_norm, v_ffn2_w_in, v_ffn2_w_out, v_sb_w_qkv, v_sb_w_o, v_kv_norm, v_kv_w, v_swa_w_q, v_swa_sinks, v_swa_w_o, v_final_norm):
    S, D = x.shape[1], x.shape[2]
    L = ffn1_w_in.shape[0]
    KV = kv_w.shape[1]
    assert L == 2 and swa_sinks.shape == (1, 2 * SWA_Q_GROUPS * KV // (2 * LANES))

    def bf(w):
        return w.astype(BF16)

    def bft(w):
        return jnp.transpose(w).astype(BF16)

    cos_t, sin_t = _rope_tables(S)
    h0 = x.reshape(S, D)
    tgt = loss_target.reshape(S, D)

    win1a_t, = _exchange("gather", [bft(ffn1_w_in[0])], "gather_first_weight")
    sv_a1, (wo1a, wqkv_t, w_sbo) = _ffn_up(
        h0, ffn1_norm[0], win1a_t, "ffn1a_up",
        carry=("gather", [bf(ffn1_w_out[0]), bft(sb_w_qkv[0]), bf(sb_w_o[0])]))
    h1 = _ffn_down(sv_a1[-1], wo1a, h0, "ffn1a")
    hn_a = _rmsnorm(h1, mix_norm[0], "mix_a_norm")
    qkv = _proj(hn_a, wqkv_t, NT, BF16, "sb_qkv")
    k_t, v_t = jnp.transpose(qkv[:, D:2 * D]), jnp.transpose(qkv[:, 2 * D:])
    o_sb, (win2a_t, wo2a, w_kv) = _sb_fwd(qkv, v_t, "sb_attn", carry=("gather", [
        bft(ffn2_w_in[0]), bf(ffn2_w_out[0]), bf(kv_w)]))
    h2 = _proj(o_sb, w_sbo, NN, F32, "sb_out", res=h1)
    h3, sv_a2, (win1b_t, wo1b, w_q, w_swo) = _ffn_fwd(h2, ffn2_norm[0], win2a_t, wo2a, "ffn2a", carry=("gather", [
        bft(ffn1_w_in[1]), bf(ffn1_w_out[1]), bf(swa_w_q[0]), bf(swa_w_o[0])]))
    kvn = _rmsnorm(h3, kv_norm, "kv_norm")
    kv_raw = _proj(kvn, w_kv, NN, F32, "kv_proj")
    kv_rot = _rotary(kv_raw, cos_t, sin_t, KV // (2 * LANES), False, "kv_rope")
    h4, sv_b1, (win2b_t, wo2b) = _ffn_fwd(h3, ffn1_norm[1], win1b_t, wo1b, "ffn1b", carry=("gather", [
        bft(ffn2_w_in[1]), bf(ffn2_w_out[1])]))
    hn_b = _rmsnorm(h4, mix_norm[1], "mix_b_norm")
    q_raw = _proj(hn_b, w_q, NN, F32, "swa_q")
    q_rot = _rotary(q_raw, cos_t, sin_t, D // LANES, False, "q_rope")
    o_sw = _swa_fwd(q_rot, kv_rot, swa_sinks, "swa_attn")
    h5 = _proj(o_sw, w_swo, NN, F32, "swa_out", res=h4)
    h6, sv_b2, _ = _ffn_fwd(h5, ffn2_norm[1], win2b_t, wo2b, "ffn2b")
    dh6, dg_final, sq_err = _final_loss(h6, final_norm, tgt, "final_loss")
    loss = lax.psum(0.5 * jnp.sum(sq_err) / D, ("x", "y", "c"))

    dh5, dg_f2b, dwin2b_t, dwo2b, _ = _ffn_bwd(dh6, h5, ffn2_norm[1], win2b_t, wo2b, sv_b2, "ffn2b")
    do_sw = _proj(dh5, w_swo, NT, BF16, "swa_out_dx")
    dw_swo = _proj_dw(o_sw, dh5, "swa_out_dw")
    (dq_rot, dk_sw, dv_sw, dsink), (p_win2b, p_wo2b, p_swo) = _swa_bwd(
        q_rot, kv_rot, swa_sinks, o_sw, do_sw, "swa_attn_bwd", carry=("scatter", [dwin2b_t, dwo2b, dw_swo]))
    dq = _rotary(dq_rot, cos_t, sin_t, D // LANES, True, "q_rope_bwd")
    dw_q = _proj_dw(hn_b, dq, "swa_q_dw")
    (dh4, dg_mix_b), _ = _dx_norm_bwd([(dq, w_q, NT, 1, 0)], h4, mix_norm[1], dh5, "swa_q_dx")
    dh3, dg_f1b, dwin1b_t, dwo1b, (p_q,) = _ffn_bwd(dh4, h3, ffn1_norm[1], win1b_t, wo1b, sv_b1, "ffn1b",
                                                    carry=("scatter", [dw_q]))
    dkv = _rotary(jnp.concatenate([dk_sw, dv_sw], axis=1), cos_t, sin_t, KV // (2 * LANES), True, "kv_rope_bwd")
    dw_kv = _proj_dw(kvn, dkv, "kv_proj_dw")
    (dh3, dg_kv), _ = _dx_norm_bwd([(dkv, w_kv, NT, 1, 0)], h3, kv_norm, dh3, "kv_proj_dx")
    dh2, dg_f2a, dwin2a_t, dwo2a, (p_win1b, p_wo1b, p_kv) = _ffn_bwd(
        dh3, h2, ffn2_norm[0], win2a_t, wo2a, sv_a2, "ffn2a", carry=("scatter", [dwin1b_t, dwo1b, dw_kv]))
    do_sb = _proj(dh2, w_sbo, NT, BF16, "sb_out_dx")
    dw_sbo = _proj_dw(o_sb, dh2, "sb_out_dw")
    (dq_sb, dk_sb, dv_sb), (p_win2a, p_wo2a, p_sbo) = _sb_bwd(
        qkv, k_t, o_sb, do_sb, "sb_attn_bwd", carry=("scatter", [dwin2a_t, dwo2a, dw_sbo]))
    dqkv = jnp.concatenate([dq_sb, dk_sb.astype(BF16), dv_sb.astype(BF16)], axis=1)
    dwqkv_t = _proj_dw(dqkv, hn_a, "sb_qkv_dw")
    (dh1, dg_mix_a), (p_qkv,) = _dx_norm_bwd([(dqkv, wqkv_t, NN, 1, 0)], h1, mix_norm[0], dh2, "sb_qkv_dx",
                                             carry=("scatter", [dwqkv_t]))
    dx, dg_f1a, p_win1a, p_wo1a = _ffn_bwd(dh1, h0, ffn1_norm[0], win1a_t, wo1a, sv_a1, "ffn1a", scatter=True)

    def natural(parts, tag):
        return _sum8(parts, f"sum_{tag}")

    def from_t(parts, tag):
        return jnp.transpose(_sum8(parts, f"sum_{tag}"))

    grads = {
        "ffn1_w_in": jnp.stack([from_t(p_win1a, "win1a"), from_t(p_win1b, "win1b")]),
        "ffn1_w_out": jnp.stack([natural(p_wo1a, "wo1a"), natural(p_wo1b, "wo1b")]),
        "ffn2_w_in": jnp.stack([from_t(p_win2a, "win2a"), from_t(p_win2b, "win2b")]),
        "ffn2_w_out": jnp.stack([natural(p_wo2a, "wo2a"), natural(p_wo2b, "wo2b")]),
        "sb_w_qkv": from_t(p_qkv, "qkv")[None],
        "sb_w_o": natural(p_sbo, "sbo")[None],
        "kv_w": natural(p_kv, "kv"),
        "swa_w_q": natural(p_q, "swq")[None],
        "swa_w_o": natural(p_swo, "swo")[None],
    }

    small_w = [ffn1_norm, mix_norm, ffn2_norm, kv_norm, final_norm, swa_sinks]
    small_m = [m_ffn1_norm, m_mix_norm, m_ffn2_norm, m_kv_norm, m_final_norm, m_swa_sinks]
    small_v = [v_ffn1_norm, v_mix_norm, v_ffn2_norm, v_kv_norm, v_final_norm, v_swa_sinks]
    SMALL_ROWS = 16

    def pack_small(ts):
        rows_ = [t.reshape(-1, D) for t in ts[:-1]]
        sink_row = jnp.pad(ts[-1].reshape(1, -1), ((0, 0), (0, D - ts[-1].size)))
        flat = jnp.concatenate(rows_ + [sink_row], axis=0)
        return jnp.pad(flat, ((0, SMALL_ROWS - flat.shape[0]), (0, 0)))

    def unpack_small(flat):
        out, r = [], 0
        for t in small_w[:-1]:
            n = t.size // D
            out.append(flat[r:r + n].reshape(t.shape))
            r += n
        out.append(flat[r, :swa_sinks.size].reshape(swa_sinks.shape))
        return out

    def gain(parts8):
        return jnp.sum(parts8, axis=0, keepdims=True)

    g_small_local = pack_small([
        jnp.concatenate([gain(dg_f1a), gain(dg_f1b)], axis=0),
        jnp.concatenate([gain(dg_mix_a), gain(dg_mix_b)], axis=0),
        jnp.concatenate([gain(dg_f2a), gain(dg_f2b)], axis=0),
        gain(dg_kv), gain(dg_final), dsink[:, :, 0].reshape(1, -1)])
    small_parts = _exchange("gather", [g_small_local], "gather_small_grads")[0]
    g_small = _sum8(small_parts.reshape(N_DEV, SMALL_ROWS, D), "sum_small")
    d_small, nm_small, nv_small = _adamw(g_small, pack_small(small_w), pack_small(small_m), pack_small(small_v), "adamw_small")
    small_names = ["ffn1_norm", "mix_norm", "ffn2_norm", "kv_norm", "final_norm", "swa_sinks"]
    result = {"grad": dict(zip(small_names, unpack_small(g_small))),
              "delta": dict(zip(small_names, unpack_small(d_small))),
              "new_m": dict(zip(small_names, unpack_small(nm_small))),
              "new_v": dict(zip(small_names, unpack_small(nv_small)))}

    big = {"ffn1_w_in": (ffn1_w_in, m_ffn1_w_in, v_ffn1_w_in), "ffn1_w_out": (ffn1_w_out, m_ffn1_w_out, v_ffn1_w_out),
           "ffn2_w_in": (ffn2_w_in, m_ffn2_w_in, v_ffn2_w_in), "ffn2_w_out": (ffn2_w_out, m_ffn2_w_out, v_ffn2_w_out),
           "sb_w_qkv": (sb_w_qkv, m_sb_w_qkv, v_sb_w_qkv), "sb_w_o": (sb_w_o, m_sb_w_o, v_sb_w_o),
           "kv_w": (kv_w, m_kv_w, v_kv_w), "swa_w_q": (swa_w_q, m_swa_w_q, v_swa_w_q),
           "swa_w_o": (swa_w_o, m_swa_w_o, v_swa_w_o)}
    for nm, (w, m, v) in big.items():
        g = grads[nm]
        two_d = lambda t: t.reshape(-1, t.shape[-1])
        d, new_m, new_v = _adamw(two_d(g), two_d(w), two_d(m), two_d(v), f"adamw_{nm}")
        result["grad"][nm] = g
        result["delta"][nm] = d.reshape(w.shape)
        result["new_m"][nm] = new_m.reshape(w.shape)
        result["new_v"][nm] = new_v.reshape(w.shape)

    order = ["ffn1_norm", "ffn1_w_in", "ffn1_w_out", "mix_norm", "ffn2_norm", "ffn2_w_in", "ffn2_w_out",
             "sb_w_qkv", "sb_w_o", "kv_norm", "kv_w", "swa_w_q", "swa_sinks", "swa_w_o", "final_norm"]
    outs = [result[kind][nm] for kind in ("grad", "delta", "new_m", "new_v") for nm in order]
    return (loss, dx.reshape(x.shape), *outs)
```

```python
import jax
import jax.numpy as jnp
from jax import lax
from jax.experimental import pallas as pl
from jax.experimental.pallas import tpu as pltpu

F32 = jnp.float32
BF16 = jnp.bfloat16

N_DEV = 8
HEAD_DIM = 64
LANES = 128
BLK = 128
RMS_EPS = 1e-6
FFN_RES_SCALE = 0.5
ROPE_THETA = 10000.0
ATTN_SCALE = HEAD_DIM ** -0.5
SB_LOG_FLOOR = -88.0
NEG_BIG = -1e30
VMEM_LIMIT_V7X = 56 * 1024 * 1024

ADAM_LR = 0.001
ADAM_B1 = 0.9
ADAM_B2 = 0.999
ADAM_EPS = 1e-08
ADAM_WD = 0.01
ADAM_STEP = 10

NN = ((1,), (0,))
NT = ((1,), (1,))
TN = ((0,), (0,))
TN_CHUNK = 2048
MESH = pl.DeviceIdType.MESH


def _dot(a, b, dims):
    return lax.dot_general(a, b, (dims, ((), ())), preferred_element_type=F32)


def _tile(n, pref, mult=LANES):
    if n <= pref:
        return n
    t = (pref // mult) * mult
    while t >= mult:
        if n % t == 0:
            return t
        t -= mult
    return n


def _params(*sem):
    return pltpu.CompilerParams(dimension_semantics=sem, vmem_limit_bytes=VMEM_LIMIT_V7X)


def _mm(a, b, dims, out_dtype, name, scale=1.0, res=None, tm=512, tn=512, tk=512):
    if dims == NN:
        (M, K), (_, N) = a.shape, b.shape
    elif dims == NT:
        (M, K), (N, _) = a.shape, b.shape
    else:
        (K, M), (_, N) = a.shape, b.shape
    tm, tn, tk = _tile(M, tm), _tile(N, tn), _tile(K, tk)
    nk = K // tk
    if dims == TN:
        a_spec = pl.BlockSpec((tk, tm), lambda i, j, k: (k, i))
    else:
        a_spec = pl.BlockSpec((tm, tk), lambda i, j, k: (i, k))
    if dims == NT:
        b_spec = pl.BlockSpec((tn, tk), lambda i, j, k: (j, k))
    else:
        b_spec = pl.BlockSpec((tk, tn), lambda i, j, k: (k, j))
    o_spec = pl.BlockSpec((tm, tn), lambda i, j, k: (i, j))
    has_res = res is not None

    def body(*refs):
        a_ref, b_ref = refs[0], refs[1]
        r_ref = refs[2] if has_res else None
        o_ref = refs[3] if has_res else refs[2]

        def finish(acc):
            r = acc * scale if scale != 1.0 else acc
            if has_res:
                r = r + r_ref[...]
            o_ref[...] = r.astype(out_dtype)

        p = _dot(a_ref[...].astype(BF16), b_ref[...].astype(BF16), dims)
        if nk == 1:
            finish(p)
        else:
            acc_ref = refs[-1]
            k = pl.program_id(2)

            @pl.when(k == 0)
            def _():
                acc_ref[...] = p

            @pl.when(k > 0)
            def _():
                acc_ref[...] += p

            @pl.when(k == nk - 1)
            def _():
                finish(acc_ref[...])

    in_specs = [a_spec, b_spec] + ([o_spec] if has_res else [])
    args = (a, b) + ((res,) if has_res else ())
    return pl.pallas_call(
        body, name=name,
        out_shape=jax.ShapeDtypeStruct((M, N), out_dtype),
        grid=(M // tm, N // tn, nk),
        in_specs=in_specs, out_specs=o_spec,
        scratch_shapes=[pltpu.VMEM((tm, tn), F32)] if nk > 1 else [],
        compiler_params=_params("parallel", "parallel", "arbitrary"),
    )(*args)


def _rows8(x):
    r, d = x.shape
    return jnp.sum(x.reshape(r // 8, 8, d), axis=0)


def _norm_proj(h, g, w, dims, name, rope=None):
    S, D = h.shape
    N = w.shape[1] if dims == NN else w.shape[0]
    tm = _tile(S, 512, 16)

    def body(h_ref, g_ref, w_ref, *rest):
        xn_ref, y_ref = rest[-2:]
        x = h_ref[...]
        r = lax.rsqrt(jnp.mean(x * x, axis=-1, keepdims=True) + RMS_EPS)
        xn = ((x * r) * g_ref[...]).astype(BF16)
        xn_ref[...] = xn
        y = _dot(xn, w_ref[...], dims)
        if rope is None:
            y_ref[...] = y.astype(BF16)
        else:
            cs, sn = rest[0][...], rest[1][...]
            for gidx in range(N // LANES):
                sl = slice(gidx * LANES, (gidx + 1) * LANES)
                v = y[:, sl]
                if gidx < rope[2]:
                    v = v * cs + _swap_halves(v) * sn
                y_ref[:, sl] = v.astype(BF16)

    row = pl.BlockSpec((tm, D), lambda i: (i, 0))
    tab = pl.BlockSpec((tm, LANES), lambda i: (i, 0))
    in_specs = [row, pl.BlockSpec((1, D), lambda i: (0, 0)), pl.BlockSpec(w.shape, lambda i: (0, 0))]
    args = (h, g.reshape(1, D), w)
    if rope is not None:
        in_specs += [tab, tab]
        args += (rope[0], rope[1])
    return pl.pallas_call(
        body, name=name,
        out_shape=(jax.ShapeDtypeStruct((S, D), BF16), jax.ShapeDtypeStruct((S, N), BF16)),
        grid=(S // tm,),
        in_specs=in_specs, out_specs=(row, pl.BlockSpec((tm, N), lambda i: (i, 0))),
        compiler_params=_params("parallel"),
    )(*args)


def _final_loss(h, g, tgt, name):
    S, D = h.shape
    ts = _tile(S, 512, 8)

    def body(h_ref, g_ref, t_ref, dh_ref, dg_ref, l_ref):
        x = h_ref[...]
        r = lax.rsqrt(jnp.mean(x * x, axis=-1, keepdims=True) + RMS_EPS)
        xhat = x * r
        err = xhat * g_ref[...] - t_ref[...]
        d = err * (1.0 / D)
        dxh = d * g_ref[...]
        c = jnp.mean(dxh * xhat, axis=-1, keepdims=True)
        dh_ref[...] = r * (dxh - xhat * c)
        part = _rows8(d * xhat)
        lpart = _rows8(err * err)

        @pl.when(pl.program_id(0) == 0)
        def _():
            dg_ref[...] = part
            l_ref[...] = lpart

        @pl.when(pl.program_id(0) > 0)
        def _():
            dg_ref[...] += part
            l_ref[...] += lpart

    row = pl.BlockSpec((ts, D), lambda i: (i, 0))
    acc = pl.BlockSpec((8, D), lambda i: (0, 0))
    return pl.pallas_call(
        body, name=name,
        out_shape=(jax.ShapeDtypeStruct((S, D), F32), jax.ShapeDtypeStruct((8, D), F32),
                   jax.ShapeDtypeStruct((8, D), F32)),
        grid=(S // ts,),
        in_specs=[row, pl.BlockSpec((1, D), lambda i: (0, 0)), row],
        out_specs=(row, acc, acc),
        compiler_params=_params("arbitrary"),
    )(h, g.reshape(1, D), tgt)


def _ffn_up(h, g, win_t, name, carry=None):
    S, D = h.shape
    F = win_t.shape[0] // 2
    tm, tn = _tile(S, 512, 16), _tile(F, 1408)
    nf = F // tn

    def body(h_ref, g_ref, wg_ref, wu_ref, xn_ref, silu_ref, dsilu_ref, up_ref, act_ref):
        x = h_ref[...]
        r = lax.rsqrt(jnp.mean(x * x, axis=-1, keepdims=True) + RMS_EPS)
        xn = ((x * r) * g_ref[...]).astype(BF16)
        xn_ref[...] = xn
        gate = _dot(xn, wg_ref[...], NT)
        up = _dot(xn, wu_ref[...], NT)
        sig = 1.0 / (1.0 + jnp.exp(-gate))
        silu = gate * sig
        up_ref[...] = up.astype(BF16)
        silu_ref[...] = silu.astype(BF16)
        dsilu_ref[...] = (sig + silu * (1.0 - sig)).astype(BF16)
        act_ref[...] = (silu * up).astype(BF16)

    row = pl.BlockSpec((tm, D), lambda i, j: (i, 0))
    blk = pl.BlockSpec((tm, tn), lambda i, j: (i, j))
    hid = jax.ShapeDtypeStruct((S, F), BF16)
    return _pcall(
        body, (h, g.reshape(1, D), win_t, win_t), name=name,
        out_shape=(jax.ShapeDtypeStruct((S, D), BF16), hid, hid, hid, hid),
        grid=(S // tm, nf),
        in_specs=[row, pl.BlockSpec((1, D), lambda i, j: (0, 0)),
                  pl.BlockSpec((tn, D), lambda i, j: (j, 0)),
                  pl.BlockSpec((tn, D), lambda i, j: (j + nf, 0))],
        out_specs=(row, blk, blk, blk, blk),
        sem=("arbitrary", "arbitrary"), carry=carry)


def _ffn_dact(dh, wo, silu, dsilu, up, name):
    S, D = dh.shape
    F = wo.shape[0]
    tm, tn = _tile(S, 512, 16), _tile(F, 1408)

    def body(dh_ref, wo_ref, s_ref, ds_ref, u_ref, dg_ref, du_ref):
        d = _dot(dh_ref[...].astype(BF16), wo_ref[...], NT) * FFN_RES_SCALE
        du_ref[...] = (d * s_ref[...].astype(F32)).astype(BF16)
        dg_ref[...] = (d * u_ref[...].astype(F32) * ds_ref[...].astype(F32)).astype(BF16)

    blk = pl.BlockSpec((tm, tn), lambda j, i: (i, j))
    hid = jax.ShapeDtypeStruct((S, F), BF16)
    return pl.pallas_call(
        body, name=name, out_shape=(hid, hid),
        grid=(F // tn, S // tm),
        in_specs=[pl.BlockSpec((tm, D), lambda j, i: (i, 0)), pl.BlockSpec((tn, D), lambda j, i: (j, 0)),
                  blk, blk, blk],
        out_specs=(blk, blk),
        compiler_params=_params("arbitrary", "arbitrary"),
    )(dh, wo, silu, dsilu, up)


def _dw_rows(srcs, x, name, carry=None, tk=TN_CHUNK):
    n = len(srcs)
    S, F = srcs[0].shape
    D = x.shape[1]
    tr, tk = _tile(F, 1408), _tile(S, tk, 16)
    nf, nk = F // tr, S // tk

    def body(*refs):
        src_refs, (x_ref, o_ref, acc_ref) = refs[:n], refs[n:]
        r, k = pl.program_id(0), pl.program_id(1)
        for s in range(n):
            @pl.when(r // nf == s)
            def _():
                p = _dot(src_refs[s][...].astype(BF16), x_ref[...], TN)

                @pl.when(k == 0)
                def _():
                    acc_ref[...] = p

                @pl.when(k > 0)
                def _():
                    acc_ref[...] += p

        @pl.when(k == nk - 1)
        def _():
            o_ref[...] = acc_ref[...].astype(BF16)

    def src_spec(s):
        return pl.BlockSpec((tk, tr), lambda r, k: (jnp.where(r // nf == s, k, 0), jnp.clip(r - s * nf, 0, nf - 1)))

    return _pcall(
        body, (*srcs, x), name=name, out_shape=jax.ShapeDtypeStruct((n * F, D), BF16),
        grid=(n * nf, nk),
        in_specs=[src_spec(s) for s in range(n)] + [pl.BlockSpec((tk, D), lambda r, k: (k, 0))],
        out_specs=pl.BlockSpec((tr, D), lambda r, k: (r, 0)),
        scratch_shapes=[pltpu.VMEM((tr, D), F32)],
        sem=("arbitrary", "arbitrary"), carry=carry)


def _dx_norm_bwd(terms, h, g, res, name, carry=None):
    S, D = h.shape
    tm = _tile(S, 256, 16)
    n = len(terms)

    def body(*refs):
        dy_refs, w_refs = refs[:n], refs[n:2 * n]
        h_ref, g_ref, r_ref, dh_ref, dg_ref = refs[2 * n:]
        d = _dot(dy_refs[0][...].astype(BF16), w_refs[0][...], terms[0][2])
        for t in range(1, n):
            d = d + _dot(dy_refs[t][...].astype(BF16), w_refs[t][...], terms[t][2])
        x = h_ref[...]
        r = lax.rsqrt(jnp.mean(x * x, axis=-1, keepdims=True) + RMS_EPS)
        xhat = x * r
        dxh = d * g_ref[...]
        c = jnp.mean(dxh * xhat, axis=-1, keepdims=True)
        dh_ref[...] = r * (dxh - xhat * c) + r_ref[...]
        part = _rows8(d * xhat)

        @pl.when(pl.program_id(0) == 0)
        def _():
            dg_ref[...] = part

        @pl.when(pl.program_id(0) > 0)
        def _():
            dg_ref[...] += part

    def w_spec(w, nblk, blk):
        return pl.BlockSpec((w.shape[0] // nblk, w.shape[1]), lambda i: (blk, 0))

    row = pl.BlockSpec((tm, D), lambda i: (i, 0))
    in_specs = [pl.BlockSpec((tm, t[0].shape[1]), lambda i: (i, 0)) for t in terms]
    in_specs += [w_spec(t[1], t[3], t[4]) for t in terms]
    in_specs += [row, pl.BlockSpec((1, D), lambda i: (0, 0)), row]
    return _pcall(
        body, (*[t[0] for t in terms], *[t[1] for t in terms], h, g.reshape(1, D), res), name=name,
        out_shape=(jax.ShapeDtypeStruct((S, D), F32), jax.ShapeDtypeStruct((8, D), F32)),
        grid=(S // tm,),
        in_specs=in_specs,
        out_specs=(row, pl.BlockSpec((8, D), lambda i: (0, 0))),
        sem=("arbitrary",), carry=carry)


def _load_resident(pairs, sems):
    @pl.when(pl.program_id(0) == 0)
    def _():
        copies = [pltpu.make_async_copy(src, dst, sems.at[n]) for n, (src, dst) in enumerate(pairs)]
        for cp in copies:
            cp.start()
        for cp in copies:
            cp.wait()


def _ffn_fwd_fused(h, g, win_t, wo, name, carry=None):
    S, D = h.shape
    F = wo.shape[0]
    tm = _tile(S, 256, 16)

    def body(h_ref, g_ref, win_hbm, wo_hbm, out_ref, xn_ref, silu_ref, dsilu_ref, up_ref, act_ref, win_v, wo_v, sems):
        _load_resident([(win_hbm, win_v), (wo_hbm, wo_v)], sems)
        x = h_ref[...]
        r = lax.rsqrt(jnp.mean(x * x, axis=-1, keepdims=True) + RMS_EPS)
        xn = ((x * r) * g_ref[...]).astype(BF16)
        xn_ref[...] = xn
        gate = _dot(xn, win_v[:F, :], NT)
        up = _dot(xn, win_v[F:, :], NT)
        sig = 1.0 / (1.0 + jnp.exp(-gate))
        silu = gate * sig
        act = (silu * up).astype(BF16)
        up_ref[...] = up.astype(BF16)
        silu_ref[...] = silu.astype(BF16)
        dsilu_ref[...] = (sig + silu * (1.0 - sig)).astype(BF16)
        act_ref[...] = act
        out_ref[...] = x + FFN_RES_SCALE * _dot(act, wo_v[...], NN)

    row = pl.BlockSpec((tm, D), lambda i: (i, 0))
    wide = pl.BlockSpec((tm, F), lambda i: (i, 0))
    hbm = pl.BlockSpec(memory_space=pl.ANY)
    hid = jax.ShapeDtypeStruct((S, F), BF16)
    res, got = _pcall(
        body, (h, g.reshape(1, D), win_t, wo), name=name,
        out_shape=(jax.ShapeDtypeStruct((S, D), F32), jax.ShapeDtypeStruct((S, D), BF16), hid, hid, hid, hid),
        grid=(S // tm,),
        in_specs=[row, pl.BlockSpec((1, D), lambda i: (0, 0)), hbm, hbm],
        out_specs=(row, row, wide, wide, wide, wide),
        scratch_shapes=[pltpu.VMEM(win_t.shape, BF16), pltpu.VMEM(wo.shape, BF16), pltpu.SemaphoreType.DMA((2,))],
        sem=("arbitrary",), carry=carry)
    return res[0], tuple(res[1:]), got


def _ffn_bwd_fused(dh, h, g, win_t, wo, silu, dsilu, up, name, carry=None):
    S, D = h.shape
    F = wo.shape[0]
    tm = _tile(S, 256, 16)

    def body(dh_ref, h_ref, g_ref, s_ref, ds_ref, u_ref, win_hbm, wo_hbm,
             dhin_ref, dgain_ref, dgate_ref, dup_ref, win_v, wo_v, sems):
        _load_resident([(win_hbm, win_v), (wo_hbm, wo_v)], sems)
        dhv = dh_ref[...]
        d = _dot(dhv.astype(BF16), wo_v[...], NT) * FFN_RES_SCALE
        dup = (d * s_ref[...].astype(F32)).astype(BF16)
        dgate = (d * u_ref[...].astype(F32) * ds_ref[...].astype(F32)).astype(BF16)
        dup_ref[...] = dup
        dgate_ref[...] = dgate
        dxn = _dot(dgate, win_v[:F, :], NN) + _dot(dup, win_v[F:, :], NN)
        x = h_ref[...]
        r = lax.rsqrt(jnp.mean(x * x, axis=-1, keepdims=True) + RMS_EPS)
        xhat = x * r
        dxh = dxn * g_ref[...]
        c = jnp.mean(dxh * xhat, axis=-1, keepdims=True)
        dhin_ref[...] = r * (dxh - xhat * c) + dhv
        part = _rows8(dxn * xhat)

        @pl.when(pl.program_id(0) == 0)
        def _():
            dgain_ref[...] = part

        @pl.when(pl.program_id(0) > 0)
        def _():
            dgain_ref[...] += part

    row = pl.BlockSpec((tm, D), lambda i: (i, 0))
    wide = pl.BlockSpec((tm, F), lambda i: (i, 0))
    hbm = pl.BlockSpec(memory_space=pl.ANY)
    hid = jax.ShapeDtypeStruct((S, F), BF16)
    return _pcall(
        body, (dh, h, g.reshape(1, D), silu, dsilu, up, win_t, wo), name=name,
        out_shape=(jax.ShapeDtypeStruct((S, D), F32), jax.ShapeDtypeStruct((8, D), F32), hid, hid),
        grid=(S // tm,),
        in_specs=[row, row, pl.BlockSpec((1, D), lambda i: (0, 0)), wide, wide, wide, hbm, hbm],
        out_specs=(row, pl.BlockSpec((8, D), lambda i: (0, 0)), wide, wide),
        scratch_shapes=[pltpu.VMEM(win_t.shape, BF16), pltpu.VMEM(wo.shape, BF16), pltpu.SemaphoreType.DMA((2,))],
        sem=("arbitrary",), carry=carry)


def _rope_tables(S):
    half = HEAD_DIM // 2
    inv_freq = ROPE_THETA ** (-jnp.arange(half, dtype=F32) / half)
    ang = jnp.arange(S).astype(F32)[:, None] * inv_freq[None, :]
    cos, sin = jnp.cos(ang), jnp.sin(ang)
    cos_t = jnp.tile(cos, (1, LANES // half))
    sin_t = jnp.tile(jnp.concatenate([-sin, sin], axis=1), (1, LANES // HEAD_DIM))
    return cos_t, sin_t


def _swap_halves(x):
    lane = lax.broadcasted_iota(jnp.int32, x.shape, 1)
    first = (lane % HEAD_DIM) < (HEAD_DIM // 2)
    return jnp.where(first, pltpu.roll(x, LANES - HEAD_DIM // 2, 1), pltpu.roll(x, HEAD_DIM // 2, 1))


def _rotary(x, cos_t, sin_t, n_rot, inverse, name):
    S, C = x.shape
    ts = _tile(S, 512, 16)
    ng = C // LANES

    def body(x_ref, c_ref, s_ref, o_ref):
        cs, sn = c_ref[...], s_ref[...]
        for gidx in range(ng):
            sl = slice(gidx * LANES, (gidx + 1) * LANES)
            v = x_ref[:, sl].astype(F32)
            if gidx < n_rot:
                if inverse:
                    v = v * cs + _swap_halves(v * sn)
                else:
                    v = v * cs + _swap_halves(v) * sn
            o_ref[:, sl] = v.astype(BF16)

    row = pl.BlockSpec((ts, C), lambda i: (i, 0))
    tab = pl.BlockSpec((ts, LANES), lambda i: (i, 0))
    return pl.pallas_call(
        body, name=name, out_shape=jax.ShapeDtypeStruct((S, C), BF16),
        grid=(S // ts,), in_specs=[row, tab, tab], out_specs=row,
        compiler_params=_params("parallel"),
    )(x, cos_t, sin_t)


def _head_masks():
    lane = lax.broadcasted_iota(jnp.int32, (BLK, LANES), 1)
    return lane < HEAD_DIM


def _split_bf16(x):
    hi = x.astype(BF16)
    lo = (x - hi.astype(F32)).astype(BF16)
    return hi, lo


def _sb_scores(qh, ks, carry, diag, tri_excl, strict):
    n_heads = len(qh)
    zs = [_dot(ks[n], qh[n], NT) for n in range(n_heads)]
    a_l, b_l, split_l = [], [], []
    for z in zs:
        a = jnp.minimum(z, 0.0) - jnp.log(1.0 + jnp.exp(-jnp.abs(z)))
        b = a - z
        if diag:
            b = jnp.where(strict, b, 0.0)
        a_l.append(a)
        b_l.append(b)
        split_l.append(_split_bf16(b))
    sufs = [_dot(tri_excl, hi, NN) + _dot(tri_excl, lo, NN) for hi, lo in split_l]
    w_l = []
    for n in range(n_heads):
        w = jnp.exp(a_l[n] + sufs[n] + carry[n])
        if diag:
            w = jnp.where(strict, w, 0.0)
        w_l.append(w)
    return a_l, b_l, w_l


SB_FWD_PAIRS = 4
SB_BWD_PAIRS = 2
SB_BWD_QBLOCKS = 2


def _any_alive(carries):
    top = carries[0]
    for c in carries[1:]:
        top = jnp.maximum(top, c)
    return (jnp.max(top) > SB_LOG_FLOOR).astype(jnp.int32)


def _sb_masks():
    row = lax.broadcasted_iota(jnp.int32, (BLK, BLK), 0)
    col = lax.broadcasted_iota(jnp.int32, (BLK, BLK), 1)
    tri_excl = jnp.where(col > row, 1.0, 0.0).astype(BF16)
    tri_incl = jnp.where(col >= row, 1.0, 0.0).astype(BF16)
    return row < HEAD_DIM, row < col, tri_excl, tri_incl


def _sb_fwd(qkv, v_t, name, carry=None):
    S, D3 = qkv.shape
    D = D3 // 3
    npair, nb = D // LANES, S // BLK
    P = min(SB_FWD_PAIRS, npair)
    ngroup = npair // P
    W = P * LANES

    def body(q_ref, k_ref, vt_ref, o_ref):
        i = pl.program_id(1)
        m0 = _head_masks()
        top, strict, tri_excl, _ = _sb_masks()
        zq = jnp.zeros((BLK, LANES), BF16)
        lanes = [slice(p * LANES, (p + 1) * LANES) for p in range(P)]
        qh = []
        for sl in lanes:
            q2 = q_ref[:, sl] * ATTN_SCALE
            qh += [jnp.where(m0, q2, zq), jnp.where(m0, zq, q2)]

        def block(j, carry, acc, diag):
            off = pl.multiple_of(j * BLK, BLK)
            ks, vth = [], []
            for sl in lanes:
                k2 = k_ref[pl.ds(off, BLK), sl]
                vt = vt_ref[sl, pl.ds(off, BLK)]
                ks += [k2, k2]
                vth += [jnp.where(top, vt, zq), jnp.where(top, zq, vt)]
            _, b_l, w_l = _sb_scores(qh, ks, carry, diag, tri_excl, strict)
            wb = [w.astype(BF16) for w in w_l]
            new_acc = [acc[p] + _dot(vth[2 * p], wb[2 * p], NN) + _dot(vth[2 * p + 1], wb[2 * p + 1], NN)
                       for p in range(P)]
            new_carry = [carry[n] + jnp.sum(b_l[n], axis=0, keepdims=True) for n in range(2 * P)]
            return new_carry, new_acc

        c0 = jnp.zeros((1, BLK), F32)
        carry, acc = block(i, [c0] * (2 * P), [jnp.zeros((LANES, BLK), F32)] * P, True)

        def cond(st):
            return jnp.logical_and(st[0] >= 0, st[1] > 0)

        def step(st):
            j, _, carry, acc = st
            carry, acc = block(j, carry, acc, False)
            return j - 1, _any_alive(carry), carry, acc

        st = lax.while_loop(cond, step, (i - 1, _any_alive(carry), carry, acc))
        for p, sl in enumerate(lanes):
            o_ref[:, sl] = jnp.transpose(st[3][p])

    return _pcall(
        body, (qkv, qkv, v_t), name=name, out_shape=jax.ShapeDtypeStruct((S, D), F32),
        grid=(ngroup, nb),
        in_specs=[pl.BlockSpec((BLK, W), lambda g, i: (i, g)),
                  pl.BlockSpec((S, W), lambda g, i: (0, ngroup + g)),
                  pl.BlockSpec((W, S), lambda g, i: (g, 0))],
        out_specs=pl.BlockSpec((BLK, W), lambda g, i: (i, g)),
        sem=("arbitrary", "arbitrary"), carry=carry)


def _sb_bwd(qkv, k_t, o, do, name, carry=None):
    S, D3 = qkv.shape
    D = D3 // 3
    npair, nb = D // LANES, S // BLK
    P = min(SB_BWD_PAIRS, npair)
    ngroup = npair // P
    W = P * LANES

    QB = SB_BWD_QBLOCKS if nb % SB_BWD_QBLOCKS == 0 else 1
    nch = QB * 2 * P

    def body(q_ref, o_ref, do_ref, qkv_hbm, kt_hbm, dq_ref, dk_ref, dv_ref, k_ref, v_ref, kt_ref, sems):
        grp = pl.program_id(0)
        i_first = pl.program_id(1) * QB
        m0 = _head_masks()
        top, strict, tri_excl, tri_incl = _sb_masks()
        zq = jnp.zeros((BLK, LANES), BF16)
        lanes = [slice(p * LANES, (p + 1) * LANES) for p in range(P)]

        @pl.when(pl.program_id(1) == 0)
        def _():
            copies = [pltpu.make_async_copy(qkv_hbm.at[:, pl.ds(pl.multiple_of((c * ngroup + grp) * W, LANES), W)],
                                            ref, sems.at[c - 1]) for c, ref in ((1, k_ref), (2, v_ref))]
            copies.append(pltpu.make_async_copy(kt_hbm.at[pl.ds(pl.multiple_of(grp * W, LANES), W), :],
                                                kt_ref, sems.at[2]))
            for cp in copies:
                cp.start()
            dk_ref[...] = jnp.zeros_like(dk_ref)
            dv_ref[...] = jnp.zeros_like(dv_ref)
            for cp in copies:
                cp.wait()

        qh, doh, delta = [], [], []
        for qb in range(QB):
            rs = slice(qb * BLK, (qb + 1) * BLK)
            for sl in lanes:
                q2, do2 = q_ref[rs, sl] * ATTN_SCALE, do_ref[rs, sl]
                qh += [jnp.where(m0, q2, zq), jnp.where(m0, zq, q2)]
                doh += [jnp.where(m0, do2, zq), jnp.where(m0, zq, do2)]
                prod_t = jnp.transpose(do2.astype(F32) * o_ref[rs, sl])
                delta += [jnp.sum(jnp.where(top, prod_t, 0.0), axis=0, keepdims=True),
                          jnp.sum(jnp.where(top, 0.0, prod_t), axis=0, keepdims=True)]

        def block(js, valid, cb, cg, dq, diag):
            offs = [pl.multiple_of(j * BLK, BLK) for j in js]
            ks, vs, kth = [], [], []
            for qb in range(QB):
                for sl in lanes:
                    k2, v2 = k_ref[pl.ds(offs[qb], BLK), sl], v_ref[pl.ds(offs[qb], BLK), sl]
                    ks += [k2, k2]
                    vs += [v2, v2]
                    kt = kt_ref[sl, pl.ds(offs[qb], BLK)] * ATTN_SCALE
                    kth += [jnp.where(top, kt, zq), jnp.where(top, zq, kt)]
            dws = [_dot(vs[n], doh[n], NT) for n in range(nch)]
            a_l, b_l, w_l = _sb_scores(qh, ks, cb, diag, tri_excl, strict)
            wb = [w.astype(BF16) for w in w_l]
            g_l = [dws[n] * wb[n].astype(F32) for n in range(nch)]
            gsplit = [_split_bf16(g) for g in g_l]
            gincs = [_dot(tri_incl, hi, NN) + _dot(tri_incl, lo, NN) for hi, lo in gsplit]
            dzs = []
            for n in range(nch):
                beta = jnp.exp(a_l[n])
                dz = g_l[n] - beta * (g_l[n] + ((delta[n] - cg[n]) - gincs[n]))
                if diag:
                    dz = jnp.where(strict, dz, 0.0)
                if valid[n // (2 * P)] is not None:
                    dz = jnp.where(valid[n // (2 * P)], dz, 0.0)
                dzs.append(dz.astype(BF16))
            ndq = []
            for qb in range(QB):
                for p, sl in enumerate(lanes):
                    n0 = qb * 2 * P + 2 * p
                    ndq.append(dq[qb * P + p] + _dot(kth[n0], dzs[n0], NN) + _dot(kth[n0 + 1], dzs[n0 + 1], NN))
                    dk_ref[pl.ds(offs[qb], BLK), sl] += _dot(dzs[n0], qh[n0], NN) + _dot(dzs[n0 + 1], qh[n0 + 1], NN)
                    dv_ref[pl.ds(offs[qb], BLK), sl] += _dot(wb[n0], doh[n0], NN) + _dot(wb[n0 + 1], doh[n0 + 1], NN)
            ncb = [cb[n] + jnp.sum(b_l[n], axis=0, keepdims=True) for n in range(nch)]
            ncg = [cg[n] + jnp.sum(g_l[n], axis=0, keepdims=True) for n in range(nch)]
            return ncb, ncg, ndq

        c0 = jnp.zeros((1, BLK), F32)
        cb, cg, dq = block([i_first + qb for qb in range(QB)], [None] * QB, [c0] * nch, [c0] * nch,
                           [jnp.zeros((LANES, BLK), F32)] * (QB * P), True)

        def cond(st):
            return jnp.logical_and(i_first + QB - 1 - st[0] >= 0, st[1] > 0)

        def step(st):
            t, _, cb, cg, dq = st
            js = [i_first + qb - t for qb in range(QB)]
            valid = [js[qb] >= 0 for qb in range(QB - 1)] + [None]
            cb = [cb[n] if valid[n // (2 * P)] is None else jnp.where(valid[n // (2 * P)], cb[n], NEG_BIG)
                  for n in range(nch)]
            cb, cg, dq = block([jnp.maximum(j, 0) for j in js], valid, cb, cg, dq, False)
            return t + 1, _any_alive(cb), cb, cg, dq

        st = lax.while_loop(cond, step, (1, _any_alive(cb), cb, cg, dq))
        for qb in range(QB):
            for p, sl in enumerate(lanes):
                dq_ref[qb * BLK:(qb + 1) * BLK, sl] = jnp.transpose(st[4][qb * P + p]).astype(BF16)

    blk = pl.BlockSpec((QB * BLK, W), lambda g, i: (i, g))
    col_all = pl.BlockSpec((S, W), lambda g, i: (0, g))
    hbm = pl.BlockSpec(memory_space=pl.ANY)
    return _pcall(
        body, (qkv, o, do, qkv, k_t), name=name,
        out_shape=(jax.ShapeDtypeStruct((S, D), BF16), jax.ShapeDtypeStruct((S, D), F32),
                   jax.ShapeDtypeStruct((S, D), F32)),
        grid=(ngroup, nb // QB),
        in_specs=[blk, blk, blk, hbm, hbm],
        out_specs=(blk, col_all, col_all),
        scratch_shapes=[pltpu.VMEM((S, W), BF16), pltpu.VMEM((S, W), BF16), pltpu.VMEM((W, S), BF16),
                        pltpu.SemaphoreType.DMA((3,))],
        sem=("arbitrary", "arbitrary"), carry=carry)


SWA_Q_GROUPS = 4


def _roll_heads(x):
    return pltpu.roll(x.astype(F32), HEAD_DIM, 1).astype(BF16)


def _swa_valid(i):
    r = lax.broadcasted_iota(jnp.int32, (BLK, 2 * BLK), 0)
    c = lax.broadcasted_iota(jnp.int32, (BLK, 2 * BLK), 1)
    diff = r + BLK - c
    return (diff >= 0) & (diff < BLK) & ((i > 0) | (c >= BLK))


def _swa_probs(z, valid, sink):
    z = jnp.where(valid, z * ATTN_SCALE, NEG_BIG)
    mx = jnp.maximum(jnp.max(z, axis=1, keepdims=True), sink)
    p = jnp.exp(z - mx)
    ps = jnp.exp(sink - mx)
    inv = 1.0 / (jnp.sum(p, axis=1, keepdims=True) + ps)
    return p * inv, ps * inv


def _swa_operands(q_ref, kc_ref, kp_ref, vc_ref, vp_ref, s_ref, m):
    m0 = _head_masks()
    m0k = jnp.concatenate([m0, m0], axis=0)
    kk = jnp.concatenate([kp_ref[...], kc_ref[...]], axis=0)
    vv = jnp.concatenate([vp_ref[...], vc_ref[...]], axis=0)
    ksw, vsw = _roll_heads(kk), _roll_heads(vv)
    zk = jnp.zeros_like(kk)
    heads = []
    for c in range(SWA_Q_GROUPS):
        qc = q_ref[:, c * LANES:(c + 1) * LANES]
        zq = jnp.zeros_like(qc)
        for u in range(2):
            same = u == c // 2
            sel = (lambda x, z, mk: jnp.where(mk, x, z)) if u == 0 else (lambda x, z, mk: jnp.where(mk, z, x))
            heads.append(dict(
                c=c, same=same, sel=sel,
                qm=sel(qc, zq, m0),
                k=kk if same else ksw, v=vv if same else vsw,
                km=sel(kk if same else ksw, zk, m0k), vm=sel(vv if same else vsw, zk, m0k),
                sink=s_ref[0, m * 2 * SWA_Q_GROUPS + 2 * c + u]))
    return heads, m0


def _swa_fwd(q, kv, sinks, name):
    S, D = q.shape
    nkvp = kv.shape[1] // (2 * LANES)
    nb = S // BLK
    qw = SWA_Q_GROUPS * LANES

    def body(q_ref, kc_ref, kp_ref, vc_ref, vp_ref, s_ref, o_ref):
        m, i = pl.program_id(0), pl.program_id(1)
        valid = _swa_valid(i)
        heads, _ = _swa_operands(q_ref, kc_ref, kp_ref, vc_ref, vp_ref, s_ref, m)
        zs = [_dot(hd["qm"], hd["k"], NT) for hd in heads]
        ps = [_swa_probs(z, valid, hd["sink"])[0].astype(BF16) for z, hd in zip(zs, heads)]
        for c in range(SWA_Q_GROUPS):
            o_ref[:, c * LANES:(c + 1) * LANES] = (_dot(ps[2 * c], heads[2 * c]["vm"], NN)
                                                   + _dot(ps[2 * c + 1], heads[2 * c + 1]["vm"], NN))

    prev = lambda i: jnp.maximum(i - 1, 0)
    return pl.pallas_call(
        body, name=name, out_shape=jax.ShapeDtypeStruct((S, D), F32),
        grid=(nkvp, nb),
        in_specs=[pl.BlockSpec((BLK, qw), lambda m, i: (i, m)),
                  pl.BlockSpec((BLK, LANES), lambda m, i: (i, m)),
                  pl.BlockSpec((BLK, LANES), lambda m, i: (prev(i), m)),
                  pl.BlockSpec((BLK, LANES), lambda m, i: (i, nkvp + m)),
                  pl.BlockSpec((BLK, LANES), lambda m, i: (prev(i), nkvp + m)),
                  pl.BlockSpec(memory_space=pltpu.SMEM)],
        out_specs=pl.BlockSpec((BLK, qw), lambda m, i: (i, m)),
        compiler_params=_params("arbitrary", "arbitrary"),
    )(q, kv, kv, kv, kv, sinks)


def _swa_bwd(q, kv, sinks, o, do, name, carry=None):
    S, D = q.shape
    nkvp = kv.shape[1] // (2 * LANES)
    nb = S // BLK
    qw = SWA_Q_GROUPS * LANES
    nh = 2 * SWA_Q_GROUPS

    def body(q_ref, kc_ref, kp_ref, vc_ref, vp_ref, s_ref, o_ref, do_ref, dq_ref, dk_ref, dv_ref, ds_ref):
        m, i = pl.program_id(0), pl.program_id(1)
        valid = _swa_valid(i)
        heads, m0 = _swa_operands(q_ref, kc_ref, kp_ref, vc_ref, vp_ref, s_ref, m)

        @pl.when(i == 0)
        def _():
            dk_ref[...] = jnp.zeros_like(dk_ref)
            dv_ref[...] = jnp.zeros_like(dv_ref)
            ds_ref[...] = jnp.zeros_like(ds_ref)

        doms, deltas = [], []
        for hd in heads:
            c = hd["c"]
            doc = do_ref[:, c * LANES:(c + 1) * LANES]
            prod = doc.astype(F32) * o_ref[:, c * LANES:(c + 1) * LANES]
            doms.append(hd["sel"](doc, jnp.zeros_like(doc), m0))
            deltas.append(jnp.sum(hd["sel"](prod, 0.0, m0), axis=1, keepdims=True))
        zs = [_dot(hd["qm"], hd["k"], NT) for hd in heads]
        dps = [_dot(dom, hd["v"], NT) for dom, hd in zip(doms, heads)]
        pbs, dscs = [], []
        for n, hd in enumerate(heads):
            p, psink = _swa_probs(zs[n], valid, hd["sink"])
            pbs.append(p.astype(BF16))
            dscs.append((p * (dps[n] - deltas[n]) * ATTN_SCALE).astype(BF16))
            dsink = jnp.sum(jnp.broadcast_to(-(psink * deltas[n]), (BLK, LANES)), axis=0, keepdims=True)
            ds_ref[0, n:n + 1, :] += dsink
        for c in range(SWA_Q_GROUPS):
            dq_ref[:, c * LANES:(c + 1) * LANES] = (_dot(dscs[2 * c], heads[2 * c]["km"], NN)
                                                    + _dot(dscs[2 * c + 1], heads[2 * c + 1]["km"], NN))
        acc = {}
        for n, hd in enumerate(heads):
            dk_n = _dot(dscs[n], hd["qm"], TN)
            dv_n = _dot(pbs[n], doms[n], TN)
            for key, val in ((("k", hd["same"]), dk_n), (("v", hd["same"]), dv_n)):
                acc[key] = val if key not in acc else acc[key] + val
        dkk = acc["k", True] + pltpu.roll(acc["k", False], HEAD_DIM, 1)
        dvv = acc["v", True] + pltpu.roll(acc["v", False], HEAD_DIM, 1)
        poff = pl.multiple_of(jnp.maximum(i - 1, 0) * BLK, BLK)
        coff = pl.multiple_of(i * BLK, BLK)
        dk_ref[pl.ds(poff, BLK), :] += dkk[:BLK]
        dv_ref[pl.ds(poff, BLK), :] += dvv[:BLK]
        dk_ref[pl.ds(coff, BLK), :] += dkk[BLK:]
        dv_ref[pl.ds(coff, BLK), :] += dvv[BLK:]

    prev = lambda i: jnp.maximum(i - 1, 0)
    qblk = pl.BlockSpec((BLK, qw), lambda m, i: (i, m))
    col_all = pl.BlockSpec((S, LANES), lambda m, i: (0, m))
    return _pcall(
        body, (q, kv, kv, kv, kv, sinks, o, do), name=name,
        out_shape=(jax.ShapeDtypeStruct((S, D), F32),
                   jax.ShapeDtypeStruct((S, nkvp * LANES), F32),
                   jax.ShapeDtypeStruct((S, nkvp * LANES), F32),
                   jax.ShapeDtypeStruct((nkvp, nh, LANES), F32)),
        grid=(nkvp, nb),
        in_specs=[qblk,
                  pl.BlockSpec((BLK, LANES), lambda m, i: (i, m)),
                  pl.BlockSpec((BLK, LANES), lambda m, i: (prev(i), m)),
                  pl.BlockSpec((BLK, LANES), lambda m, i: (i, nkvp + m)),
                  pl.BlockSpec((BLK, LANES), lambda m, i: (prev(i), nkvp + m)),
                  pl.BlockSpec(memory_space=pltpu.SMEM),
                  qblk, qblk],
        out_specs=(qblk, col_all, col_all, pl.BlockSpec((1, nh, LANES), lambda m, i: (m, 0, 0))),
        sem=("arbitrary", "arbitrary"), carry=carry)


def _dev_index(p):
    return 4 * p[0] + 2 * p[1] + p[2]


def _gather_plan(x_refs, out_refs, send_sems, recv_sems, local_sems):
    n = len(x_refs)
    x_, y_, c_ = lax.axis_index("x"), lax.axis_index("y"), lax.axis_index("c")
    me, sibling = (x_, y_, c_), (x_, y_, 1 - c_)
    chips = [(1 - x_, y_), (x_, 1 - y_), (1 - x_, 1 - y_)]

    def copy(t, k, block, to, src=None):
        dst = out_refs[t].at[_dev_index(block)]
        return pltpu.make_async_remote_copy(
            src_ref=dst if src is None else src, dst_ref=dst,
            send_sem=send_sems.at[7 * t + k], recv_sem=recv_sems.at[7 * t + k],
            device_id=to, device_id_type=MESH)

    mine = [pltpu.make_async_copy(x_refs[t], out_refs[t].at[_dev_index(me)], local_sems.at[t]) for t in range(n)]
    first = []
    for t in range(n):
        first.append(copy(t, 0, me, sibling, src=x_refs[t]))
        first += [copy(t, 1 + j, me, (*chip, c_), src=x_refs[t]) for j, chip in enumerate(chips)]
    arrived = lambda t, j: copy(t, 1 + j, (*chips[j], c_), me)
    forward = lambda t, j: copy(t, 4 + j, (*chips[j], c_), sibling)
    from_sibling = lambda t: copy(t, 0, sibling, me)
    forwarded = lambda t, j: copy(t, 4 + j, (*chips[j], 1 - c_), me)
    return n, mine, first, arrived, forward, from_sibling, forwarded


def _gather_start(x_refs, out_refs, send_sems, recv_sems, local_sems):
    _, mine, first, *_ = _gather_plan(x_refs, out_refs, send_sems, recv_sems, local_sems)
    for cp in mine + first:
        cp.start()


def _gather_forward(x_refs, out_refs, send_sems, recv_sems, local_sems):
    n, _, _, arrived, forward, _, _ = _gather_plan(x_refs, out_refs, send_sems, recv_sems, local_sems)
    for j in range(3):
        for t in range(n):
            arrived(t, j).wait_recv()
            forward(t, j).start()


def _gather_finish(x_refs, out_refs, send_sems, recv_sems, local_sems):
    n, mine, first, _, forward, from_sibling, forwarded = _gather_plan(
        x_refs, out_refs, send_sems, recv_sems, local_sems)
    for t in range(n):
        from_sibling(t).wait_recv()
    for j in range(3):
        for t in range(n):
            forwarded(t, j).wait_recv()
    for cp in first + [forward(t, j) for j in range(3) for t in range(n)]:
        cp.wait_send()
    for cp in mine:
        cp.wait()


def _scatter_plan(b_refs, out_refs, send_sems, recv_sems, local_sems):
    n = len(b_refs)
    x_, y_, c_ = lax.axis_index("x"), lax.axis_index("y"), lax.axis_index("c")
    my_idx = _dev_index((x_, y_, c_))
    mine = [pltpu.make_async_copy(b_refs[t].at[my_idx], out_refs[t].at[my_idx], local_sems.at[t]) for t in range(n)]
    copies = []
    for t in range(n):
        for k in range(1, N_DEV):
            peer = (x_ ^ ((k >> 2) & 1), y_ ^ ((k >> 1) & 1), c_ ^ (k & 1))
            copies.append(pltpu.make_async_remote_copy(
                src_ref=b_refs[t].at[_dev_index(peer)], dst_ref=out_refs[t].at[my_idx],
                send_sem=send_sems.at[7 * t + k - 1], recv_sem=recv_sems.at[7 * t + k - 1],
                device_id=peer, device_id_type=MESH))
    return mine, copies


def _scatter_start(b_refs, out_refs, send_sems, recv_sems, local_sems):
    mine, copies = _scatter_plan(b_refs, out_refs, send_sems, recv_sems, local_sems)
    for cp in mine + copies:
        cp.start()


def _scatter_finish(b_refs, out_refs, send_sems, recv_sems, local_sems):
    mine, copies = _scatter_plan(b_refs, out_refs, send_sems, recv_sems, local_sems)
    for cp in copies:
        cp.wait_recv()
    for cp in copies:
        cp.wait_send()
    for cp in mine:
        cp.wait()


def _exchange_operands(kind, tensors):
    if kind == "gather":
        args = list(tensors)
        shapes = [jax.ShapeDtypeStruct((N_DEV,) + t.shape, t.dtype) for t in tensors]
        return args, shapes, (_gather_start, _gather_forward, _gather_finish)
    args = [t.reshape(N_DEV, t.shape[0] // N_DEV, t.shape[1]) for t in tensors]
    shapes = [jax.ShapeDtypeStruct(a.shape, a.dtype) for a in args]
    return args, shapes, (_scatter_start, None, _scatter_finish)


def _exchange_results(kind, tensors, res):
    if kind == "gather":
        return [r.reshape(N_DEV * t.shape[0], t.shape[1]) for r, t in zip(res, tensors)]
    return list(res)


def _exchange_sems(n):
    return [pltpu.SemaphoreType.DMA((7 * n,)), pltpu.SemaphoreType.DMA((7 * n,)), pltpu.SemaphoreType.DMA((n,))]


def _exchange(kind, tensors, name):
    n = len(tensors)
    args, shapes, phases = _exchange_operands(kind, tensors)

    def body(*refs):
        for phase in phases:
            if phase is not None:
                phase(refs[:n], refs[n:2 * n], *refs[2 * n:])

    hbm = pl.BlockSpec(memory_space=pl.ANY)
    res = pl.pallas_call(body, name=name, out_shape=shapes, in_specs=[hbm] * n, out_specs=[hbm] * n,
                         scratch_shapes=_exchange_sems(n))(*args)
    return _exchange_results(kind, tensors, res)


def _pcall(body, args, *, name, out_shape, grid, in_specs, out_specs, sem, scratch_shapes=(), carry=None):
    if carry is None:
        out = pl.pallas_call(body, name=name, out_shape=out_shape, grid=grid, in_specs=list(in_specs),
                             out_specs=out_specs, scratch_shapes=list(scratch_shapes),
                             compiler_params=_params(*sem))(*args)
        return out, None
    kind, tensors = carry
    multi = isinstance(out_shape, (tuple, list))
    shapes = list(out_shape) if multi else [out_shape]
    ospecs = list(out_specs) if multi else [out_specs]
    n_in, n_out, n_scr, n_c = len(in_specs), len(shapes), len(scratch_shapes), len(tensors)
    c_args, c_shapes, (start, forward, finish) = _exchange_operands(kind, tensors)
    n_steps = 1
    for g in grid:
        n_steps *= g
    late = (3 * n_steps) // 4

    def wrapped(*refs):
        ins, rest = refs[:n_in], refs[n_in:]
        c_in, rest = rest[:n_c], rest[n_c:]
        outs, rest = rest[:n_out], rest[n_out:]
        c_out, rest = rest[:n_c], rest[n_c:]
        scr, sems = rest[:n_scr], rest[n_scr:]
        step = pl.program_id(0)
        for a in range(1, len(grid)):
            step = step * grid[a] + pl.program_id(a)

        @pl.when(step == 0)
        def _():
            start(c_in, c_out, *sems)

        body(*ins, *outs, *scr)

        if forward is not None:
            @pl.when(step == late)
            def _():
                forward(c_in, c_out, *sems)

        @pl.when(step == n_steps - 1)
        def _():
            finish(c_in, c_out, *sems)

    hbm = pl.BlockSpec(memory_space=pl.ANY)
    res = pl.pallas_call(
        wrapped, name=name, out_shape=shapes + c_shapes, grid=grid,
        in_specs=list(in_specs) + [hbm] * n_c, out_specs=ospecs + [hbm] * n_c,
        scratch_shapes=list(scratch_shapes) + _exchange_sems(n_c),
        compiler_params=_params(*sem))(*args, *c_args)
    outs = tuple(res[:n_out]) if multi else res[0]
    return outs, _exchange_results(kind, tensors, res[n_out:])


def _sum8(parts, name):
    _, R, C = parts.shape
    tr = _tile(R, 256, 16)

    def body(p_ref, g_ref):
        g = p_ref[0].astype(F32)
        for s in range(1, N_DEV):
            g = g + p_ref[s].astype(F32)
        g_ref[...] = g

    return pl.pallas_call(
        body, name=name, out_shape=jax.ShapeDtypeStruct((R, C), F32),
        grid=(R // tr,),
        in_specs=[pl.BlockSpec((N_DEV, tr, C), lambda i: (0, i, 0))],
        out_specs=pl.BlockSpec((tr, C), lambda i: (i, 0)),
        compiler_params=_params("parallel"),
    )(parts)


def _adamw(g, w, m, v, name):
    R, C = g.shape
    tr = _tile(R, 256, 8)
    c1 = 1.0 - ADAM_B1 ** ADAM_STEP
    c2 = 1.0 - ADAM_B2 ** ADAM_STEP

    def body(g_ref, w_ref, m_ref, v_ref, d_ref, nm_ref, nv_ref):
        gg = g_ref[...]
        nm = ADAM_B1 * m_ref[...] + (1.0 - ADAM_B1) * gg
        nv = ADAM_B2 * v_ref[...] + (1.0 - ADAM_B2) * (gg * gg)
        m_hat = nm / c1
        v_hat = nv / c2
        nm_ref[...] = nm
        nv_ref[...] = nv
        d_ref[...] = -ADAM_LR * (m_hat / (jnp.sqrt(v_hat) + ADAM_EPS) + ADAM_WD * w_ref[...])

    row = pl.BlockSpec((tr, C), lambda i: (i, 0))
    shp = jax.ShapeDtypeStruct((R, C), F32)
    return pl.pallas_call(
        body, name=name, out_shape=(shp, shp, shp),
        grid=(R // tr,), in_specs=[row, row, row, row], out_specs=(row, row, row),
        compiler_params=_params("parallel"),
    )(g, w, m, v)


def _ffn_down(act, wo, h, tag):
    return _mm(act, wo, NN, F32, f"{tag}_down", scale=FFN_RES_SCALE, res=h, tm=512, tn=1024, tk=2816)


def _ffn_fwd(h, g, win_t, wo, tag, carry=None):
    return _ffn_fwd_fused(h, g, win_t, wo, f"{tag}_fwd", carry=carry)


def _ffn_bwd(dh, h, g, win_t, wo, saved, tag, scatter=False, carry=None):
    xn, silu, dsilu, up, act = saved
    dwo = _mm(act, dh, TN, BF16, f"{tag}_dwo", scale=FFN_RES_SCALE, tm=1408, tn=1024, tk=TN_CHUNK)
    if not scatter:
        (dh_in, dg, dgate, dup), got = _ffn_bwd_fused(dh, h, g, win_t, wo, silu, dsilu, up, f"{tag}_bwd", carry=carry)
        dwin_t, _ = _dw_rows([dgate, dup], xn, f"{tag}_dwin")
        return dh_in, dg, dwin_t, dwo, got
    dgate, dup = _ffn_dact(dh, wo, silu, dsilu, up, f"{tag}_dact")
    dwin_t, got_wo = _dw_rows([dgate, dup], xn, f"{tag}_dwin", carry=("scatter", [dwo]))
    (dh_in, dg), got_win = _dx_norm_bwd([(dgate, win_t, NN, 2, 0), (dup, win_t, NN, 2, 1)], h, g, dh, f"{tag}_dx",
                                        carry=("scatter", [dwin_t]))
    return dh_in, dg, got_win[0], got_wo[0]


def _proj(a, w, dims, out_dtype, name, res=None):
    return _mm(a, w, dims, out_dtype, name, res=res, tm=1024, tn=1024, tk=1024)


def _proj_dw(x, dy, name):
    return _mm(x, dy, TN, BF16, name, tm=1024, tn=1024, tk=TN_CHUNK)


def kernel(x, ffn1_norm, ffn1_w_in, ffn1_w_out, mix_norm, ffn2_norm, ffn2_w_in, ffn2_w_out, sb_w_qkv, sb_w_o, kv_norm, kv_w, swa_w_q, swa_sinks, swa_w_o, final_norm, loss_target, m_ffn1_norm, m_ffn1_w_in, m_ffn1_w_out, m_mix_norm, m_ffn2_norm, m_ffn2_w_in, m_ffn2_w_out, m_sb_w_qkv, m_sb_w_o, m_kv_norm, m_kv_w, m_swa_w_q, m_swa_sinks, m_swa_w_o, m_final_norm, v_ffn1_norm, v_ffn1_w_in, v_ffn1_w_out, v_mix_norm, v_ffn2_norm, v_ffn2_w_in, v_ffn2_w_out, v_sb_w_qkv, v_sb_w_o, v_kv_norm, v_kv_w, v_swa_w_q, v_swa_sinks, v_swa_w_o, v_final_norm):
    S, D = x.shape[1], x.shape[2]
    L = ffn1_w_in.shape[0]
    KV = kv_w.shape[1]
    assert L == 2 and swa_sinks.shape == (1, 2 * SWA_Q_GROUPS * KV // (2 * LANES))

    def bf(w):
        return w.astype(BF16)

    def bft(w):
        return jnp.transpose(w).astype(BF16)

    cos_t, sin_t = _rope_tables(S)
    h0 = x.reshape(S, D)
    tgt = loss_target.reshape(S, D)

    win1a_t, = _exchange("gather", [bft(ffn1_w_in[0])], "gather_first_weight")
    sv_a1, (wo1a, wqkv_t, w_sbo) = _ffn_up(
        h0, ffn1_norm[0], win1a_t, "ffn1a_up",
        carry=("gather", [bf(ffn1_w_out[0]), bft(sb_w_qkv[0]), bf(sb_w_o[0])]))
    h1 = _ffn_down(sv_a1[-1], wo1a, h0, "ffn1a")
    hn_a, qkv = _norm_proj(h1, mix_norm[0], wqkv_t, NT, "sb_qkv")
    k_t, v_t = jnp.transpose(qkv[:, D:2 * D]), jnp.transpose(qkv[:, 2 * D:])
    o_sb, (win2a_t, wo2a, w_kv) = _sb_fwd(qkv, v_t, "sb_attn", carry=("gather", [
        bft(ffn2_w_in[0]), bf(ffn2_w_out[0]), bf(kv_w)]))
    h2 = _proj(o_sb, w_sbo, NN, F32, "sb_out", res=h1)
    h3, sv_a2, (win1b_t, wo1b, w_q, w_swo) = _ffn_fwd(h2, ffn2_norm[0], win2a_t, wo2a, "ffn2a", carry=("gather", [
        bft(ffn1_w_in[1]), bf(ffn1_w_out[1]), bf(swa_w_q[0]), bf(swa_w_o[0])]))
    kvn, kv_rot = _norm_proj(h3, kv_norm, w_kv, NN, "kv_proj", rope=(cos_t, sin_t, KV // (2 * LANES)))
    h4, sv_b1, (win2b_t, wo2b) = _ffn_fwd(h3, ffn1_norm[1], win1b_t, wo1b, "ffn1b", carry=("gather", [
        bft(ffn2_w_in[1]), bf(ffn2_w_out[1])]))
    hn_b, q_rot = _norm_proj(h4, mix_norm[1], w_q, NN, "swa_q", rope=(cos_t, sin_t, D // LANES))
    o_sw = _swa_fwd(q_rot, kv_rot, swa_sinks, "swa_attn")
    h5 = _proj(o_sw, w_swo, NN, F32, "swa_out", res=h4)
    h6, sv_b2, _ = _ffn_fwd(h5, ffn2_norm[1], win2b_t, wo2b, "ffn2b")
    dh6, dg_final, sq_err = _final_loss(h6, final_norm, tgt, "final_loss")
    loss = lax.psum(0.5 * jnp.sum(sq_err) / D, ("x", "y", "c"))

    dh5, dg_f2b, dwin2b_t, dwo2b, _ = _ffn_bwd(dh6, h5, ffn2_norm[1], win2b_t, wo2b, sv_b2, "ffn2b")
    do_sw = _proj(dh5, w_swo, NT, BF16, "swa_out_dx")
    dw_swo = _proj_dw(o_sw, dh5, "swa_out_dw")
    (dq_rot, dk_sw, dv_sw, dsink), (p_win2b, p_wo2b, p_swo) = _swa_bwd(
        q_rot, kv_rot, swa_sinks, o_sw, do_sw, "swa_attn_bwd", carry=("scatter", [dwin2b_t, dwo2b, dw_swo]))
    dq = _rotary(dq_rot, cos_t, sin_t, D // LANES, True, "q_rope_bwd")
    dw_q = _proj_dw(hn_b, dq, "swa_q_dw")
    (dh4, dg_mix_b), _ = _dx_norm_bwd([(dq, w_q, NT, 1, 0)], h4, mix_norm[1], dh5, "swa_q_dx")
    dh3, dg_f1b, dwin1b_t, dwo1b, (p_q,) = _ffn_bwd(dh4, h3, ffn1_norm[1], win1b_t, wo1b, sv_b1, "ffn1b",
                                                    carry=("scatter", [dw_q]))
    dkv = _rotary(jnp.concatenate([dk_sw, dv_sw], axis=1), cos_t, sin_t, KV // (2 * LANES), True, "kv_rope_bwd")
    dw_kv = _proj_dw(kvn, dkv, "kv_proj_dw")
    (dh3, dg_kv), _ = _dx_norm_bwd([(dkv, w_kv, NT, 1, 0)], h3, kv_norm, dh3, "kv_proj_dx")
    dh2, dg_f2a, dwin2a_t, dwo2a, (p_win1b, p_wo1b, p_kv) = _ffn_bwd(
        dh3, h2, ffn2_norm[0], win2a_t, wo2a, sv_a2, "ffn2a", carry=("scatter", [dwin1b_t, dwo1b, dw_kv]))
    do_sb = _proj(dh2, w_sbo, NT, BF16, "sb_out_dx")
    dw_sbo = _proj_dw(o_sb, dh2, "sb_out_dw")
    (dq_sb, dk_sb, dv_sb), (p_win2a, p_wo2a, p_sbo) = _sb_bwd(
        qkv, k_t, o_sb, do_sb, "sb_attn_bwd", carry=("scatter", [dwin2a_t, dwo2a, dw_sbo]))
    dqkv = [dq_sb, dk_sb, dv_sb]
    dwqkv_t, _ = _dw_rows(dqkv, hn_a, "sb_qkv_dw", tk=TN_CHUNK // 2)
    (dh1, dg_mix_a), (p_qkv,) = _dx_norm_bwd([(dy, wqkv_t, NN, 3, n) for n, dy in enumerate(dqkv)], h1, mix_norm[0],
                                             dh2, "sb_qkv_dx", carry=("scatter", [dwqkv_t]))
    dx, dg_f1a, p_win1a, p_wo1a = _ffn_bwd(dh1, h0, ffn1_norm[0], win1a_t, wo1a, sv_a1, "ffn1a", scatter=True)

    def natural(parts, tag):
        return _sum8(parts, f"sum_{tag}")

    def from_t(parts, tag):
        return jnp.transpose(_sum8(parts, f"sum_{tag}"))

    grads = {
        "ffn1_w_in": jnp.stack([from_t(p_win1a, "win1a"), from_t(p_win1b, "win1b")]),
        "ffn1_w_out": jnp.stack([natural(p_wo1a, "wo1a"), natural(p_wo1b, "wo1b")]),
        "ffn2_w_in": jnp.stack([from_t(p_win2a, "win2a"), from_t(p_win2b, "win2b")]),
        "ffn2_w_out": jnp.stack([natural(p_wo2a, "wo2a"), natural(p_wo2b, "wo2b")]),
        "sb_w_qkv": from_t(p_qkv, "qkv")[None],
        "sb_w_o": natural(p_sbo, "sbo")[None],
        "kv_w": natural(p_kv, "kv"),
        "swa_w_q": natural(p_q, "swq")[None],
        "swa_w_o": natural(p_swo, "swo")[None],
    }

    small_w = [ffn1_norm, mix_norm, ffn2_norm, kv_norm, final_norm, swa_sinks]
    small_m = [m_ffn1_norm, m_mix_norm, m_ffn2_norm, m_kv_norm, m_final_norm, m_swa_sinks]
    small_v = [v_ffn1_norm, v_mix_norm, v_ffn2_norm, v_kv_norm, v_final_norm, v_swa_sinks]
    SMALL_ROWS = 16

    def pack_small(ts):
        rows_ = [t.reshape(-1, D) for t in ts[:-1]]
        sink_row = jnp.pad(ts[-1].reshape(1, -1), ((0, 0), (0, D - ts[-1].size)))
        flat = jnp.concatenate(rows_ + [sink_row], axis=0)
        return jnp.pad(flat, ((0, SMALL_ROWS - flat.shape[0]), (0, 0)))

    def unpack_small(flat):
        out, r = [], 0
        for t in small_w[:-1]:
            n = t.size // D
            out.append(flat[r:r + n].reshape(t.shape))
            r += n
        out.append(flat[r, :swa_sinks.size].reshape(swa_sinks.shape))
        return out

    def gain(parts8):
        return jnp.sum(parts8, axis=0, keepdims=True)

    g_small_local = pack_small([
        jnp.concatenate([gain(dg_f1a), gain(dg_f1b)], axis=0),
        jnp.concatenate([gain(dg_mix_a), gain(dg_mix_b)], axis=0),
        jnp.concatenate([gain(dg_f2a), gain(dg_f2b)], axis=0),
        gain(dg_kv), gain(dg_final), dsink[:, :, 0].reshape(1, -1)])
    small_parts = _exchange("gather", [g_small_local], "gather_small_grads")[0]
    g_small = _sum8(small_parts.reshape(N_DEV, SMALL_ROWS, D), "sum_small")
    d_small, nm_small, nv_small = _adamw(g_small, pack_small(small_w), pack_small(small_m), pack_small(small_v), "adamw_small")
    small_names = ["ffn1_norm", "mix_norm", "ffn2_norm", "kv_norm", "final_norm", "swa_sinks"]
    result = {"grad": dict(zip(small_names, unpack_small(g_small))),
              "delta": dict(zip(small_names, unpack_small(d_small))),
              "new_m": dict(zip(small_names, unpack_small(nm_small))),
              "new_v": dict(zip(small_names, unpack_small(nv_small)))}

    big = {"ffn1_w_in": (ffn1_w_in, m_ffn1_w_in, v_ffn1_w_in), "ffn1_w_out": (ffn1_w_out, m_ffn1_w_out, v_ffn1_w_out),
           "ffn2_w_in": (ffn2_w_in, m_ffn2_w_in, v_ffn2_w_in), "ffn2_w_out": (ffn2_w_out, m_ffn2_w_out, v_ffn2_w_out),
           "sb_w_qkv": (sb_w_qkv, m_sb_w_qkv, v_sb_w_qkv), "sb_w_o": (sb_w_o, m_sb_w_o, v_sb_w_o),
           "kv_w": (kv_w, m_kv_w, v_kv_w), "swa_w_q": (swa_w_q, m_swa_w_q, v_swa_w_q),
           "swa_w_o": (swa_w_o, m_swa_w_o, v_swa_w_o)}
    for nm, (w, m, v) in big.items():
        g = grads[nm]
        two_d = lambda t: t.reshape(-1, t.shape[-1])
        d, new_m, new_v = _adamw(two_d(g), two_d(w), two_d(m), two_d(v), f"adamw_{nm}")
        result["grad"][nm] = g
        result["delta"][nm] = d.reshape(w.shape)
        result["new_m"][nm] = new_m.reshape(w.shape)
        result["new_v"][nm] = new_v.reshape(w.shape)

    order = ["ffn1_norm", "ffn1_w_in", "ffn1_w_out", "mix_norm", "ffn2_norm", "ffn2_w_in", "ffn2_w_out",
             "sb_w_qkv", "sb_w_o", "kv_norm", "kv_w", "swa_w_q", "swa_sinks", "swa_w_o", "final_norm"]
    outs = [result[kind][nm] for kind in ("grad", "delta", "new_m", "new_v") for nm in order]
    return (loss, dx.reshape(x.shape), *outs)
```

```python
import jax
import jax.numpy as jnp
from jax import lax
from jax.experimental import pallas as pl
from jax.experimental.pallas import tpu as pltpu

F32 = jnp.float32
BF16 = jnp.bfloat16

N_DEV = 8
HEAD_DIM = 64
LANES = 128
BLK = 128
RMS_EPS = 1e-6
FFN_RES_SCALE = 0.5
ROPE_THETA = 10000.0
ATTN_SCALE = HEAD_DIM ** -0.5
SB_LOG_FLOOR = -88.0
NEG_BIG = -1e30
VMEM_LIMIT_V7X = 56 * 1024 * 1024

ADAM_LR = 0.001
ADAM_B1 = 0.9
ADAM_B2 = 0.999
ADAM_EPS = 1e-08
ADAM_WD = 0.01
ADAM_STEP = 10

NN = ((1,), (0,))
NT = ((1,), (1,))
TN = ((0,), (0,))
TN_CHUNK = 2048
MESH = pl.DeviceIdType.MESH


def _dot(a, b, dims):
    return lax.dot_general(a, b, (dims, ((), ())), preferred_element_type=F32)


def _tile(n, pref, mult=LANES):
    if n <= pref:
        return n
    t = (pref // mult) * mult
    while t >= mult:
        if n % t == 0:
            return t
        t -= mult
    return n


def _params(*sem):
    return pltpu.CompilerParams(dimension_semantics=sem, vmem_limit_bytes=VMEM_LIMIT_V7X)


def _mm(a, b, dims, out_dtype, name, scale=1.0, res=None, tm=512, tn=512, tk=512):
    if dims == NN:
        (M, K), (_, N) = a.shape, b.shape
    elif dims == NT:
        (M, K), (N, _) = a.shape, b.shape
    else:
        (K, M), (_, N) = a.shape, b.shape
    tm, tn, tk = _tile(M, tm), _tile(N, tn), _tile(K, tk)
    nk = K // tk
    if dims == TN:
        a_spec = pl.BlockSpec((tk, tm), lambda i, j, k: (k, i))
    else:
        a_spec = pl.BlockSpec((tm, tk), lambda i, j, k: (i, k))
    if dims == NT:
        b_spec = pl.BlockSpec((tn, tk), lambda i, j, k: (j, k))
    else:
        b_spec = pl.BlockSpec((tk, tn), lambda i, j, k: (k, j))
    o_spec = pl.BlockSpec((tm, tn), lambda i, j, k: (i, j))
    has_res = res is not None

    def body(*refs):
        a_ref, b_ref = refs[0], refs[1]
        r_ref = refs[2] if has_res else None
        o_ref = refs[3] if has_res else refs[2]

        def finish(acc):
            r = acc * scale if scale != 1.0 else acc
            if has_res:
                r = r + r_ref[...]
            o_ref[...] = r.astype(out_dtype)

        p = _dot(a_ref[...].astype(BF16), b_ref[...].astype(BF16), dims)
        if nk == 1:
            finish(p)
        else:
            acc_ref = refs[-1]
            k = pl.program_id(2)

            @pl.when(k == 0)
            def _():
                acc_ref[...] = p

            @pl.when(k > 0)
            def _():
                acc_ref[...] += p

            @pl.when(k == nk - 1)
            def _():
                finish(acc_ref[...])

    in_specs = [a_spec, b_spec] + ([o_spec] if has_res else [])
    args = (a, b) + ((res,) if has_res else ())
    return pl.pallas_call(
        body, name=name,
        out_shape=jax.ShapeDtypeStruct((M, N), out_dtype),
        grid=(M // tm, N // tn, nk),
        in_specs=in_specs, out_specs=o_spec,
        scratch_shapes=[pltpu.VMEM((tm, tn), F32)] if nk > 1 else [],
        compiler_params=_params("parallel", "parallel", "arbitrary"),
    )(*args)


def _rows8(x):
    r, d = x.shape
    return jnp.sum(x.reshape(r // 8, 8, d), axis=0)


def _norm_proj(h, g, w, dims, name, rope=None, tail_t=0):
    S, D = h.shape
    N = w.shape[1] if dims == NN else w.shape[0]
    tm = _tile(S, 512, 16)

    def body(h_ref, g_ref, w_ref, *rest):
        xn_ref, y_ref = rest[-3:-1] if tail_t else rest[-2:]
        x = h_ref[...]
        r = lax.rsqrt(jnp.mean(x * x, axis=-1, keepdims=True) + RMS_EPS)
        xn = ((x * r) * g_ref[...]).astype(BF16)
        xn_ref[...] = xn
        y = _dot(xn, w_ref[...], dims)
        if tail_t:
            rest[-1][...] = jnp.transpose(y[:, N - tail_t:]).astype(BF16)
        if rope is None:
            y_ref[...] = y.astype(BF16)
        else:
            cs, sn = rest[0][...], rest[1][...]
            for gidx in range(N // LANES):
                sl = slice(gidx * LANES, (gidx + 1) * LANES)
                v = y[:, sl]
                if gidx < rope[2]:
                    v = v * cs + _swap_halves(v) * sn
                y_ref[:, sl] = v.astype(BF16)

    row = pl.BlockSpec((tm, D), lambda i: (i, 0))
    tab = pl.BlockSpec((tm, LANES), lambda i: (i, 0))
    in_specs = [row, pl.BlockSpec((1, D), lambda i: (0, 0)), pl.BlockSpec(w.shape, lambda i: (0, 0))]
    args = (h, g.reshape(1, D), w)
    if rope is not None:
        in_specs += [tab, tab]
        args += (rope[0], rope[1])
    out_shape = [jax.ShapeDtypeStruct((S, D), BF16), jax.ShapeDtypeStruct((S, N), BF16)]
    out_specs = [row, pl.BlockSpec((tm, N), lambda i: (i, 0))]
    if tail_t:
        out_shape.append(jax.ShapeDtypeStruct((tail_t, S), BF16))
        out_specs.append(pl.BlockSpec((tail_t, tm), lambda i: (0, i)))
    return pl.pallas_call(
        body, name=name, out_shape=out_shape, grid=(S // tm,),
        in_specs=in_specs, out_specs=out_specs,
        compiler_params=_params("parallel"),
    )(*args)


def _final_loss(h, g, tgt, name):
    S, D = h.shape
    ts = _tile(S, 512, 8)

    def body(h_ref, g_ref, t_ref, dh_ref, dg_ref, l_ref):
        x = h_ref[...]
        r = lax.rsqrt(jnp.mean(x * x, axis=-1, keepdims=True) + RMS_EPS)
        xhat = x * r
        err = xhat * g_ref[...] - t_ref[...]
        d = err * (1.0 / D)
        dxh = d * g_ref[...]
        c = jnp.mean(dxh * xhat, axis=-1, keepdims=True)
        dh_ref[...] = r * (dxh - xhat * c)
        part = _rows8(d * xhat)
        lpart = _rows8(err * err)

        @pl.when(pl.program_id(0) == 0)
        def _():
            dg_ref[...] = part
            l_ref[...] = lpart

        @pl.when(pl.program_id(0) > 0)
        def _():
            dg_ref[...] += part
            l_ref[...] += lpart

    row = pl.BlockSpec((ts, D), lambda i: (i, 0))
    acc = pl.BlockSpec((8, D), lambda i: (0, 0))
    return pl.pallas_call(
        body, name=name,
        out_shape=(jax.ShapeDtypeStruct((S, D), F32), jax.ShapeDtypeStruct((8, D), F32),
                   jax.ShapeDtypeStruct((8, D), F32)),
        grid=(S // ts,),
        in_specs=[row, pl.BlockSpec((1, D), lambda i: (0, 0)), row],
        out_specs=(row, acc, acc),
        compiler_params=_params("arbitrary"),
    )(h, g.reshape(1, D), tgt)


def _ffn_up(h, g, win_t, name, carry=None):
    S, D = h.shape
    F = win_t.shape[0] // 2
    tm, tn = _tile(S, 512, 16), _tile(F, 1408)
    nf = F // tn

    def body(h_ref, g_ref, wg_ref, wu_ref, xn_ref, silu_ref, dsilu_ref, up_ref, act_ref):
        x = h_ref[...]
        r = lax.rsqrt(jnp.mean(x * x, axis=-1, keepdims=True) + RMS_EPS)
        xn = ((x * r) * g_ref[...]).astype(BF16)
        xn_ref[...] = xn
        gate = _dot(xn, wg_ref[...], NT)
        up = _dot(xn, wu_ref[...], NT)
        sig = 1.0 / (1.0 + jnp.exp(-gate))
        silu = gate * sig
        up_ref[...] = up.astype(BF16)
        silu_ref[...] = silu.astype(BF16)
        dsilu_ref[...] = (sig + silu * (1.0 - sig)).astype(BF16)
        act_ref[...] = (silu * up).astype(BF16)

    row = pl.BlockSpec((tm, D), lambda i, j: (i, 0))
    blk = pl.BlockSpec((tm, tn), lambda i, j: (i, j))
    hid = jax.ShapeDtypeStruct((S, F), BF16)
    return _pcall(
        body, (h, g.reshape(1, D), win_t, win_t), name=name,
        out_shape=(jax.ShapeDtypeStruct((S, D), BF16), hid, hid, hid, hid),
        grid=(S // tm, nf),
        in_specs=[row, pl.BlockSpec((1, D), lambda i, j: (0, 0)),
                  pl.BlockSpec((tn, D), lambda i, j: (j, 0)),
                  pl.BlockSpec((tn, D), lambda i, j: (j + nf, 0))],
        out_specs=(row, blk, blk, blk, blk),
        sem=("arbitrary", "arbitrary"), carry=carry)


def _ffn_dact(dh, wo, silu, dsilu, up, name):
    S, D = dh.shape
    F = wo.shape[0]
    tm, tn = _tile(S, 512, 16), _tile(F, 1408)

    def body(dh_ref, wo_ref, s_ref, ds_ref, u_ref, dg_ref, du_ref):
        d = _dot(dh_ref[...].astype(BF16), wo_ref[...], NT) * FFN_RES_SCALE
        du_ref[...] = (d * s_ref[...].astype(F32)).astype(BF16)
        dg_ref[...] = (d * u_ref[...].astype(F32) * ds_ref[...].astype(F32)).astype(BF16)

    blk = pl.BlockSpec((tm, tn), lambda j, i: (i, j))
    hid = jax.ShapeDtypeStruct((S, F), BF16)
    return pl.pallas_call(
        body, name=name, out_shape=(hid, hid),
        grid=(F // tn, S // tm),
        in_specs=[pl.BlockSpec((tm, D), lambda j, i: (i, 0)), pl.BlockSpec((tn, D), lambda j, i: (j, 0)),
                  blk, blk, blk],
        out_specs=(blk, blk),
        compiler_params=_params("arbitrary", "arbitrary"),
    )(dh, wo, silu, dsilu, up)


def _dw_rows(srcs, x, name, carry=None, tk=TN_CHUNK):
    n = len(srcs)
    S, F = srcs[0].shape
    D = x.shape[1]
    tr, tk = _tile(F, 1408), _tile(S, tk, 16)
    nf, nk = F // tr, S // tk

    def body(*refs):
        src_refs, (x_ref, o_ref, acc_ref) = refs[:n], refs[n:]
        r, k = pl.program_id(0), pl.program_id(1)
        for s in range(n):
            @pl.when(r // nf == s)
            def _():
                p = _dot(src_refs[s][...].astype(BF16), x_ref[...], TN)

                @pl.when(k == 0)
                def _():
                    acc_ref[...] = p

                @pl.when(k > 0)
                def _():
                    acc_ref[...] += p

        @pl.when(k == nk - 1)
        def _():
            o_ref[...] = acc_ref[...].astype(BF16)

    def src_spec(s):
        return pl.BlockSpec((tk, tr), lambda r, k: (jnp.where(r // nf == s, k, 0), jnp.clip(r - s * nf, 0, nf - 1)))

    return _pcall(
        body, (*srcs, x), name=name, out_shape=jax.ShapeDtypeStruct((n * F, D), BF16),
        grid=(n * nf, nk),
        in_specs=[src_spec(s) for s in range(n)] + [pl.BlockSpec((tk, D), lambda r, k: (k, 0))],
        out_specs=pl.BlockSpec((tr, D), lambda r, k: (r, 0)),
        scratch_shapes=[pltpu.VMEM((tr, D), F32)],
        sem=("arbitrary", "arbitrary"), carry=carry)


def _dx_norm_bwd(terms, h, g, res, name, carry=None):
    S, D = h.shape
    tm = _tile(S, 256, 16)
    n = len(terms)

    def body(*refs):
        dy_refs, w_refs = refs[:n], refs[n:2 * n]
        h_ref, g_ref, r_ref, dh_ref, dg_ref = refs[2 * n:]
        d = _dot(dy_refs[0][...].astype(BF16), w_refs[0][...], terms[0][2])
        for t in range(1, n):
            d = d + _dot(dy_refs[t][...].astype(BF16), w_refs[t][...], terms[t][2])
        x = h_ref[...]
        r = lax.rsqrt(jnp.mean(x * x, axis=-1, keepdims=True) + RMS_EPS)
        xhat = x * r
        dxh = d * g_ref[...]
        c = jnp.mean(dxh * xhat, axis=-1, keepdims=True)
        dh_ref[...] = r * (dxh - xhat * c) + r_ref[...]
        part = _rows8(d * xhat)

        @pl.when(pl.program_id(0) == 0)
        def _():
            dg_ref[...] = part

        @pl.when(pl.program_id(0) > 0)
        def _():
            dg_ref[...] += part

    def w_spec(w, nblk, blk):
        return pl.BlockSpec((w.shape[0] // nblk, w.shape[1]), lambda i: (blk, 0))

    row = pl.BlockSpec((tm, D), lambda i: (i, 0))
    in_specs = [pl.BlockSpec((tm, t[0].shape[1]), lambda i: (i, 0)) for t in terms]
    in_specs += [w_spec(t[1], t[3], t[4]) for t in terms]
    in_specs += [row, pl.BlockSpec((1, D), lambda i: (0, 0)), row]
    return _pcall(
        body, (*[t[0] for t in terms], *[t[1] for t in terms], h, g.reshape(1, D), res), name=name,
        out_shape=(jax.ShapeDtypeStruct((S, D), F32), jax.ShapeDtypeStruct((8, D), F32)),
        grid=(S // tm,),
        in_specs=in_specs,
        out_specs=(row, pl.BlockSpec((8, D), lambda i: (0, 0))),
        sem=("arbitrary",), carry=carry)


def _load_resident(pairs, sems):
    @pl.when(pl.program_id(0) == 0)
    def _():
        copies = [pltpu.make_async_copy(src, dst, sems.at[n]) for n, (src, dst) in enumerate(pairs)]
        for cp in copies:
            cp.start()
        for cp in copies:
            cp.wait()


def _ffn_fwd_fused(h, g, win_t, wo, name, carry=None):
    S, D = h.shape
    F = wo.shape[0]
    tm = _tile(S, 256, 16)

    def body(h_ref, g_ref, win_hbm, wo_hbm, out_ref, xn_ref, silu_ref, dsilu_ref, up_ref, act_ref, win_v, wo_v, sems):
        _load_resident([(win_hbm, win_v), (wo_hbm, wo_v)], sems)
        x = h_ref[...]
        r = lax.rsqrt(jnp.mean(x * x, axis=-1, keepdims=True) + RMS_EPS)
        xn = ((x * r) * g_ref[...]).astype(BF16)
        xn_ref[...] = xn
        gate = _dot(xn, win_v[:F, :], NT)
        up = _dot(xn, win_v[F:, :], NT)
        sig = 1.0 / (1.0 + jnp.exp(-gate))
        silu = gate * sig
        act = (silu * up).astype(BF16)
        up_ref[...] = up.astype(BF16)
        silu_ref[...] = silu.astype(BF16)
        dsilu_ref[...] = (sig + silu * (1.0 - sig)).astype(BF16)
        act_ref[...] = act
        out_ref[...] = x + FFN_RES_SCALE * _dot(act, wo_v[...], NN)

    row = pl.BlockSpec((tm, D), lambda i: (i, 0))
    wide = pl.BlockSpec((tm, F), lambda i: (i, 0))
    hbm = pl.BlockSpec(memory_space=pl.ANY)
    hid = jax.ShapeDtypeStruct((S, F), BF16)
    res, got = _pcall(
        body, (h, g.reshape(1, D), win_t, wo), name=name,
        out_shape=(jax.ShapeDtypeStruct((S, D), F32), jax.ShapeDtypeStruct((S, D), BF16), hid, hid, hid, hid),
        grid=(S // tm,),
        in_specs=[row, pl.BlockSpec((1, D), lambda i: (0, 0)), hbm, hbm],
        out_specs=(row, row, wide, wide, wide, wide),
        scratch_shapes=[pltpu.VMEM(win_t.shape, BF16), pltpu.VMEM(wo.shape, BF16), pltpu.SemaphoreType.DMA((2,))],
        sem=("arbitrary",), carry=carry)
    return res[0], tuple(res[1:]), got


def _ffn_bwd_fused(dh, h, g, win_t, wo, silu, dsilu, up, name, carry=None):
    S, D = h.shape
    F = wo.shape[0]
    tm = _tile(S, 256, 16)

    def body(dh_ref, h_ref, g_ref, s_ref, ds_ref, u_ref, win_hbm, wo_hbm,
             dhin_ref, dgain_ref, dgate_ref, dup_ref, win_v, wo_v, sems):
        _load_resident([(win_hbm, win_v), (wo_hbm, wo_v)], sems)
        dhv = dh_ref[...]
        d = _dot(dhv.astype(BF16), wo_v[...], NT) * FFN_RES_SCALE
        dup = (d * s_ref[...].astype(F32)).astype(BF16)
        dgate = (d * u_ref[...].astype(F32) * ds_ref[...].astype(F32)).astype(BF16)
        dup_ref[...] = dup
        dgate_ref[...] = dgate
        dxn = _dot(dgate, win_v[:F, :], NN) + _dot(dup, win_v[F:, :], NN)
        x = h_ref[...]
        r = lax.rsqrt(jnp.mean(x * x, axis=-1, keepdims=True) + RMS_EPS)
        xhat = x * r
        dxh = dxn * g_ref[...]
        c = jnp.mean(dxh * xhat, axis=-1, keepdims=True)
        dhin_ref[...] = r * (dxh - xhat * c) + dhv
        part = _rows8(dxn * xhat)

        @pl.when(pl.program_id(0) == 0)
        def _():
            dgain_ref[...] = part

        @pl.when(pl.program_id(0) > 0)
        def _():
            dgain_ref[...] += part

    row = pl.BlockSpec((tm, D), lambda i: (i, 0))
    wide = pl.BlockSpec((tm, F), lambda i: (i, 0))
    hbm = pl.BlockSpec(memory_space=pl.ANY)
    hid = jax.ShapeDtypeStruct((S, F), BF16)
    return _pcall(
        body, (dh, h, g.reshape(1, D), silu, dsilu, up, win_t, wo), name=name,
        out_shape=(jax.ShapeDtypeStruct((S, D), F32), jax.ShapeDtypeStruct((8, D), F32), hid, hid),
        grid=(S // tm,),
        in_specs=[row, row, pl.BlockSpec((1, D), lambda i: (0, 0)), wide, wide, wide, hbm, hbm],
        out_specs=(row, pl.BlockSpec((8, D), lambda i: (0, 0)), wide, wide),
        scratch_shapes=[pltpu.VMEM(win_t.shape, BF16), pltpu.VMEM(wo.shape, BF16), pltpu.SemaphoreType.DMA((2,))],
        sem=("arbitrary",), carry=carry)


def _rope_tables(S):
    half = HEAD_DIM // 2
    inv_freq = ROPE_THETA ** (-jnp.arange(half, dtype=F32) / half)
    ang = jnp.arange(S).astype(F32)[:, None] * inv_freq[None, :]
    cos, sin = jnp.cos(ang), jnp.sin(ang)
    cos_t = jnp.tile(cos, (1, LANES // half))
    sin_t = jnp.tile(jnp.concatenate([-sin, sin], axis=1), (1, LANES // HEAD_DIM))
    return cos_t, sin_t


def _swap_halves(x):
    lane = lax.broadcasted_iota(jnp.int32, x.shape, 1)
    first = (lane % HEAD_DIM) < (HEAD_DIM // 2)
    return jnp.where(first, pltpu.roll(x, LANES - HEAD_DIM // 2, 1), pltpu.roll(x, HEAD_DIM // 2, 1))


def _rotary(x, cos_t, sin_t, n_rot, inverse, name):
    S, C = x.shape
    ts = _tile(S, 512, 16)
    ng = C // LANES

    def body(x_ref, c_ref, s_ref, o_ref):
        cs, sn = c_ref[...], s_ref[...]
        for gidx in range(ng):
            sl = slice(gidx * LANES, (gidx + 1) * LANES)
            v = x_ref[:, sl].astype(F32)
            if gidx < n_rot:
                if inverse:
                    v = v * cs + _swap_halves(v * sn)
                else:
                    v = v * cs + _swap_halves(v) * sn
            o_ref[:, sl] = v.astype(BF16)

    row = pl.BlockSpec((ts, C), lambda i: (i, 0))
    tab = pl.BlockSpec((ts, LANES), lambda i: (i, 0))
    return pl.pallas_call(
        body, name=name, out_shape=jax.ShapeDtypeStruct((S, C), BF16),
        grid=(S // ts,), in_specs=[row, tab, tab], out_specs=row,
        compiler_params=_params("parallel"),
    )(x, cos_t, sin_t)


def _head_masks():
    lane = lax.broadcasted_iota(jnp.int32, (BLK, LANES), 1)
    return lane < HEAD_DIM


def _split_bf16(x):
    hi = x.astype(BF16)
    lo = (x - hi.astype(F32)).astype(BF16)
    return hi, lo


def _sb_scores(qh, ks, carry, diag, tri_excl, strict):
    n_heads = len(qh)
    zs = [_dot(ks[n], qh[n], NT) for n in range(n_heads)]
    a_l, b_l, split_l = [], [], []
    for z in zs:
        a = jnp.minimum(z, 0.0) - jnp.log(1.0 + jnp.exp(-jnp.abs(z)))
        b = a - z
        if diag:
            b = jnp.where(strict, b, 0.0)
        a_l.append(a)
        b_l.append(b)
        split_l.append(_split_bf16(b))
    sufs = [_dot(tri_excl, hi, NN) + _dot(tri_excl, lo, NN) for hi, lo in split_l]
    w_l = []
    for n in range(n_heads):
        w = jnp.exp(a_l[n] + sufs[n] + carry[n])
        if diag:
            w = jnp.where(strict, w, 0.0)
        w_l.append(w)
    return a_l, b_l, w_l


SB_FWD_PAIRS = 4
SB_BWD_PAIRS = 2
SB_BWD_QBLOCKS = 2


def _any_alive(carries):
    top = carries[0]
    for c in carries[1:]:
        top = jnp.maximum(top, c)
    return (jnp.max(top) > SB_LOG_FLOOR).astype(jnp.int32)


def _sb_masks():
    row = lax.broadcasted_iota(jnp.int32, (BLK, BLK), 0)
    col = lax.broadcasted_iota(jnp.int32, (BLK, BLK), 1)
    tri_excl = jnp.where(col > row, 1.0, 0.0).astype(BF16)
    tri_incl = jnp.where(col >= row, 1.0, 0.0).astype(BF16)
    return row < HEAD_DIM, row < col, tri_excl, tri_incl


def _sb_fwd(qkv, kv_t, name, carry=None):
    S, D3 = qkv.shape
    D = D3 // 3
    npair, nb = D // LANES, S // BLK
    P = min(SB_FWD_PAIRS, npair)
    ngroup = npair // P
    W = P * LANES

    def body(q_ref, k_ref, vt_ref, o_ref):
        i = pl.program_id(1)
        m0 = _head_masks()
        top, strict, tri_excl, _ = _sb_masks()
        zq = jnp.zeros((BLK, LANES), BF16)
        lanes = [slice(p * LANES, (p + 1) * LANES) for p in range(P)]
        qh = []
        for sl in lanes:
            q2 = q_ref[:, sl] * ATTN_SCALE
            qh += [jnp.where(m0, q2, zq), jnp.where(m0, zq, q2)]

        def block(j, carry, acc, diag):
            off = pl.multiple_of(j * BLK, BLK)
            ks, vth = [], []
            for sl in lanes:
                k2 = k_ref[pl.ds(off, BLK), sl]
                vt = vt_ref[sl, pl.ds(off, BLK)]
                ks += [k2, k2]
                vth += [jnp.where(top, vt, zq), jnp.where(top, zq, vt)]
            _, b_l, w_l = _sb_scores(qh, ks, carry, diag, tri_excl, strict)
            wb = [w.astype(BF16) for w in w_l]
            new_acc = [acc[p] + _dot(vth[2 * p], wb[2 * p], NN) + _dot(vth[2 * p + 1], wb[2 * p + 1], NN)
                       for p in range(P)]
            new_carry = [carry[n] + jnp.sum(b_l[n], axis=0, keepdims=True) for n in range(2 * P)]
            return new_carry, new_acc

        c0 = jnp.zeros((1, BLK), F32)
        carry, acc = block(i, [c0] * (2 * P), [jnp.zeros((LANES, BLK), F32)] * P, True)

        def cond(st):
            return jnp.logical_and(st[0] >= 0, st[1] > 0)

        def step(st):
            j, _, carry, acc = st
            carry, acc = block(j, carry, acc, False)
            return j - 1, _any_alive(carry), carry, acc

        st = lax.while_loop(cond, step, (i - 1, _any_alive(carry), carry, acc))
        for p, sl in enumerate(lanes):
            o_ref[:, sl] = jnp.transpose(st[3][p])

    return _pcall(
        body, (qkv, qkv, kv_t), name=name, out_shape=jax.ShapeDtypeStruct((S, D), F32),
        grid=(ngroup, nb),
        in_specs=[pl.BlockSpec((BLK, W), lambda g, i: (i, g)),
                  pl.BlockSpec((S, W), lambda g, i: (0, ngroup + g)),
                  pl.BlockSpec((W, S), lambda g, i: (ngroup + g, 0))],
        out_specs=pl.BlockSpec((BLK, W), lambda g, i: (i, g)),
        sem=("arbitrary", "arbitrary"), carry=carry)


def _sb_bwd(qkv, kv_t, o, do, name, carry=None):
    S, D3 = qkv.shape
    D = D3 // 3
    npair, nb = D // LANES, S // BLK
    P = min(SB_BWD_PAIRS, npair)
    ngroup = npair // P
    W = P * LANES

    QB = SB_BWD_QBLOCKS if nb % SB_BWD_QBLOCKS == 0 else 1
    nch = QB * 2 * P

    def body(q_ref, o_ref, do_ref, qkv_hbm, kt_hbm, dq_ref, dk_ref, dv_ref, k_ref, v_ref, kt_ref, sems):
        grp = pl.program_id(0)
        i_first = pl.program_id(1) * QB
        m0 = _head_masks()
        top, strict, tri_excl, tri_incl = _sb_masks()
        zq = jnp.zeros((BLK, LANES), BF16)
        lanes = [slice(p * LANES, (p + 1) * LANES) for p in range(P)]

        @pl.when(pl.program_id(1) == 0)
        def _():
            copies = [pltpu.make_async_copy(qkv_hbm.at[:, pl.ds(pl.multiple_of((c * ngroup + grp) * W, LANES), W)],
                                            ref, sems.at[c - 1]) for c, ref in ((1, k_ref), (2, v_ref))]
            copies.append(pltpu.make_async_copy(kt_hbm.at[pl.ds(pl.multiple_of(grp * W, LANES), W), :],
                                                kt_ref, sems.at[2]))
            for cp in copies:
                cp.start()
            dk_ref[...] = jnp.zeros_like(dk_ref)
            dv_ref[...] = jnp.zeros_like(dv_ref)
            for cp in copies:
                cp.wait()

        qh, doh, delta = [], [], []
        for qb in range(QB):
            rs = slice(qb * BLK, (qb + 1) * BLK)
            for sl in lanes:
                q2, do2 = q_ref[rs, sl] * ATTN_SCALE, do_ref[rs, sl]
                qh += [jnp.where(m0, q2, zq), jnp.where(m0, zq, q2)]
                doh += [jnp.where(m0, do2, zq), jnp.where(m0, zq, do2)]
                prod_t = jnp.transpose(do2.astype(F32) * o_ref[rs, sl])
                delta += [jnp.sum(jnp.where(top, prod_t, 0.0), axis=0, keepdims=True),
                          jnp.sum(jnp.where(top, 0.0, prod_t), axis=0, keepdims=True)]

        def block(js, valid, cb, cg, dq, diag):
            offs = [pl.multiple_of(j * BLK, BLK) for j in js]
            ks, vs, kth = [], [], []
            for qb in range(QB):
                for sl in lanes:
                    k2, v2 = k_ref[pl.ds(offs[qb], BLK), sl], v_ref[pl.ds(offs[qb], BLK), sl]
                    ks += [k2, k2]
                    vs += [v2, v2]
                    kt = kt_ref[sl, pl.ds(offs[qb], BLK)] * ATTN_SCALE
                    kth += [jnp.where(top, kt, zq), jnp.where(top, zq, kt)]
            dws = [_dot(vs[n], doh[n], NT) for n in range(nch)]
            a_l, b_l, w_l = _sb_scores(qh, ks, cb, diag, tri_excl, strict)
            wb = [w.astype(BF16) for w in w_l]
            g_l = [dws[n] * wb[n].astype(F32) for n in range(nch)]
            gsplit = [_split_bf16(g) for g in g_l]
            gincs = [_dot(tri_incl, hi, NN) + _dot(tri_incl, lo, NN) for hi, lo in gsplit]
            dzs = []
            for n in range(nch):
                beta = jnp.exp(a_l[n])
                dz = g_l[n] - beta * (g_l[n] + ((delta[n] - cg[n]) - gincs[n]))
                if diag:
                    dz = jnp.where(strict, dz, 0.0)
                if valid[n // (2 * P)] is not None:
                    dz = jnp.where(valid[n // (2 * P)], dz, 0.0)
                dzs.append(dz.astype(BF16))
            ndq = []
            for qb in range(QB):
                for p, sl in enumerate(lanes):
                    n0 = qb * 2 * P + 2 * p
                    ndq.append(dq[qb * P + p] + _dot(kth[n0], dzs[n0], NN) + _dot(kth[n0 + 1], dzs[n0 + 1], NN))
                    dk_ref[pl.ds(offs[qb], BLK), sl] += _dot(dzs[n0], qh[n0], NN) + _dot(dzs[n0 + 1], qh[n0 + 1], NN)
                    dv_ref[pl.ds(offs[qb], BLK), sl] += _dot(wb[n0], doh[n0], NN) + _dot(wb[n0 + 1], doh[n0 + 1], NN)
            ncb = [cb[n] + jnp.sum(b_l[n], axis=0, keepdims=True) for n in range(nch)]
            ncg = [cg[n] + jnp.sum(g_l[n], axis=0, keepdims=True) for n in range(nch)]
            return ncb, ncg, ndq

        c0 = jnp.zeros((1, BLK), F32)
        cb, cg, dq = block([i_first + qb for qb in range(QB)], [None] * QB, [c0] * nch, [c0] * nch,
                           [jnp.zeros((LANES, BLK), F32)] * (QB * P), True)

        def cond(st):
            return jnp.logical_and(i_first + QB - 1 - st[0] >= 0, st[1] > 0)

        def step(st):
            t, _, cb, cg, dq = st
            js = [i_first + qb - t for qb in range(QB)]
            valid = [js[qb] >= 0 for qb in range(QB - 1)] + [None]
            cb = [cb[n] if valid[n // (2 * P)] is None else jnp.where(valid[n // (2 * P)], cb[n], NEG_BIG)
                  for n in range(nch)]
            cb, cg, dq = block([jnp.maximum(j, 0) for j in js], valid, cb, cg, dq, False)
            return t + 1, _any_alive(cb), cb, cg, dq

        st = lax.while_loop(cond, step, (1, _any_alive(cb), cb, cg, dq))
        for qb in range(QB):
            for p, sl in enumerate(lanes):
                dq_ref[qb * BLK:(qb + 1) * BLK, sl] = jnp.transpose(st[4][qb * P + p]).astype(BF16)

    blk = pl.BlockSpec((QB * BLK, W), lambda g, i: (i, g))
    col_all = pl.BlockSpec((S, W), lambda g, i: (0, g))
    hbm = pl.BlockSpec(memory_space=pl.ANY)
    return _pcall(
        body, (qkv, o, do, qkv, kv_t), name=name,
        out_shape=(jax.ShapeDtypeStruct((S, D), BF16), jax.ShapeDtypeStruct((S, D), F32),
                   jax.ShapeDtypeStruct((S, D), F32)),
        grid=(ngroup, nb // QB),
        in_specs=[blk, blk, blk, hbm, hbm],
        out_specs=(blk, col_all, col_all),
        scratch_shapes=[pltpu.VMEM((S, W), BF16), pltpu.VMEM((S, W), BF16), pltpu.VMEM((W, S), BF16),
                        pltpu.SemaphoreType.DMA((3,))],
        sem=("arbitrary", "arbitrary"), carry=carry)


SWA_Q_GROUPS = 4


def _roll_heads(x):
    return pltpu.roll(x.astype(F32), HEAD_DIM, 1).astype(BF16)


def _swa_valid(i):
    r = lax.broadcasted_iota(jnp.int32, (BLK, 2 * BLK), 0)
    c = lax.broadcasted_iota(jnp.int32, (BLK, 2 * BLK), 1)
    diff = r + BLK - c
    return (diff >= 0) & (diff < BLK) & ((i > 0) | (c >= BLK))


def _swa_probs(z, valid, sink):
    z = jnp.where(valid, z * ATTN_SCALE, NEG_BIG)
    mx = jnp.maximum(jnp.max(z, axis=1, keepdims=True), sink)
    p = jnp.exp(z - mx)
    ps = jnp.exp(sink - mx)
    inv = 1.0 / (jnp.sum(p, axis=1, keepdims=True) + ps)
    return p * inv, ps * inv


def _swa_operands(q_ref, kc_ref, kp_ref, vc_ref, vp_ref, s_ref, m):
    m0 = _head_masks()
    m0k = jnp.concatenate([m0, m0], axis=0)
    kk = jnp.concatenate([kp_ref[...], kc_ref[...]], axis=0)
    vv = jnp.concatenate([vp_ref[...], vc_ref[...]], axis=0)
    ksw, vsw = _roll_heads(kk), _roll_heads(vv)
    zk = jnp.zeros_like(kk)
    heads = []
    for c in range(SWA_Q_GROUPS):
        qc = q_ref[:, c * LANES:(c + 1) * LANES]
        zq = jnp.zeros_like(qc)
        for u in range(2):
            same = u == c // 2
            sel = (lambda x, z, mk: jnp.where(mk, x, z)) if u == 0 else (lambda x, z, mk: jnp.where(mk, z, x))
            heads.append(dict(
                c=c, same=same, sel=sel,
                qm=sel(qc, zq, m0),
                k=kk if same else ksw, v=vv if same else vsw,
                km=sel(kk if same else ksw, zk, m0k), vm=sel(vv if same else vsw, zk, m0k),
                sink=s_ref[0, m * 2 * SWA_Q_GROUPS + 2 * c + u]))
    return heads, m0


def _swa_fwd(q, kv, sinks, name):
    S, D = q.shape
    nkvp = kv.shape[1] // (2 * LANES)
    nb = S // BLK
    qw = SWA_Q_GROUPS * LANES

    def body(q_ref, kc_ref, kp_ref, vc_ref, vp_ref, s_ref, o_ref):
        m, i = pl.program_id(0), pl.program_id(1)
        valid = _swa_valid(i)
        heads, _ = _swa_operands(q_ref, kc_ref, kp_ref, vc_ref, vp_ref, s_ref, m)
        zs = [_dot(hd["qm"], hd["k"], NT) for hd in heads]
        ps = [_swa_probs(z, valid, hd["sink"])[0].astype(BF16) for z, hd in zip(zs, heads)]
        for c in range(SWA_Q_GROUPS):
            o_ref[:, c * LANES:(c + 1) * LANES] = (_dot(ps[2 * c], heads[2 * c]["vm"], NN)
                                                   + _dot(ps[2 * c + 1], heads[2 * c + 1]["vm"], NN))

    prev = lambda i: jnp.maximum(i - 1, 0)
    return pl.pallas_call(
        body, name=name, out_shape=jax.ShapeDtypeStruct((S, D), F32),
        grid=(nkvp, nb),
        in_specs=[pl.BlockSpec((BLK, qw), lambda m, i: (i, m)),
                  pl.BlockSpec((BLK, LANES), lambda m, i: (i, m)),
                  pl.BlockSpec((BLK, LANES), lambda m, i: (prev(i), m)),
                  pl.BlockSpec((BLK, LANES), lambda m, i: (i, nkvp + m)),
                  pl.BlockSpec((BLK, LANES), lambda m, i: (prev(i), nkvp + m)),
                  pl.BlockSpec(memory_space=pltpu.SMEM)],
        out_specs=pl.BlockSpec((BLK, qw), lambda m, i: (i, m)),
        compiler_params=_params("arbitrary", "arbitrary"),
    )(q, kv, kv, kv, kv, sinks)


def _swa_bwd(q, kv, sinks, o, do, cos_t, sin_t, name, carry=None):
    S, D = q.shape
    nkvp = kv.shape[1] // (2 * LANES)
    nb = S // BLK
    qw = SWA_Q_GROUPS * LANES
    nh = 2 * SWA_Q_GROUPS

    def body(q_ref, kc_ref, kp_ref, vc_ref, vp_ref, s_ref, o_ref, do_ref, c_ref, sn_ref,
             dq_ref, dk_ref, dv_ref, ds_ref):
        m, i = pl.program_id(0), pl.program_id(1)
        valid = _swa_valid(i)
        heads, m0 = _swa_operands(q_ref, kc_ref, kp_ref, vc_ref, vp_ref, s_ref, m)

        @pl.when(i == 0)
        def _():
            dk_ref[...] = jnp.zeros_like(dk_ref)
            dv_ref[...] = jnp.zeros_like(dv_ref)
            ds_ref[...] = jnp.zeros_like(ds_ref)

        doms, deltas = [], []
        for hd in heads:
            c = hd["c"]
            doc = do_ref[:, c * LANES:(c + 1) * LANES]
            prod = doc.astype(F32) * o_ref[:, c * LANES:(c + 1) * LANES]
            doms.append(hd["sel"](doc, jnp.zeros_like(doc), m0))
            deltas.append(jnp.sum(hd["sel"](prod, 0.0, m0), axis=1, keepdims=True))
        zs = [_dot(hd["qm"], hd["k"], NT) for hd in heads]
        dps = [_dot(dom, hd["v"], NT) for dom, hd in zip(doms, heads)]
        pbs, dscs = [], []
        for n, hd in enumerate(heads):
            p, psink = _swa_probs(zs[n], valid, hd["sink"])
            pbs.append(p.astype(BF16))
            dscs.append((p * (dps[n] - deltas[n]) * ATTN_SCALE).astype(BF16))
            dsink = jnp.sum(jnp.broadcast_to(-(psink * deltas[n]), (BLK, LANES)), axis=0, keepdims=True)
            ds_ref[0, n:n + 1, :] += dsink
        for c in range(SWA_Q_GROUPS):
            dq_rot = _dot(dscs[2 * c], heads[2 * c]["km"], NN) + _dot(dscs[2 * c + 1], heads[2 * c + 1]["km"], NN)
            dq_ref[:, c * LANES:(c + 1) * LANES] = (
                dq_rot * c_ref[...] + _swap_halves(dq_rot * sn_ref[...])).astype(BF16)
        acc = {}
        for n, hd in enumerate(heads):
            dk_n = _dot(dscs[n], hd["qm"], TN)
            dv_n = _dot(pbs[n], doms[n], TN)
            for key, val in ((("k", hd["same"]), dk_n), (("v", hd["same"]), dv_n)):
                acc[key] = val if key not in acc else acc[key] + val
        dkk = acc["k", True] + pltpu.roll(acc["k", False], HEAD_DIM, 1)
        dvv = acc["v", True] + pltpu.roll(acc["v", False], HEAD_DIM, 1)
        poff = pl.multiple_of(jnp.maximum(i - 1, 0) * BLK, BLK)
        coff = pl.multiple_of(i * BLK, BLK)
        dk_ref[pl.ds(poff, BLK), :] += dkk[:BLK]
        dv_ref[pl.ds(poff, BLK), :] += dvv[:BLK]
        dk_ref[pl.ds(coff, BLK), :] += dkk[BLK:]
        dv_ref[pl.ds(coff, BLK), :] += dvv[BLK:]

    prev = lambda i: jnp.maximum(i - 1, 0)
    qblk = pl.BlockSpec((BLK, qw), lambda m, i: (i, m))
    col_all = pl.BlockSpec((S, LANES), lambda m, i: (0, m))
    tab = pl.BlockSpec((BLK, LANES), lambda m, i: (i, 0))
    return _pcall(
        body, (q, kv, kv, kv, kv, sinks, o, do, cos_t, sin_t), name=name,
        out_shape=(jax.ShapeDtypeStruct((S, D), BF16),
                   jax.ShapeDtypeStruct((S, nkvp * LANES), F32),
                   jax.ShapeDtypeStruct((S, nkvp * LANES), F32),
                   jax.ShapeDtypeStruct((nkvp, nh, LANES), F32)),
        grid=(nkvp, nb),
        in_specs=[qblk,
                  pl.BlockSpec((BLK, LANES), lambda m, i: (i, m)),
                  pl.BlockSpec((BLK, LANES), lambda m, i: (prev(i), m)),
                  pl.BlockSpec((BLK, LANES), lambda m, i: (i, nkvp + m)),
                  pl.BlockSpec((BLK, LANES), lambda m, i: (prev(i), nkvp + m)),
                  pl.BlockSpec(memory_space=pltpu.SMEM),
                  qblk, qblk, tab, tab],
        out_specs=(qblk, col_all, col_all, pl.BlockSpec((1, nh, LANES), lambda m, i: (m, 0, 0))),
        sem=("arbitrary", "arbitrary"), carry=carry)


def _dev_index(p):
    return 4 * p[0] + 2 * p[1] + p[2]


def _gather_plan(x_refs, out_refs, send_sems, recv_sems, local_sems):
    n = len(x_refs)
    x_, y_, c_ = lax.axis_index("x"), lax.axis_index("y"), lax.axis_index("c")
    me, sibling = (x_, y_, c_), (x_, y_, 1 - c_)
    chips = [(1 - x_, y_), (x_, 1 - y_), (1 - x_, 1 - y_)]

    def copy(t, k, block, to, src=None):
        dst = out_refs[t].at[_dev_index(block)]
        return pltpu.make_async_remote_copy(
            src_ref=dst if src is None else src, dst_ref=dst,
            send_sem=send_sems.at[7 * t + k], recv_sem=recv_sems.at[7 * t + k],
            device_id=to, device_id_type=MESH)

    mine = [pltpu.make_async_copy(x_refs[t], out_refs[t].at[_dev_index(me)], local_sems.at[t]) for t in range(n)]
    first = []
    for t in range(n):
        first.append(copy(t, 0, me, sibling, src=x_refs[t]))
        first += [copy(t, 1 + j, me, (*chip, c_), src=x_refs[t]) for j, chip in enumerate(chips)]
    arrived = lambda t, j: copy(t, 1 + j, (*chips[j], c_), me)
    forward = lambda t, j: copy(t, 4 + j, (*chips[j], c_), sibling)
    from_sibling = lambda t: copy(t, 0, sibling, me)
    forwarded = lambda t, j: copy(t, 4 + j, (*chips[j], 1 - c_), me)
    return n, mine, first, arrived, forward, from_sibling, forwarded


def _gather_start(x_refs, out_refs, send_sems, recv_sems, local_sems):
    _, mine, first, *_ = _gather_plan(x_refs, out_refs, send_sems, recv_sems, local_sems)
    for cp in mine + first:
        cp.start()


def _gather_forward(x_refs, out_refs, send_sems, recv_sems, local_sems):
    n, _, _, arrived, forward, _, _ = _gather_plan(x_refs, out_refs, send_sems, recv_sems, local_sems)
    for j in range(3):
        for t in range(n):
            arrived(t, j).wait_recv()
            forward(t, j).start()


def _gather_finish(x_refs, out_refs, send_sems, recv_sems, local_sems):
    n, mine, first, _, forward, from_sibling, forwarded = _gather_plan(
        x_refs, out_refs, send_sems, recv_sems, local_sems)
    for t in range(n):
        from_sibling(t).wait_recv()
    for j in range(3):
        for t in range(n):
            forwarded(t, j).wait_recv()
    for cp in first + [forward(t, j) for j in range(3) for t in range(n)]:
        cp.wait_send()
    for cp in mine:
        cp.wait()


def _scatter_plan(b_refs, out_refs, send_sems, recv_sems, local_sems):
    n = len(b_refs)
    x_, y_, c_ = lax.axis_index("x"), lax.axis_index("y"), lax.axis_index("c")
    my_idx = _dev_index((x_, y_, c_))
    mine = [pltpu.make_async_copy(b_refs[t].at[my_idx], out_refs[t].at[my_idx], local_sems.at[t]) for t in range(n)]
    copies = []
    for t in range(n):
        for k in range(1, N_DEV):
            peer = (x_ ^ ((k >> 2) & 1), y_ ^ ((k >> 1) & 1), c_ ^ (k & 1))
            copies.append(pltpu.make_async_remote_copy(
                src_ref=b_refs[t].at[_dev_index(peer)], dst_ref=out_refs[t].at[my_idx],
                send_sem=send_sems.at[7 * t + k - 1], recv_sem=recv_sems.at[7 * t + k - 1],
                device_id=peer, device_id_type=MESH))
    return mine, copies


def _scatter_start(b_refs, out_refs, send_sems, recv_sems, local_sems):
    mine, copies = _scatter_plan(b_refs, out_refs, send_sems, recv_sems, local_sems)
    for cp in mine + copies:
        cp.start()


def _scatter_finish(b_refs, out_refs, send_sems, recv_sems, local_sems):
    mine, copies = _scatter_plan(b_refs, out_refs, send_sems, recv_sems, local_sems)
    for cp in copies:
        cp.wait_recv()
    for cp in copies:
        cp.wait_send()
    for cp in mine:
        cp.wait()


def _exchange_operands(kind, tensors):
    if kind == "gather":
        args = list(tensors)
        shapes = [jax.ShapeDtypeStruct((N_DEV,) + t.shape, t.dtype) for t in tensors]
        return args, shapes, (_gather_start, _gather_forward, _gather_finish)
    args = [t.reshape(N_DEV, t.shape[0] // N_DEV, t.shape[1]) for t in tensors]
    shapes = [jax.ShapeDtypeStruct(a.shape, a.dtype) for a in args]
    return args, shapes, (_scatter_start, None, _scatter_finish)


def _exchange_results(kind, tensors, res):
    if kind == "gather":
        return [r.reshape(N_DEV * t.shape[0], t.shape[1]) for r, t in zip(res, tensors)]
    return list(res)


def _exchange_sems(n):
    return [pltpu.SemaphoreType.DMA((7 * n,)), pltpu.SemaphoreType.DMA((7 * n,)), pltpu.SemaphoreType.DMA((n,))]


def _exchange(kind, tensors, name):
    n = len(tensors)
    args, shapes, phases = _exchange_operands(kind, tensors)

    def body(*refs):
        for phase in phases:
            if phase is not None:
                phase(refs[:n], refs[n:2 * n], *refs[2 * n:])

    hbm = pl.BlockSpec(memory_space=pl.ANY)
    res = pl.pallas_call(body, name=name, out_shape=shapes, in_specs=[hbm] * n, out_specs=[hbm] * n,
                         scratch_shapes=_exchange_sems(n))(*args)
    return _exchange_results(kind, tensors, res)


def _pcall(body, args, *, name, out_shape, grid, in_specs, out_specs, sem, scratch_shapes=(), carry=None):
    if carry is None:
        out = pl.pallas_call(body, name=name, out_shape=out_shape, grid=grid, in_specs=list(in_specs),
                             out_specs=out_specs, scratch_shapes=list(scratch_shapes),
                             compiler_params=_params(*sem))(*args)
        return out, None
    kind, tensors = carry
    multi = isinstance(out_shape, (tuple, list))
    shapes = list(out_shape) if multi else [out_shape]
    ospecs = list(out_specs) if multi else [out_specs]
    n_in, n_out, n_scr, n_c = len(in_specs), len(shapes), len(scratch_shapes), len(tensors)
    c_args, c_shapes, (start, forward, finish) = _exchange_operands(kind, tensors)
    n_steps = 1
    for g in grid:
        n_steps *= g
    late = (3 * n_steps) // 4

    def wrapped(*refs):
        ins, rest = refs[:n_in], refs[n_in:]
        c_in, rest = rest[:n_c], rest[n_c:]
        outs, rest = rest[:n_out], rest[n_out:]
        c_out, rest = rest[:n_c], rest[n_c:]
        scr, sems = rest[:n_scr], rest[n_scr:]
        step = pl.program_id(0)
        for a in range(1, len(grid)):
            step = step * grid[a] + pl.program_id(a)

        @pl.when(step == 0)
        def _():
            start(c_in, c_out, *sems)

        body(*ins, *outs, *scr)

        if forward is not None:
            @pl.when(step == late)
            def _():
                forward(c_in, c_out, *sems)

        @pl.when(step == n_steps - 1)
        def _():
            finish(c_in, c_out, *sems)

    hbm = pl.BlockSpec(memory_space=pl.ANY)
    res = pl.pallas_call(
        wrapped, name=name, out_shape=shapes + c_shapes, grid=grid,
        in_specs=list(in_specs) + [hbm] * n_c, out_specs=ospecs + [hbm] * n_c,
        scratch_shapes=list(scratch_shapes) + _exchange_sems(n_c),
        compiler_params=_params(*sem))(*args, *c_args)
    outs = tuple(res[:n_out]) if multi else res[0]
    return outs, _exchange_results(kind, tensors, res[n_out:])


def _sum8(parts, name):
    _, R, C = parts.shape
    tr = _tile(R, 256, 16)

    def body(p_ref, g_ref):
        g = p_ref[0].astype(F32)
        for s in range(1, N_DEV):
            g = g + p_ref[s].astype(F32)
        g_ref[...] = g

    return pl.pallas_call(
        body, name=name, out_shape=jax.ShapeDtypeStruct((R, C), F32),
        grid=(R // tr,),
        in_specs=[pl.BlockSpec((N_DEV, tr, C), lambda i: (0, i, 0))],
        out_specs=pl.BlockSpec((tr, C), lambda i: (i, 0)),
        compiler_params=_params("parallel"),
    )(parts)


def _adamw(g, w, m, v, name):
    R, C = g.shape
    tr = _tile(R, 256, 8)
    c1 = 1.0 - ADAM_B1 ** ADAM_STEP
    c2 = 1.0 - ADAM_B2 ** ADAM_STEP

    def body(g_ref, w_ref, m_ref, v_ref, d_ref, nm_ref, nv_ref):
        gg = g_ref[...]
        nm = ADAM_B1 * m_ref[...] + (1.0 - ADAM_B1) * gg
        nv = ADAM_B2 * v_ref[...] + (1.0 - ADAM_B2) * (gg * gg)
        m_hat = nm / c1
        v_hat = nv / c2
        nm_ref[...] = nm
        nv_ref[...] = nv
        d_ref[...] = -ADAM_LR * (m_hat / (jnp.sqrt(v_hat) + ADAM_EPS) + ADAM_WD * w_ref[...])

    row = pl.BlockSpec((tr, C), lambda i: (i, 0))
    shp = jax.ShapeDtypeStruct((R, C), F32)
    return pl.pallas_call(
        body, name=name, out_shape=(shp, shp, shp),
        grid=(R // tr,), in_specs=[row, row, row, row], out_specs=(row, row, row),
        compiler_params=_params("parallel"),
    )(g, w, m, v)


def _ffn_down(act, wo, h, tag):
    return _mm(act, wo, NN, F32, f"{tag}_down", scale=FFN_RES_SCALE, res=h, tm=512, tn=1024, tk=2816)


def _ffn_fwd(h, g, win_t, wo, tag, carry=None):
    return _ffn_fwd_fused(h, g, win_t, wo, f"{tag}_fwd", carry=carry)


def _ffn_bwd(dh, h, g, win_t, wo, saved, tag, scatter=False, carry=None, carry_dwin=None):
    xn, silu, dsilu, up, act = saved
    dwo = _mm(act, dh, TN, BF16, f"{tag}_dwo", scale=FFN_RES_SCALE, tm=1408, tn=1024, tk=TN_CHUNK)
    if not scatter:
        (dh_in, dg, dgate, dup), got = _ffn_bwd_fused(dh, h, g, win_t, wo, silu, dsilu, up, f"{tag}_bwd", carry=carry)
        dwin_t, got_dwin = _dw_rows([dgate, dup], xn, f"{tag}_dwin", carry=carry_dwin)
        return dh_in, dg, dwin_t, dwo, got, got_dwin
    dgate, dup = _ffn_dact(dh, wo, silu, dsilu, up, f"{tag}_dact")
    dwin_t, got_wo = _dw_rows([dgate, dup], xn, f"{tag}_dwin", carry=("scatter", [dwo]))
    (dh_in, dg), got_win = _dx_norm_bwd([(dgate, win_t, NN, 2, 0), (dup, win_t, NN, 2, 1)], h, g, dh, f"{tag}_dx",
                                        carry=("scatter", [dwin_t]))
    return dh_in, dg, got_win[0], got_wo[0]


def _proj(a, w, dims, out_dtype, name, res=None):
    return _mm(a, w, dims, out_dtype, name, res=res, tm=1024, tn=1024, tk=1024)


def _proj_dw(x, dy, name):
    return _mm(x, dy, TN, BF16, name, tm=1024, tn=1024, tk=TN_CHUNK)


def kernel(x, ffn1_norm, ffn1_w_in, ffn1_w_out, mix_norm, ffn2_norm, ffn2_w_in, ffn2_w_out, sb_w_qkv, sb_w_o, kv_norm, kv_w, swa_w_q, swa_sinks, swa_w_o, final_norm, loss_target, m_ffn1_norm, m_ffn1_w_in, m_ffn1_w_out, m_mix_norm, m_ffn2_norm, m_ffn2_w_in, m_ffn2_w_out, m_sb_w_qkv, m_sb_w_o, m_kv_norm, m_kv_w, m_swa_w_q, m_swa_sinks, m_swa_w_o, m_final_norm, v_ffn1_norm, v_ffn1_w_in, v_ffn1_w_out, v_mix_norm, v_ffn2_norm, v_ffn2_w_in, v_ffn2_w_out, v_sb_w_qkv, v_sb_w_o, v_kv_norm, v_kv_w, v_swa_w_q, v_swa_sinks, v_swa_w_o, v_final_norm):
    S, D = x.shape[1], x.shape[2]
    L = ffn1_w_in.shape[0]
    KV = kv_w.shape[1]
    assert L == 2 and swa_sinks.shape == (1, 2 * SWA_Q_GROUPS * KV // (2 * LANES))

    def bf(w):
        return w.astype(BF16)

    def bft(w):
        return jnp.transpose(w).astype(BF16)

    cos_t, sin_t = _rope_tables(S)
    h0 = x.reshape(S, D)
    tgt = loss_target.reshape(S, D)

    win1a_t, = _exchange("gather", [bft(ffn1_w_in[0])], "gather_first_weight")
    sv_a1, (wo1a, wqkv_t, w_sbo) = _ffn_up(
        h0, ffn1_norm[0], win1a_t, "ffn1a_up",
        carry=("gather", [bf(ffn1_w_out[0]), bft(sb_w_qkv[0]), bf(sb_w_o[0])]))
    h1 = _ffn_down(sv_a1[-1], wo1a, h0, "ffn1a")
    hn_a, qkv, kv_t = _norm_proj(h1, mix_norm[0], wqkv_t, NT, "sb_qkv", tail_t=2 * D)
    o_sb, (win2a_t, wo2a, w_kv) = _sb_fwd(qkv, kv_t, "sb_attn", carry=("gather", [
        bft(ffn2_w_in[0]), bf(ffn2_w_out[0]), bf(kv_w)]))
    h2 = _proj(o_sb, w_sbo, NN, F32, "sb_out", res=h1)
    h3, sv_a2, (win1b_t, wo1b, w_q, w_swo) = _ffn_fwd(h2, ffn2_norm[0], win2a_t, wo2a, "ffn2a", carry=("gather", [
        bft(ffn1_w_in[1]), bf(ffn1_w_out[1]), bf(swa_w_q[0]), bf(swa_w_o[0])]))
    kvn, kv_rot = _norm_proj(h3, kv_norm, w_kv, NN, "kv_proj", rope=(cos_t, sin_t, KV // (2 * LANES)))
    h4, sv_b1, (win2b_t, wo2b) = _ffn_fwd(h3, ffn1_norm[1], win1b_t, wo1b, "ffn1b", carry=("gather", [
        bft(ffn2_w_in[1]), bf(ffn2_w_out[1])]))
    hn_b, q_rot = _norm_proj(h4, mix_norm[1], w_q, NN, "swa_q", rope=(cos_t, sin_t, D // LANES))
    o_sw = _swa_fwd(q_rot, kv_rot, swa_sinks, "swa_attn")
    h5 = _proj(o_sw, w_swo, NN, F32, "swa_out", res=h4)
    h6, sv_b2, _ = _ffn_fwd(h5, ffn2_norm[1], win2b_t, wo2b, "ffn2b")
    dh6, dg_final, sq_err = _final_loss(h6, final_norm, tgt, "final_loss")
    loss = lax.psum(0.5 * jnp.sum(sq_err) / D, ("x", "y", "c"))

    dh5, dg_f2b, dwin2b_t, dwo2b, _, _ = _ffn_bwd(dh6, h5, ffn2_norm[1], win2b_t, wo2b, sv_b2, "ffn2b")
    do_sw = _proj(dh5, w_swo, NT, BF16, "swa_out_dx")
    dw_swo = _proj_dw(o_sw, dh5, "swa_out_dw")
    (dq, dk_sw, dv_sw, dsink), (p_win2b, p_swo) = _swa_bwd(
        q_rot, kv_rot, swa_sinks, o_sw, do_sw, cos_t, sin_t, "swa_attn_bwd", carry=("scatter", [dwin2b_t, dw_swo]))
    dw_q = _proj_dw(hn_b, dq, "swa_q_dw")
    (dh4, dg_mix_b), _ = _dx_norm_bwd([(dq, w_q, NT, 1, 0)], h4, mix_norm[1], dh5, "swa_q_dx")
    dh3, dg_f1b, dwin1b_t, dwo1b, (p_q, p_wo2b), _ = _ffn_bwd(dh4, h3, ffn1_norm[1], win1b_t, wo1b, sv_b1, "ffn1b",
                                                              carry=("scatter", [dw_q, dwo2b]))
    dkv = _rotary(jnp.concatenate([dk_sw, dv_sw], axis=1), cos_t, sin_t, KV // (2 * LANES), True, "kv_rope_bwd")
    dw_kv = _proj_dw(kvn, dkv, "kv_proj_dw")
    (dh3, dg_kv), _ = _dx_norm_bwd([(dkv, w_kv, NT, 1, 0)], h3, kv_norm, dh3, "kv_proj_dx")
    dh2, dg_f2a, dwin2a_t, dwo2a, (p_win1b, p_kv), (p_wo1b,) = _ffn_bwd(
        dh3, h2, ffn2_norm[0], win2a_t, wo2a, sv_a2, "ffn2a",
        carry=("scatter", [dwin1b_t, dw_kv]), carry_dwin=("scatter", [dwo1b]))
    do_sb = _proj(dh2, w_sbo, NT, BF16, "sb_out_dx")
    dw_sbo = _proj_dw(o_sb, dh2, "sb_out_dw")
    (dq_sb, dk_sb, dv_sb), (p_win2a, p_wo2a, p_sbo) = _sb_bwd(
        qkv, kv_t, o_sb, do_sb, "sb_attn_bwd", carry=("scatter", [dwin2a_t, dwo2a, dw_sbo]))
    dqkv = [dq_sb, dk_sb, dv_sb]
    dwqkv_t, _ = _dw_rows(dqkv, hn_a, "sb_qkv_dw", tk=TN_CHUNK // 2)
    (dh1, dg_mix_a), (p_qkv,) = _dx_norm_bwd([(dy, wqkv_t, NN, 3, n) for n, dy in enumerate(dqkv)], h1, mix_norm[0],
                                             dh2, "sb_qkv_dx", carry=("scatter", [dwqkv_t]))
    dx, dg_f1a, p_win1a, p_wo1a = _ffn_bwd(dh1, h0, ffn1_norm[0], win1a_t, wo1a, sv_a1, "ffn1a", scatter=True)

    def natural(parts, tag):
        return _sum8(parts, f"sum_{tag}")

    def from_t(parts, tag):
        return jnp.transpose(_sum8(parts, f"sum_{tag}"))

    grads = {
        "ffn1_w_in": jnp.stack([from_t(p_win1a, "win1a"), from_t(p_win1b, "win1b")]),
        "ffn1_w_out": jnp.stack([natural(p_wo1a, "wo1a"), natural(p_wo1b, "wo1b")]),
        "ffn2_w_in": jnp.stack([from_t(p_win2a, "win2a"), from_t(p_win2b, "win2b")]),
        "ffn2_w_out": jnp.stack([natural(p_wo2a, "wo2a"), natural(p_wo2b, "wo2b")]),
        "sb_w_qkv": from_t(p_qkv, "qkv")[None],
        "sb_w_o": natural(p_sbo, "sbo")[None],
        "kv_w": natural(p_kv, "kv"),
        "swa_w_q": natural(p_q, "swq")[None],
        "swa_w_o": natural(p_swo, "swo")[None],
    }

    small_w = [ffn1_norm, mix_norm, ffn2_norm, kv_norm, final_norm, swa_sinks]
    small_m = [m_ffn1_norm, m_mix_norm, m_ffn2_norm, m_kv_norm, m_final_norm, m_swa_sinks]
    small_v = [v_ffn1_norm, v_mix_norm, v_ffn2_norm, v_kv_norm, v_final_norm, v_swa_sinks]
    SMALL_ROWS = 16

    def pack_small(ts):
        rows_ = [t.reshape(-1, D) for t in ts[:-1]]
        sink_row = jnp.pad(ts[-1].reshape(1, -1), ((0, 0), (0, D - ts[-1].size)))
        flat = jnp.concatenate(rows_ + [sink_row], axis=0)
        return jnp.pad(flat, ((0, SMALL_ROWS - flat.shape[0]), (0, 0)))

    def unpack_small(flat):
        out, r = [], 0
        for t in small_w[:-1]:
            n = t.size // D
            out.append(flat[r:r + n].reshape(t.shape))
            r += n
        out.append(flat[r, :swa_sinks.size].reshape(swa_sinks.shape))
        return out

    def gain(parts8):
        return jnp.sum(parts8, axis=0, keepdims=True)

    g_small_local = pack_small([
        jnp.concatenate([gain(dg_f1a), gain(dg_f1b)], axis=0),
        jnp.concatenate([gain(dg_mix_a), gain(dg_mix_b)], axis=0),
        jnp.concatenate([gain(dg_f2a), gain(dg_f2b)], axis=0),
        gain(dg_kv), gain(dg_final), dsink[:, :, 0].reshape(1, -1)])
    small_parts = _exchange("gather", [g_small_local], "gather_small_grads")[0]
    g_small = _sum8(small_parts.reshape(N_DEV, SMALL_ROWS, D), "sum_small")
    d_small, nm_small, nv_small = _adamw(g_small, pack_small(small_w), pack_small(small_m), pack_small(small_v), "adamw_small")
    small_names = ["ffn1_norm", "mix_norm", "ffn2_norm", "kv_norm", "final_norm", "swa_sinks"]
    result = {"grad": dict(zip(small_names, unpack_small(g_small))),
              "delta": dict(zip(small_names, unpack_small(d_small))),
              "new_m": dict(zip(small_names, unpack_small(nm_small))),
              "new_v": dict(zip(small_names, unpack_small(nv_small)))}

    big = {"ffn1_w_in": (ffn1_w_in, m_ffn1_w_in, v_ffn1_w_in), "ffn1_w_out": (ffn1_w_out, m_ffn1_w_out, v_ffn1_w_out),
           "ffn2_w_in": (ffn2_w_in, m_ffn2_w_in, v_ffn2_w_in), "ffn2_w_out": (ffn2_w_out, m_ffn2_w_out, v_ffn2_w_out),
           "sb_w_qkv": (sb_w_qkv, m_sb_w_qkv, v_sb_w_qkv), "sb_w_o": (sb_w_o, m_sb_w_o, v_sb_w_o),
           "kv_w": (kv_w, m_kv_w, v_kv_w), "swa_w_q": (swa_w_q, m_swa_w_q, v_swa_w_q),
           "swa_w_o": (swa_w_o, m_swa_w_o, v_swa_w_o)}
    for nm, (w, m, v) in big.items():
        g = grads[nm]
        two_d = lambda t: t.reshape(-1, t.shape[-1])
        d, new_m, new_v = _adamw(two_d(g), two_d(w), two_d(m), two_d(v), f"adamw_{nm}")
        result["grad"][nm] = g
        result["delta"][nm] = d.reshape(w.shape)
        result["new_m"][nm] = new_m.reshape(w.shape)
        result["new_v"][nm] = new_v.reshape(w.shape)

    order = ["ffn1_norm", "ffn1_w_in", "ffn1_w_out", "mix_norm", "ffn2_norm", "ffn2_w_in", "ffn2_w_out",
             "sb_w_qkv", "sb_w_o", "kv_norm", "kv_w", "swa_w_q", "swa_sinks", "swa_w_o", "final_norm"]
    outs = [result[kind][nm] for kind in ("grad", "delta", "new_m", "new_v") for nm in order]
    return (loss, dx.reshape(x.shape), *outs)
```

```python
import jax
import jax.numpy as jnp
from jax import lax
from jax.experimental import pallas as pl
from jax.experimental.pallas import tpu as pltpu

F32 = jnp.float32
BF16 = jnp.bfloat16

N_DEV = 8
HEAD_DIM = 64
LANES = 128
BLK = 128
RMS_EPS = 1e-6
FFN_RES_SCALE = 0.5
ROPE_THETA = 10000.0
ATTN_SCALE = HEAD_DIM ** -0.5
SB_LOG_FLOOR = -88.0
NEG_BIG = -1e30
VMEM_LIMIT_V7X = 56 * 1024 * 1024

ADAM_LR = 0.001
ADAM_B1 = 0.9
ADAM_B2 = 0.999
ADAM_EPS = 1e-08
ADAM_WD = 0.01
ADAM_STEP = 10

NN = ((1,), (0,))
NT = ((1,), (1,))
TN = ((0,), (0,))
TN_CHUNK = 2048
MESH = pl.DeviceIdType.MESH


def _dot(a, b, dims):
    return lax.dot_general(a, b, (dims, ((), ())), preferred_element_type=F32)


def _tile(n, pref, mult=LANES):
    if n <= pref:
        return n
    t = (pref // mult) * mult
    while t >= mult:
        if n % t == 0:
            return t
        t -= mult
    return n


def _params(*sem):
    return pltpu.CompilerParams(dimension_semantics=sem, vmem_limit_bytes=VMEM_LIMIT_V7X)


def _mm(a, b, dims, out_dtype, name, scale=1.0, res=None, tm=512, tn=512, tk=512):
    if dims == NN:
        (M, K), (_, N) = a.shape, b.shape
    elif dims == NT:
        (M, K), (N, _) = a.shape, b.shape
    else:
        (K, M), (_, N) = a.shape, b.shape
    tm, tn, tk = _tile(M, tm), _tile(N, tn), _tile(K, tk)
    nk = K // tk
    if dims == TN:
        a_spec = pl.BlockSpec((tk, tm), lambda i, j, k: (k, i))
    else:
        a_spec = pl.BlockSpec((tm, tk), lambda i, j, k: (i, k))
    if dims == NT:
        b_spec = pl.BlockSpec((tn, tk), lambda i, j, k: (j, k))
    else:
        b_spec = pl.BlockSpec((tk, tn), lambda i, j, k: (k, j))
    o_spec = pl.BlockSpec((tm, tn), lambda i, j, k: (i, j))
    has_res = res is not None

    def body(*refs):
        a_ref, b_ref = refs[0], refs[1]
        r_ref = refs[2] if has_res else None
        o_ref = refs[3] if has_res else refs[2]

        def finish(acc):
            r = acc * scale if scale != 1.0 else acc
            if has_res:
                r = r + r_ref[...]
            o_ref[...] = r.astype(out_dtype)

        p = _dot(a_ref[...].astype(BF16), b_ref[...].astype(BF16), dims)
        if nk == 1:
            finish(p)
        else:
            acc_ref = refs[-1]
            k = pl.program_id(2)

            @pl.when(k == 0)
            def _():
                acc_ref[...] = p

            @pl.when(k > 0)
            def _():
                acc_ref[...] += p

            @pl.when(k == nk - 1)
            def _():
                finish(acc_ref[...])

    in_specs = [a_spec, b_spec] + ([o_spec] if has_res else [])
    args = (a, b) + ((res,) if has_res else ())
    return pl.pallas_call(
        body, name=name,
        out_shape=jax.ShapeDtypeStruct((M, N), out_dtype),
        grid=(M // tm, N // tn, nk),
        in_specs=in_specs, out_specs=o_spec,
        scratch_shapes=[pltpu.VMEM((tm, tn), F32)] if nk > 1 else [],
        compiler_params=_params("parallel", "parallel", "arbitrary"),
    )(*args)


def _rows8(x):
    r, d = x.shape
    return jnp.sum(x.reshape(r // 8, 8, d), axis=0)


def _norm_proj(h, g, w, dims, name, rope=None, tail_t=0):
    S, D = h.shape
    N = w.shape[1] if dims == NN else w.shape[0]
    tm = _tile(S, 512, 16)

    def body(h_ref, g_ref, w_ref, *rest):
        xn_ref, y_ref = rest[-3:-1] if tail_t else rest[-2:]
        x = h_ref[...]
        r = lax.rsqrt(jnp.mean(x * x, axis=-1, keepdims=True) + RMS_EPS)
        xn = ((x * r) * g_ref[...]).astype(BF16)
        xn_ref[...] = xn
        y = _dot(xn, w_ref[...], dims)
        if tail_t:
            rest[-1][...] = jnp.transpose(y[:, N - tail_t:]).astype(BF16)
        if rope is None:
            y_ref[...] = y.astype(BF16)
        else:
            cs, sn = rest[0][...], rest[1][...]
            for gidx in range(N // LANES):
                sl = slice(gidx * LANES, (gidx + 1) * LANES)
                v = y[:, sl]
                if gidx < rope[2]:
                    v = v * cs + _swap_halves(v) * sn
                y_ref[:, sl] = v.astype(BF16)

    row = pl.BlockSpec((tm, D), lambda i: (i, 0))
    tab = pl.BlockSpec((tm, LANES), lambda i: (i, 0))
    in_specs = [row, pl.BlockSpec((1, D), lambda i: (0, 0)), pl.BlockSpec(w.shape, lambda i: (0, 0))]
    args = (h, g.reshape(1, D), w)
    if rope is not None:
        in_specs += [tab, tab]
        args += (rope[0], rope[1])
    out_shape = [jax.ShapeDtypeStruct((S, D), BF16), jax.ShapeDtypeStruct((S, N), BF16)]
    out_specs = [row, pl.BlockSpec((tm, N), lambda i: (i, 0))]
    if tail_t:
        out_shape.append(jax.ShapeDtypeStruct((tail_t, S), BF16))
        out_specs.append(pl.BlockSpec((tail_t, tm), lambda i: (0, i)))
    return pl.pallas_call(
        body, name=name, out_shape=out_shape, grid=(S // tm,),
        in_specs=in_specs, out_specs=out_specs,
        compiler_params=_params("parallel"),
    )(*args)


def _final_loss(h, g, tgt, name):
    S, D = h.shape
    ts = _tile(S, 512, 8)

    def body(h_ref, g_ref, t_ref, dh_ref, dg_ref, l_ref):
        x = h_ref[...]
        r = lax.rsqrt(jnp.mean(x * x, axis=-1, keepdims=True) + RMS_EPS)
        xhat = x * r
        err = xhat * g_ref[...] - t_ref[...]
        d = err * (1.0 / D)
        dxh = d * g_ref[...]
        c = jnp.mean(dxh * xhat, axis=-1, keepdims=True)
        dh_ref[...] = r * (dxh - xhat * c)
        part = _rows8(d * xhat)
        lpart = _rows8(err * err)

        @pl.when(pl.program_id(0) == 0)
        def _():
            dg_ref[...] = part
            l_ref[...] = lpart

        @pl.when(pl.program_id(0) > 0)
        def _():
            dg_ref[...] += part
            l_ref[...] += lpart

    row = pl.BlockSpec((ts, D), lambda i: (i, 0))
    acc = pl.BlockSpec((8, D), lambda i: (0, 0))
    return pl.pallas_call(
        body, name=name,
        out_shape=(jax.ShapeDtypeStruct((S, D), F32), jax.ShapeDtypeStruct((8, D), F32),
                   jax.ShapeDtypeStruct((8, D), F32)),
        grid=(S // ts,),
        in_specs=[row, pl.BlockSpec((1, D), lambda i: (0, 0)), row],
        out_specs=(row, acc, acc),
        compiler_params=_params("arbitrary"),
    )(h, g.reshape(1, D), tgt)


def _ffn_up(h, g, win_t, name, carry=None):
    S, D = h.shape
    F = win_t.shape[0] // 2
    tm, tn = _tile(S, 512, 16), _tile(F, 1408)
    nf = F // tn

    def body(h_ref, g_ref, wg_ref, wu_ref, xn_ref, silu_ref, dsilu_ref, up_ref, act_ref):
        x = h_ref[...]
        r = lax.rsqrt(jnp.mean(x * x, axis=-1, keepdims=True) + RMS_EPS)
        xn = ((x * r) * g_ref[...]).astype(BF16)
        xn_ref[...] = xn
        gate = _dot(xn, wg_ref[...], NT)
        up = _dot(xn, wu_ref[...], NT)
        sig = 1.0 / (1.0 + jnp.exp(-gate))
        silu = gate * sig
        up_ref[...] = up.astype(BF16)
        silu_ref[...] = silu.astype(BF16)
        dsilu_ref[...] = (sig + silu * (1.0 - sig)).astype(BF16)
        act_ref[...] = (silu * up).astype(BF16)

    row = pl.BlockSpec((tm, D), lambda i, j: (i, 0))
    blk = pl.BlockSpec((tm, tn), lambda i, j: (i, j))
    hid = jax.ShapeDtypeStruct((S, F), BF16)
    return _pcall(
        body, (h, g.reshape(1, D), win_t, win_t), name=name,
        out_shape=(jax.ShapeDtypeStruct((S, D), BF16), hid, hid, hid, hid),
        grid=(S // tm, nf),
        in_specs=[row, pl.BlockSpec((1, D), lambda i, j: (0, 0)),
                  pl.BlockSpec((tn, D), lambda i, j: (j, 0)),
                  pl.BlockSpec((tn, D), lambda i, j: (j + nf, 0))],
        out_specs=(row, blk, blk, blk, blk),
        sem=("arbitrary", "arbitrary"), carry=carry)


def _ffn_dact(dh, wo, silu, dsilu, up, name):
    S, D = dh.shape
    F = wo.shape[0]
    tm, tn = _tile(S, 512, 16), _tile(F, 1408)

    def body(dh_ref, wo_ref, s_ref, ds_ref, u_ref, dg_ref, du_ref):
        d = _dot(dh_ref[...].astype(BF16), wo_ref[...], NT) * FFN_RES_SCALE
        du_ref[...] = (d * s_ref[...].astype(F32)).astype(BF16)
        dg_ref[...] = (d * u_ref[...].astype(F32) * ds_ref[...].astype(F32)).astype(BF16)

    blk = pl.BlockSpec((tm, tn), lambda j, i: (i, j))
    hid = jax.ShapeDtypeStruct((S, F), BF16)
    return pl.pallas_call(
        body, name=name, out_shape=(hid, hid),
        grid=(F // tn, S // tm),
        in_specs=[pl.BlockSpec((tm, D), lambda j, i: (i, 0)), pl.BlockSpec((tn, D), lambda j, i: (j, 0)),
                  blk, blk, blk],
        out_specs=(blk, blk),
        compiler_params=_params("arbitrary", "arbitrary"),
    )(dh, wo, silu, dsilu, up)


def _dw_rows(srcs, x, name, carry=None, tk=TN_CHUNK):
    n = len(srcs)
    S, F = srcs[0].shape
    D = x.shape[1]
    tr, tk = _tile(F, 1408), _tile(S, tk, 16)
    nf, nk = F // tr, S // tk

    def body(*refs):
        src_refs, (x_ref, o_ref, acc_ref) = refs[:n], refs[n:]
        r, k = pl.program_id(0), pl.program_id(1)
        for s in range(n):
            @pl.when(r // nf == s)
            def _():
                p = _dot(src_refs[s][...].astype(BF16), x_ref[...], TN)

                @pl.when(k == 0)
                def _():
                    acc_ref[...] = p

                @pl.when(k > 0)
                def _():
                    acc_ref[...] += p

        @pl.when(k == nk - 1)
        def _():
            o_ref[...] = acc_ref[...].astype(BF16)

    def src_spec(s):
        return pl.BlockSpec((tk, tr), lambda r, k: (jnp.where(r // nf == s, k, 0), jnp.clip(r - s * nf, 0, nf - 1)))

    return _pcall(
        body, (*srcs, x), name=name, out_shape=jax.ShapeDtypeStruct((n * F, D), BF16),
        grid=(n * nf, nk),
        in_specs=[src_spec(s) for s in range(n)] + [pl.BlockSpec((tk, D), lambda r, k: (k, 0))],
        out_specs=pl.BlockSpec((tr, D), lambda r, k: (r, 0)),
        scratch_shapes=[pltpu.VMEM((tr, D), F32)],
        sem=("arbitrary", "arbitrary"), carry=carry)


def _dx_norm_bwd(terms, h, g, res, name, carry=None):
    S, D = h.shape
    tm = _tile(S, 256, 16)
    n = len(terms)

    def body(*refs):
        dy_refs, w_refs = refs[:n], refs[n:2 * n]
        h_ref, g_ref, r_ref, dh_ref, dg_ref = refs[2 * n:]
        d = _dot(dy_refs[0][...].astype(BF16), w_refs[0][...], terms[0][2])
        for t in range(1, n):
            d = d + _dot(dy_refs[t][...].astype(BF16), w_refs[t][...], terms[t][2])
        x = h_ref[...]
        r = lax.rsqrt(jnp.mean(x * x, axis=-1, keepdims=True) + RMS_EPS)
        xhat = x * r
        dxh = d * g_ref[...]
        c = jnp.mean(dxh * xhat, axis=-1, keepdims=True)
        dh_ref[...] = r * (dxh - xhat * c) + r_ref[...]
        part = _rows8(d * xhat)

        @pl.when(pl.program_id(0) == 0)
        def _():
            dg_ref[...] = part

        @pl.when(pl.program_id(0) > 0)
        def _():
            dg_ref[...] += part

    def w_spec(w, nblk, blk):
        return pl.BlockSpec((w.shape[0] // nblk, w.shape[1]), lambda i: (blk, 0))

    row = pl.BlockSpec((tm, D), lambda i: (i, 0))
    in_specs = [pl.BlockSpec((tm, t[0].shape[1]), lambda i: (i, 0)) for t in terms]
    in_specs += [w_spec(t[1], t[3], t[4]) for t in terms]
    in_specs += [row, pl.BlockSpec((1, D), lambda i: (0, 0)), row]
    return _pcall(
        body, (*[t[0] for t in terms], *[t[1] for t in terms], h, g.reshape(1, D), res), name=name,
        out_shape=(jax.ShapeDtypeStruct((S, D), F32), jax.ShapeDtypeStruct((8, D), F32)),
        grid=(S // tm,),
        in_specs=in_specs,
        out_specs=(row, pl.BlockSpec((8, D), lambda i: (0, 0))),
        sem=("arbitrary",), carry=carry)


def _load_resident(pairs, sems):
    @pl.when(pl.program_id(0) == 0)
    def _():
        copies = [pltpu.make_async_copy(src, dst, sems.at[n]) for n, (src, dst) in enumerate(pairs)]
        for cp in copies:
            cp.start()
        for cp in copies:
            cp.wait()


def _ffn_fwd_fused(h, g, win_t, wo, name, carry=None):
    S, D = h.shape
    F = wo.shape[0]
    tm = _tile(S, 256, 16)

    def body(h_ref, g_ref, win_hbm, wo_hbm, out_ref, xn_ref, silu_ref, dsilu_ref, up_ref, act_ref, win_v, wo_v, sems):
        _load_resident([(win_hbm, win_v), (wo_hbm, wo_v)], sems)
        x = h_ref[...]
        r = lax.rsqrt(jnp.mean(x * x, axis=-1, keepdims=True) + RMS_EPS)
        xn = ((x * r) * g_ref[...]).astype(BF16)
        xn_ref[...] = xn
        gate = _dot(xn, win_v[:F, :], NT)
        up = _dot(xn, win_v[F:, :], NT)
        sig = 1.0 / (1.0 + jnp.exp(-gate))
        silu = gate * sig
        act = (silu * up).astype(BF16)
        up_ref[...] = up.astype(BF16)
        silu_ref[...] = silu.astype(BF16)
        dsilu_ref[...] = (sig + silu * (1.0 - sig)).astype(BF16)
        act_ref[...] = act
        out_ref[...] = x + FFN_RES_SCALE * _dot(act, wo_v[...], NN)

    row = pl.BlockSpec((tm, D), lambda i: (i, 0))
    wide = pl.BlockSpec((tm, F), lambda i: (i, 0))
    hbm = pl.BlockSpec(memory_space=pl.ANY)
    hid = jax.ShapeDtypeStruct((S, F), BF16)
    res, got = _pcall(
        body, (h, g.reshape(1, D), win_t, wo), name=name,
        out_shape=(jax.ShapeDtypeStruct((S, D), F32), jax.ShapeDtypeStruct((S, D), BF16), hid, hid, hid, hid),
        grid=(S // tm,),
        in_specs=[row, pl.BlockSpec((1, D), lambda i: (0, 0)), hbm, hbm],
        out_specs=(row, row, wide, wide, wide, wide),
        scratch_shapes=[pltpu.VMEM(win_t.shape, BF16), pltpu.VMEM(wo.shape, BF16), pltpu.SemaphoreType.DMA((2,))],
        sem=("arbitrary",), carry=carry)
    return res[0], tuple(res[1:]), got


def _ffn_bwd_fused(dh, h, g, win_t, wo, silu, dsilu, up, name, carry=None):
    S, D = h.shape
    F = wo.shape[0]
    tm = _tile(S, 256, 16)

    def body(dh_ref, h_ref, g_ref, s_ref, ds_ref, u_ref, win_hbm, wo_hbm,
             dhin_ref, dgain_ref, dgate_ref, dup_ref, win_v, wo_v, sems):
        _load_resident([(win_hbm, win_v), (wo_hbm, wo_v)], sems)
        dhv = dh_ref[...]
        d = _dot(dhv.astype(BF16), wo_v[...], NT) * FFN_RES_SCALE
        dup = (d * s_ref[...].astype(F32)).astype(BF16)
        dgate = (d * u_ref[...].astype(F32) * ds_ref[...].astype(F32)).astype(BF16)
        dup_ref[...] = dup
        dgate_ref[...] = dgate
        dxn = _dot(dgate, win_v[:F, :], NN) + _dot(dup, win_v[F:, :], NN)
        x = h_ref[...]
        r = lax.rsqrt(jnp.mean(x * x, axis=-1, keepdims=True) + RMS_EPS)
        xhat = x * r
        dxh = dxn * g_ref[...]
        c = jnp.mean(dxh * xhat, axis=-1, keepdims=True)
        dhin_ref[...] = r * (dxh - xhat * c) + dhv
        part = _rows8(dxn * xhat)

        @pl.when(pl.program_id(0) == 0)
        def _():
            dgain_ref[...] = part

        @pl.when(pl.program_id(0) > 0)
        def _():
            dgain_ref[...] += part

    row = pl.BlockSpec((tm, D), lambda i: (i, 0))
    wide = pl.BlockSpec((tm, F), lambda i: (i, 0))
    hbm = pl.BlockSpec(memory_space=pl.ANY)
    hid = jax.ShapeDtypeStruct((S, F), BF16)
    return _pcall(
        body, (dh, h, g.reshape(1, D), silu, dsilu, up, win_t, wo), name=name,
        out_shape=(jax.ShapeDtypeStruct((S, D), F32), jax.ShapeDtypeStruct((8, D), F32), hid, hid),
        grid=(S // tm,),
        in_specs=[row, row, pl.BlockSpec((1, D), lambda i: (0, 0)), wide, wide, wide, hbm, hbm],
        out_specs=(row, pl.BlockSpec((8, D), lambda i: (0, 0)), wide, wide),
        scratch_shapes=[pltpu.VMEM(win_t.shape, BF16), pltpu.VMEM(wo.shape, BF16), pltpu.SemaphoreType.DMA((2,))],
        sem=("arbitrary",), carry=carry)


def _rope_tables(S):
    half = HEAD_DIM // 2
    inv_freq = ROPE_THETA ** (-jnp.arange(half, dtype=F32) / half)
    ang = jnp.arange(S).astype(F32)[:, None] * inv_freq[None, :]
    cos, sin = jnp.cos(ang), jnp.sin(ang)
    cos_t = jnp.tile(cos, (1, LANES // half))
    sin_t = jnp.tile(jnp.concatenate([-sin, sin], axis=1), (1, LANES // HEAD_DIM))
    return cos_t, sin_t


def _swap_halves(x):
    lane = lax.broadcasted_iota(jnp.int32, x.shape, 1)
    first = (lane % HEAD_DIM) < (HEAD_DIM // 2)
    return jnp.where(first, pltpu.roll(x, LANES - HEAD_DIM // 2, 1), pltpu.roll(x, HEAD_DIM // 2, 1))


def _rotary(x, cos_t, sin_t, n_rot, inverse, name):
    S, C = x.shape
    ts = _tile(S, 512, 16)
    ng = C // LANES

    def body(x_ref, c_ref, s_ref, o_ref):
        cs, sn = c_ref[...], s_ref[...]
        for gidx in range(ng):
            sl = slice(gidx * LANES, (gidx + 1) * LANES)
            v = x_ref[:, sl].astype(F32)
            if gidx < n_rot:
                if inverse:
                    v = v * cs + _swap_halves(v * sn)
                else:
                    v = v * cs + _swap_halves(v) * sn
            o_ref[:, sl] = v.astype(BF16)

    row = pl.BlockSpec((ts, C), lambda i: (i, 0))
    tab = pl.BlockSpec((ts, LANES), lambda i: (i, 0))
    return pl.pallas_call(
        body, name=name, out_shape=jax.ShapeDtypeStruct((S, C), BF16),
        grid=(S // ts,), in_specs=[row, tab, tab], out_specs=row,
        compiler_params=_params("parallel"),
    )(x, cos_t, sin_t)


def _head_masks():
    lane = lax.broadcasted_iota(jnp.int32, (BLK, LANES), 1)
    return lane < HEAD_DIM


def _split_bf16(x):
    hi = x.astype(BF16)
    lo = (x - hi.astype(F32)).astype(BF16)
    return hi, lo


def _sb_scores(qh, ks, carry, diag, tri_excl, strict):
    n_heads = len(qh)
    zs = [_dot(ks[n], qh[n], NT) for n in range(n_heads)]
    a_l, b_l, split_l = [], [], []
    for z in zs:
        a = jnp.minimum(z, 0.0) - jnp.log(1.0 + jnp.exp(-jnp.abs(z)))
        b = a - z
        if diag:
            b = jnp.where(strict, b, 0.0)
        a_l.append(a)
        b_l.append(b)
        split_l.append(_split_bf16(b))
    sufs = [_dot(tri_excl, hi, NN) + _dot(tri_excl, lo, NN) for hi, lo in split_l]
    w_l = []
    for n in range(n_heads):
        w = jnp.exp(a_l[n] + sufs[n] + carry[n])
        if diag:
            w = jnp.where(strict, w, 0.0)
        w_l.append(w)
    return a_l, b_l, w_l


SB_FWD_PAIRS = 4
SB_FWD_QBLOCKS = 2
SB_BWD_PAIRS = 2
SB_BWD_QBLOCKS = 4


def _any_alive(carries):
    top = carries[0]
    for c in carries[1:]:
        top = jnp.maximum(top, c)
    return (jnp.max(top) > SB_LOG_FLOOR).astype(jnp.int32)


def _sb_masks():
    row = lax.broadcasted_iota(jnp.int32, (BLK, BLK), 0)
    col = lax.broadcasted_iota(jnp.int32, (BLK, BLK), 1)
    tri_excl = jnp.where(col > row, 1.0, 0.0).astype(BF16)
    tri_incl = jnp.where(col >= row, 1.0, 0.0).astype(BF16)
    return row < HEAD_DIM, row < col, tri_excl, tri_incl


def _sb_fwd(qkv, kv_t, name, carry=None):
    S, D3 = qkv.shape
    D = D3 // 3
    npair, nb = D // LANES, S // BLK
    P = min(SB_FWD_PAIRS, npair)
    ngroup = npair // P
    W = P * LANES

    QB = SB_FWD_QBLOCKS if nb % SB_FWD_QBLOCKS == 0 else 1
    nch = QB * 2 * P

    def body(q_ref, k_ref, vt_ref, o_ref):
        i_first = pl.program_id(1) * QB
        m0 = _head_masks()
        top, strict, tri_excl, _ = _sb_masks()
        zq = jnp.zeros((BLK, LANES), BF16)
        lanes = [slice(p * LANES, (p + 1) * LANES) for p in range(P)]
        qh = []
        for qb in range(QB):
            for sl in lanes:
                q2 = q_ref[qb * BLK:(qb + 1) * BLK, sl] * ATTN_SCALE
                qh += [jnp.where(m0, q2, zq), jnp.where(m0, zq, q2)]

        def block(js, carry, acc, diag):
            offs = [pl.multiple_of(j * BLK, BLK) for j in js]
            ks, vth = [], []
            for qb in range(QB):
                for sl in lanes:
                    k2 = k_ref[pl.ds(offs[qb], BLK), sl]
                    vt = vt_ref[sl, pl.ds(offs[qb], BLK)]
                    ks += [k2, k2]
                    vth += [jnp.where(top, vt, zq), jnp.where(top, zq, vt)]
            _, b_l, w_l = _sb_scores(qh, ks, carry, diag, tri_excl, strict)
            wb = [w.astype(BF16) for w in w_l]
            new_acc = [acc[m] + _dot(vth[2 * m], wb[2 * m], NN) + _dot(vth[2 * m + 1], wb[2 * m + 1], NN)
                       for m in range(QB * P)]
            new_carry = [carry[n] + jnp.sum(b_l[n], axis=0, keepdims=True) for n in range(nch)]
            return new_carry, new_acc

        c0 = jnp.zeros((1, BLK), F32)
        carry, acc = block([i_first + qb for qb in range(QB)], [c0] * nch,
                           [jnp.zeros((LANES, BLK), F32)] * (QB * P), True)

        def cond(st):
            return jnp.logical_and(i_first + QB - 1 - st[0] >= 0, st[1] > 0)

        def step(st):
            t, _, carry, acc = st
            js = [i_first + qb - t for qb in range(QB)]
            carry = [carry[n] if n // (2 * P) == QB - 1 else jnp.where(js[n // (2 * P)] >= 0, carry[n], NEG_BIG)
                     for n in range(nch)]
            carry, acc = block([jnp.maximum(j, 0) for j in js], carry, acc, False)
            return t + 1, _any_alive(carry), carry, acc

        st = lax.while_loop(cond, step, (1, _any_alive(carry), carry, acc))
        for qb in range(QB):
            for p, sl in enumerate(lanes):
                o_ref[qb * BLK:(qb + 1) * BLK, sl] = jnp.transpose(st[3][qb * P + p])

    return _pcall(
        body, (qkv, qkv, kv_t), name=name, out_shape=jax.ShapeDtypeStruct((S, D), F32),
        grid=(ngroup, nb // QB),
        in_specs=[pl.BlockSpec((QB * BLK, W), lambda g, i: (i, g)),
                  pl.BlockSpec((S, W), lambda g, i: (0, ngroup + g)),
                  pl.BlockSpec((W, S), lambda g, i: (ngroup + g, 0))],
        out_specs=pl.BlockSpec((QB * BLK, W), lambda g, i: (i, g)),
        sem=("arbitrary", "arbitrary"), carry=carry)


def _sb_bwd(qkv, kv_t, o, do, name, carry=None):
    S, D3 = qkv.shape
    D = D3 // 3
    npair, nb = D // LANES, S // BLK
    P = min(SB_BWD_PAIRS, npair)
    ngroup = npair // P
    W = P * LANES

    QB = SB_BWD_QBLOCKS if nb % SB_BWD_QBLOCKS == 0 else 1
    nch = QB * 2 * P

    def body(q_ref, o_ref, do_ref, qkv_hbm, kt_hbm, dq_ref, dk_ref, dv_ref, k_ref, v_ref, kt_ref, sems):
        grp = pl.program_id(0)
        i_first = pl.program_id(1) * QB
        m0 = _head_masks()
        top, strict, tri_excl, tri_incl = _sb_masks()
        zq = jnp.zeros((BLK, LANES), BF16)
        lanes = [slice(p * LANES, (p + 1) * LANES) for p in range(P)]

        @pl.when(pl.program_id(1) == 0)
        def _():
            copies = [pltpu.make_async_copy(qkv_hbm.at[:, pl.ds(pl.multiple_of((c * ngroup + grp) * W, LANES), W)],
                                            ref, sems.at[c - 1]) for c, ref in ((1, k_ref), (2, v_ref))]
            copies.append(pltpu.make_async_copy(kt_hbm.at[pl.ds(pl.multiple_of(grp * W, LANES), W), :],
                                                kt_ref, sems.at[2]))
            for cp in copies:
                cp.start()
            dk_ref[...] = jnp.zeros_like(dk_ref)
            dv_ref[...] = jnp.zeros_like(dv_ref)
            for cp in copies:
                cp.wait()

        qh, doh, delta = [], [], []
        for qb in range(QB):
            rs = slice(qb * BLK, (qb + 1) * BLK)
            for sl in lanes:
                q2, do2 = q_ref[rs, sl] * ATTN_SCALE, do_ref[rs, sl]
                qh += [jnp.where(m0, q2, zq), jnp.where(m0, zq, q2)]
                doh += [jnp.where(m0, do2, zq), jnp.where(m0, zq, do2)]
                prod_t = jnp.transpose(do2.astype(F32) * o_ref[rs, sl])
                delta += [jnp.sum(jnp.where(top, prod_t, 0.0), axis=0, keepdims=True),
                          jnp.sum(jnp.where(top, 0.0, prod_t), axis=0, keepdims=True)]

        def block(js, valid, cb, cg, dq, diag):
            offs = [pl.multiple_of(j * BLK, BLK) for j in js]
            ks, vs, kth = [], [], []
            for qb in range(QB):
                for sl in lanes:
                    k2, v2 = k_ref[pl.ds(offs[qb], BLK), sl], v_ref[pl.ds(offs[qb], BLK), sl]
                    ks += [k2, k2]
                    vs += [v2, v2]
                    kt = kt_ref[sl, pl.ds(offs[qb], BLK)] * ATTN_SCALE
                    kth += [jnp.where(top, kt, zq), jnp.where(top, zq, kt)]
            dws = [_dot(vs[n], doh[n], NT) for n in range(nch)]
            a_l, b_l, w_l = _sb_scores(qh, ks, cb, diag, tri_excl, strict)
            wb = [w.astype(BF16) for w in w_l]
            g_l = [dws[n] * wb[n].astype(F32) for n in range(nch)]
            gsplit = [_split_bf16(g) for g in g_l]
            gincs = [_dot(tri_incl, hi, NN) + _dot(tri_incl, lo, NN) for hi, lo in gsplit]
            dzs = []
            for n in range(nch):
                beta = jnp.exp(a_l[n])
                dz = g_l[n] - beta * (g_l[n] + ((delta[n] - cg[n]) - gincs[n]))
                if diag:
                    dz = jnp.where(strict, dz, 0.0)
                if valid[n // (2 * P)] is not None:
                    dz = jnp.where(valid[n // (2 * P)], dz, 0.0)
                dzs.append(dz.astype(BF16))
            ndq = []
            for qb in range(QB):
                for p, sl in enumerate(lanes):
                    n0 = qb * 2 * P + 2 * p
                    ndq.append(dq[qb * P + p] + _dot(kth[n0], dzs[n0], NN) + _dot(kth[n0 + 1], dzs[n0 + 1], NN))
                    dk_ref[pl.ds(offs[qb], BLK), sl] += _dot(dzs[n0], qh[n0], NN) + _dot(dzs[n0 + 1], qh[n0 + 1], NN)
                    dv_ref[pl.ds(offs[qb], BLK), sl] += _dot(wb[n0], doh[n0], NN) + _dot(wb[n0 + 1], doh[n0 + 1], NN)
            ncb = [cb[n] + jnp.sum(b_l[n], axis=0, keepdims=True) for n in range(nch)]
            ncg = [cg[n] + jnp.sum(g_l[n], axis=0, keepdims=True) for n in range(nch)]
            return ncb, ncg, ndq

        c0 = jnp.zeros((1, BLK), F32)
        cb, cg, dq = block([i_first + qb for qb in range(QB)], [None] * QB, [c0] * nch, [c0] * nch,
                           [jnp.zeros((LANES, BLK), F32)] * (QB * P), True)

        def cond(st):
            return jnp.logical_and(i_first + QB - 1 - st[0] >= 0, st[1] > 0)

        def step(st):
            t, _, cb, cg, dq = st
            js = [i_first + qb - t for qb in range(QB)]
            valid = [js[qb] >= 0 for qb in range(QB - 1)] + [None]
            cb = [cb[n] if valid[n // (2 * P)] is None else jnp.where(valid[n // (2 * P)], cb[n], NEG_BIG)
                  for n in range(nch)]
            cb, cg, dq = block([jnp.maximum(j, 0) for j in js], valid, cb, cg, dq, False)
            return t + 1, _any_alive(cb), cb, cg, dq

        st = lax.while_loop(cond, step, (1, _any_alive(cb), cb, cg, dq))
        for qb in range(QB):
            for p, sl in enumerate(lanes):
                dq_ref[qb * BLK:(qb + 1) * BLK, sl] = jnp.transpose(st[4][qb * P + p]).astype(BF16)

    blk = pl.BlockSpec((QB * BLK, W), lambda g, i: (i, g))
    col_all = pl.BlockSpec((S, W), lambda g, i: (0, g))
    hbm = pl.BlockSpec(memory_space=pl.ANY)
    return _pcall(
        body, (qkv, o, do, qkv, kv_t), name=name,
        out_shape=(jax.ShapeDtypeStruct((S, D), BF16), jax.ShapeDtypeStruct((S, D), F32),
                   jax.ShapeDtypeStruct((S, D), F32)),
        grid=(ngroup, nb // QB),
        in_specs=[blk, blk, blk, hbm, hbm],
        out_specs=(blk, col_all, col_all),
        scratch_shapes=[pltpu.VMEM((S, W), BF16), pltpu.VMEM((S, W), BF16), pltpu.VMEM((W, S), BF16),
                        pltpu.SemaphoreType.DMA((3,))],
        sem=("arbitrary", "arbitrary"), carry=carry)


SWA_Q_GROUPS = 4


def _roll_heads(x):
    return pltpu.roll(x.astype(F32), HEAD_DIM, 1).astype(BF16)


def _swa_valid(i):
    r = lax.broadcasted_iota(jnp.int32, (BLK, 2 * BLK), 0)
    c = lax.broadcasted_iota(jnp.int32, (BLK, 2 * BLK), 1)
    diff = r + BLK - c
    return (diff >= 0) & (diff < BLK) & ((i > 0) | (c >= BLK))


def _swa_probs(z, valid, sink):
    z = jnp.where(valid, z * ATTN_SCALE, NEG_BIG)
    mx = jnp.maximum(jnp.max(z, axis=1, keepdims=True), sink)
    p = jnp.exp(z - mx)
    ps = jnp.exp(sink - mx)
    inv = 1.0 / (jnp.sum(p, axis=1, keepdims=True) + ps)
    return p * inv, ps * inv


def _swa_operands(q_ref, kc_ref, kp_ref, vc_ref, vp_ref, s_ref, m):
    m0 = _head_masks()
    m0k = jnp.concatenate([m0, m0], axis=0)
    kk = jnp.concatenate([kp_ref[...], kc_ref[...]], axis=0)
    vv = jnp.concatenate([vp_ref[...], vc_ref[...]], axis=0)
    ksw, vsw = _roll_heads(kk), _roll_heads(vv)
    zk = jnp.zeros_like(kk)
    heads = []
    for c in range(SWA_Q_GROUPS):
        qc = q_ref[:, c * LANES:(c + 1) * LANES]
        zq = jnp.zeros_like(qc)
        for u in range(2):
            same = u == c // 2
            sel = (lambda x, z, mk: jnp.where(mk, x, z)) if u == 0 else (lambda x, z, mk: jnp.where(mk, z, x))
            heads.append(dict(
                c=c, same=same, sel=sel,
                qm=sel(qc, zq, m0),
                k=kk if same else ksw, v=vv if same else vsw,
                km=sel(kk if same else ksw, zk, m0k), vm=sel(vv if same else vsw, zk, m0k),
                sink=s_ref[0, m * 2 * SWA_Q_GROUPS + 2 * c + u]))
    return heads, m0


def _swa_fwd(q, kv, sinks, name):
    S, D = q.shape
    nkvp = kv.shape[1] // (2 * LANES)
    nb = S // BLK
    qw = SWA_Q_GROUPS * LANES

    def body(q_ref, kc_ref, kp_ref, vc_ref, vp_ref, s_ref, o_ref):
        m, i = pl.program_id(0), pl.program_id(1)
        valid = _swa_valid(i)
        heads, _ = _swa_operands(q_ref, kc_ref, kp_ref, vc_ref, vp_ref, s_ref, m)
        zs = [_dot(hd["qm"], hd["k"], NT) for hd in heads]
        ps = [_swa_probs(z, valid, hd["sink"])[0].astype(BF16) for z, hd in zip(zs, heads)]
        for c in range(SWA_Q_GROUPS):
            o_ref[:, c * LANES:(c + 1) * LANES] = (_dot(ps[2 * c], heads[2 * c]["vm"], NN)
                                                   + _dot(ps[2 * c + 1], heads[2 * c + 1]["vm"], NN))

    prev = lambda i: jnp.maximum(i - 1, 0)
    return pl.pallas_call(
        body, name=name, out_shape=jax.ShapeDtypeStruct((S, D), F32),
        grid=(nkvp, nb),
        in_specs=[pl.BlockSpec((BLK, qw), lambda m, i: (i, m)),
                  pl.BlockSpec((BLK, LANES), lambda m, i: (i, m)),
                  pl.BlockSpec((BLK, LANES), lambda m, i: (prev(i), m)),
                  pl.BlockSpec((BLK, LANES), lambda m, i: (i, nkvp + m)),
                  pl.BlockSpec((BLK, LANES), lambda m, i: (prev(i), nkvp + m)),
                  pl.BlockSpec(memory_space=pltpu.SMEM)],
        out_specs=pl.BlockSpec((BLK, qw), lambda m, i: (i, m)),
        compiler_params=_params("arbitrary", "arbitrary"),
    )(q, kv, kv, kv, kv, sinks)


def _swa_bwd(q, kv, sinks, o, do, cos_t, sin_t, name, carry=None):
    S, D = q.shape
    nkvp = kv.shape[1] // (2 * LANES)
    nb = S // BLK
    qw = SWA_Q_GROUPS * LANES
    nh = 2 * SWA_Q_GROUPS

    def body(q_ref, kc_ref, kp_ref, vc_ref, vp_ref, s_ref, o_ref, do_ref, c_ref, sn_ref,
             dq_ref, dk_ref, dv_ref, ds_ref):
        m, i = pl.program_id(0), pl.program_id(1)
        valid = _swa_valid(i)
        heads, m0 = _swa_operands(q_ref, kc_ref, kp_ref, vc_ref, vp_ref, s_ref, m)

        @pl.when(i == 0)
        def _():
            dk_ref[...] = jnp.zeros_like(dk_ref)
            dv_ref[...] = jnp.zeros_like(dv_ref)
            ds_ref[...] = jnp.zeros_like(ds_ref)

        doms, deltas = [], []
        for hd in heads:
            c = hd["c"]
            doc = do_ref[:, c * LANES:(c + 1) * LANES]
            prod = doc.astype(F32) * o_ref[:, c * LANES:(c + 1) * LANES]
            doms.append(hd["sel"](doc, jnp.zeros_like(doc), m0))
            deltas.append(jnp.sum(hd["sel"](prod, 0.0, m0), axis=1, keepdims=True))
        zs = [_dot(hd["qm"], hd["k"], NT) for hd in heads]
        dps = [_dot(dom, hd["v"], NT) for dom, hd in zip(doms, heads)]
        pbs, dscs = [], []
        for n, hd in enumerate(heads):
            p, psink = _swa_probs(zs[n], valid, hd["sink"])
            pbs.append(p.astype(BF16))
            dscs.append((p * (dps[n] - deltas[n]) * ATTN_SCALE).astype(BF16))
            dsink = jnp.sum(jnp.broadcast_to(-(psink * deltas[n]), (BLK, LANES)), axis=0, keepdims=True)
            ds_ref[0, n:n + 1, :] += dsink
        for c in range(SWA_Q_GROUPS):
            dq_rot = _dot(dscs[2 * c], heads[2 * c]["km"], NN) + _dot(dscs[2 * c + 1], heads[2 * c + 1]["km"], NN)
            dq_ref[:, c * LANES:(c + 1) * LANES] = (
                dq_rot * c_ref[...] + _swap_halves(dq_rot * sn_ref[...])).astype(BF16)
        acc = {}
        for n, hd in enumerate(heads):
            dk_n = _dot(dscs[n], hd["qm"], TN)
            dv_n = _dot(pbs[n], doms[n], TN)
            for key, val in ((("k", hd["same"]), dk_n), (("v", hd["same"]), dv_n)):
                acc[key] = val if key not in acc else acc[key] + val
        dkk = acc["k", True] + pltpu.roll(acc["k", False], HEAD_DIM, 1)
        dvv = acc["v", True] + pltpu.roll(acc["v", False], HEAD_DIM, 1)
        poff = pl.multiple_of(jnp.maximum(i - 1, 0) * BLK, BLK)
        coff = pl.multiple_of(i * BLK, BLK)
        dk_ref[pl.ds(poff, BLK), :] += dkk[:BLK]
        dv_ref[pl.ds(poff, BLK), :] += dvv[:BLK]
        dk_ref[pl.ds(coff, BLK), :] += dkk[BLK:]
        dv_ref[pl.ds(coff, BLK), :] += dvv[BLK:]

    prev = lambda i: jnp.maximum(i - 1, 0)
    qblk = pl.BlockSpec((BLK, qw), lambda m, i: (i, m))
    col_all = pl.BlockSpec((S, LANES), lambda m, i: (0, m))
    tab = pl.BlockSpec((BLK, LANES), lambda m, i: (i, 0))
    return _pcall(
        body, (q, kv, kv, kv, kv, sinks, o, do, cos_t, sin_t), name=name,
        out_shape=(jax.ShapeDtypeStruct((S, D), BF16),
                   jax.ShapeDtypeStruct((S, nkvp * LANES), F32),
                   jax.ShapeDtypeStruct((S, nkvp * LANES), F32),
                   jax.ShapeDtypeStruct((nkvp, nh, LANES), F32)),
        grid=(nkvp, nb),
        in_specs=[qblk,
                  pl.BlockSpec((BLK, LANES), lambda m, i: (i, m)),
                  pl.BlockSpec((BLK, LANES), lambda m, i: (prev(i), m)),
                  pl.BlockSpec((BLK, LANES), lambda m, i: (i, nkvp + m)),
                  pl.BlockSpec((BLK, LANES), lambda m, i: (prev(i), nkvp + m)),
                  pl.BlockSpec(memory_space=pltpu.SMEM),
                  qblk, qblk, tab, tab],
        out_specs=(qblk, col_all, col_all, pl.BlockSpec((1, nh, LANES), lambda m, i: (m, 0, 0))),
        sem=("arbitrary", "arbitrary"), carry=carry)


def _dev_index(p):
    return 4 * p[0] + 2 * p[1] + p[2]


def _gather_plan(x_refs, out_refs, send_sems, recv_sems, local_sems):
    n = len(x_refs)
    x_, y_, c_ = lax.axis_index("x"), lax.axis_index("y"), lax.axis_index("c")
    me, sibling = (x_, y_, c_), (x_, y_, 1 - c_)
    chips = [(1 - x_, y_), (x_, 1 - y_), (1 - x_, 1 - y_)]

    def copy(t, k, block, to, src=None):
        dst = out_refs[t].at[_dev_index(block)]
        return pltpu.make_async_remote_copy(
            src_ref=dst if src is None else src, dst_ref=dst,
            send_sem=send_sems.at[7 * t + k], recv_sem=recv_sems.at[7 * t + k],
            device_id=to, device_id_type=MESH)

    mine = [pltpu.make_async_copy(x_refs[t], out_refs[t].at[_dev_index(me)], local_sems.at[t]) for t in range(n)]
    first = []
    for t in range(n):
        first.append(copy(t, 0, me, sibling, src=x_refs[t]))
        first += [copy(t, 1 + j, me, (*chip, c_), src=x_refs[t]) for j, chip in enumerate(chips)]
    arrived = lambda t, j: copy(t, 1 + j, (*chips[j], c_), me)
    forward = lambda t, j: copy(t, 4 + j, (*chips[j], c_), sibling)
    from_sibling = lambda t: copy(t, 0, sibling, me)
    forwarded = lambda t, j: copy(t, 4 + j, (*chips[j], 1 - c_), me)
    return n, mine, first, arrived, forward, from_sibling, forwarded


def _gather_start(x_refs, out_refs, send_sems, recv_sems, local_sems):
    _, mine, first, *_ = _gather_plan(x_refs, out_refs, send_sems, recv_sems, local_sems)
    for cp in mine + first:
        cp.start()


def _gather_forward(x_refs, out_refs, send_sems, recv_sems, local_sems):
    n, _, _, arrived, forward, _, _ = _gather_plan(x_refs, out_refs, send_sems, recv_sems, local_sems)
    for j in range(3):
        for t in range(n):
            arrived(t, j).wait_recv()
            forward(t, j).start()


def _gather_finish(x_refs, out_refs, send_sems, recv_sems, local_sems):
    n, mine, first, _, forward, from_sibling, forwarded = _gather_plan(
        x_refs, out_refs, send_sems, recv_sems, local_sems)
    for t in range(n):
        from_sibling(t).wait_recv()
    for j in range(3):
        for t in range(n):
            forwarded(t, j).wait_recv()
    for cp in first + [forward(t, j) for j in range(3) for t in range(n)]:
        cp.wait_send()
    for cp in mine:
        cp.wait()


def _scatter_plan(b_refs, out_refs, send_sems, recv_sems, local_sems):
    n = len(b_refs)
    x_, y_, c_ = lax.axis_index("x"), lax.axis_index("y"), lax.axis_index("c")
    my_idx = _dev_index((x_, y_, c_))
    mine = [pltpu.make_async_copy(b_refs[t].at[my_idx], out_refs[t].at[my_idx], local_sems.at[t]) for t in range(n)]
    copies = []
    for t in range(n):
        for k in range(1, N_DEV):
            peer = (x_ ^ ((k >> 2) & 1), y_ ^ ((k >> 1) & 1), c_ ^ (k & 1))
            copies.append(pltpu.make_async_remote_copy(
                src_ref=b_refs[t].at[_dev_index(peer)], dst_ref=out_refs[t].at[my_idx],
                send_sem=send_sems.at[7 * t + k - 1], recv_sem=recv_sems.at[7 * t + k - 1],
                device_id=peer, device_id_type=MESH))
    return mine, copies


def _scatter_start(b_refs, out_refs, send_sems, recv_sems, local_sems):
    mine, copies = _scatter_plan(b_refs, out_refs, send_sems, recv_sems, local_sems)
    for cp in mine + copies:
        cp.start()


def _scatter_finish(b_refs, out_refs, send_sems, recv_sems, local_sems):
    mine, copies = _scatter_plan(b_refs, out_refs, send_sems, recv_sems, local_sems)
    for cp in copies:
        cp.wait_recv()
    for cp in copies:
        cp.wait_send()
    for cp in mine:
        cp.wait()


def _exchange_operands(kind, tensors):
    if kind == "gather":
        args = list(tensors)
        shapes = [jax.ShapeDtypeStruct((N_DEV,) + t.shape, t.dtype) for t in tensors]
        return args, shapes, (_gather_start, _gather_forward, _gather_finish)
    args = [t.reshape(N_DEV, t.shape[0] // N_DEV, t.shape[1]) for t in tensors]
    shapes = [jax.ShapeDtypeStruct(a.shape, a.dtype) for a in args]
    return args, shapes, (_scatter_start, None, _scatter_finish)


def _exchange_results(kind, tensors, res):
    if kind == "gather":
        return [r.reshape(N_DEV * t.shape[0], t.shape[1]) for r, t in zip(res, tensors)]
    return list(res)


def _exchange_sems(n):
    return [pltpu.SemaphoreType.DMA((7 * n,)), pltpu.SemaphoreType.DMA((7 * n,)), pltpu.SemaphoreType.DMA((n,))]


def _exchange(kind, tensors, name):
    n = len(tensors)
    args, shapes, phases = _exchange_operands(kind, tensors)

    def body(*refs):
        for phase in phases:
            if phase is not None:
                phase(refs[:n], refs[n:2 * n], *refs[2 * n:])

    hbm = pl.BlockSpec(memory_space=pl.ANY)
    res = pl.pallas_call(body, name=name, out_shape=shapes, in_specs=[hbm] * n, out_specs=[hbm] * n,
                         scratch_shapes=_exchange_sems(n))(*args)
    return _exchange_results(kind, tensors, res)


def _pcall(body, args, *, name, out_shape, grid, in_specs, out_specs, sem, scratch_shapes=(), carry=None):
    if carry is None:
        out = pl.pallas_call(body, name=name, out_shape=out_shape, grid=grid, in_specs=list(in_specs),
                             out_specs=out_specs, scratch_shapes=list(scratch_shapes),
                             compiler_params=_params(*sem))(*args)
        return out, None
    kind, tensors = carry
    multi = isinstance(out_shape, (tuple, list))
    shapes = list(out_shape) if multi else [out_shape]
    ospecs = list(out_specs) if multi else [out_specs]
    n_in, n_out, n_scr, n_c = len(in_specs), len(shapes), len(scratch_shapes), len(tensors)
    c_args, c_shapes, (start, forward, finish) = _exchange_operands(kind, tensors)
    n_steps = 1
    for g in grid:
        n_steps *= g
    late = (3 * n_steps) // 4

    def wrapped(*refs):
        ins, rest = refs[:n_in], refs[n_in:]
        c_in, rest = rest[:n_c], rest[n_c:]
        outs, rest = rest[:n_out], rest[n_out:]
        c_out, rest = rest[:n_c], rest[n_c:]
        scr, sems = rest[:n_scr], rest[n_scr:]
        step = pl.program_id(0)
        for a in range(1, len(grid)):
            step = step * grid[a] + pl.program_id(a)

        @pl.when(step == 0)
        def _():
            start(c_in, c_out, *sems)

        body(*ins, *outs, *scr)

        if forward is not None:
            @pl.when(step == late)
            def _():
                forward(c_in, c_out, *sems)

        @pl.when(step == n_steps - 1)
        def _():
            finish(c_in, c_out, *sems)

    hbm = pl.BlockSpec(memory_space=pl.ANY)
    res = pl.pallas_call(
        wrapped, name=name, out_shape=shapes + c_shapes, grid=grid,
        in_specs=list(in_specs) + [hbm] * n_c, out_specs=ospecs + [hbm] * n_c,
        scratch_shapes=list(scratch_shapes) + _exchange_sems(n_c),
        compiler_params=_params(*sem))(*args, *c_args)
    outs = tuple(res[:n_out]) if multi else res[0]
    return outs, _exchange_results(kind, tensors, res[n_out:])


def _sum8(parts, name):
    _, R, C = parts.shape
    tr = _tile(R, 256, 16)

    def body(p_ref, g_ref):
        g = p_ref[0].astype(F32)
        for s in range(1, N_DEV):
            g = g + p_ref[s].astype(F32)
        g_ref[...] = g

    return pl.pallas_call(
        body, name=name, out_shape=jax.ShapeDtypeStruct((R, C), F32),
        grid=(R // tr,),
        in_specs=[pl.BlockSpec((N_DEV, tr, C), lambda i: (0, i, 0))],
        out_specs=pl.BlockSpec((tr, C), lambda i: (i, 0)),
        compiler_params=_params("parallel"),
    )(parts)


def _adamw(g, w, m, v, name):
    R, C = g.shape
    tr = _tile(R, 256, 8)
    c1 = 1.0 - ADAM_B1 ** ADAM_STEP
    c2 = 1.0 - ADAM_B2 ** ADAM_STEP

    def body(g_ref, w_ref, m_ref, v_ref, d_ref, nm_ref, nv_ref):
        gg = g_ref[...]
        nm = ADAM_B1 * m_ref[...] + (1.0 - ADAM_B1) * gg
        nv = ADAM_B2 * v_ref[...] + (1.0 - ADAM_B2) * (gg * gg)
        m_hat = nm / c1
        v_hat = nv / c2
        nm_ref[...] = nm
        nv_ref[...] = nv
        d_ref[...] = -ADAM_LR * (m_hat / (jnp.sqrt(v_hat) + ADAM_EPS) + ADAM_WD * w_ref[...])

    row = pl.BlockSpec((tr, C), lambda i: (i, 0))
    shp = jax.ShapeDtypeStruct((R, C), F32)
    return pl.pallas_call(
        body, name=name, out_shape=(shp, shp, shp),
        grid=(R // tr,), in_specs=[row, row, row, row], out_specs=(row, row, row),
        compiler_params=_params("parallel"),
    )(g, w, m, v)


def _ffn_down(act, wo, h, tag):
    return _mm(act, wo, NN, F32, f"{tag}_down", scale=FFN_RES_SCALE, res=h, tm=512, tn=1024, tk=2816)


def _ffn_fwd(h, g, win_t, wo, tag, carry=None):
    return _ffn_fwd_fused(h, g, win_t, wo, f"{tag}_fwd", carry=carry)


def _ffn_bwd(dh, h, g, win_t, wo, saved, tag, scatter=False, carry=None, carry_dwin=None):
    xn, silu, dsilu, up, act = saved
    dwo = _mm(act, dh, TN, BF16, f"{tag}_dwo", scale=FFN_RES_SCALE, tm=1408, tn=1024, tk=TN_CHUNK)
    if not scatter:
        (dh_in, dg, dgate, dup), got = _ffn_bwd_fused(dh, h, g, win_t, wo, silu, dsilu, up, f"{tag}_bwd", carry=carry)
        dwin_t, got_dwin = _dw_rows([dgate, dup], xn, f"{tag}_dwin", carry=carry_dwin)
        return dh_in, dg, dwin_t, dwo, got, got_dwin
    dgate, dup = _ffn_dact(dh, wo, silu, dsilu, up, f"{tag}_dact")
    dwin_t, got_wo = _dw_rows([dgate, dup], xn, f"{tag}_dwin", carry=("scatter", [dwo]))
    (dh_in, dg), got_win = _dx_norm_bwd([(dgate, win_t, NN, 2, 0), (dup, win_t, NN, 2, 1)], h, g, dh, f"{tag}_dx",
                                        carry=("scatter", [dwin_t]))
    return dh_in, dg, got_win[0], got_wo[0]


def _proj(a, w, dims, out_dtype, name, res=None):
    return _mm(a, w, dims, out_dtype, name, res=res, tm=1024, tn=1024, tk=1024)


def _proj_dw(x, dy, name):
    return _mm(x, dy, TN, BF16, name, tm=1024, tn=1024, tk=TN_CHUNK)


def kernel(x, ffn1_norm, ffn1_w_in, ffn1_w_out, mix_norm, ffn2_norm, ffn2_w_in, ffn2_w_out, sb_w_qkv, sb_w_o, kv_norm, kv_w, swa_w_q, swa_sinks, swa_w_o, final_norm, loss_target, m_ffn1_norm, m_ffn1_w_in, m_ffn1_w_out, m_mix_norm, m_ffn2_norm, m_ffn2_w_in, m_ffn2_w_out, m_sb_w_qkv, m_sb_w_o, m_kv_norm, m_kv_w, m_swa_w_q, m_swa_sinks, m_swa_w_o, m_final_norm, v_ffn1_norm, v_ffn1_w_in, v_ffn1_w_out, v_mix_norm, v_ffn2_norm, v_ffn2_w_in, v_ffn2_w_out, v_sb_w_qkv, v_sb_w_o, v_kv_norm, v_kv_w, v_swa_w_q, v_swa_sinks, v_swa_w_o, v_final_norm):
    S, D = x.shape[1], x.shape[2]
    L = ffn1_w_in.shape[0]
    KV = kv_w.shape[1]
    assert L == 2 and swa_sinks.shape == (1, 2 * SWA_Q_GROUPS * KV // (2 * LANES))

    def bf(w):
        return w.astype(BF16)

    def bft(w):
        return jnp.transpose(w).astype(BF16)

    cos_t, sin_t = _rope_tables(S)
    h0 = x.reshape(S, D)
    tgt = loss_target.reshape(S, D)

    win1a_t, = _exchange("gather", [bft(ffn1_w_in[0])], "gather_first_weight")
    sv_a1, (wo1a, wqkv_t, w_sbo) = _ffn_up(
        h0, ffn1_norm[0], win1a_t, "ffn1a_up",
        carry=("gather", [bf(ffn1_w_out[0]), bft(sb_w_qkv[0]), bf(sb_w_o[0])]))
    h1 = _ffn_down(sv_a1[-1], wo1a, h0, "ffn1a")
    hn_a, qkv, kv_t = _norm_proj(h1, mix_norm[0], wqkv_t, NT, "sb_qkv", tail_t=2 * D)
    o_sb, (win2a_t, wo2a, w_kv) = _sb_fwd(qkv, kv_t, "sb_attn", carry=("gather", [
        bft(ffn2_w_in[0]), bf(ffn2_w_out[0]), bf(kv_w)]))
    h2 = _proj(o_sb, w_sbo, NN, F32, "sb_out", res=h1)
    h3, sv_a2, (win1b_t, wo1b, w_q, w_swo) = _ffn_fwd(h2, ffn2_norm[0], win2a_t, wo2a, "ffn2a", carry=("gather", [
        bft(ffn1_w_in[1]), bf(ffn1_w_out[1]), bf(swa_w_q[0]), bf(swa_w_o[0])]))
    kvn, kv_rot = _norm_proj(h3, kv_norm, w_kv, NN, "kv_proj", rope=(cos_t, sin_t, KV // (2 * LANES)))
    h4, sv_b1, (win2b_t, wo2b) = _ffn_fwd(h3, ffn1_norm[1], win1b_t, wo1b, "ffn1b", carry=("gather", [
        bft(ffn2_w_in[1]), bf(ffn2_w_out[1])]))
    hn_b, q_rot = _norm_proj(h4, mix_norm[1], w_q, NN, "swa_q", rope=(cos_t, sin_t, D // LANES))
    o_sw = _swa_fwd(q_rot, kv_rot, swa_sinks, "swa_attn")
    h5 = _proj(o_sw, w_swo, NN, F32, "swa_out", res=h4)
    h6, sv_b2, _ = _ffn_fwd(h5, ffn2_norm[1], win2b_t, wo2b, "ffn2b")
    dh6, dg_final, sq_err = _final_loss(h6, final_norm, tgt, "final_loss")
    loss = lax.psum(0.5 * jnp.sum(sq_err) / D, ("x", "y", "c"))

    dh5, dg_f2b, dwin2b_t, dwo2b, _, _ = _ffn_bwd(dh6, h5, ffn2_norm[1], win2b_t, wo2b, sv_b2, "ffn2b")
    do_sw = _proj(dh5, w_swo, NT, BF16, "swa_out_dx")
    dw_swo = _proj_dw(o_sw, dh5, "swa_out_dw")
    (dq, dk_sw, dv_sw, dsink), (p_win2b, p_swo) = _swa_bwd(
        q_rot, kv_rot, swa_sinks, o_sw, do_sw, cos_t, sin_t, "swa_attn_bwd", carry=("scatter", [dwin2b_t, dw_swo]))
    dw_q = _proj_dw(hn_b, dq, "swa_q_dw")
    (dh4, dg_mix_b), _ = _dx_norm_bwd([(dq, w_q, NT, 1, 0)], h4, mix_norm[1], dh5, "swa_q_dx")
    dh3, dg_f1b, dwin1b_t, dwo1b, (p_q, p_wo2b), _ = _ffn_bwd(dh4, h3, ffn1_norm[1], win1b_t, wo1b, sv_b1, "ffn1b",
                                                              carry=("scatter", [dw_q, dwo2b]))
    dkv = _rotary(jnp.concatenate([dk_sw, dv_sw], axis=1), cos_t, sin_t, KV // (2 * LANES), True, "kv_rope_bwd")
    dw_kv = _proj_dw(kvn, dkv, "kv_proj_dw")
    (dh3, dg_kv), _ = _dx_norm_bwd([(dkv, w_kv, NT, 1, 0)], h3, kv_norm, dh3, "kv_proj_dx")
    dh2, dg_f2a, dwin2a_t, dwo2a, (p_win1b, p_kv), (p_wo1b,) = _ffn_bwd(
        dh3, h2, ffn2_norm[0], win2a_t, wo2a, sv_a2, "ffn2a",
        carry=("scatter", [dwin1b_t, dw_kv]), carry_dwin=("scatter", [dwo1b]))
    do_sb = _proj(dh2, w_sbo, NT, BF16, "sb_out_dx")
    dw_sbo = _proj_dw(o_sb, dh2, "sb_out_dw")
    (dq_sb, dk_sb, dv_sb), (p_win2a, p_wo2a, p_sbo) = _sb_bwd(
        qkv, kv_t, o_sb, do_sb, "sb_attn_bwd", carry=("scatter", [dwin2a_t, dwo2a, dw_sbo]))
    dqkv = [dq_sb, dk_sb, dv_sb]
    dwqkv_t, _ = _dw_rows(dqkv, hn_a, "sb_qkv_dw", tk=TN_CHUNK // 2)
    (dh1, dg_mix_a), (p_qkv,) = _dx_norm_bwd([(dy, wqkv_t, NN, 3, n) for n, dy in enumerate(dqkv)], h1, mix_norm[0],
                                             dh2, "sb_qkv_dx", carry=("scatter", [dwqkv_t]))
    dx, dg_f1a, p_win1a, p_wo1a = _ffn_bwd(dh1, h0, ffn1_norm[0], win1a_t, wo1a, sv_a1, "ffn1a", scatter=True)

    def natural(parts, tag):
        return _sum8(parts, f"sum_{tag}")

    def from_t(parts, tag):
        return jnp.transpose(_sum8(parts, f"sum_{tag}"))

    grads = {
        "ffn1_w_in": jnp.stack([from_t(p_win1a, "win1a"), from_t(p_win1b, "win1b")]),
        "ffn1_w_out": jnp.stack([natural(p_wo1a, "wo1a"), natural(p_wo1b, "wo1b")]),
        "ffn2_w_in": jnp.stack([from_t(p_win2a, "win2a"), from_t(p_win2b, "win2b")]),
        "ffn2_w_out": jnp.stack([natural(p_wo2a, "wo2a"), natural(p_wo2b, "wo2b")]),
        "sb_w_qkv": from_t(p_qkv, "qkv")[None],
        "sb_w_o": natural(p_sbo, "sbo")[None],
        "kv_w": natural(p_kv, "kv"),
        "swa_w_q": natural(p_q, "swq")[None],
        "swa_w_o": natural(p_swo, "swo")[None],
    }

    small_w = [ffn1_norm, mix_norm, ffn2_norm, kv_norm, final_norm, swa_sinks]
    small_m = [m_ffn1_norm, m_mix_norm, m_ffn2_norm, m_kv_norm, m_final_norm, m_swa_sinks]
    small_v = [v_ffn1_norm, v_mix_norm, v_ffn2_norm, v_kv_norm, v_final_norm, v_swa_sinks]
    SMALL_ROWS = 16

    def pack_small(ts):
        rows_ = [t.reshape(-1, D) for t in ts[:-1]]
        sink_row = jnp.pad(ts[-1].reshape(1, -1), ((0, 0), (0, D - ts[-1].size)))
        flat = jnp.concatenate(rows_ + [sink_row], axis=0)
        return jnp.pad(flat, ((0, SMALL_ROWS - flat.shape[0]), (0, 0)))

    def unpack_small(flat):
        out, r = [], 0
        for t in small_w[:-1]:
            n = t.size // D
            out.append(flat[r:r + n].reshape(t.shape))
            r += n
        out.append(flat[r, :swa_sinks.size].reshape(swa_sinks.shape))
        return out

    def gain(parts8):
        return jnp.sum(parts8, axis=0, keepdims=True)

    g_small_local = pack_small([
        jnp.concatenate([gain(dg_f1a), gain(dg_f1b)], axis=0),
        jnp.concatenate([gain(dg_mix_a), gain(dg_mix_b)], axis=0),
        jnp.concatenate([gain(dg_f2a), gain(dg_f2b)], axis=0),
        gain(dg_kv), gain(dg_final), dsink[:, :, 0].reshape(1, -1)])
    small_parts = _exchange("gather", [g_small_local], "gather_small_grads")[0]
    g_small = _sum8(small_parts.reshape(N_DEV, SMALL_ROWS, D), "sum_small")
    d_small, nm_small, nv_small = _adamw(g_small, pack_small(small_w), pack_small(small_m), pack_small(small_v), "adamw_small")
    small_names = ["ffn1_norm", "mix_norm", "ffn2_norm", "kv_norm", "final_norm", "swa_sinks"]
    result = {"grad": dict(zip(small_names, unpack_small(g_small))),
              "delta": dict(zip(small_names, unpack_small(d_small))),
              "new_m": dict(zip(small_names, unpack_small(nm_small))),
              "new_v": dict(zip(small_names, unpack_small(nv_small)))}

    big = {"ffn1_w_in": (ffn1_w_in, m_ffn1_w_in, v_ffn1_w_in), "ffn1_w_out": (ffn1_w_out, m_ffn1_w_out, v_ffn1_w_out),
           "ffn2_w_in": (ffn2_w_in, m_ffn2_w_in, v_ffn2_w_in), "ffn2_w_out": (ffn2_w_out, m_ffn2_w_out, v_ffn2_w_out),
           "sb_w_qkv": (sb_w_qkv, m_sb_w_qkv, v_sb_w_qkv), "sb_w_o": (sb_w_o, m_sb_w_o, v_sb_w_o),
           "kv_w": (kv_w, m_kv_w, v_kv_w), "swa_w_q": (swa_w_q, m_swa_w_q, v_swa_w_q),
           "swa_w_o": (swa_w_o, m_swa_w_o, v_swa_w_o)}
    for nm, (w, m, v) in big.items():
        g = grads[nm]
        two_d = lambda t: t.reshape(-1, t.shape[-1])
        d, new_m, new_v = _adamw(two_d(g), two_d(w), two_d(m), two_d(v), f"adamw_{nm}")
        result["grad"][nm] = g
        result["delta"][nm] = d.reshape(w.shape)
        result["new_m"][nm] = new_m.reshape(w.shape)
        result["new_v"][nm] = new_v.reshape(w.shape)

    order = ["ffn1_norm", "ffn1_w_in", "ffn1_w_out", "mix_norm", "ffn2_norm", "ffn2_w_in", "ffn2_w_out",
             "sb_w_qkv", "sb_w_o", "kv_norm", "kv_w", "swa_w_q", "swa_sinks", "swa_w_o", "final_norm"]
    outs = [result[kind][nm] for kind in ("grad", "delta", "new_m", "new_v") for nm in order]
    return (loss, dx.reshape(x.shape), *outs)
```

```python
import jax
import jax.numpy as jnp
from jax import lax
from jax.experimental import pallas as pl
from jax.experimental.pallas import tpu as pltpu

F32 = jnp.float32
BF16 = jnp.bfloat16

N_DEV = 8
HEAD_DIM = 64
LANES = 128
BLK = 128
RMS_EPS = 1e-6
FFN_RES_SCALE = 0.5
ROPE_THETA = 10000.0
ATTN_SCALE = HEAD_DIM ** -0.5
SB_LOG_FLOOR = -88.0
NEG_BIG = -1e30
VMEM_LIMIT_V7X = 56 * 1024 * 1024

ADAM_LR = 0.001
ADAM_B1 = 0.9
ADAM_B2 = 0.999
ADAM_EPS = 1e-08
ADAM_WD = 0.01
ADAM_STEP = 10

NN = ((1,), (0,))
NT = ((1,), (1,))
TN = ((0,), (0,))
TN_CHUNK = 2048
MESH = pl.DeviceIdType.MESH


def _dot(a, b, dims):
    return lax.dot_general(a, b, (dims, ((), ())), preferred_element_type=F32)


def _tile(n, pref, mult=LANES):
    if n <= pref:
        return n
    t = (pref // mult) * mult
    while t >= mult:
        if n % t == 0:
            return t
        t -= mult
    return n


def _params(*sem):
    return pltpu.CompilerParams(dimension_semantics=sem, vmem_limit_bytes=VMEM_LIMIT_V7X)


def _mm(a, b, dims, out_dtype, name, scale=1.0, res=None, tm=512, tn=512, tk=512):
    if dims == NN:
        (M, K), (_, N) = a.shape, b.shape
    elif dims == NT:
        (M, K), (N, _) = a.shape, b.shape
    else:
        (K, M), (_, N) = a.shape, b.shape
    tm, tn, tk = _tile(M, tm), _tile(N, tn), _tile(K, tk)
    nk = K // tk
    if dims == TN:
        a_spec = pl.BlockSpec((tk, tm), lambda i, j, k: (k, i))
    else:
        a_spec = pl.BlockSpec((tm, tk), lambda i, j, k: (i, k))
    if dims == NT:
        b_spec = pl.BlockSpec((tn, tk), lambda i, j, k: (j, k))
    else:
        b_spec = pl.BlockSpec((tk, tn), lambda i, j, k: (k, j))
    o_spec = pl.BlockSpec((tm, tn), lambda i, j, k: (i, j))
    has_res = res is not None

    def body(*refs):
        a_ref, b_ref = refs[0], refs[1]
        r_ref = refs[2] if has_res else None
        o_ref = refs[3] if has_res else refs[2]

        def finish(acc):
            r = acc * scale if scale != 1.0 else acc
            if has_res:
                r = r + r_ref[...]
            o_ref[...] = r.astype(out_dtype)

        p = _dot(a_ref[...].astype(BF16), b_ref[...].astype(BF16), dims)
        if nk == 1:
            finish(p)
        else:
            acc_ref = refs[-1]
            k = pl.program_id(2)

            @pl.when(k == 0)
            def _():
                acc_ref[...] = p

            @pl.when(k > 0)
            def _():
                acc_ref[...] += p

            @pl.when(k == nk - 1)
            def _():
                finish(acc_ref[...])

    in_specs = [a_spec, b_spec] + ([o_spec] if has_res else [])
    args = (a, b) + ((res,) if has_res else ())
    return pl.pallas_call(
        body, name=name,
        out_shape=jax.ShapeDtypeStruct((M, N), out_dtype),
        grid=(M // tm, N // tn, nk),
        in_specs=in_specs, out_specs=o_spec,
        scratch_shapes=[pltpu.VMEM((tm, tn), F32)] if nk > 1 else [],
        compiler_params=_params("parallel", "parallel", "arbitrary"),
    )(*args)


def _rows8(x):
    r, d = x.shape
    return jnp.sum(x.reshape(r // 8, 8, d), axis=0)


def _norm_proj(h, g, w, dims, name, rope=None, tail_t=0):
    S, D = h.shape
    N = w.shape[1] if dims == NN else w.shape[0]
    tm = _tile(S, 512, 16)

    def body(h_ref, g_ref, w_ref, *rest):
        xn_ref, y_ref = rest[-3:-1] if tail_t else rest[-2:]
        x = h_ref[...]
        r = lax.rsqrt(jnp.mean(x * x, axis=-1, keepdims=True) + RMS_EPS)
        xn = ((x * r) * g_ref[...]).astype(BF16)
        xn_ref[...] = xn
        y = _dot(xn, w_ref[...], dims)
        if rope is not None:
            cs, sn = rest[0][...], rest[1][...]
            groups = [y[:, gidx * LANES:(gidx + 1) * LANES] for gidx in range(N // LANES)]
            y = jnp.concatenate([v * cs + _swap_halves(v) * sn if gidx < rope[2] else v
                                 for gidx, v in enumerate(groups)], axis=1)
        y_ref[...] = y.astype(BF16)
        if tail_t:
            rest[-1][...] = jnp.transpose(y[:, N - tail_t:]).astype(BF16)

    row = pl.BlockSpec((tm, D), lambda i: (i, 0))
    tab = pl.BlockSpec((tm, LANES), lambda i: (i, 0))
    in_specs = [row, pl.BlockSpec((1, D), lambda i: (0, 0)), pl.BlockSpec(w.shape, lambda i: (0, 0))]
    args = (h, g.reshape(1, D), w)
    if rope is not None:
        in_specs += [tab, tab]
        args += (rope[0], rope[1])
    out_shape = [jax.ShapeDtypeStruct((S, D), BF16), jax.ShapeDtypeStruct((S, N), BF16)]
    out_specs = [row, pl.BlockSpec((tm, N), lambda i: (i, 0))]
    if tail_t:
        out_shape.append(jax.ShapeDtypeStruct((tail_t, S), BF16))
        out_specs.append(pl.BlockSpec((tail_t, tm), lambda i: (0, i)))
    return pl.pallas_call(
        body, name=name, out_shape=out_shape, grid=(S // tm,),
        in_specs=in_specs, out_specs=out_specs,
        compiler_params=_params("parallel"),
    )(*args)


def _final_loss(h, g, tgt, name):
    S, D = h.shape
    ts = _tile(S, 512, 8)

    def body(h_ref, g_ref, t_ref, dh_ref, dg_ref, l_ref):
        x = h_ref[...]
        r = lax.rsqrt(jnp.mean(x * x, axis=-1, keepdims=True) + RMS_EPS)
        xhat = x * r
        err = xhat * g_ref[...] - t_ref[...]
        d = err * (1.0 / D)
        dxh = d * g_ref[...]
        c = jnp.mean(dxh * xhat, axis=-1, keepdims=True)
        dh_ref[...] = r * (dxh - xhat * c)
        part = _rows8(d * xhat)
        lpart = _rows8(err * err)

        @pl.when(pl.program_id(0) == 0)
        def _():
            dg_ref[...] = part
            l_ref[...] = lpart

        @pl.when(pl.program_id(0) > 0)
        def _():
            dg_ref[...] += part
            l_ref[...] += lpart

    row = pl.BlockSpec((ts, D), lambda i: (i, 0))
    acc = pl.BlockSpec((8, D), lambda i: (0, 0))
    return pl.pallas_call(
        body, name=name,
        out_shape=(jax.ShapeDtypeStruct((S, D), F32), jax.ShapeDtypeStruct((8, D), F32),
                   jax.ShapeDtypeStruct((8, D), F32)),
        grid=(S // ts,),
        in_specs=[row, pl.BlockSpec((1, D), lambda i: (0, 0)), row],
        out_specs=(row, acc, acc),
        compiler_params=_params("arbitrary"),
    )(h, g.reshape(1, D), tgt)


def _ffn_up(h, g, win_t, name, carry=None):
    S, D = h.shape
    F = win_t.shape[0] // 2
    tm, tn = _tile(S, 512, 16), _tile(F, 1408)
    nf = F // tn

    def body(h_ref, g_ref, wg_ref, wu_ref, xn_ref, silu_ref, dsilu_ref, up_ref, act_ref):
        x = h_ref[...]
        r = lax.rsqrt(jnp.mean(x * x, axis=-1, keepdims=True) + RMS_EPS)
        xn = ((x * r) * g_ref[...]).astype(BF16)
        xn_ref[...] = xn
        gate = _dot(xn, wg_ref[...], NT)
        up = _dot(xn, wu_ref[...], NT)
        sig = 1.0 / (1.0 + jnp.exp(-gate))
        silu = gate * sig
        up_ref[...] = up.astype(BF16)
        silu_ref[...] = silu.astype(BF16)
        dsilu_ref[...] = (sig + silu * (1.0 - sig)).astype(BF16)
        act_ref[...] = (silu * up).astype(BF16)

    row = pl.BlockSpec((tm, D), lambda i, j: (i, 0))
    blk = pl.BlockSpec((tm, tn), lambda i, j: (i, j))
    hid = jax.ShapeDtypeStruct((S, F), BF16)
    return _pcall(
        body, (h, g.reshape(1, D), win_t, win_t), name=name,
        out_shape=(jax.ShapeDtypeStruct((S, D), BF16), hid, hid, hid, hid),
        grid=(S // tm, nf),
        in_specs=[row, pl.BlockSpec((1, D), lambda i, j: (0, 0)),
                  pl.BlockSpec((tn, D), lambda i, j: (j, 0)),
                  pl.BlockSpec((tn, D), lambda i, j: (j + nf, 0))],
        out_specs=(row, blk, blk, blk, blk),
        sem=("arbitrary", "arbitrary"), carry=carry)


def _ffn_dact(dh, wo, silu, dsilu, up, name):
    S, D = dh.shape
    F = wo.shape[0]
    tm, tn = _tile(S, 512, 16), _tile(F, 1408)

    def body(dh_ref, wo_ref, s_ref, ds_ref, u_ref, dg_ref, du_ref):
        d = _dot(dh_ref[...].astype(BF16), wo_ref[...], NT) * FFN_RES_SCALE
        du_ref[...] = (d * s_ref[...].astype(F32)).astype(BF16)
        dg_ref[...] = (d * u_ref[...].astype(F32) * ds_ref[...].astype(F32)).astype(BF16)

    blk = pl.BlockSpec((tm, tn), lambda j, i: (i, j))
    hid = jax.ShapeDtypeStruct((S, F), BF16)
    return pl.pallas_call(
        body, name=name, out_shape=(hid, hid),
        grid=(F // tn, S // tm),
        in_specs=[pl.BlockSpec((tm, D), lambda j, i: (i, 0)), pl.BlockSpec((tn, D), lambda j, i: (j, 0)),
                  blk, blk, blk],
        out_specs=(blk, blk),
        compiler_params=_params("arbitrary", "arbitrary"),
    )(dh, wo, silu, dsilu, up)


def _dw_rows(srcs, x, name, carry=None, tk=TN_CHUNK):
    n = len(srcs)
    S, F = srcs[0].shape
    D = x.shape[1]
    tr, tk = _tile(F, 1408), _tile(S, tk, 16)
    nf, nk = F // tr, S // tk

    def body(*refs):
        src_refs, (x_ref, o_ref, acc_ref) = refs[:n], refs[n:]
        r, k = pl.program_id(0), pl.program_id(1)
        for s in range(n):
            @pl.when(r // nf == s)
            def _():
                p = _dot(src_refs[s][...].astype(BF16), x_ref[...], TN)

                @pl.when(k == 0)
                def _():
                    acc_ref[...] = p

                @pl.when(k > 0)
                def _():
                    acc_ref[...] += p

        @pl.when(k == nk - 1)
        def _():
            o_ref[...] = acc_ref[...].astype(BF16)

    def src_spec(s):
        return pl.BlockSpec((tk, tr), lambda r, k: (jnp.where(r // nf == s, k, 0), jnp.clip(r - s * nf, 0, nf - 1)))

    return _pcall(
        body, (*srcs, x), name=name, out_shape=jax.ShapeDtypeStruct((n * F, D), BF16),
        grid=(n * nf, nk),
        in_specs=[src_spec(s) for s in range(n)] + [pl.BlockSpec((tk, D), lambda r, k: (k, 0))],
        out_specs=pl.BlockSpec((tr, D), lambda r, k: (r, 0)),
        scratch_shapes=[pltpu.VMEM((tr, D), F32)],
        sem=("arbitrary", "arbitrary"), carry=carry)


def _dx_norm_bwd(terms, h, g, res, name, carry=None):
    S, D = h.shape
    tm = _tile(S, 256, 16)
    n = len(terms)

    def body(*refs):
        dy_refs, w_refs = refs[:n], refs[n:2 * n]
        h_ref, g_ref, r_ref, dh_ref, dg_ref = refs[2 * n:]
        d = _dot(dy_refs[0][...].astype(BF16), w_refs[0][...], terms[0][2])
        for t in range(1, n):
            d = d + _dot(dy_refs[t][...].astype(BF16), w_refs[t][...], terms[t][2])
        x = h_ref[...]
        r = lax.rsqrt(jnp.mean(x * x, axis=-1, keepdims=True) + RMS_EPS)
        xhat = x * r
        dxh = d * g_ref[...]
        c = jnp.mean(dxh * xhat, axis=-1, keepdims=True)
        dh_ref[...] = r * (dxh - xhat * c) + r_ref[...]
        part = _rows8(d * xhat)

        @pl.when(pl.program_id(0) == 0)
        def _():
            dg_ref[...] = part

        @pl.when(pl.program_id(0) > 0)
        def _():
            dg_ref[...] += part

    def w_spec(w, nblk, blk):
        return pl.BlockSpec((w.shape[0] // nblk, w.shape[1]), lambda i: (blk, 0))

    row = pl.BlockSpec((tm, D), lambda i: (i, 0))
    in_specs = [pl.BlockSpec((tm, t[0].shape[1]), lambda i: (i, 0)) for t in terms]
    in_specs += [w_spec(t[1], t[3], t[4]) for t in terms]
    in_specs += [row, pl.BlockSpec((1, D), lambda i: (0, 0)), row]
    return _pcall(
        body, (*[t[0] for t in terms], *[t[1] for t in terms], h, g.reshape(1, D), res), name=name,
        out_shape=(jax.ShapeDtypeStruct((S, D), F32), jax.ShapeDtypeStruct((8, D), F32)),
        grid=(S // tm,),
        in_specs=in_specs,
        out_specs=(row, pl.BlockSpec((8, D), lambda i: (0, 0))),
        sem=("arbitrary",), carry=carry)


def _load_resident(pairs, sems):
    @pl.when(pl.program_id(0) == 0)
    def _():
        copies = [pltpu.make_async_copy(src, dst, sems.at[n]) for n, (src, dst) in enumerate(pairs)]
        for cp in copies:
            cp.start()
        for cp in copies:
            cp.wait()


def _ffn_fwd_fused(h, g, win_t, wo, name, carry=None):
    S, D = h.shape
    F = wo.shape[0]
    tm = _tile(S, 256, 16)

    def body(h_ref, g_ref, win_hbm, wo_hbm, out_ref, xn_ref, silu_ref, dsilu_ref, up_ref, act_ref, win_v, wo_v, sems):
        _load_resident([(win_hbm, win_v), (wo_hbm, wo_v)], sems)
        x = h_ref[...]
        r = lax.rsqrt(jnp.mean(x * x, axis=-1, keepdims=True) + RMS_EPS)
        xn = ((x * r) * g_ref[...]).astype(BF16)
        xn_ref[...] = xn
        gate = _dot(xn, win_v[:F, :], NT)
        up = _dot(xn, win_v[F:, :], NT)
        sig = 1.0 / (1.0 + jnp.exp(-gate))
        silu = gate * sig
        act = (silu * up).astype(BF16)
        up_ref[...] = up.astype(BF16)
        silu_ref[...] = silu.astype(BF16)
        dsilu_ref[...] = (sig + silu * (1.0 - sig)).astype(BF16)
        act_ref[...] = act
        out_ref[...] = x + FFN_RES_SCALE * _dot(act, wo_v[...], NN)

    row = pl.BlockSpec((tm, D), lambda i: (i, 0))
    wide = pl.BlockSpec((tm, F), lambda i: (i, 0))
    hbm = pl.BlockSpec(memory_space=pl.ANY)
    hid = jax.ShapeDtypeStruct((S, F), BF16)
    res, got = _pcall(
        body, (h, g.reshape(1, D), win_t, wo), name=name,
        out_shape=(jax.ShapeDtypeStruct((S, D), F32), jax.ShapeDtypeStruct((S, D), BF16), hid, hid, hid, hid),
        grid=(S // tm,),
        in_specs=[row, pl.BlockSpec((1, D), lambda i: (0, 0)), hbm, hbm],
        out_specs=(row, row, wide, wide, wide, wide),
        scratch_shapes=[pltpu.VMEM(win_t.shape, BF16), pltpu.VMEM(wo.shape, BF16), pltpu.SemaphoreType.DMA((2,))],
        sem=("arbitrary",), carry=carry)
    return res[0], tuple(res[1:]), got


def _ffn_bwd_fused(dh, h, g, win_t, wo, silu, dsilu, up, name, carry=None):
    S, D = h.shape
    F = wo.shape[0]
    tm = _tile(S, 256, 16)

    def body(dh_ref, h_ref, g_ref, s_ref, ds_ref, u_ref, win_hbm, wo_hbm,
             dhin_ref, dgain_ref, dgate_ref, dup_ref, win_v, wo_v, sems):
        _load_resident([(win_hbm, win_v), (wo_hbm, wo_v)], sems)
        dhv = dh_ref[...]
        d = _dot(dhv.astype(BF16), wo_v[...], NT) * FFN_RES_SCALE
        dup = (d * s_ref[...].astype(F32)).astype(BF16)
        dgate = (d * u_ref[...].astype(F32) * ds_ref[...].astype(F32)).astype(BF16)
        dup_ref[...] = dup
        dgate_ref[...] = dgate
        dxn = _dot(dgate, win_v[:F, :], NN) + _dot(dup, win_v[F:, :], NN)
        x = h_ref[...]
        r = lax.rsqrt(jnp.mean(x * x, axis=-1, keepdims=True) + RMS_EPS)
        xhat = x * r
        dxh = dxn * g_ref[...]
        c = jnp.mean(dxh * xhat, axis=-1, keepdims=True)
        dhin_ref[...] = r * (dxh - xhat * c) + dhv
        part = _rows8(dxn * xhat)

        @pl.when(pl.program_id(0) == 0)
        def _():
            dgain_ref[...] = part

        @pl.when(pl.program_id(0) > 0)
        def _():
            dgain_ref[...] += part

    row = pl.BlockSpec((tm, D), lambda i: (i, 0))
    wide = pl.BlockSpec((tm, F), lambda i: (i, 0))
    hbm = pl.BlockSpec(memory_space=pl.ANY)
    hid = jax.ShapeDtypeStruct((S, F), BF16)
    return _pcall(
        body, (dh, h, g.reshape(1, D), silu, dsilu, up, win_t, wo), name=name,
        out_shape=(jax.ShapeDtypeStruct((S, D), F32), jax.ShapeDtypeStruct((8, D), F32), hid, hid),
        grid=(S // tm,),
        in_specs=[row, row, pl.BlockSpec((1, D), lambda i: (0, 0)), wide, wide, wide, hbm, hbm],
        out_specs=(row, pl.BlockSpec((8, D), lambda i: (0, 0)), wide, wide),
        scratch_shapes=[pltpu.VMEM(win_t.shape, BF16), pltpu.VMEM(wo.shape, BF16), pltpu.SemaphoreType.DMA((2,))],
        sem=("arbitrary",), carry=carry)


def _rope_tables(S):
    half = HEAD_DIM // 2
    inv_freq = ROPE_THETA ** (-jnp.arange(half, dtype=F32) / half)
    ang = jnp.arange(S).astype(F32)[:, None] * inv_freq[None, :]
    cos, sin = jnp.cos(ang), jnp.sin(ang)
    cos_t = jnp.tile(cos, (1, LANES // half))
    sin_t = jnp.tile(jnp.concatenate([-sin, sin], axis=1), (1, LANES // HEAD_DIM))
    return cos_t, sin_t


def _swap_halves(x):
    lane = lax.broadcasted_iota(jnp.int32, x.shape, 1)
    first = (lane % HEAD_DIM) < (HEAD_DIM // 2)
    return jnp.where(first, pltpu.roll(x, LANES - HEAD_DIM // 2, 1), pltpu.roll(x, HEAD_DIM // 2, 1))


def _rotary(x, cos_t, sin_t, n_rot, inverse, name):
    S, C = x.shape
    ts = _tile(S, 512, 16)
    ng = C // LANES

    def body(x_ref, c_ref, s_ref, o_ref):
        cs, sn = c_ref[...], s_ref[...]
        for gidx in range(ng):
            sl = slice(gidx * LANES, (gidx + 1) * LANES)
            v = x_ref[:, sl].astype(F32)
            if gidx < n_rot:
                if inverse:
                    v = v * cs + _swap_halves(v * sn)
                else:
                    v = v * cs + _swap_halves(v) * sn
            o_ref[:, sl] = v.astype(BF16)

    row = pl.BlockSpec((ts, C), lambda i: (i, 0))
    tab = pl.BlockSpec((ts, LANES), lambda i: (i, 0))
    return pl.pallas_call(
        body, name=name, out_shape=jax.ShapeDtypeStruct((S, C), BF16),
        grid=(S // ts,), in_specs=[row, tab, tab], out_specs=row,
        compiler_params=_params("parallel"),
    )(x, cos_t, sin_t)


def _head_masks():
    lane = lax.broadcasted_iota(jnp.int32, (BLK, LANES), 1)
    return lane < HEAD_DIM


def _split_bf16(x):
    hi = x.astype(BF16)
    lo = (x - hi.astype(F32)).astype(BF16)
    return hi, lo


def _sb_scores(qh, ks, carry, diag, tri_excl, strict):
    n_heads = len(qh)
    zs = [_dot(ks[n], qh[n], NT) for n in range(n_heads)]
    a_l, b_l, split_l = [], [], []
    for z in zs:
        a = jnp.minimum(z, 0.0) - jnp.log(1.0 + jnp.exp(-jnp.abs(z)))
        b = a - z
        if diag:
            b = jnp.where(strict, b, 0.0)
        a_l.append(a)
        b_l.append(b)
        split_l.append(_split_bf16(b))
    sufs = [_dot(tri_excl, hi, NN) + _dot(tri_excl, lo, NN) for hi, lo in split_l]
    w_l = []
    for n in range(n_heads):
        w = jnp.exp(a_l[n] + sufs[n] + carry[n])
        if diag:
            w = jnp.where(strict, w, 0.0)
        w_l.append(w)
    return a_l, b_l, w_l


SB_FWD_PAIRS = 4
SB_FWD_QBLOCKS = 2
SB_BWD_PAIRS = 2
SB_BWD_QBLOCKS = 4


def _any_alive(carries):
    top = carries[0]
    for c in carries[1:]:
        top = jnp.maximum(top, c)
    return (jnp.max(top) > SB_LOG_FLOOR).astype(jnp.int32)


def _sb_masks():
    row = lax.broadcasted_iota(jnp.int32, (BLK, BLK), 0)
    col = lax.broadcasted_iota(jnp.int32, (BLK, BLK), 1)
    tri_excl = jnp.where(col > row, 1.0, 0.0).astype(BF16)
    tri_incl = jnp.where(col >= row, 1.0, 0.0).astype(BF16)
    return row < HEAD_DIM, row < col, tri_excl, tri_incl


def _sb_fwd(qkv, kv_t, name, carry=None):
    S, D3 = qkv.shape
    D = D3 // 3
    npair, nb = D // LANES, S // BLK
    P = min(SB_FWD_PAIRS, npair)
    ngroup = npair // P
    W = P * LANES

    QB = SB_FWD_QBLOCKS if nb % SB_FWD_QBLOCKS == 0 else 1
    nch = QB * 2 * P

    def body(q_ref, k_ref, vt_ref, o_ref):
        i_first = pl.program_id(1) * QB
        m0 = _head_masks()
        top, strict, tri_excl, _ = _sb_masks()
        zq = jnp.zeros((BLK, LANES), BF16)
        lanes = [slice(p * LANES, (p + 1) * LANES) for p in range(P)]
        qh = []
        for qb in range(QB):
            for sl in lanes:
                q2 = q_ref[qb * BLK:(qb + 1) * BLK, sl] * ATTN_SCALE
                qh += [jnp.where(m0, q2, zq), jnp.where(m0, zq, q2)]

        def block(js, carry, acc, diag):
            offs = [pl.multiple_of(j * BLK, BLK) for j in js]
            ks, vth = [], []
            for qb in range(QB):
                for sl in lanes:
                    k2 = k_ref[pl.ds(offs[qb], BLK), sl]
                    vt = vt_ref[sl, pl.ds(offs[qb], BLK)]
                    ks += [k2, k2]
                    vth += [jnp.where(top, vt, zq), jnp.where(top, zq, vt)]
            _, b_l, w_l = _sb_scores(qh, ks, carry, diag, tri_excl, strict)
            wb = [w.astype(BF16) for w in w_l]
            new_acc = [acc[m] + _dot(vth[2 * m], wb[2 * m], NN) + _dot(vth[2 * m + 1], wb[2 * m + 1], NN)
                       for m in range(QB * P)]
            new_carry = [carry[n] + jnp.sum(b_l[n], axis=0, keepdims=True) for n in range(nch)]
            return new_carry, new_acc

        c0 = jnp.zeros((1, BLK), F32)
        carry, acc = block([i_first + qb for qb in range(QB)], [c0] * nch,
                           [jnp.zeros((LANES, BLK), F32)] * (QB * P), True)

        def cond(st):
            return jnp.logical_and(i_first + QB - 1 - st[0] >= 0, st[1] > 0)

        def step(st):
            t, _, carry, acc = st
            js = [i_first + qb - t for qb in range(QB)]
            carry = [carry[n] if n // (2 * P) == QB - 1 else jnp.where(js[n // (2 * P)] >= 0, carry[n], NEG_BIG)
                     for n in range(nch)]
            carry, acc = block([jnp.maximum(j, 0) for j in js], carry, acc, False)
            return t + 1, _any_alive(carry), carry, acc

        st = lax.while_loop(cond, step, (1, _any_alive(carry), carry, acc))
        for qb in range(QB):
            for p, sl in enumerate(lanes):
                o_ref[qb * BLK:(qb + 1) * BLK, sl] = jnp.transpose(st[3][qb * P + p])

    return _pcall(
        body, (qkv, qkv, kv_t), name=name, out_shape=jax.ShapeDtypeStruct((S, D), F32),
        grid=(ngroup, nb // QB),
        in_specs=[pl.BlockSpec((QB * BLK, W), lambda g, i: (i, g)),
                  pl.BlockSpec((S, W), lambda g, i: (0, ngroup + g)),
                  pl.BlockSpec((W, S), lambda g, i: (ngroup + g, 0))],
        out_specs=pl.BlockSpec((QB * BLK, W), lambda g, i: (i, g)),
        sem=("arbitrary", "arbitrary"), carry=carry)


def _sb_bwd(qkv, kv_t, o, do, name, carry=None):
    S, D3 = qkv.shape
    D = D3 // 3
    npair, nb = D // LANES, S // BLK
    P = min(SB_BWD_PAIRS, npair)
    ngroup = npair // P
    W = P * LANES

    QB = SB_BWD_QBLOCKS if nb % SB_BWD_QBLOCKS == 0 else 1
    nch = QB * 2 * P

    def body(q_ref, o_ref, do_ref, qkv_hbm, kt_hbm, dq_ref, dk_ref, dv_ref, k_ref, v_ref, kt_ref, sems):
        grp = pl.program_id(0)
        i_first = pl.program_id(1) * QB
        m0 = _head_masks()
        top, strict, tri_excl, tri_incl = _sb_masks()
        zq = jnp.zeros((BLK, LANES), BF16)
        lanes = [slice(p * LANES, (p + 1) * LANES) for p in range(P)]

        @pl.when(pl.program_id(1) == 0)
        def _():
            copies = [pltpu.make_async_copy(qkv_hbm.at[:, pl.ds(pl.multiple_of((c * ngroup + grp) * W, LANES), W)],
                                            ref, sems.at[c - 1]) for c, ref in ((1, k_ref), (2, v_ref))]
            copies.append(pltpu.make_async_copy(kt_hbm.at[pl.ds(pl.multiple_of(grp * W, LANES), W), :],
                                                kt_ref, sems.at[2]))
            for cp in copies:
                cp.start()
            dk_ref[...] = jnp.zeros_like(dk_ref)
            dv_ref[...] = jnp.zeros_like(dv_ref)
            for cp in copies:
                cp.wait()

        qh, doh, delta = [], [], []
        for qb in range(QB):
            rs = slice(qb * BLK, (qb + 1) * BLK)
            for sl in lanes:
                q2, do2 = q_ref[rs, sl] * ATTN_SCALE, do_ref[rs, sl]
                qh += [jnp.where(m0, q2, zq), jnp.where(m0, zq, q2)]
                doh += [jnp.where(m0, do2, zq), jnp.where(m0, zq, do2)]
                prod_t = jnp.transpose(do2.astype(F32) * o_ref[rs, sl])
                delta += [jnp.sum(jnp.where(top, prod_t, 0.0), axis=0, keepdims=True),
                          jnp.sum(jnp.where(top, 0.0, prod_t), axis=0, keepdims=True)]

        def block(js, valid, cb, cg, dq, diag):
            offs = [pl.multiple_of(j * BLK, BLK) for j in js]
            ks, vs, kth = [], [], []
            for qb in range(QB):
                for sl in lanes:
                    k2, v2 = k_ref[pl.ds(offs[qb], BLK), sl], v_ref[pl.ds(offs[qb], BLK), sl]
                    ks += [k2, k2]
                    vs += [v2, v2]
                    kt = kt_ref[sl, pl.ds(offs[qb], BLK)] * ATTN_SCALE
                    kth += [jnp.where(top, kt, zq), jnp.where(top, zq, kt)]
            dws = [_dot(vs[n], doh[n], NT) for n in range(nch)]
            a_l, b_l, w_l = _sb_scores(qh, ks, cb, diag, tri_excl, strict)
            wb = [w.astype(BF16) for w in w_l]
            g_l = [dws[n] * wb[n].astype(F32) for n in range(nch)]
            gsplit = [_split_bf16(g) for g in g_l]
            gincs = [_dot(tri_incl, hi, NN) + _dot(tri_incl, lo, NN) for hi, lo in gsplit]
            dzs = []
            for n in range(nch):
                beta = jnp.exp(a_l[n])
                dz = g_l[n] - beta * (g_l[n] + ((delta[n] - cg[n]) - gincs[n]))
                if diag:
                    dz = jnp.where(strict, dz, 0.0)
                if valid[n // (2 * P)] is not None:
                    dz = jnp.where(valid[n // (2 * P)], dz, 0.0)
                dzs.append(dz.astype(BF16))
            ndq = []
            for qb in range(QB):
                for p, sl in enumerate(lanes):
                    n0 = qb * 2 * P + 2 * p
                    ndq.append(dq[qb * P + p] + _dot(kth[n0], dzs[n0], NN) + _dot(kth[n0 + 1], dzs[n0 + 1], NN))
                    dk_ref[pl.ds(offs[qb], BLK), sl] += _dot(dzs[n0], qh[n0], NN) + _dot(dzs[n0 + 1], qh[n0 + 1], NN)
                    dv_ref[pl.ds(offs[qb], BLK), sl] += _dot(wb[n0], doh[n0], NN) + _dot(wb[n0 + 1], doh[n0 + 1], NN)
            ncb = [cb[n] + jnp.sum(b_l[n], axis=0, keepdims=True) for n in range(nch)]
            ncg = [cg[n] + jnp.sum(g_l[n], axis=0, keepdims=True) for n in range(nch)]
            return ncb, ncg, ndq

        c0 = jnp.zeros((1, BLK), F32)
        cb, cg, dq = block([i_first + qb for qb in range(QB)], [None] * QB, [c0] * nch, [c0] * nch,
                           [jnp.zeros((LANES, BLK), F32)] * (QB * P), True)

        def cond(st):
            return jnp.logical_and(i_first + QB - 1 - st[0] >= 0, st[1] > 0)

        def step(st):
            t, _, cb, cg, dq = st
            js = [i_first + qb - t for qb in range(QB)]
            valid = [js[qb] >= 0 for qb in range(QB - 1)] + [None]
            cb = [cb[n] if valid[n // (2 * P)] is None else jnp.where(valid[n // (2 * P)], cb[n], NEG_BIG)
                  for n in range(nch)]
            cb, cg, dq = block([jnp.maximum(j, 0) for j in js], valid, cb, cg, dq, False)
            return t + 1, _any_alive(cb), cb, cg, dq

        st = lax.while_loop(cond, step, (1, _any_alive(cb), cb, cg, dq))
        for qb in range(QB):
            for p, sl in enumerate(lanes):
                dq_ref[qb * BLK:(qb + 1) * BLK, sl] = jnp.transpose(st[4][qb * P + p]).astype(BF16)

    blk = pl.BlockSpec((QB * BLK, W), lambda g, i: (i, g))
    col_all = pl.BlockSpec((S, W), lambda g, i: (0, g))
    hbm = pl.BlockSpec(memory_space=pl.ANY)
    return _pcall(
        body, (qkv, o, do, qkv, kv_t), name=name,
        out_shape=(jax.ShapeDtypeStruct((S, D), BF16), jax.ShapeDtypeStruct((S, D), F32),
                   jax.ShapeDtypeStruct((S, D), F32)),
        grid=(ngroup, nb // QB),
        in_specs=[blk, blk, blk, hbm, hbm],
        out_specs=(blk, col_all, col_all),
        scratch_shapes=[pltpu.VMEM((S, W), BF16), pltpu.VMEM((S, W), BF16), pltpu.VMEM((W, S), BF16),
                        pltpu.SemaphoreType.DMA((3,))],
        sem=("arbitrary", "arbitrary"), carry=carry)


SWA_Q_GROUPS = 4


def _roll_heads(x):
    return pltpu.roll(x.astype(F32), HEAD_DIM, 1).astype(BF16)


def _roll_rows(x):
    return pltpu.roll(x.astype(F32), HEAD_DIM, 0).astype(BF16)


def _swa_valid(i):
    k = lax.broadcasted_iota(jnp.int32, (2 * BLK, BLK), 0)
    q = lax.broadcasted_iota(jnp.int32, (2 * BLK, BLK), 1)
    diff = q + BLK - k
    return (diff >= 0) & (diff < BLK) & ((i > 0) | (k >= BLK))


def _swa_probs(z, valid, sink):
    z = jnp.where(valid, z * ATTN_SCALE, NEG_BIG)
    mx = jnp.maximum(jnp.max(z, axis=0, keepdims=True), sink)
    p = jnp.exp(z - mx)
    ps = jnp.exp(sink - mx)
    inv = 1.0 / (jnp.sum(p, axis=0, keepdims=True) + ps)
    return p * inv, ps * inv


def _swa_operands(q_ref, kc_ref, kp_ref, vc_ref, vp_ref, tc_ref, tp_ref, s_ref, m):
    m0 = _head_masks()
    top = lax.broadcasted_iota(jnp.int32, (LANES, 2 * BLK), 0) < HEAD_DIM
    kk = jnp.concatenate([kp_ref[...], kc_ref[...]], axis=0)
    vv = jnp.concatenate([vp_ref[...], vc_ref[...]], axis=0)
    tt = jnp.concatenate([tp_ref[...], tc_ref[...]], axis=1)
    ksw, vsw, tsw = _roll_heads(kk), _roll_heads(vv), _roll_rows(tt)
    zt = jnp.zeros_like(tt)
    heads = []
    for c in range(SWA_Q_GROUPS):
        qc = q_ref[:, c * LANES:(c + 1) * LANES]
        zq = jnp.zeros_like(qc)
        for u in range(2):
            same = u == c // 2
            sel = (lambda x, z, mk: jnp.where(mk, x, z)) if u == 0 else (lambda x, z, mk: jnp.where(mk, z, x))
            heads.append(dict(
                c=c, same=same, sel=sel, qm=sel(qc, zq, m0),
                k=kk if same else ksw, v=vv if same else vsw,
                tm=sel(tt if same else tsw, zt, top),
                sink=s_ref[0, m * 2 * SWA_Q_GROUPS + 2 * c + u]))
    return heads, m0


def _swa_specs(nkvp, qw, t_row0):
    prev = lambda i: jnp.maximum(i - 1, 0)
    return [pl.BlockSpec((BLK, qw), lambda m, i: (i, m)),
            pl.BlockSpec((BLK, LANES), lambda m, i: (i, m)),
            pl.BlockSpec((BLK, LANES), lambda m, i: (prev(i), m)),
            pl.BlockSpec((BLK, LANES), lambda m, i: (i, nkvp + m)),
            pl.BlockSpec((BLK, LANES), lambda m, i: (prev(i), nkvp + m)),
            pl.BlockSpec((LANES, BLK), lambda m, i: (t_row0 + m, i)),
            pl.BlockSpec((LANES, BLK), lambda m, i: (t_row0 + m, prev(i))),
            pl.BlockSpec(memory_space=pltpu.SMEM)]


def _swa_fwd(q, kv, kv_t, sinks, name):
    S, D = q.shape
    nkvp = kv.shape[1] // (2 * LANES)
    nb = S // BLK
    qw = SWA_Q_GROUPS * LANES

    def body(q_ref, kc_ref, kp_ref, vc_ref, vp_ref, tc_ref, tp_ref, s_ref, o_ref):
        m, i = pl.program_id(0), pl.program_id(1)
        valid = _swa_valid(i)
        heads, _ = _swa_operands(q_ref, kc_ref, kp_ref, vc_ref, vp_ref, tc_ref, tp_ref, s_ref, m)
        zs = [_dot(hd["k"], hd["qm"], NT) for hd in heads]
        ps = [_swa_probs(z, valid, hd["sink"])[0].astype(BF16) for z, hd in zip(zs, heads)]
        for c in range(SWA_Q_GROUPS):
            o_t = _dot(heads[2 * c]["tm"], ps[2 * c], NN) + _dot(heads[2 * c + 1]["tm"], ps[2 * c + 1], NN)
            o_ref[:, c * LANES:(c + 1) * LANES] = jnp.transpose(o_t)

    return pl.pallas_call(
        body, name=name, out_shape=jax.ShapeDtypeStruct((S, D), F32),
        grid=(nkvp, nb),
        in_specs=_swa_specs(nkvp, qw, nkvp),
        out_specs=pl.BlockSpec((BLK, qw), lambda m, i: (i, m)),
        compiler_params=_params("arbitrary", "arbitrary"),
    )(q, kv, kv, kv, kv, kv_t, kv_t, sinks)


def _swa_bwd(q, kv, kv_t, sinks, o, do, cos_t, sin_t, name, carry=None):
    S, D = q.shape
    nkvp = kv.shape[1] // (2 * LANES)
    nb = S // BLK
    qw = SWA_Q_GROUPS * LANES
    nh = 2 * SWA_Q_GROUPS

    def body(q_ref, kc_ref, kp_ref, vc_ref, vp_ref, tc_ref, tp_ref, s_ref, o_ref, do_ref, c_ref, sn_ref,
             dq_ref, dk_ref, dv_ref, ds_ref):
        m, i = pl.program_id(0), pl.program_id(1)
        valid = _swa_valid(i)
        heads, m0 = _swa_operands(q_ref, kc_ref, kp_ref, vc_ref, vp_ref, tc_ref, tp_ref, s_ref, m)
        top_q = lax.broadcasted_iota(jnp.int32, (LANES, BLK), 0) < HEAD_DIM

        @pl.when(i == 0)
        def _():
            dk_ref[...] = jnp.zeros_like(dk_ref)
            dv_ref[...] = jnp.zeros_like(dv_ref)
            ds_ref[...] = jnp.zeros_like(ds_ref)

        doms, deltas = [], []
        for c in range(SWA_Q_GROUPS):
            doc = do_ref[:, c * LANES:(c + 1) * LANES]
            prod_t = jnp.transpose(doc.astype(F32) * o_ref[:, c * LANES:(c + 1) * LANES])
            for u in range(2):
                hd = heads[2 * c + u]
                doms.append(hd["sel"](doc, jnp.zeros_like(doc), m0))
                deltas.append(jnp.sum(hd["sel"](prod_t, 0.0, top_q), axis=0, keepdims=True))
        zs = [_dot(hd["k"], hd["qm"], NT) for hd in heads]
        dps = [_dot(hd["v"], dom, NT) for dom, hd in zip(doms, heads)]
        pbs, dscs = [], []
        for n, hd in enumerate(heads):
            p, psink = _swa_probs(zs[n], valid, hd["sink"])
            pbs.append(p.astype(BF16))
            dscs.append((p * (dps[n] - deltas[n]) * ATTN_SCALE).astype(BF16))
            ds_ref[0, n:n + 1, :] += -(psink * deltas[n])
        for c in range(SWA_Q_GROUPS):
            dq_rot = jnp.transpose(_dot(heads[2 * c]["tm"], dscs[2 * c], NN)
                                   + _dot(heads[2 * c + 1]["tm"], dscs[2 * c + 1], NN))
            dq_ref[:, c * LANES:(c + 1) * LANES] = (
                dq_rot * c_ref[...] + _swap_halves(dq_rot * sn_ref[...])).astype(BF16)
        acc = {}
        for n, hd in enumerate(heads):
            dk_n = _dot(dscs[n], hd["qm"], NN)
            dv_n = _dot(pbs[n], doms[n], NN)
            for key, val in ((("k", hd["same"]), dk_n), (("v", hd["same"]), dv_n)):
                acc[key] = val if key not in acc else acc[key] + val
        dkk = acc["k", True] + pltpu.roll(acc["k", False], HEAD_DIM, 1)
        dvv = acc["v", True] + pltpu.roll(acc["v", False], HEAD_DIM, 1)
        poff = pl.multiple_of(jnp.maximum(i - 1, 0) * BLK, BLK)
        coff = pl.multiple_of(i * BLK, BLK)
        dk_ref[pl.ds(poff, BLK), :] += dkk[:BLK]
        dv_ref[pl.ds(poff, BLK), :] += dvv[:BLK]
        dk_ref[pl.ds(coff, BLK), :] += dkk[BLK:]
        dv_ref[pl.ds(coff, BLK), :] += dvv[BLK:]

    qblk = pl.BlockSpec((BLK, qw), lambda m, i: (i, m))
    col_all = pl.BlockSpec((S, LANES), lambda m, i: (0, m))
    tab = pl.BlockSpec((BLK, LANES), lambda m, i: (i, 0))
    return _pcall(
        body, (q, kv, kv, kv, kv, kv_t, kv_t, sinks, o, do, cos_t, sin_t), name=name,
        out_shape=(jax.ShapeDtypeStruct((S, D), BF16),
                   jax.ShapeDtypeStruct((S, nkvp * LANES), F32),
                   jax.ShapeDtypeStruct((S, nkvp * LANES), F32),
                   jax.ShapeDtypeStruct((nkvp, nh, LANES), F32)),
        grid=(nkvp, nb),
        in_specs=_swa_specs(nkvp, qw, 0) + [qblk, qblk, tab, tab],
        out_specs=(qblk, col_all, col_all, pl.BlockSpec((1, nh, LANES), lambda m, i: (m, 0, 0))),
        sem=("arbitrary", "arbitrary"), carry=carry)


def _dev_index(p):
    return 4 * p[0] + 2 * p[1] + p[2]


def _gather_plan(x_refs, out_refs, send_sems, recv_sems, local_sems):
    n = len(x_refs)
    x_, y_, c_ = lax.axis_index("x"), lax.axis_index("y"), lax.axis_index("c")
    me, sibling = (x_, y_, c_), (x_, y_, 1 - c_)
    chips = [(1 - x_, y_), (x_, 1 - y_), (1 - x_, 1 - y_)]

    def copy(t, k, block, to, src=None):
        dst = out_refs[t].at[_dev_index(block)]
        return pltpu.make_async_remote_copy(
            src_ref=dst if src is None else src, dst_ref=dst,
            send_sem=send_sems.at[7 * t + k], recv_sem=recv_sems.at[7 * t + k],
            device_id=to, device_id_type=MESH)

    mine = [pltpu.make_async_copy(x_refs[t], out_refs[t].at[_dev_index(me)], local_sems.at[t]) for t in range(n)]
    first = []
    for t in range(n):
        first.append(copy(t, 0, me, sibling, src=x_refs[t]))
        first += [copy(t, 1 + j, me, (*chip, c_), src=x_refs[t]) for j, chip in enumerate(chips)]
    arrived = lambda t, j: copy(t, 1 + j, (*chips[j], c_), me)
    forward = lambda t, j: copy(t, 4 + j, (*chips[j], c_), sibling)
    from_sibling = lambda t: copy(t, 0, sibling, me)
    forwarded = lambda t, j: copy(t, 4 + j, (*chips[j], 1 - c_), me)
    return n, mine, first, arrived, forward, from_sibling, forwarded


def _gather_start(x_refs, out_refs, send_sems, recv_sems, local_sems):
    _, mine, first, *_ = _gather_plan(x_refs, out_refs, send_sems, recv_sems, local_sems)
    for cp in mine + first:
        cp.start()


def _gather_forward(x_refs, out_refs, send_sems, recv_sems, local_sems):
    n, _, _, arrived, forward, _, _ = _gather_plan(x_refs, out_refs, send_sems, recv_sems, local_sems)
    for j in range(3):
        for t in range(n):
            arrived(t, j).wait_recv()
            forward(t, j).start()


def _gather_finish(x_refs, out_refs, send_sems, recv_sems, local_sems):
    n, mine, first, _, forward, from_sibling, forwarded = _gather_plan(
        x_refs, out_refs, send_sems, recv_sems, local_sems)
    for t in range(n):
        from_sibling(t).wait_recv()
    for j in range(3):
        for t in range(n):
            forwarded(t, j).wait_recv()
    for cp in first + [forward(t, j) for j in range(3) for t in range(n)]:
        cp.wait_send()
    for cp in mine:
        cp.wait()


def _scatter_plan(b_refs, out_refs, send_sems, recv_sems, local_sems):
    n = len(b_refs)
    x_, y_, c_ = lax.axis_index("x"), lax.axis_index("y"), lax.axis_index("c")
    my_idx = _dev_index((x_, y_, c_))
    mine = [pltpu.make_async_copy(b_refs[t].at[my_idx], out_refs[t].at[my_idx], local_sems.at[t]) for t in range(n)]
    copies = []
    for t in range(n):
        for k in range(1, N_DEV):
            peer = (x_ ^ ((k >> 2) & 1), y_ ^ ((k >> 1) & 1), c_ ^ (k & 1))
            copies.append(pltpu.make_async_remote_copy(
                src_ref=b_refs[t].at[_dev_index(peer)], dst_ref=out_refs[t].at[my_idx],
                send_sem=send_sems.at[7 * t + k - 1], recv_sem=recv_sems.at[7 * t + k - 1],
                device_id=peer, device_id_type=MESH))
    return mine, copies


def _scatter_start(b_refs, out_refs, send_sems, recv_sems, local_sems):
    mine, copies = _scatter_plan(b_refs, out_refs, send_sems, recv_sems, local_sems)
    for cp in mine + copies:
        cp.start()


def _scatter_finish(b_refs, out_refs, send_sems, recv_sems, local_sems):
    mine, copies = _scatter_plan(b_refs, out_refs, send_sems, recv_sems, local_sems)
    for cp in copies:
        cp.wait_recv()
    for cp in copies:
        cp.wait_send()
    for cp in mine:
        cp.wait()


def _exchange_operands(kind, tensors):
    if kind == "gather":
        args = list(tensors)
        shapes = [jax.ShapeDtypeStruct((N_DEV,) + t.shape, t.dtype) for t in tensors]
        return args, shapes, (_gather_start, _gather_forward, _gather_finish)
    args = [t.reshape(N_DEV, t.shape[0] // N_DEV, t.shape[1]) for t in tensors]
    shapes = [jax.ShapeDtypeStruct(a.shape, a.dtype) for a in args]
    return args, shapes, (_scatter_start, None, _scatter_finish)


def _exchange_results(kind, tensors, res):
    if kind == "gather":
        return [r.reshape(N_DEV * t.shape[0], t.shape[1]) for r, t in zip(res, tensors)]
    return list(res)


def _exchange_sems(n):
    return [pltpu.SemaphoreType.DMA((7 * n,)), pltpu.SemaphoreType.DMA((7 * n,)), pltpu.SemaphoreType.DMA((n,))]


def _exchange(kind, tensors, name):
    n = len(tensors)
    args, shapes, phases = _exchange_operands(kind, tensors)

    def body(*refs):
        for phase in phases:
            if phase is not None:
                phase(refs[:n], refs[n:2 * n], *refs[2 * n:])

    hbm = pl.BlockSpec(memory_space=pl.ANY)
    res = pl.pallas_call(body, name=name, out_shape=shapes, in_specs=[hbm] * n, out_specs=[hbm] * n,
                         scratch_shapes=_exchange_sems(n))(*args)
    return _exchange_results(kind, tensors, res)


def _pcall(body, args, *, name, out_shape, grid, in_specs, out_specs, sem, scratch_shapes=(), carry=None):
    if carry is None:
        out = pl.pallas_call(body, name=name, out_shape=out_shape, grid=grid, in_specs=list(in_specs),
                             out_specs=out_specs, scratch_shapes=list(scratch_shapes),
                             compiler_params=_params(*sem))(*args)
        return out, None
    kind, tensors = carry
    multi = isinstance(out_shape, (tuple, list))
    shapes = list(out_shape) if multi else [out_shape]
    ospecs = list(out_specs) if multi else [out_specs]
    n_in, n_out, n_scr, n_c = len(in_specs), len(shapes), len(scratch_shapes), len(tensors)
    c_args, c_shapes, (start, forward, finish) = _exchange_operands(kind, tensors)
    n_steps = 1
    for g in grid:
        n_steps *= g
    late = (3 * n_steps) // 4

    def wrapped(*refs):
        ins, rest = refs[:n_in], refs[n_in:]
        c_in, rest = rest[:n_c], rest[n_c:]
        outs, rest = rest[:n_out], rest[n_out:]
        c_out, rest = rest[:n_c], rest[n_c:]
        scr, sems = rest[:n_scr], rest[n_scr:]
        step = pl.program_id(0)
        for a in range(1, len(grid)):
            step = step * grid[a] + pl.program_id(a)

        @pl.when(step == 0)
        def _():
            start(c_in, c_out, *sems)

        body(*ins, *outs, *scr)

        if forward is not None:
            @pl.when(step == late)
            def _():
                forward(c_in, c_out, *sems)

        @pl.when(step == n_steps - 1)
        def _():
            finish(c_in, c_out, *sems)

    hbm = pl.BlockSpec(memory_space=pl.ANY)
    res = pl.pallas_call(
        wrapped, name=name, out_shape=shapes + c_shapes, grid=grid,
        in_specs=list(in_specs) + [hbm] * n_c, out_specs=ospecs + [hbm] * n_c,
        scratch_shapes=list(scratch_shapes) + _exchange_sems(n_c),
        compiler_params=_params(*sem))(*args, *c_args)
    outs = tuple(res[:n_out]) if multi else res[0]
    return outs, _exchange_results(kind, tensors, res[n_out:])


def _sum8(parts, name):
    _, R, C = parts.shape
    tr = _tile(R, 256, 16)

    def body(p_ref, g_ref):
        g = p_ref[0].astype(F32)
        for s in range(1, N_DEV):
            g = g + p_ref[s].astype(F32)
        g_ref[...] = g

    return pl.pallas_call(
        body, name=name, out_shape=jax.ShapeDtypeStruct((R, C), F32),
        grid=(R // tr,),
        in_specs=[pl.BlockSpec((N_DEV, tr, C), lambda i: (0, i, 0))],
        out_specs=pl.BlockSpec((tr, C), lambda i: (i, 0)),
        compiler_params=_params("parallel"),
    )(parts)


def _adamw(g, w, m, v, name):
    R, C = g.shape
    tr = _tile(R, 256, 8)
    c1 = 1.0 - ADAM_B1 ** ADAM_STEP
    c2 = 1.0 - ADAM_B2 ** ADAM_STEP

    def body(g_ref, w_ref, m_ref, v_ref, d_ref, nm_ref, nv_ref):
        gg = g_ref[...]
        nm = ADAM_B1 * m_ref[...] + (1.0 - ADAM_B1) * gg
        nv = ADAM_B2 * v_ref[...] + (1.0 - ADAM_B2) * (gg * gg)
        m_hat = nm / c1
        v_hat = nv / c2
        nm_ref[...] = nm
        nv_ref[...] = nv
        d_ref[...] = -ADAM_LR * (m_hat / (jnp.sqrt(v_hat) + ADAM_EPS) + ADAM_WD * w_ref[...])

    row = pl.BlockSpec((tr, C), lambda i: (i, 0))
    shp = jax.ShapeDtypeStruct((R, C), F32)
    return pl.pallas_call(
        body, name=name, out_shape=(shp, shp, shp),
        grid=(R // tr,), in_specs=[row, row, row, row], out_specs=(row, row, row),
        compiler_params=_params("parallel"),
    )(g, w, m, v)


def _ffn_down(act, wo, h, tag):
    return _mm(act, wo, NN, F32, f"{tag}_down", scale=FFN_RES_SCALE, res=h, tm=512, tn=1024, tk=2816)


def _ffn_fwd(h, g, win_t, wo, tag, carry=None):
    return _ffn_fwd_fused(h, g, win_t, wo, f"{tag}_fwd", carry=carry)


def _ffn_bwd(dh, h, g, win_t, wo, saved, tag, scatter=False, carry=None, carry_dwin=None):
    xn, silu, dsilu, up, act = saved
    dwo = _mm(act, dh, TN, BF16, f"{tag}_dwo", scale=FFN_RES_SCALE, tm=1408, tn=1024, tk=TN_CHUNK)
    if not scatter:
        (dh_in, dg, dgate, dup), got = _ffn_bwd_fused(dh, h, g, win_t, wo, silu, dsilu, up, f"{tag}_bwd", carry=carry)
        dwin_t, got_dwin = _dw_rows([dgate, dup], xn, f"{tag}_dwin", carry=carry_dwin)
        return dh_in, dg, dwin_t, dwo, got, got_dwin
    dgate, dup = _ffn_dact(dh, wo, silu, dsilu, up, f"{tag}_dact")
    dwin_t, got_wo = _dw_rows([dgate, dup], xn, f"{tag}_dwin", carry=("scatter", [dwo]))
    (dh_in, dg), got_win = _dx_norm_bwd([(dgate, win_t, NN, 2, 0), (dup, win_t, NN, 2, 1)], h, g, dh, f"{tag}_dx",
                                        carry=("scatter", [dwin_t]))
    return dh_in, dg, got_win[0], got_wo[0]


def _proj(a, w, dims, out_dtype, name, res=None):
    return _mm(a, w, dims, out_dtype, name, res=res, tm=1024, tn=1024, tk=1024)


def _proj_dw(x, dy, name):
    return _mm(x, dy, TN, BF16, name, tm=1024, tn=1024, tk=TN_CHUNK)


def kernel(x, ffn1_norm, ffn1_w_in, ffn1_w_out, mix_norm, ffn2_norm, ffn2_w_in, ffn2_w_out, sb_w_qkv, sb_w_o, kv_norm, kv_w, swa_w_q, swa_sinks, swa_w_o, final_norm, loss_target, m_ffn1_norm, m_ffn1_w_in, m_ffn1_w_out, m_mix_norm, m_ffn2_norm, m_ffn2_w_in, m_ffn2_w_out, m_sb_w_qkv, m_sb_w_o, m_kv_norm, m_kv_w, m_swa_w_q, m_swa_sinks, m_swa_w_o, m_final_norm, v_ffn1_norm, v_ffn1_w_in, v_ffn1_w_out, v_mix_norm, v_ffn2_norm, v_ffn2_w_in, v_ffn2_w_out, v_sb_w_qkv, v_sb_w_o, v_kv_norm, v_kv_w, v_swa_w_q, v_swa_sinks, v_swa_w_o, v_final_norm):
    S, D = x.shape[1], x.shape[2]
    L = ffn1_w_in.shape[0]
    KV = kv_w.shape[1]
    assert L == 2 and swa_sinks.shape == (1, 2 * SWA_Q_GROUPS * KV // (2 * LANES))

    def bf(w):
        return w.astype(BF16)

    def bft(w):
        return jnp.transpose(w).astype(BF16)

    cos_t, sin_t = _rope_tables(S)
    h0 = x.reshape(S, D)
    tgt = loss_target.reshape(S, D)

    win1a_t, = _exchange("gather", [bft(ffn1_w_in[0])], "gather_first_weight")
    sv_a1, (wo1a, wqkv_t, w_sbo) = _ffn_up(
        h0, ffn1_norm[0], win1a_t, "ffn1a_up",
        carry=("gather", [bf(ffn1_w_out[0]), bft(sb_w_qkv[0]), bf(sb_w_o[0])]))
    h1 = _ffn_down(sv_a1[-1], wo1a, h0, "ffn1a")
    hn_a, qkv, kv_t = _norm_proj(h1, mix_norm[0], wqkv_t, NT, "sb_qkv", tail_t=2 * D)
    o_sb, (win2a_t, wo2a, w_kv) = _sb_fwd(qkv, kv_t, "sb_attn", carry=("gather", [
        bft(ffn2_w_in[0]), bf(ffn2_w_out[0]), bf(kv_w)]))
    h2 = _proj(o_sb, w_sbo, NN, F32, "sb_out", res=h1)
    h3, sv_a2, (win1b_t, wo1b, w_q, w_swo) = _ffn_fwd(h2, ffn2_norm[0], win2a_t, wo2a, "ffn2a", carry=("gather", [
        bft(ffn1_w_in[1]), bf(ffn1_w_out[1]), bf(swa_w_q[0]), bf(swa_w_o[0])]))
    kvn, kv_rot, kv_rot_t = _norm_proj(h3, kv_norm, w_kv, NN, "kv_proj", rope=(cos_t, sin_t, KV // (2 * LANES)),
                                       tail_t=KV)
    h4, sv_b1, (win2b_t, wo2b) = _ffn_fwd(h3, ffn1_norm[1], win1b_t, wo1b, "ffn1b", carry=("gather", [
        bft(ffn2_w_in[1]), bf(ffn2_w_out[1])]))
    hn_b, q_rot = _norm_proj(h4, mix_norm[1], w_q, NN, "swa_q", rope=(cos_t, sin_t, D // LANES))
    o_sw = _swa_fwd(q_rot, kv_rot, kv_rot_t, swa_sinks, "swa_attn")
    h5 = _proj(o_sw, w_swo, NN, F32, "swa_out", res=h4)
    h6, sv_b2, _ = _ffn_fwd(h5, ffn2_norm[1], win2b_t, wo2b, "ffn2b")
    dh6, dg_final, sq_err = _final_loss(h6, final_norm, tgt, "final_loss")
    loss = lax.psum(0.5 * jnp.sum(sq_err) / D, ("x", "y", "c"))

    dh5, dg_f2b, dwin2b_t, dwo2b, _, _ = _ffn_bwd(dh6, h5, ffn2_norm[1], win2b_t, wo2b, sv_b2, "ffn2b")
    do_sw = _proj(dh5, w_swo, NT, BF16, "swa_out_dx")
    dw_swo = _proj_dw(o_sw, dh5, "swa_out_dw")
    (dq, dk_sw, dv_sw, dsink), (p_win2b, p_swo) = _swa_bwd(
        q_rot, kv_rot, kv_rot_t, swa_sinks, o_sw, do_sw, cos_t, sin_t, "swa_attn_bwd",
        carry=("scatter", [dwin2b_t, dw_swo]))
    dw_q = _proj_dw(hn_b, dq, "swa_q_dw")
    (dh4, dg_mix_b), _ = _dx_norm_bwd([(dq, w_q, NT, 1, 0)], h4, mix_norm[1], dh5, "swa_q_dx")
    dh3, dg_f1b, dwin1b_t, dwo1b, (p_q, p_wo2b), _ = _ffn_bwd(dh4, h3, ffn1_norm[1], win1b_t, wo1b, sv_b1, "ffn1b",
                                                              carry=("scatter", [dw_q, dwo2b]))
    dkv = _rotary(jnp.concatenate([dk_sw, dv_sw], axis=1), cos_t, sin_t, KV // (2 * LANES), True, "kv_rope_bwd")
    dw_kv = _proj_dw(kvn, dkv, "kv_proj_dw")
    (dh3, dg_kv), _ = _dx_norm_bwd([(dkv, w_kv, NT, 1, 0)], h3, kv_norm, dh3, "kv_proj_dx")
    dh2, dg_f2a, dwin2a_t, dwo2a, (p_win1b, p_kv), (p_wo1b,) = _ffn_bwd(
        dh3, h2, ffn2_norm[0], win2a_t, wo2a, sv_a2, "ffn2a",
        carry=("scatter", [dwin1b_t, dw_kv]), carry_dwin=("scatter", [dwo1b]))
    do_sb = _proj(dh2, w_sbo, NT, BF16, "sb_out_dx")
    dw_sbo = _proj_dw(o_sb, dh2, "sb_out_dw")
    (dq_sb, dk_sb, dv_sb), (p_win2a, p_wo2a, p_sbo) = _sb_bwd(
        qkv, kv_t, o_sb, do_sb, "sb_attn_bwd", carry=("scatter", [dwin2a_t, dwo2a, dw_sbo]))
    dqkv = [dq_sb, dk_sb, dv_sb]
    dwqkv_t, _ = _dw_rows(dqkv, hn_a, "sb_qkv_dw", tk=TN_CHUNK // 2)
    (dh1, dg_mix_a), (p_qkv,) = _dx_norm_bwd([(dy, wqkv_t, NN, 3, n) for n, dy in enumerate(dqkv)], h1, mix_norm[0],
                                             dh2, "sb_qkv_dx", carry=("scatter", [dwqkv_t]))
    dx, dg_f1a, p_win1a, p_wo1a = _ffn_bwd(dh1, h0, ffn1_norm[0], win1a_t, wo1a, sv_a1, "ffn1a", scatter=True)

    def natural(parts, tag):
        return _sum8(parts, f"sum_{tag}")

    def from_t(parts, tag):
        return jnp.transpose(_sum8(parts, f"sum_{tag}"))

    grads = {
        "ffn1_w_in": jnp.stack([from_t(p_win1a, "win1a"), from_t(p_win1b, "win1b")]),
        "ffn1_w_out": jnp.stack([natural(p_wo1a, "wo1a"), natural(p_wo1b, "wo1b")]),
        "ffn2_w_in": jnp.stack([from_t(p_win2a, "win2a"), from_t(p_win2b, "win2b")]),
        "ffn2_w_out": jnp.stack([natural(p_wo2a, "wo2a"), natural(p_wo2b, "wo2b")]),
        "sb_w_qkv": from_t(p_qkv, "qkv")[None],
        "sb_w_o": natural(p_sbo, "sbo")[None],
        "kv_w": natural(p_kv, "kv"),
        "swa_w_q": natural(p_q, "swq")[None],
        "swa_w_o": natural(p_swo, "swo")[None],
    }

    small_w = [ffn1_norm, mix_norm, ffn2_norm, kv_norm, final_norm, swa_sinks]
    small_m = [m_ffn1_norm, m_mix_norm, m_ffn2_norm, m_kv_norm, m_final_norm, m_swa_sinks]
    small_v = [v_ffn1_norm, v_mix_norm, v_ffn2_norm, v_kv_norm, v_final_norm, v_swa_sinks]
    SMALL_ROWS = 16

    def pack_small(ts):
        rows_ = [t.reshape(-1, D) for t in ts[:-1]]
        sink_row = jnp.pad(ts[-1].reshape(1, -1), ((0, 0), (0, D - ts[-1].size)))
        flat = jnp.concatenate(rows_ + [sink_row], axis=0)
        return jnp.pad(flat, ((0, SMALL_ROWS - flat.shape[0]), (0, 0)))

    def unpack_small(flat):
        out, r = [], 0
        for t in small_w[:-1]:
            n = t.size // D
            out.append(flat[r:r + n].reshape(t.shape))
            r += n
        out.append(flat[r, :swa_sinks.size].reshape(swa_sinks.shape))
        return out

    def gain(parts8):
        return jnp.sum(parts8, axis=0, keepdims=True)

    g_small_local = pack_small([
        jnp.concatenate([gain(dg_f1a), gain(dg_f1b)], axis=0),
        jnp.concatenate([gain(dg_mix_a), gain(dg_mix_b)], axis=0),
        jnp.concatenate([gain(dg_f2a), gain(dg_f2b)], axis=0),
        gain(dg_kv), gain(dg_final), jnp.sum(dsink, axis=-1).reshape(1, -1)])
    small_parts = _exchange("gather", [g_small_local], "gather_small_grads")[0]
    g_small = _sum8(small_parts.reshape(N_DEV, SMALL_ROWS, D), "sum_small")
    d_small, nm_small, nv_small = _adamw(g_small, pack_small(small_w), pack_small(small_m), pack_small(small_v), "adamw_small")
    small_names = ["ffn1_norm", "mix_norm", "ffn2_norm", "kv_norm", "final_norm", "swa_sinks"]
    result = {"grad": dict(zip(small_names, unpack_small(g_small))),
              "delta": dict(zip(small_names, unpack_small(d_small))),
              "new_m": dict(zip(small_names, unpack_small(nm_small))),
              "new_v": dict(zip(small_names, unpack_small(nv_small)))}

    big = {"ffn1_w_in": (ffn1_w_in, m_ffn1_w_in, v_ffn1_w_in), "ffn1_w_out": (ffn1_w_out, m_ffn1_w_out, v_ffn1_w_out),
           "ffn2_w_in": (ffn2_w_in, m_ffn2_w_in, v_ffn2_w_in), "ffn2_w_out": (ffn2_w_out, m_ffn2_w_out, v_ffn2_w_out),
           "sb_w_qkv": (sb_w_qkv, m_sb_w_qkv, v_sb_w_qkv), "sb_w_o": (sb_w_o, m_sb_w_o, v_sb_w_o),
           "kv_w": (kv_w, m_kv_w, v_kv_w), "swa_w_q": (swa_w_q, m_swa_w_q, v_swa_w_q),
           "swa_w_o": (swa_w_o, m_swa_w_o, v_swa_w_o)}
    for nm, (w, m, v) in big.items():
        g = grads[nm]
        two_d = lambda t: t.reshape(-1, t.shape[-1])
        d, new_m, new_v = _adamw(two_d(g), two_d(w), two_d(m), two_d(v), f"adamw_{nm}")
        result["grad"][nm] = g
        result["delta"][nm] = d.reshape(w.shape)
        result["new_m"][nm] = new_m.reshape(w.shape)
        result["new_v"][nm] = new_v.reshape(w.shape)

    order = ["ffn1_norm", "ffn1_w_in", "ffn1_w_out", "mix_norm", "ffn2_norm", "ffn2_w_in", "ffn2_w_out",
             "sb_w_qkv", "sb_w_o", "kv_norm", "kv_w", "swa_w_q", "swa_sinks", "swa_w_o", "final_norm"]
    outs = [result[kind][nm] for kind in ("grad", "delta", "new_m", "new_v") for nm in order]
    return (loss, dx.reshape(x.shape), *outs)
```

```python
import jax
import jax.numpy as jnp
from jax import lax
from jax.experimental import pallas as pl
from jax.experimental.pallas import tpu as pltpu

F32 = jnp.float32
BF16 = jnp.bfloat16

N_DEV = 8
HEAD_DIM = 64
LANES = 128
BLK = 128
RMS_EPS = 1e-6
FFN_RES_SCALE = 0.5
ROPE_THETA = 10000.0
ATTN_SCALE = HEAD_DIM ** -0.5
SB_LOG_FLOOR = -88.0
NEG_BIG = -1e30
VMEM_LIMIT_V7X = 56 * 1024 * 1024

ADAM_LR = 0.001
ADAM_B1 = 0.9
ADAM_B2 = 0.999
ADAM_EPS = 1e-08
ADAM_WD = 0.01
ADAM_STEP = 10

NN = ((1,), (0,))
NT = ((1,), (1,))
TN = ((0,), (0,))
TN_CHUNK = 2048
MESH = pl.DeviceIdType.MESH


def _dot(a, b, dims):
    return lax.dot_general(a, b, (dims, ((), ())), preferred_element_type=F32)


def _tile(n, pref, mult=LANES):
    if n <= pref:
        return n
    t = (pref // mult) * mult
    while t >= mult:
        if n % t == 0:
            return t
        t -= mult
    return n


def _params(*sem):
    return pltpu.CompilerParams(dimension_semantics=sem, vmem_limit_bytes=VMEM_LIMIT_V7X)


def _mm(a, b, dims, out_dtype, name, scale=1.0, res=None, tm=512, tn=512, tk=512):
    if dims == NN:
        (M, K), (_, N) = a.shape, b.shape
    elif dims == NT:
        (M, K), (N, _) = a.shape, b.shape
    else:
        (K, M), (_, N) = a.shape, b.shape
    tm, tn, tk = _tile(M, tm), _tile(N, tn), _tile(K, tk)
    nk = K // tk
    if dims == TN:
        a_spec = pl.BlockSpec((tk, tm), lambda i, j, k: (k, i))
    else:
        a_spec = pl.BlockSpec((tm, tk), lambda i, j, k: (i, k))
    if dims == NT:
        b_spec = pl.BlockSpec((tn, tk), lambda i, j, k: (j, k))
    else:
        b_spec = pl.BlockSpec((tk, tn), lambda i, j, k: (k, j))
    o_spec = pl.BlockSpec((tm, tn), lambda i, j, k: (i, j))
    has_res = res is not None

    def body(*refs):
        a_ref, b_ref = refs[0], refs[1]
        r_ref = refs[2] if has_res else None
        o_ref = refs[3] if has_res else refs[2]

        def finish(acc):
            r = acc * scale if scale != 1.0 else acc
            if has_res:
                r = r + r_ref[...]
            o_ref[...] = r.astype(out_dtype)

        p = _dot(a_ref[...].astype(BF16), b_ref[...].astype(BF16), dims)
        if nk == 1:
            finish(p)
        else:
            acc_ref = refs[-1]
            k = pl.program_id(2)

            @pl.when(k == 0)
            def _():
                acc_ref[...] = p

            @pl.when(k > 0)
            def _():
                acc_ref[...] += p

            @pl.when(k == nk - 1)
            def _():
                finish(acc_ref[...])

    in_specs = [a_spec, b_spec] + ([o_spec] if has_res else [])
    args = (a, b) + ((res,) if has_res else ())
    return pl.pallas_call(
        body, name=name,
        out_shape=jax.ShapeDtypeStruct((M, N), out_dtype),
        grid=(M // tm, N // tn, nk),
        in_specs=in_specs, out_specs=o_spec,
        scratch_shapes=[pltpu.VMEM((tm, tn), F32)] if nk > 1 else [],
        compiler_params=_params("parallel", "parallel", "arbitrary"),
    )(*args)


def _rows8(x):
    r, d = x.shape
    return jnp.sum(x.reshape(r // 8, 8, d), axis=0)


def _norm_proj(h, g, w, dims, name, rope=None, tail_t=0):
    S, D = h.shape
    N = w.shape[1] if dims == NN else w.shape[0]
    tm = _tile(S, 512, 16)

    def body(h_ref, g_ref, w_ref, *rest):
        xn_ref, y_ref = rest[-3:-1] if tail_t else rest[-2:]
        x = h_ref[...]
        r = lax.rsqrt(jnp.mean(x * x, axis=-1, keepdims=True) + RMS_EPS)
        xn = ((x * r) * g_ref[...]).astype(BF16)
        xn_ref[...] = xn
        y = _dot(xn, w_ref[...], dims)
        if rope is not None:
            cs, sn = rest[0][...], rest[1][...]
            groups = [y[:, gidx * LANES:(gidx + 1) * LANES] for gidx in range(N // LANES)]
            y = jnp.concatenate([v * cs + _swap_halves(v) * sn if gidx < rope[2] else v
                                 for gidx, v in enumerate(groups)], axis=1)
        y_ref[...] = y.astype(BF16)
        if tail_t:
            rest[-1][...] = jnp.transpose(y[:, N - tail_t:]).astype(BF16)

    row = pl.BlockSpec((tm, D), lambda i: (i, 0))
    tab = pl.BlockSpec((tm, LANES), lambda i: (i, 0))
    in_specs = [row, pl.BlockSpec((1, D), lambda i: (0, 0)), pl.BlockSpec(w.shape, lambda i: (0, 0))]
    args = (h, g.reshape(1, D), w)
    if rope is not None:
        in_specs += [tab, tab]
        args += (rope[0], rope[1])
    out_shape = [jax.ShapeDtypeStruct((S, D), BF16), jax.ShapeDtypeStruct((S, N), BF16)]
    out_specs = [row, pl.BlockSpec((tm, N), lambda i: (i, 0))]
    if tail_t:
        out_shape.append(jax.ShapeDtypeStruct((tail_t, S), BF16))
        out_specs.append(pl.BlockSpec((tail_t, tm), lambda i: (0, i)))
    return pl.pallas_call(
        body, name=name, out_shape=out_shape, grid=(S // tm,),
        in_specs=in_specs, out_specs=out_specs,
        compiler_params=_params("parallel"),
    )(*args)


def _final_loss(h, g, tgt, name):
    S, D = h.shape
    ts = _tile(S, 512, 8)

    def body(h_ref, g_ref, t_ref, dh_ref, dg_ref, l_ref):
        x = h_ref[...]
        r = lax.rsqrt(jnp.mean(x * x, axis=-1, keepdims=True) + RMS_EPS)
        xhat = x * r
        err = xhat * g_ref[...] - t_ref[...]
        d = err * (1.0 / D)
        dxh = d * g_ref[...]
        c = jnp.mean(dxh * xhat, axis=-1, keepdims=True)
        dh_ref[...] = r * (dxh - xhat * c)
        part = _rows8(d * xhat)
        lpart = _rows8(err * err)

        @pl.when(pl.program_id(0) == 0)
        def _():
            dg_ref[...] = part
            l_ref[...] = lpart

        @pl.when(pl.program_id(0) > 0)
        def _():
            dg_ref[...] += part
            l_ref[...] += lpart

    row = pl.BlockSpec((ts, D), lambda i: (i, 0))
    acc = pl.BlockSpec((8, D), lambda i: (0, 0))
    return pl.pallas_call(
        body, name=name,
        out_shape=(jax.ShapeDtypeStruct((S, D), F32), jax.ShapeDtypeStruct((8, D), F32),
                   jax.ShapeDtypeStruct((8, D), F32)),
        grid=(S // ts,),
        in_specs=[row, pl.BlockSpec((1, D), lambda i: (0, 0)), row],
        out_specs=(row, acc, acc),
        compiler_params=_params("arbitrary"),
    )(h, g.reshape(1, D), tgt)


def _ffn_up(h, g, win_t, name, carry=None):
    S, D = h.shape
    F = win_t.shape[0] // 2
    tm, tn = _tile(S, 512, 16), _tile(F, 1408)
    nf = F // tn

    def body(h_ref, g_ref, wg_ref, wu_ref, xn_ref, silu_ref, dsilu_ref, up_ref, act_ref):
        x = h_ref[...]
        r = lax.rsqrt(jnp.mean(x * x, axis=-1, keepdims=True) + RMS_EPS)
        xn = ((x * r) * g_ref[...]).astype(BF16)
        xn_ref[...] = xn
        gate = _dot(xn, wg_ref[...], NT)
        up = _dot(xn, wu_ref[...], NT)
        sig = 1.0 / (1.0 + jnp.exp(-gate))
        silu = gate * sig
        up_ref[...] = up.astype(BF16)
        silu_ref[...] = silu.astype(BF16)
        dsilu_ref[...] = (sig + silu * (1.0 - sig)).astype(BF16)
        act_ref[...] = (silu * up).astype(BF16)

    row = pl.BlockSpec((tm, D), lambda i, j: (i, 0))
    blk = pl.BlockSpec((tm, tn), lambda i, j: (i, j))
    hid = jax.ShapeDtypeStruct((S, F), BF16)
    return _pcall(
        body, (h, g.reshape(1, D), win_t, win_t), name=name,
        out_shape=(jax.ShapeDtypeStruct((S, D), BF16), hid, hid, hid, hid),
        grid=(S // tm, nf),
        in_specs=[row, pl.BlockSpec((1, D), lambda i, j: (0, 0)),
                  pl.BlockSpec((tn, D), lambda i, j: (j, 0)),
                  pl.BlockSpec((tn, D), lambda i, j: (j + nf, 0))],
        out_specs=(row, blk, blk, blk, blk),
        sem=("arbitrary", "arbitrary"), carry=carry)


def _ffn_dact(dh, wo, silu, dsilu, up, name):
    S, D = dh.shape
    F = wo.shape[0]
    tm, tn = _tile(S, 512, 16), _tile(F, 1408)

    def body(dh_ref, wo_ref, s_ref, ds_ref, u_ref, dg_ref, du_ref):
        d = _dot(dh_ref[...].astype(BF16), wo_ref[...], NT) * FFN_RES_SCALE
        du_ref[...] = (d * s_ref[...].astype(F32)).astype(BF16)
        dg_ref[...] = (d * u_ref[...].astype(F32) * ds_ref[...].astype(F32)).astype(BF16)

    blk = pl.BlockSpec((tm, tn), lambda j, i: (i, j))
    hid = jax.ShapeDtypeStruct((S, F), BF16)
    return pl.pallas_call(
        body, name=name, out_shape=(hid, hid),
        grid=(F // tn, S // tm),
        in_specs=[pl.BlockSpec((tm, D), lambda j, i: (i, 0)), pl.BlockSpec((tn, D), lambda j, i: (j, 0)),
                  blk, blk, blk],
        out_specs=(blk, blk),
        compiler_params=_params("arbitrary", "arbitrary"),
    )(dh, wo, silu, dsilu, up)


def _dw_rows(srcs, x, name, carry=None, tk=TN_CHUNK):
    n = len(srcs)
    S, F = srcs[0].shape
    D = x.shape[1]
    tr, tk = _tile(F, 1408), _tile(S, tk, 16)
    nf, nk = F // tr, S // tk

    def body(*refs):
        src_refs, (x_ref, o_ref, acc_ref) = refs[:n], refs[n:]
        r, k = pl.program_id(0), pl.program_id(1)
        for s in range(n):
            @pl.when(r // nf == s)
            def _():
                p = _dot(src_refs[s][...].astype(BF16), x_ref[...], TN)

                @pl.when(k == 0)
                def _():
                    acc_ref[...] = p

                @pl.when(k > 0)
                def _():
                    acc_ref[...] += p

        @pl.when(k == nk - 1)
        def _():
            o_ref[...] = acc_ref[...].astype(BF16)

    def src_spec(s):
        return pl.BlockSpec((tk, tr), lambda r, k: (jnp.where(r // nf == s, k, 0), jnp.clip(r - s * nf, 0, nf - 1)))

    return _pcall(
        body, (*srcs, x), name=name, out_shape=jax.ShapeDtypeStruct((n * F, D), BF16),
        grid=(n * nf, nk),
        in_specs=[src_spec(s) for s in range(n)] + [pl.BlockSpec((tk, D), lambda r, k: (k, 0))],
        out_specs=pl.BlockSpec((tr, D), lambda r, k: (r, 0)),
        scratch_shapes=[pltpu.VMEM((tr, D), F32)],
        sem=("arbitrary", "arbitrary"), carry=carry)


def _dx_norm_bwd(terms, h, g, res, name, carry=None):
    S, D = h.shape
    tm = _tile(S, 256, 16)
    n = len(terms)

    def body(*refs):
        dy_refs, w_refs = refs[:n], refs[n:2 * n]
        h_ref, g_ref, r_ref, dh_ref, dg_ref = refs[2 * n:]
        d = _dot(dy_refs[0][...].astype(BF16), w_refs[0][...], terms[0][2])
        for t in range(1, n):
            d = d + _dot(dy_refs[t][...].astype(BF16), w_refs[t][...], terms[t][2])
        x = h_ref[...]
        r = lax.rsqrt(jnp.mean(x * x, axis=-1, keepdims=True) + RMS_EPS)
        xhat = x * r
        dxh = d * g_ref[...]
        c = jnp.mean(dxh * xhat, axis=-1, keepdims=True)
        dh_ref[...] = r * (dxh - xhat * c) + r_ref[...]
        part = _rows8(d * xhat)

        @pl.when(pl.program_id(0) == 0)
        def _():
            dg_ref[...] = part

        @pl.when(pl.program_id(0) > 0)
        def _():
            dg_ref[...] += part

    def w_spec(w, nblk, blk):
        return pl.BlockSpec((w.shape[0] // nblk, w.shape[1]), lambda i: (blk, 0))

    row = pl.BlockSpec((tm, D), lambda i: (i, 0))
    in_specs = [pl.BlockSpec((tm, t[0].shape[1]), lambda i: (i, 0)) for t in terms]
    in_specs += [w_spec(t[1], t[3], t[4]) for t in terms]
    in_specs += [row, pl.BlockSpec((1, D), lambda i: (0, 0)), row]
    return _pcall(
        body, (*[t[0] for t in terms], *[t[1] for t in terms], h, g.reshape(1, D), res), name=name,
        out_shape=(jax.ShapeDtypeStruct((S, D), F32), jax.ShapeDtypeStruct((8, D), F32)),
        grid=(S // tm,),
        in_specs=in_specs,
        out_specs=(row, pl.BlockSpec((8, D), lambda i: (0, 0))),
        sem=("arbitrary",), carry=carry)


def _load_resident(pairs, sems):
    @pl.when(pl.program_id(0) == 0)
    def _():
        copies = [pltpu.make_async_copy(src, dst, sems.at[n]) for n, (src, dst) in enumerate(pairs)]
        for cp in copies:
            cp.start()
        for cp in copies:
            cp.wait()


def _ffn_fwd_fused(h, g, win_t, wo, name, carry=None):
    S, D = h.shape
    F = wo.shape[0]
    tm = _tile(S, 256, 16)

    def body(h_ref, g_ref, win_hbm, wo_hbm, out_ref, xn_ref, silu_ref, dsilu_ref, up_ref, act_ref, win_v, wo_v, sems):
        _load_resident([(win_hbm, win_v), (wo_hbm, wo_v)], sems)
        x = h_ref[...]
        r = lax.rsqrt(jnp.mean(x * x, axis=-1, keepdims=True) + RMS_EPS)
        xn = ((x * r) * g_ref[...]).astype(BF16)
        xn_ref[...] = xn
        gate = _dot(xn, win_v[:F, :], NT)
        up = _dot(xn, win_v[F:, :], NT)
        sig = 1.0 / (1.0 + jnp.exp(-gate))
        silu = gate * sig
        act = (silu * up).astype(BF16)
        up_ref[...] = up.astype(BF16)
        silu_ref[...] = silu.astype(BF16)
        dsilu_ref[...] = (sig + silu * (1.0 - sig)).astype(BF16)
        act_ref[...] = act
        out_ref[...] = x + FFN_RES_SCALE * _dot(act, wo_v[...], NN)

    row = pl.BlockSpec((tm, D), lambda i: (i, 0))
    wide = pl.BlockSpec((tm, F), lambda i: (i, 0))
    hbm = pl.BlockSpec(memory_space=pl.ANY)
    hid = jax.ShapeDtypeStruct((S, F), BF16)
    res, got = _pcall(
        body, (h, g.reshape(1, D), win_t, wo), name=name,
        out_shape=(jax.ShapeDtypeStruct((S, D), F32), jax.ShapeDtypeStruct((S, D), BF16), hid, hid, hid, hid),
        grid=(S // tm,),
        in_specs=[row, pl.BlockSpec((1, D), lambda i: (0, 0)), hbm, hbm],
        out_specs=(row, row, wide, wide, wide, wide),
        scratch_shapes=[pltpu.VMEM(win_t.shape, BF16), pltpu.VMEM(wo.shape, BF16), pltpu.SemaphoreType.DMA((2,))],
        sem=("arbitrary",), carry=carry)
    return res[0], tuple(res[1:]), got


def _ffn_bwd_fused(dh, h, g, win_t, wo, silu, dsilu, up, name, carry=None):
    S, D = h.shape
    F = wo.shape[0]
    tm = _tile(S, 256, 16)

    def body(dh_ref, h_ref, g_ref, s_ref, ds_ref, u_ref, win_hbm, wo_hbm,
             dhin_ref, dgain_ref, dgate_ref, dup_ref, win_v, wo_v, sems):
        _load_resident([(win_hbm, win_v), (wo_hbm, wo_v)], sems)
        dhv = dh_ref[...]
        d = _dot(dhv.astype(BF16), wo_v[...], NT) * FFN_RES_SCALE
        dup = (d * s_ref[...].astype(F32)).astype(BF16)
        dgate = (d * u_ref[...].astype(F32) * ds_ref[...].astype(F32)).astype(BF16)
        dup_ref[...] = dup
        dgate_ref[...] = dgate
        dxn = _dot(dgate, win_v[:F, :], NN) + _dot(dup, win_v[F:, :], NN)
        x = h_ref[...]
        r = lax.rsqrt(jnp.mean(x * x, axis=-1, keepdims=True) + RMS_EPS)
        xhat = x * r
        dxh = dxn * g_ref[...]
        c = jnp.mean(dxh * xhat, axis=-1, keepdims=True)
        dhin_ref[...] = r * (dxh - xhat * c) + dhv
        part = _rows8(dxn * xhat)

        @pl.when(pl.program_id(0) == 0)
        def _():
            dgain_ref[...] = part

        @pl.when(pl.program_id(0) > 0)
        def _():
            dgain_ref[...] += part

    row = pl.BlockSpec((tm, D), lambda i: (i, 0))
    wide = pl.BlockSpec((tm, F), lambda i: (i, 0))
    hbm = pl.BlockSpec(memory_space=pl.ANY)
    hid = jax.ShapeDtypeStruct((S, F), BF16)
    return _pcall(
        body, (dh, h, g.reshape(1, D), silu, dsilu, up, win_t, wo), name=name,
        out_shape=(jax.ShapeDtypeStruct((S, D), F32), jax.ShapeDtypeStruct((8, D), F32), hid, hid),
        grid=(S // tm,),
        in_specs=[row, row, pl.BlockSpec((1, D), lambda i: (0, 0)), wide, wide, wide, hbm, hbm],
        out_specs=(row, pl.BlockSpec((8, D), lambda i: (0, 0)), wide, wide),
        scratch_shapes=[pltpu.VMEM(win_t.shape, BF16), pltpu.VMEM(wo.shape, BF16), pltpu.SemaphoreType.DMA((2,))],
        sem=("arbitrary",), carry=carry)


def _rope_tables(S):
    half = HEAD_DIM // 2
    inv_freq = ROPE_THETA ** (-jnp.arange(half, dtype=F32) / half)
    ang = jnp.arange(S).astype(F32)[:, None] * inv_freq[None, :]
    cos, sin = jnp.cos(ang), jnp.sin(ang)
    cos_t = jnp.tile(cos, (1, LANES // half))
    sin_t = jnp.tile(jnp.concatenate([-sin, sin], axis=1), (1, LANES // HEAD_DIM))
    return cos_t, sin_t


def _swap_halves(x):
    lane = lax.broadcasted_iota(jnp.int32, x.shape, 1)
    first = (lane % HEAD_DIM) < (HEAD_DIM // 2)
    return jnp.where(first, pltpu.roll(x, LANES - HEAD_DIM // 2, 1), pltpu.roll(x, HEAD_DIM // 2, 1))


def _rotary(x, cos_t, sin_t, n_rot, inverse, name):
    S, C = x.shape
    ts = _tile(S, 512, 16)
    ng = C // LANES

    def body(x_ref, c_ref, s_ref, o_ref):
        cs, sn = c_ref[...], s_ref[...]
        for gidx in range(ng):
            sl = slice(gidx * LANES, (gidx + 1) * LANES)
            v = x_ref[:, sl].astype(F32)
            if gidx < n_rot:
                if inverse:
                    v = v * cs + _swap_halves(v * sn)
                else:
                    v = v * cs + _swap_halves(v) * sn
            o_ref[:, sl] = v.astype(BF16)

    row = pl.BlockSpec((ts, C), lambda i: (i, 0))
    tab = pl.BlockSpec((ts, LANES), lambda i: (i, 0))
    return pl.pallas_call(
        body, name=name, out_shape=jax.ShapeDtypeStruct((S, C), BF16),
        grid=(S // ts,), in_specs=[row, tab, tab], out_specs=row,
        compiler_params=_params("parallel"),
    )(x, cos_t, sin_t)


def _head_masks():
    lane = lax.broadcasted_iota(jnp.int32, (BLK, LANES), 1)
    return lane < HEAD_DIM


def _split_bf16(x):
    hi = x.astype(BF16)
    lo = (x - hi.astype(F32)).astype(BF16)
    return hi, lo


def _sb_scores(qh, ks, carry, diag, tri_excl, strict):
    n_heads = len(qh)
    zs = [_dot(ks[n], qh[n], NT) for n in range(n_heads)]
    a_l, b_l, split_l = [], [], []
    for z in zs:
        a = jnp.minimum(z, 0.0) - jnp.log(1.0 + jnp.exp(-jnp.abs(z)))
        b = a - z
        if diag:
            b = jnp.where(strict, b, 0.0)
        a_l.append(a)
        b_l.append(b)
        split_l.append(_split_bf16(b))
    sufs = [_dot(tri_excl, hi, NN) + _dot(tri_excl, lo, NN) for hi, lo in split_l]
    w_l = []
    for n in range(n_heads):
        w = jnp.exp(a_l[n] + sufs[n] + carry[n])
        if diag:
            w = jnp.where(strict, w, 0.0)
        w_l.append(w)
    return a_l, b_l, w_l


SB_FWD_PAIRS = 4
SB_FWD_QBLOCKS = 2
SB_BWD_PAIRS = 2
SB_BWD_QBLOCKS = 4


def _any_alive(carries):
    top = carries[0]
    for c in carries[1:]:
        top = jnp.maximum(top, c)
    return (jnp.max(top) > SB_LOG_FLOOR).astype(jnp.int32)


def _sb_masks():
    row = lax.broadcasted_iota(jnp.int32, (BLK, BLK), 0)
    col = lax.broadcasted_iota(jnp.int32, (BLK, BLK), 1)
    tri_excl = jnp.where(col > row, 1.0, 0.0).astype(BF16)
    tri_incl = jnp.where(col >= row, 1.0, 0.0).astype(BF16)
    return row < HEAD_DIM, row < col, tri_excl, tri_incl


def _sb_fwd(qkv, kv_t, name, carry=None):
    S, D3 = qkv.shape
    D = D3 // 3
    npair, nb = D // LANES, S // BLK
    P = min(SB_FWD_PAIRS, npair)
    ngroup = npair // P
    W = P * LANES

    QB = SB_FWD_QBLOCKS if nb % SB_FWD_QBLOCKS == 0 else 1
    nch = QB * 2 * P

    def body(q_ref, k_ref, vt_ref, o_ref):
        i_first = pl.program_id(1) * QB
        m0 = _head_masks()
        top, strict, tri_excl, _ = _sb_masks()
        zq = jnp.zeros((BLK, LANES), BF16)
        lanes = [slice(p * LANES, (p + 1) * LANES) for p in range(P)]
        qh = []
        for qb in range(QB):
            for sl in lanes:
                q2 = q_ref[qb * BLK:(qb + 1) * BLK, sl] * ATTN_SCALE
                qh += [jnp.where(m0, q2, zq), jnp.where(m0, zq, q2)]

        def block(js, carry, acc, diag):
            offs = [pl.multiple_of(j * BLK, BLK) for j in js]
            ks, vth = [], []
            for qb in range(QB):
                for sl in lanes:
                    k2 = k_ref[pl.ds(offs[qb], BLK), sl]
                    vt = vt_ref[sl, pl.ds(offs[qb], BLK)]
                    ks += [k2, k2]
                    vth += [jnp.where(top, vt, zq), jnp.where(top, zq, vt)]
            _, b_l, w_l = _sb_scores(qh, ks, carry, diag, tri_excl, strict)
            wb = [w.astype(BF16) for w in w_l]
            new_acc = [acc[m] + _dot(vth[2 * m], wb[2 * m], NN) + _dot(vth[2 * m + 1], wb[2 * m + 1], NN)
                       for m in range(QB * P)]
            new_carry = [carry[n] + jnp.sum(b_l[n], axis=0, keepdims=True) for n in range(nch)]
            return new_carry, new_acc

        c0 = jnp.zeros((1, BLK), F32)
        carry, acc = block([i_first + qb for qb in range(QB)], [c0] * nch,
                           [jnp.zeros((LANES, BLK), F32)] * (QB * P), True)

        def cond(st):
            return jnp.logical_and(i_first + QB - 1 - st[0] >= 0, st[1] > 0)

        def step(st):
            t, _, carry, acc = st
            js = [i_first + qb - t for qb in range(QB)]
            carry = [carry[n] if n // (2 * P) == QB - 1 else jnp.where(js[n // (2 * P)] >= 0, carry[n], NEG_BIG)
                     for n in range(nch)]
            carry, acc = block([jnp.maximum(j, 0) for j in js], carry, acc, False)
            return t + 1, _any_alive(carry), carry, acc

        st = lax.while_loop(cond, step, (1, _any_alive(carry), carry, acc))
        for qb in range(QB):
            for p, sl in enumerate(lanes):
                o_ref[qb * BLK:(qb + 1) * BLK, sl] = jnp.transpose(st[3][qb * P + p])

    return _pcall(
        body, (qkv, qkv, kv_t), name=name, out_shape=jax.ShapeDtypeStruct((S, D), F32),
        grid=(ngroup, nb // QB),
        in_specs=[pl.BlockSpec((QB * BLK, W), lambda g, i: (i, g)),
                  pl.BlockSpec((S, W), lambda g, i: (0, ngroup + g)),
                  pl.BlockSpec((W, S), lambda g, i: (ngroup + g, 0))],
        out_specs=pl.BlockSpec((QB * BLK, W), lambda g, i: (i, g)),
        sem=("arbitrary", "arbitrary"), carry=carry)


def _sb_bwd(qkv, kv_t, o, do, name, carry=None):
    S, D3 = qkv.shape
    D = D3 // 3
    npair, nb = D // LANES, S // BLK
    P = min(SB_BWD_PAIRS, npair)
    ngroup = npair // P
    W = P * LANES

    QB = SB_BWD_QBLOCKS if nb % SB_BWD_QBLOCKS == 0 else 1
    nch = QB * 2 * P

    def body(q_ref, o_ref, do_ref, qkv_hbm, kt_hbm, dq_ref, dk_ref, dv_ref, k_ref, v_ref, kt_ref, sems):
        grp = pl.program_id(0)
        i_first = pl.program_id(1) * QB
        m0 = _head_masks()
        top, strict, tri_excl, tri_incl = _sb_masks()
        zq = jnp.zeros((BLK, LANES), BF16)
        lanes = [slice(p * LANES, (p + 1) * LANES) for p in range(P)]

        @pl.when(pl.program_id(1) == 0)
        def _():
            copies = [pltpu.make_async_copy(qkv_hbm.at[:, pl.ds(pl.multiple_of((c * ngroup + grp) * W, LANES), W)],
                                            ref, sems.at[c - 1]) for c, ref in ((1, k_ref), (2, v_ref))]
            copies.append(pltpu.make_async_copy(kt_hbm.at[pl.ds(pl.multiple_of(grp * W, LANES), W), :],
                                                kt_ref, sems.at[2]))
            for cp in copies:
                cp.start()
            dk_ref[...] = jnp.zeros_like(dk_ref)
            dv_ref[...] = jnp.zeros_like(dv_ref)
            for cp in copies:
                cp.wait()

        qh, doh, delta = [], [], []
        for qb in range(QB):
            rs = slice(qb * BLK, (qb + 1) * BLK)
            for sl in lanes:
                q2, do2 = q_ref[rs, sl] * ATTN_SCALE, do_ref[rs, sl]
                qh += [jnp.where(m0, q2, zq), jnp.where(m0, zq, q2)]
                doh += [jnp.where(m0, do2, zq), jnp.where(m0, zq, do2)]
                prod_t = jnp.transpose(do2.astype(F32) * o_ref[rs, sl])
                delta += [jnp.sum(jnp.where(top, prod_t, 0.0), axis=0, keepdims=True),
                          jnp.sum(jnp.where(top, 0.0, prod_t), axis=0, keepdims=True)]

        def block(js, valid, cb, cg, dq, diag):
            offs = [pl.multiple_of(j * BLK, BLK) for j in js]
            ks, vs, kth = [], [], []
            for qb in range(QB):
                for sl in lanes:
                    k2, v2 = k_ref[pl.ds(offs[qb], BLK), sl], v_ref[pl.ds(offs[qb], BLK), sl]
                    ks += [k2, k2]
                    vs += [v2, v2]
                    kt = kt_ref[sl, pl.ds(offs[qb], BLK)] * ATTN_SCALE
                    kth += [jnp.where(top, kt, zq), jnp.where(top, zq, kt)]
            dws = [_dot(vs[n], doh[n], NT) for n in range(nch)]
            a_l, b_l, w_l = _sb_scores(qh, ks, cb, diag, tri_excl, strict)
            wb = [w.astype(BF16) for w in w_l]
            g_l = [dws[n] * wb[n].astype(F32) for n in range(nch)]
            gsplit = [_split_bf16(g) for g in g_l]
            gincs = [_dot(tri_incl, hi, NN) + _dot(tri_incl, lo, NN) for hi, lo in gsplit]
            dzs = []
            for n in range(nch):
                beta = jnp.exp(a_l[n])
                dz = g_l[n] - beta * (g_l[n] + ((delta[n] - cg[n]) - gincs[n]))
                if diag:
                    dz = jnp.where(strict, dz, 0.0)
                if valid[n // (2 * P)] is not None:
                    dz = jnp.where(valid[n // (2 * P)], dz, 0.0)
                dzs.append(dz.astype(BF16))
            ndq = []
            for qb in range(QB):
                for p, sl in enumerate(lanes):
                    n0 = qb * 2 * P + 2 * p
                    ndq.append(dq[qb * P + p] + _dot(kth[n0], dzs[n0], NN) + _dot(kth[n0 + 1], dzs[n0 + 1], NN))
                    dk_ref[pl.ds(offs[qb], BLK), sl] += _dot(dzs[n0], qh[n0], NN) + _dot(dzs[n0 + 1], qh[n0 + 1], NN)
                    dv_ref[pl.ds(offs[qb], BLK), sl] += _dot(wb[n0], doh[n0], NN) + _dot(wb[n0 + 1], doh[n0 + 1], NN)
            ncb = [cb[n] + jnp.sum(b_l[n], axis=0, keepdims=True) for n in range(nch)]
            ncg = [cg[n] + jnp.sum(g_l[n], axis=0, keepdims=True) for n in range(nch)]
            return ncb, ncg, ndq

        c0 = jnp.zeros((1, BLK), F32)
        cb, cg, dq = block([i_first + qb for qb in range(QB)], [None] * QB, [c0] * nch, [c0] * nch,
                           [jnp.zeros((LANES, BLK), F32)] * (QB * P), True)

        def cond(st):
            return jnp.logical_and(i_first + QB - 1 - st[0] >= 0, st[1] > 0)

        def step(st):
            t, _, cb, cg, dq = st
            js = [i_first + qb - t for qb in range(QB)]
            valid = [js[qb] >= 0 for qb in range(QB - 1)] + [None]
            cb = [cb[n] if valid[n // (2 * P)] is None else jnp.where(valid[n // (2 * P)], cb[n], NEG_BIG)
                  for n in range(nch)]
            cb, cg, dq = block([jnp.maximum(j, 0) for j in js], valid, cb, cg, dq, False)
            return t + 1, _any_alive(cb), cb, cg, dq

        st = lax.while_loop(cond, step, (1, _any_alive(cb), cb, cg, dq))
        for qb in range(QB):
            for p, sl in enumerate(lanes):
                dq_ref[qb * BLK:(qb + 1) * BLK, sl] = jnp.transpose(st[4][qb * P + p]).astype(BF16)

    blk = pl.BlockSpec((QB * BLK, W), lambda g, i: (i, g))
    col_all = pl.BlockSpec((S, W), lambda g, i: (0, g))
    hbm = pl.BlockSpec(memory_space=pl.ANY)
    return _pcall(
        body, (qkv, o, do, qkv, kv_t), name=name,
        out_shape=(jax.ShapeDtypeStruct((S, D), BF16), jax.ShapeDtypeStruct((S, D), F32),
                   jax.ShapeDtypeStruct((S, D), F32)),
        grid=(ngroup, nb // QB),
        in_specs=[blk, blk, blk, hbm, hbm],
        out_specs=(blk, col_all, col_all),
        scratch_shapes=[pltpu.VMEM((S, W), BF16), pltpu.VMEM((S, W), BF16), pltpu.VMEM((W, S), BF16),
                        pltpu.SemaphoreType.DMA((3,))],
        sem=("arbitrary", "arbitrary"), carry=carry)


SWA_Q_GROUPS = 4


def _roll_heads(x):
    return pltpu.roll(x.astype(F32), HEAD_DIM, 1).astype(BF16)


def _roll_rows(x):
    return pltpu.roll(x.astype(F32), HEAD_DIM, 0).astype(BF16)


def _swa_valid(i):
    k = lax.broadcasted_iota(jnp.int32, (2 * BLK, BLK), 0)
    q = lax.broadcasted_iota(jnp.int32, (2 * BLK, BLK), 1)
    diff = q + BLK - k
    return (diff >= 0) & (diff < BLK) & ((i > 0) | (k >= BLK))


def _swa_probs(z, valid, sink):
    z = jnp.where(valid, z * ATTN_SCALE, NEG_BIG)
    mx = jnp.maximum(jnp.max(z, axis=0, keepdims=True), sink)
    p = jnp.exp(z - mx)
    ps = jnp.exp(sink - mx)
    inv = 1.0 / (jnp.sum(p, axis=0, keepdims=True) + ps)
    return p * inv, ps * inv


def _swa_operands(q_ref, kc_ref, kp_ref, vc_ref, vp_ref, tc_ref, tp_ref, s_ref, nkvp):
    m0 = _head_masks()
    top = lax.broadcasted_iota(jnp.int32, (LANES, 2 * BLK), 0) < HEAD_DIM
    heads = []
    for m in range(nkvp):
        pair = slice(m * LANES, (m + 1) * LANES)
        kk = jnp.concatenate([kp_ref[:, pair], kc_ref[:, pair]], axis=0)
        vv = jnp.concatenate([vp_ref[:, pair], vc_ref[:, pair]], axis=0)
        tt = jnp.concatenate([tp_ref[pair, :], tc_ref[pair, :]], axis=1)
        ksw, vsw, tsw = _roll_heads(kk), _roll_heads(vv), _roll_rows(tt)
        zt = jnp.zeros_like(tt)
        for c in range(SWA_Q_GROUPS):
            q_lanes = slice((m * SWA_Q_GROUPS + c) * LANES, (m * SWA_Q_GROUPS + c + 1) * LANES)
            qc = q_ref[:, q_lanes]
            zq = jnp.zeros_like(qc)
            for u in range(2):
                same = u == c // 2
                sel = (lambda x, z, mk: jnp.where(mk, x, z)) if u == 0 else (lambda x, z, mk: jnp.where(mk, z, x))
                heads.append(dict(
                    m=m, q_lanes=q_lanes, same=same, sel=sel, qm=sel(qc, zq, m0),
                    k=kk if same else ksw, v=vv if same else vsw,
                    tm=sel(tt if same else tsw, zt, top),
                    sink=s_ref[0, (m * SWA_Q_GROUPS + c) * 2 + u]))
    return heads, m0


def _swa_specs(D, half, t_block):
    prev = lambda i: jnp.maximum(i - 1, 0)
    return [pl.BlockSpec((BLK, D), lambda i: (i, 0)),
            pl.BlockSpec((BLK, half), lambda i: (i, 0)),
            pl.BlockSpec((BLK, half), lambda i: (prev(i), 0)),
            pl.BlockSpec((BLK, half), lambda i: (i, 1)),
            pl.BlockSpec((BLK, half), lambda i: (prev(i), 1)),
            pl.BlockSpec((half, BLK), lambda i: (t_block, i)),
            pl.BlockSpec((half, BLK), lambda i: (t_block, prev(i))),
            pl.BlockSpec(memory_space=pltpu.SMEM)]


def _swa_fwd(q, kv, kv_t, sinks, name):
    S, D = q.shape
    half = kv.shape[1] // 2
    nkvp = half // LANES

    def body(q_ref, kc_ref, kp_ref, vc_ref, vp_ref, tc_ref, tp_ref, s_ref, o_ref):
        valid = _swa_valid(pl.program_id(0))
        heads, _ = _swa_operands(q_ref, kc_ref, kp_ref, vc_ref, vp_ref, tc_ref, tp_ref, s_ref, nkvp)
        zs = [_dot(hd["k"], hd["qm"], NT) for hd in heads]
        ps = [_swa_probs(z, valid, hd["sink"])[0].astype(BF16) for z, hd in zip(zs, heads)]
        for n in range(0, len(heads), 2):
            o_t = _dot(heads[n]["tm"], ps[n], NN) + _dot(heads[n + 1]["tm"], ps[n + 1], NN)
            o_ref[:, heads[n]["q_lanes"]] = jnp.transpose(o_t)

    return pl.pallas_call(
        body, name=name, out_shape=jax.ShapeDtypeStruct((S, D), F32),
        grid=(S // BLK,),
        in_specs=_swa_specs(D, half, 1),
        out_specs=pl.BlockSpec((BLK, D), lambda i: (i, 0)),
        compiler_params=_params("arbitrary"),
    )(q, kv, kv, kv, kv, kv_t, kv_t, sinks)


def _swa_bwd(q, kv, kv_t, sinks, o, do, cos_t, sin_t, name, carry=None):
    S, D = q.shape
    half = kv.shape[1] // 2
    nkvp = half // LANES
    nh = nkvp * 2 * SWA_Q_GROUPS

    def body(q_ref, kc_ref, kp_ref, vc_ref, vp_ref, tc_ref, tp_ref, s_ref, o_ref, do_ref, c_ref, sn_ref,
             dq_ref, dk_ref, dv_ref, ds_ref):
        i = pl.program_id(0)
        valid = _swa_valid(i)
        heads, m0 = _swa_operands(q_ref, kc_ref, kp_ref, vc_ref, vp_ref, tc_ref, tp_ref, s_ref, nkvp)
        top_q = lax.broadcasted_iota(jnp.int32, (LANES, BLK), 0) < HEAD_DIM

        @pl.when(i == 0)
        def _():
            dk_ref[...] = jnp.zeros_like(dk_ref)
            dv_ref[...] = jnp.zeros_like(dv_ref)
            ds_ref[...] = jnp.zeros_like(ds_ref)

        doms, deltas = [], []
        for n in range(0, nh, 2):
            doc = do_ref[:, heads[n]["q_lanes"]]
            prod_t = jnp.transpose(doc.astype(F32) * o_ref[:, heads[n]["q_lanes"]])
            for hd in heads[n:n + 2]:
                doms.append(hd["sel"](doc, jnp.zeros_like(doc), m0))
                deltas.append(jnp.sum(hd["sel"](prod_t, 0.0, top_q), axis=0, keepdims=True))
        zs = [_dot(hd["k"], hd["qm"], NT) for hd in heads]
        dps = [_dot(hd["v"], dom, NT) for dom, hd in zip(doms, heads)]
        pbs, dscs = [], []
        for n, hd in enumerate(heads):
            p, psink = _swa_probs(zs[n], valid, hd["sink"])
            pbs.append(p.astype(BF16))
            dscs.append((p * (dps[n] - deltas[n]) * ATTN_SCALE).astype(BF16))
            ds_ref[n:n + 1, :] += -(psink * deltas[n])
        for n in range(0, nh, 2):
            dq_rot = jnp.transpose(_dot(heads[n]["tm"], dscs[n], NN) + _dot(heads[n + 1]["tm"], dscs[n + 1], NN))
            dq_ref[:, heads[n]["q_lanes"]] = (
                dq_rot * c_ref[...] + _swap_halves(dq_rot * sn_ref[...])).astype(BF16)
        acc = {}
        for n, hd in enumerate(heads):
            dk_n = _dot(dscs[n], hd["qm"], NN)
            dv_n = _dot(pbs[n], doms[n], NN)
            for key, val in ((("k", hd["m"], hd["same"]), dk_n), (("v", hd["m"], hd["same"]), dv_n)):
                acc[key] = val if key not in acc else acc[key] + val
        poff = pl.multiple_of(jnp.maximum(i - 1, 0) * BLK, BLK)
        coff = pl.multiple_of(i * BLK, BLK)
        for m in range(nkvp):
            pair = slice(m * LANES, (m + 1) * LANES)
            dkk = acc["k", m, True] + pltpu.roll(acc["k", m, False], HEAD_DIM, 1)
            dvv = acc["v", m, True] + pltpu.roll(acc["v", m, False], HEAD_DIM, 1)
            dk_ref[pl.ds(poff, BLK), pair] += dkk[:BLK]
            dv_ref[pl.ds(poff, BLK), pair] += dvv[:BLK]
            dk_ref[pl.ds(coff, BLK), pair] += dkk[BLK:]
            dv_ref[pl.ds(coff, BLK), pair] += dvv[BLK:]

    qblk = pl.BlockSpec((BLK, D), lambda i: (i, 0))
    whole = pl.BlockSpec((S, half), lambda i: (0, 0))
    tab = pl.BlockSpec((BLK, LANES), lambda i: (i, 0))
    return _pcall(
        body, (q, kv, kv, kv, kv, kv_t, kv_t, sinks, o, do, cos_t, sin_t), name=name,
        out_shape=(jax.ShapeDtypeStruct((S, D), BF16),
                   jax.ShapeDtypeStruct((S, half), F32),
                   jax.ShapeDtypeStruct((S, half), F32),
                   jax.ShapeDtypeStruct((nh, LANES), F32)),
        grid=(S // BLK,),
        in_specs=_swa_specs(D, half, 0) + [qblk, qblk, tab, tab],
        out_specs=(qblk, whole, whole, pl.BlockSpec((nh, LANES), lambda i: (0, 0))),
        sem=("arbitrary",), carry=carry)


def _dev_index(p):
    return 4 * p[0] + 2 * p[1] + p[2]


def _gather_plan(x_refs, out_refs, send_sems, recv_sems, local_sems):
    n = len(x_refs)
    x_, y_, c_ = lax.axis_index("x"), lax.axis_index("y"), lax.axis_index("c")
    me, sibling = (x_, y_, c_), (x_, y_, 1 - c_)
    chips = [(1 - x_, y_), (x_, 1 - y_), (1 - x_, 1 - y_)]

    def copy(t, k, block, to, src=None):
        dst = out_refs[t].at[_dev_index(block)]
        return pltpu.make_async_remote_copy(
            src_ref=dst if src is None else src, dst_ref=dst,
            send_sem=send_sems.at[7 * t + k], recv_sem=recv_sems.at[7 * t + k],
            device_id=to, device_id_type=MESH)

    mine = [pltpu.make_async_copy(x_refs[t], out_refs[t].at[_dev_index(me)], local_sems.at[t]) for t in range(n)]
    first = []
    for t in range(n):
        first.append(copy(t, 0, me, sibling, src=x_refs[t]))
        first += [copy(t, 1 + j, me, (*chip, c_), src=x_refs[t]) for j, chip in enumerate(chips)]
    arrived = lambda t, j: copy(t, 1 + j, (*chips[j], c_), me)
    forward = lambda t, j: copy(t, 4 + j, (*chips[j], c_), sibling)
    from_sibling = lambda t: copy(t, 0, sibling, me)
    forwarded = lambda t, j: copy(t, 4 + j, (*chips[j], 1 - c_), me)
    return n, mine, first, arrived, forward, from_sibling, forwarded


def _gather_start(x_refs, out_refs, send_sems, recv_sems, local_sems):
    _, mine, first, *_ = _gather_plan(x_refs, out_refs, send_sems, recv_sems, local_sems)
    for cp in mine + first:
        cp.start()


def _gather_forward(x_refs, out_refs, send_sems, recv_sems, local_sems):
    n, _, _, arrived, forward, _, _ = _gather_plan(x_refs, out_refs, send_sems, recv_sems, local_sems)
    for j in range(3):
        for t in range(n):
            arrived(t, j).wait_recv()
            forward(t, j).start()


def _gather_finish(x_refs, out_refs, send_sems, recv_sems, local_sems):
    n, mine, first, _, forward, from_sibling, forwarded = _gather_plan(
        x_refs, out_refs, send_sems, recv_sems, local_sems)
    for t in range(n):
        from_sibling(t).wait_recv()
    for j in range(3):
        for t in range(n):
            forwarded(t, j).wait_recv()
    for cp in first + [forward(t, j) for j in range(3) for t in range(n)]:
        cp.wait_send()
    for cp in mine:
        cp.wait()


def _scatter_plan(b_refs, out_refs, send_sems, recv_sems, local_sems):
    n = len(b_refs)
    x_, y_, c_ = lax.axis_index("x"), lax.axis_index("y"), lax.axis_index("c")
    my_idx = _dev_index((x_, y_, c_))
    mine = [pltpu.make_async_copy(b_refs[t].at[my_idx], out_refs[t].at[my_idx], local_sems.at[t]) for t in range(n)]
    copies = []
    for t in range(n):
        for k in range(1, N_DEV):
            peer = (x_ ^ ((k >> 2) & 1), y_ ^ ((k >> 1) & 1), c_ ^ (k & 1))
            copies.append(pltpu.make_async_remote_copy(
                src_ref=b_refs[t].at[_dev_index(peer)], dst_ref=out_refs[t].at[my_idx],
                send_sem=send_sems.at[7 * t + k - 1], recv_sem=recv_sems.at[7 * t + k - 1],
                device_id=peer, device_id_type=MESH))
    return mine, copies


def _scatter_start(b_refs, out_refs, send_sems, recv_sems, local_sems):
    mine, copies = _scatter_plan(b_refs, out_refs, send_sems, recv_sems, local_sems)
    for cp in mine + copies:
        cp.start()


def _scatter_finish(b_refs, out_refs, send_sems, recv_sems, local_sems):
    mine, copies = _scatter_plan(b_refs, out_refs, send_sems, recv_sems, local_sems)
    for cp in copies:
        cp.wait_recv()
    for cp in copies:
        cp.wait_send()
    for cp in mine:
        cp.wait()


def _exchange_operands(kind, tensors):
    if kind == "gather":
        args = list(tensors)
        shapes = [jax.ShapeDtypeStruct((N_DEV,) + t.shape, t.dtype) for t in tensors]
        return args, shapes, (_gather_start, _gather_forward, _gather_finish)
    args = [t.reshape(N_DEV, t.shape[0] // N_DEV, t.shape[1]) for t in tensors]
    shapes = [jax.ShapeDtypeStruct(a.shape, a.dtype) for a in args]
    return args, shapes, (_scatter_start, None, _scatter_finish)


def _exchange_results(kind, tensors, res):
    if kind == "gather":
        return [r.reshape(N_DEV * t.shape[0], t.shape[1]) for r, t in zip(res, tensors)]
    return list(res)


def _exchange_sems(n):
    return [pltpu.SemaphoreType.DMA((7 * n,)), pltpu.SemaphoreType.DMA((7 * n,)), pltpu.SemaphoreType.DMA((n,))]


def _exchange(kind, tensors, name):
    n = len(tensors)
    args, shapes, phases = _exchange_operands(kind, tensors)

    def body(*refs):
        for phase in phases:
            if phase is not None:
                phase(refs[:n], refs[n:2 * n], *refs[2 * n:])

    hbm = pl.BlockSpec(memory_space=pl.ANY)
    res = pl.pallas_call(body, name=name, out_shape=shapes, in_specs=[hbm] * n, out_specs=[hbm] * n,
                         scratch_shapes=_exchange_sems(n))(*args)
    return _exchange_results(kind, tensors, res)


def _pcall(body, args, *, name, out_shape, grid, in_specs, out_specs, sem, scratch_shapes=(), carry=None):
    if carry is None:
        out = pl.pallas_call(body, name=name, out_shape=out_shape, grid=grid, in_specs=list(in_specs),
                             out_specs=out_specs, scratch_shapes=list(scratch_shapes),
                             compiler_params=_params(*sem))(*args)
        return out, None
    kind, tensors = carry
    multi = isinstance(out_shape, (tuple, list))
    shapes = list(out_shape) if multi else [out_shape]
    ospecs = list(out_specs) if multi else [out_specs]
    n_in, n_out, n_scr, n_c = len(in_specs), len(shapes), len(scratch_shapes), len(tensors)
    c_args, c_shapes, (start, forward, finish) = _exchange_operands(kind, tensors)
    n_steps = 1
    for g in grid:
        n_steps *= g
    late = (3 * n_steps) // 4

    def wrapped(*refs):
        ins, rest = refs[:n_in], refs[n_in:]
        c_in, rest = rest[:n_c], rest[n_c:]
        outs, rest = rest[:n_out], rest[n_out:]
        c_out, rest = rest[:n_c], rest[n_c:]
        scr, sems = rest[:n_scr], rest[n_scr:]
        step = pl.program_id(0)
        for a in range(1, len(grid)):
            step = step * grid[a] + pl.program_id(a)

        @pl.when(step == 0)
        def _():
            start(c_in, c_out, *sems)

        body(*ins, *outs, *scr)

        if forward is not None:
            @pl.when(step == late)
            def _():
                forward(c_in, c_out, *sems)

        @pl.when(step == n_steps - 1)
        def _():
            finish(c_in, c_out, *sems)

    hbm = pl.BlockSpec(memory_space=pl.ANY)
    res = pl.pallas_call(
        wrapped, name=name, out_shape=shapes + c_shapes, grid=grid,
        in_specs=list(in_specs) + [hbm] * n_c, out_specs=ospecs + [hbm] * n_c,
        scratch_shapes=list(scratch_shapes) + _exchange_sems(n_c),
        compiler_params=_params(*sem))(*args, *c_args)
    outs = tuple(res[:n_out]) if multi else res[0]
    return outs, _exchange_results(kind, tensors, res[n_out:])


def _sum8(parts, name):
    _, R, C = parts.shape
    tr = _tile(R, 256, 16)

    def body(p_ref, g_ref):
        g = p_ref[0].astype(F32)
        for s in range(1, N_DEV):
            g = g + p_ref[s].astype(F32)
        g_ref[...] = g

    return pl.pallas_call(
        body, name=name, out_shape=jax.ShapeDtypeStruct((R, C), F32),
        grid=(R // tr,),
        in_specs=[pl.BlockSpec((N_DEV, tr, C), lambda i: (0, i, 0))],
        out_specs=pl.BlockSpec((tr, C), lambda i: (i, 0)),
        compiler_params=_params("parallel"),
    )(parts)


def _adamw(g, w, m, v, name):
    R, C = g.shape
    tr = _tile(R, 256, 8)
    c1 = 1.0 - ADAM_B1 ** ADAM_STEP
    c2 = 1.0 - ADAM_B2 ** ADAM_STEP

    def body(g_ref, w_ref, m_ref, v_ref, d_ref, nm_ref, nv_ref):
        gg = g_ref[...]
        nm = ADAM_B1 * m_ref[...] + (1.0 - ADAM_B1) * gg
        nv = ADAM_B2 * v_ref[...] + (1.0 - ADAM_B2) * (gg * gg)
        m_hat = nm / c1
        v_hat = nv / c2
        nm_ref[...] = nm
        nv_ref[...] = nv
        d_ref[...] = -ADAM_LR * (m_hat / (jnp.sqrt(v_hat) + ADAM_EPS) + ADAM_WD * w_ref[...])

    row = pl.BlockSpec((tr, C), lambda i: (i, 0))
    shp = jax.ShapeDtypeStruct((R, C), F32)
    return pl.pallas_call(
        body, name=name, out_shape=(shp, shp, shp),
        grid=(R // tr,), in_specs=[row, row, row, row], out_specs=(row, row, row),
        compiler_params=_params("parallel"),
    )(g, w, m, v)


def _ffn_down(act, wo, h, tag):
    return _mm(act, wo, NN, F32, f"{tag}_down", scale=FFN_RES_SCALE, res=h, tm=512, tn=1024, tk=2816)


def _ffn_fwd(h, g, win_t, wo, tag, carry=None):
    return _ffn_fwd_fused(h, g, win_t, wo, f"{tag}_fwd", carry=carry)


def _ffn_bwd(dh, h, g, win_t, wo, saved, tag, scatter=False, carry=None, carry_dwin=None):
    xn, silu, dsilu, up, act = saved
    dwo = _mm(act, dh, TN, BF16, f"{tag}_dwo", scale=FFN_RES_SCALE, tm=1408, tn=1024, tk=TN_CHUNK)
    if not scatter:
        (dh_in, dg, dgate, dup), got = _ffn_bwd_fused(dh, h, g, win_t, wo, silu, dsilu, up, f"{tag}_bwd", carry=carry)
        dwin_t, got_dwin = _dw_rows([dgate, dup], xn, f"{tag}_dwin", carry=carry_dwin)
        return dh_in, dg, dwin_t, dwo, got, got_dwin
    dgate, dup = _ffn_dact(dh, wo, silu, dsilu, up, f"{tag}_dact")
    dwin_t, got_wo = _dw_rows([dgate, dup], xn, f"{tag}_dwin", carry=("scatter", [dwo]))
    (dh_in, dg), got_win = _dx_norm_bwd([(dgate, win_t, NN, 2, 0), (dup, win_t, NN, 2, 1)], h, g, dh, f"{tag}_dx",
                                        carry=("scatter", [dwin_t]))
    return dh_in, dg, got_win[0], got_wo[0]


def _proj(a, w, dims, out_dtype, name, res=None):
    return _mm(a, w, dims, out_dtype, name, res=res, tm=1024, tn=1024, tk=1024)


def _proj_dw(x, dy, name):
    return _mm(x, dy, TN, BF16, name, tm=1024, tn=1024, tk=TN_CHUNK)


def kernel(x, ffn1_norm, ffn1_w_in, ffn1_w_out, mix_norm, ffn2_norm, ffn2_w_in, ffn2_w_out, sb_w_qkv, sb_w_o, kv_norm, kv_w, swa_w_q, swa_sinks, swa_w_o, final_norm, loss_target, m_ffn1_norm, m_ffn1_w_in, m_ffn1_w_out, m_mix_norm, m_ffn2_norm, m_ffn2_w_in, m_ffn2_w_out, m_sb_w_qkv, m_sb_w_o, m_kv_norm, m_kv_w, m_swa_w_q, m_swa_sinks, m_swa_w_o, m_final_norm, v_ffn1_norm, v_ffn1_w_in, v_ffn1_w_out, v_mix_norm, v_ffn2_norm, v_ffn2_w_in, v_ffn2_w_out, v_sb_w_qkv, v_sb_w_o, v_kv_norm, v_kv_w, v_swa_w_q, v_swa_sinks, v_swa_w_o, v_final_norm):
    S, D = x.shape[1], x.shape[2]
    L = ffn1_w_in.shape[0]
    KV = kv_w.shape[1]
    assert L == 2 and swa_sinks.shape == (1, 2 * SWA_Q_GROUPS * KV // (2 * LANES))

    def bf(w):
        return w.astype(BF16)

    def bft(w):
        return jnp.transpose(w).astype(BF16)

    cos_t, sin_t = _rope_tables(S)
    h0 = x.reshape(S, D)
    tgt = loss_target.reshape(S, D)

    win1a_t, = _exchange("gather", [bft(ffn1_w_in[0])], "gather_first_weight")
    sv_a1, (wo1a, wqkv_t, w_sbo) = _ffn_up(
        h0, ffn1_norm[0], win1a_t, "ffn1a_up",
        carry=("gather", [bf(ffn1_w_out[0]), bft(sb_w_qkv[0]), bf(sb_w_o[0])]))
    h1 = _ffn_down(sv_a1[-1], wo1a, h0, "ffn1a")
    hn_a, qkv, kv_t = _norm_proj(h1, mix_norm[0], wqkv_t, NT, "sb_qkv", tail_t=2 * D)
    o_sb, (win2a_t, wo2a, w_kv) = _sb_fwd(qkv, kv_t, "sb_attn", carry=("gather", [
        bft(ffn2_w_in[0]), bf(ffn2_w_out[0]), bf(kv_w)]))
    h2 = _proj(o_sb, w_sbo, NN, F32, "sb_out", res=h1)
    h3, sv_a2, (win1b_t, wo1b, w_q, w_swo) = _ffn_fwd(h2, ffn2_norm[0], win2a_t, wo2a, "ffn2a", carry=("gather", [
        bft(ffn1_w_in[1]), bf(ffn1_w_out[1]), bf(swa_w_q[0]), bf(swa_w_o[0])]))
    kvn, kv_rot, kv_rot_t = _norm_proj(h3, kv_norm, w_kv, NN, "kv_proj", rope=(cos_t, sin_t, KV // (2 * LANES)),
                                       tail_t=KV)
    h4, sv_b1, (win2b_t, wo2b) = _ffn_fwd(h3, ffn1_norm[1], win1b_t, wo1b, "ffn1b", carry=("gather", [
        bft(ffn2_w_in[1]), bf(ffn2_w_out[1])]))
    hn_b, q_rot = _norm_proj(h4, mix_norm[1], w_q, NN, "swa_q", rope=(cos_t, sin_t, D // LANES))
    o_sw = _swa_fwd(q_rot, kv_rot, kv_rot_t, swa_sinks, "swa_attn")
    h5 = _proj(o_sw, w_swo, NN, F32, "swa_out", res=h4)
    h6, sv_b2, _ = _ffn_fwd(h5, ffn2_norm[1], win2b_t, wo2b, "ffn2b")
    dh6, dg_final, sq_err = _final_loss(h6, final_norm, tgt, "final_loss")
    loss = lax.psum(0.5 * jnp.sum(sq_err) / D, ("x", "y", "c"))

    dh5, dg_f2b, dwin2b_t, dwo2b, _, _ = _ffn_bwd(dh6, h5, ffn2_norm[1], win2b_t, wo2b, sv_b2, "ffn2b")
    do_sw = _proj(dh5, w_swo, NT, BF16, "swa_out_dx")
    dw_swo = _proj_dw(o_sw, dh5, "swa_out_dw")
    (dq, dk_sw, dv_sw, dsink), (p_win2b, p_swo) = _swa_bwd(
        q_rot, kv_rot, kv_rot_t, swa_sinks, o_sw, do_sw, cos_t, sin_t, "swa_attn_bwd",
        carry=("scatter", [dwin2b_t, dw_swo]))
    dw_q = _proj_dw(hn_b, dq, "swa_q_dw")
    (dh4, dg_mix_b), _ = _dx_norm_bwd([(dq, w_q, NT, 1, 0)], h4, mix_norm[1], dh5, "swa_q_dx")
    dh3, dg_f1b, dwin1b_t, dwo1b, (p_q, p_wo2b), _ = _ffn_bwd(dh4, h3, ffn1_norm[1], win1b_t, wo1b, sv_b1, "ffn1b",
                                                              carry=("scatter", [dw_q, dwo2b]))
    dkv = _rotary(jnp.concatenate([dk_sw, dv_sw], axis=1), cos_t, sin_t, KV // (2 * LANES), True, "kv_rope_bwd")
    dw_kv = _proj_dw(kvn, dkv, "kv_proj_dw")
    (dh3, dg_kv), _ = _dx_norm_bwd([(dkv, w_kv, NT, 1, 0)], h3, kv_norm, dh3, "kv_proj_dx")
    dh2, dg_f2a, dwin2a_t, dwo2a, (p_win1b, p_kv), (p_wo1b,) = _ffn_bwd(
        dh3, h2, ffn2_norm[0], win2a_t, wo2a, sv_a2, "ffn2a",
        carry=("scatter", [dwin1b_t, dw_kv]), carry_dwin=("scatter", [dwo1b]))
    do_sb = _proj(dh2, w_sbo, NT, BF16, "sb_out_dx")
    dw_sbo = _proj_dw(o_sb, dh2, "sb_out_dw")
    (dq_sb, dk_sb, dv_sb), (p_win2a, p_wo2a, p_sbo) = _sb_bwd(
        qkv, kv_t, o_sb, do_sb, "sb_attn_bwd", carry=("scatter", [dwin2a_t, dwo2a, dw_sbo]))
    dqkv = [dq_sb, dk_sb, dv_sb]
    dwqkv_t, _ = _dw_rows(dqkv, hn_a, "sb_qkv_dw", tk=TN_CHUNK // 2)
    (dh1, dg_mix_a), (p_qkv,) = _dx_norm_bwd([(dy, wqkv_t, NN, 3, n) for n, dy in enumerate(dqkv)], h1, mix_norm[0],
                                             dh2, "sb_qkv_dx", carry=("scatter", [dwqkv_t]))
    dx, dg_f1a, p_win1a, p_wo1a = _ffn_bwd(dh1, h0, ffn1_norm[0], win1a_t, wo1a, sv_a1, "ffn1a", scatter=True)

    def natural(parts, tag):
        return _sum8(parts, f"sum_{tag}")

    def from_t(parts, tag):
        return jnp.transpose(_sum8(parts, f"sum_{tag}"))

    grads = {
        "ffn1_w_in": jnp.stack([from_t(p_win1a, "win1a"), from_t(p_win1b, "win1b")]),
        "ffn1_w_out": jnp.stack([natural(p_wo1a, "wo1a"), natural(p_wo1b, "wo1b")]),
        "ffn2_w_in": jnp.stack([from_t(p_win2a, "win2a"), from_t(p_win2b, "win2b")]),
        "ffn2_w_out": jnp.stack([natural(p_wo2a, "wo2a"), natural(p_wo2b, "wo2b")]),
        "sb_w_qkv": from_t(p_qkv, "qkv")[None],
        "sb_w_o": natural(p_sbo, "sbo")[None],
        "kv_w": natural(p_kv, "kv"),
        "swa_w_q": natural(p_q, "swq")[None],
        "swa_w_o": natural(p_swo, "swo")[None],
    }

    small_w = [ffn1_norm, mix_norm, ffn2_norm, kv_norm, final_norm, swa_sinks]
    small_m = [m_ffn1_norm, m_mix_norm, m_ffn2_norm, m_kv_norm, m_final_norm, m_swa_sinks]
    small_v = [v_ffn1_norm, v_mix_norm, v_ffn2_norm, v_kv_norm, v_final_norm, v_swa_sinks]
    SMALL_ROWS = 16

    def pack_small(ts):
        rows_ = [t.reshape(-1, D) for t in ts[:-1]]
        sink_row = jnp.pad(ts[-1].reshape(1, -1), ((0, 0), (0, D - ts[-1].size)))
        flat = jnp.concatenate(rows_ + [sink_row], axis=0)
        return jnp.pad(flat, ((0, SMALL_ROWS - flat.shape[0]), (0, 0)))

    def unpack_small(flat):
        out, r = [], 0
        for t in small_w[:-1]:
            n = t.size // D
            out.append(flat[r:r + n].reshape(t.shape))
            r += n
        out.append(flat[r, :swa_sinks.size].reshape(swa_sinks.shape))
        return out

    def gain(parts8):
        return jnp.sum(parts8, axis=0, keepdims=True)

    g_small_local = pack_small([
        jnp.concatenate([gain(dg_f1a), gain(dg_f1b)], axis=0),
        jnp.concatenate([gain(dg_mix_a), gain(dg_mix_b)], axis=0),
        jnp.concatenate([gain(dg_f2a), gain(dg_f2b)], axis=0),
        gain(dg_kv), gain(dg_final), jnp.sum(dsink, axis=-1).reshape(1, -1)])
    small_parts = _exchange("gather", [g_small_local], "gather_small_grads")[0]
    g_small = _sum8(small_parts.reshape(N_DEV, SMALL_ROWS, D), "sum_small")
    d_small, nm_small, nv_small = _adamw(g_small, pack_small(small_w), pack_small(small_m), pack_small(small_v), "adamw_small")
    small_names = ["ffn1_norm", "mix_norm", "ffn2_norm", "kv_norm", "final_norm", "swa_sinks"]
    result = {"grad": dict(zip(small_names, unpack_small(g_small))),
              "delta": dict(zip(small_names, unpack_small(d_small))),
              "new_m": dict(zip(small_names, unpack_small(nm_small))),
              "new_v": dict(zip(small_names, unpack_small(nv_small)))}

    big = {"ffn1_w_in": (ffn1_w_in, m_ffn1_w_in, v_ffn1_w_in), "ffn1_w_out": (ffn1_w_out, m_ffn1_w_out, v_ffn1_w_out),
           "ffn2_w_in": (ffn2_w_in, m_ffn2_w_in, v_ffn2_w_in), "ffn2_w_out": (ffn2_w_out, m_ffn2_w_out, v_ffn2_w_out),
           "sb_w_qkv": (sb_w_qkv, m_sb_w_qkv, v_sb_w_qkv), "sb_w_o": (sb_w_o, m_sb_w_o, v_sb_w_o),
           "kv_w": (kv_w, m_kv_w, v_kv_w), "swa_w_q": (swa_w_q, m_swa_w_q, v_swa_w_q),
           "swa_w_o": (swa_w_o, m_swa_w_o, v_swa_w_o)}
    for nm, (w, m, v) in big.items():
        g = grads[nm]
        two_d = lambda t: t.reshape(-1, t.shape[-1])
        d, new_m, new_v = _adamw(two_d(g), two_d(w), two_d(m), two_d(v), f"adamw_{nm}")
        result["grad"][nm] = g
        result["delta"][nm] = d.reshape(w.shape)
        result["new_m"][nm] = new_m.reshape(w.shape)
        result["new_v"][nm] = new_v.reshape(w.shape)

    order = ["ffn1_norm", "ffn1_w_in", "ffn1_w_out", "mix_norm", "ffn2_norm", "ffn2_w_in", "ffn2_w_out",
             "sb_w_qkv", "sb_w_o", "kv_norm", "kv_w", "swa_w_q", "swa_sinks", "swa_w_o", "final_norm"]
    outs = [result[kind][nm] for kind in ("grad", "delta", "new_m", "new_v") for nm in order]
    return (loss, dx.reshape(x.shape), *outs)
```

```python
import jax
import jax.numpy as jnp
from jax import lax
from jax.experimental import pallas as pl
from jax.experimental.pallas import tpu as pltpu

F32 = jnp.float32
BF16 = jnp.bfloat16

N_DEV = 8
HEAD_DIM = 64
LANES = 128
BLK = 128
RMS_EPS = 1e-6
FFN_RES_SCALE = 0.5
ROPE_THETA = 10000.0
ATTN_SCALE = HEAD_DIM ** -0.5
SB_LOG_FLOOR = -88.0
NEG_BIG = -1e30
VMEM_LIMIT_V7X = 56 * 1024 * 1024

ADAM_LR = 0.001
ADAM_B1 = 0.9
ADAM_B2 = 0.999
ADAM_EPS = 1e-08
ADAM_WD = 0.01
ADAM_STEP = 10

NN = ((1,), (0,))
NT = ((1,), (1,))
TN = ((0,), (0,))
TN_CHUNK = 2048
MESH = pl.DeviceIdType.MESH


def _dot(a, b, dims):
    return lax.dot_general(a, b, (dims, ((), ())), preferred_element_type=F32)


def _tile(n, pref, mult=LANES):
    if n <= pref:
        return n
    t = (pref // mult) * mult
    while t >= mult:
        if n % t == 0:
            return t
        t -= mult
    return n


def _params(*sem):
    return pltpu.CompilerParams(dimension_semantics=sem, vmem_limit_bytes=VMEM_LIMIT_V7X)


def _mm(a, b, dims, out_dtype, name, scale=1.0, res=None, tm=512, tn=512, tk=512):
    if dims == NN:
        (M, K), (_, N) = a.shape, b.shape
    elif dims == NT:
        (M, K), (N, _) = a.shape, b.shape
    else:
        (K, M), (_, N) = a.shape, b.shape
    tm, tn, tk = _tile(M, tm), _tile(N, tn), _tile(K, tk)
    nk = K // tk
    if dims == TN:
        a_spec = pl.BlockSpec((tk, tm), lambda i, j, k: (k, i))
    else:
        a_spec = pl.BlockSpec((tm, tk), lambda i, j, k: (i, k))
    if dims == NT:
        b_spec = pl.BlockSpec((tn, tk), lambda i, j, k: (j, k))
    else:
        b_spec = pl.BlockSpec((tk, tn), lambda i, j, k: (k, j))
    o_spec = pl.BlockSpec((tm, tn), lambda i, j, k: (i, j))
    has_res = res is not None

    def body(*refs):
        a_ref, b_ref = refs[0], refs[1]
        r_ref = refs[2] if has_res else None
        o_ref = refs[3] if has_res else refs[2]

        def finish(acc):
            r = acc * scale if scale != 1.0 else acc
            if has_res:
                r = r + r_ref[...]
            o_ref[...] = r.astype(out_dtype)

        p = _dot(a_ref[...].astype(BF16), b_ref[...].astype(BF16), dims)
        if nk == 1:
            finish(p)
        else:
            acc_ref = refs[-1]
            k = pl.program_id(2)

            @pl.when(k == 0)
            def _():
                acc_ref[...] = p

            @pl.when(k > 0)
            def _():
                acc_ref[...] += p

            @pl.when(k == nk - 1)
            def _():
                finish(acc_ref[...])

    in_specs = [a_spec, b_spec] + ([o_spec] if has_res else [])
    args = (a, b) + ((res,) if has_res else ())
    return pl.pallas_call(
        body, name=name,
        out_shape=jax.ShapeDtypeStruct((M, N), out_dtype),
        grid=(M // tm, N // tn, nk),
        in_specs=in_specs, out_specs=o_spec,
        scratch_shapes=[pltpu.VMEM((tm, tn), F32)] if nk > 1 else [],
        compiler_params=_params("parallel", "parallel", "arbitrary"),
    )(*args)


def _rows8(x):
    r, d = x.shape
    return jnp.sum(x.reshape(r // 8, 8, d), axis=0)


def _norm_proj(h, g, w, dims, name, rope=None, tail_t=0):
    S, D = h.shape
    N = w.shape[1] if dims == NN else w.shape[0]
    tm = _tile(S, 512, 16)

    def body(h_ref, g_ref, w_ref, *rest):
        xn_ref, y_ref = rest[-3:-1] if tail_t else rest[-2:]
        x = h_ref[...]
        r = lax.rsqrt(jnp.mean(x * x, axis=-1, keepdims=True) + RMS_EPS)
        xn = ((x * r) * g_ref[...]).astype(BF16)
        xn_ref[...] = xn
        y = _dot(xn, w_ref[...], dims)
        if rope is not None:
            cs, sn = rest[0][...], rest[1][...]
            groups = [y[:, gidx * LANES:(gidx + 1) * LANES] for gidx in range(N // LANES)]
            y = jnp.concatenate([v * cs + _swap_halves(v) * sn if gidx < rope[2] else v
                                 for gidx, v in enumerate(groups)], axis=1)
        y_ref[...] = y.astype(BF16)
        if tail_t:
            rest[-1][...] = jnp.transpose(y[:, N - tail_t:]).astype(BF16)

    row = pl.BlockSpec((tm, D), lambda i: (i, 0))
    tab = pl.BlockSpec((tm, LANES), lambda i: (i, 0))
    in_specs = [row, pl.BlockSpec((1, D), lambda i: (0, 0)), pl.BlockSpec(w.shape, lambda i: (0, 0))]
    args = (h, g.reshape(1, D), w)
    if rope is not None:
        in_specs += [tab, tab]
        args += (rope[0], rope[1])
    out_shape = [jax.ShapeDtypeStruct((S, D), BF16), jax.ShapeDtypeStruct((S, N), BF16)]
    out_specs = [row, pl.BlockSpec((tm, N), lambda i: (i, 0))]
    if tail_t:
        out_shape.append(jax.ShapeDtypeStruct((tail_t, S), BF16))
        out_specs.append(pl.BlockSpec((tail_t, tm), lambda i: (0, i)))
    return pl.pallas_call(
        body, name=name, out_shape=out_shape, grid=(S // tm,),
        in_specs=in_specs, out_specs=out_specs,
        compiler_params=_params("parallel"),
    )(*args)


def _ffn_up(h, g, win_t, name, carry=None):
    S, D = h.shape
    F = win_t.shape[0] // 2
    tm, tn = _tile(S, 512, 16), _tile(F, 1408)
    nf = F // tn

    def body(h_ref, g_ref, wg_ref, wu_ref, xn_ref, silu_ref, dsilu_ref, up_ref, act_ref):
        x = h_ref[...]
        r = lax.rsqrt(jnp.mean(x * x, axis=-1, keepdims=True) + RMS_EPS)
        xn = ((x * r) * g_ref[...]).astype(BF16)
        xn_ref[...] = xn
        gate = _dot(xn, wg_ref[...], NT)
        up = _dot(xn, wu_ref[...], NT)
        sig = 1.0 / (1.0 + jnp.exp(-gate))
        silu = gate * sig
        up_ref[...] = up.astype(BF16)
        silu_ref[...] = silu.astype(BF16)
        dsilu_ref[...] = (sig + silu * (1.0 - sig)).astype(BF16)
        act_ref[...] = (silu * up).astype(BF16)

    row = pl.BlockSpec((tm, D), lambda i, j: (i, 0))
    blk = pl.BlockSpec((tm, tn), lambda i, j: (i, j))
    hid = jax.ShapeDtypeStruct((S, F), BF16)
    return _pcall(
        body, (h, g.reshape(1, D), win_t, win_t), name=name,
        out_shape=(jax.ShapeDtypeStruct((S, D), BF16), hid, hid, hid, hid),
        grid=(S // tm, nf),
        in_specs=[row, pl.BlockSpec((1, D), lambda i, j: (0, 0)),
                  pl.BlockSpec((tn, D), lambda i, j: (j, 0)),
                  pl.BlockSpec((tn, D), lambda i, j: (j + nf, 0))],
        out_specs=(row, blk, blk, blk, blk),
        sem=("arbitrary", "arbitrary"), carry=carry)


def _ffn_dact(dh, wo, silu, dsilu, up, name):
    S, D = dh.shape
    F = wo.shape[0]
    tm = _tile(S, 256, 16)

    def body(dh_ref, wo_hbm, s_ref, ds_ref, u_ref, dg_ref, du_ref, wo_v, sems):
        _load_resident([(wo_hbm, wo_v)], sems)
        d = _dot(dh_ref[...].astype(BF16), wo_v[...], NT) * FFN_RES_SCALE
        du_ref[...] = (d * s_ref[...].astype(F32)).astype(BF16)
        dg_ref[...] = (d * u_ref[...].astype(F32) * ds_ref[...].astype(F32)).astype(BF16)

    wide = pl.BlockSpec((tm, F), lambda i: (i, 0))
    hid = jax.ShapeDtypeStruct((S, F), BF16)
    return pl.pallas_call(
        body, name=name, out_shape=(hid, hid),
        grid=(S // tm,),
        in_specs=[pl.BlockSpec((tm, D), lambda i: (i, 0)), pl.BlockSpec(memory_space=pl.ANY), wide, wide, wide],
        out_specs=(wide, wide),
        scratch_shapes=[pltpu.VMEM(wo.shape, BF16), pltpu.SemaphoreType.DMA((1,))],
        compiler_params=_params("arbitrary"),
    )(dh, wo, silu, dsilu, up)


def _dw_rows(srcs, x, name, carry=None, tk=TN_CHUNK):
    n = len(srcs)
    S, F = srcs[0].shape
    D = x.shape[1]
    tr, tk = _tile(F, 1408), _tile(S, tk, 16)
    nf, nk = F // tr, S // tk

    def body(*refs):
        src_refs, (x_ref, o_ref, acc_ref) = refs[:n], refs[n:]
        r, k = pl.program_id(0), pl.program_id(1)
        for s in range(n):
            @pl.when(r // nf == s)
            def _():
                p = _dot(src_refs[s][...].astype(BF16), x_ref[...], TN)

                @pl.when(k == 0)
                def _():
                    acc_ref[...] = p

                @pl.when(k > 0)
                def _():
                    acc_ref[...] += p

        @pl.when(k == nk - 1)
        def _():
            o_ref[...] = acc_ref[...].astype(BF16)

    def src_spec(s):
        return pl.BlockSpec((tk, tr), lambda r, k: (jnp.where(r // nf == s, k, 0), jnp.clip(r - s * nf, 0, nf - 1)))

    return _pcall(
        body, (*srcs, x), name=name, out_shape=jax.ShapeDtypeStruct((n * F, D), BF16),
        grid=(n * nf, nk),
        in_specs=[src_spec(s) for s in range(n)] + [pl.BlockSpec((tk, D), lambda r, k: (k, 0))],
        out_specs=pl.BlockSpec((tr, D), lambda r, k: (r, 0)),
        scratch_shapes=[pltpu.VMEM((tr, D), F32)],
        sem=("arbitrary", "arbitrary"), carry=carry)


def _dx_norm_bwd(terms, h, g, res, name, carry=None, tm=256):
    S, D = h.shape
    tm = _tile(S, tm, 16)
    n = len(terms)

    def body(*refs):
        dy_refs, w_refs = refs[:n], refs[n:2 * n]
        h_ref, g_ref, r_ref, dh_ref, dg_ref = refs[2 * n:]
        d = _dot(dy_refs[0][...].astype(BF16), w_refs[0][...], terms[0][2])
        for t in range(1, n):
            d = d + _dot(dy_refs[t][...].astype(BF16), w_refs[t][...], terms[t][2])
        x = h_ref[...]
        r = lax.rsqrt(jnp.mean(x * x, axis=-1, keepdims=True) + RMS_EPS)
        xhat = x * r
        dxh = d * g_ref[...]
        c = jnp.mean(dxh * xhat, axis=-1, keepdims=True)
        dh_ref[...] = r * (dxh - xhat * c) + r_ref[...]
        part = _rows8(d * xhat)

        @pl.when(pl.program_id(0) == 0)
        def _():
            dg_ref[...] = part

        @pl.when(pl.program_id(0) > 0)
        def _():
            dg_ref[...] += part

    def w_spec(w, nblk, blk):
        return pl.BlockSpec((w.shape[0] // nblk, w.shape[1]), lambda i: (blk, 0))

    row = pl.BlockSpec((tm, D), lambda i: (i, 0))
    in_specs = [pl.BlockSpec((tm, t[0].shape[1]), lambda i: (i, 0)) for t in terms]
    in_specs += [w_spec(t[1], t[3], t[4]) for t in terms]
    in_specs += [row, pl.BlockSpec((1, D), lambda i: (0, 0)), row]
    return _pcall(
        body, (*[t[0] for t in terms], *[t[1] for t in terms], h, g.reshape(1, D), res), name=name,
        out_shape=(jax.ShapeDtypeStruct((S, D), F32), jax.ShapeDtypeStruct((8, D), F32)),
        grid=(S // tm,),
        in_specs=in_specs,
        out_specs=(row, pl.BlockSpec((8, D), lambda i: (0, 0))),
        sem=("arbitrary",), carry=carry)


def _load_resident(pairs, sems):
    @pl.when(pl.program_id(0) == 0)
    def _():
        copies = [pltpu.make_async_copy(src, dst, sems.at[n]) for n, (src, dst) in enumerate(pairs)]
        for cp in copies:
            cp.start()
        for cp in copies:
            cp.wait()


def _loss_tail(y_in, g, tgt):
    D = y_in.shape[-1]
    r = lax.rsqrt(jnp.mean(y_in * y_in, axis=-1, keepdims=True) + RMS_EPS)
    xhat = y_in * r
    err = xhat * g - tgt
    d = err * (1.0 / D)
    dxh = d * g
    c = jnp.mean(dxh * xhat, axis=-1, keepdims=True)
    return r * (dxh - xhat * c), _rows8(d * xhat), _rows8(err * err)


def _ffn_fwd_fused(h, g, win_t, wo, name, carry=None, loss=None):
    S, D = h.shape
    F = wo.shape[0]
    tm = _tile(S, 256, 16)
    n_head = 3 if loss is not None else 1

    def body(h_ref, g_ref, win_hbm, wo_hbm, *rest):
        lead, (xn_ref, silu_ref, dsilu_ref, up_ref, act_ref, win_v, wo_v, sems) = rest[:-8], rest[-8:]
        _load_resident([(win_hbm, win_v), (wo_hbm, wo_v)], sems)
        x = h_ref[...]
        r = lax.rsqrt(jnp.mean(x * x, axis=-1, keepdims=True) + RMS_EPS)
        xn = ((x * r) * g_ref[...]).astype(BF16)
        xn_ref[...] = xn
        gate = _dot(xn, win_v[:F, :], NT)
        up = _dot(xn, win_v[F:, :], NT)
        sig = 1.0 / (1.0 + jnp.exp(-gate))
        silu = gate * sig
        act = (silu * up).astype(BF16)
        up_ref[...] = up.astype(BF16)
        silu_ref[...] = silu.astype(BF16)
        dsilu_ref[...] = (sig + silu * (1.0 - sig)).astype(BF16)
        act_ref[...] = act
        out = x + FFN_RES_SCALE * _dot(act, wo_v[...], NN)
        if loss is None:
            lead[0][...] = out
        else:
            gf_ref, t_ref, dy_ref, dgf_ref, sq_ref = lead
            dy, dgf, sq = _loss_tail(out, gf_ref[...], t_ref[...])
            dy_ref[...] = dy

            @pl.when(pl.program_id(0) == 0)
            def _():
                dgf_ref[...] = dgf
                sq_ref[...] = sq

            @pl.when(pl.program_id(0) > 0)
            def _():
                dgf_ref[...] += dgf
                sq_ref[...] += sq

    row = pl.BlockSpec((tm, D), lambda i: (i, 0))
    vec = pl.BlockSpec((1, D), lambda i: (0, 0))
    acc = pl.BlockSpec((8, D), lambda i: (0, 0))
    wide = pl.BlockSpec((tm, F), lambda i: (i, 0))
    hbm = pl.BlockSpec(memory_space=pl.ANY)
    hid = jax.ShapeDtypeStruct((S, F), BF16)
    full = jax.ShapeDtypeStruct((S, D), F32)
    part = jax.ShapeDtypeStruct((8, D), F32)
    args, in_specs = (h, g.reshape(1, D), win_t, wo), [row, vec, hbm, hbm]
    lead_shapes, lead_specs = (full,), (row,)
    if loss is not None:
        args, in_specs = args + (loss[0].reshape(1, D), loss[1]), in_specs + [vec, row]
        lead_shapes, lead_specs = (full, part, part), (row, acc, acc)
    res, got = _pcall(
        body, args, name=name,
        out_shape=lead_shapes + (jax.ShapeDtypeStruct((S, D), BF16), hid, hid, hid, hid),
        grid=(S // tm,),
        in_specs=in_specs,
        out_specs=lead_specs + (row, wide, wide, wide, wide),
        scratch_shapes=[pltpu.VMEM(win_t.shape, BF16), pltpu.VMEM(wo.shape, BF16), pltpu.SemaphoreType.DMA((2,))],
        sem=("arbitrary",), carry=carry)
    first = res[0] if loss is None else tuple(res[:3])
    return first, tuple(res[n_head:]), got


def _ffn_bwd_fused(dh, h, g, win_t, wo, silu, dsilu, up, name, carry=None):
    S, D = h.shape
    F = wo.shape[0]
    tm = _tile(S, 256, 16)

    def body(dh_ref, h_ref, g_ref, s_ref, ds_ref, u_ref, win_hbm, wo_hbm,
             dhin_ref, dgain_ref, dgate_ref, dup_ref, win_v, wo_v, sems):
        _load_resident([(win_hbm, win_v), (wo_hbm, wo_v)], sems)
        dhv = dh_ref[...]
        d = _dot(dhv.astype(BF16), wo_v[...], NT) * FFN_RES_SCALE
        dup = (d * s_ref[...].astype(F32)).astype(BF16)
        dgate = (d * u_ref[...].astype(F32) * ds_ref[...].astype(F32)).astype(BF16)
        dup_ref[...] = dup
        dgate_ref[...] = dgate
        dxn = _dot(dgate, win_v[:F, :], NN) + _dot(dup, win_v[F:, :], NN)
        x = h_ref[...]
        r = lax.rsqrt(jnp.mean(x * x, axis=-1, keepdims=True) + RMS_EPS)
        xhat = x * r
        dxh = dxn * g_ref[...]
        c = jnp.mean(dxh * xhat, axis=-1, keepdims=True)
        dhin_ref[...] = r * (dxh - xhat * c) + dhv
        part = _rows8(dxn * xhat)

        @pl.when(pl.program_id(0) == 0)
        def _():
            dgain_ref[...] = part

        @pl.when(pl.program_id(0) > 0)
        def _():
            dgain_ref[...] += part

    row = pl.BlockSpec((tm, D), lambda i: (i, 0))
    wide = pl.BlockSpec((tm, F), lambda i: (i, 0))
    hbm = pl.BlockSpec(memory_space=pl.ANY)
    hid = jax.ShapeDtypeStruct((S, F), BF16)
    return _pcall(
        body, (dh, h, g.reshape(1, D), silu, dsilu, up, win_t, wo), name=name,
        out_shape=(jax.ShapeDtypeStruct((S, D), F32), jax.ShapeDtypeStruct((8, D), F32), hid, hid),
        grid=(S // tm,),
        in_specs=[row, row, pl.BlockSpec((1, D), lambda i: (0, 0)), wide, wide, wide, hbm, hbm],
        out_specs=(row, pl.BlockSpec((8, D), lambda i: (0, 0)), wide, wide),
        scratch_shapes=[pltpu.VMEM(win_t.shape, BF16), pltpu.VMEM(wo.shape, BF16), pltpu.SemaphoreType.DMA((2,))],
        sem=("arbitrary",), carry=carry)


def _rope_tables(S):
    half = HEAD_DIM // 2
    inv_freq = ROPE_THETA ** (-jnp.arange(half, dtype=F32) / half)
    ang = jnp.arange(S).astype(F32)[:, None] * inv_freq[None, :]
    cos, sin = jnp.cos(ang), jnp.sin(ang)
    cos_t = jnp.tile(cos, (1, LANES // half))
    sin_t = jnp.tile(jnp.concatenate([-sin, sin], axis=1), (1, LANES // HEAD_DIM))
    return cos_t, sin_t


def _swap_halves(x):
    lane = lax.broadcasted_iota(jnp.int32, x.shape, 1)
    first = (lane % HEAD_DIM) < (HEAD_DIM // 2)
    return jnp.where(first, pltpu.roll(x, LANES - HEAD_DIM // 2, 1), pltpu.roll(x, HEAD_DIM // 2, 1))


def _rotary(x, cos_t, sin_t, n_rot, inverse, name):
    S, C = x.shape
    ts = _tile(S, 512, 16)
    ng = C // LANES

    def body(x_ref, c_ref, s_ref, o_ref):
        cs, sn = c_ref[...], s_ref[...]
        for gidx in range(ng):
            sl = slice(gidx * LANES, (gidx + 1) * LANES)
            v = x_ref[:, sl].astype(F32)
            if gidx < n_rot:
                if inverse:
                    v = v * cs + _swap_halves(v * sn)
                else:
                    v = v * cs + _swap_halves(v) * sn
            o_ref[:, sl] = v.astype(BF16)

    row = pl.BlockSpec((ts, C), lambda i: (i, 0))
    tab = pl.BlockSpec((ts, LANES), lambda i: (i, 0))
    return pl.pallas_call(
        body, name=name, out_shape=jax.ShapeDtypeStruct((S, C), BF16),
        grid=(S // ts,), in_specs=[row, tab, tab], out_specs=row,
        compiler_params=_params("parallel"),
    )(x, cos_t, sin_t)


def _head_masks():
    lane = lax.broadcasted_iota(jnp.int32, (BLK, LANES), 1)
    return lane < HEAD_DIM


def _split_bf16(x):
    hi = x.astype(BF16)
    lo = (x - hi.astype(F32)).astype(BF16)
    return hi, lo


def _sb_scores(qh, ks, carry, diag, tri_excl, strict):
    n_heads = len(qh)
    zs = [_dot(ks[n], qh[n], NT) for n in range(n_heads)]
    a_l, b_l, split_l = [], [], []
    for z in zs:
        a = jnp.minimum(z, 0.0) - jnp.log(1.0 + jnp.exp(-jnp.abs(z)))
        b = a - z
        if diag:
            b = jnp.where(strict, b, 0.0)
        a_l.append(a)
        b_l.append(b)
        split_l.append(_split_bf16(b))
    sufs = [_dot(tri_excl, hi, NN) + _dot(tri_excl, lo, NN) for hi, lo in split_l]
    w_l = []
    for n in range(n_heads):
        w = jnp.exp(a_l[n] + sufs[n] + carry[n])
        if diag:
            w = jnp.where(strict, w, 0.0)
        w_l.append(w)
    return a_l, b_l, w_l


SB_FWD_PAIRS = 4
SB_FWD_QBLOCKS = 2
SB_BWD_PAIRS = 2
SB_BWD_QBLOCKS = 4


def _any_alive(carries):
    top = carries[0]
    for c in carries[1:]:
        top = jnp.maximum(top, c)
    return (jnp.max(top) > SB_LOG_FLOOR).astype(jnp.int32)


def _sb_masks():
    row = lax.broadcasted_iota(jnp.int32, (BLK, BLK), 0)
    col = lax.broadcasted_iota(jnp.int32, (BLK, BLK), 1)
    tri_excl = jnp.where(col > row, 1.0, 0.0).astype(BF16)
    tri_incl = jnp.where(col >= row, 1.0, 0.0).astype(BF16)
    return row < HEAD_DIM, row < col, tri_excl, tri_incl


def _sb_fwd(qkv, kv_t, name, carry=None):
    S, D3 = qkv.shape
    D = D3 // 3
    npair, nb = D // LANES, S // BLK
    P = min(SB_FWD_PAIRS, npair)
    ngroup = npair // P
    W = P * LANES

    QB = SB_FWD_QBLOCKS if nb % SB_FWD_QBLOCKS == 0 else 1
    nch = QB * 2 * P

    def body(q_ref, k_ref, vt_ref, o_ref):
        i_first = pl.program_id(1) * QB
        m0 = _head_masks()
        top, strict, tri_excl, _ = _sb_masks()
        zq = jnp.zeros((BLK, LANES), BF16)
        lanes = [slice(p * LANES, (p + 1) * LANES) for p in range(P)]
        qh = []
        for qb in range(QB):
            for sl in lanes:
                q2 = q_ref[qb * BLK:(qb + 1) * BLK, sl] * ATTN_SCALE
                qh += [jnp.where(m0, q2, zq), jnp.where(m0, zq, q2)]

        def block(js, carry, acc, diag):
            offs = [pl.multiple_of(j * BLK, BLK) for j in js]
            ks, vth = [], []
            for qb in range(QB):
                for sl in lanes:
                    k2 = k_ref[pl.ds(offs[qb], BLK), sl]
                    vt = vt_ref[sl, pl.ds(offs[qb], BLK)]
                    ks += [k2, k2]
                    vth += [jnp.where(top, vt, zq), jnp.where(top, zq, vt)]
            _, b_l, w_l = _sb_scores(qh, ks, carry, diag, tri_excl, strict)
            wb = [w.astype(BF16) for w in w_l]
            new_acc = [acc[m] + _dot(vth[2 * m], wb[2 * m], NN) + _dot(vth[2 * m + 1], wb[2 * m + 1], NN)
                       for m in range(QB * P)]
            new_carry = [carry[n] + jnp.sum(b_l[n], axis=0, keepdims=True) for n in range(nch)]
            return new_carry, new_acc

        c0 = jnp.zeros((1, BLK), F32)
        carry, acc = block([i_first + qb for qb in range(QB)], [c0] * nch,
                           [jnp.zeros((LANES, BLK), F32)] * (QB * P), True)

        def cond(st):
            return jnp.logical_and(i_first + QB - 1 - st[0] >= 0, st[1] > 0)

        def step(st):
            t, _, carry, acc = st
            js = [i_first + qb - t for qb in range(QB)]
            carry = [carry[n] if n // (2 * P) == QB - 1 else jnp.where(js[n // (2 * P)] >= 0, carry[n], NEG_BIG)
                     for n in range(nch)]
            carry, acc = block([jnp.maximum(j, 0) for j in js], carry, acc, False)
            return t + 1, _any_alive(carry), carry, acc

        st = lax.while_loop(cond, step, (1, _any_alive(carry), carry, acc))
        for qb in range(QB):
            for p, sl in enumerate(lanes):
                o_ref[qb * BLK:(qb + 1) * BLK, sl] = jnp.transpose(st[3][qb * P + p])

    return _pcall(
        body, (qkv, qkv, kv_t), name=name, out_shape=jax.ShapeDtypeStruct((S, D), F32),
        grid=(ngroup, nb // QB),
        in_specs=[pl.BlockSpec((QB * BLK, W), lambda g, i: (i, g)),
                  pl.BlockSpec((S, W), lambda g, i: (0, ngroup + g)),
                  pl.BlockSpec((W, S), lambda g, i: (ngroup + g, 0))],
        out_specs=pl.BlockSpec((QB * BLK, W), lambda g, i: (i, g)),
        sem=("arbitrary", "arbitrary"), carry=carry)


def _sb_bwd(qkv, kv_t, o, do, name, carry=None):
    S, D3 = qkv.shape
    D = D3 // 3
    npair, nb = D // LANES, S // BLK
    P = min(SB_BWD_PAIRS, npair)
    ngroup = npair // P
    W = P * LANES

    QB = SB_BWD_QBLOCKS if nb % SB_BWD_QBLOCKS == 0 else 1
    nch = QB * 2 * P

    def body(q_ref, o_ref, do_ref, qkv_hbm, kt_hbm, dq_ref, dk_ref, dv_ref, k_ref, v_ref, kt_ref, sems):
        grp = pl.program_id(0)
        i_first = pl.program_id(1) * QB
        m0 = _head_masks()
        top, strict, tri_excl, tri_incl = _sb_masks()
        zq = jnp.zeros((BLK, LANES), BF16)
        lanes = [slice(p * LANES, (p + 1) * LANES) for p in range(P)]

        @pl.when(pl.program_id(1) == 0)
        def _():
            copies = [pltpu.make_async_copy(qkv_hbm.at[:, pl.ds(pl.multiple_of((c * ngroup + grp) * W, LANES), W)],
                                            ref, sems.at[c - 1]) for c, ref in ((1, k_ref), (2, v_ref))]
            copies.append(pltpu.make_async_copy(kt_hbm.at[pl.ds(pl.multiple_of(grp * W, LANES), W), :],
                                                kt_ref, sems.at[2]))
            for cp in copies:
                cp.start()
            dk_ref[...] = jnp.zeros_like(dk_ref)
            dv_ref[...] = jnp.zeros_like(dv_ref)
            for cp in copies:
                cp.wait()

        qh, doh, delta = [], [], []
        for qb in range(QB):
            rs = slice(qb * BLK, (qb + 1) * BLK)
            for sl in lanes:
                q2, do2 = q_ref[rs, sl] * ATTN_SCALE, do_ref[rs, sl]
                qh += [jnp.where(m0, q2, zq), jnp.where(m0, zq, q2)]
                doh += [jnp.where(m0, do2, zq), jnp.where(m0, zq, do2)]
                prod_t = jnp.transpose(do2.astype(F32) * o_ref[rs, sl])
                delta += [jnp.sum(jnp.where(top, prod_t, 0.0), axis=0, keepdims=True),
                          jnp.sum(jnp.where(top, 0.0, prod_t), axis=0, keepdims=True)]

        def block(js, valid, cb, cg, dq, diag):
            offs = [pl.multiple_of(j * BLK, BLK) for j in js]
            ks, vs, kth = [], [], []
            for qb in range(QB):
                for sl in lanes:
                    k2, v2 = k_ref[pl.ds(offs[qb], BLK), sl], v_ref[pl.ds(offs[qb], BLK), sl]
                    ks += [k2, k2]
                    vs += [v2, v2]
                    kt = kt_ref[sl, pl.ds(offs[qb], BLK)] * ATTN_SCALE
                    kth += [jnp.where(top, kt, zq), jnp.where(top, zq, kt)]
            dws = [_dot(vs[n], doh[n], NT) for n in range(nch)]
            a_l, b_l, w_l = _sb_scores(qh, ks, cb, diag, tri_excl, strict)
            wb = [w.astype(BF16) for w in w_l]
            g_l = [dws[n] * wb[n].astype(F32) for n in range(nch)]
            gsplit = [_split_bf16(g) for g in g_l]
            gincs = [_dot(tri_incl, hi, NN) + _dot(tri_incl, lo, NN) for hi, lo in gsplit]
            dzs = []
            for n in range(nch):
                beta = jnp.exp(a_l[n])
                dz = g_l[n] - beta * (g_l[n] + ((delta[n] - cg[n]) - gincs[n]))
                if diag:
                    dz = jnp.where(strict, dz, 0.0)
                if valid[n // (2 * P)] is not None:
                    dz = jnp.where(valid[n // (2 * P)], dz, 0.0)
                dzs.append(dz.astype(BF16))
            ndq = []
            for qb in range(QB):
                for p, sl in enumerate(lanes):
                    n0 = qb * 2 * P + 2 * p
                    ndq.append(dq[qb * P + p] + _dot(kth[n0], dzs[n0], NN) + _dot(kth[n0 + 1], dzs[n0 + 1], NN))
                    dk_ref[pl.ds(offs[qb], BLK), sl] += _dot(dzs[n0], qh[n0], NN) + _dot(dzs[n0 + 1], qh[n0 + 1], NN)
                    dv_ref[pl.ds(offs[qb], BLK), sl] += _dot(wb[n0], doh[n0], NN) + _dot(wb[n0 + 1], doh[n0 + 1], NN)
            ncb = [cb[n] + jnp.sum(b_l[n], axis=0, keepdims=True) for n in range(nch)]
            ncg = [cg[n] + jnp.sum(g_l[n], axis=0, keepdims=True) for n in range(nch)]
            return ncb, ncg, ndq

        c0 = jnp.zeros((1, BLK), F32)
        cb, cg, dq = block([i_first + qb for qb in range(QB)], [None] * QB, [c0] * nch, [c0] * nch,
                           [jnp.zeros((LANES, BLK), F32)] * (QB * P), True)

        def cond(st):
            return jnp.logical_and(i_first + QB - 1 - st[0] >= 0, st[1] > 0)

        def step(st):
            t, _, cb, cg, dq = st
            js = [i_first + qb - t for qb in range(QB)]
            valid = [js[qb] >= 0 for qb in range(QB - 1)] + [None]
            cb = [cb[n] if valid[n // (2 * P)] is None else jnp.where(valid[n // (2 * P)], cb[n], NEG_BIG)
                  for n in range(nch)]
            cb, cg, dq = block([jnp.maximum(j, 0) for j in js], valid, cb, cg, dq, False)
            return t + 1, _any_alive(cb), cb, cg, dq

        st = lax.while_loop(cond, step, (1, _any_alive(cb), cb, cg, dq))
        for qb in range(QB):
            for p, sl in enumerate(lanes):
                dq_ref[qb * BLK:(qb + 1) * BLK, sl] = jnp.transpose(st[4][qb * P + p]).astype(BF16)

    blk = pl.BlockSpec((QB * BLK, W), lambda g, i: (i, g))
    col_all = pl.BlockSpec((S, W), lambda g, i: (0, g))
    hbm = pl.BlockSpec(memory_space=pl.ANY)
    return _pcall(
        body, (qkv, o, do, qkv, kv_t), name=name,
        out_shape=(jax.ShapeDtypeStruct((S, D), BF16), jax.ShapeDtypeStruct((S, D), F32),
                   jax.ShapeDtypeStruct((S, D), F32)),
        grid=(ngroup, nb // QB),
        in_specs=[blk, blk, blk, hbm, hbm],
        out_specs=(blk, col_all, col_all),
        scratch_shapes=[pltpu.VMEM((S, W), BF16), pltpu.VMEM((S, W), BF16), pltpu.VMEM((W, S), BF16),
                        pltpu.SemaphoreType.DMA((3,))],
        sem=("arbitrary", "arbitrary"), carry=carry)


SWA_Q_GROUPS = 4


def _roll_heads(x):
    return pltpu.roll(x.astype(F32), HEAD_DIM, 1).astype(BF16)


def _roll_rows(x):
    return pltpu.roll(x.astype(F32), HEAD_DIM, 0).astype(BF16)


def _swa_valid(i):
    k = lax.broadcasted_iota(jnp.int32, (2 * BLK, BLK), 0)
    q = lax.broadcasted_iota(jnp.int32, (2 * BLK, BLK), 1)
    diff = q + BLK - k
    return (diff >= 0) & (diff < BLK) & ((i > 0) | (k >= BLK))


def _swa_probs(z, valid, sink):
    z = jnp.where(valid, z * ATTN_SCALE, NEG_BIG)
    mx = jnp.maximum(jnp.max(z, axis=0, keepdims=True), sink)
    p = jnp.exp(z - mx)
    ps = jnp.exp(sink - mx)
    inv = 1.0 / (jnp.sum(p, axis=0, keepdims=True) + ps)
    return p * inv, ps * inv


def _swa_operands(q_ref, kc_ref, kp_ref, vc_ref, vp_ref, tc_ref, tp_ref, s_ref, nkvp):
    m0 = _head_masks()
    top = lax.broadcasted_iota(jnp.int32, (LANES, 2 * BLK), 0) < HEAD_DIM
    heads = []
    for m in range(nkvp):
        pair = slice(m * LANES, (m + 1) * LANES)
        kk = jnp.concatenate([kp_ref[:, pair], kc_ref[:, pair]], axis=0)
        vv = jnp.concatenate([vp_ref[:, pair], vc_ref[:, pair]], axis=0)
        tt = jnp.concatenate([tp_ref[pair, :], tc_ref[pair, :]], axis=1)
        ksw, vsw, tsw = _roll_heads(kk), _roll_heads(vv), _roll_rows(tt)
        zt = jnp.zeros_like(tt)
        for c in range(SWA_Q_GROUPS):
            q_lanes = slice((m * SWA_Q_GROUPS + c) * LANES, (m * SWA_Q_GROUPS + c + 1) * LANES)
            qc = q_ref[:, q_lanes]
            zq = jnp.zeros_like(qc)
            for u in range(2):
                same = u == c // 2
                sel = (lambda x, z, mk: jnp.where(mk, x, z)) if u == 0 else (lambda x, z, mk: jnp.where(mk, z, x))
                heads.append(dict(
                    m=m, q_lanes=q_lanes, same=same, sel=sel, qm=sel(qc, zq, m0),
                    k=kk if same else ksw, v=vv if same else vsw,
                    tm=sel(tt if same else tsw, zt, top),
                    sink=s_ref[0, (m * SWA_Q_GROUPS + c) * 2 + u]))
    return heads, m0


def _swa_specs(D, half, t_block):
    prev = lambda i: jnp.maximum(i - 1, 0)
    return [pl.BlockSpec((BLK, D), lambda i: (i, 0)),
            pl.BlockSpec((BLK, half), lambda i: (i, 0)),
            pl.BlockSpec((BLK, half), lambda i: (prev(i), 0)),
            pl.BlockSpec((BLK, half), lambda i: (i, 1)),
            pl.BlockSpec((BLK, half), lambda i: (prev(i), 1)),
            pl.BlockSpec((half, BLK), lambda i: (t_block, i)),
            pl.BlockSpec((half, BLK), lambda i: (t_block, prev(i))),
            pl.BlockSpec(memory_space=pltpu.SMEM)]


def _swa_fwd(q, kv, kv_t, sinks, name):
    S, D = q.shape
    half = kv.shape[1] // 2
    nkvp = half // LANES

    def body(q_ref, kc_ref, kp_ref, vc_ref, vp_ref, tc_ref, tp_ref, s_ref, o_ref):
        valid = _swa_valid(pl.program_id(0))
        heads, _ = _swa_operands(q_ref, kc_ref, kp_ref, vc_ref, vp_ref, tc_ref, tp_ref, s_ref, nkvp)
        zs = [_dot(hd["k"], hd["qm"], NT) for hd in heads]
        ps = [_swa_probs(z, valid, hd["sink"])[0].astype(BF16) for z, hd in zip(zs, heads)]
        for n in range(0, len(heads), 2):
            o_t = _dot(heads[n]["tm"], ps[n], NN) + _dot(heads[n + 1]["tm"], ps[n + 1], NN)
            o_ref[:, heads[n]["q_lanes"]] = jnp.transpose(o_t)

    return pl.pallas_call(
        body, name=name, out_shape=jax.ShapeDtypeStruct((S, D), F32),
        grid=(S // BLK,),
        in_specs=_swa_specs(D, half, 1),
        out_specs=pl.BlockSpec((BLK, D), lambda i: (i, 0)),
        compiler_params=_params("arbitrary"),
    )(q, kv, kv, kv, kv, kv_t, kv_t, sinks)


def _swa_bwd(q, kv, kv_t, sinks, o, do, cos_t, sin_t, name, carry=None):
    S, D = q.shape
    half = kv.shape[1] // 2
    nkvp = half // LANES
    nh = nkvp * 2 * SWA_Q_GROUPS

    def body(q_ref, kc_ref, kp_ref, vc_ref, vp_ref, tc_ref, tp_ref, s_ref, o_ref, do_ref, c_ref, sn_ref,
             dq_ref, dk_ref, dv_ref, ds_ref):
        i = pl.program_id(0)
        valid = _swa_valid(i)
        heads, m0 = _swa_operands(q_ref, kc_ref, kp_ref, vc_ref, vp_ref, tc_ref, tp_ref, s_ref, nkvp)
        top_q = lax.broadcasted_iota(jnp.int32, (LANES, BLK), 0) < HEAD_DIM

        @pl.when(i == 0)
        def _():
            dk_ref[...] = jnp.zeros_like(dk_ref)
            dv_ref[...] = jnp.zeros_like(dv_ref)
            ds_ref[...] = jnp.zeros_like(ds_ref)

        doms, deltas = [], []
        for n in range(0, nh, 2):
            doc = do_ref[:, heads[n]["q_lanes"]]
            prod_t = jnp.transpose(doc.astype(F32) * o_ref[:, heads[n]["q_lanes"]])
            for hd in heads[n:n + 2]:
                doms.append(hd["sel"](doc, jnp.zeros_like(doc), m0))
                deltas.append(jnp.sum(hd["sel"](prod_t, 0.0, top_q), axis=0, keepdims=True))
        zs = [_dot(hd["k"], hd["qm"], NT) for hd in heads]
        dps = [_dot(hd["v"], dom, NT) for dom, hd in zip(doms, heads)]
        pbs, dscs = [], []
        for n, hd in enumerate(heads):
            p, psink = _swa_probs(zs[n], valid, hd["sink"])
            pbs.append(p.astype(BF16))
            dscs.append((p * (dps[n] - deltas[n]) * ATTN_SCALE).astype(BF16))
            ds_ref[n:n + 1, :] += -(psink * deltas[n])
        for n in range(0, nh, 2):
            dq_rot = jnp.transpose(_dot(heads[n]["tm"], dscs[n], NN) + _dot(heads[n + 1]["tm"], dscs[n + 1], NN))
            dq_ref[:, heads[n]["q_lanes"]] = (
                dq_rot * c_ref[...] + _swap_halves(dq_rot * sn_ref[...])).astype(BF16)
        acc = {}
        for n, hd in enumerate(heads):
            dk_n = _dot(dscs[n], hd["qm"], NN)
            dv_n = _dot(pbs[n], doms[n], NN)
            for key, val in ((("k", hd["m"], hd["same"]), dk_n), (("v", hd["m"], hd["same"]), dv_n)):
                acc[key] = val if key not in acc else acc[key] + val
        poff = pl.multiple_of(jnp.maximum(i - 1, 0) * BLK, BLK)
        coff = pl.multiple_of(i * BLK, BLK)
        for m in range(nkvp):
            pair = slice(m * LANES, (m + 1) * LANES)
            dkk = acc["k", m, True] + pltpu.roll(acc["k", m, False], HEAD_DIM, 1)
            dvv = acc["v", m, True] + pltpu.roll(acc["v", m, False], HEAD_DIM, 1)
            dk_ref[pl.ds(poff, BLK), pair] += dkk[:BLK]
            dv_ref[pl.ds(poff, BLK), pair] += dvv[:BLK]
            dk_ref[pl.ds(coff, BLK), pair] += dkk[BLK:]
            dv_ref[pl.ds(coff, BLK), pair] += dvv[BLK:]

    qblk = pl.BlockSpec((BLK, D), lambda i: (i, 0))
    whole = pl.BlockSpec((S, half), lambda i: (0, 0))
    tab = pl.BlockSpec((BLK, LANES), lambda i: (i, 0))
    return _pcall(
        body, (q, kv, kv, kv, kv, kv_t, kv_t, sinks, o, do, cos_t, sin_t), name=name,
        out_shape=(jax.ShapeDtypeStruct((S, D), BF16),
                   jax.ShapeDtypeStruct((S, half), F32),
                   jax.ShapeDtypeStruct((S, half), F32),
                   jax.ShapeDtypeStruct((nh, LANES), F32)),
        grid=(S // BLK,),
        in_specs=_swa_specs(D, half, 0) + [qblk, qblk, tab, tab],
        out_specs=(qblk, whole, whole, pl.BlockSpec((nh, LANES), lambda i: (0, 0))),
        sem=("arbitrary",), carry=carry)


def _dev_index(p):
    return 4 * p[0] + 2 * p[1] + p[2]


def _gather_plan(x_refs, out_refs, send_sems, recv_sems, local_sems):
    n = len(x_refs)
    x_, y_, c_ = lax.axis_index("x"), lax.axis_index("y"), lax.axis_index("c")
    me, sibling = (x_, y_, c_), (x_, y_, 1 - c_)
    chips = [(1 - x_, y_), (x_, 1 - y_), (1 - x_, 1 - y_)]

    def copy(t, k, block, to, src=None):
        dst = out_refs[t].at[_dev_index(block)]
        return pltpu.make_async_remote_copy(
            src_ref=dst if src is None else src, dst_ref=dst,
            send_sem=send_sems.at[7 * t + k], recv_sem=recv_sems.at[7 * t + k],
            device_id=to, device_id_type=MESH)

    mine = [pltpu.make_async_copy(x_refs[t], out_refs[t].at[_dev_index(me)], local_sems.at[t]) for t in range(n)]
    first = []
    for t in range(n):
        first.append(copy(t, 0, me, sibling, src=x_refs[t]))
        first += [copy(t, 1 + j, me, (*chip, c_), src=x_refs[t]) for j, chip in enumerate(chips)]
    arrived = lambda t, j: copy(t, 1 + j, (*chips[j], c_), me)
    forward = lambda t, j: copy(t, 4 + j, (*chips[j], c_), sibling)
    from_sibling = lambda t: copy(t, 0, sibling, me)
    forwarded = lambda t, j: copy(t, 4 + j, (*chips[j], 1 - c_), me)
    return n, mine, first, arrived, forward, from_sibling, forwarded


def _gather_start(x_refs, out_refs, send_sems, recv_sems, local_sems):
    _, mine, first, *_ = _gather_plan(x_refs, out_refs, send_sems, recv_sems, local_sems)
    for cp in mine + first:
        cp.start()


def _gather_forward(x_refs, out_refs, send_sems, recv_sems, local_sems):
    n, _, _, arrived, forward, _, _ = _gather_plan(x_refs, out_refs, send_sems, recv_sems, local_sems)
    for j in range(3):
        for t in range(n):
            arrived(t, j).wait_recv()
            forward(t, j).start()


def _gather_finish(x_refs, out_refs, send_sems, recv_sems, local_sems):
    n, mine, first, _, forward, from_sibling, forwarded = _gather_plan(
        x_refs, out_refs, send_sems, recv_sems, local_sems)
    for t in range(n):
        from_sibling(t).wait_recv()
    for j in range(3):
        for t in range(n):
            forwarded(t, j).wait_recv()
    for cp in first + [forward(t, j) for j in range(3) for t in range(n)]:
        cp.wait_send()
    for cp in mine:
        cp.wait()


def _scatter_plan(b_refs, out_refs, send_sems, recv_sems, local_sems):
    n = len(b_refs)
    x_, y_, c_ = lax.axis_index("x"), lax.axis_index("y"), lax.axis_index("c")
    my_idx = _dev_index((x_, y_, c_))
    mine = [pltpu.make_async_copy(b_refs[t].at[my_idx], out_refs[t].at[my_idx], local_sems.at[t]) for t in range(n)]
    copies = []
    for t in range(n):
        for k in range(1, N_DEV):
            peer = (x_ ^ ((k >> 2) & 1), y_ ^ ((k >> 1) & 1), c_ ^ (k & 1))
            copies.append(pltpu.make_async_remote_copy(
                src_ref=b_refs[t].at[_dev_index(peer)], dst_ref=out_refs[t].at[my_idx],
                send_sem=send_sems.at[7 * t + k - 1], recv_sem=recv_sems.at[7 * t + k - 1],
                device_id=peer, device_id_type=MESH))
    return mine, copies


def _scatter_start(b_refs, out_refs, send_sems, recv_sems, local_sems):
    mine, copies = _scatter_plan(b_refs, out_refs, send_sems, recv_sems, local_sems)
    for cp in mine + copies:
        cp.start()


def _scatter_finish(b_refs, out_refs, send_sems, recv_sems, local_sems):
    mine, copies = _scatter_plan(b_refs, out_refs, send_sems, recv_sems, local_sems)
    for cp in copies:
        cp.wait_recv()
    for cp in copies:
        cp.wait_send()
    for cp in mine:
        cp.wait()


def _exchange_operands(kind, tensors):
    if kind == "gather":
        args = list(tensors)
        shapes = [jax.ShapeDtypeStruct((N_DEV,) + t.shape, t.dtype) for t in tensors]
        return args, shapes, (_gather_start, _gather_forward, _gather_finish)
    args = [t.reshape(N_DEV, t.shape[0] // N_DEV, t.shape[1]) for t in tensors]
    shapes = [jax.ShapeDtypeStruct(a.shape, a.dtype) for a in args]
    return args, shapes, (_scatter_start, None, _scatter_finish)


def _exchange_results(kind, tensors, res):
    if kind == "gather":
        return [r.reshape(N_DEV * t.shape[0], t.shape[1]) for r, t in zip(res, tensors)]
    return list(res)


def _exchange_sems(n):
    return [pltpu.SemaphoreType.DMA((7 * n,)), pltpu.SemaphoreType.DMA((7 * n,)), pltpu.SemaphoreType.DMA((n,))]


def _exchange(kind, tensors, name):
    n = len(tensors)
    args, shapes, phases = _exchange_operands(kind, tensors)

    def body(*refs):
        for phase in phases:
            if phase is not None:
                phase(refs[:n], refs[n:2 * n], *refs[2 * n:])

    hbm = pl.BlockSpec(memory_space=pl.ANY)
    res = pl.pallas_call(body, name=name, out_shape=shapes, in_specs=[hbm] * n, out_specs=[hbm] * n,
                         scratch_shapes=_exchange_sems(n))(*args)
    return _exchange_results(kind, tensors, res)


def _pcall(body, args, *, name, out_shape, grid, in_specs, out_specs, sem, scratch_shapes=(), carry=None):
    if carry is None:
        out = pl.pallas_call(body, name=name, out_shape=out_shape, grid=grid, in_specs=list(in_specs),
                             out_specs=out_specs, scratch_shapes=list(scratch_shapes),
                             compiler_params=_params(*sem))(*args)
        return out, None
    kind, tensors = carry
    multi = isinstance(out_shape, (tuple, list))
    shapes = list(out_shape) if multi else [out_shape]
    ospecs = list(out_specs) if multi else [out_specs]
    n_in, n_out, n_scr, n_c = len(in_specs), len(shapes), len(scratch_shapes), len(tensors)
    c_args, c_shapes, (start, forward, finish) = _exchange_operands(kind, tensors)
    n_steps = 1
    for g in grid:
        n_steps *= g
    late = (3 * n_steps) // 4

    def wrapped(*refs):
        ins, rest = refs[:n_in], refs[n_in:]
        c_in, rest = rest[:n_c], rest[n_c:]
        outs, rest = rest[:n_out], rest[n_out:]
        c_out, rest = rest[:n_c], rest[n_c:]
        scr, sems = rest[:n_scr], rest[n_scr:]
        step = pl.program_id(0)
        for a in range(1, len(grid)):
            step = step * grid[a] + pl.program_id(a)

        @pl.when(step == 0)
        def _():
            start(c_in, c_out, *sems)

        body(*ins, *outs, *scr)

        if forward is not None:
            @pl.when(step == late)
            def _():
                forward(c_in, c_out, *sems)

        @pl.when(step == n_steps - 1)
        def _():
            finish(c_in, c_out, *sems)

    hbm = pl.BlockSpec(memory_space=pl.ANY)
    res = pl.pallas_call(
        wrapped, name=name, out_shape=shapes + c_shapes, grid=grid,
        in_specs=list(in_specs) + [hbm] * n_c, out_specs=ospecs + [hbm] * n_c,
        scratch_shapes=list(scratch_shapes) + _exchange_sems(n_c),
        compiler_params=_params(*sem))(*args, *c_args)
    outs = tuple(res[:n_out]) if multi else res[0]
    return outs, _exchange_results(kind, tensors, res[n_out:])


def _sum8(parts, name):
    _, R, C = parts.shape
    tr = _tile(R, 256, 16)

    def body(p_ref, g_ref):
        g = p_ref[0].astype(F32)
        for s in range(1, N_DEV):
            g = g + p_ref[s].astype(F32)
        g_ref[...] = g

    return pl.pallas_call(
        body, name=name, out_shape=jax.ShapeDtypeStruct((R, C), F32),
        grid=(R // tr,),
        in_specs=[pl.BlockSpec((N_DEV, tr, C), lambda i: (0, i, 0))],
        out_specs=pl.BlockSpec((tr, C), lambda i: (i, 0)),
        compiler_params=_params("parallel"),
    )(parts)


def _adamw(g, w, m, v, name):
    R, C = g.shape
    tr = _tile(R, 256, 8)
    c1 = 1.0 - ADAM_B1 ** ADAM_STEP
    c2 = 1.0 - ADAM_B2 ** ADAM_STEP

    def body(g_ref, w_ref, m_ref, v_ref, d_ref, nm_ref, nv_ref):
        gg = g_ref[...]
        nm = ADAM_B1 * m_ref[...] + (1.0 - ADAM_B1) * gg
        nv = ADAM_B2 * v_ref[...] + (1.0 - ADAM_B2) * (gg * gg)
        m_hat = nm / c1
        v_hat = nv / c2
        nm_ref[...] = nm
        nv_ref[...] = nv
        d_ref[...] = -ADAM_LR * (m_hat / (jnp.sqrt(v_hat) + ADAM_EPS) + ADAM_WD * w_ref[...])

    row = pl.BlockSpec((tr, C), lambda i: (i, 0))
    shp = jax.ShapeDtypeStruct((R, C), F32)
    return pl.pallas_call(
        body, name=name, out_shape=(shp, shp, shp),
        grid=(R // tr,), in_specs=[row, row, row, row], out_specs=(row, row, row),
        compiler_params=_params("parallel"),
    )(g, w, m, v)


def _ffn_down(act, wo, h, tag):
    return _mm(act, wo, NN, F32, f"{tag}_down", scale=FFN_RES_SCALE, res=h, tm=512, tn=1024, tk=2816)


def _ffn_fwd(h, g, win_t, wo, tag, carry=None, loss=None):
    return _ffn_fwd_fused(h, g, win_t, wo, f"{tag}_fwd", carry=carry, loss=loss)


def _ffn_bwd(dh, h, g, win_t, wo, saved, tag, scatter=False, carry=None, carry_dwin=None):
    xn, silu, dsilu, up, act = saved
    dwo = _mm(act, dh, TN, BF16, f"{tag}_dwo", scale=FFN_RES_SCALE, tm=1408, tn=1024, tk=TN_CHUNK)
    if not scatter:
        (dh_in, dg, dgate, dup), got = _ffn_bwd_fused(dh, h, g, win_t, wo, silu, dsilu, up, f"{tag}_bwd", carry=carry)
        dwin_t, got_dwin = _dw_rows([dgate, dup], xn, f"{tag}_dwin", carry=carry_dwin)
        return dh_in, dg, dwin_t, dwo, got, got_dwin
    dgate, dup = _ffn_dact(dh, wo, silu, dsilu, up, f"{tag}_dact")
    dwin_t, got_wo = _dw_rows([dgate, dup], xn, f"{tag}_dwin", carry=("scatter", [dwo]))
    (dh_in, dg), got_win = _dx_norm_bwd([(dgate, win_t, NN, 2, 0), (dup, win_t, NN, 2, 1)], h, g, dh, f"{tag}_dx",
                                        carry=("scatter", [dwin_t]))
    return dh_in, dg, got_win[0], got_wo[0]


def _proj(a, w, dims, out_dtype, name, res=None):
    return _mm(a, w, dims, out_dtype, name, res=res, tm=1024, tn=1024, tk=1024)


def _proj_dw(x, dy, name):
    return _mm(x, dy, TN, BF16, name, tm=1024, tn=1024, tk=TN_CHUNK)


def kernel(x, ffn1_norm, ffn1_w_in, ffn1_w_out, mix_norm, ffn2_norm, ffn2_w_in, ffn2_w_out, sb_w_qkv, sb_w_o, kv_norm, kv_w, swa_w_q, swa_sinks, swa_w_o, final_norm, loss_target, m_ffn1_norm, m_ffn1_w_in, m_ffn1_w_out, m_mix_norm, m_ffn2_norm, m_ffn2_w_in, m_ffn2_w_out, m_sb_w_qkv, m_sb_w_o, m_kv_norm, m_kv_w, m_swa_w_q, m_swa_sinks, m_swa_w_o, m_final_norm, v_ffn1_norm, v_ffn1_w_in, v_ffn1_w_out, v_mix_norm, v_ffn2_norm, v_ffn2_w_in, v_ffn2_w_out, v_sb_w_qkv, v_sb_w_o, v_kv_norm, v_kv_w, v_swa_w_q, v_swa_sinks, v_swa_w_o, v_final_norm):
    S, D = x.shape[1], x.shape[2]
    L = ffn1_w_in.shape[0]
    KV = kv_w.shape[1]
    assert L == 2 and swa_sinks.shape == (1, 2 * SWA_Q_GROUPS * KV // (2 * LANES))

    def bf(w):
        return w.astype(BF16)

    def bft(w):
        return jnp.transpose(w).astype(BF16)

    cos_t, sin_t = _rope_tables(S)
    h0 = x.reshape(S, D)
    tgt = loss_target.reshape(S, D)

    win1a_t, = _exchange("gather", [bft(ffn1_w_in[0])], "gather_first_weight")
    sv_a1, (wo1a, wqkv_t, w_sbo) = _ffn_up(
        h0, ffn1_norm[0], win1a_t, "ffn1a_up",
        carry=("gather", [bf(ffn1_w_out[0]), bft(sb_w_qkv[0]), bf(sb_w_o[0])]))
    h1 = _ffn_down(sv_a1[-1], wo1a, h0, "ffn1a")
    hn_a, qkv, kv_t = _norm_proj(h1, mix_norm[0], wqkv_t, NT, "sb_qkv", tail_t=2 * D)
    o_sb, (win2a_t, wo2a, w_kv) = _sb_fwd(qkv, kv_t, "sb_attn", carry=("gather", [
        bft(ffn2_w_in[0]), bf(ffn2_w_out[0]), bf(kv_w)]))
    h2 = _proj(o_sb, w_sbo, NN, F32, "sb_out", res=h1)
    h3, sv_a2, (win1b_t, wo1b, w_q, w_swo) = _ffn_fwd(h2, ffn2_norm[0], win2a_t, wo2a, "ffn2a", carry=("gather", [
        bft(ffn1_w_in[1]), bf(ffn1_w_out[1]), bf(swa_w_q[0]), bf(swa_w_o[0])]))
    kvn, kv_rot, kv_rot_t = _norm_proj(h3, kv_norm, w_kv, NN, "kv_proj", rope=(cos_t, sin_t, KV // (2 * LANES)),
                                       tail_t=KV)
    h4, sv_b1, (win2b_t, wo2b) = _ffn_fwd(h3, ffn1_norm[1], win1b_t, wo1b, "ffn1b", carry=("gather", [
        bft(ffn2_w_in[1]), bf(ffn2_w_out[1])]))
    hn_b, q_rot = _norm_proj(h4, mix_norm[1], w_q, NN, "swa_q", rope=(cos_t, sin_t, D // LANES))
    o_sw = _swa_fwd(q_rot, kv_rot, kv_rot_t, swa_sinks, "swa_attn")
    h5 = _proj(o_sw, w_swo, NN, F32, "swa_out", res=h4)
    (dh6, dg_final, sq_err), sv_b2, _ = _ffn_fwd(h5, ffn2_norm[1], win2b_t, wo2b, "ffn2b", loss=(final_norm, tgt))
    loss = lax.psum(0.5 * jnp.sum(sq_err) / D, ("x", "y", "c"))

    dh5, dg_f2b, dwin2b_t, dwo2b, _, _ = _ffn_bwd(dh6, h5, ffn2_norm[1], win2b_t, wo2b, sv_b2, "ffn2b")
    do_sw = _proj(dh5, w_swo, NT, BF16, "swa_out_dx")
    dw_swo = _proj_dw(o_sw, dh5, "swa_out_dw")
    (dq, dk_sw, dv_sw, dsink), (p_win2b, p_swo) = _swa_bwd(
        q_rot, kv_rot, kv_rot_t, swa_sinks, o_sw, do_sw, cos_t, sin_t, "swa_attn_bwd",
        carry=("scatter", [dwin2b_t, dw_swo]))
    dw_q = _proj_dw(hn_b, dq, "swa_q_dw")
    (dh4, dg_mix_b), _ = _dx_norm_bwd([(dq, w_q, NT, 1, 0)], h4, mix_norm[1], dh5, "swa_q_dx", tm=512)
    dh3, dg_f1b, dwin1b_t, dwo1b, (p_q, p_wo2b), _ = _ffn_bwd(dh4, h3, ffn1_norm[1], win1b_t, wo1b, sv_b1, "ffn1b",
                                                              carry=("scatter", [dw_q, dwo2b]))
    dkv = _rotary(jnp.concatenate([dk_sw, dv_sw], axis=1), cos_t, sin_t, KV // (2 * LANES), True, "kv_rope_bwd")
    dw_kv = _proj_dw(kvn, dkv, "kv_proj_dw")
    (dh3, dg_kv), _ = _dx_norm_bwd([(dkv, w_kv, NT, 1, 0)], h3, kv_norm, dh3, "kv_proj_dx", tm=512)
    dh2, dg_f2a, dwin2a_t, dwo2a, (p_win1b, p_kv), (p_wo1b,) = _ffn_bwd(
        dh3, h2, ffn2_norm[0], win2a_t, wo2a, sv_a2, "ffn2a",
        carry=("scatter", [dwin1b_t, dw_kv]), carry_dwin=("scatter", [dwo1b]))
    do_sb = _proj(dh2, w_sbo, NT, BF16, "sb_out_dx")
    dw_sbo = _proj_dw(o_sb, dh2, "sb_out_dw")
    (dq_sb, dk_sb, dv_sb), (p_win2a, p_wo2a, p_sbo) = _sb_bwd(
        qkv, kv_t, o_sb, do_sb, "sb_attn_bwd", carry=("scatter", [dwin2a_t, dwo2a, dw_sbo]))
    dqkv = [dq_sb, dk_sb, dv_sb]
    dwqkv_t, _ = _dw_rows(dqkv, hn_a, "sb_qkv_dw", tk=TN_CHUNK // 2)
    (dh1, dg_mix_a), (p_qkv,) = _dx_norm_bwd([(dy, wqkv_t, NN, 3, n) for n, dy in enumerate(dqkv)], h1, mix_norm[0],
                                             dh2, "sb_qkv_dx", carry=("scatter", [dwqkv_t]))
    dx, dg_f1a, p_win1a, p_wo1a = _ffn_bwd(dh1, h0, ffn1_norm[0], win1a_t, wo1a, sv_a1, "ffn1a", scatter=True)

    def natural(parts, tag):
        return _sum8(parts, f"sum_{tag}")

    def from_t(parts, tag):
        return jnp.transpose(_sum8(parts, f"sum_{tag}"))

    grads = {
        "ffn1_w_in": jnp.stack([from_t(p_win1a, "win1a"), from_t(p_win1b, "win1b")]),
        "ffn1_w_out": jnp.stack([natural(p_wo1a, "wo1a"), natural(p_wo1b, "wo1b")]),
        "ffn2_w_in": jnp.stack([from_t(p_win2a, "win2a"), from_t(p_win2b, "win2b")]),
        "ffn2_w_out": jnp.stack([natural(p_wo2a, "wo2a"), natural(p_wo2b, "wo2b")]),
        "sb_w_qkv": from_t(p_qkv, "qkv")[None],
        "sb_w_o": natural(p_sbo, "sbo")[None],
        "kv_w": natural(p_kv, "kv"),
        "swa_w_q": natural(p_q, "swq")[None],
        "swa_w_o": natural(p_swo, "swo")[None],
    }

    small_w = [ffn1_norm, mix_norm, ffn2_norm, kv_norm, final_norm, swa_sinks]
    small_m = [m_ffn1_norm, m_mix_norm, m_ffn2_norm, m_kv_norm, m_final_norm, m_swa_sinks]
    small_v = [v_ffn1_norm, v_mix_norm, v_ffn2_norm, v_kv_norm, v_final_norm, v_swa_sinks]
    SMALL_ROWS = 16

    def pack_small(ts):
        rows_ = [t.reshape(-1, D) for t in ts[:-1]]
        sink_row = jnp.pad(ts[-1].reshape(1, -1), ((0, 0), (0, D - ts[-1].size)))
        flat = jnp.concatenate(rows_ + [sink_row], axis=0)
        return jnp.pad(flat, ((0, SMALL_ROWS - flat.shape[0]), (0, 0)))

    def unpack_small(flat):
        out, r = [], 0
        for t in small_w[:-1]:
            n = t.size // D
            out.append(flat[r:r + n].reshape(t.shape))
            r += n
        out.append(flat[r, :swa_sinks.size].reshape(swa_sinks.shape))
        return out

    def gain(parts8):
        return jnp.sum(parts8, axis=0, keepdims=True)

    g_small_local = pack_small([
        jnp.concatenate([gain(dg_f1a), gain(dg_f1b)], axis=0),
        jnp.concatenate([gain(dg_mix_a), gain(dg_mix_b)], axis=0),
        jnp.concatenate([gain(dg_f2a), gain(dg_f2b)], axis=0),
        gain(dg_kv), gain(dg_final), jnp.sum(dsink, axis=-1).reshape(1, -1)])
    small_parts = _exchange("gather", [g_small_local], "gather_small_grads")[0]
    g_small = _sum8(small_parts.reshape(N_DEV, SMALL_ROWS, D), "sum_small")
    d_small, nm_small, nv_small = _adamw(g_small, pack_small(small_w), pack_small(small_m), pack_small(small_v), "adamw_small")
    small_names = ["ffn1_norm", "mix_norm", "ffn2_norm", "kv_norm", "final_norm", "swa_sinks"]
    result = {"grad": dict(zip(small_names, unpack_small(g_small))),
              "delta": dict(zip(small_names, unpack_small(d_small))),
              "new_m": dict(zip(small_names, unpack_small(nm_small))),
              "new_v": dict(zip(small_names, unpack_small(nv_small)))}

    big = {"ffn1_w_in": (ffn1_w_in, m_ffn1_w_in, v_ffn1_w_in), "ffn1_w_out": (ffn1_w_out, m_ffn1_w_out, v_ffn1_w_out),
           "ffn2_w_in": (ffn2_w_in, m_ffn2_w_in, v_ffn2_w_in), "ffn2_w_out": (ffn2_w_out, m_ffn2_w_out, v_ffn2_w_out),
           "sb_w_qkv": (sb_w_qkv, m_sb_w_qkv, v_sb_w_qkv), "sb_w_o": (sb_w_o, m_sb_w_o, v_sb_w_o),
           "kv_w": (kv_w, m_kv_w, v_kv_w), "swa_w_q": (swa_w_q, m_swa_w_q, v_swa_w_q),
           "swa_w_o": (swa_w_o, m_swa_w_o, v_swa_w_o)}
    for nm, (w, m, v) in big.items():
        g = grads[nm]
        two_d = lambda t: t.reshape(-1, t.shape[-1])
        d, new_m, new_v = _adamw(two_d(g), two_d(w), two_d(m), two_d(v), f"adamw_{nm}")
        result["grad"][nm] = g
        result["delta"][nm] = d.reshape(w.shape)
        result["new_m"][nm] = new_m.reshape(w.shape)
        result["new_v"][nm] = new_v.reshape(w.shape)

    order = ["ffn1_norm", "ffn1_w_in", "ffn1_w_out", "mix_norm", "ffn2_norm", "ffn2_w_in", "ffn2_w_out",
             "sb_w_qkv", "sb_w_o", "kv_norm", "kv_w", "swa_w_q", "swa_sinks", "swa_w_o", "final_norm"]
    outs = [result[kind][nm] for kind in ("grad", "delta", "new_m", "new_v") for nm in order]
    return (loss, dx.reshape(x.shape), *outs)
```

```python
import jax
import jax.numpy as jnp
from jax import lax
from jax.experimental import pallas as pl
from jax.experimental.pallas import tpu as pltpu

F32 = jnp.float32
BF16 = jnp.bfloat16

N_DEV = 8
HEAD_DIM = 64
LANES = 128
BLK = 128
RMS_EPS = 1e-6
FFN_RES_SCALE = 0.5
ROPE_THETA = 10000.0
ATTN_SCALE = HEAD_DIM ** -0.5
SB_LOG_FLOOR = -88.0
NEG_BIG = -1e30
VMEM_LIMIT_V7X = 56 * 1024 * 1024

ADAM_LR = 0.001
ADAM_B1 = 0.9
ADAM_B2 = 0.999
ADAM_EPS = 1e-08
ADAM_WD = 0.01
ADAM_STEP = 10

NN = ((1,), (0,))
NT = ((1,), (1,))
TN = ((0,), (0,))
TN_CHUNK = 2048
MESH = pl.DeviceIdType.MESH


def _dot(a, b, dims):
    return lax.dot_general(a, b, (dims, ((), ())), preferred_element_type=F32)


def _tile(n, pref, mult=LANES):
    if n <= pref:
        return n
    t = (pref // mult) * mult
    while t >= mult:
        if n % t == 0:
            return t
        t -= mult
    return n


def _params(*sem):
    return pltpu.CompilerParams(dimension_semantics=sem, vmem_limit_bytes=VMEM_LIMIT_V7X)


def _mm(a, b, dims, out_dtype, name, scale=1.0, res=None, tm=512, tn=512, tk=512):
    if dims == NN:
        (M, K), (_, N) = a.shape, b.shape
    elif dims == NT:
        (M, K), (N, _) = a.shape, b.shape
    else:
        (K, M), (_, N) = a.shape, b.shape
    tm, tn, tk = _tile(M, tm), _tile(N, tn), _tile(K, tk)
    nk = K // tk
    if dims == TN:
        a_spec = pl.BlockSpec((tk, tm), lambda i, j, k: (k, i))
    else:
        a_spec = pl.BlockSpec((tm, tk), lambda i, j, k: (i, k))
    if dims == NT:
        b_spec = pl.BlockSpec((tn, tk), lambda i, j, k: (j, k))
    else:
        b_spec = pl.BlockSpec((tk, tn), lambda i, j, k: (k, j))
    o_spec = pl.BlockSpec((tm, tn), lambda i, j, k: (i, j))
    has_res = res is not None

    def body(*refs):
        a_ref, b_ref = refs[0], refs[1]
        r_ref = refs[2] if has_res else None
        o_ref = refs[3] if has_res else refs[2]

        def finish(acc):
            r = acc * scale if scale != 1.0 else acc
            if has_res:
                r = r + r_ref[...]
            o_ref[...] = r.astype(out_dtype)

        p = _dot(a_ref[...].astype(BF16), b_ref[...].astype(BF16), dims)
        if nk == 1:
            finish(p)
        else:
            acc_ref = refs[-1]
            k = pl.program_id(2)

            @pl.when(k == 0)
            def _():
                acc_ref[...] = p

            @pl.when(k > 0)
            def _():
                acc_ref[...] += p

            @pl.when(k == nk - 1)
            def _():
                finish(acc_ref[...])

    in_specs = [a_spec, b_spec] + ([o_spec] if has_res else [])
    args = (a, b) + ((res,) if has_res else ())
    return pl.pallas_call(
        body, name=name,
        out_shape=jax.ShapeDtypeStruct((M, N), out_dtype),
        grid=(M // tm, N // tn, nk),
        in_specs=in_specs, out_specs=o_spec,
        scratch_shapes=[pltpu.VMEM((tm, tn), F32)] if nk > 1 else [],
        compiler_params=_params("parallel", "parallel", "arbitrary"),
    )(*args)


def _rows8(x):
    r, d = x.shape
    return jnp.sum(x.reshape(r // 8, 8, d), axis=0)


def _norm_proj(h, g, w, dims, name, rope=None, tail_t=0):
    S, D = h.shape
    N = w.shape[1] if dims == NN else w.shape[0]
    tm = _tile(S, 512, 16)

    def body(h_ref, g_ref, w_ref, *rest):
        xn_ref, y_ref = rest[-3:-1] if tail_t else rest[-2:]
        x = h_ref[...]
        r = lax.rsqrt(jnp.mean(x * x, axis=-1, keepdims=True) + RMS_EPS)
        xn = ((x * r) * g_ref[...]).astype(BF16)
        xn_ref[...] = xn
        y = _dot(xn, w_ref[...], dims)
        if rope is not None:
            cs, sn = rest[0][...], rest[1][...]
            groups = [y[:, gidx * LANES:(gidx + 1) * LANES] for gidx in range(N // LANES)]
            y = jnp.concatenate([v * cs + _swap_halves(v) * sn if gidx < rope[2] else v
                                 for gidx, v in enumerate(groups)], axis=1)
        y_ref[...] = y.astype(BF16)
        if tail_t:
            rest[-1][...] = jnp.transpose(y[:, N - tail_t:]).astype(BF16)

    row = pl.BlockSpec((tm, D), lambda i: (i, 0))
    tab = pl.BlockSpec((tm, LANES), lambda i: (i, 0))
    in_specs = [row, pl.BlockSpec((1, D), lambda i: (0, 0)), pl.BlockSpec(w.shape, lambda i: (0, 0))]
    args = (h, g.reshape(1, D), w)
    if rope is not None:
        in_specs += [tab, tab]
        args += (rope[0], rope[1])
    out_shape = [jax.ShapeDtypeStruct((S, D), BF16), jax.ShapeDtypeStruct((S, N), BF16)]
    out_specs = [row, pl.BlockSpec((tm, N), lambda i: (i, 0))]
    if tail_t:
        out_shape.append(jax.ShapeDtypeStruct((tail_t, S), BF16))
        out_specs.append(pl.BlockSpec((tail_t, tm), lambda i: (0, i)))
    return pl.pallas_call(
        body, name=name, out_shape=out_shape, grid=(S // tm,),
        in_specs=in_specs, out_specs=out_specs,
        compiler_params=_params("parallel"),
    )(*args)


def _ffn_up(h, g, win_t, name, carry=None):
    S, D = h.shape
    F = win_t.shape[0] // 2
    tm, tn = _tile(S, 512, 16), _tile(F, 1408)
    nf = F // tn

    def body(h_ref, g_ref, wg_ref, wu_ref, xn_ref, silu_ref, dsilu_ref, up_ref, act_ref):
        x = h_ref[...]
        r = lax.rsqrt(jnp.mean(x * x, axis=-1, keepdims=True) + RMS_EPS)
        xn = ((x * r) * g_ref[...]).astype(BF16)
        xn_ref[...] = xn
        gate = _dot(xn, wg_ref[...], NT)
        up = _dot(xn, wu_ref[...], NT)
        sig = 1.0 / (1.0 + jnp.exp(-gate))
        silu = gate * sig
        up_ref[...] = up.astype(BF16)
        silu_ref[...] = silu.astype(BF16)
        dsilu_ref[...] = (sig + silu * (1.0 - sig)).astype(BF16)
        act_ref[...] = (silu * up).astype(BF16)

    row = pl.BlockSpec((tm, D), lambda i, j: (i, 0))
    blk = pl.BlockSpec((tm, tn), lambda i, j: (i, j))
    hid = jax.ShapeDtypeStruct((S, F), BF16)
    return _pcall(
        body, (h, g.reshape(1, D), win_t, win_t), name=name,
        out_shape=(jax.ShapeDtypeStruct((S, D), BF16), hid, hid, hid, hid),
        grid=(S // tm, nf),
        in_specs=[row, pl.BlockSpec((1, D), lambda i, j: (0, 0)),
                  pl.BlockSpec((tn, D), lambda i, j: (j, 0)),
                  pl.BlockSpec((tn, D), lambda i, j: (j + nf, 0))],
        out_specs=(row, blk, blk, blk, blk),
        sem=("arbitrary", "arbitrary"), carry=carry)


def _ffn_dact(dh, wo, silu, dsilu, up, name):
    S, D = dh.shape
    F = wo.shape[0]
    tm = _tile(S, 256, 16)

    def body(dh_ref, wo_hbm, s_ref, ds_ref, u_ref, dg_ref, du_ref, wo_v, sems):
        _load_resident([(wo_hbm, wo_v)], sems)
        d = _dot(dh_ref[...].astype(BF16), wo_v[...], NT) * FFN_RES_SCALE
        du_ref[...] = (d * s_ref[...].astype(F32)).astype(BF16)
        dg_ref[...] = (d * u_ref[...].astype(F32) * ds_ref[...].astype(F32)).astype(BF16)

    wide = pl.BlockSpec((tm, F), lambda i: (i, 0))
    hid = jax.ShapeDtypeStruct((S, F), BF16)
    return pl.pallas_call(
        body, name=name, out_shape=(hid, hid),
        grid=(S // tm,),
        in_specs=[pl.BlockSpec((tm, D), lambda i: (i, 0)), pl.BlockSpec(memory_space=pl.ANY), wide, wide, wide],
        out_specs=(wide, wide),
        scratch_shapes=[pltpu.VMEM(wo.shape, BF16), pltpu.SemaphoreType.DMA((1,))],
        compiler_params=_params("arbitrary"),
    )(dh, wo, silu, dsilu, up)


def _dw_rows(srcs, x, name, carry=None, tk=TN_CHUNK):
    n = len(srcs)
    S, F = srcs[0].shape
    D = x.shape[1]
    tr, tk = _tile(F, 1408), _tile(S, tk, 16)
    nf, nk = F // tr, S // tk

    def body(*refs):
        src_refs, (x_ref, o_ref, acc_ref) = refs[:n], refs[n:]
        r, k = pl.program_id(0), pl.program_id(1)
        for s in range(n):
            @pl.when(r // nf == s)
            def _():
                p = _dot(src_refs[s][...].astype(BF16), x_ref[...], TN)

                @pl.when(k == 0)
                def _():
                    acc_ref[...] = p

                @pl.when(k > 0)
                def _():
                    acc_ref[...] += p

        @pl.when(k == nk - 1)
        def _():
            o_ref[...] = acc_ref[...].astype(BF16)

    def src_spec(s):
        return pl.BlockSpec((tk, tr), lambda r, k: (jnp.where(r // nf == s, k, 0), jnp.clip(r - s * nf, 0, nf - 1)))

    return _pcall(
        body, (*srcs, x), name=name, out_shape=jax.ShapeDtypeStruct((n * F, D), BF16),
        grid=(n * nf, nk),
        in_specs=[src_spec(s) for s in range(n)] + [pl.BlockSpec((tk, D), lambda r, k: (k, 0))],
        out_specs=pl.BlockSpec((tr, D), lambda r, k: (r, 0)),
        scratch_shapes=[pltpu.VMEM((tr, D), F32)],
        sem=("arbitrary", "arbitrary"), carry=carry)


def _dx_norm_bwd(terms, h, g, res, name, carry=None, tm=256):
    S, D = h.shape
    tm = _tile(S, tm, 16)
    n = len(terms)

    def body(*refs):
        dy_refs, w_refs = refs[:n], refs[n:2 * n]
        h_ref, g_ref, r_ref, dh_ref, dg_ref = refs[2 * n:]
        d = _dot(dy_refs[0][...].astype(BF16), w_refs[0][...], terms[0][2])
        for t in range(1, n):
            d = d + _dot(dy_refs[t][...].astype(BF16), w_refs[t][...], terms[t][2])
        x = h_ref[...]
        r = lax.rsqrt(jnp.mean(x * x, axis=-1, keepdims=True) + RMS_EPS)
        xhat = x * r
        dxh = d * g_ref[...]
        c = jnp.mean(dxh * xhat, axis=-1, keepdims=True)
        dh_ref[...] = r * (dxh - xhat * c) + r_ref[...]
        part = _rows8(d * xhat)

        @pl.when(pl.program_id(0) == 0)
        def _():
            dg_ref[...] = part

        @pl.when(pl.program_id(0) > 0)
        def _():
            dg_ref[...] += part

    def w_spec(w, nblk, blk):
        return pl.BlockSpec((w.shape[0] // nblk, w.shape[1]), lambda i: (blk, 0))

    row = pl.BlockSpec((tm, D), lambda i: (i, 0))
    in_specs = [pl.BlockSpec((tm, t[0].shape[1]), lambda i: (i, 0)) for t in terms]
    in_specs += [w_spec(t[1], t[3], t[4]) for t in terms]
    in_specs += [row, pl.BlockSpec((1, D), lambda i: (0, 0)), row]
    return _pcall(
        body, (*[t[0] for t in terms], *[t[1] for t in terms], h, g.reshape(1, D), res), name=name,
        out_shape=(jax.ShapeDtypeStruct((S, D), F32), jax.ShapeDtypeStruct((8, D), F32)),
        grid=(S // tm,),
        in_specs=in_specs,
        out_specs=(row, pl.BlockSpec((8, D), lambda i: (0, 0))),
        sem=("arbitrary",), carry=carry)


def _load_resident(pairs, sems):
    @pl.when(pl.program_id(0) == 0)
    def _():
        copies = [pltpu.make_async_copy(src, dst, sems.at[n]) for n, (src, dst) in enumerate(pairs)]
        for cp in copies:
            cp.start()
        for cp in copies:
            cp.wait()


def _loss_tail(y_in, g, tgt):
    D = y_in.shape[-1]
    r = lax.rsqrt(jnp.mean(y_in * y_in, axis=-1, keepdims=True) + RMS_EPS)
    xhat = y_in * r
    err = xhat * g - tgt
    d = err * (1.0 / D)
    dxh = d * g
    c = jnp.mean(dxh * xhat, axis=-1, keepdims=True)
    return r * (dxh - xhat * c), _rows8(d * xhat), _rows8(err * err)


def _ffn_fwd_fused(h, g, win_t, wo, name, carry=None, loss=None):
    S, D = h.shape
    F = wo.shape[0]
    tm = _tile(S, 256, 16)
    n_head = 3 if loss is not None else 1

    def body(h_ref, g_ref, win_hbm, wo_hbm, *rest):
        lead, (xn_ref, silu_ref, dsilu_ref, up_ref, act_ref, win_v, wo_v, sems) = rest[:-8], rest[-8:]
        _load_resident([(win_hbm, win_v), (wo_hbm, wo_v)], sems)
        x = h_ref[...]
        r = lax.rsqrt(jnp.mean(x * x, axis=-1, keepdims=True) + RMS_EPS)
        xn = ((x * r) * g_ref[...]).astype(BF16)
        xn_ref[...] = xn
        gate = _dot(xn, win_v[:F, :], NT)
        up = _dot(xn, win_v[F:, :], NT)
        sig = 1.0 / (1.0 + jnp.exp(-gate))
        silu = gate * sig
        act = (silu * up).astype(BF16)
        up_ref[...] = up.astype(BF16)
        silu_ref[...] = silu.astype(BF16)
        dsilu_ref[...] = (sig + silu * (1.0 - sig)).astype(BF16)
        act_ref[...] = act
        out = x + FFN_RES_SCALE * _dot(act, wo_v[...], NN)
        if loss is None:
            lead[0][...] = out
        else:
            gf_ref, t_ref, dy_ref, dgf_ref, sq_ref = lead
            dy, dgf, sq = _loss_tail(out, gf_ref[...], t_ref[...])
            dy_ref[...] = dy

            @pl.when(pl.program_id(0) == 0)
            def _():
                dgf_ref[...] = dgf
                sq_ref[...] = sq

            @pl.when(pl.program_id(0) > 0)
            def _():
                dgf_ref[...] += dgf
                sq_ref[...] += sq

    row = pl.BlockSpec((tm, D), lambda i: (i, 0))
    vec = pl.BlockSpec((1, D), lambda i: (0, 0))
    acc = pl.BlockSpec((8, D), lambda i: (0, 0))
    wide = pl.BlockSpec((tm, F), lambda i: (i, 0))
    hbm = pl.BlockSpec(memory_space=pl.ANY)
    hid = jax.ShapeDtypeStruct((S, F), BF16)
    full = jax.ShapeDtypeStruct((S, D), F32)
    part = jax.ShapeDtypeStruct((8, D), F32)
    args, in_specs = (h, g.reshape(1, D), win_t, wo), [row, vec, hbm, hbm]
    lead_shapes, lead_specs = (full,), (row,)
    if loss is not None:
        args, in_specs = args + (loss[0].reshape(1, D), loss[1]), in_specs + [vec, row]
        lead_shapes, lead_specs = (full, part, part), (row, acc, acc)
    res, got = _pcall(
        body, args, name=name,
        out_shape=lead_shapes + (jax.ShapeDtypeStruct((S, D), BF16), hid, hid, hid, hid),
        grid=(S // tm,),
        in_specs=in_specs,
        out_specs=lead_specs + (row, wide, wide, wide, wide),
        scratch_shapes=[pltpu.VMEM(win_t.shape, BF16), pltpu.VMEM(wo.shape, BF16), pltpu.SemaphoreType.DMA((2,))],
        sem=("arbitrary",), carry=carry)
    first = res[0] if loss is None else tuple(res[:3])
    return first, tuple(res[n_head:]), got


def _ffn_bwd_fused(dh, h, g, win_t, wo, silu, dsilu, up, name, carry=None):
    S, D = h.shape
    F = wo.shape[0]
    tm = _tile(S, 256, 16)

    def body(dh_ref, h_ref, g_ref, s_ref, ds_ref, u_ref, win_hbm, wo_hbm,
             dhin_ref, dgain_ref, dgate_ref, dup_ref, win_v, wo_v, sems):
        _load_resident([(win_hbm, win_v), (wo_hbm, wo_v)], sems)
        dhv = dh_ref[...]
        d = _dot(dhv.astype(BF16), wo_v[...], NT) * FFN_RES_SCALE
        dup = (d * s_ref[...].astype(F32)).astype(BF16)
        dgate = (d * u_ref[...].astype(F32) * ds_ref[...].astype(F32)).astype(BF16)
        dup_ref[...] = dup
        dgate_ref[...] = dgate
        dxn = _dot(dgate, win_v[:F, :], NN) + _dot(dup, win_v[F:, :], NN)
        x = h_ref[...]
        r = lax.rsqrt(jnp.mean(x * x, axis=-1, keepdims=True) + RMS_EPS)
        xhat = x * r
        dxh = dxn * g_ref[...]
        c = jnp.mean(dxh * xhat, axis=-1, keepdims=True)
        dhin_ref[...] = r * (dxh - xhat * c) + dhv
        part = _rows8(dxn * xhat)

        @pl.when(pl.program_id(0) == 0)
        def _():
            dgain_ref[...] = part

        @pl.when(pl.program_id(0) > 0)
        def _():
            dgain_ref[...] += part

    row = pl.BlockSpec((tm, D), lambda i: (i, 0))
    wide = pl.BlockSpec((tm, F), lambda i: (i, 0))
    hbm = pl.BlockSpec(memory_space=pl.ANY)
    hid = jax.ShapeDtypeStruct((S, F), BF16)
    return _pcall(
        body, (dh, h, g.reshape(1, D), silu, dsilu, up, win_t, wo), name=name,
        out_shape=(jax.ShapeDtypeStruct((S, D), F32), jax.ShapeDtypeStruct((8, D), F32), hid, hid),
        grid=(S // tm,),
        in_specs=[row, row, pl.BlockSpec((1, D), lambda i: (0, 0)), wide, wide, wide, hbm, hbm],
        out_specs=(row, pl.BlockSpec((8, D), lambda i: (0, 0)), wide, wide),
        scratch_shapes=[pltpu.VMEM(win_t.shape, BF16), pltpu.VMEM(wo.shape, BF16), pltpu.SemaphoreType.DMA((2,))],
        sem=("arbitrary",), carry=carry)


def _rope_tables(S):
    half = HEAD_DIM // 2
    inv_freq = ROPE_THETA ** (-jnp.arange(half, dtype=F32) / half)
    ang = jnp.arange(S).astype(F32)[:, None] * inv_freq[None, :]
    cos, sin = jnp.cos(ang), jnp.sin(ang)
    cos_t = jnp.tile(cos, (1, LANES // half))
    sin_t = jnp.tile(jnp.concatenate([-sin, sin], axis=1), (1, LANES // HEAD_DIM))
    return cos_t, sin_t


def _swap_halves(x):
    lane = lax.broadcasted_iota(jnp.int32, x.shape, 1)
    first = (lane % HEAD_DIM) < (HEAD_DIM // 2)
    return jnp.where(first, pltpu.roll(x, LANES - HEAD_DIM // 2, 1), pltpu.roll(x, HEAD_DIM // 2, 1))


def _rotary(x, cos_t, sin_t, n_rot, inverse, name):
    S, C = x.shape
    ts = _tile(S, 512, 16)
    ng = C // LANES

    def body(x_ref, c_ref, s_ref, o_ref):
        cs, sn = c_ref[...], s_ref[...]
        for gidx in range(ng):
            sl = slice(gidx * LANES, (gidx + 1) * LANES)
            v = x_ref[:, sl].astype(F32)
            if gidx < n_rot:
                if inverse:
                    v = v * cs + _swap_halves(v * sn)
                else:
                    v = v * cs + _swap_halves(v) * sn
            o_ref[:, sl] = v.astype(BF16)

    row = pl.BlockSpec((ts, C), lambda i: (i, 0))
    tab = pl.BlockSpec((ts, LANES), lambda i: (i, 0))
    return pl.pallas_call(
        body, name=name, out_shape=jax.ShapeDtypeStruct((S, C), BF16),
        grid=(S // ts,), in_specs=[row, tab, tab], out_specs=row,
        compiler_params=_params("parallel"),
    )(x, cos_t, sin_t)


def _head_masks():
    lane = lax.broadcasted_iota(jnp.int32, (BLK, LANES), 1)
    return lane < HEAD_DIM


def _split_bf16(x):
    hi = x.astype(BF16)
    lo = (x - hi.astype(F32)).astype(BF16)
    return hi, lo


def _sb_scores(qh, ks, carry, diag, tri_excl, strict):
    n_heads = len(qh)
    zs = [_dot(ks[n], qh[n], NT) for n in range(n_heads)]
    a_l, b_l, split_l = [], [], []
    for z in zs:
        a = jnp.minimum(z, 0.0) - jnp.log(1.0 + jnp.exp(-jnp.abs(z)))
        b = a - z
        if diag:
            b = jnp.where(strict, b, 0.0)
        a_l.append(a)
        b_l.append(b)
        split_l.append(_split_bf16(b))
    sufs = [_dot(tri_excl, hi, NN) + _dot(tri_excl, lo, NN) for hi, lo in split_l]
    w_l = []
    for n in range(n_heads):
        w = jnp.exp(a_l[n] + sufs[n] + carry[n])
        if diag:
            w = jnp.where(strict, w, 0.0)
        w_l.append(w)
    return a_l, b_l, w_l


SB_FWD_PAIRS = 4
SB_FWD_QBLOCKS = 4
SB_BWD_PAIRS = 2
SB_BWD_QBLOCKS = 4


def _any_alive(carries):
    top = carries[0]
    for c in carries[1:]:
        top = jnp.maximum(top, c)
    return (jnp.max(top) > SB_LOG_FLOOR).astype(jnp.int32)


def _sb_masks():
    row = lax.broadcasted_iota(jnp.int32, (BLK, BLK), 0)
    col = lax.broadcasted_iota(jnp.int32, (BLK, BLK), 1)
    tri_excl = jnp.where(col > row, 1.0, 0.0).astype(BF16)
    tri_incl = jnp.where(col >= row, 1.0, 0.0).astype(BF16)
    return row < HEAD_DIM, row < col, tri_excl, tri_incl


def _sb_fwd(qkv, kv_t, name, carry=None):
    S, D3 = qkv.shape
    D = D3 // 3
    npair, nb = D // LANES, S // BLK
    P = min(SB_FWD_PAIRS, npair)
    ngroup = npair // P
    W = P * LANES

    QB = SB_FWD_QBLOCKS if nb % SB_FWD_QBLOCKS == 0 else 1
    nch = QB * 2 * P

    def body(q_ref, k_ref, vt_ref, o_ref):
        i_first = pl.program_id(1) * QB
        m0 = _head_masks()
        top, strict, tri_excl, _ = _sb_masks()
        zq = jnp.zeros((BLK, LANES), BF16)
        lanes = [slice(p * LANES, (p + 1) * LANES) for p in range(P)]
        qh = []
        for qb in range(QB):
            for sl in lanes:
                q2 = q_ref[qb * BLK:(qb + 1) * BLK, sl] * ATTN_SCALE
                qh += [jnp.where(m0, q2, zq), jnp.where(m0, zq, q2)]

        def block(js, carry, acc, diag):
            offs = [pl.multiple_of(j * BLK, BLK) for j in js]
            ks, vth = [], []
            for qb in range(QB):
                for sl in lanes:
                    k2 = k_ref[pl.ds(offs[qb], BLK), sl]
                    vt = vt_ref[sl, pl.ds(offs[qb], BLK)]
                    ks += [k2, k2]
                    vth += [jnp.where(top, vt, zq), jnp.where(top, zq, vt)]
            _, b_l, w_l = _sb_scores(qh, ks, carry, diag, tri_excl, strict)
            wb = [w.astype(BF16) for w in w_l]
            new_acc = [acc[m] + _dot(vth[2 * m], wb[2 * m], NN) + _dot(vth[2 * m + 1], wb[2 * m + 1], NN)
                       for m in range(QB * P)]
            new_carry = [carry[n] + jnp.sum(b_l[n], axis=0, keepdims=True) for n in range(nch)]
            return new_carry, new_acc

        c0 = jnp.zeros((1, BLK), F32)
        carry, acc = block([i_first + qb for qb in range(QB)], [c0] * nch,
                           [jnp.zeros((LANES, BLK), F32)] * (QB * P), True)

        def cond(st):
            return jnp.logical_and(i_first + QB - 1 - st[0] >= 0, st[1] > 0)

        def step(st):
            t, _, carry, acc = st
            js = [i_first + qb - t for qb in range(QB)]
            carry = [carry[n] if n // (2 * P) == QB - 1 else jnp.where(js[n // (2 * P)] >= 0, carry[n], NEG_BIG)
                     for n in range(nch)]
            carry, acc = block([jnp.maximum(j, 0) for j in js], carry, acc, False)
            return t + 1, _any_alive(carry), carry, acc

        st = lax.while_loop(cond, step, (1, _any_alive(carry), carry, acc))
        for qb in range(QB):
            for p, sl in enumerate(lanes):
                o_ref[qb * BLK:(qb + 1) * BLK, sl] = jnp.transpose(st[3][qb * P + p])

    return _pcall(
        body, (qkv, qkv, kv_t), name=name, out_shape=jax.ShapeDtypeStruct((S, D), F32),
        grid=(ngroup, nb // QB),
        in_specs=[pl.BlockSpec((QB * BLK, W), lambda g, i: (i, g)),
                  pl.BlockSpec((S, W), lambda g, i: (0, ngroup + g)),
                  pl.BlockSpec((W, S), lambda g, i: (ngroup + g, 0))],
        out_specs=pl.BlockSpec((QB * BLK, W), lambda g, i: (i, g)),
        sem=("arbitrary", "arbitrary"), carry=carry)


def _sb_bwd(qkv, kv_t, o, do, name, carry=None):
    S, D3 = qkv.shape
    D = D3 // 3
    npair, nb = D // LANES, S // BLK
    P = min(SB_BWD_PAIRS, npair)
    ngroup = npair // P
    W = P * LANES

    QB = SB_BWD_QBLOCKS if nb % SB_BWD_QBLOCKS == 0 else 1
    nch = QB * 2 * P

    def body(q_ref, o_ref, do_ref, qkv_hbm, kt_hbm, dq_ref, dk_ref, dv_ref, k_ref, v_ref, kt_ref, sems):
        grp = pl.program_id(0)
        i_first = pl.program_id(1) * QB
        m0 = _head_masks()
        top, strict, tri_excl, tri_incl = _sb_masks()
        zq = jnp.zeros((BLK, LANES), BF16)
        lanes = [slice(p * LANES, (p + 1) * LANES) for p in range(P)]

        @pl.when(pl.program_id(1) == 0)
        def _():
            copies = [pltpu.make_async_copy(qkv_hbm.at[:, pl.ds(pl.multiple_of((c * ngroup + grp) * W, LANES), W)],
                                            ref, sems.at[c - 1]) for c, ref in ((1, k_ref), (2, v_ref))]
            copies.append(pltpu.make_async_copy(kt_hbm.at[pl.ds(pl.multiple_of(grp * W, LANES), W), :],
                                                kt_ref, sems.at[2]))
            for cp in copies:
                cp.start()
            dk_ref[...] = jnp.zeros_like(dk_ref)
            dv_ref[...] = jnp.zeros_like(dv_ref)
            for cp in copies:
                cp.wait()

        qh, doh, delta = [], [], []
        for qb in range(QB):
            rs = slice(qb * BLK, (qb + 1) * BLK)
            for sl in lanes:
                q2, do2 = q_ref[rs, sl] * ATTN_SCALE, do_ref[rs, sl]
                qh += [jnp.where(m0, q2, zq), jnp.where(m0, zq, q2)]
                doh += [jnp.where(m0, do2, zq), jnp.where(m0, zq, do2)]
                prod_t = jnp.transpose(do2.astype(F32) * o_ref[rs, sl])
                delta += [jnp.sum(jnp.where(top, prod_t, 0.0), axis=0, keepdims=True),
                          jnp.sum(jnp.where(top, 0.0, prod_t), axis=0, keepdims=True)]

        def block(js, valid, cb, cg, dq, diag):
            offs = [pl.multiple_of(j * BLK, BLK) for j in js]
            ks, vs, kth = [], [], []
            for qb in range(QB):
                for sl in lanes:
                    k2, v2 = k_ref[pl.ds(offs[qb], BLK), sl], v_ref[pl.ds(offs[qb], BLK), sl]
                    ks += [k2, k2]
                    vs += [v2, v2]
                    kt = kt_ref[sl, pl.ds(offs[qb], BLK)] * ATTN_SCALE
                    kth += [jnp.where(top, kt, zq), jnp.where(top, zq, kt)]
            dws = [_dot(vs[n], doh[n], NT) for n in range(nch)]
            a_l, b_l, w_l = _sb_scores(qh, ks, cb, diag, tri_excl, strict)
            wb = [w.astype(BF16) for w in w_l]
            g_l = [dws[n] * wb[n].astype(F32) for n in range(nch)]
            gsplit = [_split_bf16(g) for g in g_l]
            gincs = [_dot(tri_incl, hi, NN) + _dot(tri_incl, lo, NN) for hi, lo in gsplit]
            dzs = []
            for n in range(nch):
                beta = jnp.exp(a_l[n])
                dz = g_l[n] - beta * (g_l[n] + ((delta[n] - cg[n]) - gincs[n]))
                if diag:
                    dz = jnp.where(strict, dz, 0.0)
                if valid[n // (2 * P)] is not None:
                    dz = jnp.where(valid[n // (2 * P)], dz, 0.0)
                dzs.append(dz.astype(BF16))
            ndq = []
            for qb in range(QB):
                for p, sl in enumerate(lanes):
                    n0 = qb * 2 * P + 2 * p
                    ndq.append(dq[qb * P + p] + _dot(kth[n0], dzs[n0], NN) + _dot(kth[n0 + 1], dzs[n0 + 1], NN))
                    dk_ref[pl.ds(offs[qb], BLK), sl] += _dot(dzs[n0], qh[n0], NN) + _dot(dzs[n0 + 1], qh[n0 + 1], NN)
                    dv_ref[pl.ds(offs[qb], BLK), sl] += _dot(wb[n0], doh[n0], NN) + _dot(wb[n0 + 1], doh[n0 + 1], NN)
            ncb = [cb[n] + jnp.sum(b_l[n], axis=0, keepdims=True) for n in range(nch)]
            ncg = [cg[n] + jnp.sum(g_l[n], axis=0, keepdims=True) for n in range(nch)]
            return ncb, ncg, ndq

        c0 = jnp.zeros((1, BLK), F32)
        cb, cg, dq = block([i_first + qb for qb in range(QB)], [None] * QB, [c0] * nch, [c0] * nch,
                           [jnp.zeros((LANES, BLK), F32)] * (QB * P), True)

        def cond(st):
            return jnp.logical_and(i_first + QB - 1 - st[0] >= 0, st[1] > 0)

        def step(st):
            t, _, cb, cg, dq = st
            js = [i_first + qb - t for qb in range(QB)]
            valid = [js[qb] >= 0 for qb in range(QB - 1)] + [None]
            cb = [cb[n] if valid[n // (2 * P)] is None else jnp.where(valid[n // (2 * P)], cb[n], NEG_BIG)
                  for n in range(nch)]
            cb, cg, dq = block([jnp.maximum(j, 0) for j in js], valid, cb, cg, dq, False)
            return t + 1, _any_alive(cb), cb, cg, dq

        st = lax.while_loop(cond, step, (1, _any_alive(cb), cb, cg, dq))
        for qb in range(QB):
            for p, sl in enumerate(lanes):
                dq_ref[qb * BLK:(qb + 1) * BLK, sl] = jnp.transpose(st[4][qb * P + p]).astype(BF16)

    blk = pl.BlockSpec((QB * BLK, W), lambda g, i: (i, g))
    col_all = pl.BlockSpec((S, W), lambda g, i: (0, g))
    hbm = pl.BlockSpec(memory_space=pl.ANY)
    return _pcall(
        body, (qkv, o, do, qkv, kv_t), name=name,
        out_shape=(jax.ShapeDtypeStruct((S, D), BF16), jax.ShapeDtypeStruct((S, D), F32),
                   jax.ShapeDtypeStruct((S, D), F32)),
        grid=(ngroup, nb // QB),
        in_specs=[blk, blk, blk, hbm, hbm],
        out_specs=(blk, col_all, col_all),
        scratch_shapes=[pltpu.VMEM((S, W), BF16), pltpu.VMEM((S, W), BF16), pltpu.VMEM((W, S), BF16),
                        pltpu.SemaphoreType.DMA((3,))],
        sem=("arbitrary", "arbitrary"), carry=carry)


SWA_Q_GROUPS = 4


def _roll_heads(x):
    return pltpu.roll(x.astype(F32), HEAD_DIM, 1).astype(BF16)


def _roll_rows(x):
    return pltpu.roll(x.astype(F32), HEAD_DIM, 0).astype(BF16)


def _swa_valid(i):
    k = lax.broadcasted_iota(jnp.int32, (2 * BLK, BLK), 0)
    q = lax.broadcasted_iota(jnp.int32, (2 * BLK, BLK), 1)
    diff = q + BLK - k
    return (diff >= 0) & (diff < BLK) & ((i > 0) | (k >= BLK))


def _swa_probs(z, valid, sink):
    z = jnp.where(valid, z * ATTN_SCALE, NEG_BIG)
    mx = jnp.maximum(jnp.max(z, axis=0, keepdims=True), sink)
    p = jnp.exp(z - mx)
    ps = jnp.exp(sink - mx)
    inv = 1.0 / (jnp.sum(p, axis=0, keepdims=True) + ps)
    return p * inv, ps * inv


def _swa_operands(q_ref, kc_ref, kp_ref, vc_ref, vp_ref, tc_ref, tp_ref, s_ref, nkvp):
    m0 = _head_masks()
    top = lax.broadcasted_iota(jnp.int32, (LANES, 2 * BLK), 0) < HEAD_DIM
    heads = []
    for m in range(nkvp):
        pair = slice(m * LANES, (m + 1) * LANES)
        kk = jnp.concatenate([kp_ref[:, pair], kc_ref[:, pair]], axis=0)
        vv = jnp.concatenate([vp_ref[:, pair], vc_ref[:, pair]], axis=0)
        tt = jnp.concatenate([tp_ref[pair, :], tc_ref[pair, :]], axis=1)
        ksw, vsw, tsw = _roll_heads(kk), _roll_heads(vv), _roll_rows(tt)
        zt = jnp.zeros_like(tt)
        for c in range(SWA_Q_GROUPS):
            q_lanes = slice((m * SWA_Q_GROUPS + c) * LANES, (m * SWA_Q_GROUPS + c + 1) * LANES)
            qc = q_ref[:, q_lanes]
            zq = jnp.zeros_like(qc)
            for u in range(2):
                same = u == c // 2
                sel = (lambda x, z, mk: jnp.where(mk, x, z)) if u == 0 else (lambda x, z, mk: jnp.where(mk, z, x))
                heads.append(dict(
                    m=m, q_lanes=q_lanes, same=same, sel=sel, qm=sel(qc, zq, m0),
                    k=kk if same else ksw, v=vv if same else vsw,
                    tm=sel(tt if same else tsw, zt, top),
                    sink=s_ref[0, (m * SWA_Q_GROUPS + c) * 2 + u]))
    return heads, m0


def _swa_specs(D, half, t_block):
    prev = lambda i: jnp.maximum(i - 1, 0)
    return [pl.BlockSpec((BLK, D), lambda i: (i, 0)),
            pl.BlockSpec((BLK, half), lambda i: (i, 0)),
            pl.BlockSpec((BLK, half), lambda i: (prev(i), 0)),
            pl.BlockSpec((BLK, half), lambda i: (i, 1)),
            pl.BlockSpec((BLK, half), lambda i: (prev(i), 1)),
            pl.BlockSpec((half, BLK), lambda i: (t_block, i)),
            pl.BlockSpec((half, BLK), lambda i: (t_block, prev(i))),
            pl.BlockSpec(memory_space=pltpu.SMEM)]


def _swa_fwd(q, kv, kv_t, sinks, name):
    S, D = q.shape
    half = kv.shape[1] // 2
    nkvp = half // LANES

    def body(q_ref, kc_ref, kp_ref, vc_ref, vp_ref, tc_ref, tp_ref, s_ref, o_ref):
        valid = _swa_valid(pl.program_id(0))
        heads, _ = _swa_operands(q_ref, kc_ref, kp_ref, vc_ref, vp_ref, tc_ref, tp_ref, s_ref, nkvp)
        zs = [_dot(hd["k"], hd["qm"], NT) for hd in heads]
        ps = [_swa_probs(z, valid, hd["sink"])[0].astype(BF16) for z, hd in zip(zs, heads)]
        for n in range(0, len(heads), 2):
            o_t = _dot(heads[n]["tm"], ps[n], NN) + _dot(heads[n + 1]["tm"], ps[n + 1], NN)
            o_ref[:, heads[n]["q_lanes"]] = jnp.transpose(o_t)

    return pl.pallas_call(
        body, name=name, out_shape=jax.ShapeDtypeStruct((S, D), F32),
        grid=(S // BLK,),
        in_specs=_swa_specs(D, half, 1),
        out_specs=pl.BlockSpec((BLK, D), lambda i: (i, 0)),
        compiler_params=_params("arbitrary"),
    )(q, kv, kv, kv, kv, kv_t, kv_t, sinks)


def _swa_bwd(q, kv, kv_t, sinks, o, do, cos_t, sin_t, name, carry=None):
    S, D = q.shape
    half = kv.shape[1] // 2
    nkvp = half // LANES
    nh = nkvp * 2 * SWA_Q_GROUPS

    def body(q_ref, kc_ref, kp_ref, vc_ref, vp_ref, tc_ref, tp_ref, s_ref, o_ref, do_ref, c_ref, sn_ref,
             dq_ref, dk_ref, dv_ref, ds_ref):
        i = pl.program_id(0)
        valid = _swa_valid(i)
        heads, m0 = _swa_operands(q_ref, kc_ref, kp_ref, vc_ref, vp_ref, tc_ref, tp_ref, s_ref, nkvp)
        top_q = lax.broadcasted_iota(jnp.int32, (LANES, BLK), 0) < HEAD_DIM

        @pl.when(i == 0)
        def _():
            dk_ref[...] = jnp.zeros_like(dk_ref)
            dv_ref[...] = jnp.zeros_like(dv_ref)
            ds_ref[...] = jnp.zeros_like(ds_ref)

        doms, deltas = [], []
        for n in range(0, nh, 2):
            doc = do_ref[:, heads[n]["q_lanes"]]
            prod_t = jnp.transpose(doc.astype(F32) * o_ref[:, heads[n]["q_lanes"]])
            for hd in heads[n:n + 2]:
                doms.append(hd["sel"](doc, jnp.zeros_like(doc), m0))
                deltas.append(jnp.sum(hd["sel"](prod_t, 0.0, top_q), axis=0, keepdims=True))
        zs = [_dot(hd["k"], hd["qm"], NT) for hd in heads]
        dps = [_dot(hd["v"], dom, NT) for dom, hd in zip(doms, heads)]
        pbs, dscs = [], []
        for n, hd in enumerate(heads):
            p, psink = _swa_probs(zs[n], valid, hd["sink"])
            pbs.append(p.astype(BF16))
            dscs.append((p * (dps[n] - deltas[n]) * ATTN_SCALE).astype(BF16))
            ds_ref[n:n + 1, :] += -(psink * deltas[n])
        for n in range(0, nh, 2):
            dq_rot = jnp.transpose(_dot(heads[n]["tm"], dscs[n], NN) + _dot(heads[n + 1]["tm"], dscs[n + 1], NN))
            dq_ref[:, heads[n]["q_lanes"]] = (
                dq_rot * c_ref[...] + _swap_halves(dq_rot * sn_ref[...])).astype(BF16)
        acc = {}
        for n, hd in enumerate(heads):
            dk_n = _dot(dscs[n], hd["qm"], NN)
            dv_n = _dot(pbs[n], doms[n], NN)
            for key, val in ((("k", hd["m"], hd["same"]), dk_n), (("v", hd["m"], hd["same"]), dv_n)):
                acc[key] = val if key not in acc else acc[key] + val
        poff = pl.multiple_of(jnp.maximum(i - 1, 0) * BLK, BLK)
        coff = pl.multiple_of(i * BLK, BLK)
        for m in range(nkvp):
            pair = slice(m * LANES, (m + 1) * LANES)
            dkk = acc["k", m, True] + pltpu.roll(acc["k", m, False], HEAD_DIM, 1)
            dvv = acc["v", m, True] + pltpu.roll(acc["v", m, False], HEAD_DIM, 1)
            dk_ref[pl.ds(poff, BLK), pair] += dkk[:BLK]
            dv_ref[pl.ds(poff, BLK), pair] += dvv[:BLK]
            dk_ref[pl.ds(coff, BLK), pair] += dkk[BLK:]
            dv_ref[pl.ds(coff, BLK), pair] += dvv[BLK:]

    qblk = pl.BlockSpec((BLK, D), lambda i: (i, 0))
    whole = pl.BlockSpec((S, half), lambda i: (0, 0))
    tab = pl.BlockSpec((BLK, LANES), lambda i: (i, 0))
    return _pcall(
        body, (q, kv, kv, kv, kv, kv_t, kv_t, sinks, o, do, cos_t, sin_t), name=name,
        out_shape=(jax.ShapeDtypeStruct((S, D), BF16),
                   jax.ShapeDtypeStruct((S, half), F32),
                   jax.ShapeDtypeStruct((S, half), F32),
                   jax.ShapeDtypeStruct((nh, LANES), F32)),
        grid=(S // BLK,),
        in_specs=_swa_specs(D, half, 0) + [qblk, qblk, tab, tab],
        out_specs=(qblk, whole, whole, pl.BlockSpec((nh, LANES), lambda i: (0, 0))),
        sem=("arbitrary",), carry=carry)


def _dev_index(p):
    return 4 * p[0] + 2 * p[1] + p[2]


def _gather_plan(x_refs, out_refs, send_sems, recv_sems, local_sems):
    n = len(x_refs)
    x_, y_, c_ = lax.axis_index("x"), lax.axis_index("y"), lax.axis_index("c")
    me, sibling = (x_, y_, c_), (x_, y_, 1 - c_)
    chips = [(1 - x_, y_), (x_, 1 - y_), (1 - x_, 1 - y_)]

    def copy(t, k, block, to, src=None):
        dst = out_refs[t].at[_dev_index(block)]
        return pltpu.make_async_remote_copy(
            src_ref=dst if src is None else src, dst_ref=dst,
            send_sem=send_sems.at[7 * t + k], recv_sem=recv_sems.at[7 * t + k],
            device_id=to, device_id_type=MESH)

    mine = [pltpu.make_async_copy(x_refs[t], out_refs[t].at[_dev_index(me)], local_sems.at[t]) for t in range(n)]
    first = []
    for t in range(n):
        first.append(copy(t, 0, me, sibling, src=x_refs[t]))
        first += [copy(t, 1 + j, me, (*chip, c_), src=x_refs[t]) for j, chip in enumerate(chips)]
    arrived = lambda t, j: copy(t, 1 + j, (*chips[j], c_), me)
    forward = lambda t, j: copy(t, 4 + j, (*chips[j], c_), sibling)
    from_sibling = lambda t: copy(t, 0, sibling, me)
    forwarded = lambda t, j: copy(t, 4 + j, (*chips[j], 1 - c_), me)
    return n, mine, first, arrived, forward, from_sibling, forwarded


def _gather_start(x_refs, out_refs, send_sems, recv_sems, local_sems):
    _, mine, first, *_ = _gather_plan(x_refs, out_refs, send_sems, recv_sems, local_sems)
    for cp in mine + first:
        cp.start()


def _gather_forward(x_refs, out_refs, send_sems, recv_sems, local_sems):
    n, _, _, arrived, forward, _, _ = _gather_plan(x_refs, out_refs, send_sems, recv_sems, local_sems)
    for j in range(3):
        for t in range(n):
            arrived(t, j).wait_recv()
            forward(t, j).start()


def _gather_finish(x_refs, out_refs, send_sems, recv_sems, local_sems):
    n, mine, first, _, forward, from_sibling, forwarded = _gather_plan(
        x_refs, out_refs, send_sems, recv_sems, local_sems)
    for t in range(n):
        from_sibling(t).wait_recv()
    for j in range(3):
        for t in range(n):
            forwarded(t, j).wait_recv()
    for cp in first + [forward(t, j) for j in range(3) for t in range(n)]:
        cp.wait_send()
    for cp in mine:
        cp.wait()


def _scatter_plan(b_refs, out_refs, send_sems, recv_sems, local_sems):
    n = len(b_refs)
    x_, y_, c_ = lax.axis_index("x"), lax.axis_index("y"), lax.axis_index("c")
    my_idx = _dev_index((x_, y_, c_))
    mine = [pltpu.make_async_copy(b_refs[t].at[my_idx], out_refs[t].at[my_idx], local_sems.at[t]) for t in range(n)]
    copies = []
    for t in range(n):
        for k in range(1, N_DEV):
            peer = (x_ ^ ((k >> 2) & 1), y_ ^ ((k >> 1) & 1), c_ ^ (k & 1))
            copies.append(pltpu.make_async_remote_copy(
                src_ref=b_refs[t].at[_dev_index(peer)], dst_ref=out_refs[t].at[my_idx],
                send_sem=send_sems.at[7 * t + k - 1], recv_sem=recv_sems.at[7 * t + k - 1],
                device_id=peer, device_id_type=MESH))
    return mine, copies


def _scatter_start(b_refs, out_refs, send_sems, recv_sems, local_sems):
    mine, copies = _scatter_plan(b_refs, out_refs, send_sems, recv_sems, local_sems)
    for cp in mine + copies:
        cp.start()


def _scatter_finish(b_refs, out_refs, send_sems, recv_sems, local_sems):
    mine, copies = _scatter_plan(b_refs, out_refs, send_sems, recv_sems, local_sems)
    for cp in copies:
        cp.wait_recv()
    for cp in copies:
        cp.wait_send()
    for cp in mine:
        cp.wait()


def _exchange_operands(kind, tensors):
    if kind == "gather":
        args = list(tensors)
        shapes = [jax.ShapeDtypeStruct((N_DEV,) + t.shape, t.dtype) for t in tensors]
        return args, shapes, (_gather_start, _gather_forward, _gather_finish)
    args = [t.reshape(N_DEV, t.shape[0] // N_DEV, t.shape[1]) for t in tensors]
    shapes = [jax.ShapeDtypeStruct(a.shape, a.dtype) for a in args]
    return args, shapes, (_scatter_start, None, _scatter_finish)


def _exchange_results(kind, tensors, res):
    if kind == "gather":
        return [r.reshape(N_DEV * t.shape[0], t.shape[1]) for r, t in zip(res, tensors)]
    return list(res)


def _exchange_sems(n):
    return [pltpu.SemaphoreType.DMA((7 * n,)), pltpu.SemaphoreType.DMA((7 * n,)), pltpu.SemaphoreType.DMA((n,))]


def _exchange(kind, tensors, name):
    n = len(tensors)
    args, shapes, phases = _exchange_operands(kind, tensors)

    def body(*refs):
        for phase in phases:
            if phase is not None:
                phase(refs[:n], refs[n:2 * n], *refs[2 * n:])

    hbm = pl.BlockSpec(memory_space=pl.ANY)
    res = pl.pallas_call(body, name=name, out_shape=shapes, in_specs=[hbm] * n, out_specs=[hbm] * n,
                         scratch_shapes=_exchange_sems(n))(*args)
    return _exchange_results(kind, tensors, res)


def _pcall(body, args, *, name, out_shape, grid, in_specs, out_specs, sem, scratch_shapes=(), carry=None):
    if carry is None:
        out = pl.pallas_call(body, name=name, out_shape=out_shape, grid=grid, in_specs=list(in_specs),
                             out_specs=out_specs, scratch_shapes=list(scratch_shapes),
                             compiler_params=_params(*sem))(*args)
        return out, None
    kind, tensors = carry
    multi = isinstance(out_shape, (tuple, list))
    shapes = list(out_shape) if multi else [out_shape]
    ospecs = list(out_specs) if multi else [out_specs]
    n_in, n_out, n_scr, n_c = len(in_specs), len(shapes), len(scratch_shapes), len(tensors)
    c_args, c_shapes, (start, forward, finish) = _exchange_operands(kind, tensors)
    n_steps = 1
    for g in grid:
        n_steps *= g
    late = (3 * n_steps) // 4

    def wrapped(*refs):
        ins, rest = refs[:n_in], refs[n_in:]
        c_in, rest = rest[:n_c], rest[n_c:]
        outs, rest = rest[:n_out], rest[n_out:]
        c_out, rest = rest[:n_c], rest[n_c:]
        scr, sems = rest[:n_scr], rest[n_scr:]
        step = pl.program_id(0)
        for a in range(1, len(grid)):
            step = step * grid[a] + pl.program_id(a)

        @pl.when(step == 0)
        def _():
            start(c_in, c_out, *sems)

        body(*ins, *outs, *scr)

        if forward is not None:
            @pl.when(step == late)
            def _():
                forward(c_in, c_out, *sems)

        @pl.when(step == n_steps - 1)
        def _():
            finish(c_in, c_out, *sems)

    hbm = pl.BlockSpec(memory_space=pl.ANY)
    res = pl.pallas_call(
        wrapped, name=name, out_shape=shapes + c_shapes, grid=grid,
        in_specs=list(in_specs) + [hbm] * n_c, out_specs=ospecs + [hbm] * n_c,
        scratch_shapes=list(scratch_shapes) + _exchange_sems(n_c),
        compiler_params=_params(*sem))(*args, *c_args)
    outs = tuple(res[:n_out]) if multi else res[0]
    return outs, _exchange_results(kind, tensors, res[n_out:])


def _sum8(parts, name):
    _, R, C = parts.shape
    tr = _tile(R, 256, 16)

    def body(p_ref, g_ref):
        g = p_ref[0].astype(F32)
        for s in range(1, N_DEV):
            g = g + p_ref[s].astype(F32)
        g_ref[...] = g

    return pl.pallas_call(
        body, name=name, out_shape=jax.ShapeDtypeStruct((R, C), F32),
        grid=(R // tr,),
        in_specs=[pl.BlockSpec((N_DEV, tr, C), lambda i: (0, i, 0))],
        out_specs=pl.BlockSpec((tr, C), lambda i: (i, 0)),
        compiler_params=_params("parallel"),
    )(parts)


def _adamw(g, w, m, v, name):
    R, C = g.shape
    tr = _tile(R, 256, 8)
    c1 = 1.0 - ADAM_B1 ** ADAM_STEP
    c2 = 1.0 - ADAM_B2 ** ADAM_STEP

    def body(g_ref, w_ref, m_ref, v_ref, d_ref, nm_ref, nv_ref):
        gg = g_ref[...]
        nm = ADAM_B1 * m_ref[...] + (1.0 - ADAM_B1) * gg
        nv = ADAM_B2 * v_ref[...] + (1.0 - ADAM_B2) * (gg * gg)
        m_hat = nm / c1
        v_hat = nv / c2
        nm_ref[...] = nm
        nv_ref[...] = nv
        d_ref[...] = -ADAM_LR * (m_hat / (jnp.sqrt(v_hat) + ADAM_EPS) + ADAM_WD * w_ref[...])

    row = pl.BlockSpec((tr, C), lambda i: (i, 0))
    shp = jax.ShapeDtypeStruct((R, C), F32)
    return pl.pallas_call(
        body, name=name, out_shape=(shp, shp, shp),
        grid=(R // tr,), in_specs=[row, row, row, row], out_specs=(row, row, row),
        compiler_params=_params("parallel"),
    )(g, w, m, v)


def _ffn_down(act, wo, h, tag):
    return _mm(act, wo, NN, F32, f"{tag}_down", scale=FFN_RES_SCALE, res=h, tm=512, tn=1024, tk=2816)


def _ffn_fwd(h, g, win_t, wo, tag, carry=None, loss=None):
    return _ffn_fwd_fused(h, g, win_t, wo, f"{tag}_fwd", carry=carry, loss=loss)


def _ffn_bwd(dh, h, g, win_t, wo, saved, tag, scatter=False, carry=None, carry_dwin=None):
    xn, silu, dsilu, up, act = saved
    dwo = _mm(act, dh, TN, BF16, f"{tag}_dwo", scale=FFN_RES_SCALE, tm=1408, tn=1024, tk=TN_CHUNK)
    if not scatter:
        (dh_in, dg, dgate, dup), got = _ffn_bwd_fused(dh, h, g, win_t, wo, silu, dsilu, up, f"{tag}_bwd", carry=carry)
        dwin_t, got_dwin = _dw_rows([dgate, dup], xn, f"{tag}_dwin", carry=carry_dwin)
        return dh_in, dg, dwin_t, dwo, got, got_dwin
    dgate, dup = _ffn_dact(dh, wo, silu, dsilu, up, f"{tag}_dact")
    dwin_t, got_wo = _dw_rows([dgate, dup], xn, f"{tag}_dwin", carry=("scatter", [dwo]))
    (dh_in, dg), got_win = _dx_norm_bwd([(dgate, win_t, NN, 2, 0), (dup, win_t, NN, 2, 1)], h, g, dh, f"{tag}_dx",
                                        carry=("scatter", [dwin_t]))
    return dh_in, dg, got_win[0], got_wo[0]


def _proj(a, w, dims, out_dtype, name, res=None):
    return _mm(a, w, dims, out_dtype, name, res=res, tm=1024, tn=1024, tk=1024)


def _proj_dw(x, dy, name):
    return _mm(x, dy, TN, BF16, name, tm=1024, tn=1024, tk=TN_CHUNK)


def kernel(x, ffn1_norm, ffn1_w_in, ffn1_w_out, mix_norm, ffn2_norm, ffn2_w_in, ffn2_w_out, sb_w_qkv, sb_w_o, kv_norm, kv_w, swa_w_q, swa_sinks, swa_w_o, final_norm, loss_target, m_ffn1_norm, m_ffn1_w_in, m_ffn1_w_out, m_mix_norm, m_ffn2_norm, m_ffn2_w_in, m_ffn2_w_out, m_sb_w_qkv, m_sb_w_o, m_kv_norm, m_kv_w, m_swa_w_q, m_swa_sinks, m_swa_w_o, m_final_norm, v_ffn1_norm, v_ffn1_w_in, v_ffn1_w_out, v_mix_norm, v_ffn2_norm, v_ffn2_w_in, v_ffn2_w_out, v_sb_w_qkv, v_sb_w_o, v_kv_norm, v_kv_w, v_swa_w_q, v_swa_sinks, v_swa_w_o, v_final_norm):
    S, D = x.shape[1], x.shape[2]
    L = ffn1_w_in.shape[0]
    KV = kv_w.shape[1]
    assert L == 2 and swa_sinks.shape == (1, 2 * SWA_Q_GROUPS * KV // (2 * LANES))

    def bf(w):
        return w.astype(BF16)

    def bft(w):
        return jnp.transpose(w).astype(BF16)

    cos_t, sin_t = _rope_tables(S)
    h0 = x.reshape(S, D)
    tgt = loss_target.reshape(S, D)

    win1a_t, = _exchange("gather", [bft(ffn1_w_in[0])], "gather_first_weight")
    sv_a1, (wo1a, wqkv_t, w_sbo) = _ffn_up(
        h0, ffn1_norm[0], win1a_t, "ffn1a_up",
        carry=("gather", [bf(ffn1_w_out[0]), bft(sb_w_qkv[0]), bf(sb_w_o[0])]))
    h1 = _ffn_down(sv_a1[-1], wo1a, h0, "ffn1a")
    hn_a, qkv, kv_t = _norm_proj(h1, mix_norm[0], wqkv_t, NT, "sb_qkv", tail_t=2 * D)
    o_sb, (win2a_t, wo2a, w_kv) = _sb_fwd(qkv, kv_t, "sb_attn", carry=("gather", [
        bft(ffn2_w_in[0]), bf(ffn2_w_out[0]), bf(kv_w)]))
    h2 = _proj(o_sb, w_sbo, NN, F32, "sb_out", res=h1)
    h3, sv_a2, (win1b_t, wo1b, w_q, w_swo) = _ffn_fwd(h2, ffn2_norm[0], win2a_t, wo2a, "ffn2a", carry=("gather", [
        bft(ffn1_w_in[1]), bf(ffn1_w_out[1]), bf(swa_w_q[0]), bf(swa_w_o[0])]))
    kvn, kv_rot, kv_rot_t = _norm_proj(h3, kv_norm, w_kv, NN, "kv_proj", rope=(cos_t, sin_t, KV // (2 * LANES)),
                                       tail_t=KV)
    h4, sv_b1, (win2b_t, wo2b) = _ffn_fwd(h3, ffn1_norm[1], win1b_t, wo1b, "ffn1b", carry=("gather", [
        bft(ffn2_w_in[1]), bf(ffn2_w_out[1])]))
    hn_b, q_rot = _norm_proj(h4, mix_norm[1], w_q, NN, "swa_q", rope=(cos_t, sin_t, D // LANES))
    o_sw = _swa_fwd(q_rot, kv_rot, kv_rot_t, swa_sinks, "swa_attn")
    h5 = _proj(o_sw, w_swo, NN, F32, "swa_out", res=h4)
    (dh6, dg_final, sq_err), sv_b2, _ = _ffn_fwd(h5, ffn2_norm[1], win2b_t, wo2b, "ffn2b", loss=(final_norm, tgt))
    loss = lax.psum(0.5 * jnp.sum(sq_err) / D, ("x", "y", "c"))

    dh5, dg_f2b, dwin2b_t, dwo2b, _, _ = _ffn_bwd(dh6, h5, ffn2_norm[1], win2b_t, wo2b, sv_b2, "ffn2b")
    do_sw = _proj(dh5, w_swo, NT, BF16, "swa_out_dx")
    dw_swo = _proj_dw(o_sw, dh5, "swa_out_dw")
    (dq, dk_sw, dv_sw, dsink), (p_win2b, p_swo) = _swa_bwd(
        q_rot, kv_rot, kv_rot_t, swa_sinks, o_sw, do_sw, cos_t, sin_t, "swa_attn_bwd",
        carry=("scatter", [dwin2b_t, dw_swo]))
    dw_q = _proj_dw(hn_b, dq, "swa_q_dw")
    (dh4, dg_mix_b), _ = _dx_norm_bwd([(dq, w_q, NT, 1, 0)], h4, mix_norm[1], dh5, "swa_q_dx", tm=512)
    dh3, dg_f1b, dwin1b_t, dwo1b, (p_q, p_wo2b), _ = _ffn_bwd(dh4, h3, ffn1_norm[1], win1b_t, wo1b, sv_b1, "ffn1b",
                                                              carry=("scatter", [dw_q, dwo2b]))
    dkv = _rotary(jnp.concatenate([dk_sw, dv_sw], axis=1), cos_t, sin_t, KV // (2 * LANES), True, "kv_rope_bwd")
    dw_kv = _proj_dw(kvn, dkv, "kv_proj_dw")
    (dh3, dg_kv), _ = _dx_norm_bwd([(dkv, w_kv, NT, 1, 0)], h3, kv_norm, dh3, "kv_proj_dx", tm=512)
    dh2, dg_f2a, dwin2a_t, dwo2a, (p_win1b, p_kv), (p_wo1b,) = _ffn_bwd(
        dh3, h2, ffn2_norm[0], win2a_t, wo2a, sv_a2, "ffn2a",
        carry=("scatter", [dwin1b_t, dw_kv]), carry_dwin=("scatter", [dwo1b]))
    do_sb = _proj(dh2, w_sbo, NT, BF16, "sb_out_dx")
    dw_sbo = _proj_dw(o_sb, dh2, "sb_out_dw")
    (dq_sb, dk_sb, dv_sb), (p_win2a, p_wo2a, p_sbo) = _sb_bwd(
        qkv, kv_t, o_sb, do_sb, "sb_attn_bwd", carry=("scatter", [dwin2a_t, dwo2a, dw_sbo]))
    dqkv = [dq_sb, dk_sb, dv_sb]
    dwqkv_t, _ = _dw_rows(dqkv, hn_a, "sb_qkv_dw", tk=TN_CHUNK // 2)
    (dh1, dg_mix_a), (p_qkv,) = _dx_norm_bwd([(dy, wqkv_t, NN, 3, n) for n, dy in enumerate(dqkv)], h1, mix_norm[0],
                                             dh2, "sb_qkv_dx", carry=("scatter", [dwqkv_t]), tm=512)
    dx, dg_f1a, p_win1a, p_wo1a = _ffn_bwd(dh1, h0, ffn1_norm[0], win1a_t, wo1a, sv_a1, "ffn1a", scatter=True)

    def natural(parts, tag):
        return _sum8(parts, f"sum_{tag}")

    def from_t(parts, tag):
        return jnp.transpose(_sum8(parts, f"sum_{tag}"))

    grads = {
        "ffn1_w_in": jnp.stack([from_t(p_win1a, "win1a"), from_t(p_win1b, "win1b")]),
        "ffn1_w_out": jnp.stack([natural(p_wo1a, "wo1a"), natural(p_wo1b, "wo1b")]),
        "ffn2_w_in": jnp.stack([from_t(p_win2a, "win2a"), from_t(p_win2b, "win2b")]),
        "ffn2_w_out": jnp.stack([natural(p_wo2a, "wo2a"), natural(p_wo2b, "wo2b")]),
        "sb_w_qkv": from_t(p_qkv, "qkv")[None],
        "sb_w_o": natural(p_sbo, "sbo")[None],
        "kv_w": natural(p_kv, "kv"),
        "swa_w_q": natural(p_q, "swq")[None],
        "swa_w_o": natural(p_swo, "swo")[None],
    }

    small_w = [ffn1_norm, mix_norm, ffn2_norm, kv_norm, final_norm, swa_sinks]
    small_m = [m_ffn1_norm, m_mix_norm, m_ffn2_norm, m_kv_norm, m_final_norm, m_swa_sinks]
    small_v = [v_ffn1_norm, v_mix_norm, v_ffn2_norm, v_kv_norm, v_final_norm, v_swa_sinks]
    SMALL_ROWS = 16

    def pack_small(ts):
        rows_ = [t.reshape(-1, D) for t in ts[:-1]]
        sink_row = jnp.pad(ts[-1].reshape(1, -1), ((0, 0), (0, D - ts[-1].size)))
        flat = jnp.concatenate(rows_ + [sink_row], axis=0)
        return jnp.pad(flat, ((0, SMALL_ROWS - flat.shape[0]), (0, 0)))

    def unpack_small(flat):
        out, r = [], 0
        for t in small_w[:-1]:
            n = t.size // D
            out.append(flat[r:r + n].reshape(t.shape))
            r += n
        out.append(flat[r, :swa_sinks.size].reshape(swa_sinks.shape))
        return out

    def gain(parts8):
        return jnp.sum(parts8, axis=0, keepdims=True)

    g_small_local = pack_small([
        jnp.concatenate([gain(dg_f1a), gain(dg_f1b)], axis=0),
        jnp.concatenate([gain(dg_mix_a), gain(dg_mix_b)], axis=0),
        jnp.concatenate([gain(dg_f2a), gain(dg_f2b)], axis=0),
        gain(dg_kv), gain(dg_final), jnp.sum(dsink, axis=-1).reshape(1, -1)])
    small_parts = _exchange("gather", [g_small_local], "gather_small_grads")[0]
    g_small = _sum8(small_parts.reshape(N_DEV, SMALL_ROWS, D), "sum_small")
    d_small, nm_small, nv_small = _adamw(g_small, pack_small(small_w), pack_small(small_m), pack_small(small_v), "adamw_small")
    small_names = ["ffn1_norm", "mix_norm", "ffn2_norm", "kv_norm", "final_norm", "swa_sinks"]
    result = {"grad": dict(zip(small_names, unpack_small(g_small))),
              "delta": dict(zip(small_names, unpack_small(d_small))),
              "new_m": dict(zip(small_names, unpack_small(nm_small))),
              "new_v": dict(zip(small_names, unpack_small(nv_small)))}

    big = {"ffn1_w_in": (ffn1_w_in, m_ffn1_w_in, v_ffn1_w_in), "ffn1_w_out": (ffn1_w_out, m_ffn1_w_out, v_ffn1_w_out),
           "ffn2_w_in": (ffn2_w_in, m_ffn2_w_in, v_ffn2_w_in), "ffn2_w_out": (ffn2_w_out, m_ffn2_w_out, v_ffn2_w_out),
           "sb_w_qkv": (sb_w_qkv, m_sb_w_qkv, v_sb_w_qkv), "sb_w_o": (sb_w_o, m_sb_w_o, v_sb_w_o),
           "kv_w": (kv_w, m_kv_w, v_kv_w), "swa_w_q": (swa_w_q, m_swa_w_q, v_swa_w_q),
           "swa_w_o": (swa_w_o, m_swa_w_o, v_swa_w_o)}
    for nm, (w, m, v) in big.items():
        g = grads[nm]
        two_d = lambda t: t.reshape(-1, t.shape[-1])
        d, new_m, new_v = _adamw(two_d(g), two_d(w), two_d(m), two_d(v), f"adamw_{nm}")
        result["grad"][nm] = g
        result["delta"][nm] = d.reshape(w.shape)
        result["new_m"][nm] = new_m.reshape(w.shape)
        result["new_v"][nm] = new_v.reshape(w.shape)

    order = ["ffn1_norm", "ffn1_w_in", "ffn1_w_out", "mix_norm", "ffn2_norm", "ffn2_w_in", "ffn2_w_out",
             "sb_w_qkv", "sb_w_o", "kv_norm", "kv_w", "swa_w_q", "swa_sinks", "swa_w_o", "final_norm"]
    outs = [result[kind][nm] for kind in ("grad", "delta", "new_m", "new_v") for nm in order]
    return (loss, dx.reshape(x.shape), *outs)
```

```python
import jax
import jax.numpy as jnp
from jax import lax
from jax.experimental import pallas as pl
from jax.experimental.pallas import tpu as pltpu

F32 = jnp.float32
BF16 = jnp.bfloat16

N_DEV = 8
HEAD_DIM = 64
LANES = 128
BLK = 128
RMS_EPS = 1e-6
FFN_RES_SCALE = 0.5
ROPE_THETA = 10000.0
ATTN_SCALE = HEAD_DIM ** -0.5
SB_LOG_FLOOR = -88.0
NEG_BIG = -1e30
VMEM_LIMIT_V7X = 56 * 1024 * 1024

ADAM_LR = 0.001
ADAM_B1 = 0.9
ADAM_B2 = 0.999
ADAM_EPS = 1e-08
ADAM_WD = 0.01
ADAM_STEP = 10

NN = ((1,), (0,))
NT = ((1,), (1,))
TN = ((0,), (0,))
TN_CHUNK = 2048
MESH = pl.DeviceIdType.MESH


def _dot(a, b, dims):
    return lax.dot_general(a, b, (dims, ((), ())), preferred_element_type=F32)


def _tile(n, pref, mult=LANES):
    if n <= pref:
        return n
    t = (pref // mult) * mult
    while t >= mult:
        if n % t == 0:
            return t
        t -= mult
    return n


def _params(*sem):
    return pltpu.CompilerParams(dimension_semantics=sem, vmem_limit_bytes=VMEM_LIMIT_V7X)


def _mm(a, b, dims, out_dtype, name, scale=1.0, res=None, tm=512, tn=512, tk=512):
    if dims == NN:
        (M, K), (_, N) = a.shape, b.shape
    elif dims == NT:
        (M, K), (N, _) = a.shape, b.shape
    else:
        (K, M), (_, N) = a.shape, b.shape
    tm, tn, tk = _tile(M, tm), _tile(N, tn), _tile(K, tk)
    nk = K // tk
    if dims == TN:
        a_spec = pl.BlockSpec((tk, tm), lambda i, j, k: (k, i))
    else:
        a_spec = pl.BlockSpec((tm, tk), lambda i, j, k: (i, k))
    if dims == NT:
        b_spec = pl.BlockSpec((tn, tk), lambda i, j, k: (j, k))
    else:
        b_spec = pl.BlockSpec((tk, tn), lambda i, j, k: (k, j))
    o_spec = pl.BlockSpec((tm, tn), lambda i, j, k: (i, j))
    has_res = res is not None

    def body(*refs):
        a_ref, b_ref = refs[0], refs[1]
        r_ref = refs[2] if has_res else None
        o_ref = refs[3] if has_res else refs[2]

        def finish(acc):
            r = acc * scale if scale != 1.0 else acc
            if has_res:
                r = r + r_ref[...]
            o_ref[...] = r.astype(out_dtype)

        p = _dot(a_ref[...].astype(BF16), b_ref[...].astype(BF16), dims)
        if nk == 1:
            finish(p)
        else:
            acc_ref = refs[-1]
            k = pl.program_id(2)

            @pl.when(k == 0)
            def _():
                acc_ref[...] = p

            @pl.when(k > 0)
            def _():
                acc_ref[...] += p

            @pl.when(k == nk - 1)
            def _():
                finish(acc_ref[...])

    in_specs = [a_spec, b_spec] + ([o_spec] if has_res else [])
    args = (a, b) + ((res,) if has_res else ())
    return pl.pallas_call(
        body, name=name,
        out_shape=jax.ShapeDtypeStruct((M, N), out_dtype),
        grid=(M // tm, N // tn, nk),
        in_specs=in_specs, out_specs=o_spec,
        scratch_shapes=[pltpu.VMEM((tm, tn), F32)] if nk > 1 else [],
        compiler_params=_params("parallel", "parallel", "arbitrary"),
    )(*args)


def _rows8(x):
    r, d = x.shape
    return jnp.sum(x.reshape(r // 8, 8, d), axis=0)


def _norm_proj(h, g, w, dims, name, rope=None, tail_t=0):
    S, D = h.shape
    N = w.shape[1] if dims == NN else w.shape[0]
    tm = _tile(S, 512, 16)

    def body(h_ref, g_ref, w_ref, *rest):
        xn_ref, y_ref = rest[-3:-1] if tail_t else rest[-2:]
        x = h_ref[...]
        r = lax.rsqrt(jnp.mean(x * x, axis=-1, keepdims=True) + RMS_EPS)
        xn = ((x * r) * g_ref[...]).astype(BF16)
        xn_ref[...] = xn
        y = _dot(xn, w_ref[...], dims)
        if rope is not None:
            cs, sn = rest[0][...], rest[1][...]
            groups = [y[:, gidx * LANES:(gidx + 1) * LANES] for gidx in range(N // LANES)]
            y = jnp.concatenate([v * cs + _swap_halves(v) * sn if gidx < rope[2] else v
                                 for gidx, v in enumerate(groups)], axis=1)
        y_ref[...] = y.astype(BF16)
        if tail_t:
            rest[-1][...] = jnp.transpose(y[:, N - tail_t:]).astype(BF16)

    row = pl.BlockSpec((tm, D), lambda i: (i, 0))
    tab = pl.BlockSpec((tm, LANES), lambda i: (i, 0))
    in_specs = [row, pl.BlockSpec((1, D), lambda i: (0, 0)), pl.BlockSpec(w.shape, lambda i: (0, 0))]
    args = (h, g.reshape(1, D), w)
    if rope is not None:
        in_specs += [tab, tab]
        args += (rope[0], rope[1])
    out_shape = [jax.ShapeDtypeStruct((S, D), BF16), jax.ShapeDtypeStruct((S, N), BF16)]
    out_specs = [row, pl.BlockSpec((tm, N), lambda i: (i, 0))]
    if tail_t:
        out_shape.append(jax.ShapeDtypeStruct((tail_t, S), BF16))
        out_specs.append(pl.BlockSpec((tail_t, tm), lambda i: (0, i)))
    return pl.pallas_call(
        body, name=name, out_shape=out_shape, grid=(S // tm,),
        in_specs=in_specs, out_specs=out_specs,
        compiler_params=_params("parallel"),
    )(*args)


def _ffn_up(h, g, win_t, name, carry=None):
    S, D = h.shape
    F = win_t.shape[0] // 2
    tm, tn = _tile(S, 512, 16), _tile(F, 1408)
    nf = F // tn

    def body(h_ref, g_ref, wg_ref, wu_ref, xn_ref, silu_ref, dsilu_ref, up_ref, act_ref):
        x = h_ref[...]
        r = lax.rsqrt(jnp.mean(x * x, axis=-1, keepdims=True) + RMS_EPS)
        xn = ((x * r) * g_ref[...]).astype(BF16)
        xn_ref[...] = xn
        gate = _dot(xn, wg_ref[...], NT)
        up = _dot(xn, wu_ref[...], NT)
        sig = 1.0 / (1.0 + jnp.exp(-gate))
        silu = gate * sig
        up_ref[...] = up.astype(BF16)
        silu_ref[...] = silu.astype(BF16)
        dsilu_ref[...] = (sig + silu * (1.0 - sig)).astype(BF16)
        act_ref[...] = (silu * up).astype(BF16)

    row = pl.BlockSpec((tm, D), lambda i, j: (i, 0))
    blk = pl.BlockSpec((tm, tn), lambda i, j: (i, j))
    hid = jax.ShapeDtypeStruct((S, F), BF16)
    return _pcall(
        body, (h, g.reshape(1, D), win_t, win_t), name=name,
        out_shape=(jax.ShapeDtypeStruct((S, D), BF16), hid, hid, hid, hid),
        grid=(S // tm, nf),
        in_specs=[row, pl.BlockSpec((1, D), lambda i, j: (0, 0)),
                  pl.BlockSpec((tn, D), lambda i, j: (j, 0)),
                  pl.BlockSpec((tn, D), lambda i, j: (j + nf, 0))],
        out_specs=(row, blk, blk, blk, blk),
        sem=("arbitrary", "arbitrary"), carry=carry)


def _ffn_dact(dh, wo, silu, dsilu, up, name):
    S, D = dh.shape
    F = wo.shape[0]
    tm = _tile(S, 256, 16)

    def body(dh_ref, wo_hbm, s_ref, ds_ref, u_ref, dg_ref, du_ref, wo_v, sems):
        _load_resident([(wo_hbm, wo_v)], sems)
        d = _dot(dh_ref[...].astype(BF16), wo_v[...], NT) * FFN_RES_SCALE
        du_ref[...] = (d * s_ref[...].astype(F32)).astype(BF16)
        dg_ref[...] = (d * u_ref[...].astype(F32) * ds_ref[...].astype(F32)).astype(BF16)

    wide = pl.BlockSpec((tm, F), lambda i: (i, 0))
    hid = jax.ShapeDtypeStruct((S, F), BF16)
    return pl.pallas_call(
        body, name=name, out_shape=(hid, hid),
        grid=(S // tm,),
        in_specs=[pl.BlockSpec((tm, D), lambda i: (i, 0)), pl.BlockSpec(memory_space=pl.ANY), wide, wide, wide],
        out_specs=(wide, wide),
        scratch_shapes=[pltpu.VMEM(wo.shape, BF16), pltpu.SemaphoreType.DMA((1,))],
        compiler_params=_params("arbitrary"),
    )(dh, wo, silu, dsilu, up)


def _dw_rows(srcs, x, name, carry=None, tk=TN_CHUNK):
    n = len(srcs)
    S, F = srcs[0].shape
    D = x.shape[1]
    tr, tk = _tile(F, 1408), _tile(S, tk, 16)
    nf, nk = F // tr, S // tk

    def body(*refs):
        src_refs, (x_ref, o_ref, acc_ref) = refs[:n], refs[n:]
        r, k = pl.program_id(0), pl.program_id(1)
        for s in range(n):
            @pl.when(r // nf == s)
            def _():
                p = _dot(src_refs[s][...].astype(BF16), x_ref[...], TN)

                @pl.when(k == 0)
                def _():
                    acc_ref[...] = p

                @pl.when(k > 0)
                def _():
                    acc_ref[...] += p

        @pl.when(k == nk - 1)
        def _():
            o_ref[...] = acc_ref[...].astype(BF16)

    def src_spec(s):
        return pl.BlockSpec((tk, tr), lambda r, k: (jnp.where(r // nf == s, k, 0), jnp.clip(r - s * nf, 0, nf - 1)))

    return _pcall(
        body, (*srcs, x), name=name, out_shape=jax.ShapeDtypeStruct((n * F, D), BF16),
        grid=(n * nf, nk),
        in_specs=[src_spec(s) for s in range(n)] + [pl.BlockSpec((tk, D), lambda r, k: (k, 0))],
        out_specs=pl.BlockSpec((tr, D), lambda r, k: (r, 0)),
        scratch_shapes=[pltpu.VMEM((tr, D), F32)],
        sem=("arbitrary", "arbitrary"), carry=carry)


def _dx_norm_bwd(terms, h, g, res, name, carry=None, tm=256):
    S, D = h.shape
    tm = _tile(S, tm, 16)
    n = len(terms)

    def body(*refs):
        dy_refs, w_refs = refs[:n], refs[n:2 * n]
        h_ref, g_ref, r_ref, dh_ref, dg_ref = refs[2 * n:]
        d = _dot(dy_refs[0][...].astype(BF16), w_refs[0][...], terms[0][2])
        for t in range(1, n):
            d = d + _dot(dy_refs[t][...].astype(BF16), w_refs[t][...], terms[t][2])
        x = h_ref[...]
        r = lax.rsqrt(jnp.mean(x * x, axis=-1, keepdims=True) + RMS_EPS)
        xhat = x * r
        dxh = d * g_ref[...]
        c = jnp.mean(dxh * xhat, axis=-1, keepdims=True)
        dh_ref[...] = r * (dxh - xhat * c) + r_ref[...]
        part = _rows8(d * xhat)

        @pl.when(pl.program_id(0) == 0)
        def _():
            dg_ref[...] = part

        @pl.when(pl.program_id(0) > 0)
        def _():
            dg_ref[...] += part

    def w_spec(w, nblk, blk):
        return pl.BlockSpec((w.shape[0] // nblk, w.shape[1]), lambda i: (blk, 0))

    row = pl.BlockSpec((tm, D), lambda i: (i, 0))
    in_specs = [pl.BlockSpec((tm, t[0].shape[1]), lambda i: (i, 0)) for t in terms]
    in_specs += [w_spec(t[1], t[3], t[4]) for t in terms]
    in_specs += [row, pl.BlockSpec((1, D), lambda i: (0, 0)), row]
    return _pcall(
        body, (*[t[0] for t in terms], *[t[1] for t in terms], h, g.reshape(1, D), res), name=name,
        out_shape=(jax.ShapeDtypeStruct((S, D), F32), jax.ShapeDtypeStruct((8, D), F32)),
        grid=(S // tm,),
        in_specs=in_specs,
        out_specs=(row, pl.BlockSpec((8, D), lambda i: (0, 0))),
        sem=("arbitrary",), carry=carry)


def _load_resident(pairs, sems):
    @pl.when(pl.program_id(0) == 0)
    def _():
        copies = [pltpu.make_async_copy(src, dst, sems.at[n]) for n, (src, dst) in enumerate(pairs)]
        for cp in copies:
            cp.start()
        for cp in copies:
            cp.wait()


def _loss_tail(y_in, g, tgt):
    D = y_in.shape[-1]
    r = lax.rsqrt(jnp.mean(y_in * y_in, axis=-1, keepdims=True) + RMS_EPS)
    xhat = y_in * r
    err = xhat * g - tgt
    d = err * (1.0 / D)
    dxh = d * g
    c = jnp.mean(dxh * xhat, axis=-1, keepdims=True)
    return r * (dxh - xhat * c), _rows8(d * xhat), _rows8(err * err)


def _ffn_fwd_fused(h, g, win_t, wo, name, carry=None, loss=None):
    S, D = h.shape
    F = wo.shape[0]
    tm = _tile(S, 256, 16)
    n_head = 3 if loss is not None else 1

    def body(h_ref, g_ref, win_hbm, wo_hbm, *rest):
        lead, (xn_ref, silu_ref, dsilu_ref, up_ref, act_ref, win_v, wo_v, sems) = rest[:-8], rest[-8:]
        _load_resident([(win_hbm, win_v), (wo_hbm, wo_v)], sems)
        x = h_ref[...]
        r = lax.rsqrt(jnp.mean(x * x, axis=-1, keepdims=True) + RMS_EPS)
        xn = ((x * r) * g_ref[...]).astype(BF16)
        xn_ref[...] = xn
        gate = _dot(xn, win_v[:F, :], NT)
        up = _dot(xn, win_v[F:, :], NT)
        sig = 1.0 / (1.0 + jnp.exp(-gate))
        silu = gate * sig
        act = (silu * up).astype(BF16)
        up_ref[...] = up.astype(BF16)
        silu_ref[...] = silu.astype(BF16)
        dsilu_ref[...] = (sig + silu * (1.0 - sig)).astype(BF16)
        act_ref[...] = act
        out = x + FFN_RES_SCALE * _dot(act, wo_v[...], NN)
        if loss is None:
            lead[0][...] = out
        else:
            gf_ref, t_ref, dy_ref, dgf_ref, sq_ref = lead
            dy, dgf, sq = _loss_tail(out, gf_ref[...], t_ref[...])
            dy_ref[...] = dy

            @pl.when(pl.program_id(0) == 0)
            def _():
                dgf_ref[...] = dgf
                sq_ref[...] = sq

            @pl.when(pl.program_id(0) > 0)
            def _():
                dgf_ref[...] += dgf
                sq_ref[...] += sq

    row = pl.BlockSpec((tm, D), lambda i: (i, 0))
    vec = pl.BlockSpec((1, D), lambda i: (0, 0))
    acc = pl.BlockSpec((8, D), lambda i: (0, 0))
    wide = pl.BlockSpec((tm, F), lambda i: (i, 0))
    hbm = pl.BlockSpec(memory_space=pl.ANY)
    hid = jax.ShapeDtypeStruct((S, F), BF16)
    full = jax.ShapeDtypeStruct((S, D), F32)
    part = jax.ShapeDtypeStruct((8, D), F32)
    args, in_specs = (h, g.reshape(1, D), win_t, wo), [row, vec, hbm, hbm]
    lead_shapes, lead_specs = (full,), (row,)
    if loss is not None:
        args, in_specs = args + (loss[0].reshape(1, D), loss[1]), in_specs + [vec, row]
        lead_shapes, lead_specs = (full, part, part), (row, acc, acc)
    res, got = _pcall(
        body, args, name=name,
        out_shape=lead_shapes + (jax.ShapeDtypeStruct((S, D), BF16), hid, hid, hid, hid),
        grid=(S // tm,),
        in_specs=in_specs,
        out_specs=lead_specs + (row, wide, wide, wide, wide),
        scratch_shapes=[pltpu.VMEM(win_t.shape, BF16), pltpu.VMEM(wo.shape, BF16), pltpu.SemaphoreType.DMA((2,))],
        sem=("arbitrary",), carry=carry)
    first = res[0] if loss is None else tuple(res[:3])
    return first, tuple(res[n_head:]), got


def _ffn_bwd_fused(dh, h, g, win_t, wo, silu, dsilu, up, name, carry=None):
    S, D = h.shape
    F = wo.shape[0]
    tm = _tile(S, 256, 16)

    def body(dh_ref, h_ref, g_ref, s_ref, ds_ref, u_ref, win_hbm, wo_hbm,
             dhin_ref, dgain_ref, dgate_ref, dup_ref, win_v, wo_v, sems):
        _load_resident([(win_hbm, win_v), (wo_hbm, wo_v)], sems)
        dhv = dh_ref[...]
        d = _dot(dhv.astype(BF16), wo_v[...], NT) * FFN_RES_SCALE
        dup = (d * s_ref[...].astype(F32)).astype(BF16)
        dgate = (d * u_ref[...].astype(F32) * ds_ref[...].astype(F32)).astype(BF16)
        dup_ref[...] = dup
        dgate_ref[...] = dgate
        dxn = _dot(dgate, win_v[:F, :], NN) + _dot(dup, win_v[F:, :], NN)
        x = h_ref[...]
        r = lax.rsqrt(jnp.mean(x * x, axis=-1, keepdims=True) + RMS_EPS)
        xhat = x * r
        dxh = dxn * g_ref[...]
        c = jnp.mean(dxh * xhat, axis=-1, keepdims=True)
        dhin_ref[...] = r * (dxh - xhat * c) + dhv
        part = _rows8(dxn * xhat)

        @pl.when(pl.program_id(0) == 0)
        def _():
            dgain_ref[...] = part

        @pl.when(pl.program_id(0) > 0)
        def _():
            dgain_ref[...] += part

    row = pl.BlockSpec((tm, D), lambda i: (i, 0))
    wide = pl.BlockSpec((tm, F), lambda i: (i, 0))
    hbm = pl.BlockSpec(memory_space=pl.ANY)
    hid = jax.ShapeDtypeStruct((S, F), BF16)
    return _pcall(
        body, (dh, h, g.reshape(1, D), silu, dsilu, up, win_t, wo), name=name,
        out_shape=(jax.ShapeDtypeStruct((S, D), F32), jax.ShapeDtypeStruct((8, D), F32), hid, hid),
        grid=(S // tm,),
        in_specs=[row, row, pl.BlockSpec((1, D), lambda i: (0, 0)), wide, wide, wide, hbm, hbm],
        out_specs=(row, pl.BlockSpec((8, D), lambda i: (0, 0)), wide, wide),
        scratch_shapes=[pltpu.VMEM(win_t.shape, BF16), pltpu.VMEM(wo.shape, BF16), pltpu.SemaphoreType.DMA((2,))],
        sem=("arbitrary",), carry=carry)


def _rope_tables(S):
    half = HEAD_DIM // 2
    inv_freq = ROPE_THETA ** (-jnp.arange(half, dtype=F32) / half)
    ang = jnp.arange(S).astype(F32)[:, None] * inv_freq[None, :]
    cos, sin = jnp.cos(ang), jnp.sin(ang)
    cos_t = jnp.tile(cos, (1, LANES // half))
    sin_t = jnp.tile(jnp.concatenate([-sin, sin], axis=1), (1, LANES // HEAD_DIM))
    return cos_t, sin_t


def _swap_halves(x):
    lane = lax.broadcasted_iota(jnp.int32, x.shape, 1)
    first = (lane % HEAD_DIM) < (HEAD_DIM // 2)
    return jnp.where(first, pltpu.roll(x, LANES - HEAD_DIM // 2, 1), pltpu.roll(x, HEAD_DIM // 2, 1))


def _rotary(x, cos_t, sin_t, n_rot, inverse, name):
    S, C = x.shape
    ts = _tile(S, 512, 16)
    ng = C // LANES

    def body(x_ref, c_ref, s_ref, o_ref):
        cs, sn = c_ref[...], s_ref[...]
        for gidx in range(ng):
            sl = slice(gidx * LANES, (gidx + 1) * LANES)
            v = x_ref[:, sl].astype(F32)
            if gidx < n_rot:
                if inverse:
                    v = v * cs + _swap_halves(v * sn)
                else:
                    v = v * cs + _swap_halves(v) * sn
            o_ref[:, sl] = v.astype(BF16)

    row = pl.BlockSpec((ts, C), lambda i: (i, 0))
    tab = pl.BlockSpec((ts, LANES), lambda i: (i, 0))
    return pl.pallas_call(
        body, name=name, out_shape=jax.ShapeDtypeStruct((S, C), BF16),
        grid=(S // ts,), in_specs=[row, tab, tab], out_specs=row,
        compiler_params=_params("parallel"),
    )(x, cos_t, sin_t)


def _head_masks():
    lane = lax.broadcasted_iota(jnp.int32, (BLK, LANES), 1)
    return lane < HEAD_DIM


def _split_bf16(x):
    hi = x.astype(BF16)
    lo = (x - hi.astype(F32)).astype(BF16)
    return hi, lo


def _sb_scores(qh, ks, carry, diag, tri_excl, strict):
    n_heads = len(qh)
    zs = [_dot(ks[n], qh[n], NT) for n in range(n_heads)]
    a_l, b_l, split_l = [], [], []
    for z in zs:
        a = jnp.minimum(z, 0.0) - jnp.log(1.0 + jnp.exp(-jnp.abs(z)))
        b = a - z
        if diag:
            b = jnp.where(strict, b, 0.0)
        a_l.append(a)
        b_l.append(b)
        split_l.append(_split_bf16(b))
    sufs = [_dot(tri_excl, hi, NN) + _dot(tri_excl, lo, NN) for hi, lo in split_l]
    w_l = []
    for n in range(n_heads):
        w = jnp.exp(a_l[n] + sufs[n] + carry[n])
        if diag:
            w = jnp.where(strict, w, 0.0)
        w_l.append(w)
    return a_l, b_l, w_l


SB_FWD_PAIRS = 4
SB_FWD_QBLOCKS = 4
SB_BWD_PAIRS = 2
SB_BWD_QBLOCKS = 4


def _any_alive(carries):
    top = carries[0]
    for c in carries[1:]:
        top = jnp.maximum(top, c)
    return (jnp.max(top) > SB_LOG_FLOOR).astype(jnp.int32)


def _sb_masks():
    row = lax.broadcasted_iota(jnp.int32, (BLK, BLK), 0)
    col = lax.broadcasted_iota(jnp.int32, (BLK, BLK), 1)
    tri_excl = jnp.where(col > row, 1.0, 0.0).astype(BF16)
    tri_incl = jnp.where(col >= row, 1.0, 0.0).astype(BF16)
    return row < HEAD_DIM, row < col, tri_excl, tri_incl


def _sb_fwd(qkv, kv_t, name, carry=None):
    S, D3 = qkv.shape
    D = D3 // 3
    npair, nb = D // LANES, S // BLK
    P = min(SB_FWD_PAIRS, npair)
    ngroup = npair // P
    W = P * LANES

    QB = SB_FWD_QBLOCKS if nb % SB_FWD_QBLOCKS == 0 else 1
    nch = QB * 2 * P

    def body(q_ref, k_ref, vt_ref, o_ref):
        i_first = pl.program_id(1) * QB
        m0 = _head_masks()
        top, strict, tri_excl, _ = _sb_masks()
        zq = jnp.zeros((BLK, LANES), BF16)
        lanes = [slice(p * LANES, (p + 1) * LANES) for p in range(P)]
        qh = []
        for qb in range(QB):
            for sl in lanes:
                q2 = q_ref[qb * BLK:(qb + 1) * BLK, sl] * ATTN_SCALE
                qh += [jnp.where(m0, q2, zq), jnp.where(m0, zq, q2)]

        def block(qbs, js, carry, acc, diag):
            offs = [pl.multiple_of(j * BLK, BLK) for j in js]
            ks, vth, qs = [], [], []
            for n_qb, qb in enumerate(qbs):
                qs += qh[qb * 2 * P:(qb + 1) * 2 * P]
                for sl in lanes:
                    k2 = k_ref[pl.ds(offs[n_qb], BLK), sl]
                    vt = vt_ref[sl, pl.ds(offs[n_qb], BLK)]
                    ks += [k2, k2]
                    vth += [jnp.where(top, vt, zq), jnp.where(top, zq, vt)]
            _, b_l, w_l = _sb_scores(qs, ks, carry, diag, tri_excl, strict)
            wb = [w.astype(BF16) for w in w_l]
            new_acc = [acc[m] + _dot(vth[2 * m], wb[2 * m], NN) + _dot(vth[2 * m + 1], wb[2 * m + 1], NN)
                       for m in range(len(qbs) * P)]
            new_carry = [carry[n] + jnp.sum(b_l[n], axis=0, keepdims=True) for n in range(len(carry))]
            return new_carry, new_acc

        every = list(range(QB))
        c0 = jnp.zeros((1, BLK), F32)
        carry, acc = block(every, [i_first + qb for qb in every], [c0] * nch,
                           [jnp.zeros((LANES, BLK), F32)] * (QB * P), True)
        carry = [jnp.where(i_first > 0, c, NEG_BIG) for c in carry[:2 * P]] + carry[2 * P:]
        carry, acc = block(every, [jnp.maximum(i_first + qb - 1, 0) for qb in every], carry, acc, False)

        for qb in range(QB):
            i_qb = i_first + qb
            sub = slice(qb * 2 * P, (qb + 1) * 2 * P)

            def cond(st):
                return jnp.logical_and(i_qb - st[0] >= 0, st[1] > 0)

            def step(st, qb=qb, i_qb=i_qb):
                t, _, c_qb, a_qb = st
                c_qb, a_qb = block([qb], [i_qb - t], c_qb, a_qb, False)
                return t + 1, _any_alive(c_qb), c_qb, a_qb

            st = lax.while_loop(cond, step, (2, _any_alive(carry[sub]), carry[sub], acc[qb * P:(qb + 1) * P]))
            for p, sl in enumerate(lanes):
                o_ref[qb * BLK:(qb + 1) * BLK, sl] = jnp.transpose(st[3][p])

    return _pcall(
        body, (qkv, qkv, kv_t), name=name, out_shape=jax.ShapeDtypeStruct((S, D), F32),
        grid=(ngroup, nb // QB),
        in_specs=[pl.BlockSpec((QB * BLK, W), lambda g, i: (i, g)),
                  pl.BlockSpec((S, W), lambda g, i: (0, ngroup + g)),
                  pl.BlockSpec((W, S), lambda g, i: (ngroup + g, 0))],
        out_specs=pl.BlockSpec((QB * BLK, W), lambda g, i: (i, g)),
        sem=("arbitrary", "arbitrary"), carry=carry)


def _sb_bwd(qkv, kv_t, o, do, name, carry=None):
    S, D3 = qkv.shape
    D = D3 // 3
    npair, nb = D // LANES, S // BLK
    P = min(SB_BWD_PAIRS, npair)
    ngroup = npair // P
    W = P * LANES

    QB = SB_BWD_QBLOCKS if nb % SB_BWD_QBLOCKS == 0 else 1
    nch = QB * 2 * P

    def body(q_ref, o_ref, do_ref, qkv_hbm, kt_hbm, dq_ref, dk_ref, dv_ref, k_ref, v_ref, kt_ref, sems):
        grp = pl.program_id(0)
        i_first = pl.program_id(1) * QB
        m0 = _head_masks()
        top, strict, tri_excl, tri_incl = _sb_masks()
        zq = jnp.zeros((BLK, LANES), BF16)
        lanes = [slice(p * LANES, (p + 1) * LANES) for p in range(P)]

        @pl.when(pl.program_id(1) == 0)
        def _():
            copies = [pltpu.make_async_copy(qkv_hbm.at[:, pl.ds(pl.multiple_of((c * ngroup + grp) * W, LANES), W)],
                                            ref, sems.at[c - 1]) for c, ref in ((1, k_ref), (2, v_ref))]
            copies.append(pltpu.make_async_copy(kt_hbm.at[pl.ds(pl.multiple_of(grp * W, LANES), W), :],
                                                kt_ref, sems.at[2]))
            for cp in copies:
                cp.start()
            dk_ref[...] = jnp.zeros_like(dk_ref)
            dv_ref[...] = jnp.zeros_like(dv_ref)
            for cp in copies:
                cp.wait()

        qh, doh, delta = [], [], []
        for qb in range(QB):
            rs = slice(qb * BLK, (qb + 1) * BLK)
            for sl in lanes:
                q2, do2 = q_ref[rs, sl] * ATTN_SCALE, do_ref[rs, sl]
                qh += [jnp.where(m0, q2, zq), jnp.where(m0, zq, q2)]
                doh += [jnp.where(m0, do2, zq), jnp.where(m0, zq, do2)]
                prod_t = jnp.transpose(do2.astype(F32) * o_ref[rs, sl])
                delta += [jnp.sum(jnp.where(top, prod_t, 0.0), axis=0, keepdims=True),
                          jnp.sum(jnp.where(top, 0.0, prod_t), axis=0, keepdims=True)]

        def block(qbs, js, valid, cb, cg, dq, diag):
            offs = [pl.multiple_of(j * BLK, BLK) for j in js]
            n_ch = len(qbs) * 2 * P
            ks, vs, kth, qs, dos, dls = [], [], [], [], [], []
            for n_qb, qb in enumerate(qbs):
                chains = slice(qb * 2 * P, (qb + 1) * 2 * P)
                qs, dos, dls = qs + qh[chains], dos + doh[chains], dls + delta[chains]
                for sl in lanes:
                    k2, v2 = k_ref[pl.ds(offs[n_qb], BLK), sl], v_ref[pl.ds(offs[n_qb], BLK), sl]
                    ks += [k2, k2]
                    vs += [v2, v2]
                    kt = kt_ref[sl, pl.ds(offs[n_qb], BLK)] * ATTN_SCALE
                    kth += [jnp.where(top, kt, zq), jnp.where(top, zq, kt)]
            dws = [_dot(vs[n], dos[n], NT) for n in range(n_ch)]
            a_l, b_l, w_l = _sb_scores(qs, ks, cb, diag, tri_excl, strict)
            wb = [w.astype(BF16) for w in w_l]
            g_l = [dws[n] * wb[n].astype(F32) for n in range(n_ch)]
            gsplit = [_split_bf16(g) for g in g_l]
            gincs = [_dot(tri_incl, hi, NN) + _dot(tri_incl, lo, NN) for hi, lo in gsplit]
            dzs = []
            for n in range(n_ch):
                beta = jnp.exp(a_l[n])
                dz = g_l[n] - beta * (g_l[n] + ((dls[n] - cg[n]) - gincs[n]))
                if diag:
                    dz = jnp.where(strict, dz, 0.0)
                if valid[n // (2 * P)] is not None:
                    dz = jnp.where(valid[n // (2 * P)], dz, 0.0)
                dzs.append(dz.astype(BF16))
            ndq = []
            for n_qb in range(len(qbs)):
                for p, sl in enumerate(lanes):
                    n0 = n_qb * 2 * P + 2 * p
                    ndq.append(dq[n_qb * P + p] + _dot(kth[n0], dzs[n0], NN) + _dot(kth[n0 + 1], dzs[n0 + 1], NN))
                    dk_ref[pl.ds(offs[n_qb], BLK), sl] += _dot(dzs[n0], qs[n0], NN) + _dot(dzs[n0 + 1], qs[n0 + 1], NN)
                    dv_ref[pl.ds(offs[n_qb], BLK), sl] += _dot(wb[n0], dos[n0], NN) + _dot(wb[n0 + 1], dos[n0 + 1], NN)
            ncb = [cb[n] + jnp.sum(b_l[n], axis=0, keepdims=True) for n in range(n_ch)]
            ncg = [cg[n] + jnp.sum(g_l[n], axis=0, keepdims=True) for n in range(n_ch)]
            return ncb, ncg, ndq

        every = list(range(QB))
        c0 = jnp.zeros((1, BLK), F32)
        cb, cg, dq = block(every, [i_first + qb for qb in every], [None] * QB, [c0] * nch, [c0] * nch,
                           [jnp.zeros((LANES, BLK), F32)] * (QB * P), True)
        has_prev = i_first > 0
        cb = [jnp.where(has_prev, c, NEG_BIG) for c in cb[:2 * P]] + cb[2 * P:]
        cb, cg, dq = block(every, [jnp.maximum(i_first + qb - 1, 0) for qb in every], [has_prev] + [None] * (QB - 1),
                           cb, cg, dq, False)

        for qb in range(QB):
            i_qb = i_first + qb
            sub = slice(qb * 2 * P, (qb + 1) * 2 * P)

            def cond(st):
                return jnp.logical_and(i_qb - st[0] >= 0, st[1] > 0)

            def step(st, qb=qb, i_qb=i_qb):
                t, _, b_qb, g_qb, dq_qb = st
                b_qb, g_qb, dq_qb = block([qb], [i_qb - t], [None], b_qb, g_qb, dq_qb, False)
                return t + 1, _any_alive(b_qb), b_qb, g_qb, dq_qb

            st = lax.while_loop(cond, step, (2, _any_alive(cb[sub]), cb[sub], cg[sub], dq[qb * P:(qb + 1) * P]))
            for p, sl in enumerate(lanes):
                dq_ref[qb * BLK:(qb + 1) * BLK, sl] = jnp.transpose(st[4][p]).astype(BF16)

    blk = pl.BlockSpec((QB * BLK, W), lambda g, i: (i, g))
    col_all = pl.BlockSpec((S, W), lambda g, i: (0, g))
    hbm = pl.BlockSpec(memory_space=pl.ANY)
    return _pcall(
        body, (qkv, o, do, qkv, kv_t), name=name,
        out_shape=(jax.ShapeDtypeStruct((S, D), BF16), jax.ShapeDtypeStruct((S, D), F32),
                   jax.ShapeDtypeStruct((S, D), F32)),
        grid=(ngroup, nb // QB),
        in_specs=[blk, blk, blk, hbm, hbm],
        out_specs=(blk, col_all, col_all),
        scratch_shapes=[pltpu.VMEM((S, W), BF16), pltpu.VMEM((S, W), BF16), pltpu.VMEM((W, S), BF16),
                        pltpu.SemaphoreType.DMA((3,))],
        sem=("arbitrary", "arbitrary"), carry=carry)


SWA_Q_GROUPS = 4


def _roll_heads(x):
    return pltpu.roll(x.astype(F32), HEAD_DIM, 1).astype(BF16)


def _roll_rows(x):
    return pltpu.roll(x.astype(F32), HEAD_DIM, 0).astype(BF16)


def _swa_valid(i):
    k = lax.broadcasted_iota(jnp.int32, (2 * BLK, BLK), 0)
    q = lax.broadcasted_iota(jnp.int32, (2 * BLK, BLK), 1)
    diff = q + BLK - k
    return (diff >= 0) & (diff < BLK) & ((i > 0) | (k >= BLK))


def _swa_probs(z, valid, sink):
    z = jnp.where(valid, z * ATTN_SCALE, NEG_BIG)
    mx = jnp.maximum(jnp.max(z, axis=0, keepdims=True), sink)
    p = jnp.exp(z - mx)
    ps = jnp.exp(sink - mx)
    inv = 1.0 / (jnp.sum(p, axis=0, keepdims=True) + ps)
    return p * inv, ps * inv


def _swa_operands(q_ref, kc_ref, kp_ref, vc_ref, vp_ref, tc_ref, tp_ref, s_ref, nkvp):
    m0 = _head_masks()
    top = lax.broadcasted_iota(jnp.int32, (LANES, 2 * BLK), 0) < HEAD_DIM
    heads = []
    for m in range(nkvp):
        pair = slice(m * LANES, (m + 1) * LANES)
        kk = jnp.concatenate([kp_ref[:, pair], kc_ref[:, pair]], axis=0)
        vv = jnp.concatenate([vp_ref[:, pair], vc_ref[:, pair]], axis=0)
        tt = jnp.concatenate([tp_ref[pair, :], tc_ref[pair, :]], axis=1)
        ksw, vsw, tsw = _roll_heads(kk), _roll_heads(vv), _roll_rows(tt)
        zt = jnp.zeros_like(tt)
        for c in range(SWA_Q_GROUPS):
            q_lanes = slice((m * SWA_Q_GROUPS + c) * LANES, (m * SWA_Q_GROUPS + c + 1) * LANES)
            qc = q_ref[:, q_lanes]
            zq = jnp.zeros_like(qc)
            for u in range(2):
                same = u == c // 2
                sel = (lambda x, z, mk: jnp.where(mk, x, z)) if u == 0 else (lambda x, z, mk: jnp.where(mk, z, x))
                heads.append(dict(
                    m=m, q_lanes=q_lanes, same=same, sel=sel, qm=sel(qc, zq, m0),
                    k=kk if same else ksw, v=vv if same else vsw,
                    tm=sel(tt if same else tsw, zt, top),
                    sink=s_ref[0, (m * SWA_Q_GROUPS + c) * 2 + u]))
    return heads, m0


def _swa_specs(D, half, t_block):
    prev = lambda i: jnp.maximum(i - 1, 0)
    return [pl.BlockSpec((BLK, D), lambda i: (i, 0)),
            pl.BlockSpec((BLK, half), lambda i: (i, 0)),
            pl.BlockSpec((BLK, half), lambda i: (prev(i), 0)),
            pl.BlockSpec((BLK, half), lambda i: (i, 1)),
            pl.BlockSpec((BLK, half), lambda i: (prev(i), 1)),
            pl.BlockSpec((half, BLK), lambda i: (t_block, i)),
            pl.BlockSpec((half, BLK), lambda i: (t_block, prev(i))),
            pl.BlockSpec(memory_space=pltpu.SMEM)]


def _swa_fwd(q, kv, kv_t, sinks, name):
    S, D = q.shape
    half = kv.shape[1] // 2
    nkvp = half // LANES

    def body(q_ref, kc_ref, kp_ref, vc_ref, vp_ref, tc_ref, tp_ref, s_ref, o_ref):
        valid = _swa_valid(pl.program_id(0))
        heads, _ = _swa_operands(q_ref, kc_ref, kp_ref, vc_ref, vp_ref, tc_ref, tp_ref, s_ref, nkvp)
        zs = [_dot(hd["k"], hd["qm"], NT) for hd in heads]
        ps = [_swa_probs(z, valid, hd["sink"])[0].astype(BF16) for z, hd in zip(zs, heads)]
        for n in range(0, len(heads), 2):
            o_t = _dot(heads[n]["tm"], ps[n], NN) + _dot(heads[n + 1]["tm"], ps[n + 1], NN)
            o_ref[:, heads[n]["q_lanes"]] = jnp.transpose(o_t)

    return pl.pallas_call(
        body, name=name, out_shape=jax.ShapeDtypeStruct((S, D), F32),
        grid=(S // BLK,),
        in_specs=_swa_specs(D, half, 1),
        out_specs=pl.BlockSpec((BLK, D), lambda i: (i, 0)),
        compiler_params=_params("arbitrary"),
    )(q, kv, kv, kv, kv, kv_t, kv_t, sinks)


def _swa_bwd(q, kv, kv_t, sinks, o, do, cos_t, sin_t, name, carry=None):
    S, D = q.shape
    half = kv.shape[1] // 2
    nkvp = half // LANES
    nh = nkvp * 2 * SWA_Q_GROUPS

    def body(q_ref, kc_ref, kp_ref, vc_ref, vp_ref, tc_ref, tp_ref, s_ref, o_ref, do_ref, c_ref, sn_ref,
             dq_ref, dk_ref, dv_ref, ds_ref):
        i = pl.program_id(0)
        valid = _swa_valid(i)
        heads, m0 = _swa_operands(q_ref, kc_ref, kp_ref, vc_ref, vp_ref, tc_ref, tp_ref, s_ref, nkvp)
        top_q = lax.broadcasted_iota(jnp.int32, (LANES, BLK), 0) < HEAD_DIM

        @pl.when(i == 0)
        def _():
            dk_ref[...] = jnp.zeros_like(dk_ref)
            dv_ref[...] = jnp.zeros_like(dv_ref)
            ds_ref[...] = jnp.zeros_like(ds_ref)

        doms, deltas = [], []
        for n in range(0, nh, 2):
            doc = do_ref[:, heads[n]["q_lanes"]]
            prod_t = jnp.transpose(doc.astype(F32) * o_ref[:, heads[n]["q_lanes"]])
            for hd in heads[n:n + 2]:
                doms.append(hd["sel"](doc, jnp.zeros_like(doc), m0))
                deltas.append(jnp.sum(hd["sel"](prod_t, 0.0, top_q), axis=0, keepdims=True))
        zs = [_dot(hd["k"], hd["qm"], NT) for hd in heads]
        dps = [_dot(hd["v"], dom, NT) for dom, hd in zip(doms, heads)]
        pbs, dscs = [], []
        for n, hd in enumerate(heads):
            p, psink = _swa_probs(zs[n], valid, hd["sink"])
            pbs.append(p.astype(BF16))
            dscs.append((p * (dps[n] - deltas[n]) * ATTN_SCALE).astype(BF16))
            ds_ref[n:n + 1, :] += -(psink * deltas[n])
        for n in range(0, nh, 2):
            dq_rot = jnp.transpose(_dot(heads[n]["tm"], dscs[n], NN) + _dot(heads[n + 1]["tm"], dscs[n + 1], NN))
            dq_ref[:, heads[n]["q_lanes"]] = (
                dq_rot * c_ref[...] + _swap_halves(dq_rot * sn_ref[...])).astype(BF16)
        acc = {}
        for n, hd in enumerate(heads):
            dk_n = _dot(dscs[n], hd["qm"], NN)
            dv_n = _dot(pbs[n], doms[n], NN)
            for key, val in ((("k", hd["m"], hd["same"]), dk_n), (("v", hd["m"], hd["same"]), dv_n)):
                acc[key] = val if key not in acc else acc[key] + val
        poff = pl.multiple_of(jnp.maximum(i - 1, 0) * BLK, BLK)
        coff = pl.multiple_of(i * BLK, BLK)
        for m in range(nkvp):
            pair = slice(m * LANES, (m + 1) * LANES)
            dkk = acc["k", m, True] + pltpu.roll(acc["k", m, False], HEAD_DIM, 1)
            dvv = acc["v", m, True] + pltpu.roll(acc["v", m, False], HEAD_DIM, 1)
            dk_ref[pl.ds(poff, BLK), pair] += dkk[:BLK]
            dv_ref[pl.ds(poff, BLK), pair] += dvv[:BLK]
            dk_ref[pl.ds(coff, BLK), pair] += dkk[BLK:]
            dv_ref[pl.ds(coff, BLK), pair] += dvv[BLK:]

    qblk = pl.BlockSpec((BLK, D), lambda i: (i, 0))
    whole = pl.BlockSpec((S, half), lambda i: (0, 0))
    tab = pl.BlockSpec((BLK, LANES), lambda i: (i, 0))
    return _pcall(
        body, (q, kv, kv, kv, kv, kv_t, kv_t, sinks, o, do, cos_t, sin_t), name=name,
        out_shape=(jax.ShapeDtypeStruct((S, D), BF16),
                   jax.ShapeDtypeStruct((S, half), F32),
                   jax.ShapeDtypeStruct((S, half), F32),
                   jax.ShapeDtypeStruct((nh, LANES), F32)),
        grid=(S // BLK,),
        in_specs=_swa_specs(D, half, 0) + [qblk, qblk, tab, tab],
        out_specs=(qblk, whole, whole, pl.BlockSpec((nh, LANES), lambda i: (0, 0))),
        sem=("arbitrary",), carry=carry)


def _dev_index(p):
    return 4 * p[0] + 2 * p[1] + p[2]


def _gather_plan(x_refs, out_refs, send_sems, recv_sems, local_sems):
    n = len(x_refs)
    x_, y_, c_ = lax.axis_index("x"), lax.axis_index("y"), lax.axis_index("c")
    me, sibling = (x_, y_, c_), (x_, y_, 1 - c_)
    chips = [(1 - x_, y_), (x_, 1 - y_), (1 - x_, 1 - y_)]

    def copy(t, k, block, to, src=None):
        dst = out_refs[t].at[_dev_index(block)]
        return pltpu.make_async_remote_copy(
            src_ref=dst if src is None else src, dst_ref=dst,
            send_sem=send_sems.at[7 * t + k], recv_sem=recv_sems.at[7 * t + k],
            device_id=to, device_id_type=MESH)

    mine = [pltpu.make_async_copy(x_refs[t], out_refs[t].at[_dev_index(me)], local_sems.at[t]) for t in range(n)]
    first = []
    for t in range(n):
        first.append(copy(t, 0, me, sibling, src=x_refs[t]))
        first += [copy(t, 1 + j, me, (*chip, c_), src=x_refs[t]) for j, chip in enumerate(chips)]
    arrived = lambda t, j: copy(t, 1 + j, (*chips[j], c_), me)
    forward = lambda t, j: copy(t, 4 + j, (*chips[j], c_), sibling)
    from_sibling = lambda t: copy(t, 0, sibling, me)
    forwarded = lambda t, j: copy(t, 4 + j, (*chips[j], 1 - c_), me)
    return n, mine, first, arrived, forward, from_sibling, forwarded


def _gather_start(x_refs, out_refs, send_sems, recv_sems, local_sems):
    _, mine, first, *_ = _gather_plan(x_refs, out_refs, send_sems, recv_sems, local_sems)
    for cp in mine + first:
        cp.start()


def _gather_forward(x_refs, out_refs, send_sems, recv_sems, local_sems):
    n, _, _, arrived, forward, _, _ = _gather_plan(x_refs, out_refs, send_sems, recv_sems, local_sems)
    for j in range(3):
        for t in range(n):
            arrived(t, j).wait_recv()
            forward(t, j).start()


def _gather_finish(x_refs, out_refs, send_sems, recv_sems, local_sems):
    n, mine, first, _, forward, from_sibling, forwarded = _gather_plan(
        x_refs, out_refs, send_sems, recv_sems, local_sems)
    for t in range(n):
        from_sibling(t).wait_recv()
    for j in range(3):
        for t in range(n):
            forwarded(t, j).wait_recv()
    for cp in first + [forward(t, j) for j in range(3) for t in range(n)]:
        cp.wait_send()
    for cp in mine:
        cp.wait()


def _scatter_plan(b_refs, out_refs, send_sems, recv_sems, local_sems):
    n = len(b_refs)
    x_, y_, c_ = lax.axis_index("x"), lax.axis_index("y"), lax.axis_index("c")
    my_idx = _dev_index((x_, y_, c_))
    mine = [pltpu.make_async_copy(b_refs[t].at[my_idx], out_refs[t].at[my_idx], local_sems.at[t]) for t in range(n)]
    copies = []
    for t in range(n):
        for k in range(1, N_DEV):
            peer = (x_ ^ ((k >> 2) & 1), y_ ^ ((k >> 1) & 1), c_ ^ (k & 1))
            copies.append(pltpu.make_async_remote_copy(
                src_ref=b_refs[t].at[_dev_index(peer)], dst_ref=out_refs[t].at[my_idx],
                send_sem=send_sems.at[7 * t + k - 1], recv_sem=recv_sems.at[7 * t + k - 1],
                device_id=peer, device_id_type=MESH))
    return mine, copies


def _scatter_start(b_refs, out_refs, send_sems, recv_sems, local_sems):
    mine, copies = _scatter_plan(b_refs, out_refs, send_sems, recv_sems, local_sems)
    for cp in mine + copies:
        cp.start()


def _scatter_finish(b_refs, out_refs, send_sems, recv_sems, local_sems):
    mine, copies = _scatter_plan(b_refs, out_refs, send_sems, recv_sems, local_sems)
    for cp in copies:
        cp.wait_recv()
    for cp in copies:
        cp.wait_send()
    for cp in mine:
        cp.wait()


def _exchange_operands(kind, tensors):
    if kind == "gather":
        args = list(tensors)
        shapes = [jax.ShapeDtypeStruct((N_DEV,) + t.shape, t.dtype) for t in tensors]
        return args, shapes, (_gather_start, _gather_forward, _gather_finish)
    args = [t.reshape(N_DEV, t.shape[0] // N_DEV, t.shape[1]) for t in tensors]
    shapes = [jax.ShapeDtypeStruct(a.shape, a.dtype) for a in args]
    return args, shapes, (_scatter_start, None, _scatter_finish)


def _exchange_results(kind, tensors, res):
    if kind == "gather":
        return [r.reshape(N_DEV * t.shape[0], t.shape[1]) for r, t in zip(res, tensors)]
    return list(res)


def _exchange_sems(n):
    return [pltpu.SemaphoreType.DMA((7 * n,)), pltpu.SemaphoreType.DMA((7 * n,)), pltpu.SemaphoreType.DMA((n,))]


def _exchange(kind, tensors, name):
    n = len(tensors)
    args, shapes, phases = _exchange_operands(kind, tensors)

    def body(*refs):
        for phase in phases:
            if phase is not None:
                phase(refs[:n], refs[n:2 * n], *refs[2 * n:])

    hbm = pl.BlockSpec(memory_space=pl.ANY)
    res = pl.pallas_call(body, name=name, out_shape=shapes, in_specs=[hbm] * n, out_specs=[hbm] * n,
                         scratch_shapes=_exchange_sems(n))(*args)
    return _exchange_results(kind, tensors, res)


def _pcall(body, args, *, name, out_shape, grid, in_specs, out_specs, sem, scratch_shapes=(), carry=None):
    if carry is None:
        out = pl.pallas_call(body, name=name, out_shape=out_shape, grid=grid, in_specs=list(in_specs),
                             out_specs=out_specs, scratch_shapes=list(scratch_shapes),
                             compiler_params=_params(*sem))(*args)
        return out, None
    kind, tensors = carry
    multi = isinstance(out_shape, (tuple, list))
    shapes = list(out_shape) if multi else [out_shape]
    ospecs = list(out_specs) if multi else [out_specs]
    n_in, n_out, n_scr, n_c = len(in_specs), len(shapes), len(scratch_shapes), len(tensors)
    c_args, c_shapes, (start, forward, finish) = _exchange_operands(kind, tensors)
    n_steps = 1
    for g in grid:
        n_steps *= g
    late = (3 * n_steps) // 4

    def wrapped(*refs):
        ins, rest = refs[:n_in], refs[n_in:]
        c_in, rest = rest[:n_c], rest[n_c:]
        outs, rest = rest[:n_out], rest[n_out:]
        c_out, rest = rest[:n_c], rest[n_c:]
        scr, sems = rest[:n_scr], rest[n_scr:]
        step = pl.program_id(0)
        for a in range(1, len(grid)):
            step = step * grid[a] + pl.program_id(a)

        @pl.when(step == 0)
        def _():
            start(c_in, c_out, *sems)

        body(*ins, *outs, *scr)

        if forward is not None:
            @pl.when(step == late)
            def _():
                forward(c_in, c_out, *sems)

        @pl.when(step == n_steps - 1)
        def _():
            finish(c_in, c_out, *sems)

    hbm = pl.BlockSpec(memory_space=pl.ANY)
    res = pl.pallas_call(
        wrapped, name=name, out_shape=shapes + c_shapes, grid=grid,
        in_specs=list(in_specs) + [hbm] * n_c, out_specs=ospecs + [hbm] * n_c,
        scratch_shapes=list(scratch_shapes) + _exchange_sems(n_c),
        compiler_params=_params(*sem))(*args, *c_args)
    outs = tuple(res[:n_out]) if multi else res[0]
    return outs, _exchange_results(kind, tensors, res[n_out:])


def _sum8(parts, name):
    _, R, C = parts.shape
    tr = _tile(R, 256, 16)

    def body(p_ref, g_ref):
        g = p_ref[0].astype(F32)
        for s in range(1, N_DEV):
            g = g + p_ref[s].astype(F32)
        g_ref[...] = g

    return pl.pallas_call(
        body, name=name, out_shape=jax.ShapeDtypeStruct((R, C), F32),
        grid=(R // tr,),
        in_specs=[pl.BlockSpec((N_DEV, tr, C), lambda i: (0, i, 0))],
        out_specs=pl.BlockSpec((tr, C), lambda i: (i, 0)),
        compiler_params=_params("parallel"),
    )(parts)


def _adamw(g, w, m, v, name):
    R, C = g.shape
    tr = _tile(R, 256, 8)
    c1 = 1.0 - ADAM_B1 ** ADAM_STEP
    c2 = 1.0 - ADAM_B2 ** ADAM_STEP

    def body(g_ref, w_ref, m_ref, v_ref, d_ref, nm_ref, nv_ref):
        gg = g_ref[...]
        nm = ADAM_B1 * m_ref[...] + (1.0 - ADAM_B1) * gg
        nv = ADAM_B2 * v_ref[...] + (1.0 - ADAM_B2) * (gg * gg)
        m_hat = nm / c1
        v_hat = nv / c2
        nm_ref[...] = nm
        nv_ref[...] = nv
        d_ref[...] = -ADAM_LR * (m_hat / (jnp.sqrt(v_hat) + ADAM_EPS) + ADAM_WD * w_ref[...])

    row = pl.BlockSpec((tr, C), lambda i: (i, 0))
    shp = jax.ShapeDtypeStruct((R, C), F32)
    return pl.pallas_call(
        body, name=name, out_shape=(shp, shp, shp),
        grid=(R // tr,), in_specs=[row, row, row, row], out_specs=(row, row, row),
        compiler_params=_params("parallel"),
    )(g, w, m, v)


def _ffn_down(act, wo, h, tag):
    return _mm(act, wo, NN, F32, f"{tag}_down", scale=FFN_RES_SCALE, res=h, tm=512, tn=1024, tk=2816)


def _ffn_fwd(h, g, win_t, wo, tag, carry=None, loss=None):
    return _ffn_fwd_fused(h, g, win_t, wo, f"{tag}_fwd", carry=carry, loss=loss)


def _ffn_bwd(dh, h, g, win_t, wo, saved, tag, scatter=False, carry=None, carry_dwin=None):
    xn, silu, dsilu, up, act = saved
    dwo = _mm(act, dh, TN, BF16, f"{tag}_dwo", scale=FFN_RES_SCALE, tm=1408, tn=1024, tk=TN_CHUNK)
    if not scatter:
        (dh_in, dg, dgate, dup), got = _ffn_bwd_fused(dh, h, g, win_t, wo, silu, dsilu, up, f"{tag}_bwd", carry=carry)
        dwin_t, got_dwin = _dw_rows([dgate, dup], xn, f"{tag}_dwin", carry=carry_dwin)
        return dh_in, dg, dwin_t, dwo, got, got_dwin
    dgate, dup = _ffn_dact(dh, wo, silu, dsilu, up, f"{tag}_dact")
    dwin_t, got_wo = _dw_rows([dgate, dup], xn, f"{tag}_dwin", carry=("scatter", [dwo]))
    (dh_in, dg), got_win = _dx_norm_bwd([(dgate, win_t, NN, 2, 0), (dup, win_t, NN, 2, 1)], h, g, dh, f"{tag}_dx",
                                        carry=("scatter", [dwin_t]))
    return dh_in, dg, got_win[0], got_wo[0]


def _proj(a, w, dims, out_dtype, name, res=None):
    return _mm(a, w, dims, out_dtype, name, res=res, tm=1024, tn=1024, tk=1024)


def _proj_dw(x, dy, name):
    return _mm(x, dy, TN, BF16, name, tm=1024, tn=1024, tk=TN_CHUNK)


def kernel(x, ffn1_norm, ffn1_w_in, ffn1_w_out, mix_norm, ffn2_norm, ffn2_w_in, ffn2_w_out, sb_w_qkv, sb_w_o, kv_norm, kv_w, swa_w_q, swa_sinks, swa_w_o, final_norm, loss_target, m_ffn1_norm, m_ffn1_w_in, m_ffn1_w_out, m_mix_norm, m_ffn2_norm, m_ffn2_w_in, m_ffn2_w_out, m_sb_w_qkv, m_sb_w_o, m_kv_norm, m_kv_w, m_swa_w_q, m_swa_sinks, m_swa_w_o, m_final_norm, v_ffn1_norm, v_ffn1_w_in, v_ffn1_w_out, v_mix_norm, v_ffn2_norm, v_ffn2_w_in, v_ffn2_w_out, v_sb_w_qkv, v_sb_w_o, v_kv_norm, v_kv_w, v_swa_w_q, v_swa_sinks, v_swa_w_o, v_final_norm):
    S, D = x.shape[1], x.shape[2]
    L = ffn1_w_in.shape[0]
    KV = kv_w.shape[1]
    assert L == 2 and swa_sinks.shape == (1, 2 * SWA_Q_GROUPS * KV // (2 * LANES))

    def bf(w):
        return w.astype(BF16)

    def bft(w):
        return jnp.transpose(w).astype(BF16)

    cos_t, sin_t = _rope_tables(S)
    h0 = x.reshape(S, D)
    tgt = loss_target.reshape(S, D)

    win1a_t, = _exchange("gather", [bft(ffn1_w_in[0])], "gather_first_weight")
    sv_a1, (wo1a, wqkv_t, w_sbo) = _ffn_up(
        h0, ffn1_norm[0], win1a_t, "ffn1a_up",
        carry=("gather", [bf(ffn1_w_out[0]), bft(sb_w_qkv[0]), bf(sb_w_o[0])]))
    h1 = _ffn_down(sv_a1[-1], wo1a, h0, "ffn1a")
    hn_a, qkv, kv_t = _norm_proj(h1, mix_norm[0], wqkv_t, NT, "sb_qkv", tail_t=2 * D)
    o_sb, (win2a_t, wo2a, w_kv) = _sb_fwd(qkv, kv_t, "sb_attn", carry=("gather", [
        bft(ffn2_w_in[0]), bf(ffn2_w_out[0]), bf(kv_w)]))
    h2 = _proj(o_sb, w_sbo, NN, F32, "sb_out", res=h1)
    h3, sv_a2, (win1b_t, wo1b, w_q, w_swo) = _ffn_fwd(h2, ffn2_norm[0], win2a_t, wo2a, "ffn2a", carry=("gather", [
        bft(ffn1_w_in[1]), bf(ffn1_w_out[1]), bf(swa_w_q[0]), bf(swa_w_o[0])]))
    kvn, kv_rot, kv_rot_t = _norm_proj(h3, kv_norm, w_kv, NN, "kv_proj", rope=(cos_t, sin_t, KV // (2 * LANES)),
                                       tail_t=KV)
    h4, sv_b1, (win2b_t, wo2b) = _ffn_fwd(h3, ffn1_norm[1], win1b_t, wo1b, "ffn1b", carry=("gather", [
        bft(ffn2_w_in[1]), bf(ffn2_w_out[1])]))
    hn_b, q_rot = _norm_proj(h4, mix_norm[1], w_q, NN, "swa_q", rope=(cos_t, sin_t, D // LANES))
    o_sw = _swa_fwd(q_rot, kv_rot, kv_rot_t, swa_sinks, "swa_attn")
    h5 = _proj(o_sw, w_swo, NN, F32, "swa_out", res=h4)
    (dh6, dg_final, sq_err), sv_b2, _ = _ffn_fwd(h5, ffn2_norm[1], win2b_t, wo2b, "ffn2b", loss=(final_norm, tgt))
    loss = lax.psum(0.5 * jnp.sum(sq_err) / D, ("x", "y", "c"))

    dh5, dg_f2b, dwin2b_t, dwo2b, _, _ = _ffn_bwd(dh6, h5, ffn2_norm[1], win2b_t, wo2b, sv_b2, "ffn2b")
    do_sw = _proj(dh5, w_swo, NT, BF16, "swa_out_dx")
    dw_swo = _proj_dw(o_sw, dh5, "swa_out_dw")
    (dq, dk_sw, dv_sw, dsink), (p_win2b, p_swo) = _swa_bwd(
        q_rot, kv_rot, kv_rot_t, swa_sinks, o_sw, do_sw, cos_t, sin_t, "swa_attn_bwd",
        carry=("scatter", [dwin2b_t, dw_swo]))
    dw_q = _proj_dw(hn_b, dq, "swa_q_dw")
    (dh4, dg_mix_b), _ = _dx_norm_bwd([(dq, w_q, NT, 1, 0)], h4, mix_norm[1], dh5, "swa_q_dx", tm=512)
    dh3, dg_f1b, dwin1b_t, dwo1b, (p_q, p_wo2b), _ = _ffn_bwd(dh4, h3, ffn1_norm[1], win1b_t, wo1b, sv_b1, "ffn1b",
                                                              carry=("scatter", [dw_q, dwo2b]))
    dkv = _rotary(jnp.concatenate([dk_sw, dv_sw], axis=1), cos_t, sin_t, KV // (2 * LANES), True, "kv_rope_bwd")
    dw_kv = _proj_dw(kvn, dkv, "kv_proj_dw")
    (dh3, dg_kv), _ = _dx_norm_bwd([(dkv, w_kv, NT, 1, 0)], h3, kv_norm, dh3, "kv_proj_dx", tm=512)
    dh2, dg_f2a, dwin2a_t, dwo2a, (p_win1b, p_kv), (p_wo1b,) = _ffn_bwd(
        dh3, h2, ffn2_norm[0], win2a_t, wo2a, sv_a2, "ffn2a",
        carry=("scatter", [dwin1b_t, dw_kv]), carry_dwin=("scatter", [dwo1b]))
    do_sb = _proj(dh2, w_sbo, NT, BF16, "sb_out_dx")
    dw_sbo = _proj_dw(o_sb, dh2, "sb_out_dw")
    (dq_sb, dk_sb, dv_sb), (p_win2a, p_wo2a, p_sbo) = _sb_bwd(
        qkv, kv_t, o_sb, do_sb, "sb_attn_bwd", carry=("scatter", [dwin2a_t, dwo2a, dw_sbo]))
    dqkv = [dq_sb, dk_sb, dv_sb]
    dwqkv_t, _ = _dw_rows(dqkv, hn_a, "sb_qkv_dw", tk=TN_CHUNK // 2)
    (dh1, dg_mix_a), (p_qkv,) = _dx_norm_bwd([(dy, wqkv_t, NN, 3, n) for n, dy in enumerate(dqkv)], h1, mix_norm[0],
                                             dh2, "sb_qkv_dx", carry=("scatter", [dwqkv_t]), tm=512)
    dx, dg_f1a, p_win1a, p_wo1a = _ffn_bwd(dh1, h0, ffn1_norm[0], win1a_t, wo1a, sv_a1, "ffn1a", scatter=True)

    def natural(parts, tag):
        return _sum8(parts, f"sum_{tag}")

    def from_t(parts, tag):
        return jnp.transpose(_sum8(parts, f"sum_{tag}"))

    grads = {
        "ffn1_w_in": jnp.stack([from_t(p_win1a, "win1a"), from_t(p_win1b, "win1b")]),
        "ffn1_w_out": jnp.stack([natural(p_wo1a, "wo1a"), natural(p_wo1b, "wo1b")]),
        "ffn2_w_in": jnp.stack([from_t(p_win2a, "win2a"), from_t(p_win2b, "win2b")]),
        "ffn2_w_out": jnp.stack([natural(p_wo2a, "wo2a"), natural(p_wo2b, "wo2b")]),
        "sb_w_qkv": from_t(p_qkv, "qkv")[None],
        "sb_w_o": natural(p_sbo, "sbo")[None],
        "kv_w": natural(p_kv, "kv"),
        "swa_w_q": natural(p_q, "swq")[None],
        "swa_w_o": natural(p_swo, "swo")[None],
    }

    small_w = [ffn1_norm, mix_norm, ffn2_norm, kv_norm, final_norm, swa_sinks]
    small_m = [m_ffn1_norm, m_mix_norm, m_ffn2_norm, m_kv_norm, m_final_norm, m_swa_sinks]
    small_v = [v_ffn1_norm, v_mix_norm, v_ffn2_norm, v_kv_norm, v_final_norm, v_swa_sinks]
    SMALL_ROWS = 16

    def pack_small(ts):
        rows_ = [t.reshape(-1, D) for t in ts[:-1]]
        sink_row = jnp.pad(ts[-1].reshape(1, -1), ((0, 0), (0, D - ts[-1].size)))
        flat = jnp.concatenate(rows_ + [sink_row], axis=0)
        return jnp.pad(flat, ((0, SMALL_ROWS - flat.shape[0]), (0, 0)))

    def unpack_small(flat):
        out, r = [], 0
        for t in small_w[:-1]:
            n = t.size // D
            out.append(flat[r:r + n].reshape(t.shape))
            r += n
        out.append(flat[r, :swa_sinks.size].reshape(swa_sinks.shape))
        return out

    def gain(parts8):
        return jnp.sum(parts8, axis=0, keepdims=True)

    g_small_local = pack_small([
        jnp.concatenate([gain(dg_f1a), gain(dg_f1b)], axis=0),
        jnp.concatenate([gain(dg_mix_a), gain(dg_mix_b)], axis=0),
        jnp.concatenate([gain(dg_f2a), gain(dg_f2b)], axis=0),
        gain(dg_kv), gain(dg_final), jnp.sum(dsink, axis=-1).reshape(1, -1)])
    small_parts = _exchange("gather", [g_small_local], "gather_small_grads")[0]
    g_small = _sum8(small_parts.reshape(N_DEV, SMALL_ROWS, D), "sum_small")
    d_small, nm_small, nv_small = _adamw(g_small, pack_small(small_w), pack_small(small_m), pack_small(small_v), "adamw_small")
    small_names = ["ffn1_norm", "mix_norm", "ffn2_norm", "kv_norm", "final_norm", "swa_sinks"]
    result = {"grad": dict(zip(small_names, unpack_small(g_small))),
              "delta": dict(zip(small_names, unpack_small(d_small))),
              "new_m": dict(zip(small_names, unpack_small(nm_small))),
              "new_v": dict(zip(small_names, unpack_small(nv_small)))}

    big = {"ffn1_w_in": (ffn1_w_in, m_ffn1_w_in, v_ffn1_w_in), "ffn1_w_out": (ffn1_w_out, m_ffn1_w_out, v_ffn1_w_out),
           "ffn2_w_in": (ffn2_w_in, m_ffn2_w_in, v_ffn2_w_in), "ffn2_w_out": (ffn2_w_out, m_ffn2_w_out, v_ffn2_w_out),
           "sb_w_qkv": (sb_w_qkv, m_sb_w_qkv, v_sb_w_qkv), "sb_w_o": (sb_w_o, m_sb_w_o, v_sb_w_o),
           "kv_w": (kv_w, m_kv_w, v_kv_w), "swa_w_q": (swa_w_q, m_swa_w_q, v_swa_w_q),
           "swa_w_o": (swa_w_o, m_swa_w_o, v_swa_w_o)}
    for nm, (w, m, v) in big.items():
        g = grads[nm]
        two_d = lambda t: t.reshape(-1, t.shape[-1])
        d, new_m, new_v = _adamw(two_d(g), two_d(w), two_d(m), two_d(v), f"adamw_{nm}")
        result["grad"][nm] = g
        result["delta"][nm] = d.reshape(w.shape)
        result["new_m"][nm] = new_m.reshape(w.shape)
        result["new_v"][nm] = new_v.reshape(w.shape)

    order = ["ffn1_norm", "ffn1_w_in", "ffn1_w_out", "mix_norm", "ffn2_norm", "ffn2_w_in", "ffn2_w_out",
             "sb_w_qkv", "sb_w_o", "kv_norm", "kv_w", "swa_w_q", "swa_sinks", "swa_w_o", "final_norm"]
    outs = [result[kind][nm] for kind in ("grad", "delta", "new_m", "new_v") for nm in order]
    return (loss, dx.reshape(x.shape), *outs)
```

```python
import jax
import jax.numpy as jnp
from jax import lax
from jax.experimental import pallas as pl
from jax.experimental.pallas import tpu as pltpu

F32 = jnp.float32
BF16 = jnp.bfloat16

N_DEV = 8
HEAD_DIM = 64
LANES = 128
BLK = 128
RMS_EPS = 1e-6
FFN_RES_SCALE = 0.5
ROPE_THETA = 10000.0
ATTN_SCALE = HEAD_DIM ** -0.5
SB_LOG_FLOOR = -88.0
NEG_BIG = -1e30
VMEM_LIMIT_V7X = 56 * 1024 * 1024

ADAM_LR = 0.001
ADAM_B1 = 0.9
ADAM_B2 = 0.999
ADAM_EPS = 1e-08
ADAM_WD = 0.01
ADAM_STEP = 10

NN = ((1,), (0,))
NT = ((1,), (1,))
TN = ((0,), (0,))
TN_CHUNK = 2048
MESH = pl.DeviceIdType.MESH


def _dot(a, b, dims):
    return lax.dot_general(a, b, (dims, ((), ())), preferred_element_type=F32)


def _tile(n, pref, mult=LANES):
    if n <= pref:
        return n
    t = (pref // mult) * mult
    while t >= mult:
        if n % t == 0:
            return t
        t -= mult
    return n


def _params(*sem):
    return pltpu.CompilerParams(dimension_semantics=sem, vmem_limit_bytes=VMEM_LIMIT_V7X)


def _mm(a, b, dims, out_dtype, name, scale=1.0, res=None, tm=512, tn=512, tk=512):
    if dims == NN:
        (M, K), (_, N) = a.shape, b.shape
    elif dims == NT:
        (M, K), (N, _) = a.shape, b.shape
    else:
        (K, M), (_, N) = a.shape, b.shape
    tm, tn, tk = _tile(M, tm), _tile(N, tn), _tile(K, tk)
    nk = K // tk
    if dims == TN:
        a_spec = pl.BlockSpec((tk, tm), lambda i, j, k: (k, i))
    else:
        a_spec = pl.BlockSpec((tm, tk), lambda i, j, k: (i, k))
    if dims == NT:
        b_spec = pl.BlockSpec((tn, tk), lambda i, j, k: (j, k))
    else:
        b_spec = pl.BlockSpec((tk, tn), lambda i, j, k: (k, j))
    o_spec = pl.BlockSpec((tm, tn), lambda i, j, k: (i, j))
    has_res = res is not None

    def body(*refs):
        a_ref, b_ref = refs[0], refs[1]
        r_ref = refs[2] if has_res else None
        o_ref = refs[3] if has_res else refs[2]

        def finish(acc):
            r = acc * scale if scale != 1.0 else acc
            if has_res:
                r = r + r_ref[...]
            o_ref[...] = r.astype(out_dtype)

        p = _dot(a_ref[...].astype(BF16), b_ref[...].astype(BF16), dims)
        if nk == 1:
            finish(p)
        else:
            acc_ref = refs[-1]
            k = pl.program_id(2)

            @pl.when(k == 0)
            def _():
                acc_ref[...] = p

            @pl.when(k > 0)
            def _():
                acc_ref[...] += p

            @pl.when(k == nk - 1)
            def _():
                finish(acc_ref[...])

    in_specs = [a_spec, b_spec] + ([o_spec] if has_res else [])
    args = (a, b) + ((res,) if has_res else ())
    return pl.pallas_call(
        body, name=name,
        out_shape=jax.ShapeDtypeStruct((M, N), out_dtype),
        grid=(M // tm, N // tn, nk),
        in_specs=in_specs, out_specs=o_spec,
        scratch_shapes=[pltpu.VMEM((tm, tn), F32)] if nk > 1 else [],
        compiler_params=_params("parallel", "parallel", "arbitrary"),
    )(*args)


def _rows8(x):
    r, d = x.shape
    return jnp.sum(x.reshape(r // 8, 8, d), axis=0)


def _norm_proj(h, g, w, dims, name, rope=None, tail_t=0):
    S, D = h.shape
    N = w.shape[1] if dims == NN else w.shape[0]
    tm = _tile(S, 512, 16)

    def body(h_ref, g_ref, w_ref, *rest):
        xn_ref, y_ref = rest[-3:-1] if tail_t else rest[-2:]
        x = h_ref[...]
        r = lax.rsqrt(jnp.mean(x * x, axis=-1, keepdims=True) + RMS_EPS)
        xn = ((x * r) * g_ref[...]).astype(BF16)
        xn_ref[...] = xn
        y = _dot(xn, w_ref[...], dims)
        if rope is not None:
            cs, sn = rest[0][...], rest[1][...]
            groups = [y[:, gidx * LANES:(gidx + 1) * LANES] for gidx in range(N // LANES)]
            y = jnp.concatenate([v * cs + _swap_halves(v) * sn if gidx < rope[2] else v
                                 for gidx, v in enumerate(groups)], axis=1)
        y_ref[...] = y.astype(BF16)
        if tail_t:
            rest[-1][...] = jnp.transpose(y[:, N - tail_t:]).astype(BF16)

    row = pl.BlockSpec((tm, D), lambda i: (i, 0))
    tab = pl.BlockSpec((tm, LANES), lambda i: (i, 0))
    in_specs = [row, pl.BlockSpec((1, D), lambda i: (0, 0)), pl.BlockSpec(w.shape, lambda i: (0, 0))]
    args = (h, g.reshape(1, D), w)
    if rope is not None:
        in_specs += [tab, tab]
        args += (rope[0], rope[1])
    out_shape = [jax.ShapeDtypeStruct((S, D), BF16), jax.ShapeDtypeStruct((S, N), BF16)]
    out_specs = [row, pl.BlockSpec((tm, N), lambda i: (i, 0))]
    if tail_t:
        out_shape.append(jax.ShapeDtypeStruct((tail_t, S), BF16))
        out_specs.append(pl.BlockSpec((tail_t, tm), lambda i: (0, i)))
    return pl.pallas_call(
        body, name=name, out_shape=out_shape, grid=(S // tm,),
        in_specs=in_specs, out_specs=out_specs,
        compiler_params=_params("parallel"),
    )(*args)


def _ffn_up(h, g, win_t, name, carry=None):
    S, D = h.shape
    F = win_t.shape[0] // 2
    tm = _tile(S, 256, 16)

    def body(h_ref, g_ref, win_hbm, xn_ref, silu_ref, dsilu_ref, up_ref, act_ref, win_v, sems):
        _load_resident([(win_hbm, win_v)], sems)
        x = h_ref[...]
        r = lax.rsqrt(jnp.mean(x * x, axis=-1, keepdims=True) + RMS_EPS)
        xn = ((x * r) * g_ref[...]).astype(BF16)
        xn_ref[...] = xn
        gate = _dot(xn, win_v[:F, :], NT)
        up = _dot(xn, win_v[F:, :], NT)
        sig = 1.0 / (1.0 + jnp.exp(-gate))
        silu = gate * sig
        up_ref[...] = up.astype(BF16)
        silu_ref[...] = silu.astype(BF16)
        dsilu_ref[...] = (sig + silu * (1.0 - sig)).astype(BF16)
        act_ref[...] = (silu * up).astype(BF16)

    row = pl.BlockSpec((tm, D), lambda i: (i, 0))
    wide = pl.BlockSpec((tm, F), lambda i: (i, 0))
    hid = jax.ShapeDtypeStruct((S, F), BF16)
    return _pcall(
        body, (h, g.reshape(1, D), win_t), name=name,
        out_shape=(jax.ShapeDtypeStruct((S, D), BF16), hid, hid, hid, hid),
        grid=(S // tm,),
        in_specs=[row, pl.BlockSpec((1, D), lambda i: (0, 0)), pl.BlockSpec(memory_space=pl.ANY)],
        out_specs=(row, wide, wide, wide, wide),
        scratch_shapes=[pltpu.VMEM(win_t.shape, BF16), pltpu.SemaphoreType.DMA((1,))],
        sem=("arbitrary",), carry=carry)


def _ffn_dact(dh, wo, silu, dsilu, up, name):
    S, D = dh.shape
    F = wo.shape[0]
    tm = _tile(S, 256, 16)

    def body(dh_ref, wo_hbm, s_ref, ds_ref, u_ref, dg_ref, du_ref, wo_v, sems):
        _load_resident([(wo_hbm, wo_v)], sems)
        d = _dot(dh_ref[...].astype(BF16), wo_v[...], NT) * FFN_RES_SCALE
        du_ref[...] = (d * s_ref[...].astype(F32)).astype(BF16)
        dg_ref[...] = (d * u_ref[...].astype(F32) * ds_ref[...].astype(F32)).astype(BF16)

    wide = pl.BlockSpec((tm, F), lambda i: (i, 0))
    hid = jax.ShapeDtypeStruct((S, F), BF16)
    return pl.pallas_call(
        body, name=name, out_shape=(hid, hid),
        grid=(S // tm,),
        in_specs=[pl.BlockSpec((tm, D), lambda i: (i, 0)), pl.BlockSpec(memory_space=pl.ANY), wide, wide, wide],
        out_specs=(wide, wide),
        scratch_shapes=[pltpu.VMEM(wo.shape, BF16), pltpu.SemaphoreType.DMA((1,))],
        compiler_params=_params("arbitrary"),
    )(dh, wo, silu, dsilu, up)


def _dw_rows(srcs, x, name, carry=None, tk=TN_CHUNK):
    n = len(srcs)
    S, F = srcs[0].shape
    D = x.shape[1]
    tr, tk = _tile(F, 1408), _tile(S, tk, 16)
    nf, nk = F // tr, S // tk

    def body(*refs):
        src_refs, (x_ref, o_ref, acc_ref) = refs[:n], refs[n:]
        r, k = pl.program_id(0), pl.program_id(1)
        for s in range(n):
            @pl.when(r // nf == s)
            def _():
                p = _dot(src_refs[s][...].astype(BF16), x_ref[...], TN)

                @pl.when(k == 0)
                def _():
                    acc_ref[...] = p

                @pl.when(k > 0)
                def _():
                    acc_ref[...] += p

        @pl.when(k == nk - 1)
        def _():
            o_ref[...] = acc_ref[...].astype(BF16)

    def src_spec(s):
        return pl.BlockSpec((tk, tr), lambda r, k: (jnp.where(r // nf == s, k, 0), jnp.clip(r - s * nf, 0, nf - 1)))

    return _pcall(
        body, (*srcs, x), name=name, out_shape=jax.ShapeDtypeStruct((n * F, D), BF16),
        grid=(n * nf, nk),
        in_specs=[src_spec(s) for s in range(n)] + [pl.BlockSpec((tk, D), lambda r, k: (k, 0))],
        out_specs=pl.BlockSpec((tr, D), lambda r, k: (r, 0)),
        scratch_shapes=[pltpu.VMEM((tr, D), F32)],
        sem=("arbitrary", "arbitrary"), carry=carry)


def _dx_norm_bwd(terms, h, g, res, name, carry=None, tm=256):
    S, D = h.shape
    tm = _tile(S, tm, 16)
    n = len(terms)

    def body(*refs):
        dy_refs, w_refs = refs[:n], refs[n:2 * n]
        h_ref, g_ref, r_ref, dh_ref, dg_ref = refs[2 * n:]
        d = _dot(dy_refs[0][...].astype(BF16), w_refs[0][...], terms[0][2])
        for t in range(1, n):
            d = d + _dot(dy_refs[t][...].astype(BF16), w_refs[t][...], terms[t][2])
        x = h_ref[...]
        r = lax.rsqrt(jnp.mean(x * x, axis=-1, keepdims=True) + RMS_EPS)
        xhat = x * r
        dxh = d * g_ref[...]
        c = jnp.mean(dxh * xhat, axis=-1, keepdims=True)
        dh_ref[...] = r * (dxh - xhat * c) + r_ref[...]
        part = _rows8(d * xhat)

        @pl.when(pl.program_id(0) == 0)
        def _():
            dg_ref[...] = part

        @pl.when(pl.program_id(0) > 0)
        def _():
            dg_ref[...] += part

    def w_spec(w, nblk, blk):
        return pl.BlockSpec((w.shape[0] // nblk, w.shape[1]), lambda i: (blk, 0))

    row = pl.BlockSpec((tm, D), lambda i: (i, 0))
    in_specs = [pl.BlockSpec((tm, t[0].shape[1]), lambda i: (i, 0)) for t in terms]
    in_specs += [w_spec(t[1], t[3], t[4]) for t in terms]
    in_specs += [row, pl.BlockSpec((1, D), lambda i: (0, 0)), row]
    return _pcall(
        body, (*[t[0] for t in terms], *[t[1] for t in terms], h, g.reshape(1, D), res), name=name,
        out_shape=(jax.ShapeDtypeStruct((S, D), F32), jax.ShapeDtypeStruct((8, D), F32)),
        grid=(S // tm,),
        in_specs=in_specs,
        out_specs=(row, pl.BlockSpec((8, D), lambda i: (0, 0))),
        sem=("arbitrary",), carry=carry)


def _load_resident(pairs, sems):
    @pl.when(pl.program_id(0) == 0)
    def _():
        copies = [pltpu.make_async_copy(src, dst, sems.at[n]) for n, (src, dst) in enumerate(pairs)]
        for cp in copies:
            cp.start()
        for cp in copies:
            cp.wait()


def _loss_tail(y_in, g, tgt):
    D = y_in.shape[-1]
    r = lax.rsqrt(jnp.mean(y_in * y_in, axis=-1, keepdims=True) + RMS_EPS)
    xhat = y_in * r
    err = xhat * g - tgt
    d = err * (1.0 / D)
    dxh = d * g
    c = jnp.mean(dxh * xhat, axis=-1, keepdims=True)
    return r * (dxh - xhat * c), _rows8(d * xhat), _rows8(err * err)


def _ffn_fwd_fused(h, g, win_t, wo, name, carry=None, loss=None):
    S, D = h.shape
    F = wo.shape[0]
    tm = _tile(S, 256, 16)
    n_head = 3 if loss is not None else 1

    def body(h_ref, g_ref, win_hbm, wo_hbm, *rest):
        lead, (xn_ref, silu_ref, dsilu_ref, up_ref, act_ref, win_v, wo_v, sems) = rest[:-8], rest[-8:]
        _load_resident([(win_hbm, win_v), (wo_hbm, wo_v)], sems)
        x = h_ref[...]
        r = lax.rsqrt(jnp.mean(x * x, axis=-1, keepdims=True) + RMS_EPS)
        xn = ((x * r) * g_ref[...]).astype(BF16)
        xn_ref[...] = xn
        gate = _dot(xn, win_v[:F, :], NT)
        up = _dot(xn, win_v[F:, :], NT)
        sig = 1.0 / (1.0 + jnp.exp(-gate))
        silu = gate * sig
        act = (silu * up).astype(BF16)
        up_ref[...] = up.astype(BF16)
        silu_ref[...] = silu.astype(BF16)
        dsilu_ref[...] = (sig + silu * (1.0 - sig)).astype(BF16)
        act_ref[...] = act
        out = x + FFN_RES_SCALE * _dot(act, wo_v[...], NN)
        if loss is None:
            lead[0][...] = out
        else:
            gf_ref, t_ref, dy_ref, dgf_ref, sq_ref = lead
            dy, dgf, sq = _loss_tail(out, gf_ref[...], t_ref[...])
            dy_ref[...] = dy

            @pl.when(pl.program_id(0) == 0)
            def _():
                dgf_ref[...] = dgf
                sq_ref[...] = sq

            @pl.when(pl.program_id(0) > 0)
            def _():
                dgf_ref[...] += dgf
                sq_ref[...] += sq

    row = pl.BlockSpec((tm, D), lambda i: (i, 0))
    vec = pl.BlockSpec((1, D), lambda i: (0, 0))
    acc = pl.BlockSpec((8, D), lambda i: (0, 0))
    wide = pl.BlockSpec((tm, F), lambda i: (i, 0))
    hbm = pl.BlockSpec(memory_space=pl.ANY)
    hid = jax.ShapeDtypeStruct((S, F), BF16)
    full = jax.ShapeDtypeStruct((S, D), F32)
    part = jax.ShapeDtypeStruct((8, D), F32)
    args, in_specs = (h, g.reshape(1, D), win_t, wo), [row, vec, hbm, hbm]
    lead_shapes, lead_specs = (full,), (row,)
    if loss is not None:
        args, in_specs = args + (loss[0].reshape(1, D), loss[1]), in_specs + [vec, row]
        lead_shapes, lead_specs = (full, part, part), (row, acc, acc)
    res, got = _pcall(
        body, args, name=name,
        out_shape=lead_shapes + (jax.ShapeDtypeStruct((S, D), BF16), hid, hid, hid, hid),
        grid=(S // tm,),
        in_specs=in_specs,
        out_specs=lead_specs + (row, wide, wide, wide, wide),
        scratch_shapes=[pltpu.VMEM(win_t.shape, BF16), pltpu.VMEM(wo.shape, BF16), pltpu.SemaphoreType.DMA((2,))],
        sem=("arbitrary",), carry=carry)
    first = res[0] if loss is None else tuple(res[:3])
    return first, tuple(res[n_head:]), got


def _ffn_bwd_fused(dh, h, g, win_t, wo, silu, dsilu, up, name, carry=None):
    S, D = h.shape
    F = wo.shape[0]
    tm = _tile(S, 256, 16)

    def body(dh_ref, h_ref, g_ref, s_ref, ds_ref, u_ref, win_hbm, wo_hbm,
             dhin_ref, dgain_ref, dgate_ref, dup_ref, win_v, wo_v, sems):
        _load_resident([(win_hbm, win_v), (wo_hbm, wo_v)], sems)
        dhv = dh_ref[...]
        d = _dot(dhv.astype(BF16), wo_v[...], NT) * FFN_RES_SCALE
        dup = (d * s_ref[...].astype(F32)).astype(BF16)
        dgate = (d * u_ref[...].astype(F32) * ds_ref[...].astype(F32)).astype(BF16)
        dup_ref[...] = dup
        dgate_ref[...] = dgate
        dxn = _dot(dgate, win_v[:F, :], NN) + _dot(dup, win_v[F:, :], NN)
        x = h_ref[...]
        r = lax.rsqrt(jnp.mean(x * x, axis=-1, keepdims=True) + RMS_EPS)
        xhat = x * r
        dxh = dxn * g_ref[...]
        c = jnp.mean(dxh * xhat, axis=-1, keepdims=True)
        dhin_ref[...] = r * (dxh - xhat * c) + dhv
        part = _rows8(dxn * xhat)

        @pl.when(pl.program_id(0) == 0)
        def _():
            dgain_ref[...] = part

        @pl.when(pl.program_id(0) > 0)
        def _():
            dgain_ref[...] += part

    row = pl.BlockSpec((tm, D), lambda i: (i, 0))
    wide = pl.BlockSpec((tm, F), lambda i: (i, 0))
    hbm = pl.BlockSpec(memory_space=pl.ANY)
    hid = jax.ShapeDtypeStruct((S, F), BF16)
    return _pcall(
        body, (dh, h, g.reshape(1, D), silu, dsilu, up, win_t, wo), name=name,
        out_shape=(jax.ShapeDtypeStruct((S, D), F32), jax.ShapeDtypeStruct((8, D), F32), hid, hid),
        grid=(S // tm,),
        in_specs=[row, row, pl.BlockSpec((1, D), lambda i: (0, 0)), wide, wide, wide, hbm, hbm],
        out_specs=(row, pl.BlockSpec((8, D), lambda i: (0, 0)), wide, wide),
        scratch_shapes=[pltpu.VMEM(win_t.shape, BF16), pltpu.VMEM(wo.shape, BF16), pltpu.SemaphoreType.DMA((2,))],
        sem=("arbitrary",), carry=carry)


def _rope_tables(S):
    half = HEAD_DIM // 2
    inv_freq = ROPE_THETA ** (-jnp.arange(half, dtype=F32) / half)
    ang = jnp.arange(S).astype(F32)[:, None] * inv_freq[None, :]
    cos, sin = jnp.cos(ang), jnp.sin(ang)
    cos_t = jnp.tile(cos, (1, LANES // half))
    sin_t = jnp.tile(jnp.concatenate([-sin, sin], axis=1), (1, LANES // HEAD_DIM))
    return cos_t, sin_t


def _swap_halves(x):
    lane = lax.broadcasted_iota(jnp.int32, x.shape, 1)
    first = (lane % HEAD_DIM) < (HEAD_DIM // 2)
    return jnp.where(first, pltpu.roll(x, LANES - HEAD_DIM // 2, 1), pltpu.roll(x, HEAD_DIM // 2, 1))


def _rotary(x, cos_t, sin_t, n_rot, inverse, name):
    S, C = x.shape
    ts = _tile(S, 512, 16)
    ng = C // LANES

    def body(x_ref, c_ref, s_ref, o_ref):
        cs, sn = c_ref[...], s_ref[...]
        for gidx in range(ng):
            sl = slice(gidx * LANES, (gidx + 1) * LANES)
            v = x_ref[:, sl].astype(F32)
            if gidx < n_rot:
                if inverse:
                    v = v * cs + _swap_halves(v * sn)
                else:
                    v = v * cs + _swap_halves(v) * sn
            o_ref[:, sl] = v.astype(BF16)

    row = pl.BlockSpec((ts, C), lambda i: (i, 0))
    tab = pl.BlockSpec((ts, LANES), lambda i: (i, 0))
    return pl.pallas_call(
        body, name=name, out_shape=jax.ShapeDtypeStruct((S, C), BF16),
        grid=(S // ts,), in_specs=[row, tab, tab], out_specs=row,
        compiler_params=_params("parallel"),
    )(x, cos_t, sin_t)


def _head_masks():
    lane = lax.broadcasted_iota(jnp.int32, (BLK, LANES), 1)
    return lane < HEAD_DIM


def _split_bf16(x):
    hi = x.astype(BF16)
    lo = (x - hi.astype(F32)).astype(BF16)
    return hi, lo


def _sb_scores(qh, ks, carry, diag, tri_excl, strict):
    n_heads = len(qh)
    zs = [_dot(ks[n], qh[n], NT) for n in range(n_heads)]
    a_l, b_l, split_l = [], [], []
    for z in zs:
        a = jnp.minimum(z, 0.0) - jnp.log(1.0 + jnp.exp(-jnp.abs(z)))
        b = a - z
        if diag:
            b = jnp.where(strict, b, 0.0)
        a_l.append(a)
        b_l.append(b)
        split_l.append(_split_bf16(b))
    sufs = [_dot(tri_excl, hi, NN) + _dot(tri_excl, lo, NN) for hi, lo in split_l]
    w_l = []
    for n in range(n_heads):
        w = jnp.exp(a_l[n] + sufs[n] + carry[n])
        if diag:
            w = jnp.where(strict, w, 0.0)
        w_l.append(w)
    return a_l, b_l, w_l


SB_FWD_PAIRS = 4
SB_FWD_QBLOCKS = 4
SB_BWD_PAIRS = 2
SB_BWD_QBLOCKS = 4


def _any_alive(carries):
    top = carries[0]
    for c in carries[1:]:
        top = jnp.maximum(top, c)
    return (jnp.max(top) > SB_LOG_FLOOR).astype(jnp.int32)


def _sb_masks():
    row = lax.broadcasted_iota(jnp.int32, (BLK, BLK), 0)
    col = lax.broadcasted_iota(jnp.int32, (BLK, BLK), 1)
    tri_excl = jnp.where(col > row, 1.0, 0.0).astype(BF16)
    tri_incl = jnp.where(col >= row, 1.0, 0.0).astype(BF16)
    return row < HEAD_DIM, row < col, tri_excl, tri_incl


def _sb_fwd(qkv, kv_t, name, carry=None):
    S, D3 = qkv.shape
    D = D3 // 3
    npair, nb = D // LANES, S // BLK
    P = min(SB_FWD_PAIRS, npair)
    ngroup = npair // P
    W = P * LANES

    QB = SB_FWD_QBLOCKS if nb % SB_FWD_QBLOCKS == 0 else 1
    nch = QB * 2 * P

    def body(q_ref, k_ref, vt_ref, o_ref):
        i_first = pl.program_id(1) * QB
        m0 = _head_masks()
        top, strict, tri_excl, _ = _sb_masks()
        zq = jnp.zeros((BLK, LANES), BF16)
        lanes = [slice(p * LANES, (p + 1) * LANES) for p in range(P)]
        qh = []
        for qb in range(QB):
            for sl in lanes:
                q2 = q_ref[qb * BLK:(qb + 1) * BLK, sl] * ATTN_SCALE
                qh += [jnp.where(m0, q2, zq), jnp.where(m0, zq, q2)]

        def block(qbs, js, carry, acc, diag):
            offs = [pl.multiple_of(j * BLK, BLK) for j in js]
            ks, vth, qs = [], [], []
            for n_qb, qb in enumerate(qbs):
                qs += qh[qb * 2 * P:(qb + 1) * 2 * P]
                for sl in lanes:
                    k2 = k_ref[pl.ds(offs[n_qb], BLK), sl]
                    vt = vt_ref[sl, pl.ds(offs[n_qb], BLK)]
                    ks += [k2, k2]
                    vth += [jnp.where(top, vt, zq), jnp.where(top, zq, vt)]
            _, b_l, w_l = _sb_scores(qs, ks, carry, diag, tri_excl, strict)
            wb = [w.astype(BF16) for w in w_l]
            new_acc = [acc[m] + _dot(vth[2 * m], wb[2 * m], NN) + _dot(vth[2 * m + 1], wb[2 * m + 1], NN)
                       for m in range(len(qbs) * P)]
            new_carry = [carry[n] + jnp.sum(b_l[n], axis=0, keepdims=True) for n in range(len(carry))]
            return new_carry, new_acc

        every = list(range(QB))
        c0 = jnp.zeros((1, BLK), F32)
        carry, acc = block(every, [i_first + qb for qb in every], [c0] * nch,
                           [jnp.zeros((LANES, BLK), F32)] * (QB * P), True)
        carry = [jnp.where(i_first > 0, c, NEG_BIG) for c in carry[:2 * P]] + carry[2 * P:]
        carry, acc = block(every, [jnp.maximum(i_first + qb - 1, 0) for qb in every], carry, acc, False)

        for qb in range(QB):
            i_qb = i_first + qb
            sub = slice(qb * 2 * P, (qb + 1) * 2 * P)

            def cond(st):
                return jnp.logical_and(i_qb - st[0] >= 0, st[1] > 0)

            def step(st, qb=qb, i_qb=i_qb):
                t, _, c_qb, a_qb = st
                c_qb, a_qb = block([qb], [i_qb - t], c_qb, a_qb, False)
                return t + 1, _any_alive(c_qb), c_qb, a_qb

            st = lax.while_loop(cond, step, (2, _any_alive(carry[sub]), carry[sub], acc[qb * P:(qb + 1) * P]))
            for p, sl in enumerate(lanes):
                o_ref[qb * BLK:(qb + 1) * BLK, sl] = jnp.transpose(st[3][p])

    return _pcall(
        body, (qkv, qkv, kv_t), name=name, out_shape=jax.ShapeDtypeStruct((S, D), F32),
        grid=(ngroup, nb // QB),
        in_specs=[pl.BlockSpec((QB * BLK, W), lambda g, i: (i, g)),
                  pl.BlockSpec((S, W), lambda g, i: (0, ngroup + g)),
                  pl.BlockSpec((W, S), lambda g, i: (ngroup + g, 0))],
        out_specs=pl.BlockSpec((QB * BLK, W), lambda g, i: (i, g)),
        sem=("arbitrary", "arbitrary"), carry=carry)


def _sb_bwd(qkv, kv_t, o, do, name, carry=None):
    S, D3 = qkv.shape
    D = D3 // 3
    npair, nb = D // LANES, S // BLK
    P = min(SB_BWD_PAIRS, npair)
    ngroup = npair // P
    W = P * LANES

    QB = SB_BWD_QBLOCKS if nb % SB_BWD_QBLOCKS == 0 else 1
    nch = QB * 2 * P

    def body(q_ref, o_ref, do_ref, qkv_hbm, kt_hbm, dq_ref, dk_ref, dv_ref, k_ref, v_ref, kt_ref, sems):
        grp = pl.program_id(0)
        i_first = pl.program_id(1) * QB
        m0 = _head_masks()
        top, strict, tri_excl, tri_incl = _sb_masks()
        zq = jnp.zeros((BLK, LANES), BF16)
        lanes = [slice(p * LANES, (p + 1) * LANES) for p in range(P)]

        @pl.when(pl.program_id(1) == 0)
        def _():
            copies = [pltpu.make_async_copy(qkv_hbm.at[:, pl.ds(pl.multiple_of((c * ngroup + grp) * W, LANES), W)],
                                            ref, sems.at[c - 1]) for c, ref in ((1, k_ref), (2, v_ref))]
            copies.append(pltpu.make_async_copy(kt_hbm.at[pl.ds(pl.multiple_of(grp * W, LANES), W), :],
                                                kt_ref, sems.at[2]))
            for cp in copies:
                cp.start()
            dk_ref[...] = jnp.zeros_like(dk_ref)
            dv_ref[...] = jnp.zeros_like(dv_ref)
            for cp in copies:
                cp.wait()

        qh, doh, delta = [], [], []
        for qb in range(QB):
            rs = slice(qb * BLK, (qb + 1) * BLK)
            for sl in lanes:
                q2, do2 = q_ref[rs, sl] * ATTN_SCALE, do_ref[rs, sl]
                qh += [jnp.where(m0, q2, zq), jnp.where(m0, zq, q2)]
                doh += [jnp.where(m0, do2, zq), jnp.where(m0, zq, do2)]
                prod_t = jnp.transpose(do2.astype(F32) * o_ref[rs, sl])
                delta += [jnp.sum(jnp.where(top, prod_t, 0.0), axis=0, keepdims=True),
                          jnp.sum(jnp.where(top, 0.0, prod_t), axis=0, keepdims=True)]

        def block(qbs, js, valid, cb, cg, dq, diag):
            offs = [pl.multiple_of(j * BLK, BLK) for j in js]
            n_ch = len(qbs) * 2 * P
            ks, vs, kth, qs, dos, dls = [], [], [], [], [], []
            for n_qb, qb in enumerate(qbs):
                chains = slice(qb * 2 * P, (qb + 1) * 2 * P)
                qs, dos, dls = qs + qh[chains], dos + doh[chains], dls + delta[chains]
                for sl in lanes:
                    k2, v2 = k_ref[pl.ds(offs[n_qb], BLK), sl], v_ref[pl.ds(offs[n_qb], BLK), sl]
                    ks += [k2, k2]
                    vs += [v2, v2]
                    kt = kt_ref[sl, pl.ds(offs[n_qb], BLK)] * ATTN_SCALE
                    kth += [jnp.where(top, kt, zq), jnp.where(top, zq, kt)]
            dws = [_dot(vs[n], dos[n], NT) for n in range(n_ch)]
            a_l, b_l, w_l = _sb_scores(qs, ks, cb, diag, tri_excl, strict)
            wb = [w.astype(BF16) for w in w_l]
            g_l = [dws[n] * wb[n].astype(F32) for n in range(n_ch)]
            gsplit = [_split_bf16(g) for g in g_l]
            gincs = [_dot(tri_incl, hi, NN) + _dot(tri_incl, lo, NN) for hi, lo in gsplit]
            dzs = []
            for n in range(n_ch):
                beta = jnp.exp(a_l[n])
                dz = g_l[n] - beta * (g_l[n] + ((dls[n] - cg[n]) - gincs[n]))
                if diag:
                    dz = jnp.where(strict, dz, 0.0)
                if valid[n // (2 * P)] is not None:
                    dz = jnp.where(valid[n // (2 * P)], dz, 0.0)
                dzs.append(dz.astype(BF16))
            ndq = []
            for n_qb in range(len(qbs)):
                for p, sl in enumerate(lanes):
                    n0 = n_qb * 2 * P + 2 * p
                    ndq.append(dq[n_qb * P + p] + _dot(kth[n0], dzs[n0], NN) + _dot(kth[n0 + 1], dzs[n0 + 1], NN))
                    dk_ref[pl.ds(offs[n_qb], BLK), sl] += _dot(dzs[n0], qs[n0], NN) + _dot(dzs[n0 + 1], qs[n0 + 1], NN)
                    dv_ref[pl.ds(offs[n_qb], BLK), sl] += _dot(wb[n0], dos[n0], NN) + _dot(wb[n0 + 1], dos[n0 + 1], NN)
            ncb = [cb[n] + jnp.sum(b_l[n], axis=0, keepdims=True) for n in range(n_ch)]
            ncg = [cg[n] + jnp.sum(g_l[n], axis=0, keepdims=True) for n in range(n_ch)]
            return ncb, ncg, ndq

        every = list(range(QB))
        c0 = jnp.zeros((1, BLK), F32)
        cb, cg, dq = block(every, [i_first + qb for qb in every], [None] * QB, [c0] * nch, [c0] * nch,
                           [jnp.zeros((LANES, BLK), F32)] * (QB * P), True)
        has_prev = i_first > 0
        cb = [jnp.where(has_prev, c, NEG_BIG) for c in cb[:2 * P]] + cb[2 * P:]
        cb, cg, dq = block(every, [jnp.maximum(i_first + qb - 1, 0) for qb in every], [has_prev] + [None] * (QB - 1),
                           cb, cg, dq, False)

        for qb in range(QB):
            i_qb = i_first + qb
            sub = slice(qb * 2 * P, (qb + 1) * 2 * P)

            def cond(st):
                return jnp.logical_and(i_qb - st[0] >= 0, st[1] > 0)

            def step(st, qb=qb, i_qb=i_qb):
                t, _, b_qb, g_qb, dq_qb = st
                b_qb, g_qb, dq_qb = block([qb], [i_qb - t], [None], b_qb, g_qb, dq_qb, False)
                return t + 1, _any_alive(b_qb), b_qb, g_qb, dq_qb

            st = lax.while_loop(cond, step, (2, _any_alive(cb[sub]), cb[sub], cg[sub], dq[qb * P:(qb + 1) * P]))
            for p, sl in enumerate(lanes):
                dq_ref[qb * BLK:(qb + 1) * BLK, sl] = jnp.transpose(st[4][p]).astype(BF16)

    blk = pl.BlockSpec((QB * BLK, W), lambda g, i: (i, g))
    col_all = pl.BlockSpec((S, W), lambda g, i: (0, g))
    hbm = pl.BlockSpec(memory_space=pl.ANY)
    return _pcall(
        body, (qkv, o, do, qkv, kv_t), name=name,
        out_shape=(jax.ShapeDtypeStruct((S, D), BF16), jax.ShapeDtypeStruct((S, D), F32),
                   jax.ShapeDtypeStruct((S, D), F32)),
        grid=(ngroup, nb // QB),
        in_specs=[blk, blk, blk, hbm, hbm],
        out_specs=(blk, col_all, col_all),
        scratch_shapes=[pltpu.VMEM((S, W), BF16), pltpu.VMEM((S, W), BF16), pltpu.VMEM((W, S), BF16),
                        pltpu.SemaphoreType.DMA((3,))],
        sem=("arbitrary", "arbitrary"), carry=carry)


SWA_Q_GROUPS = 4


def _roll_heads(x):
    return pltpu.roll(x.astype(F32), HEAD_DIM, 1).astype(BF16)


def _roll_rows(x):
    return pltpu.roll(x.astype(F32), HEAD_DIM, 0).astype(BF16)


def _swa_valid(i):
    k = lax.broadcasted_iota(jnp.int32, (2 * BLK, BLK), 0)
    q = lax.broadcasted_iota(jnp.int32, (2 * BLK, BLK), 1)
    diff = q + BLK - k
    return (diff >= 0) & (diff < BLK) & ((i > 0) | (k >= BLK))


def _swa_probs(z, valid, sink):
    z = jnp.where(valid, z * ATTN_SCALE, NEG_BIG)
    mx = jnp.maximum(jnp.max(z, axis=0, keepdims=True), sink)
    p = jnp.exp(z - mx)
    ps = jnp.exp(sink - mx)
    inv = 1.0 / (jnp.sum(p, axis=0, keepdims=True) + ps)
    return p * inv, ps * inv


def _swa_operands(q_ref, kc_ref, kp_ref, vc_ref, vp_ref, tc_ref, tp_ref, s_ref, nkvp):
    m0 = _head_masks()
    top = lax.broadcasted_iota(jnp.int32, (LANES, 2 * BLK), 0) < HEAD_DIM
    heads = []
    for m in range(nkvp):
        pair = slice(m * LANES, (m + 1) * LANES)
        kk = jnp.concatenate([kp_ref[:, pair], kc_ref[:, pair]], axis=0)
        vv = jnp.concatenate([vp_ref[:, pair], vc_ref[:, pair]], axis=0)
        tt = jnp.concatenate([tp_ref[pair, :], tc_ref[pair, :]], axis=1)
        ksw, vsw, tsw = _roll_heads(kk), _roll_heads(vv), _roll_rows(tt)
        zt = jnp.zeros_like(tt)
        for c in range(SWA_Q_GROUPS):
            q_lanes = slice((m * SWA_Q_GROUPS + c) * LANES, (m * SWA_Q_GROUPS + c + 1) * LANES)
            qc = q_ref[:, q_lanes]
            zq = jnp.zeros_like(qc)
            for u in range(2):
                same = u == c // 2
                sel = (lambda x, z, mk: jnp.where(mk, x, z)) if u == 0 else (lambda x, z, mk: jnp.where(mk, z, x))
                heads.append(dict(
                    m=m, q_lanes=q_lanes, same=same, sel=sel, qm=sel(qc, zq, m0),
                    k=kk if same else ksw, v=vv if same else vsw,
                    tm=sel(tt if same else tsw, zt, top),
                    sink=s_ref[0, (m * SWA_Q_GROUPS + c) * 2 + u]))
    return heads, m0


def _swa_specs(D, half, t_block):
    prev = lambda i: jnp.maximum(i - 1, 0)
    return [pl.BlockSpec((BLK, D), lambda i: (i, 0)),
            pl.BlockSpec((BLK, half), lambda i: (i, 0)),
            pl.BlockSpec((BLK, half), lambda i: (prev(i), 0)),
            pl.BlockSpec((BLK, half), lambda i: (i, 1)),
            pl.BlockSpec((BLK, half), lambda i: (prev(i), 1)),
            pl.BlockSpec((half, BLK), lambda i: (t_block, i)),
            pl.BlockSpec((half, BLK), lambda i: (t_block, prev(i))),
            pl.BlockSpec(memory_space=pltpu.SMEM)]


def _swa_fwd(q, kv, kv_t, sinks, name):
    S, D = q.shape
    half = kv.shape[1] // 2
    nkvp = half // LANES

    def body(q_ref, kc_ref, kp_ref, vc_ref, vp_ref, tc_ref, tp_ref, s_ref, o_ref):
        valid = _swa_valid(pl.program_id(0))
        heads, _ = _swa_operands(q_ref, kc_ref, kp_ref, vc_ref, vp_ref, tc_ref, tp_ref, s_ref, nkvp)
        zs = [_dot(hd["k"], hd["qm"], NT) for hd in heads]
        ps = [_swa_probs(z, valid, hd["sink"])[0].astype(BF16) for z, hd in zip(zs, heads)]
        for n in range(0, len(heads), 2):
            o_t = _dot(heads[n]["tm"], ps[n], NN) + _dot(heads[n + 1]["tm"], ps[n + 1], NN)
            o_ref[:, heads[n]["q_lanes"]] = jnp.transpose(o_t)

    return pl.pallas_call(
        body, name=name, out_shape=jax.ShapeDtypeStruct((S, D), F32),
        grid=(S // BLK,),
        in_specs=_swa_specs(D, half, 1),
        out_specs=pl.BlockSpec((BLK, D), lambda i: (i, 0)),
        compiler_params=_params("arbitrary"),
    )(q, kv, kv, kv, kv, kv_t, kv_t, sinks)


def _swa_bwd(q, kv, kv_t, sinks, o, do, cos_t, sin_t, name, carry=None):
    S, D = q.shape
    half = kv.shape[1] // 2
    nkvp = half // LANES
    nh = nkvp * 2 * SWA_Q_GROUPS

    def body(q_ref, kc_ref, kp_ref, vc_ref, vp_ref, tc_ref, tp_ref, s_ref, o_ref, do_ref, c_ref, sn_ref,
             dq_ref, dk_ref, dv_ref, ds_ref):
        i = pl.program_id(0)
        valid = _swa_valid(i)
        heads, m0 = _swa_operands(q_ref, kc_ref, kp_ref, vc_ref, vp_ref, tc_ref, tp_ref, s_ref, nkvp)
        top_q = lax.broadcasted_iota(jnp.int32, (LANES, BLK), 0) < HEAD_DIM

        @pl.when(i == 0)
        def _():
            dk_ref[...] = jnp.zeros_like(dk_ref)
            dv_ref[...] = jnp.zeros_like(dv_ref)
            ds_ref[...] = jnp.zeros_like(ds_ref)

        doms, deltas = [], []
        for n in range(0, nh, 2):
            doc = do_ref[:, heads[n]["q_lanes"]]
            prod_t = jnp.transpose(doc.astype(F32) * o_ref[:, heads[n]["q_lanes"]])
            for hd in heads[n:n + 2]:
                doms.append(hd["sel"](doc, jnp.zeros_like(doc), m0))
                deltas.append(jnp.sum(hd["sel"](prod_t, 0.0, top_q), axis=0, keepdims=True))
        zs = [_dot(hd["k"], hd["qm"], NT) for hd in heads]
        dps = [_dot(hd["v"], dom, NT) for dom, hd in zip(doms, heads)]
        pbs, dscs = [], []
        for n, hd in enumerate(heads):
            p, psink = _swa_probs(zs[n], valid, hd["sink"])
            pbs.append(p.astype(BF16))
            dscs.append((p * (dps[n] - deltas[n]) * ATTN_SCALE).astype(BF16))
            ds_ref[n:n + 1, :] += -(psink * deltas[n])
        for n in range(0, nh, 2):
            dq_rot = jnp.transpose(_dot(heads[n]["tm"], dscs[n], NN) + _dot(heads[n + 1]["tm"], dscs[n + 1], NN))
            dq_ref[:, heads[n]["q_lanes"]] = (
                dq_rot * c_ref[...] + _swap_halves(dq_rot * sn_ref[...])).astype(BF16)
        acc = {}
        for n, hd in enumerate(heads):
            dk_n = _dot(dscs[n], hd["qm"], NN)
            dv_n = _dot(pbs[n], doms[n], NN)
            for key, val in ((("k", hd["m"], hd["same"]), dk_n), (("v", hd["m"], hd["same"]), dv_n)):
                acc[key] = val if key not in acc else acc[key] + val
        poff = pl.multiple_of(jnp.maximum(i - 1, 0) * BLK, BLK)
        coff = pl.multiple_of(i * BLK, BLK)
        for m in range(nkvp):
            pair = slice(m * LANES, (m + 1) * LANES)
            dkk = acc["k", m, True] + pltpu.roll(acc["k", m, False], HEAD_DIM, 1)
            dvv = acc["v", m, True] + pltpu.roll(acc["v", m, False], HEAD_DIM, 1)
            dk_ref[pl.ds(poff, BLK), pair] += dkk[:BLK]
            dv_ref[pl.ds(poff, BLK), pair] += dvv[:BLK]
            dk_ref[pl.ds(coff, BLK), pair] += dkk[BLK:]
            dv_ref[pl.ds(coff, BLK), pair] += dvv[BLK:]

    qblk = pl.BlockSpec((BLK, D), lambda i: (i, 0))
    whole = pl.BlockSpec((S, half), lambda i: (0, 0))
    tab = pl.BlockSpec((BLK, LANES), lambda i: (i, 0))
    return _pcall(
        body, (q, kv, kv, kv, kv, kv_t, kv_t, sinks, o, do, cos_t, sin_t), name=name,
        out_shape=(jax.ShapeDtypeStruct((S, D), BF16),
                   jax.ShapeDtypeStruct((S, half), F32),
                   jax.ShapeDtypeStruct((S, half), F32),
                   jax.ShapeDtypeStruct((nh, LANES), F32)),
        grid=(S // BLK,),
        in_specs=_swa_specs(D, half, 0) + [qblk, qblk, tab, tab],
        out_specs=(qblk, whole, whole, pl.BlockSpec((nh, LANES), lambda i: (0, 0))),
        sem=("arbitrary",), carry=carry)


def _dev_index(p):
    return 4 * p[0] + 2 * p[1] + p[2]


def _gather_plan(x_refs, out_refs, send_sems, recv_sems, local_sems):
    n = len(x_refs)
    x_, y_, c_ = lax.axis_index("x"), lax.axis_index("y"), lax.axis_index("c")
    me, sibling = (x_, y_, c_), (x_, y_, 1 - c_)
    chips = [(1 - x_, y_), (x_, 1 - y_), (1 - x_, 1 - y_)]

    def copy(t, k, block, to, src=None):
        dst = out_refs[t].at[_dev_index(block)]
        return pltpu.make_async_remote_copy(
            src_ref=dst if src is None else src, dst_ref=dst,
            send_sem=send_sems.at[7 * t + k], recv_sem=recv_sems.at[7 * t + k],
            device_id=to, device_id_type=MESH)

    mine = [pltpu.make_async_copy(x_refs[t], out_refs[t].at[_dev_index(me)], local_sems.at[t]) for t in range(n)]
    first = []
    for t in range(n):
        first.append(copy(t, 0, me, sibling, src=x_refs[t]))
        first += [copy(t, 1 + j, me, (*chip, c_), src=x_refs[t]) for j, chip in enumerate(chips)]
    arrived = lambda t, j: copy(t, 1 + j, (*chips[j], c_), me)
    forward = lambda t, j: copy(t, 4 + j, (*chips[j], c_), sibling)
    from_sibling = lambda t: copy(t, 0, sibling, me)
    forwarded = lambda t, j: copy(t, 4 + j, (*chips[j], 1 - c_), me)
    return n, mine, first, arrived, forward, from_sibling, forwarded


def _gather_start(x_refs, out_refs, send_sems, recv_sems, local_sems):
    _, mine, first, *_ = _gather_plan(x_refs, out_refs, send_sems, recv_sems, local_sems)
    for cp in mine + first:
        cp.start()


def _gather_forward(x_refs, out_refs, send_sems, recv_sems, local_sems):
    n, _, _, arrived, forward, _, _ = _gather_plan(x_refs, out_refs, send_sems, recv_sems, local_sems)
    for j in range(3):
        for t in range(n):
            arrived(t, j).wait_recv()
            forward(t, j).start()


def _gather_finish(x_refs, out_refs, send_sems, recv_sems, local_sems):
    n, mine, first, _, forward, from_sibling, forwarded = _gather_plan(
        x_refs, out_refs, send_sems, recv_sems, local_sems)
    for t in range(n):
        from_sibling(t).wait_recv()
    for j in range(3):
        for t in range(n):
            forwarded(t, j).wait_recv()
    for cp in first + [forward(t, j) for j in range(3) for t in range(n)]:
        cp.wait_send()
    for cp in mine:
        cp.wait()


def _scatter_plan(b_refs, out_refs, send_sems, recv_sems, local_sems):
    n = len(b_refs)
    x_, y_, c_ = lax.axis_index("x"), lax.axis_index("y"), lax.axis_index("c")
    my_idx = _dev_index((x_, y_, c_))
    mine = [pltpu.make_async_copy(b_refs[t].at[my_idx], out_refs[t].at[my_idx], local_sems.at[t]) for t in range(n)]
    copies = []
    for t in range(n):
        for k in range(1, N_DEV):
            peer = (x_ ^ ((k >> 2) & 1), y_ ^ ((k >> 1) & 1), c_ ^ (k & 1))
            copies.append(pltpu.make_async_remote_copy(
                src_ref=b_refs[t].at[_dev_index(peer)], dst_ref=out_refs[t].at[my_idx],
                send_sem=send_sems.at[7 * t + k - 1], recv_sem=recv_sems.at[7 * t + k - 1],
                device_id=peer, device_id_type=MESH))
    return mine, copies


def _scatter_start(b_refs, out_refs, send_sems, recv_sems, local_sems):
    mine, copies = _scatter_plan(b_refs, out_refs, send_sems, recv_sems, local_sems)
    for cp in mine + copies:
        cp.start()


def _scatter_finish(b_refs, out_refs, send_sems, recv_sems, local_sems):
    mine, copies = _scatter_plan(b_refs, out_refs, send_sems, recv_sems, local_sems)
    for cp in copies:
        cp.wait_recv()
    for cp in copies:
        cp.wait_send()
    for cp in mine:
        cp.wait()


def _exchange_operands(kind, tensors):
    if kind == "gather":
        args = list(tensors)
        shapes = [jax.ShapeDtypeStruct((N_DEV,) + t.shape, t.dtype) for t in tensors]
        return args, shapes, (_gather_start, _gather_forward, _gather_finish)
    args = [t.reshape(N_DEV, t.shape[0] // N_DEV, t.shape[1]) for t in tensors]
    shapes = [jax.ShapeDtypeStruct(a.shape, a.dtype) for a in args]
    return args, shapes, (_scatter_start, None, _scatter_finish)


def _exchange_results(kind, tensors, res):
    if kind == "gather":
        return [r.reshape(N_DEV * t.shape[0], t.shape[1]) for r, t in zip(res, tensors)]
    return list(res)


def _exchange_sems(n):
    return [pltpu.SemaphoreType.DMA((7 * n,)), pltpu.SemaphoreType.DMA((7 * n,)), pltpu.SemaphoreType.DMA((n,))]


def _exchange(kind, tensors, name):
    n = len(tensors)
    args, shapes, phases = _exchange_operands(kind, tensors)

    def body(*refs):
        for phase in phases:
            if phase is not None:
                phase(refs[:n], refs[n:2 * n], *refs[2 * n:])

    hbm = pl.BlockSpec(memory_space=pl.ANY)
    res = pl.pallas_call(body, name=name, out_shape=shapes, in_specs=[hbm] * n, out_specs=[hbm] * n,
                         scratch_shapes=_exchange_sems(n))(*args)
    return _exchange_results(kind, tensors, res)


def _pcall(body, args, *, name, out_shape, grid, in_specs, out_specs, sem, scratch_shapes=(), carry=None):
    if carry is None:
        out = pl.pallas_call(body, name=name, out_shape=out_shape, grid=grid, in_specs=list(in_specs),
                             out_specs=out_specs, scratch_shapes=list(scratch_shapes),
                             compiler_params=_params(*sem))(*args)
        return out, None
    kind, tensors = carry
    multi = isinstance(out_shape, (tuple, list))
    shapes = list(out_shape) if multi else [out_shape]
    ospecs = list(out_specs) if multi else [out_specs]
    n_in, n_out, n_scr, n_c = len(in_specs), len(shapes), len(scratch_shapes), len(tensors)
    c_args, c_shapes, (start, forward, finish) = _exchange_operands(kind, tensors)
    n_steps = 1
    for g in grid:
        n_steps *= g
    late = (3 * n_steps) // 4

    def wrapped(*refs):
        ins, rest = refs[:n_in], refs[n_in:]
        c_in, rest = rest[:n_c], rest[n_c:]
        outs, rest = rest[:n_out], rest[n_out:]
        c_out, rest = rest[:n_c], rest[n_c:]
        scr, sems = rest[:n_scr], rest[n_scr:]
        step = pl.program_id(0)
        for a in range(1, len(grid)):
            step = step * grid[a] + pl.program_id(a)

        @pl.when(step == 0)
        def _():
            start(c_in, c_out, *sems)

        body(*ins, *outs, *scr)

        if forward is not None:
            @pl.when(step == late)
            def _():
                forward(c_in, c_out, *sems)

        @pl.when(step == n_steps - 1)
        def _():
            finish(c_in, c_out, *sems)

    hbm = pl.BlockSpec(memory_space=pl.ANY)
    res = pl.pallas_call(
        wrapped, name=name, out_shape=shapes + c_shapes, grid=grid,
        in_specs=list(in_specs) + [hbm] * n_c, out_specs=ospecs + [hbm] * n_c,
        scratch_shapes=list(scratch_shapes) + _exchange_sems(n_c),
        compiler_params=_params(*sem))(*args, *c_args)
    outs = tuple(res[:n_out]) if multi else res[0]
    return outs, _exchange_results(kind, tensors, res[n_out:])


def _sum8(parts, name):
    _, R, C = parts.shape
    tr = _tile(R, 256, 16)

    def body(p_ref, g_ref):
        g = p_ref[0].astype(F32)
        for s in range(1, N_DEV):
            g = g + p_ref[s].astype(F32)
        g_ref[...] = g

    return pl.pallas_call(
        body, name=name, out_shape=jax.ShapeDtypeStruct((R, C), F32),
        grid=(R // tr,),
        in_specs=[pl.BlockSpec((N_DEV, tr, C), lambda i: (0, i, 0))],
        out_specs=pl.BlockSpec((tr, C), lambda i: (i, 0)),
        compiler_params=_params("parallel"),
    )(parts)


def _adamw(g, w, m, v, name):
    R, C = g.shape
    tr = _tile(R, 256, 8)
    c1 = 1.0 - ADAM_B1 ** ADAM_STEP
    c2 = 1.0 - ADAM_B2 ** ADAM_STEP

    def body(g_ref, w_ref, m_ref, v_ref, d_ref, nm_ref, nv_ref):
        gg = g_ref[...]
        nm = ADAM_B1 * m_ref[...] + (1.0 - ADAM_B1) * gg
        nv = ADAM_B2 * v_ref[...] + (1.0 - ADAM_B2) * (gg * gg)
        m_hat = nm / c1
        v_hat = nv / c2
        nm_ref[...] = nm
        nv_ref[...] = nv
        d_ref[...] = -ADAM_LR * (m_hat / (jnp.sqrt(v_hat) + ADAM_EPS) + ADAM_WD * w_ref[...])

    row = pl.BlockSpec((tr, C), lambda i: (i, 0))
    shp = jax.ShapeDtypeStruct((R, C), F32)
    return pl.pallas_call(
        body, name=name, out_shape=(shp, shp, shp),
        grid=(R // tr,), in_specs=[row, row, row, row], out_specs=(row, row, row),
        compiler_params=_params("parallel"),
    )(g, w, m, v)


def _ffn_down(act, wo, h, tag):
    return _mm(act, wo, NN, F32, f"{tag}_down", scale=FFN_RES_SCALE, res=h, tm=512, tn=1024, tk=2816)


def _ffn_fwd(h, g, win_t, wo, tag, carry=None, loss=None):
    return _ffn_fwd_fused(h, g, win_t, wo, f"{tag}_fwd", carry=carry, loss=loss)


def _ffn_bwd(dh, h, g, win_t, wo, saved, tag, scatter=False, carry=None, carry_dwin=None):
    xn, silu, dsilu, up, act = saved
    dwo = _mm(act, dh, TN, BF16, f"{tag}_dwo", scale=FFN_RES_SCALE, tm=1408, tn=1024, tk=TN_CHUNK)
    if not scatter:
        (dh_in, dg, dgate, dup), got = _ffn_bwd_fused(dh, h, g, win_t, wo, silu, dsilu, up, f"{tag}_bwd", carry=carry)
        dwin_t, got_dwin = _dw_rows([dgate, dup], xn, f"{tag}_dwin", carry=carry_dwin)
        return dh_in, dg, dwin_t, dwo, got, got_dwin
    dgate, dup = _ffn_dact(dh, wo, silu, dsilu, up, f"{tag}_dact")
    dwin_t, got_wo = _dw_rows([dgate, dup], xn, f"{tag}_dwin", carry=("scatter", [dwo]))
    (dh_in, dg), got_win = _dx_norm_bwd([(dgate, win_t, NN, 2, 0), (dup, win_t, NN, 2, 1)], h, g, dh, f"{tag}_dx",
                                        carry=("scatter", [dwin_t]))
    return dh_in, dg, got_win[0], got_wo[0]


def _proj(a, w, dims, out_dtype, name, res=None):
    return _mm(a, w, dims, out_dtype, name, res=res, tm=1024, tn=1024, tk=1024)


def _proj_dw(x, dy, name):
    return _mm(x, dy, TN, BF16, name, tm=1024, tn=1024, tk=TN_CHUNK)


def kernel(x, ffn1_norm, ffn1_w_in, ffn1_w_out, mix_norm, ffn2_norm, ffn2_w_in, ffn2_w_out, sb_w_qkv, sb_w_o, kv_norm, kv_w, swa_w_q, swa_sinks, swa_w_o, final_norm, loss_target, m_ffn1_norm, m_ffn1_w_in, m_ffn1_w_out, m_mix_norm, m_ffn2_norm, m_ffn2_w_in, m_ffn2_w_out, m_sb_w_qkv, m_sb_w_o, m_kv_norm, m_kv_w, m_swa_w_q, m_swa_sinks, m_swa_w_o, m_final_norm, v_ffn1_norm, v_ffn1_w_in, v_ffn1_w_out, v_mix_norm, v_ffn2_norm, v_ffn2_w_in, v_ffn2_w_out, v_sb_w_qkv, v_sb_w_o, v_kv_norm, v_kv_w, v_swa_w_q, v_swa_sinks, v_swa_w_o, v_final_norm):
    S, D = x.shape[1], x.shape[2]
    L = ffn1_w_in.shape[0]
    KV = kv_w.shape[1]
    assert L == 2 and swa_sinks.shape == (1, 2 * SWA_Q_GROUPS * KV // (2 * LANES))

    def bf(w):
        return w.astype(BF16)

    def bft(w):
        return jnp.transpose(w).astype(BF16)

    cos_t, sin_t = _rope_tables(S)
    h0 = x.reshape(S, D)
    tgt = loss_target.reshape(S, D)

    win1a_t, = _exchange("gather", [bft(ffn1_w_in[0])], "gather_first_weight")
    sv_a1, (wo1a, wqkv_t, w_sbo) = _ffn_up(
        h0, ffn1_norm[0], win1a_t, "ffn1a_up",
        carry=("gather", [bf(ffn1_w_out[0]), bft(sb_w_qkv[0]), bf(sb_w_o[0])]))
    h1 = _ffn_down(sv_a1[-1], wo1a, h0, "ffn1a")
    hn_a, qkv, kv_t = _norm_proj(h1, mix_norm[0], wqkv_t, NT, "sb_qkv", tail_t=2 * D)
    o_sb, (win2a_t, wo2a, w_kv) = _sb_fwd(qkv, kv_t, "sb_attn", carry=("gather", [
        bft(ffn2_w_in[0]), bf(ffn2_w_out[0]), bf(kv_w)]))
    h2 = _proj(o_sb, w_sbo, NN, F32, "sb_out", res=h1)
    h3, sv_a2, (win1b_t, wo1b, w_q, w_swo) = _ffn_fwd(h2, ffn2_norm[0], win2a_t, wo2a, "ffn2a", carry=("gather", [
        bft(ffn1_w_in[1]), bf(ffn1_w_out[1]), bf(swa_w_q[0]), bf(swa_w_o[0])]))
    kvn, kv_rot, kv_rot_t = _norm_proj(h3, kv_norm, w_kv, NN, "kv_proj", rope=(cos_t, sin_t, KV // (2 * LANES)),
                                       tail_t=KV)
    h4, sv_b1, (win2b_t, wo2b) = _ffn_fwd(h3, ffn1_norm[1], win1b_t, wo1b, "ffn1b", carry=("gather", [
        bft(ffn2_w_in[1]), bf(ffn2_w_out[1])]))
    hn_b, q_rot = _norm_proj(h4, mix_norm[1], w_q, NN, "swa_q", rope=(cos_t, sin_t, D // LANES))
    o_sw = _swa_fwd(q_rot, kv_rot, kv_rot_t, swa_sinks, "swa_attn")
    h5 = _proj(o_sw, w_swo, NN, F32, "swa_out", res=h4)
    (dh6, dg_final, sq_err), sv_b2, _ = _ffn_fwd(h5, ffn2_norm[1], win2b_t, wo2b, "ffn2b", loss=(final_norm, tgt))
    loss = lax.psum(0.5 * jnp.sum(sq_err) / D, ("x", "y", "c"))

    dh5, dg_f2b, dwin2b_t, dwo2b, _, _ = _ffn_bwd(dh6, h5, ffn2_norm[1], win2b_t, wo2b, sv_b2, "ffn2b")
    do_sw = _proj(dh5, w_swo, NT, BF16, "swa_out_dx")
    dw_swo = _proj_dw(o_sw, dh5, "swa_out_dw")
    (dq, dk_sw, dv_sw, dsink), (p_win2b, p_swo) = _swa_bwd(
        q_rot, kv_rot, kv_rot_t, swa_sinks, o_sw, do_sw, cos_t, sin_t, "swa_attn_bwd",
        carry=("scatter", [dwin2b_t, dw_swo]))
    dw_q = _proj_dw(hn_b, dq, "swa_q_dw")
    (dh4, dg_mix_b), _ = _dx_norm_bwd([(dq, w_q, NT, 1, 0)], h4, mix_norm[1], dh5, "swa_q_dx", tm=512)
    dh3, dg_f1b, dwin1b_t, dwo1b, (p_q, p_wo2b), _ = _ffn_bwd(dh4, h3, ffn1_norm[1], win1b_t, wo1b, sv_b1, "ffn1b",
                                                              carry=("scatter", [dw_q, dwo2b]))
    dkv = _rotary(jnp.concatenate([dk_sw, dv_sw], axis=1), cos_t, sin_t, KV // (2 * LANES), True, "kv_rope_bwd")
    dw_kv = _proj_dw(kvn, dkv, "kv_proj_dw")
    (dh3, dg_kv), _ = _dx_norm_bwd([(dkv, w_kv, NT, 1, 0)], h3, kv_norm, dh3, "kv_proj_dx", tm=512)
    dh2, dg_f2a, dwin2a_t, dwo2a, (p_win1b, p_kv), (p_wo1b,) = _ffn_bwd(
        dh3, h2, ffn2_norm[0], win2a_t, wo2a, sv_a2, "ffn2a",
        carry=("scatter", [dwin1b_t, dw_kv]), carry_dwin=("scatter", [dwo1b]))
    do_sb = _proj(dh2, w_sbo, NT, BF16, "sb_out_dx")
    dw_sbo = _proj_dw(o_sb, dh2, "sb_out_dw")
    (dq_sb, dk_sb, dv_sb), (p_win2a, p_wo2a, p_sbo) = _sb_bwd(
        qkv, kv_t, o_sb, do_sb, "sb_attn_bwd", carry=("scatter", [dwin2a_t, dwo2a, dw_sbo]))
    dqkv = [dq_sb, dk_sb, dv_sb]
    dwqkv_t, _ = _dw_rows(dqkv, hn_a, "sb_qkv_dw", tk=TN_CHUNK // 2)
    (dh1, dg_mix_a), (p_qkv,) = _dx_norm_bwd([(dy, wqkv_t, NN, 3, n) for n, dy in enumerate(dqkv)], h1, mix_norm[0],
                                             dh2, "sb_qkv_dx", carry=("scatter", [dwqkv_t]), tm=512)
    dx, dg_f1a, p_win1a, p_wo1a = _ffn_bwd(dh1, h0, ffn1_norm[0], win1a_t, wo1a, sv_a1, "ffn1a", scatter=True)

    def natural(parts, tag):
        return _sum8(parts, f"sum_{tag}")

    def from_t(parts, tag):
        return jnp.transpose(_sum8(parts, f"sum_{tag}"))

    grads = {
        "ffn1_w_in": jnp.stack([from_t(p_win1a, "win1a"), from_t(p_win1b, "win1b")]),
        "ffn1_w_out": jnp.stack([natural(p_wo1a, "wo1a"), natural(p_wo1b, "wo1b")]),
        "ffn2_w_in": jnp.stack([from_t(p_win2a, "win2a"), from_t(p_win2b, "win2b")]),
        "ffn2_w_out": jnp.stack([natural(p_wo2a, "wo2a"), natural(p_wo2b, "wo2b")]),
        "sb_w_qkv": from_t(p_qkv, "qkv")[None],
        "sb_w_o": natural(p_sbo, "sbo")[None],
        "kv_w": natural(p_kv, "kv"),
        "swa_w_q": natural(p_q, "swq")[None],
        "swa_w_o": natural(p_swo, "swo")[None],
    }

    small_w = [ffn1_norm, mix_norm, ffn2_norm, kv_norm, final_norm, swa_sinks]
    small_m = [m_ffn1_norm, m_mix_norm, m_ffn2_norm, m_kv_norm, m_final_norm, m_swa_sinks]
    small_v = [v_ffn1_norm, v_mix_norm, v_ffn2_norm, v_kv_norm, v_final_norm, v_swa_sinks]
    SMALL_ROWS = 16

    def pack_small(ts):
        rows_ = [t.reshape(-1, D) for t in ts[:-1]]
        sink_row = jnp.pad(ts[-1].reshape(1, -1), ((0, 0), (0, D - ts[-1].size)))
        flat = jnp.concatenate(rows_ + [sink_row], axis=0)
        return jnp.pad(flat, ((0, SMALL_ROWS - flat.shape[0]), (0, 0)))

    def unpack_small(flat):
        out, r = [], 0
        for t in small_w[:-1]:
            n = t.size // D
            out.append(flat[r:r + n].reshape(t.shape))
            r += n
        out.append(flat[r, :swa_sinks.size].reshape(swa_sinks.shape))
        return out

    def gain(parts8):
        return jnp.sum(parts8, axis=0, keepdims=True)

    g_small_local = pack_small([
        jnp.concatenate([gain(dg_f1a), gain(dg_f1b)], axis=0),
        jnp.concatenate([gain(dg_mix_a), gain(dg_mix_b)], axis=0),
        jnp.concatenate([gain(dg_f2a), gain(dg_f2b)], axis=0),
        gain(dg_kv), gain(dg_final), jnp.sum(dsink, axis=-1).reshape(1, -1)])
    small_parts = _exchange("gather", [g_small_local], "gather_small_grads")[0]
    g_small = _sum8(small_parts.reshape(N_DEV, SMALL_ROWS, D), "sum_small")
    d_small, nm_small, nv_small = _adamw(g_small, pack_small(small_w), pack_small(small_m), pack_small(small_v), "adamw_small")
    small_names = ["ffn1_norm", "mix_norm", "ffn2_norm", "kv_norm", "final_norm", "swa_sinks"]
    result = {"grad": dict(zip(small_names, unpack_small(g_small))),
              "delta": dict(zip(small_names, unpack_small(d_small))),
              "new_m": dict(zip(small_names, unpack_small(nm_small))),
              "new_v": dict(zip(small_names, unpack_small(nv_small)))}

    big = {"ffn1_w_in": (ffn1_w_in, m_ffn1_w_in, v_ffn1_w_in), "ffn1_w_out": (ffn1_w_out, m_ffn1_w_out, v_ffn1_w_out),
           "ffn2_w_in": (ffn2_w_in, m_ffn2_w_in, v_ffn2_w_in), "ffn2_w_out": (ffn2_w_out, m_ffn2_w_out, v_ffn2_w_out),
           "sb_w_qkv": (sb_w_qkv, m_sb_w_qkv, v_sb_w_qkv), "sb_w_o": (sb_w_o, m_sb_w_o, v_sb_w_o),
           "kv_w": (kv_w, m_kv_w, v_kv_w), "swa_w_q": (swa_w_q, m_swa_w_q, v_swa_w_q),
           "swa_w_o": (swa_w_o, m_swa_w_o, v_swa_w_o)}
    for nm, (w, m, v) in big.items():
        g = grads[nm]
        two_d = lambda t: t.reshape(-1, t.shape[-1])
        d, new_m, new_v = _adamw(two_d(g), two_d(w), two_d(m), two_d(v), f"adamw_{nm}")
        result["grad"][nm] = g
        result["delta"][nm] = d.reshape(w.shape)
        result["new_m"][nm] = new_m.reshape(w.shape)
        result["new_v"][nm] = new_v.reshape(w.shape)

    order = ["ffn1_norm", "ffn1_w_in", "ffn1_w_out", "mix_norm", "ffn2_norm", "ffn2_w_in", "ffn2_w_out",
             "sb_w_qkv", "sb_w_o", "kv_norm", "kv_w", "swa_w_q", "swa_sinks", "swa_w_o", "final_norm"]
    outs = [result[kind][nm] for kind in ("grad", "delta", "new_m", "new_v") for nm in order]
    return (loss, dx.reshape(x.shape), *outs)
```

```python
import jax
import jax.numpy as jnp
from jax import lax
from jax.experimental import pallas as pl
from jax.experimental.pallas import tpu as pltpu

F32 = jnp.float32
BF16 = jnp.bfloat16

N_DEV = 8
HEAD_DIM = 64
LANES = 128
BLK = 128
RMS_EPS = 1e-6
FFN_RES_SCALE = 0.5
ROPE_THETA = 10000.0
ATTN_SCALE = HEAD_DIM ** -0.5
SB_LOG_FLOOR = -88.0
NEG_BIG = -1e30
VMEM_LIMIT_V7X = 56 * 1024 * 1024

ADAM_LR = 0.001
ADAM_B1 = 0.9
ADAM_B2 = 0.999
ADAM_EPS = 1e-08
ADAM_WD = 0.01
ADAM_STEP = 10

NN = ((1,), (0,))
NT = ((1,), (1,))
TN = ((0,), (0,))
TN_CHUNK = 2048
MESH = pl.DeviceIdType.MESH


def _dot(a, b, dims):
    return lax.dot_general(a, b, (dims, ((), ())), preferred_element_type=F32)


def _tile(n, pref, mult=LANES):
    if n <= pref:
        return n
    t = (pref // mult) * mult
    while t >= mult:
        if n % t == 0:
            return t
        t -= mult
    return n


def _params(*sem):
    return pltpu.CompilerParams(dimension_semantics=sem, vmem_limit_bytes=VMEM_LIMIT_V7X)


def _mm(a, b, dims, out_dtype, name, scale=1.0, res=None, tm=512, tn=512, tk=512):
    if dims == NN:
        (M, K), (_, N) = a.shape, b.shape
    elif dims == NT:
        (M, K), (N, _) = a.shape, b.shape
    else:
        (K, M), (_, N) = a.shape, b.shape
    tm, tn, tk = _tile(M, tm), _tile(N, tn), _tile(K, tk)
    nk = K // tk
    if dims == TN:
        a_spec = pl.BlockSpec((tk, tm), lambda i, j, k: (k, i))
    else:
        a_spec = pl.BlockSpec((tm, tk), lambda i, j, k: (i, k))
    if dims == NT:
        b_spec = pl.BlockSpec((tn, tk), lambda i, j, k: (j, k))
    else:
        b_spec = pl.BlockSpec((tk, tn), lambda i, j, k: (k, j))
    o_spec = pl.BlockSpec((tm, tn), lambda i, j, k: (i, j))
    has_res = res is not None

    def body(*refs):
        a_ref, b_ref = refs[0], refs[1]
        r_ref = refs[2] if has_res else None
        o_ref = refs[3] if has_res else refs[2]

        def finish(acc):
            r = acc * scale if scale != 1.0 else acc
            if has_res:
                r = r + r_ref[...]
            o_ref[...] = r.astype(out_dtype)

        p = _dot(a_ref[...].astype(BF16), b_ref[...].astype(BF16), dims)
        if nk == 1:
            finish(p)
        else:
            acc_ref = refs[-1]
            k = pl.program_id(2)

            @pl.when(k == 0)
            def _():
                acc_ref[...] = p

            @pl.when(k > 0)
            def _():
                acc_ref[...] += p

            @pl.when(k == nk - 1)
            def _():
                finish(acc_ref[...])

    in_specs = [a_spec, b_spec] + ([o_spec] if has_res else [])
    args = (a, b) + ((res,) if has_res else ())
    return pl.pallas_call(
        body, name=name,
        out_shape=jax.ShapeDtypeStruct((M, N), out_dtype),
        grid=(M // tm, N // tn, nk),
        in_specs=in_specs, out_specs=o_spec,
        scratch_shapes=[pltpu.VMEM((tm, tn), F32)] if nk > 1 else [],
        compiler_params=_params("parallel", "parallel", "arbitrary"),
    )(*args)


def _rows8(x):
    r, d = x.shape
    return jnp.sum(x.reshape(r // 8, 8, d), axis=0)


def _norm_proj(h, g, w, dims, name, rope=None, tail_t=0):
    S, D = h.shape
    N = w.shape[1] if dims == NN else w.shape[0]
    tm = _tile(S, 512, 16)

    def body(h_ref, g_ref, w_ref, *rest):
        xn_ref, y_ref = rest[-3:-1] if tail_t else rest[-2:]
        x = h_ref[...]
        r = lax.rsqrt(jnp.mean(x * x, axis=-1, keepdims=True) + RMS_EPS)
        xn = ((x * r) * g_ref[...]).astype(BF16)
        xn_ref[...] = xn
        y = _dot(xn, w_ref[...], dims)
        if rope is not None:
            cs, sn = rest[0][...], rest[1][...]
            groups = [y[:, gidx * LANES:(gidx + 1) * LANES] for gidx in range(N // LANES)]
            y = jnp.concatenate([v * cs + _swap_halves(v) * sn if gidx < rope[2] else v
                                 for gidx, v in enumerate(groups)], axis=1)
        y_ref[...] = y.astype(BF16)
        if tail_t:
            rest[-1][...] = jnp.transpose(y[:, N - tail_t:]).astype(BF16)

    row = pl.BlockSpec((tm, D), lambda i: (i, 0))
    tab = pl.BlockSpec((tm, LANES), lambda i: (i, 0))
    in_specs = [row, pl.BlockSpec((1, D), lambda i: (0, 0)), pl.BlockSpec(w.shape, lambda i: (0, 0))]
    args = (h, g.reshape(1, D), w)
    if rope is not None:
        in_specs += [tab, tab]
        args += (rope[0], rope[1])
    out_shape = [jax.ShapeDtypeStruct((S, D), BF16), jax.ShapeDtypeStruct((S, N), BF16)]
    out_specs = [row, pl.BlockSpec((tm, N), lambda i: (i, 0))]
    if tail_t:
        out_shape.append(jax.ShapeDtypeStruct((tail_t, S), BF16))
        out_specs.append(pl.BlockSpec((tail_t, tm), lambda i: (0, i)))
    return pl.pallas_call(
        body, name=name, out_shape=out_shape, grid=(S // tm,),
        in_specs=in_specs, out_specs=out_specs,
        compiler_params=_params("parallel"),
    )(*args)


def _ffn_up(h, g, win_t, name, carry=None):
    S, D = h.shape
    F = win_t.shape[0] // 2
    tm = _tile(S, 256, 16)

    def body(h_ref, g_ref, win_hbm, xn_ref, silu_ref, dsilu_ref, up_ref, act_ref, win_v, sems):
        _load_resident([(win_hbm, win_v)], sems)
        x = h_ref[...]
        r = lax.rsqrt(jnp.mean(x * x, axis=-1, keepdims=True) + RMS_EPS)
        xn = ((x * r) * g_ref[...]).astype(BF16)
        xn_ref[...] = xn
        gate = _dot(xn, win_v[:F, :], NT)
        up = _dot(xn, win_v[F:, :], NT)
        sig = 1.0 / (1.0 + jnp.exp(-gate))
        silu = gate * sig
        up_ref[...] = up.astype(BF16)
        silu_ref[...] = silu.astype(BF16)
        dsilu_ref[...] = (sig + silu * (1.0 - sig)).astype(BF16)
        act_ref[...] = (silu * up).astype(BF16)

    row = pl.BlockSpec((tm, D), lambda i: (i, 0))
    wide = pl.BlockSpec((tm, F), lambda i: (i, 0))
    hid = jax.ShapeDtypeStruct((S, F), BF16)
    return _pcall(
        body, (h, g.reshape(1, D), win_t), name=name,
        out_shape=(jax.ShapeDtypeStruct((S, D), BF16), hid, hid, hid, hid),
        grid=(S // tm,),
        in_specs=[row, pl.BlockSpec((1, D), lambda i: (0, 0)), pl.BlockSpec(memory_space=pl.ANY)],
        out_specs=(row, wide, wide, wide, wide),
        scratch_shapes=[pltpu.VMEM(win_t.shape, BF16), pltpu.SemaphoreType.DMA((1,))],
        sem=("arbitrary",), carry=carry)


def _ffn_dact(dh, wo, silu, dsilu, up, name):
    S, D = dh.shape
    F = wo.shape[0]
    tm = _tile(S, 256, 16)

    def body(dh_ref, wo_hbm, s_ref, ds_ref, u_ref, dg_ref, du_ref, wo_v, sems):
        _load_resident([(wo_hbm, wo_v)], sems)
        d = _dot(dh_ref[...].astype(BF16), wo_v[...], NT) * FFN_RES_SCALE
        du_ref[...] = (d * s_ref[...].astype(F32)).astype(BF16)
        dg_ref[...] = (d * u_ref[...].astype(F32) * ds_ref[...].astype(F32)).astype(BF16)

    wide = pl.BlockSpec((tm, F), lambda i: (i, 0))
    hid = jax.ShapeDtypeStruct((S, F), BF16)
    return pl.pallas_call(
        body, name=name, out_shape=(hid, hid),
        grid=(S // tm,),
        in_specs=[pl.BlockSpec((tm, D), lambda i: (i, 0)), pl.BlockSpec(memory_space=pl.ANY), wide, wide, wide],
        out_specs=(wide, wide),
        scratch_shapes=[pltpu.VMEM(wo.shape, BF16), pltpu.SemaphoreType.DMA((1,))],
        compiler_params=_params("arbitrary"),
    )(dh, wo, silu, dsilu, up)


def _dw_rows(srcs, x, name, carry=None, tk=TN_CHUNK):
    n = len(srcs)
    S, F = srcs[0].shape
    D = x.shape[1]
    tr, tk = _tile(F, 1408), _tile(S, tk, 16)
    nf, nk = F // tr, S // tk

    def body(*refs):
        src_refs, (x_ref, o_ref, acc_ref) = refs[:n], refs[n:]
        r, k = pl.program_id(0), pl.program_id(1)
        for s in range(n):
            @pl.when(r // nf == s)
            def _():
                p = _dot(src_refs[s][...].astype(BF16), x_ref[...], TN)

                @pl.when(k == 0)
                def _():
                    acc_ref[...] = p

                @pl.when(k > 0)
                def _():
                    acc_ref[...] += p

        @pl.when(k == nk - 1)
        def _():
            o_ref[...] = acc_ref[...].astype(BF16)

    def src_spec(s):
        return pl.BlockSpec((tk, tr), lambda r, k: (jnp.where(r // nf == s, k, 0), jnp.clip(r - s * nf, 0, nf - 1)))

    return _pcall(
        body, (*srcs, x), name=name, out_shape=jax.ShapeDtypeStruct((n * F, D), BF16),
        grid=(n * nf, nk),
        in_specs=[src_spec(s) for s in range(n)] + [pl.BlockSpec((tk, D), lambda r, k: (k, 0))],
        out_specs=pl.BlockSpec((tr, D), lambda r, k: (r, 0)),
        scratch_shapes=[pltpu.VMEM((tr, D), F32)],
        sem=("arbitrary", "arbitrary"), carry=carry)


def _dx_norm_bwd(terms, h, g, res, name, carry=None, tm=256):
    S, D = h.shape
    tm = _tile(S, tm, 16)
    n = len(terms)

    def body(*refs):
        dy_refs, w_refs = refs[:n], refs[n:2 * n]
        h_ref, g_ref, r_ref, dh_ref, dg_ref = refs[2 * n:]
        d = _dot(dy_refs[0][...].astype(BF16), w_refs[0][...], terms[0][2])
        for t in range(1, n):
            d = d + _dot(dy_refs[t][...].astype(BF16), w_refs[t][...], terms[t][2])
        x = h_ref[...]
        r = lax.rsqrt(jnp.mean(x * x, axis=-1, keepdims=True) + RMS_EPS)
        xhat = x * r
        dxh = d * g_ref[...]
        c = jnp.mean(dxh * xhat, axis=-1, keepdims=True)
        dh_ref[...] = r * (dxh - xhat * c) + r_ref[...]
        part = _rows8(d * xhat)

        @pl.when(pl.program_id(0) == 0)
        def _():
            dg_ref[...] = part

        @pl.when(pl.program_id(0) > 0)
        def _():
            dg_ref[...] += part

    def w_spec(w, nblk, blk):
        return pl.BlockSpec((w.shape[0] // nblk, w.shape[1]), lambda i: (blk, 0))

    row = pl.BlockSpec((tm, D), lambda i: (i, 0))
    in_specs = [pl.BlockSpec((tm, t[0].shape[1]), lambda i: (i, 0)) for t in terms]
    in_specs += [w_spec(t[1], t[3], t[4]) for t in terms]
    in_specs += [row, pl.BlockSpec((1, D), lambda i: (0, 0)), row]
    return _pcall(
        body, (*[t[0] for t in terms], *[t[1] for t in terms], h, g.reshape(1, D), res), name=name,
        out_shape=(jax.ShapeDtypeStruct((S, D), F32), jax.ShapeDtypeStruct((8, D), F32)),
        grid=(S // tm,),
        in_specs=in_specs,
        out_specs=(row, pl.BlockSpec((8, D), lambda i: (0, 0))),
        sem=("arbitrary",), carry=carry)


def _load_resident(pairs, sems):
    @pl.when(pl.program_id(0) == 0)
    def _():
        copies = [pltpu.make_async_copy(src, dst, sems.at[n]) for n, (src, dst) in enumerate(pairs)]
        for cp in copies:
            cp.start()
        for cp in copies:
            cp.wait()


def _loss_tail(y_in, g, tgt):
    D = y_in.shape[-1]
    r = lax.rsqrt(jnp.mean(y_in * y_in, axis=-1, keepdims=True) + RMS_EPS)
    xhat = y_in * r
    err = xhat * g - tgt
    d = err * (1.0 / D)
    dxh = d * g
    c = jnp.mean(dxh * xhat, axis=-1, keepdims=True)
    return r * (dxh - xhat * c), _rows8(d * xhat), _rows8(err * err)


def _ffn_fwd_fused(h, g, win_t, wo, name, carry=None, loss=None):
    S, D = h.shape
    F = wo.shape[0]
    tm = _tile(S, 256, 16)
    n_head = 3 if loss is not None else 1

    def body(h_ref, g_ref, win_hbm, wo_hbm, *rest):
        lead, (xn_ref, silu_ref, dsilu_ref, up_ref, act_ref, win_v, wo_v, sems) = rest[:-8], rest[-8:]
        _load_resident([(win_hbm, win_v), (wo_hbm, wo_v)], sems)
        x = h_ref[...]
        r = lax.rsqrt(jnp.mean(x * x, axis=-1, keepdims=True) + RMS_EPS)
        xn = ((x * r) * g_ref[...]).astype(BF16)
        xn_ref[...] = xn
        gate = _dot(xn, win_v[:F, :], NT)
        up = _dot(xn, win_v[F:, :], NT)
        sig = 1.0 / (1.0 + jnp.exp(-gate))
        silu = gate * sig
        act = (silu * up).astype(BF16)
        up_ref[...] = up.astype(BF16)
        silu_ref[...] = silu.astype(BF16)
        dsilu_ref[...] = (sig + silu * (1.0 - sig)).astype(BF16)
        act_ref[...] = act
        out = x + FFN_RES_SCALE * _dot(act, wo_v[...], NN)
        if loss is None:
            lead[0][...] = out
        else:
            gf_ref, t_ref, dy_ref, dgf_ref, sq_ref = lead
            dy, dgf, sq = _loss_tail(out, gf_ref[...], t_ref[...])
            dy_ref[...] = dy

            @pl.when(pl.program_id(0) == 0)
            def _():
                dgf_ref[...] = dgf
                sq_ref[...] = sq

            @pl.when(pl.program_id(0) > 0)
            def _():
                dgf_ref[...] += dgf
                sq_ref[...] += sq

    row = pl.BlockSpec((tm, D), lambda i: (i, 0))
    vec = pl.BlockSpec((1, D), lambda i: (0, 0))
    acc = pl.BlockSpec((8, D), lambda i: (0, 0))
    wide = pl.BlockSpec((tm, F), lambda i: (i, 0))
    hbm = pl.BlockSpec(memory_space=pl.ANY)
    hid = jax.ShapeDtypeStruct((S, F), BF16)
    full = jax.ShapeDtypeStruct((S, D), F32)
    part = jax.ShapeDtypeStruct((8, D), F32)
    args, in_specs = (h, g.reshape(1, D), win_t, wo), [row, vec, hbm, hbm]
    lead_shapes, lead_specs = (full,), (row,)
    if loss is not None:
        args, in_specs = args + (loss[0].reshape(1, D), loss[1]), in_specs + [vec, row]
        lead_shapes, lead_specs = (full, part, part), (row, acc, acc)
    res, got = _pcall(
        body, args, name=name,
        out_shape=lead_shapes + (jax.ShapeDtypeStruct((S, D), BF16), hid, hid, hid, hid),
        grid=(S // tm,),
        in_specs=in_specs,
        out_specs=lead_specs + (row, wide, wide, wide, wide),
        scratch_shapes=[pltpu.VMEM(win_t.shape, BF16), pltpu.VMEM(wo.shape, BF16), pltpu.SemaphoreType.DMA((2,))],
        sem=("arbitrary",), carry=carry)
    first = res[0] if loss is None else tuple(res[:3])
    return first, tuple(res[n_head:]), got


def _ffn_bwd_fused(dh, h, g, win_t, wo, silu, dsilu, up, name, carry=None):
    S, D = h.shape
    F = wo.shape[0]
    tm = _tile(S, 256, 16)

    def body(dh_ref, h_ref, g_ref, s_ref, ds_ref, u_ref, win_hbm, wo_hbm,
             dhin_ref, dgain_ref, dgate_ref, dup_ref, win_v, wo_v, sems):
        _load_resident([(win_hbm, win_v), (wo_hbm, wo_v)], sems)
        dhv = dh_ref[...]
        d = _dot(dhv.astype(BF16), wo_v[...], NT) * FFN_RES_SCALE
        dup = (d * s_ref[...].astype(F32)).astype(BF16)
        dgate = (d * u_ref[...].astype(F32) * ds_ref[...].astype(F32)).astype(BF16)
        dup_ref[...] = dup
        dgate_ref[...] = dgate
        dxn = _dot(dgate, win_v[:F, :], NN) + _dot(dup, win_v[F:, :], NN)
        x = h_ref[...]
        r = lax.rsqrt(jnp.mean(x * x, axis=-1, keepdims=True) + RMS_EPS)
        xhat = x * r
        dxh = dxn * g_ref[...]
        c = jnp.mean(dxh * xhat, axis=-1, keepdims=True)
        dhin_ref[...] = r * (dxh - xhat * c) + dhv
        part = _rows8(dxn * xhat)

        @pl.when(pl.program_id(0) == 0)
        def _():
            dgain_ref[...] = part

        @pl.when(pl.program_id(0) > 0)
        def _():
            dgain_ref[...] += part

    row = pl.BlockSpec((tm, D), lambda i: (i, 0))
    wide = pl.BlockSpec((tm, F), lambda i: (i, 0))
    hbm = pl.BlockSpec(memory_space=pl.ANY)
    hid = jax.ShapeDtypeStruct((S, F), BF16)
    return _pcall(
        body, (dh, h, g.reshape(1, D), silu, dsilu, up, win_t, wo), name=name,
        out_shape=(jax.ShapeDtypeStruct((S, D), F32), jax.ShapeDtypeStruct((8, D), F32), hid, hid),
        grid=(S // tm,),
        in_specs=[row, row, pl.BlockSpec((1, D), lambda i: (0, 0)), wide, wide, wide, hbm, hbm],
        out_specs=(row, pl.BlockSpec((8, D), lambda i: (0, 0)), wide, wide),
        scratch_shapes=[pltpu.VMEM(win_t.shape, BF16), pltpu.VMEM(wo.shape, BF16), pltpu.SemaphoreType.DMA((2,))],
        sem=("arbitrary",), carry=carry)


def _rope_tables(S):
    half = HEAD_DIM // 2
    inv_freq = ROPE_THETA ** (-jnp.arange(half, dtype=F32) / half)
    ang = jnp.arange(S).astype(F32)[:, None] * inv_freq[None, :]
    cos, sin = jnp.cos(ang), jnp.sin(ang)
    cos_t = jnp.tile(cos, (1, LANES // half))
    sin_t = jnp.tile(jnp.concatenate([-sin, sin], axis=1), (1, LANES // HEAD_DIM))
    return cos_t, sin_t


def _swap_halves(x):
    lane = lax.broadcasted_iota(jnp.int32, x.shape, 1)
    first = (lane % HEAD_DIM) < (HEAD_DIM // 2)
    return jnp.where(first, pltpu.roll(x, LANES - HEAD_DIM // 2, 1), pltpu.roll(x, HEAD_DIM // 2, 1))


def _rotary(x, cos_t, sin_t, n_rot, inverse, name):
    S, C = x.shape
    ts = _tile(S, 512, 16)
    ng = C // LANES

    def body(x_ref, c_ref, s_ref, o_ref):
        cs, sn = c_ref[...], s_ref[...]
        for gidx in range(ng):
            sl = slice(gidx * LANES, (gidx + 1) * LANES)
            v = x_ref[:, sl].astype(F32)
            if gidx < n_rot:
                if inverse:
                    v = v * cs + _swap_halves(v * sn)
                else:
                    v = v * cs + _swap_halves(v) * sn
            o_ref[:, sl] = v.astype(BF16)

    row = pl.BlockSpec((ts, C), lambda i: (i, 0))
    tab = pl.BlockSpec((ts, LANES), lambda i: (i, 0))
    return pl.pallas_call(
        body, name=name, out_shape=jax.ShapeDtypeStruct((S, C), BF16),
        grid=(S // ts,), in_specs=[row, tab, tab], out_specs=row,
        compiler_params=_params("parallel"),
    )(x, cos_t, sin_t)


def _head_masks():
    lane = lax.broadcasted_iota(jnp.int32, (BLK, LANES), 1)
    return lane < HEAD_DIM


def _split_bf16(x):
    hi = x.astype(BF16)
    lo = (x - hi.astype(F32)).astype(BF16)
    return hi, lo


def _sb_scores(qh, ks, carry, diag, tri_excl, strict):
    n_heads = len(qh)
    zs = [_dot(ks[n], qh[n], NT) for n in range(n_heads)]
    a_l, b_l, split_l = [], [], []
    for z in zs:
        a = jnp.minimum(z, 0.0) - jnp.log(1.0 + jnp.exp(-jnp.abs(z)))
        b = a - z
        if diag:
            b = jnp.where(strict, b, 0.0)
        a_l.append(a)
        b_l.append(b)
        split_l.append(_split_bf16(b))
    sufs = [_dot(tri_excl, hi, NN) + _dot(tri_excl, lo, NN) for hi, lo in split_l]
    w_l = []
    for n in range(n_heads):
        w = jnp.exp(a_l[n] + sufs[n] + carry[n])
        if diag:
            w = jnp.where(strict, w, 0.0)
        w_l.append(w)
    return a_l, b_l, w_l


SB_FWD_PAIRS = 4
SB_FWD_QBLOCKS = 4
SB_BWD_PAIRS = 2
SB_BWD_QBLOCKS = 4


def _any_alive(carries):
    top = carries[0]
    for c in carries[1:]:
        top = jnp.maximum(top, c)
    return (jnp.max(top) > SB_LOG_FLOOR).astype(jnp.int32)


def _sb_masks():
    row = lax.broadcasted_iota(jnp.int32, (BLK, BLK), 0)
    col = lax.broadcasted_iota(jnp.int32, (BLK, BLK), 1)
    tri_excl = jnp.where(col > row, 1.0, 0.0).astype(BF16)
    tri_incl = jnp.where(col >= row, 1.0, 0.0).astype(BF16)
    return row < HEAD_DIM, row < col, tri_excl, tri_incl


def _sb_fwd(qkv, kv_t, name, carry=None):
    S, D3 = qkv.shape
    D = D3 // 3
    npair, nb = D // LANES, S // BLK
    P = min(SB_FWD_PAIRS, npair)
    ngroup = npair // P
    W = P * LANES

    QB = SB_FWD_QBLOCKS if nb % SB_FWD_QBLOCKS == 0 else 1
    nch = QB * 2 * P

    def body(q_ref, k_ref, vt_ref, o_ref):
        i_first = pl.program_id(1) * QB
        m0 = _head_masks()
        top, strict, tri_excl, _ = _sb_masks()
        zq = jnp.zeros((BLK, LANES), BF16)
        lanes = [slice(p * LANES, (p + 1) * LANES) for p in range(P)]
        qh = []
        for qb in range(QB):
            for sl in lanes:
                q2 = q_ref[qb * BLK:(qb + 1) * BLK, sl] * ATTN_SCALE
                qh += [jnp.where(m0, q2, zq), jnp.where(m0, zq, q2)]

        def block(qbs, js, carry, acc, diag):
            offs = [pl.multiple_of(j * BLK, BLK) for j in js]
            ks, vth, qs = [], [], []
            for n_qb, qb in enumerate(qbs):
                qs += qh[qb * 2 * P:(qb + 1) * 2 * P]
                for sl in lanes:
                    k2 = k_ref[pl.ds(offs[n_qb], BLK), sl]
                    vt = vt_ref[sl, pl.ds(offs[n_qb], BLK)]
                    ks += [k2, k2]
                    vth += [jnp.where(top, vt, zq), jnp.where(top, zq, vt)]
            _, b_l, w_l = _sb_scores(qs, ks, carry, diag, tri_excl, strict)
            wb = [w.astype(BF16) for w in w_l]
            new_acc = [acc[m] + _dot(vth[2 * m], wb[2 * m], NN) + _dot(vth[2 * m + 1], wb[2 * m + 1], NN)
                       for m in range(len(qbs) * P)]
            new_carry = [carry[n] + jnp.sum(b_l[n], axis=0, keepdims=True) for n in range(len(carry))]
            return new_carry, new_acc

        every = list(range(QB))
        c0 = jnp.zeros((1, BLK), F32)
        carry, acc = block(every, [i_first + qb for qb in every], [c0] * nch,
                           [jnp.zeros((LANES, BLK), F32)] * (QB * P), True)
        carry = [jnp.where(i_first > 0, c, NEG_BIG) for c in carry[:2 * P]] + carry[2 * P:]
        carry, acc = block(every, [jnp.maximum(i_first + qb - 1, 0) for qb in every], carry, acc, False)

        for qb in range(QB):
            i_qb = i_first + qb
            sub = slice(qb * 2 * P, (qb + 1) * 2 * P)

            def cond(st):
                return jnp.logical_and(i_qb - st[0] >= 0, st[1] > 0)

            def step(st, qb=qb, i_qb=i_qb):
                t, _, c_qb, a_qb = st
                c_qb, a_qb = block([qb], [i_qb - t], c_qb, a_qb, False)
                return t + 1, _any_alive(c_qb), c_qb, a_qb

            st = lax.while_loop(cond, step, (2, _any_alive(carry[sub]), carry[sub], acc[qb * P:(qb + 1) * P]))
            for p, sl in enumerate(lanes):
                o_ref[qb * BLK:(qb + 1) * BLK, sl] = jnp.transpose(st[3][p])

    return _pcall(
        body, (qkv, qkv, kv_t), name=name, out_shape=jax.ShapeDtypeStruct((S, D), F32),
        grid=(ngroup, nb // QB),
        in_specs=[pl.BlockSpec((QB * BLK, W), lambda g, i: (i, g)),
                  pl.BlockSpec((S, W), lambda g, i: (0, ngroup + g)),
                  pl.BlockSpec((W, S), lambda g, i: (ngroup + g, 0))],
        out_specs=pl.BlockSpec((QB * BLK, W), lambda g, i: (i, g)),
        sem=("arbitrary", "arbitrary"), carry=carry)


def _sb_bwd(qkv, kv_t, o, do, name, carry=None):
    S, D3 = qkv.shape
    D = D3 // 3
    npair, nb = D // LANES, S // BLK
    P = min(SB_BWD_PAIRS, npair)
    ngroup = npair // P
    W = P * LANES

    QB = SB_BWD_QBLOCKS if nb % SB_BWD_QBLOCKS == 0 else 1
    nch = QB * 2 * P

    def body(q_ref, o_ref, do_ref, qkv_hbm, kt_hbm, dq_ref, dk_ref, dv_ref, k_ref, v_ref, kt_ref, sems):
        grp = pl.program_id(0)
        i_first = pl.program_id(1) * QB
        m0 = _head_masks()
        top, strict, tri_excl, tri_incl = _sb_masks()
        zq = jnp.zeros((BLK, LANES), BF16)
        lanes = [slice(p * LANES, (p + 1) * LANES) for p in range(P)]

        @pl.when(pl.program_id(1) == 0)
        def _():
            copies = [pltpu.make_async_copy(qkv_hbm.at[:, pl.ds(pl.multiple_of((c * ngroup + grp) * W, LANES), W)],
                                            ref, sems.at[c - 1]) for c, ref in ((1, k_ref), (2, v_ref))]
            copies.append(pltpu.make_async_copy(kt_hbm.at[pl.ds(pl.multiple_of(grp * W, LANES), W), :],
                                                kt_ref, sems.at[2]))
            for cp in copies:
                cp.start()
            dk_ref[...] = jnp.zeros_like(dk_ref)
            dv_ref[...] = jnp.zeros_like(dv_ref)
            for cp in copies:
                cp.wait()

        qh, doh, delta = [], [], []
        for qb in range(QB):
            rs = slice(qb * BLK, (qb + 1) * BLK)
            for sl in lanes:
                q2, do2 = q_ref[rs, sl] * ATTN_SCALE, do_ref[rs, sl]
                qh += [jnp.where(m0, q2, zq), jnp.where(m0, zq, q2)]
                doh += [jnp.where(m0, do2, zq), jnp.where(m0, zq, do2)]
                prod_t = jnp.transpose(do2.astype(F32) * o_ref[rs, sl])
                delta += [jnp.sum(jnp.where(top, prod_t, 0.0), axis=0, keepdims=True),
                          jnp.sum(jnp.where(top, 0.0, prod_t), axis=0, keepdims=True)]

        def block(qbs, js, valid, cb, cg, dq, diag):
            offs = [pl.multiple_of(j * BLK, BLK) for j in js]
            n_ch = len(qbs) * 2 * P
            ks, vs, kth, qs, dos, dls = [], [], [], [], [], []
            for n_qb, qb in enumerate(qbs):
                chains = slice(qb * 2 * P, (qb + 1) * 2 * P)
                qs, dos, dls = qs + qh[chains], dos + doh[chains], dls + delta[chains]
                for sl in lanes:
                    k2, v2 = k_ref[pl.ds(offs[n_qb], BLK), sl], v_ref[pl.ds(offs[n_qb], BLK), sl]
                    ks += [k2, k2]
                    vs += [v2, v2]
                    kt = kt_ref[sl, pl.ds(offs[n_qb], BLK)] * ATTN_SCALE
                    kth += [jnp.where(top, kt, zq), jnp.where(top, zq, kt)]
            dws = [_dot(vs[n], dos[n], NT) for n in range(n_ch)]
            a_l, b_l, w_l = _sb_scores(qs, ks, cb, diag, tri_excl, strict)
            wb = [w.astype(BF16) for w in w_l]
            g_l = [dws[n] * wb[n].astype(F32) for n in range(n_ch)]
            gsplit = [_split_bf16(g) for g in g_l]
            gincs = [_dot(tri_incl, hi, NN) + _dot(tri_incl, lo, NN) for hi, lo in gsplit]
            dzs = []
            for n in range(n_ch):
                beta = jnp.exp(a_l[n])
                dz = g_l[n] - beta * (g_l[n] + ((dls[n] - cg[n]) - gincs[n]))
                if diag:
                    dz = jnp.where(strict, dz, 0.0)
                if valid[n // (2 * P)] is not None:
                    dz = jnp.where(valid[n // (2 * P)], dz, 0.0)
                dzs.append(dz.astype(BF16))
            ndq = []
            for n_qb in range(len(qbs)):
                for p, sl in enumerate(lanes):
                    n0 = n_qb * 2 * P + 2 * p
                    ndq.append(dq[n_qb * P + p] + _dot(kth[n0], dzs[n0], NN) + _dot(kth[n0 + 1], dzs[n0 + 1], NN))
                    dk_ref[pl.ds(offs[n_qb], BLK), sl] += _dot(dzs[n0], qs[n0], NN) + _dot(dzs[n0 + 1], qs[n0 + 1], NN)
                    dv_ref[pl.ds(offs[n_qb], BLK), sl] += _dot(wb[n0], dos[n0], NN) + _dot(wb[n0 + 1], dos[n0 + 1], NN)
            ncb = [cb[n] + jnp.sum(b_l[n], axis=0, keepdims=True) for n in range(n_ch)]
            ncg = [cg[n] + jnp.sum(g_l[n], axis=0, keepdims=True) for n in range(n_ch)]
            return ncb, ncg, ndq

        every = list(range(QB))
        c0 = jnp.zeros((1, BLK), F32)
        cb, cg, dq = block(every, [i_first + qb for qb in every], [None] * QB, [c0] * nch, [c0] * nch,
                           [jnp.zeros((LANES, BLK), F32)] * (QB * P), True)
        has_prev = i_first > 0
        cb = [jnp.where(has_prev, c, NEG_BIG) for c in cb[:2 * P]] + cb[2 * P:]
        cb, cg, dq = block(every, [jnp.maximum(i_first + qb - 1, 0) for qb in every], [has_prev] + [None] * (QB - 1),
                           cb, cg, dq, False)

        for qb in range(QB):
            i_qb = i_first + qb
            sub = slice(qb * 2 * P, (qb + 1) * 2 * P)

            def cond(st):
                return jnp.logical_and(i_qb - st[0] >= 0, st[1] > 0)

            def step(st, qb=qb, i_qb=i_qb):
                t, _, b_qb, g_qb, dq_qb = st
                b_qb, g_qb, dq_qb = block([qb], [i_qb - t], [None], b_qb, g_qb, dq_qb, False)
                return t + 1, _any_alive(b_qb), b_qb, g_qb, dq_qb

            st = lax.while_loop(cond, step, (2, _any_alive(cb[sub]), cb[sub], cg[sub], dq[qb * P:(qb + 1) * P]))
            for p, sl in enumerate(lanes):
                dq_ref[qb * BLK:(qb + 1) * BLK, sl] = jnp.transpose(st[4][p]).astype(BF16)

    blk = pl.BlockSpec((QB * BLK, W), lambda g, i: (i, g))
    col_all = pl.BlockSpec((S, W), lambda g, i: (0, g))
    hbm = pl.BlockSpec(memory_space=pl.ANY)
    return _pcall(
        body, (qkv, o, do, qkv, kv_t), name=name,
        out_shape=(jax.ShapeDtypeStruct((S, D), BF16), jax.ShapeDtypeStruct((S, D), F32),
                   jax.ShapeDtypeStruct((S, D), F32)),
        grid=(ngroup, nb // QB),
        in_specs=[blk, blk, blk, hbm, hbm],
        out_specs=(blk, col_all, col_all),
        scratch_shapes=[pltpu.VMEM((S, W), BF16), pltpu.VMEM((S, W), BF16), pltpu.VMEM((W, S), BF16),
                        pltpu.SemaphoreType.DMA((3,))],
        sem=("arbitrary", "arbitrary"), carry=carry)


SWA_Q_GROUPS = 4


def _roll_heads(x):
    return pltpu.roll(x.astype(F32), HEAD_DIM, 1).astype(BF16)


def _roll_rows(x):
    return pltpu.roll(x.astype(F32), HEAD_DIM, 0).astype(BF16)


def _swa_valid(i):
    k = lax.broadcasted_iota(jnp.int32, (2 * BLK, BLK), 0)
    q = lax.broadcasted_iota(jnp.int32, (2 * BLK, BLK), 1)
    diff = q + BLK - k
    return (diff >= 0) & (diff < BLK) & ((i > 0) | (k >= BLK))


def _swa_probs(z, valid, sink):
    z = jnp.where(valid, z * ATTN_SCALE, NEG_BIG)
    mx = jnp.maximum(jnp.max(z, axis=0, keepdims=True), sink)
    p = jnp.exp(z - mx)
    ps = jnp.exp(sink - mx)
    inv = 1.0 / (jnp.sum(p, axis=0, keepdims=True) + ps)
    return p * inv, ps * inv


def _swa_operands(q_ref, kc_ref, kp_ref, vc_ref, vp_ref, tc_ref, tp_ref, s_ref, nkvp):
    m0 = _head_masks()
    top = lax.broadcasted_iota(jnp.int32, (LANES, 2 * BLK), 0) < HEAD_DIM
    heads = []
    for m in range(nkvp):
        pair = slice(m * LANES, (m + 1) * LANES)
        kk = jnp.concatenate([kp_ref[:, pair], kc_ref[:, pair]], axis=0)
        vv = jnp.concatenate([vp_ref[:, pair], vc_ref[:, pair]], axis=0)
        tt = jnp.concatenate([tp_ref[pair, :], tc_ref[pair, :]], axis=1)
        ksw, vsw, tsw = _roll_heads(kk), _roll_heads(vv), _roll_rows(tt)
        zt = jnp.zeros_like(tt)
        for c in range(SWA_Q_GROUPS):
            q_lanes = slice((m * SWA_Q_GROUPS + c) * LANES, (m * SWA_Q_GROUPS + c + 1) * LANES)
            qc = q_ref[:, q_lanes]
            zq = jnp.zeros_like(qc)
            for u in range(2):
                same = u == c // 2
                sel = (lambda x, z, mk: jnp.where(mk, x, z)) if u == 0 else (lambda x, z, mk: jnp.where(mk, z, x))
                heads.append(dict(
                    m=m, q_lanes=q_lanes, same=same, sel=sel, qm=sel(qc, zq, m0),
                    k=kk if same else ksw, v=vv if same else vsw,
                    tm=sel(tt if same else tsw, zt, top),
                    sink=s_ref[0, (m * SWA_Q_GROUPS + c) * 2 + u]))
    return heads, m0


def _swa_specs(D, half, t_block):
    prev = lambda i: jnp.maximum(i - 1, 0)
    return [pl.BlockSpec((BLK, D), lambda i: (i, 0)),
            pl.BlockSpec((BLK, half), lambda i: (i, 0)),
            pl.BlockSpec((BLK, half), lambda i: (prev(i), 0)),
            pl.BlockSpec((BLK, half), lambda i: (i, 1)),
            pl.BlockSpec((BLK, half), lambda i: (prev(i), 1)),
            pl.BlockSpec((half, BLK), lambda i: (t_block, i)),
            pl.BlockSpec((half, BLK), lambda i: (t_block, prev(i))),
            pl.BlockSpec(memory_space=pltpu.SMEM)]


def _swa_fwd(q, kv, kv_t, sinks, name):
    S, D = q.shape
    half = kv.shape[1] // 2
    nkvp = half // LANES

    def body(q_ref, kc_ref, kp_ref, vc_ref, vp_ref, tc_ref, tp_ref, s_ref, o_ref):
        valid = _swa_valid(pl.program_id(0))
        heads, _ = _swa_operands(q_ref, kc_ref, kp_ref, vc_ref, vp_ref, tc_ref, tp_ref, s_ref, nkvp)
        zs = [_dot(hd["k"], hd["qm"], NT) for hd in heads]
        ps = [_swa_probs(z, valid, hd["sink"])[0].astype(BF16) for z, hd in zip(zs, heads)]
        for n in range(0, len(heads), 2):
            o_t = _dot(heads[n]["tm"], ps[n], NN) + _dot(heads[n + 1]["tm"], ps[n + 1], NN)
            o_ref[:, heads[n]["q_lanes"]] = jnp.transpose(o_t)

    return pl.pallas_call(
        body, name=name, out_shape=jax.ShapeDtypeStruct((S, D), F32),
        grid=(S // BLK,),
        in_specs=_swa_specs(D, half, 1),
        out_specs=pl.BlockSpec((BLK, D), lambda i: (i, 0)),
        compiler_params=_params("arbitrary"),
    )(q, kv, kv, kv, kv, kv_t, kv_t, sinks)


def _swa_bwd(q, kv, kv_t, sinks, o, do, cos_t, sin_t, name, carry=None):
    S, D = q.shape
    half = kv.shape[1] // 2
    nkvp = half // LANES
    nh = nkvp * 2 * SWA_Q_GROUPS

    def body(q_ref, kc_ref, kp_ref, vc_ref, vp_ref, tc_ref, tp_ref, s_ref, o_ref, do_ref, c_ref, sn_ref,
             dq_ref, dk_ref, dv_ref, ds_ref):
        i = pl.program_id(0)
        valid = _swa_valid(i)
        heads, m0 = _swa_operands(q_ref, kc_ref, kp_ref, vc_ref, vp_ref, tc_ref, tp_ref, s_ref, nkvp)
        top_q = lax.broadcasted_iota(jnp.int32, (LANES, BLK), 0) < HEAD_DIM

        @pl.when(i == 0)
        def _():
            dk_ref[...] = jnp.zeros_like(dk_ref)
            dv_ref[...] = jnp.zeros_like(dv_ref)
            ds_ref[...] = jnp.zeros_like(ds_ref)

        doms, deltas = [], []
        for n in range(0, nh, 2):
            doc = do_ref[:, heads[n]["q_lanes"]]
            prod_t = jnp.transpose(doc.astype(F32) * o_ref[:, heads[n]["q_lanes"]])
            for hd in heads[n:n + 2]:
                doms.append(hd["sel"](doc, jnp.zeros_like(doc), m0))
                deltas.append(jnp.sum(hd["sel"](prod_t, 0.0, top_q), axis=0, keepdims=True))
        zs = [_dot(hd["k"], hd["qm"], NT) for hd in heads]
        dps = [_dot(hd["v"], dom, NT) for dom, hd in zip(doms, heads)]
        pbs, dscs = [], []
        for n, hd in enumerate(heads):
            p, psink = _swa_probs(zs[n], valid, hd["sink"])
            pbs.append(p.astype(BF16))
            dscs.append((p * (dps[n] - deltas[n]) * ATTN_SCALE).astype(BF16))
            ds_ref[n:n + 1, :] += -(psink * deltas[n])
        for n in range(0, nh, 2):
            dq_rot = jnp.transpose(_dot(heads[n]["tm"], dscs[n], NN) + _dot(heads[n + 1]["tm"], dscs[n + 1], NN))
            dq_ref[:, heads[n]["q_lanes"]] = (
                dq_rot * c_ref[...] + _swap_halves(dq_rot * sn_ref[...])).astype(BF16)
        acc = {}
        for n, hd in enumerate(heads):
            dk_n = _dot(dscs[n], hd["qm"], NN)
            dv_n = _dot(pbs[n], doms[n], NN)
            for key, val in ((("k", hd["m"], hd["same"]), dk_n), (("v", hd["m"], hd["same"]), dv_n)):
                acc[key] = val if key not in acc else acc[key] + val
        poff = pl.multiple_of(jnp.maximum(i - 1, 0) * BLK, BLK)
        coff = pl.multiple_of(i * BLK, BLK)
        for m in range(nkvp):
            pair = slice(m * LANES, (m + 1) * LANES)
            dkk = acc["k", m, True] + pltpu.roll(acc["k", m, False], HEAD_DIM, 1)
            dvv = acc["v", m, True] + pltpu.roll(acc["v", m, False], HEAD_DIM, 1)
            dk_ref[pl.ds(poff, BLK), pair] += dkk[:BLK]
            dv_ref[pl.ds(poff, BLK), pair] += dvv[:BLK]
            dk_ref[pl.ds(coff, BLK), pair] += dkk[BLK:]
            dv_ref[pl.ds(coff, BLK), pair] += dvv[BLK:]

    qblk = pl.BlockSpec((BLK, D), lambda i: (i, 0))
    whole = pl.BlockSpec((S, half), lambda i: (0, 0))
    tab = pl.BlockSpec((BLK, LANES), lambda i: (i, 0))
    return _pcall(
        body, (q, kv, kv, kv, kv, kv_t, kv_t, sinks, o, do, cos_t, sin_t), name=name,
        out_shape=(jax.ShapeDtypeStruct((S, D), BF16),
                   jax.ShapeDtypeStruct((S, half), F32),
                   jax.ShapeDtypeStruct((S, half), F32),
                   jax.ShapeDtypeStruct((nh, LANES), F32)),
        grid=(S // BLK,),
        in_specs=_swa_specs(D, half, 0) + [qblk, qblk, tab, tab],
        out_specs=(qblk, whole, whole, pl.BlockSpec((nh, LANES), lambda i: (0, 0))),
        sem=("arbitrary",), carry=carry)


def _dev_index(p):
    return 4 * p[0] + 2 * p[1] + p[2]


def _gather_plan(x_refs, out_refs, send_sems, recv_sems, local_sems):
    n = len(x_refs)
    x_, y_, c_ = lax.axis_index("x"), lax.axis_index("y"), lax.axis_index("c")
    me, sibling = (x_, y_, c_), (x_, y_, 1 - c_)
    chips = [(1 - x_, y_), (x_, 1 - y_), (1 - x_, 1 - y_)]

    def copy(t, k, block, to, src=None):
        dst = out_refs[t].at[_dev_index(block)]
        return pltpu.make_async_remote_copy(
            src_ref=dst if src is None else src, dst_ref=dst,
            send_sem=send_sems.at[7 * t + k], recv_sem=recv_sems.at[7 * t + k],
            device_id=to, device_id_type=MESH)

    mine = [pltpu.make_async_copy(x_refs[t], out_refs[t].at[_dev_index(me)], local_sems.at[t]) for t in range(n)]
    first = []
    for t in range(n):
        first.append(copy(t, 0, me, sibling, src=x_refs[t]))
        first += [copy(t, 1 + j, me, (*chip, c_), src=x_refs[t]) for j, chip in enumerate(chips)]
    arrived = lambda t, j: copy(t, 1 + j, (*chips[j], c_), me)
    forward = lambda t, j: copy(t, 4 + j, (*chips[j], c_), sibling)
    from_sibling = lambda t: copy(t, 0, sibling, me)
    forwarded = lambda t, j: copy(t, 4 + j, (*chips[j], 1 - c_), me)
    return n, mine, first, arrived, forward, from_sibling, forwarded


def _gather_start(x_refs, out_refs, send_sems, recv_sems, local_sems):
    _, mine, first, *_ = _gather_plan(x_refs, out_refs, send_sems, recv_sems, local_sems)
    for cp in mine + first:
        cp.start()


def _gather_forward(x_refs, out_refs, send_sems, recv_sems, local_sems):
    n, _, _, arrived, forward, _, _ = _gather_plan(x_refs, out_refs, send_sems, recv_sems, local_sems)
    for j in range(3):
        for t in range(n):
            arrived(t, j).wait_recv()
            forward(t, j).start()


def _gather_finish(x_refs, out_refs, send_sems, recv_sems, local_sems):
    n, mine, first, _, forward, from_sibling, forwarded = _gather_plan(
        x_refs, out_refs, send_sems, recv_sems, local_sems)
    for t in range(n):
        from_sibling(t).wait_recv()
    for j in range(3):
        for t in range(n):
            forwarded(t, j).wait_recv()
    for cp in first + [forward(t, j) for j in range(3) for t in range(n)]:
        cp.wait_send()
    for cp in mine:
        cp.wait()


def _scatter_plan(b_refs, out_refs, send_sems, recv_sems, local_sems):
    n = len(b_refs)
    x_, y_, c_ = lax.axis_index("x"), lax.axis_index("y"), lax.axis_index("c")
    my_idx = _dev_index((x_, y_, c_))
    mine = [pltpu.make_async_copy(b_refs[t].at[my_idx], out_refs[t].at[my_idx], local_sems.at[t]) for t in range(n)]
    copies = []
    for t in range(n):
        for k in range(1, N_DEV):
            peer = (x_ ^ ((k >> 2) & 1), y_ ^ ((k >> 1) & 1), c_ ^ (k & 1))
            copies.append(pltpu.make_async_remote_copy(
                src_ref=b_refs[t].at[_dev_index(peer)], dst_ref=out_refs[t].at[my_idx],
                send_sem=send_sems.at[7 * t + k - 1], recv_sem=recv_sems.at[7 * t + k - 1],
                device_id=peer, device_id_type=MESH))
    return mine, copies


def _scatter_start(b_refs, out_refs, send_sems, recv_sems, local_sems):
    mine, copies = _scatter_plan(b_refs, out_refs, send_sems, recv_sems, local_sems)
    for cp in mine + copies:
        cp.start()


def _scatter_finish(b_refs, out_refs, send_sems, recv_sems, local_sems):
    mine, copies = _scatter_plan(b_refs, out_refs, send_sems, recv_sems, local_sems)
    for cp in copies:
        cp.wait_recv()
    for cp in copies:
        cp.wait_send()
    for cp in mine:
        cp.wait()


def _exchange_operands(kind, tensors):
    if kind == "gather":
        args = list(tensors)
        shapes = [jax.ShapeDtypeStruct((N_DEV,) + t.shape, t.dtype) for t in tensors]
        return args, shapes, (_gather_start, _gather_forward, _gather_finish)
    args = [t.reshape(N_DEV, t.shape[0] // N_DEV, t.shape[1]) for t in tensors]
    shapes = [jax.ShapeDtypeStruct(a.shape, a.dtype) for a in args]
    return args, shapes, (_scatter_start, None, _scatter_finish)


def _exchange_results(kind, tensors, res):
    if kind == "gather":
        return [r.reshape(N_DEV * t.shape[0], t.shape[1]) for r, t in zip(res, tensors)]
    return list(res)


def _exchange_sems(n):
    return [pltpu.SemaphoreType.DMA((7 * n,)), pltpu.SemaphoreType.DMA((7 * n,)), pltpu.SemaphoreType.DMA((n,))]


def _exchange(kind, tensors, name):
    n = len(tensors)
    args, shapes, phases = _exchange_operands(kind, tensors)

    def body(*refs):
        for phase in phases:
            if phase is not None:
                phase(refs[:n], refs[n:2 * n], *refs[2 * n:])

    hbm = pl.BlockSpec(memory_space=pl.ANY)
    res = pl.pallas_call(body, name=name, out_shape=shapes, in_specs=[hbm] * n, out_specs=[hbm] * n,
                         scratch_shapes=_exchange_sems(n))(*args)
    return _exchange_results(kind, tensors, res)


def _pcall(body, args, *, name, out_shape, grid, in_specs, out_specs, sem, scratch_shapes=(), carry=None):
    if carry is None:
        out = pl.pallas_call(body, name=name, out_shape=out_shape, grid=grid, in_specs=list(in_specs),
                             out_specs=out_specs, scratch_shapes=list(scratch_shapes),
                             compiler_params=_params(*sem))(*args)
        return out, None
    kind, tensors = carry
    multi = isinstance(out_shape, (tuple, list))
    shapes = list(out_shape) if multi else [out_shape]
    ospecs = list(out_specs) if multi else [out_specs]
    n_in, n_out, n_scr, n_c = len(in_specs), len(shapes), len(scratch_shapes), len(tensors)
    c_args, c_shapes, (start, forward, finish) = _exchange_operands(kind, tensors)
    n_steps = 1
    for g in grid:
        n_steps *= g
    late = (3 * n_steps) // 4

    def wrapped(*refs):
        ins, rest = refs[:n_in], refs[n_in:]
        c_in, rest = rest[:n_c], rest[n_c:]
        outs, rest = rest[:n_out], rest[n_out:]
        c_out, rest = rest[:n_c], rest[n_c:]
        scr, sems = rest[:n_scr], rest[n_scr:]
        step = pl.program_id(0)
        for a in range(1, len(grid)):
            step = step * grid[a] + pl.program_id(a)

        @pl.when(step == 0)
        def _():
            start(c_in, c_out, *sems)

        body(*ins, *outs, *scr)

        if forward is not None:
            @pl.when(step == late)
            def _():
                forward(c_in, c_out, *sems)

        @pl.when(step == n_steps - 1)
        def _():
            finish(c_in, c_out, *sems)

    hbm = pl.BlockSpec(memory_space=pl.ANY)
    res = pl.pallas_call(
        wrapped, name=name, out_shape=shapes + c_shapes, grid=grid,
        in_specs=list(in_specs) + [hbm] * n_c, out_specs=ospecs + [hbm] * n_c,
        scratch_shapes=list(scratch_shapes) + _exchange_sems(n_c),
        compiler_params=_params(*sem))(*args, *c_args)
    outs = tuple(res[:n_out]) if multi else res[0]
    return outs, _exchange_results(kind, tensors, res[n_out:])


def _sum8(parts, name):
    _, R, C = parts.shape
    tr = _tile(R, 256, 16)

    def body(p_ref, g_ref):
        g = p_ref[0].astype(F32)
        for s in range(1, N_DEV):
            g = g + p_ref[s].astype(F32)
        g_ref[...] = g

    return pl.pallas_call(
        body, name=name, out_shape=jax.ShapeDtypeStruct((R, C), F32),
        grid=(R // tr,),
        in_specs=[pl.BlockSpec((N_DEV, tr, C), lambda i: (0, i, 0))],
        out_specs=pl.BlockSpec((tr, C), lambda i: (i, 0)),
        compiler_params=_params("parallel"),
    )(parts)


def _adamw(g, w, m, v, name):
    R, C = g.shape
    tr = _tile(R, 256, 8)
    c1 = 1.0 - ADAM_B1 ** ADAM_STEP
    c2 = 1.0 - ADAM_B2 ** ADAM_STEP

    def body(g_ref, w_ref, m_ref, v_ref, d_ref, nm_ref, nv_ref):
        gg = g_ref[...]
        nm = ADAM_B1 * m_ref[...] + (1.0 - ADAM_B1) * gg
        nv = ADAM_B2 * v_ref[...] + (1.0 - ADAM_B2) * (gg * gg)
        m_hat = nm / c1
        v_hat = nv / c2
        nm_ref[...] = nm
        nv_ref[...] = nv
        d_ref[...] = -ADAM_LR * (m_hat / (jnp.sqrt(v_hat) + ADAM_EPS) + ADAM_WD * w_ref[...])

    row = pl.BlockSpec((tr, C), lambda i: (i, 0))
    shp = jax.ShapeDtypeStruct((R, C), F32)
    return pl.pallas_call(
        body, name=name, out_shape=(shp, shp, shp),
        grid=(R // tr,), in_specs=[row, row, row, row], out_specs=(row, row, row),
        compiler_params=_params("parallel"),
    )(g, w, m, v)


def _ffn_down(act, wo, h, tag):
    return _mm(act, wo, NN, F32, f"{tag}_down", scale=FFN_RES_SCALE, res=h, tm=512, tn=1024, tk=2816)


def _ffn_fwd(h, g, win_t, wo, tag, carry=None, loss=None):
    return _ffn_fwd_fused(h, g, win_t, wo, f"{tag}_fwd", carry=carry, loss=loss)


def _ffn_bwd(dh, h, g, win_t, wo, saved, tag, scatter=False, carry=None, carry_dwin=None):
    xn, silu, dsilu, up, act = saved
    dwo = _mm(act, dh, TN, BF16, f"{tag}_dwo", scale=FFN_RES_SCALE, tm=1408, tn=1024, tk=TN_CHUNK)
    if not scatter:
        (dh_in, dg, dgate, dup), got = _ffn_bwd_fused(dh, h, g, win_t, wo, silu, dsilu, up, f"{tag}_bwd", carry=carry)
        dwin_t, got_dwin = _dw_rows([dgate, dup], xn, f"{tag}_dwin", carry=carry_dwin)
        return dh_in, dg, dwin_t, dwo, got, got_dwin
    dgate, dup = _ffn_dact(dh, wo, silu, dsilu, up, f"{tag}_dact")
    dwin_t, got_wo = _dw_rows([dgate, dup], xn, f"{tag}_dwin", carry=("scatter", [dwo]))
    (dh_in, dg), got_win = _dx_norm_bwd([(dgate, win_t, NN, 2, 0), (dup, win_t, NN, 2, 1)], h, g, dh, f"{tag}_dx",
                                        carry=("scatter", [dwin_t]))
    return dh_in, dg, got_win[0], got_wo[0]


def _proj(a, w, dims, out_dtype, name, res=None):
    return _mm(a, w, dims, out_dtype, name, res=res, tm=1024, tn=1024, tk=1024)


def _proj_dw(x, dy, name):
    return _mm(x, dy, TN, BF16, name, tm=1024, tn=1024, tk=TN_CHUNK)


def kernel(x, ffn1_norm, ffn1_w_in, ffn1_w_out, mix_norm, ffn2_norm, ffn2_w_in, ffn2_w_out, sb_w_qkv, sb_w_o, kv_norm, kv_w, swa_w_q, swa_sinks, swa_w_o, final_norm, loss_target, m_ffn1_norm, m_ffn1_w_in, m_ffn1_w_out, m_mix_norm, m_ffn2_norm, m_ffn2_w_in, m_ffn2_w_out, m_sb_w_qkv, m_sb_w_o, m_kv_norm, m_kv_w, m_swa_w_q, m_swa_sinks, m_swa_w_o, m_final_norm, v_ffn1_norm, v_ffn1_w_in, v_ffn1_w_out, v_mix_norm, v_ffn2_norm, v_ffn2_w_in, v_ffn2_w_out, v_sb_w_qkv, v_sb_w_o, v_kv_norm, v_kv_w, v_swa_w_q, v_swa_sinks, v_swa_w_o, v_final_norm):
    S, D = x.shape[1], x.shape[2]
    L = ffn1_w_in.shape[0]
    KV = kv_w.shape[1]
    assert L == 2 and swa_sinks.shape == (1, 2 * SWA_Q_GROUPS * KV // (2 * LANES))

    def bf(w):
        return w.astype(BF16)

    def bft(w):
        return jnp.transpose(w).astype(BF16)

    cos_t, sin_t = _rope_tables(S)
    h0 = x.reshape(S, D)
    tgt = loss_target.reshape(S, D)

    win1a_t, = _exchange("gather", [bft(ffn1_w_in[0])], "gather_first_weight")
    sv_a1, (wo1a, wqkv_t, w_sbo) = _ffn_up(
        h0, ffn1_norm[0], win1a_t, "ffn1a_up",
        carry=("gather", [bf(ffn1_w_out[0]), bft(sb_w_qkv[0]), bf(sb_w_o[0])]))
    h1 = _ffn_down(sv_a1[-1], wo1a, h0, "ffn1a")
    hn_a, qkv, kv_t = _norm_proj(h1, mix_norm[0], wqkv_t, NT, "sb_qkv", tail_t=2 * D)
    o_sb, (win2a_t, wo2a, w_kv) = _sb_fwd(qkv, kv_t, "sb_attn", carry=("gather", [
        bft(ffn2_w_in[0]), bf(ffn2_w_out[0]), bf(kv_w)]))
    h2 = _proj(o_sb, w_sbo, NN, F32, "sb_out", res=h1)
    h3, sv_a2, (win1b_t, wo1b, w_q, w_swo) = _ffn_fwd(h2, ffn2_norm[0], win2a_t, wo2a, "ffn2a", carry=("gather", [
        bft(ffn1_w_in[1]), bf(ffn1_w_out[1]), bf(swa_w_q[0]), bf(swa_w_o[0])]))
    kvn, kv_rot, kv_rot_t = _norm_proj(h3, kv_norm, w_kv, NN, "kv_proj", rope=(cos_t, sin_t, KV // (2 * LANES)),
                                       tail_t=KV)
    h4, sv_b1, (win2b_t, wo2b) = _ffn_fwd(h3, ffn1_norm[1], win1b_t, wo1b, "ffn1b", carry=("gather", [
        bft(ffn2_w_in[1]), bf(ffn2_w_out[1])]))
    hn_b, q_rot = _norm_proj(h4, mix_norm[1], w_q, NN, "swa_q", rope=(cos_t, sin_t, D // LANES))
    o_sw = _swa_fwd(q_rot, kv_rot, kv_rot_t, swa_sinks, "swa_attn")
    h5 = _proj(o_sw, w_swo, NN, F32, "swa_out", res=h4)
    (dh6, dg_final, sq_err), sv_b2, _ = _ffn_fwd(h5, ffn2_norm[1], win2b_t, wo2b, "ffn2b", loss=(final_norm, tgt))
    loss_local = 0.5 * jnp.sum(sq_err) / D

    dh5, dg_f2b, dwin2b_t, dwo2b, _, _ = _ffn_bwd(dh6, h5, ffn2_norm[1], win2b_t, wo2b, sv_b2, "ffn2b")
    do_sw = _proj(dh5, w_swo, NT, BF16, "swa_out_dx")
    dw_swo = _proj_dw(o_sw, dh5, "swa_out_dw")
    (dq, dk_sw, dv_sw, dsink), (p_win2b, p_swo) = _swa_bwd(
        q_rot, kv_rot, kv_rot_t, swa_sinks, o_sw, do_sw, cos_t, sin_t, "swa_attn_bwd",
        carry=("scatter", [dwin2b_t, dw_swo]))
    dw_q = _proj_dw(hn_b, dq, "swa_q_dw")
    (dh4, dg_mix_b), _ = _dx_norm_bwd([(dq, w_q, NT, 1, 0)], h4, mix_norm[1], dh5, "swa_q_dx", tm=512)
    dh3, dg_f1b, dwin1b_t, dwo1b, _, _ = _ffn_bwd(dh4, h3, ffn1_norm[1], win1b_t, wo1b, sv_b1, "ffn1b")
    dkv = _rotary(jnp.concatenate([dk_sw, dv_sw], axis=1), cos_t, sin_t, KV // (2 * LANES), True, "kv_rope_bwd")
    dw_kv = _proj_dw(kvn, dkv, "kv_proj_dw")
    (dh3, dg_kv), _ = _dx_norm_bwd([(dkv, w_kv, NT, 1, 0)], h3, kv_norm, dh3, "kv_proj_dx", tm=512)
    dh2, dg_f2a, dwin2a_t, dwo2a, (p_win1b, p_kv), _ = _ffn_bwd(
        dh3, h2, ffn2_norm[0], win2a_t, wo2a, sv_a2, "ffn2a", carry=("scatter", [dwin1b_t, dw_kv]))
    do_sb = _proj(dh2, w_sbo, NT, BF16, "sb_out_dx")
    dw_sbo = _proj_dw(o_sb, dh2, "sb_out_dw")
    (dq_sb, dk_sb, dv_sb), (p_win2a, p_wo2a, p_sbo, p_wo1b, p_q, p_wo2b) = _sb_bwd(
        qkv, kv_t, o_sb, do_sb, "sb_attn_bwd", carry=("scatter", [dwin2a_t, dwo2a, dw_sbo, dwo1b, dw_q, dwo2b]))
    dqkv = [dq_sb, dk_sb, dv_sb]
    dwqkv_t, _ = _dw_rows(dqkv, hn_a, "sb_qkv_dw", tk=TN_CHUNK // 2)
    (dh1, dg_mix_a), (p_qkv,) = _dx_norm_bwd([(dy, wqkv_t, NN, 3, n) for n, dy in enumerate(dqkv)], h1, mix_norm[0],
                                             dh2, "sb_qkv_dx", carry=("scatter", [dwqkv_t]), tm=512)
    dx, dg_f1a, p_win1a, p_wo1a = _ffn_bwd(dh1, h0, ffn1_norm[0], win1a_t, wo1a, sv_a1, "ffn1a", scatter=True)

    def natural(parts, tag):
        return _sum8(parts, f"sum_{tag}")

    def from_t(parts, tag):
        return jnp.transpose(_sum8(parts, f"sum_{tag}"))

    grads = {
        "ffn1_w_in": jnp.stack([from_t(p_win1a, "win1a"), from_t(p_win1b, "win1b")]),
        "ffn1_w_out": jnp.stack([natural(p_wo1a, "wo1a"), natural(p_wo1b, "wo1b")]),
        "ffn2_w_in": jnp.stack([from_t(p_win2a, "win2a"), from_t(p_win2b, "win2b")]),
        "ffn2_w_out": jnp.stack([natural(p_wo2a, "wo2a"), natural(p_wo2b, "wo2b")]),
        "sb_w_qkv": from_t(p_qkv, "qkv")[None],
        "sb_w_o": natural(p_sbo, "sbo")[None],
        "kv_w": natural(p_kv, "kv"),
        "swa_w_q": natural(p_q, "swq")[None],
        "swa_w_o": natural(p_swo, "swo")[None],
    }

    small_w = [ffn1_norm, mix_norm, ffn2_norm, kv_norm, final_norm, swa_sinks]
    small_m = [m_ffn1_norm, m_mix_norm, m_ffn2_norm, m_kv_norm, m_final_norm, m_swa_sinks]
    small_v = [v_ffn1_norm, v_mix_norm, v_ffn2_norm, v_kv_norm, v_final_norm, v_swa_sinks]
    SMALL_ROWS = 16

    def pack_small(ts):
        rows_ = [t.reshape(-1, D) for t in ts[:-1]]
        sink_row = jnp.pad(ts[-1].reshape(1, -1), ((0, 0), (0, D - ts[-1].size)))
        flat = jnp.concatenate(rows_ + [sink_row], axis=0)
        return jnp.pad(flat, ((0, SMALL_ROWS - flat.shape[0]), (0, 0)))

    def unpack_small(flat):
        out, r = [], 0
        for t in small_w[:-1]:
            n = t.size // D
            out.append(flat[r:r + n].reshape(t.shape))
            r += n
        out.append(flat[r, :swa_sinks.size].reshape(swa_sinks.shape))
        return out

    def gain(parts8):
        return jnp.sum(parts8, axis=0, keepdims=True)

    g_small_local = pack_small([
        jnp.concatenate([gain(dg_f1a), gain(dg_f1b)], axis=0),
        jnp.concatenate([gain(dg_mix_a), gain(dg_mix_b)], axis=0),
        jnp.concatenate([gain(dg_f2a), gain(dg_f2b)], axis=0),
        gain(dg_kv), gain(dg_final), jnp.sum(dsink, axis=-1).reshape(1, -1)])
    loss_row = sum(t.size for t in small_w[:-1]) // D + 1
    assert loss_row < SMALL_ROWS
    g_small_local = g_small_local.at[loss_row, 0].set(loss_local)
    small_parts = _exchange("gather", [g_small_local], "gather_small_grads")[0]
    g_small = _sum8(small_parts.reshape(N_DEV, SMALL_ROWS, D), "sum_small")
    loss = g_small[loss_row, 0]
    d_small, nm_small, nv_small = _adamw(g_small, pack_small(small_w), pack_small(small_m), pack_small(small_v), "adamw_small")
    small_names = ["ffn1_norm", "mix_norm", "ffn2_norm", "kv_norm", "final_norm", "swa_sinks"]
    result = {"grad": dict(zip(small_names, unpack_small(g_small))),
              "delta": dict(zip(small_names, unpack_small(d_small))),
              "new_m": dict(zip(small_names, unpack_small(nm_small))),
              "new_v": dict(zip(small_names, unpack_small(nv_small)))}

    big = {"ffn1_w_in": (ffn1_w_in, m_ffn1_w_in, v_ffn1_w_in), "ffn1_w_out": (ffn1_w_out, m_ffn1_w_out, v_ffn1_w_out),
           "ffn2_w_in": (ffn2_w_in, m_ffn2_w_in, v_ffn2_w_in), "ffn2_w_out": (ffn2_w_out, m_ffn2_w_out, v_ffn2_w_out),
           "sb_w_qkv": (sb_w_qkv, m_sb_w_qkv, v_sb_w_qkv), "sb_w_o": (sb_w_o, m_sb_w_o, v_sb_w_o),
           "kv_w": (kv_w, m_kv_w, v_kv_w), "swa_w_q": (swa_w_q, m_swa_w_q, v_swa_w_q),
           "swa_w_o": (swa_w_o, m_swa_w_o, v_swa_w_o)}
    for nm, (w, m, v) in big.items():
        g = grads[nm]
        two_d = lambda t: t.reshape(-1, t.shape[-1])
        d, new_m, new_v = _adamw(two_d(g), two_d(w), two_d(m), two_d(v), f"adamw_{nm}")
        result["grad"][nm] = g
        result["delta"][nm] = d.reshape(w.shape)
        result["new_m"][nm] = new_m.reshape(w.shape)
        result["new_v"][nm] = new_v.reshape(w.shape)

    order = ["ffn1_norm", "ffn1_w_in", "ffn1_w_out", "mix_norm", "ffn2_norm", "ffn2_w_in", "ffn2_w_out",
             "sb_w_qkv", "sb_w_o", "kv_norm", "kv_w", "swa_w_q", "swa_sinks", "swa_w_o", "final_norm"]
    outs = [result[kind][nm] for kind in ("grad", "delta", "new_m", "new_v") for nm in order]
    return (loss, dx.reshape(x.shape), *outs)
```

```python
import jax
import jax.numpy as jnp
from jax import lax
from jax.experimental import pallas as pl
from jax.experimental.pallas import tpu as pltpu

F32 = jnp.float32
BF16 = jnp.bfloat16

N_DEV = 8
HEAD_DIM = 64
LANES = 128
BLK = 128
RMS_EPS = 1e-6
FFN_RES_SCALE = 0.5
ROPE_THETA = 10000.0
ATTN_SCALE = HEAD_DIM ** -0.5
SB_LOG_FLOOR = -88.0
NEG_BIG = -1e30
VMEM_LIMIT_V7X = 56 * 1024 * 1024

ADAM_LR = 0.001
ADAM_B1 = 0.9
ADAM_B2 = 0.999
ADAM_EPS = 1e-08
ADAM_WD = 0.01
ADAM_STEP = 10

NN = ((1,), (0,))
NT = ((1,), (1,))
TN = ((0,), (0,))
TN_CHUNK = 2048
MESH = pl.DeviceIdType.MESH


def _dot(a, b, dims):
    return lax.dot_general(a, b, (dims, ((), ())), preferred_element_type=F32)


def _tile(n, pref, mult=LANES):
    if n <= pref:
        return n
    t = (pref // mult) * mult
    while t >= mult:
        if n % t == 0:
            return t
        t -= mult
    return n


def _params(*sem):
    return pltpu.CompilerParams(dimension_semantics=sem, vmem_limit_bytes=VMEM_LIMIT_V7X)


def _mm(a, b, dims, out_dtype, name, scale=1.0, res=None, tm=512, tn=512, tk=512):
    if dims == NN:
        (M, K), (_, N) = a.shape, b.shape
    elif dims == NT:
        (M, K), (N, _) = a.shape, b.shape
    else:
        (K, M), (_, N) = a.shape, b.shape
    tm, tn, tk = _tile(M, tm), _tile(N, tn), _tile(K, tk)
    nk = K // tk
    if dims == TN:
        a_spec = pl.BlockSpec((tk, tm), lambda i, j, k: (k, i))
    else:
        a_spec = pl.BlockSpec((tm, tk), lambda i, j, k: (i, k))
    if dims == NT:
        b_spec = pl.BlockSpec((tn, tk), lambda i, j, k: (j, k))
    else:
        b_spec = pl.BlockSpec((tk, tn), lambda i, j, k: (k, j))
    o_spec = pl.BlockSpec((tm, tn), lambda i, j, k: (i, j))
    has_res = res is not None

    def body(*refs):
        a_ref, b_ref = refs[0], refs[1]
        r_ref = refs[2] if has_res else None
        o_ref = refs[3] if has_res else refs[2]

        def finish(acc):
            r = acc * scale if scale != 1.0 else acc
            if has_res:
                r = r + r_ref[...]
            o_ref[...] = r.astype(out_dtype)

        p = _dot(a_ref[...].astype(BF16), b_ref[...].astype(BF16), dims)
        if nk == 1:
            finish(p)
        else:
            acc_ref = refs[-1]
            k = pl.program_id(2)

            @pl.when(k == 0)
            def _():
                acc_ref[...] = p

            @pl.when(k > 0)
            def _():
                acc_ref[...] += p

            @pl.when(k == nk - 1)
            def _():
                finish(acc_ref[...])

    in_specs = [a_spec, b_spec] + ([o_spec] if has_res else [])
    args = (a, b) + ((res,) if has_res else ())
    return pl.pallas_call(
        body, name=name,
        out_shape=jax.ShapeDtypeStruct((M, N), out_dtype),
        grid=(M // tm, N // tn, nk),
        in_specs=in_specs, out_specs=o_spec,
        scratch_shapes=[pltpu.VMEM((tm, tn), F32)] if nk > 1 else [],
        compiler_params=_params("parallel", "parallel", "arbitrary"),
    )(*args)


def _rows8(x):
    r, d = x.shape
    return jnp.sum(x.reshape(r // 8, 8, d), axis=0)


def _norm_proj(h, g, w, dims, name, rope=None, tail_t=0):
    S, D = h.shape
    N = w.shape[1] if dims == NN else w.shape[0]
    tm = _tile(S, 512, 16)

    def body(h_ref, g_ref, w_ref, *rest):
        xn_ref, y_ref = rest[-3:-1] if tail_t else rest[-2:]
        x = h_ref[...]
        r = lax.rsqrt(jnp.mean(x * x, axis=-1, keepdims=True) + RMS_EPS)
        xn = ((x * r) * g_ref[...]).astype(BF16)
        xn_ref[...] = xn
        y = _dot(xn, w_ref[...], dims)
        if rope is not None:
            cs, sn = rest[0][...], rest[1][...]
            groups = [y[:, gidx * LANES:(gidx + 1) * LANES] for gidx in range(N // LANES)]
            y = jnp.concatenate([v * cs + _swap_halves(v) * sn if gidx < rope[2] else v
                                 for gidx, v in enumerate(groups)], axis=1)
        y_ref[...] = y.astype(BF16)
        if tail_t:
            rest[-1][...] = jnp.transpose(y[:, N - tail_t:]).astype(BF16)

    row = pl.BlockSpec((tm, D), lambda i: (i, 0))
    tab = pl.BlockSpec((tm, LANES), lambda i: (i, 0))
    in_specs = [row, pl.BlockSpec((1, D), lambda i: (0, 0)), pl.BlockSpec(w.shape, lambda i: (0, 0))]
    args = (h, g.reshape(1, D), w)
    if rope is not None:
        in_specs += [tab, tab]
        args += (rope[0], rope[1])
    out_shape = [jax.ShapeDtypeStruct((S, D), BF16), jax.ShapeDtypeStruct((S, N), BF16)]
    out_specs = [row, pl.BlockSpec((tm, N), lambda i: (i, 0))]
    if tail_t:
        out_shape.append(jax.ShapeDtypeStruct((tail_t, S), BF16))
        out_specs.append(pl.BlockSpec((tail_t, tm), lambda i: (0, i)))
    return pl.pallas_call(
        body, name=name, out_shape=out_shape, grid=(S // tm,),
        in_specs=in_specs, out_specs=out_specs,
        compiler_params=_params("parallel"),
    )(*args)


def _ffn_up(h, g, win_t, name, carry=None):
    S, D = h.shape
    F = win_t.shape[0] // 2
    tm = _tile(S, 256, 16)

    def body(h_ref, g_ref, win_hbm, xn_ref, silu_ref, dsilu_ref, up_ref, act_ref, win_v, sems):
        _load_resident([(win_hbm, win_v)], sems)
        x = h_ref[...]
        r = lax.rsqrt(jnp.mean(x * x, axis=-1, keepdims=True) + RMS_EPS)
        xn = ((x * r) * g_ref[...]).astype(BF16)
        xn_ref[...] = xn
        gate = _dot(xn, win_v[:F, :], NT)
        up = _dot(xn, win_v[F:, :], NT)
        sig = 1.0 / (1.0 + jnp.exp(-gate))
        silu = gate * sig
        up_ref[...] = up.astype(BF16)
        silu_ref[...] = silu.astype(BF16)
        dsilu_ref[...] = (sig + silu * (1.0 - sig)).astype(BF16)
        act_ref[...] = (silu * up).astype(BF16)

    row = pl.BlockSpec((tm, D), lambda i: (i, 0))
    wide = pl.BlockSpec((tm, F), lambda i: (i, 0))
    hid = jax.ShapeDtypeStruct((S, F), BF16)
    return _pcall(
        body, (h, g.reshape(1, D), win_t), name=name,
        out_shape=(jax.ShapeDtypeStruct((S, D), BF16), hid, hid, hid, hid),
        grid=(S // tm,),
        in_specs=[row, pl.BlockSpec((1, D), lambda i: (0, 0)), pl.BlockSpec(memory_space=pl.ANY)],
        out_specs=(row, wide, wide, wide, wide),
        scratch_shapes=[pltpu.VMEM(win_t.shape, BF16), pltpu.SemaphoreType.DMA((1,))],
        sem=("arbitrary",), carry=carry)


def _ffn_dact(dh, wo, silu, dsilu, up, name):
    S, D = dh.shape
    F = wo.shape[0]
    tm = _tile(S, 256, 16)

    def body(dh_ref, wo_hbm, s_ref, ds_ref, u_ref, dg_ref, du_ref, wo_v, sems):
        _load_resident([(wo_hbm, wo_v)], sems)
        d = _dot(dh_ref[...].astype(BF16), wo_v[...], NT) * FFN_RES_SCALE
        du_ref[...] = (d * s_ref[...].astype(F32)).astype(BF16)
        dg_ref[...] = (d * u_ref[...].astype(F32) * ds_ref[...].astype(F32)).astype(BF16)

    wide = pl.BlockSpec((tm, F), lambda i: (i, 0))
    hid = jax.ShapeDtypeStruct((S, F), BF16)
    return pl.pallas_call(
        body, name=name, out_shape=(hid, hid),
        grid=(S // tm,),
        in_specs=[pl.BlockSpec((tm, D), lambda i: (i, 0)), pl.BlockSpec(memory_space=pl.ANY), wide, wide, wide],
        out_specs=(wide, wide),
        scratch_shapes=[pltpu.VMEM(wo.shape, BF16), pltpu.SemaphoreType.DMA((1,))],
        compiler_params=_params("arbitrary"),
    )(dh, wo, silu, dsilu, up)


def _dw_rows(srcs, x, name, carry=None, tk=TN_CHUNK):
    n = len(srcs)
    S, F = srcs[0].shape
    D = x.shape[1]
    tr, tk = _tile(F, 1408), _tile(S, tk, 16)
    nf, nk = F // tr, S // tk

    def body(*refs):
        src_refs, (x_ref, o_ref, acc_ref) = refs[:n], refs[n:]
        r, k = pl.program_id(0), pl.program_id(1)
        for s in range(n):
            @pl.when(r // nf == s)
            def _():
                p = _dot(src_refs[s][...].astype(BF16), x_ref[...], TN)

                @pl.when(k == 0)
                def _():
                    acc_ref[...] = p

                @pl.when(k > 0)
                def _():
                    acc_ref[...] += p

        @pl.when(k == nk - 1)
        def _():
            o_ref[...] = acc_ref[...].astype(BF16)

    def src_spec(s):
        return pl.BlockSpec((tk, tr), lambda r, k: (jnp.where(r // nf == s, k, 0), jnp.clip(r - s * nf, 0, nf - 1)))

    return _pcall(
        body, (*srcs, x), name=name, out_shape=jax.ShapeDtypeStruct((n * F, D), BF16),
        grid=(n * nf, nk),
        in_specs=[src_spec(s) for s in range(n)] + [pl.BlockSpec((tk, D), lambda r, k: (k, 0))],
        out_specs=pl.BlockSpec((tr, D), lambda r, k: (r, 0)),
        scratch_shapes=[pltpu.VMEM((tr, D), F32)],
        sem=("arbitrary", "arbitrary"), carry=carry)


def _dx_norm_bwd(terms, h, g, res, name, carry=None, tm=256):
    S, D = h.shape
    tm = _tile(S, tm, 16)
    n = len(terms)

    def body(*refs):
        dy_refs, w_refs = refs[:n], refs[n:2 * n]
        h_ref, g_ref, r_ref, dh_ref, dg_ref = refs[2 * n:]
        d = _dot(dy_refs[0][...].astype(BF16), w_refs[0][...], terms[0][2])
        for t in range(1, n):
            d = d + _dot(dy_refs[t][...].astype(BF16), w_refs[t][...], terms[t][2])
        x = h_ref[...]
        r = lax.rsqrt(jnp.mean(x * x, axis=-1, keepdims=True) + RMS_EPS)
        xhat = x * r
        dxh = d * g_ref[...]
        c = jnp.mean(dxh * xhat, axis=-1, keepdims=True)
        dh_ref[...] = r * (dxh - xhat * c) + r_ref[...]
        part = _rows8(d * xhat)

        @pl.when(pl.program_id(0) == 0)
        def _():
            dg_ref[...] = part

        @pl.when(pl.program_id(0) > 0)
        def _():
            dg_ref[...] += part

    def w_spec(w, nblk, blk):
        return pl.BlockSpec((w.shape[0] // nblk, w.shape[1]), lambda i: (blk, 0))

    row = pl.BlockSpec((tm, D), lambda i: (i, 0))
    in_specs = [pl.BlockSpec((tm, t[0].shape[1]), lambda i: (i, 0)) for t in terms]
    in_specs += [w_spec(t[1], t[3], t[4]) for t in terms]
    in_specs += [row, pl.BlockSpec((1, D), lambda i: (0, 0)), row]
    return _pcall(
        body, (*[t[0] for t in terms], *[t[1] for t in terms], h, g.reshape(1, D), res), name=name,
        out_shape=(jax.ShapeDtypeStruct((S, D), F32), jax.ShapeDtypeStruct((8, D), F32)),
        grid=(S // tm,),
        in_specs=in_specs,
        out_specs=(row, pl.BlockSpec((8, D), lambda i: (0, 0))),
        sem=("arbitrary",), carry=carry)


def _load_resident(pairs, sems):
    @pl.when(pl.program_id(0) == 0)
    def _():
        copies = [pltpu.make_async_copy(src, dst, sems.at[n]) for n, (src, dst) in enumerate(pairs)]
        for cp in copies:
            cp.start()
        for cp in copies:
            cp.wait()


def _loss_tail(y_in, g, tgt):
    D = y_in.shape[-1]
    r = lax.rsqrt(jnp.mean(y_in * y_in, axis=-1, keepdims=True) + RMS_EPS)
    xhat = y_in * r
    err = xhat * g - tgt
    d = err * (1.0 / D)
    dxh = d * g
    c = jnp.mean(dxh * xhat, axis=-1, keepdims=True)
    return r * (dxh - xhat * c), _rows8(d * xhat), _rows8(err * err)


def _ffn_fwd_fused(h, g, win_t, wo, name, carry=None, loss=None):
    S, D = h.shape
    F = wo.shape[0]
    tm = _tile(S, 256, 16)
    n_head = 3 if loss is not None else 1

    def body(h_ref, g_ref, win_hbm, wo_hbm, *rest):
        lead, (xn_ref, silu_ref, dsilu_ref, up_ref, act_ref, win_v, wo_v, sems) = rest[:-8], rest[-8:]
        _load_resident([(win_hbm, win_v), (wo_hbm, wo_v)], sems)
        x = h_ref[...]
        r = lax.rsqrt(jnp.mean(x * x, axis=-1, keepdims=True) + RMS_EPS)
        xn = ((x * r) * g_ref[...]).astype(BF16)
        xn_ref[...] = xn
        gate = _dot(xn, win_v[:F, :], NT)
        up = _dot(xn, win_v[F:, :], NT)
        sig = 1.0 / (1.0 + jnp.exp(-gate))
        silu = gate * sig
        act = (silu * up).astype(BF16)
        up_ref[...] = up.astype(BF16)
        silu_ref[...] = silu.astype(BF16)
        dsilu_ref[...] = (sig + silu * (1.0 - sig)).astype(BF16)
        act_ref[...] = act
        out = x + FFN_RES_SCALE * _dot(act, wo_v[...], NN)
        if loss is None:
            lead[0][...] = out
        else:
            gf_ref, t_ref, dy_ref, dgf_ref, sq_ref = lead
            dy, dgf, sq = _loss_tail(out, gf_ref[...], t_ref[...])
            dy_ref[...] = dy

            @pl.when(pl.program_id(0) == 0)
            def _():
                dgf_ref[...] = dgf
                sq_ref[...] = sq

            @pl.when(pl.program_id(0) > 0)
            def _():
                dgf_ref[...] += dgf
                sq_ref[...] += sq

    row = pl.BlockSpec((tm, D), lambda i: (i, 0))
    vec = pl.BlockSpec((1, D), lambda i: (0, 0))
    acc = pl.BlockSpec((8, D), lambda i: (0, 0))
    wide = pl.BlockSpec((tm, F), lambda i: (i, 0))
    hbm = pl.BlockSpec(memory_space=pl.ANY)
    hid = jax.ShapeDtypeStruct((S, F), BF16)
    full = jax.ShapeDtypeStruct((S, D), F32)
    part = jax.ShapeDtypeStruct((8, D), F32)
    args, in_specs = (h, g.reshape(1, D), win_t, wo), [row, vec, hbm, hbm]
    lead_shapes, lead_specs = (full,), (row,)
    if loss is not None:
        args, in_specs = args + (loss[0].reshape(1, D), loss[1]), in_specs + [vec, row]
        lead_shapes, lead_specs = (full, part, part), (row, acc, acc)
    res, got = _pcall(
        body, args, name=name,
        out_shape=lead_shapes + (jax.ShapeDtypeStruct((S, D), BF16), hid, hid, hid, hid),
        grid=(S // tm,),
        in_specs=in_specs,
        out_specs=lead_specs + (row, wide, wide, wide, wide),
        scratch_shapes=[pltpu.VMEM(win_t.shape, BF16), pltpu.VMEM(wo.shape, BF16), pltpu.SemaphoreType.DMA((2,))],
        sem=("arbitrary",), carry=carry)
    first = res[0] if loss is None else tuple(res[:3])
    return first, tuple(res[n_head:]), got


def _ffn_bwd_fused(dh, h, g, win_t, wo, silu, dsilu, up, name, carry=None):
    S, D = h.shape
    F = wo.shape[0]
    tm = _tile(S, 256, 16)

    def body(dh_ref, h_ref, g_ref, s_ref, ds_ref, u_ref, win_hbm, wo_hbm,
             dhin_ref, dgain_ref, dgate_ref, dup_ref, win_v, wo_v, sems):
        _load_resident([(win_hbm, win_v), (wo_hbm, wo_v)], sems)
        dhv = dh_ref[...]
        d = _dot(dhv.astype(BF16), wo_v[...], NT) * FFN_RES_SCALE
        dup = (d * s_ref[...].astype(F32)).astype(BF16)
        dgate = (d * u_ref[...].astype(F32) * ds_ref[...].astype(F32)).astype(BF16)
        dup_ref[...] = dup
        dgate_ref[...] = dgate
        dxn = _dot(dgate, win_v[:F, :], NN) + _dot(dup, win_v[F:, :], NN)
        x = h_ref[...]
        r = lax.rsqrt(jnp.mean(x * x, axis=-1, keepdims=True) + RMS_EPS)
        xhat = x * r
        dxh = dxn * g_ref[...]
        c = jnp.mean(dxh * xhat, axis=-1, keepdims=True)
        dhin_ref[...] = r * (dxh - xhat * c) + dhv
        part = _rows8(dxn * xhat)

        @pl.when(pl.program_id(0) == 0)
        def _():
            dgain_ref[...] = part

        @pl.when(pl.program_id(0) > 0)
        def _():
            dgain_ref[...] += part

    row = pl.BlockSpec((tm, D), lambda i: (i, 0))
    wide = pl.BlockSpec((tm, F), lambda i: (i, 0))
    hbm = pl.BlockSpec(memory_space=pl.ANY)
    hid = jax.ShapeDtypeStruct((S, F), BF16)
    return _pcall(
        body, (dh, h, g.reshape(1, D), silu, dsilu, up, win_t, wo), name=name,
        out_shape=(jax.ShapeDtypeStruct((S, D), F32), jax.ShapeDtypeStruct((8, D), F32), hid, hid),
        grid=(S // tm,),
        in_specs=[row, row, pl.BlockSpec((1, D), lambda i: (0, 0)), wide, wide, wide, hbm, hbm],
        out_specs=(row, pl.BlockSpec((8, D), lambda i: (0, 0)), wide, wide),
        scratch_shapes=[pltpu.VMEM(win_t.shape, BF16), pltpu.VMEM(wo.shape, BF16), pltpu.SemaphoreType.DMA((2,))],
        sem=("arbitrary",), carry=carry)


def _rope_tables(S):
    half = HEAD_DIM // 2
    inv_freq = ROPE_THETA ** (-jnp.arange(half, dtype=F32) / half)
    ang = jnp.arange(S).astype(F32)[:, None] * inv_freq[None, :]
    cos, sin = jnp.cos(ang), jnp.sin(ang)
    cos_t = jnp.tile(cos, (1, LANES // half))
    sin_t = jnp.tile(jnp.concatenate([-sin, sin], axis=1), (1, LANES // HEAD_DIM))
    return cos_t, sin_t


def _swap_halves(x):
    lane = lax.broadcasted_iota(jnp.int32, x.shape, 1)
    first = (lane % HEAD_DIM) < (HEAD_DIM // 2)
    return jnp.where(first, pltpu.roll(x, LANES - HEAD_DIM // 2, 1), pltpu.roll(x, HEAD_DIM // 2, 1))


def _rotary_bwd(dy, cos_t, sin_t, n_rot, name):
    S, C = dy.shape
    ts = _tile(S, 512, 16)
    ng = C // LANES

    def body(x_ref, c_ref, s_ref, o_ref):
        cs, sn = c_ref[...], s_ref[...]
        for gidx in range(ng):
            sl = slice(gidx * LANES, (gidx + 1) * LANES)
            v = x_ref[:, sl].astype(F32)
            if gidx < n_rot:
                v = v * cs + _swap_halves(v * sn)
            o_ref[:, sl] = v.astype(BF16)

    row = pl.BlockSpec((ts, C), lambda i: (i, 0))
    tab = pl.BlockSpec((ts, LANES), lambda i: (i, 0))
    return pl.pallas_call(
        body, name=name, out_shape=jax.ShapeDtypeStruct((S, C), BF16),
        grid=(S // ts,), in_specs=[row, tab, tab], out_specs=row,
        compiler_params=_params("parallel"),
    )(dy, cos_t, sin_t)


def _head_masks():
    lane = lax.broadcasted_iota(jnp.int32, (BLK, LANES), 1)
    return lane < HEAD_DIM


def _split_bf16(x):
    hi = x.astype(BF16)
    lo = (x - hi.astype(F32)).astype(BF16)
    return hi, lo


def _sb_scores(qh, ks, carry, diag, tri_excl, strict):
    n_heads = len(qh)
    zs = [_dot(ks[n], qh[n], NT) for n in range(n_heads)]
    a_l, b_l, split_l = [], [], []
    for z in zs:
        a = jnp.minimum(z, 0.0) - jnp.log(1.0 + jnp.exp(-jnp.abs(z)))
        b = a - z
        if diag:
            b = jnp.where(strict, b, 0.0)
        a_l.append(a)
        b_l.append(b)
        split_l.append(_split_bf16(b))
    sufs = [_dot(tri_excl, hi, NN) + _dot(tri_excl, lo, NN) for hi, lo in split_l]
    w_l = []
    for n in range(n_heads):
        w = jnp.exp(a_l[n] + sufs[n] + carry[n])
        if diag:
            w = jnp.where(strict, w, 0.0)
        w_l.append(w)
    return a_l, b_l, w_l


SB_FWD_PAIRS = 4
SB_FWD_QBLOCKS = 4
SB_BWD_PAIRS = 2
SB_BWD_QBLOCKS = 4


def _any_alive(carries):
    top = carries[0]
    for c in carries[1:]:
        top = jnp.maximum(top, c)
    return (jnp.max(top) > SB_LOG_FLOOR).astype(jnp.int32)


def _sb_masks():
    row = lax.broadcasted_iota(jnp.int32, (BLK, BLK), 0)
    col = lax.broadcasted_iota(jnp.int32, (BLK, BLK), 1)
    tri_excl = jnp.where(col > row, 1.0, 0.0).astype(BF16)
    tri_incl = jnp.where(col >= row, 1.0, 0.0).astype(BF16)
    return row < HEAD_DIM, row < col, tri_excl, tri_incl


def _sb_fwd(qkv, kv_t, name, carry=None):
    S, D3 = qkv.shape
    D = D3 // 3
    npair, nb = D // LANES, S // BLK
    P = min(SB_FWD_PAIRS, npair)
    ngroup = npair // P
    W = P * LANES

    QB = SB_FWD_QBLOCKS if nb % SB_FWD_QBLOCKS == 0 else 1
    nch = QB * 2 * P

    def body(q_ref, k_ref, vt_ref, o_ref):
        i_first = pl.program_id(1) * QB
        m0 = _head_masks()
        top, strict, tri_excl, _ = _sb_masks()
        zq = jnp.zeros((BLK, LANES), BF16)
        lanes = [slice(p * LANES, (p + 1) * LANES) for p in range(P)]
        qh = []
        for qb in range(QB):
            for sl in lanes:
                q2 = q_ref[qb * BLK:(qb + 1) * BLK, sl] * ATTN_SCALE
                qh += [jnp.where(m0, q2, zq), jnp.where(m0, zq, q2)]

        def block(qbs, js, carry, acc, diag):
            offs = [pl.multiple_of(j * BLK, BLK) for j in js]
            ks, vth, qs = [], [], []
            for n_qb, qb in enumerate(qbs):
                qs += qh[qb * 2 * P:(qb + 1) * 2 * P]
                for sl in lanes:
                    k2 = k_ref[pl.ds(offs[n_qb], BLK), sl]
                    vt = vt_ref[sl, pl.ds(offs[n_qb], BLK)]
                    ks += [k2, k2]
                    vth += [jnp.where(top, vt, zq), jnp.where(top, zq, vt)]
            _, b_l, w_l = _sb_scores(qs, ks, carry, diag, tri_excl, strict)
            wb = [w.astype(BF16) for w in w_l]
            new_acc = [acc[m] + _dot(vth[2 * m], wb[2 * m], NN) + _dot(vth[2 * m + 1], wb[2 * m + 1], NN)
                       for m in range(len(qbs) * P)]
            new_carry = [carry[n] + jnp.sum(b_l[n], axis=0, keepdims=True) for n in range(len(carry))]
            return new_carry, new_acc

        every = list(range(QB))
        c0 = jnp.zeros((1, BLK), F32)
        carry, acc = block(every, [i_first + qb for qb in every], [c0] * nch,
                           [jnp.zeros((LANES, BLK), F32)] * (QB * P), True)
        carry = [jnp.where(i_first > 0, c, NEG_BIG) for c in carry[:2 * P]] + carry[2 * P:]
        carry, acc = block(every, [jnp.maximum(i_first + qb - 1, 0) for qb in every], carry, acc, False)

        for qb in range(QB):
            i_qb = i_first + qb
            sub = slice(qb * 2 * P, (qb + 1) * 2 * P)

            def cond(st):
                return jnp.logical_and(i_qb - st[0] >= 0, st[1] > 0)

            def step(st, qb=qb, i_qb=i_qb):
                t, _, c_qb, a_qb = st
                c_qb, a_qb = block([qb], [i_qb - t], c_qb, a_qb, False)
                return t + 1, _any_alive(c_qb), c_qb, a_qb

            st = lax.while_loop(cond, step, (2, _any_alive(carry[sub]), carry[sub], acc[qb * P:(qb + 1) * P]))
            for p, sl in enumerate(lanes):
                o_ref[qb * BLK:(qb + 1) * BLK, sl] = jnp.transpose(st[3][p])

    return _pcall(
        body, (qkv, qkv, kv_t), name=name, out_shape=jax.ShapeDtypeStruct((S, D), F32),
        grid=(ngroup, nb // QB),
        in_specs=[pl.BlockSpec((QB * BLK, W), lambda g, i: (i, g)),
                  pl.BlockSpec((S, W), lambda g, i: (0, ngroup + g)),
                  pl.BlockSpec((W, S), lambda g, i: (ngroup + g, 0))],
        out_specs=pl.BlockSpec((QB * BLK, W), lambda g, i: (i, g)),
        sem=("arbitrary", "arbitrary"), carry=carry)


def _sb_bwd(qkv, kv_t, o, do, name, carry=None):
    S, D3 = qkv.shape
    D = D3 // 3
    npair, nb = D // LANES, S // BLK
    P = min(SB_BWD_PAIRS, npair)
    ngroup = npair // P
    W = P * LANES

    QB = SB_BWD_QBLOCKS if nb % SB_BWD_QBLOCKS == 0 else 1
    nch = QB * 2 * P

    def body(q_ref, o_ref, do_ref, qkv_hbm, kt_hbm, dq_ref, dk_ref, dv_ref, k_ref, v_ref, kt_ref, sems):
        grp = pl.program_id(0)
        i_first = pl.program_id(1) * QB
        m0 = _head_masks()
        top, strict, tri_excl, tri_incl = _sb_masks()
        zq = jnp.zeros((BLK, LANES), BF16)
        lanes = [slice(p * LANES, (p + 1) * LANES) for p in range(P)]

        @pl.when(pl.program_id(1) == 0)
        def _():
            copies = [pltpu.make_async_copy(qkv_hbm.at[:, pl.ds(pl.multiple_of((c * ngroup + grp) * W, LANES), W)],
                                            ref, sems.at[c - 1]) for c, ref in ((1, k_ref), (2, v_ref))]
            copies.append(pltpu.make_async_copy(kt_hbm.at[pl.ds(pl.multiple_of(grp * W, LANES), W), :],
                                                kt_ref, sems.at[2]))
            for cp in copies:
                cp.start()
            dk_ref[...] = jnp.zeros_like(dk_ref)
            dv_ref[...] = jnp.zeros_like(dv_ref)
            for cp in copies:
                cp.wait()

        qh, doh, delta = [], [], []
        for qb in range(QB):
            rs = slice(qb * BLK, (qb + 1) * BLK)
            for sl in lanes:
                q2, do2 = q_ref[rs, sl] * ATTN_SCALE, do_ref[rs, sl]
                qh += [jnp.where(m0, q2, zq), jnp.where(m0, zq, q2)]
                doh += [jnp.where(m0, do2, zq), jnp.where(m0, zq, do2)]
                prod_t = jnp.transpose(do2.astype(F32) * o_ref[rs, sl])
                delta += [jnp.sum(jnp.where(top, prod_t, 0.0), axis=0, keepdims=True),
                          jnp.sum(jnp.where(top, 0.0, prod_t), axis=0, keepdims=True)]

        def block(qbs, js, valid, cb, cg, dq, diag):
            offs = [pl.multiple_of(j * BLK, BLK) for j in js]
            n_ch = len(qbs) * 2 * P
            ks, vs, kth, qs, dos, dls = [], [], [], [], [], []
            for n_qb, qb in enumerate(qbs):
                chains = slice(qb * 2 * P, (qb + 1) * 2 * P)
                qs, dos, dls = qs + qh[chains], dos + doh[chains], dls + delta[chains]
                for sl in lanes:
                    k2, v2 = k_ref[pl.ds(offs[n_qb], BLK), sl], v_ref[pl.ds(offs[n_qb], BLK), sl]
                    ks += [k2, k2]
                    vs += [v2, v2]
                    kt = kt_ref[sl, pl.ds(offs[n_qb], BLK)] * ATTN_SCALE
                    kth += [jnp.where(top, kt, zq), jnp.where(top, zq, kt)]
            dws = [_dot(vs[n], dos[n], NT) for n in range(n_ch)]
            a_l, b_l, w_l = _sb_scores(qs, ks, cb, diag, tri_excl, strict)
            wb = [w.astype(BF16) for w in w_l]
            g_l = [dws[n] * wb[n].astype(F32) for n in range(n_ch)]
            gsplit = [_split_bf16(g) for g in g_l]
            gincs = [_dot(tri_incl, hi, NN) + _dot(tri_incl, lo, NN) for hi, lo in gsplit]
            dzs = []
            for n in range(n_ch):
                beta = jnp.exp(a_l[n])
                dz = g_l[n] - beta * (g_l[n] + ((dls[n] - cg[n]) - gincs[n]))
                if diag:
                    dz = jnp.where(strict, dz, 0.0)
                if valid[n // (2 * P)] is not None:
                    dz = jnp.where(valid[n // (2 * P)], dz, 0.0)
                dzs.append(dz.astype(BF16))
            ndq = []
            for n_qb in range(len(qbs)):
                for p, sl in enumerate(lanes):
                    n0 = n_qb * 2 * P + 2 * p
                    ndq.append(dq[n_qb * P + p] + _dot(kth[n0], dzs[n0], NN) + _dot(kth[n0 + 1], dzs[n0 + 1], NN))
                    dk_ref[pl.ds(offs[n_qb], BLK), sl] += _dot(dzs[n0], qs[n0], NN) + _dot(dzs[n0 + 1], qs[n0 + 1], NN)
                    dv_ref[pl.ds(offs[n_qb], BLK), sl] += _dot(wb[n0], dos[n0], NN) + _dot(wb[n0 + 1], dos[n0 + 1], NN)
            ncb = [cb[n] + jnp.sum(b_l[n], axis=0, keepdims=True) for n in range(n_ch)]
            ncg = [cg[n] + jnp.sum(g_l[n], axis=0, keepdims=True) for n in range(n_ch)]
            return ncb, ncg, ndq

        every = list(range(QB))
        c0 = jnp.zeros((1, BLK), F32)
        cb, cg, dq = block(every, [i_first + qb for qb in every], [None] * QB, [c0] * nch, [c0] * nch,
                           [jnp.zeros((LANES, BLK), F32)] * (QB * P), True)
        has_prev = i_first > 0
        cb = [jnp.where(has_prev, c, NEG_BIG) for c in cb[:2 * P]] + cb[2 * P:]
        cb, cg, dq = block(every, [jnp.maximum(i_first + qb - 1, 0) for qb in every], [has_prev] + [None] * (QB - 1),
                           cb, cg, dq, False)

        for qb in range(QB):
            i_qb = i_first + qb
            sub = slice(qb * 2 * P, (qb + 1) * 2 * P)

            def cond(st):
                return jnp.logical_and(i_qb - st[0] >= 0, st[1] > 0)

            def step(st, qb=qb, i_qb=i_qb):
                t, _, b_qb, g_qb, dq_qb = st
                b_qb, g_qb, dq_qb = block([qb], [i_qb - t], [None], b_qb, g_qb, dq_qb, False)
                return t + 1, _any_alive(b_qb), b_qb, g_qb, dq_qb

            st = lax.while_loop(cond, step, (2, _any_alive(cb[sub]), cb[sub], cg[sub], dq[qb * P:(qb + 1) * P]))
            for p, sl in enumerate(lanes):
                dq_ref[qb * BLK:(qb + 1) * BLK, sl] = jnp.transpose(st[4][p]).astype(BF16)

    blk = pl.BlockSpec((QB * BLK, W), lambda g, i: (i, g))
    col_all = pl.BlockSpec((S, W), lambda g, i: (0, g))
    hbm = pl.BlockSpec(memory_space=pl.ANY)
    return _pcall(
        body, (qkv, o, do, qkv, kv_t), name=name,
        out_shape=(jax.ShapeDtypeStruct((S, D), BF16), jax.ShapeDtypeStruct((S, D), F32),
                   jax.ShapeDtypeStruct((S, D), F32)),
        grid=(ngroup, nb // QB),
        in_specs=[blk, blk, blk, hbm, hbm],
        out_specs=(blk, col_all, col_all),
        scratch_shapes=[pltpu.VMEM((S, W), BF16), pltpu.VMEM((S, W), BF16), pltpu.VMEM((W, S), BF16),
                        pltpu.SemaphoreType.DMA((3,))],
        sem=("arbitrary", "arbitrary"), carry=carry)


SWA_Q_GROUPS = 4


def _roll_heads(x):
    return pltpu.roll(x.astype(F32), HEAD_DIM, 1).astype(BF16)


def _roll_rows(x):
    return pltpu.roll(x.astype(F32), HEAD_DIM, 0).astype(BF16)


def _swa_valid(i):
    k = lax.broadcasted_iota(jnp.int32, (2 * BLK, BLK), 0)
    q = lax.broadcasted_iota(jnp.int32, (2 * BLK, BLK), 1)
    diff = q + BLK - k
    return (diff >= 0) & (diff < BLK) & ((i > 0) | (k >= BLK))


def _swa_probs(z, valid, sink):
    z = jnp.where(valid, z * ATTN_SCALE, NEG_BIG)
    mx = jnp.maximum(jnp.max(z, axis=0, keepdims=True), sink)
    p = jnp.exp(z - mx)
    ps = jnp.exp(sink - mx)
    inv = 1.0 / (jnp.sum(p, axis=0, keepdims=True) + ps)
    return p * inv, ps * inv


def _swa_operands(q_ref, kc_ref, kp_ref, vc_ref, vp_ref, tc_ref, tp_ref, s_ref, nkvp):
    m0 = _head_masks()
    top = lax.broadcasted_iota(jnp.int32, (LANES, 2 * BLK), 0) < HEAD_DIM
    heads = []
    for m in range(nkvp):
        pair = slice(m * LANES, (m + 1) * LANES)
        kk = jnp.concatenate([kp_ref[:, pair], kc_ref[:, pair]], axis=0)
        vv = jnp.concatenate([vp_ref[:, pair], vc_ref[:, pair]], axis=0)
        tt = jnp.concatenate([tp_ref[pair, :], tc_ref[pair, :]], axis=1)
        ksw, vsw, tsw = _roll_heads(kk), _roll_heads(vv), _roll_rows(tt)
        zt = jnp.zeros_like(tt)
        for c in range(SWA_Q_GROUPS):
            q_lanes = slice((m * SWA_Q_GROUPS + c) * LANES, (m * SWA_Q_GROUPS + c + 1) * LANES)
            qc = q_ref[:, q_lanes]
            zq = jnp.zeros_like(qc)
            for u in range(2):
                same = u == c // 2
                sel = (lambda x, z, mk: jnp.where(mk, x, z)) if u == 0 else (lambda x, z, mk: jnp.where(mk, z, x))
                heads.append(dict(
                    m=m, q_lanes=q_lanes, same=same, sel=sel, qm=sel(qc, zq, m0),
                    k=kk if same else ksw, v=vv if same else vsw,
                    tm=sel(tt if same else tsw, zt, top),
                    sink=s_ref[0, (m * SWA_Q_GROUPS + c) * 2 + u]))
    return heads, m0


def _swa_specs(D, half, t_block):
    prev = lambda i: jnp.maximum(i - 1, 0)
    return [pl.BlockSpec((BLK, D), lambda i: (i, 0)),
            pl.BlockSpec((BLK, half), lambda i: (i, 0)),
            pl.BlockSpec((BLK, half), lambda i: (prev(i), 0)),
            pl.BlockSpec((BLK, half), lambda i: (i, 1)),
            pl.BlockSpec((BLK, half), lambda i: (prev(i), 1)),
            pl.BlockSpec((half, BLK), lambda i: (t_block, i)),
            pl.BlockSpec((half, BLK), lambda i: (t_block, prev(i))),
            pl.BlockSpec(memory_space=pltpu.SMEM)]


def _swa_fwd(q, kv, kv_t, sinks, name):
    S, D = q.shape
    half = kv.shape[1] // 2
    nkvp = half // LANES

    def body(q_ref, kc_ref, kp_ref, vc_ref, vp_ref, tc_ref, tp_ref, s_ref, o_ref):
        valid = _swa_valid(pl.program_id(0))
        heads, _ = _swa_operands(q_ref, kc_ref, kp_ref, vc_ref, vp_ref, tc_ref, tp_ref, s_ref, nkvp)
        zs = [_dot(hd["k"], hd["qm"], NT) for hd in heads]
        ps = [_swa_probs(z, valid, hd["sink"])[0].astype(BF16) for z, hd in zip(zs, heads)]
        for n in range(0, len(heads), 2):
            o_t = _dot(heads[n]["tm"], ps[n], NN) + _dot(heads[n + 1]["tm"], ps[n + 1], NN)
            o_ref[:, heads[n]["q_lanes"]] = jnp.transpose(o_t)

    return pl.pallas_call(
        body, name=name, out_shape=jax.ShapeDtypeStruct((S, D), F32),
        grid=(S // BLK,),
        in_specs=_swa_specs(D, half, 1),
        out_specs=pl.BlockSpec((BLK, D), lambda i: (i, 0)),
        compiler_params=_params("arbitrary"),
    )(q, kv, kv, kv, kv, kv_t, kv_t, sinks)


def _swa_bwd(q, kv, kv_t, sinks, o, do, cos_t, sin_t, name, carry=None):
    S, D = q.shape
    half = kv.shape[1] // 2
    nkvp = half // LANES
    nh = nkvp * 2 * SWA_Q_GROUPS

    def body(q_ref, kc_ref, kp_ref, vc_ref, vp_ref, tc_ref, tp_ref, s_ref, o_ref, do_ref, c_ref, sn_ref,
             dq_ref, dk_ref, dv_ref, ds_ref):
        i = pl.program_id(0)
        valid = _swa_valid(i)
        heads, m0 = _swa_operands(q_ref, kc_ref, kp_ref, vc_ref, vp_ref, tc_ref, tp_ref, s_ref, nkvp)
        top_q = lax.broadcasted_iota(jnp.int32, (LANES, BLK), 0) < HEAD_DIM

        @pl.when(i == 0)
        def _():
            dk_ref[...] = jnp.zeros_like(dk_ref)
            dv_ref[...] = jnp.zeros_like(dv_ref)
            ds_ref[...] = jnp.zeros_like(ds_ref)

        doms, deltas = [], []
        for n in range(0, nh, 2):
            doc = do_ref[:, heads[n]["q_lanes"]]
            prod_t = jnp.transpose(doc.astype(F32) * o_ref[:, heads[n]["q_lanes"]])
            for hd in heads[n:n + 2]:
                doms.append(hd["sel"](doc, jnp.zeros_like(doc), m0))
                deltas.append(jnp.sum(hd["sel"](prod_t, 0.0, top_q), axis=0, keepdims=True))
        zs = [_dot(hd["k"], hd["qm"], NT) for hd in heads]
        dps = [_dot(hd["v"], dom, NT) for dom, hd in zip(doms, heads)]
        pbs, dscs = [], []
        for n, hd in enumerate(heads):
            p, psink = _swa_probs(zs[n], valid, hd["sink"])
            pbs.append(p.astype(BF16))
            dscs.append((p * (dps[n] - deltas[n]) * ATTN_SCALE).astype(BF16))
            ds_ref[n:n + 1, :] += -(psink * deltas[n])
        for n in range(0, nh, 2):
            dq_rot = jnp.transpose(_dot(heads[n]["tm"], dscs[n], NN) + _dot(heads[n + 1]["tm"], dscs[n + 1], NN))
            dq_ref[:, heads[n]["q_lanes"]] = (
                dq_rot * c_ref[...] + _swap_halves(dq_rot * sn_ref[...])).astype(BF16)
        acc = {}
        for n, hd in enumerate(heads):
            dk_n = _dot(dscs[n], hd["qm"], NN)
            dv_n = _dot(pbs[n], doms[n], NN)
            for key, val in ((("k", hd["m"], hd["same"]), dk_n), (("v", hd["m"], hd["same"]), dv_n)):
                acc[key] = val if key not in acc else acc[key] + val
        poff = pl.multiple_of(jnp.maximum(i - 1, 0) * BLK, BLK)
        coff = pl.multiple_of(i * BLK, BLK)
        for m in range(nkvp):
            pair = slice(m * LANES, (m + 1) * LANES)
            dkk = acc["k", m, True] + pltpu.roll(acc["k", m, False], HEAD_DIM, 1)
            dvv = acc["v", m, True] + pltpu.roll(acc["v", m, False], HEAD_DIM, 1)
            dk_ref[pl.ds(poff, BLK), pair] += dkk[:BLK]
            dv_ref[pl.ds(poff, BLK), pair] += dvv[:BLK]
            dk_ref[pl.ds(coff, BLK), pair] += dkk[BLK:]
            dv_ref[pl.ds(coff, BLK), pair] += dvv[BLK:]

    qblk = pl.BlockSpec((BLK, D), lambda i: (i, 0))
    whole = pl.BlockSpec((S, half), lambda i: (0, 0))
    tab = pl.BlockSpec((BLK, LANES), lambda i: (i, 0))
    return _pcall(
        body, (q, kv, kv, kv, kv, kv_t, kv_t, sinks, o, do, cos_t, sin_t), name=name,
        out_shape=(jax.ShapeDtypeStruct((S, D), BF16),
                   jax.ShapeDtypeStruct((S, half), F32),
                   jax.ShapeDtypeStruct((S, half), F32),
                   jax.ShapeDtypeStruct((nh, LANES), F32)),
        grid=(S // BLK,),
        in_specs=_swa_specs(D, half, 0) + [qblk, qblk, tab, tab],
        out_specs=(qblk, whole, whole, pl.BlockSpec((nh, LANES), lambda i: (0, 0))),
        sem=("arbitrary",), carry=carry)


def _dev_index(p):
    return 4 * p[0] + 2 * p[1] + p[2]


def _gather_plan(x_refs, out_refs, send_sems, recv_sems, local_sems):
    n = len(x_refs)
    x_, y_, c_ = lax.axis_index("x"), lax.axis_index("y"), lax.axis_index("c")
    me, sibling = (x_, y_, c_), (x_, y_, 1 - c_)
    chips = [(1 - x_, y_), (x_, 1 - y_), (1 - x_, 1 - y_)]

    def copy(t, k, block, to, src=None):
        dst = out_refs[t].at[_dev_index(block)]
        return pltpu.make_async_remote_copy(
            src_ref=dst if src is None else src, dst_ref=dst,
            send_sem=send_sems.at[7 * t + k], recv_sem=recv_sems.at[7 * t + k],
            device_id=to, device_id_type=MESH)

    mine = [pltpu.make_async_copy(x_refs[t], out_refs[t].at[_dev_index(me)], local_sems.at[t]) for t in range(n)]
    first = []
    for t in range(n):
        first.append(copy(t, 0, me, sibling, src=x_refs[t]))
        first += [copy(t, 1 + j, me, (*chip, c_), src=x_refs[t]) for j, chip in enumerate(chips)]
    arrived = lambda t, j: copy(t, 1 + j, (*chips[j], c_), me)
    forward = lambda t, j: copy(t, 4 + j, (*chips[j], c_), sibling)
    from_sibling = lambda t: copy(t, 0, sibling, me)
    forwarded = lambda t, j: copy(t, 4 + j, (*chips[j], 1 - c_), me)
    return n, mine, first, arrived, forward, from_sibling, forwarded


def _gather_start(x_refs, out_refs, send_sems, recv_sems, local_sems):
    _, mine, first, *_ = _gather_plan(x_refs, out_refs, send_sems, recv_sems, local_sems)
    for cp in mine + first:
        cp.start()


def _gather_forward(x_refs, out_refs, send_sems, recv_sems, local_sems):
    n, _, _, arrived, forward, _, _ = _gather_plan(x_refs, out_refs, send_sems, recv_sems, local_sems)
    for j in range(3):
        for t in range(n):
            arrived(t, j).wait_recv()
            forward(t, j).start()


def _gather_finish(x_refs, out_refs, send_sems, recv_sems, local_sems):
    n, mine, first, _, forward, from_sibling, forwarded = _gather_plan(
        x_refs, out_refs, send_sems, recv_sems, local_sems)
    for t in range(n):
        from_sibling(t).wait_recv()
    for j in range(3):
        for t in range(n):
            forwarded(t, j).wait_recv()
    for cp in first + [forward(t, j) for j in range(3) for t in range(n)]:
        cp.wait_send()
    for cp in mine:
        cp.wait()


def _scatter_plan(b_refs, out_refs, send_sems, recv_sems, local_sems):
    n = len(b_refs)
    x_, y_, c_ = lax.axis_index("x"), lax.axis_index("y"), lax.axis_index("c")
    my_idx = _dev_index((x_, y_, c_))
    mine = [pltpu.make_async_copy(b_refs[t].at[my_idx], out_refs[t].at[my_idx], local_sems.at[t]) for t in range(n)]
    copies = []
    for t in range(n):
        for k in range(1, N_DEV):
            peer = (x_ ^ ((k >> 2) & 1), y_ ^ ((k >> 1) & 1), c_ ^ (k & 1))
            copies.append(pltpu.make_async_remote_copy(
                src_ref=b_refs[t].at[_dev_index(peer)], dst_ref=out_refs[t].at[my_idx],
                send_sem=send_sems.at[7 * t + k - 1], recv_sem=recv_sems.at[7 * t + k - 1],
                device_id=peer, device_id_type=MESH))
    return mine, copies


def _scatter_start(b_refs, out_refs, send_sems, recv_sems, local_sems):
    mine, copies = _scatter_plan(b_refs, out_refs, send_sems, recv_sems, local_sems)
    for cp in mine + copies:
        cp.start()


def _scatter_finish(b_refs, out_refs, send_sems, recv_sems, local_sems):
    mine, copies = _scatter_plan(b_refs, out_refs, send_sems, recv_sems, local_sems)
    for cp in copies:
        cp.wait_recv()
    for cp in copies:
        cp.wait_send()
    for cp in mine:
        cp.wait()


def _exchange_operands(kind, tensors):
    if kind == "gather":
        args = list(tensors)
        shapes = [jax.ShapeDtypeStruct((N_DEV,) + t.shape, t.dtype) for t in tensors]
        return args, shapes, (_gather_start, _gather_forward, _gather_finish)
    args = [t.reshape(N_DEV, t.shape[0] // N_DEV, t.shape[1]) for t in tensors]
    shapes = [jax.ShapeDtypeStruct(a.shape, a.dtype) for a in args]
    return args, shapes, (_scatter_start, None, _scatter_finish)


def _exchange_results(kind, tensors, res):
    if kind == "gather":
        return [r.reshape(N_DEV * t.shape[0], t.shape[1]) for r, t in zip(res, tensors)]
    return list(res)


def _exchange_sems(n):
    return [pltpu.SemaphoreType.DMA((7 * n,)), pltpu.SemaphoreType.DMA((7 * n,)), pltpu.SemaphoreType.DMA((n,))]


def _exchange(kind, tensors, name):
    n = len(tensors)
    args, shapes, phases = _exchange_operands(kind, tensors)

    def body(*refs):
        for phase in phases:
            if phase is not None:
                phase(refs[:n], refs[n:2 * n], *refs[2 * n:])

    hbm = pl.BlockSpec(memory_space=pl.ANY)
    res = pl.pallas_call(body, name=name, out_shape=shapes, in_specs=[hbm] * n, out_specs=[hbm] * n,
                         scratch_shapes=_exchange_sems(n))(*args)
    return _exchange_results(kind, tensors, res)


def _pcall(body, args, *, name, out_shape, grid, in_specs, out_specs, sem, scratch_shapes=(), carry=None):
    if carry is None:
        out = pl.pallas_call(body, name=name, out_shape=out_shape, grid=grid, in_specs=list(in_specs),
                             out_specs=out_specs, scratch_shapes=list(scratch_shapes),
                             compiler_params=_params(*sem))(*args)
        return out, None
    kind, tensors = carry
    multi = isinstance(out_shape, (tuple, list))
    shapes = list(out_shape) if multi else [out_shape]
    ospecs = list(out_specs) if multi else [out_specs]
    n_in, n_out, n_scr, n_c = len(in_specs), len(shapes), len(scratch_shapes), len(tensors)
    c_args, c_shapes, (start, forward, finish) = _exchange_operands(kind, tensors)
    n_steps = 1
    for g in grid:
        n_steps *= g
    late = (3 * n_steps) // 4

    def wrapped(*refs):
        ins, rest = refs[:n_in], refs[n_in:]
        c_in, rest = rest[:n_c], rest[n_c:]
        outs, rest = rest[:n_out], rest[n_out:]
        c_out, rest = rest[:n_c], rest[n_c:]
        scr, sems = rest[:n_scr], rest[n_scr:]
        step = pl.program_id(0)
        for a in range(1, len(grid)):
            step = step * grid[a] + pl.program_id(a)

        @pl.when(step == 0)
        def _():
            start(c_in, c_out, *sems)

        body(*ins, *outs, *scr)

        if forward is not None:
            @pl.when(step == late)
            def _():
                forward(c_in, c_out, *sems)

        @pl.when(step == n_steps - 1)
        def _():
            finish(c_in, c_out, *sems)

    hbm = pl.BlockSpec(memory_space=pl.ANY)
    res = pl.pallas_call(
        wrapped, name=name, out_shape=shapes + c_shapes, grid=grid,
        in_specs=list(in_specs) + [hbm] * n_c, out_specs=ospecs + [hbm] * n_c,
        scratch_shapes=list(scratch_shapes) + _exchange_sems(n_c),
        compiler_params=_params(*sem))(*args, *c_args)
    outs = tuple(res[:n_out]) if multi else res[0]
    return outs, _exchange_results(kind, tensors, res[n_out:])


def _sum8(parts, name):
    _, R, C = parts.shape
    tr = _tile(R, 256, 16)

    def body(p_ref, g_ref):
        g = p_ref[0].astype(F32)
        for s in range(1, N_DEV):
            g = g + p_ref[s].astype(F32)
        g_ref[...] = g

    return pl.pallas_call(
        body, name=name, out_shape=jax.ShapeDtypeStruct((R, C), F32),
        grid=(R // tr,),
        in_specs=[pl.BlockSpec((N_DEV, tr, C), lambda i: (0, i, 0))],
        out_specs=pl.BlockSpec((tr, C), lambda i: (i, 0)),
        compiler_params=_params("parallel"),
    )(parts)


def _adamw(g, w, m, v, name):
    R, C = g.shape
    tr = _tile(R, 256, 8)
    c1 = 1.0 - ADAM_B1 ** ADAM_STEP
    c2 = 1.0 - ADAM_B2 ** ADAM_STEP

    def body(g_ref, w_ref, m_ref, v_ref, d_ref, nm_ref, nv_ref):
        gg = g_ref[...]
        nm = ADAM_B1 * m_ref[...] + (1.0 - ADAM_B1) * gg
        nv = ADAM_B2 * v_ref[...] + (1.0 - ADAM_B2) * (gg * gg)
        m_hat = nm / c1
        v_hat = nv / c2
        nm_ref[...] = nm
        nv_ref[...] = nv
        d_ref[...] = -ADAM_LR * (m_hat / (jnp.sqrt(v_hat) + ADAM_EPS) + ADAM_WD * w_ref[...])

    row = pl.BlockSpec((tr, C), lambda i: (i, 0))
    shp = jax.ShapeDtypeStruct((R, C), F32)
    return pl.pallas_call(
        body, name=name, out_shape=(shp, shp, shp),
        grid=(R // tr,), in_specs=[row, row, row, row], out_specs=(row, row, row),
        compiler_params=_params("parallel"),
    )(g, w, m, v)


def _sum_adamw(parts_list, w, m, v, name):
    L, R, C = w.shape
    tr = _tile(R, 256, 16)
    c1 = 1.0 - ADAM_B1 ** ADAM_STEP
    c2 = 1.0 - ADAM_B2 ** ADAM_STEP

    def body(*refs):
        p_refs = refs[:L]
        w_ref, m_ref, v_ref, g_ref, d_ref, nm_ref, nv_ref = refs[L:]
        for layer in range(L):
            @pl.when(pl.program_id(0) == layer)
            def _():
                g = p_refs[layer][0].astype(F32)
                for s in range(1, N_DEV):
                    g = g + p_refs[layer][s].astype(F32)
                nm = ADAM_B1 * m_ref[0] + (1.0 - ADAM_B1) * g
                nv = ADAM_B2 * v_ref[0] + (1.0 - ADAM_B2) * (g * g)
                g_ref[0] = g
                nm_ref[0] = nm
                nv_ref[0] = nv
                d_ref[0] = -ADAM_LR * ((nm / c1) / (jnp.sqrt(nv / c2) + ADAM_EPS) + ADAM_WD * w_ref[0])

    def parts_spec(layer):
        return pl.BlockSpec((N_DEV, tr, C), lambda l, i: (0, jnp.where(l == layer, i, 0), 0))

    blk = pl.BlockSpec((1, tr, C), lambda l, i: (l, i, 0))
    shp = jax.ShapeDtypeStruct((L, R, C), F32)
    return pl.pallas_call(
        body, name=name, out_shape=(shp, shp, shp, shp),
        grid=(L, R // tr),
        in_specs=[parts_spec(layer) for layer in range(L)] + [blk, blk, blk],
        out_specs=(blk, blk, blk, blk),
        compiler_params=_params("arbitrary", "arbitrary"),
    )(*parts_list, w, m, v)


def _ffn_down(act, wo, h, tag):
    return _mm(act, wo, NN, F32, f"{tag}_down", scale=FFN_RES_SCALE, res=h, tm=512, tn=1024, tk=2816)


def _ffn_fwd(h, g, win_t, wo, tag, carry=None, loss=None):
    return _ffn_fwd_fused(h, g, win_t, wo, f"{tag}_fwd", carry=carry, loss=loss)


def _ffn_bwd(dh, h, g, win_t, wo, saved, tag, scatter=False, carry=None):
    xn, silu, dsilu, up, act = saved
    dwo = _mm(act, dh, TN, BF16, f"{tag}_dwo", scale=FFN_RES_SCALE, tm=1408, tn=1024, tk=TN_CHUNK)
    if not scatter:
        (dh_in, dg, dgate, dup), got = _ffn_bwd_fused(dh, h, g, win_t, wo, silu, dsilu, up, f"{tag}_bwd", carry=carry)
        dwin_t, _ = _dw_rows([dgate, dup], xn, f"{tag}_dwin")
        return dh_in, dg, dwin_t, dwo, got
    dgate, dup = _ffn_dact(dh, wo, silu, dsilu, up, f"{tag}_dact")
    dwin_t, got_wo = _dw_rows([dgate, dup], xn, f"{tag}_dwin", carry=("scatter", [dwo]))
    (dh_in, dg), got_win = _dx_norm_bwd([(dgate, win_t, NN, 2, 0), (dup, win_t, NN, 2, 1)], h, g, dh, f"{tag}_dx",
                                        carry=("scatter", [dwin_t]))
    return dh_in, dg, got_win[0], got_wo[0]


def _proj(a, w, dims, out_dtype, name, res=None):
    return _mm(a, w, dims, out_dtype, name, res=res, tm=1024, tn=1024, tk=1024)


def _proj_dw(x, dy, name):
    return _mm(x, dy, TN, BF16, name, tm=1024, tn=1024, tk=TN_CHUNK)


def kernel(x, ffn1_norm, ffn1_w_in, ffn1_w_out, mix_norm, ffn2_norm, ffn2_w_in, ffn2_w_out, sb_w_qkv, sb_w_o, kv_norm, kv_w, swa_w_q, swa_sinks, swa_w_o, final_norm, loss_target, m_ffn1_norm, m_ffn1_w_in, m_ffn1_w_out, m_mix_norm, m_ffn2_norm, m_ffn2_w_in, m_ffn2_w_out, m_sb_w_qkv, m_sb_w_o, m_kv_norm, m_kv_w, m_swa_w_q, m_swa_sinks, m_swa_w_o, m_final_norm, v_ffn1_norm, v_ffn1_w_in, v_ffn1_w_out, v_mix_norm, v_ffn2_norm, v_ffn2_w_in, v_ffn2_w_out, v_sb_w_qkv, v_sb_w_o, v_kv_norm, v_kv_w, v_swa_w_q, v_swa_sinks, v_swa_w_o, v_final_norm):
    S, D = x.shape[1], x.shape[2]
    L = ffn1_w_in.shape[0]
    KV = kv_w.shape[1]
    assert L == 2 and swa_sinks.shape == (1, 2 * SWA_Q_GROUPS * KV // (2 * LANES))

    def bf(w):
        return w.astype(BF16)

    def bft(w):
        return jnp.transpose(w).astype(BF16)

    cos_t, sin_t = _rope_tables(S)
    h0 = x.reshape(S, D)
    tgt = loss_target.reshape(S, D)

    win1a_t, = _exchange("gather", [bft(ffn1_w_in[0])], "gather_first_weight")
    sv_a1, (wo1a, wqkv_t, w_sbo) = _ffn_up(
        h0, ffn1_norm[0], win1a_t, "ffn1a_up",
        carry=("gather", [bf(ffn1_w_out[0]), bft(sb_w_qkv[0]), bf(sb_w_o[0])]))
    h1 = _ffn_down(sv_a1[-1], wo1a, h0, "ffn1a")
    hn_a, qkv, kv_t = _norm_proj(h1, mix_norm[0], wqkv_t, NT, "sb_qkv", tail_t=2 * D)
    o_sb, (win2a_t, wo2a, w_kv) = _sb_fwd(qkv, kv_t, "sb_attn", carry=("gather", [
        bft(ffn2_w_in[0]), bf(ffn2_w_out[0]), bf(kv_w)]))
    h2 = _proj(o_sb, w_sbo, NN, F32, "sb_out", res=h1)
    h3, sv_a2, (win1b_t, wo1b, w_q, w_swo) = _ffn_fwd(h2, ffn2_norm[0], win2a_t, wo2a, "ffn2a", carry=("gather", [
        bft(ffn1_w_in[1]), bf(ffn1_w_out[1]), bf(swa_w_q[0]), bf(swa_w_o[0])]))
    kvn, kv_rot, kv_rot_t = _norm_proj(h3, kv_norm, w_kv, NN, "kv_proj", rope=(cos_t, sin_t, KV // (2 * LANES)),
                                       tail_t=KV)
    h4, sv_b1, (win2b_t, wo2b) = _ffn_fwd(h3, ffn1_norm[1], win1b_t, wo1b, "ffn1b", carry=("gather", [
        bft(ffn2_w_in[1]), bf(ffn2_w_out[1])]))
    hn_b, q_rot = _norm_proj(h4, mix_norm[1], w_q, NN, "swa_q", rope=(cos_t, sin_t, D // LANES))
    o_sw = _swa_fwd(q_rot, kv_rot, kv_rot_t, swa_sinks, "swa_attn")
    h5 = _proj(o_sw, w_swo, NN, F32, "swa_out", res=h4)
    (dh6, dg_final, sq_err), sv_b2, _ = _ffn_fwd(h5, ffn2_norm[1], win2b_t, wo2b, "ffn2b", loss=(final_norm, tgt))
    loss_local = 0.5 * jnp.sum(sq_err) / D

    dh5, dg_f2b, dwin2b_t, dwo2b, _ = _ffn_bwd(dh6, h5, ffn2_norm[1], win2b_t, wo2b, sv_b2, "ffn2b")
    do_sw = _proj(dh5, w_swo, NT, BF16, "swa_out_dx")
    dw_swo = _proj_dw(o_sw, dh5, "swa_out_dw")
    (dq, dk_sw, dv_sw, dsink), (p_win2b, p_swo) = _swa_bwd(
        q_rot, kv_rot, kv_rot_t, swa_sinks, o_sw, do_sw, cos_t, sin_t, "swa_attn_bwd",
        carry=("scatter", [dwin2b_t, dw_swo]))
    dw_q = _proj_dw(hn_b, dq, "swa_q_dw")
    (dh4, dg_mix_b), _ = _dx_norm_bwd([(dq, w_q, NT, 1, 0)], h4, mix_norm[1], dh5, "swa_q_dx", tm=512)
    dh3, dg_f1b, dwin1b_t, dwo1b, _ = _ffn_bwd(dh4, h3, ffn1_norm[1], win1b_t, wo1b, sv_b1, "ffn1b")
    dkv = _rotary_bwd(jnp.concatenate([dk_sw, dv_sw], axis=1), cos_t, sin_t, KV // (2 * LANES), "kv_rope_bwd")
    dw_kv = _proj_dw(kvn, dkv, "kv_proj_dw")
    (dh3, dg_kv), _ = _dx_norm_bwd([(dkv, w_kv, NT, 1, 0)], h3, kv_norm, dh3, "kv_proj_dx", tm=512)
    dh2, dg_f2a, dwin2a_t, dwo2a, (p_win1b, p_kv) = _ffn_bwd(
        dh3, h2, ffn2_norm[0], win2a_t, wo2a, sv_a2, "ffn2a", carry=("scatter", [dwin1b_t, dw_kv]))
    do_sb = _proj(dh2, w_sbo, NT, BF16, "sb_out_dx")
    dw_sbo = _proj_dw(o_sb, dh2, "sb_out_dw")
    (dq_sb, dk_sb, dv_sb), (p_win2a, p_wo2a, p_sbo, p_wo1b, p_q, p_wo2b) = _sb_bwd(
        qkv, kv_t, o_sb, do_sb, "sb_attn_bwd", carry=("scatter", [dwin2a_t, dwo2a, dw_sbo, dwo1b, dw_q, dwo2b]))
    dqkv = [dq_sb, dk_sb, dv_sb]
    dwqkv_t, _ = _dw_rows(dqkv, hn_a, "sb_qkv_dw", tk=TN_CHUNK // 2)
    (dh1, dg_mix_a), (p_qkv,) = _dx_norm_bwd([(dy, wqkv_t, NN, 3, n) for n, dy in enumerate(dqkv)], h1, mix_norm[0],
                                             dh2, "sb_qkv_dx", carry=("scatter", [dwqkv_t]), tm=512)
    dx, dg_f1a, p_win1a, p_wo1a = _ffn_bwd(dh1, h0, ffn1_norm[0], win1a_t, wo1a, sv_a1, "ffn1a", scatter=True)

    def from_t(parts, tag):
        return jnp.transpose(_sum8(parts, f"sum_{tag}"))

    grads = {
        "ffn1_w_in": jnp.stack([from_t(p_win1a, "win1a"), from_t(p_win1b, "win1b")]),
        "ffn2_w_in": jnp.stack([from_t(p_win2a, "win2a"), from_t(p_win2b, "win2b")]),
        "sb_w_qkv": from_t(p_qkv, "qkv")[None],
    }
    row_parts = {"ffn1_w_out": [p_wo1a, p_wo1b], "ffn2_w_out": [p_wo2a, p_wo2b], "sb_w_o": [p_sbo],
                 "kv_w": [p_kv], "swa_w_q": [p_q], "swa_w_o": [p_swo]}

    small_w = [ffn1_norm, mix_norm, ffn2_norm, kv_norm, final_norm, swa_sinks]
    small_m = [m_ffn1_norm, m_mix_norm, m_ffn2_norm, m_kv_norm, m_final_norm, m_swa_sinks]
    small_v = [v_ffn1_norm, v_mix_norm, v_ffn2_norm, v_kv_norm, v_final_norm, v_swa_sinks]
    SMALL_ROWS = 16

    def pack_small(ts):
        rows_ = [t.reshape(-1, D) for t in ts[:-1]]
        sink_row = jnp.pad(ts[-1].reshape(1, -1), ((0, 0), (0, D - ts[-1].size)))
        flat = jnp.concatenate(rows_ + [sink_row], axis=0)
        return jnp.pad(flat, ((0, SMALL_ROWS - flat.shape[0]), (0, 0)))

    def unpack_small(flat):
        out, r = [], 0
        for t in small_w[:-1]:
            n = t.size // D
            out.append(flat[r:r + n].reshape(t.shape))
            r += n
        out.append(flat[r, :swa_sinks.size].reshape(swa_sinks.shape))
        return out

    def gain(parts8):
        return jnp.sum(parts8, axis=0, keepdims=True)

    g_small_local = pack_small([
        jnp.concatenate([gain(dg_f1a), gain(dg_f1b)], axis=0),
        jnp.concatenate([gain(dg_mix_a), gain(dg_mix_b)], axis=0),
        jnp.concatenate([gain(dg_f2a), gain(dg_f2b)], axis=0),
        gain(dg_kv), gain(dg_final), jnp.sum(dsink, axis=-1).reshape(1, -1)])
    loss_row = sum(t.size for t in small_w[:-1]) // D + 1
    assert loss_row < SMALL_ROWS
    g_small_local = g_small_local.at[loss_row, 0].set(loss_local)
    small_parts = _exchange("gather", [g_small_local], "gather_small_grads")[0]
    g_small = _sum8(small_parts.reshape(N_DEV, SMALL_ROWS, D), "sum_small")
    loss = g_small[loss_row, 0]
    d_small, nm_small, nv_small = _adamw(g_small, pack_small(small_w), pack_small(small_m), pack_small(small_v), "adamw_small")
    small_names = ["ffn1_norm", "mix_norm", "ffn2_norm", "kv_norm", "final_norm", "swa_sinks"]
    result = {"grad": dict(zip(small_names, unpack_small(g_small))),
              "delta": dict(zip(small_names, unpack_small(d_small))),
              "new_m": dict(zip(small_names, unpack_small(nm_small))),
              "new_v": dict(zip(small_names, unpack_small(nv_small)))}

    big = {"ffn1_w_in": (ffn1_w_in, m_ffn1_w_in, v_ffn1_w_in), "ffn1_w_out": (ffn1_w_out, m_ffn1_w_out, v_ffn1_w_out),
           "ffn2_w_in": (ffn2_w_in, m_ffn2_w_in, v_ffn2_w_in), "ffn2_w_out": (ffn2_w_out, m_ffn2_w_out, v_ffn2_w_out),
           "sb_w_qkv": (sb_w_qkv, m_sb_w_qkv, v_sb_w_qkv), "sb_w_o": (sb_w_o, m_sb_w_o, v_sb_w_o),
           "kv_w": (kv_w, m_kv_w, v_kv_w), "swa_w_q": (swa_w_q, m_swa_w_q, v_swa_w_q),
           "swa_w_o": (swa_w_o, m_swa_w_o, v_swa_w_o)}
    for nm, (w, m, v) in big.items():
        if nm in row_parts:
            three_d = lambda t: t.reshape((len(row_parts[nm]),) + t.shape[-2:])
            g, d, new_m, new_v = _sum_adamw(row_parts[nm], three_d(w), three_d(m), three_d(v), f"adamw_{nm}")
        else:
            g = grads[nm]
            two_d = lambda t: t.reshape(-1, t.shape[-1])
            d, new_m, new_v = _adamw(two_d(g), two_d(w), two_d(m), two_d(v), f"adamw_{nm}")
        result["grad"][nm] = g.reshape(w.shape)
        result["delta"][nm] = d.reshape(w.shape)
        result["new_m"][nm] = new_m.reshape(w.shape)
        result["new_v"][nm] = new_v.reshape(w.shape)

    order = ["ffn1_norm", "ffn1_w_in", "ffn1_w_out", "mix_norm", "ffn2_norm", "ffn2_w_in", "ffn2_w_out",
             "sb_w_qkv", "sb_w_o", "kv_norm", "kv_w", "swa_w_q", "swa_sinks", "swa_w_o", "final_norm"]
    outs = [result[kind][nm] for kind in ("grad", "delta", "new_m", "new_v") for nm in order]
    return (loss, dx.reshape(x.shape), *outs)
```

```python
import jax
import jax.numpy as jnp
from jax import lax
from jax.experimental import pallas as pl
from jax.experimental.pallas import tpu as pltpu

F32 = jnp.float32
BF16 = jnp.bfloat16

N_DEV = 8
HEAD_DIM = 64
LANES = 128
BLK = 128
RMS_EPS = 1e-6
FFN_RES_SCALE = 0.5
ROPE_THETA = 10000.0
ATTN_SCALE = HEAD_DIM ** -0.5
SB_LOG_FLOOR = -88.0
NEG_BIG = -1e30
VMEM_LIMIT_V7X = 56 * 1024 * 1024

ADAM_LR = 0.001
ADAM_B1 = 0.9
ADAM_B2 = 0.999
ADAM_EPS = 1e-08
ADAM_WD = 0.01
ADAM_STEP = 10

NN = ((1,), (0,))
NT = ((1,), (1,))
TN = ((0,), (0,))
TN_CHUNK = 2048
MESH = pl.DeviceIdType.MESH


def _dot(a, b, dims):
    return lax.dot_general(a, b, (dims, ((), ())), preferred_element_type=F32)


def _tile(n, pref, mult=LANES):
    if n <= pref:
        return n
    t = (pref // mult) * mult
    while t >= mult:
        if n % t == 0:
            return t
        t -= mult
    return n


def _params(*sem):
    return pltpu.CompilerParams(dimension_semantics=sem, vmem_limit_bytes=VMEM_LIMIT_V7X)


def _mm(a, b, dims, out_dtype, name, scale=1.0, res=None, tm=512, tn=512, tk=512):
    if dims == NN:
        (M, K), (_, N) = a.shape, b.shape
    elif dims == NT:
        (M, K), (N, _) = a.shape, b.shape
    else:
        (K, M), (_, N) = a.shape, b.shape
    tm, tn, tk = _tile(M, tm), _tile(N, tn), _tile(K, tk)
    nk = K // tk
    if dims == TN:
        a_spec = pl.BlockSpec((tk, tm), lambda i, j, k: (k, i))
    else:
        a_spec = pl.BlockSpec((tm, tk), lambda i, j, k: (i, k))
    if dims == NT:
        b_spec = pl.BlockSpec((tn, tk), lambda i, j, k: (j, k))
    else:
        b_spec = pl.BlockSpec((tk, tn), lambda i, j, k: (k, j))
    o_spec = pl.BlockSpec((tm, tn), lambda i, j, k: (i, j))
    has_res = res is not None

    def body(*refs):
        a_ref, b_ref = refs[0], refs[1]
        r_ref = refs[2] if has_res else None
        o_ref = refs[3] if has_res else refs[2]

        def finish(acc):
            r = acc * scale if scale != 1.0 else acc
            if has_res:
                r = r + r_ref[...]
            o_ref[...] = r.astype(out_dtype)

        p = _dot(a_ref[...].astype(BF16), b_ref[...].astype(BF16), dims)
        if nk == 1:
            finish(p)
        else:
            acc_ref = refs[-1]
            k = pl.program_id(2)

            @pl.when(k == 0)
            def _():
                acc_ref[...] = p

            @pl.when(k > 0)
            def _():
                acc_ref[...] += p

            @pl.when(k == nk - 1)
            def _():
                finish(acc_ref[...])

    in_specs = [a_spec, b_spec] + ([o_spec] if has_res else [])
    args = (a, b) + ((res,) if has_res else ())
    return pl.pallas_call(
        body, name=name,
        out_shape=jax.ShapeDtypeStruct((M, N), out_dtype),
        grid=(M // tm, N // tn, nk),
        in_specs=in_specs, out_specs=o_spec,
        scratch_shapes=[pltpu.VMEM((tm, tn), F32)] if nk > 1 else [],
        compiler_params=_params("parallel", "parallel", "arbitrary"),
    )(*args)


def _rows8(x):
    r, d = x.shape
    return jnp.sum(x.reshape(r // 8, 8, d), axis=0)


def _norm_proj(h, g, w, dims, name, rope=None, tail_t=0):
    S, D = h.shape
    N = w.shape[1] if dims == NN else w.shape[0]
    tm = _tile(S, 512, 16)

    def body(h_ref, g_ref, w_ref, *rest):
        xn_ref, y_ref = rest[-3:-1] if tail_t else rest[-2:]
        x = h_ref[...]
        r = lax.rsqrt(jnp.mean(x * x, axis=-1, keepdims=True) + RMS_EPS)
        xn = ((x * r) * g_ref[...]).astype(BF16)
        xn_ref[...] = xn
        y = _dot(xn, w_ref[...], dims)
        if rope is not None:
            cs, sn = rest[0][...], rest[1][...]
            groups = [y[:, gidx * LANES:(gidx + 1) * LANES] for gidx in range(N // LANES)]
            y = jnp.concatenate([v * cs + _swap_halves(v) * sn if gidx < rope[2] else v
                                 for gidx, v in enumerate(groups)], axis=1)
        y_ref[...] = y.astype(BF16)
        if tail_t:
            rest[-1][...] = jnp.transpose(y[:, N - tail_t:]).astype(BF16)

    row = pl.BlockSpec((tm, D), lambda i: (i, 0))
    tab = pl.BlockSpec((tm, LANES), lambda i: (i, 0))
    in_specs = [row, pl.BlockSpec((1, D), lambda i: (0, 0)), pl.BlockSpec(w.shape, lambda i: (0, 0))]
    args = (h, g.reshape(1, D), w)
    if rope is not None:
        in_specs += [tab, tab]
        args += (rope[0], rope[1])
    out_shape = [jax.ShapeDtypeStruct((S, D), BF16), jax.ShapeDtypeStruct((S, N), BF16)]
    out_specs = [row, pl.BlockSpec((tm, N), lambda i: (i, 0))]
    if tail_t:
        out_shape.append(jax.ShapeDtypeStruct((tail_t, S), BF16))
        out_specs.append(pl.BlockSpec((tail_t, tm), lambda i: (0, i)))
    return pl.pallas_call(
        body, name=name, out_shape=out_shape, grid=(S // tm,),
        in_specs=in_specs, out_specs=out_specs,
        compiler_params=_params("parallel"),
    )(*args)


def _ffn_up(h, g, win_t, name, carry=None):
    S, D = h.shape
    F = win_t.shape[0] // 2
    tm = _tile(S, 256, 16)

    def body(h_ref, g_ref, win_hbm, xn_ref, silu_ref, dsilu_ref, up_ref, act_ref, win_v, sems):
        _load_resident([(win_hbm, win_v)], sems)
        x = h_ref[...]
        r = lax.rsqrt(jnp.mean(x * x, axis=-1, keepdims=True) + RMS_EPS)
        xn = ((x * r) * g_ref[...]).astype(BF16)
        xn_ref[...] = xn
        gate = _dot(xn, win_v[:F, :], NT)
        up = _dot(xn, win_v[F:, :], NT)
        sig = 1.0 / (1.0 + jnp.exp(-gate))
        silu = gate * sig
        up_ref[...] = up.astype(BF16)
        silu_ref[...] = silu.astype(BF16)
        dsilu_ref[...] = (sig + silu * (1.0 - sig)).astype(BF16)
        act_ref[...] = (silu * up).astype(BF16)

    row = pl.BlockSpec((tm, D), lambda i: (i, 0))
    wide = pl.BlockSpec((tm, F), lambda i: (i, 0))
    hid = jax.ShapeDtypeStruct((S, F), BF16)
    return _pcall(
        body, (h, g.reshape(1, D), win_t), name=name,
        out_shape=(jax.ShapeDtypeStruct((S, D), BF16), hid, hid, hid, hid),
        grid=(S // tm,),
        in_specs=[row, pl.BlockSpec((1, D), lambda i: (0, 0)), pl.BlockSpec(memory_space=pl.ANY)],
        out_specs=(row, wide, wide, wide, wide),
        scratch_shapes=[pltpu.VMEM(win_t.shape, BF16), pltpu.SemaphoreType.DMA((1,))],
        sem=("arbitrary",), carry=carry)


def _ffn_dact(dh, wo, silu, dsilu, up, name):
    S, D = dh.shape
    F = wo.shape[0]
    tm = _tile(S, 256, 16)

    def body(dh_ref, wo_hbm, s_ref, ds_ref, u_ref, dg_ref, du_ref, wo_v, sems):
        _load_resident([(wo_hbm, wo_v)], sems)
        d = _dot(dh_ref[...].astype(BF16), wo_v[...], NT) * FFN_RES_SCALE
        du_ref[...] = (d * s_ref[...].astype(F32)).astype(BF16)
        dg_ref[...] = (d * u_ref[...].astype(F32) * ds_ref[...].astype(F32)).astype(BF16)

    wide = pl.BlockSpec((tm, F), lambda i: (i, 0))
    hid = jax.ShapeDtypeStruct((S, F), BF16)
    return pl.pallas_call(
        body, name=name, out_shape=(hid, hid),
        grid=(S // tm,),
        in_specs=[pl.BlockSpec((tm, D), lambda i: (i, 0)), pl.BlockSpec(memory_space=pl.ANY), wide, wide, wide],
        out_specs=(wide, wide),
        scratch_shapes=[pltpu.VMEM(wo.shape, BF16), pltpu.SemaphoreType.DMA((1,))],
        compiler_params=_params("arbitrary"),
    )(dh, wo, silu, dsilu, up)


def _dw_rows(srcs, x, name, carry=None, tk=TN_CHUNK):
    n = len(srcs)
    S, F = srcs[0].shape
    D = x.shape[1]
    tr, tk = _tile(F, 1408), _tile(S, tk, 16)
    nf, nk = F // tr, S // tk

    def body(*refs):
        src_refs, (x_ref, o_ref, acc_ref) = refs[:n], refs[n:]
        r, k = pl.program_id(0), pl.program_id(1)
        for s in range(n):
            @pl.when(r // nf == s)
            def _():
                p = _dot(src_refs[s][...].astype(BF16), x_ref[...], TN)

                @pl.when(k == 0)
                def _():
                    acc_ref[...] = p

                @pl.when(k > 0)
                def _():
                    acc_ref[...] += p

        @pl.when(k == nk - 1)
        def _():
            o_ref[...] = acc_ref[...].astype(BF16)

    def src_spec(s):
        return pl.BlockSpec((tk, tr), lambda r, k: (jnp.where(r // nf == s, k, 0), jnp.clip(r - s * nf, 0, nf - 1)))

    return _pcall(
        body, (*srcs, x), name=name, out_shape=jax.ShapeDtypeStruct((n * F, D), BF16),
        grid=(n * nf, nk),
        in_specs=[src_spec(s) for s in range(n)] + [pl.BlockSpec((tk, D), lambda r, k: (k, 0))],
        out_specs=pl.BlockSpec((tr, D), lambda r, k: (r, 0)),
        scratch_shapes=[pltpu.VMEM((tr, D), F32)],
        sem=("arbitrary", "arbitrary"), carry=carry)


def _dx_norm_bwd(terms, h, g, res, name, carry=None, tm=256):
    S, D = h.shape
    tm = _tile(S, tm, 16)
    n = len(terms)

    def body(*refs):
        dy_refs, w_refs = refs[:n], refs[n:2 * n]
        h_ref, g_ref, r_ref, dh_ref, dg_ref = refs[2 * n:]
        d = _dot(dy_refs[0][...].astype(BF16), w_refs[0][...], terms[0][2])
        for t in range(1, n):
            d = d + _dot(dy_refs[t][...].astype(BF16), w_refs[t][...], terms[t][2])
        x = h_ref[...]
        r = lax.rsqrt(jnp.mean(x * x, axis=-1, keepdims=True) + RMS_EPS)
        xhat = x * r
        dxh = d * g_ref[...]
        c = jnp.mean(dxh * xhat, axis=-1, keepdims=True)
        dh_ref[...] = r * (dxh - xhat * c) + r_ref[...]
        part = _rows8(d * xhat)

        @pl.when(pl.program_id(0) == 0)
        def _():
            dg_ref[...] = part

        @pl.when(pl.program_id(0) > 0)
        def _():
            dg_ref[...] += part

    def w_spec(w, nblk, blk):
        return pl.BlockSpec((w.shape[0] // nblk, w.shape[1]), lambda i: (blk, 0))

    row = pl.BlockSpec((tm, D), lambda i: (i, 0))
    in_specs = [pl.BlockSpec((tm, t[0].shape[1]), lambda i: (i, 0)) for t in terms]
    in_specs += [w_spec(t[1], t[3], t[4]) for t in terms]
    in_specs += [row, pl.BlockSpec((1, D), lambda i: (0, 0)), row]
    return _pcall(
        body, (*[t[0] for t in terms], *[t[1] for t in terms], h, g.reshape(1, D), res), name=name,
        out_shape=(jax.ShapeDtypeStruct((S, D), F32), jax.ShapeDtypeStruct((8, D), F32)),
        grid=(S // tm,),
        in_specs=in_specs,
        out_specs=(row, pl.BlockSpec((8, D), lambda i: (0, 0))),
        sem=("arbitrary",), carry=carry)


def _load_resident(pairs, sems):
    @pl.when(pl.program_id(0) == 0)
    def _():
        copies = [pltpu.make_async_copy(src, dst, sems.at[n]) for n, (src, dst) in enumerate(pairs)]
        for cp in copies:
            cp.start()
        for cp in copies:
            cp.wait()


def _loss_tail(y_in, g, tgt):
    D = y_in.shape[-1]
    r = lax.rsqrt(jnp.mean(y_in * y_in, axis=-1, keepdims=True) + RMS_EPS)
    xhat = y_in * r
    err = xhat * g - tgt
    d = err * (1.0 / D)
    dxh = d * g
    c = jnp.mean(dxh * xhat, axis=-1, keepdims=True)
    return r * (dxh - xhat * c), _rows8(d * xhat), _rows8(err * err)


def _ffn_fwd_fused(h, g, win_t, wo, name, carry=None, loss=None):
    S, D = h.shape
    F = wo.shape[0]
    tm = _tile(S, 256, 16)
    n_head = 3 if loss is not None else 1

    def body(h_ref, g_ref, win_hbm, wo_hbm, *rest):
        lead, (xn_ref, silu_ref, dsilu_ref, up_ref, act_ref, win_v, wo_v, sems) = rest[:-8], rest[-8:]
        _load_resident([(win_hbm, win_v), (wo_hbm, wo_v)], sems)
        x = h_ref[...]
        r = lax.rsqrt(jnp.mean(x * x, axis=-1, keepdims=True) + RMS_EPS)
        xn = ((x * r) * g_ref[...]).astype(BF16)
        xn_ref[...] = xn
        gate = _dot(xn, win_v[:F, :], NT)
        up = _dot(xn, win_v[F:, :], NT)
        sig = 1.0 / (1.0 + jnp.exp(-gate))
        silu = gate * sig
        act = (silu * up).astype(BF16)
        up_ref[...] = up.astype(BF16)
        silu_ref[...] = silu.astype(BF16)
        dsilu_ref[...] = (sig + silu * (1.0 - sig)).astype(BF16)
        act_ref[...] = act
        out = x + FFN_RES_SCALE * _dot(act, wo_v[...], NN)
        if loss is None:
            lead[0][...] = out
        else:
            gf_ref, t_ref, dy_ref, dgf_ref, sq_ref = lead
            dy, dgf, sq = _loss_tail(out, gf_ref[...], t_ref[...])
            dy_ref[...] = dy

            @pl.when(pl.program_id(0) == 0)
            def _():
                dgf_ref[...] = dgf
                sq_ref[...] = sq

            @pl.when(pl.program_id(0) > 0)
            def _():
                dgf_ref[...] += dgf
                sq_ref[...] += sq

    row = pl.BlockSpec((tm, D), lambda i: (i, 0))
    vec = pl.BlockSpec((1, D), lambda i: (0, 0))
    acc = pl.BlockSpec((8, D), lambda i: (0, 0))
    wide = pl.BlockSpec((tm, F), lambda i: (i, 0))
    hbm = pl.BlockSpec(memory_space=pl.ANY)
    hid = jax.ShapeDtypeStruct((S, F), BF16)
    full = jax.ShapeDtypeStruct((S, D), F32)
    part = jax.ShapeDtypeStruct((8, D), F32)
    args, in_specs = (h, g.reshape(1, D), win_t, wo), [row, vec, hbm, hbm]
    lead_shapes, lead_specs = (full,), (row,)
    if loss is not None:
        args, in_specs = args + (loss[0].reshape(1, D), loss[1]), in_specs + [vec, row]
        lead_shapes, lead_specs = (full, part, part), (row, acc, acc)
    res, got = _pcall(
        body, args, name=name,
        out_shape=lead_shapes + (jax.ShapeDtypeStruct((S, D), BF16), hid, hid, hid, hid),
        grid=(S // tm,),
        in_specs=in_specs,
        out_specs=lead_specs + (row, wide, wide, wide, wide),
        scratch_shapes=[pltpu.VMEM(win_t.shape, BF16), pltpu.VMEM(wo.shape, BF16), pltpu.SemaphoreType.DMA((2,))],
        sem=("arbitrary",), carry=carry)
    first = res[0] if loss is None else tuple(res[:3])
    return first, tuple(res[n_head:]), got


def _ffn_bwd_fused(dh, h, g, win_t, wo, silu, dsilu, up, name, carry=None):
    S, D = h.shape
    F = wo.shape[0]
    tm = _tile(S, 256, 16)

    def body(dh_ref, h_ref, g_ref, s_ref, ds_ref, u_ref, win_hbm, wo_hbm,
             dhin_ref, dgain_ref, dgate_ref, dup_ref, win_v, wo_v, sems):
        _load_resident([(win_hbm, win_v), (wo_hbm, wo_v)], sems)
        dhv = dh_ref[...]
        d = _dot(dhv.astype(BF16), wo_v[...], NT) * FFN_RES_SCALE
        dup = (d * s_ref[...].astype(F32)).astype(BF16)
        dgate = (d * u_ref[...].astype(F32) * ds_ref[...].astype(F32)).astype(BF16)
        dup_ref[...] = dup
        dgate_ref[...] = dgate
        dxn = _dot(dgate, win_v[:F, :], NN) + _dot(dup, win_v[F:, :], NN)
        x = h_ref[...]
        r = lax.rsqrt(jnp.mean(x * x, axis=-1, keepdims=True) + RMS_EPS)
        xhat = x * r
        dxh = dxn * g_ref[...]
        c = jnp.mean(dxh * xhat, axis=-1, keepdims=True)
        dhin_ref[...] = r * (dxh - xhat * c) + dhv
        part = _rows8(dxn * xhat)

        @pl.when(pl.program_id(0) == 0)
        def _():
            dgain_ref[...] = part

        @pl.when(pl.program_id(0) > 0)
        def _():
            dgain_ref[...] += part

    row = pl.BlockSpec((tm, D), lambda i: (i, 0))
    wide = pl.BlockSpec((tm, F), lambda i: (i, 0))
    hbm = pl.BlockSpec(memory_space=pl.ANY)
    hid = jax.ShapeDtypeStruct((S, F), BF16)
    return _pcall(
        body, (dh, h, g.reshape(1, D), silu, dsilu, up, win_t, wo), name=name,
        out_shape=(jax.ShapeDtypeStruct((S, D), F32), jax.ShapeDtypeStruct((8, D), F32), hid, hid),
        grid=(S // tm,),
        in_specs=[row, row, pl.BlockSpec((1, D), lambda i: (0, 0)), wide, wide, wide, hbm, hbm],
        out_specs=(row, pl.BlockSpec((8, D), lambda i: (0, 0)), wide, wide),
        scratch_shapes=[pltpu.VMEM(win_t.shape, BF16), pltpu.VMEM(wo.shape, BF16), pltpu.SemaphoreType.DMA((2,))],
        sem=("arbitrary",), carry=carry)


def _rope_tables(S):
    half = HEAD_DIM // 2
    inv_freq = ROPE_THETA ** (-jnp.arange(half, dtype=F32) / half)
    ang = jnp.arange(S).astype(F32)[:, None] * inv_freq[None, :]
    cos, sin = jnp.cos(ang), jnp.sin(ang)
    cos_t = jnp.tile(cos, (1, LANES // half))
    sin_t = jnp.tile(jnp.concatenate([-sin, sin], axis=1), (1, LANES // HEAD_DIM))
    return cos_t, sin_t


def _swap_halves(x):
    lane = lax.broadcasted_iota(jnp.int32, x.shape, 1)
    first = (lane % HEAD_DIM) < (HEAD_DIM // 2)
    return jnp.where(first, pltpu.roll(x, LANES - HEAD_DIM // 2, 1), pltpu.roll(x, HEAD_DIM // 2, 1))


def _rotary_bwd(dys, cos_t, sin_t, n_rot, name):
    S = dys[0].shape[0]
    widths = [dy.shape[1] for dy in dys]
    ts = _tile(S, 512, 16)

    def body(*refs):
        x_refs, (c_ref, s_ref, o_ref) = refs[:len(dys)], refs[len(dys):]
        cs, sn = c_ref[...], s_ref[...]
        gidx = 0
        for x_ref, width in zip(x_refs, widths):
            for g in range(width // LANES):
                v = x_ref[:, g * LANES:(g + 1) * LANES].astype(F32)
                if gidx < n_rot:
                    v = v * cs + _swap_halves(v * sn)
                o_ref[:, gidx * LANES:(gidx + 1) * LANES] = v.astype(BF16)
                gidx += 1

    tab = pl.BlockSpec((ts, LANES), lambda i: (i, 0))
    return pl.pallas_call(
        body, name=name, out_shape=jax.ShapeDtypeStruct((S, sum(widths)), BF16),
        grid=(S // ts,),
        in_specs=[pl.BlockSpec((ts, width), lambda i: (i, 0)) for width in widths] + [tab, tab],
        out_specs=pl.BlockSpec((ts, sum(widths)), lambda i: (i, 0)),
        compiler_params=_params("parallel"),
    )(*dys, cos_t, sin_t)


def _head_masks():
    lane = lax.broadcasted_iota(jnp.int32, (BLK, LANES), 1)
    return lane < HEAD_DIM


def _split_bf16(x):
    hi = x.astype(BF16)
    lo = (x - hi.astype(F32)).astype(BF16)
    return hi, lo


def _sb_scores(qh, ks, carry, diag, tri_excl, strict):
    n_heads = len(qh)
    zs = [_dot(ks[n], qh[n], NT) for n in range(n_heads)]
    a_l, b_l, split_l = [], [], []
    for z in zs:
        a = jnp.minimum(z, 0.0) - jnp.log(1.0 + jnp.exp(-jnp.abs(z)))
        b = a - z
        if diag:
            b = jnp.where(strict, b, 0.0)
        a_l.append(a)
        b_l.append(b)
        split_l.append(_split_bf16(b))
    sufs = [_dot(tri_excl, hi, NN) + _dot(tri_excl, lo, NN) for hi, lo in split_l]
    w_l = []
    for n in range(n_heads):
        w = jnp.exp(a_l[n] + sufs[n] + carry[n])
        if diag:
            w = jnp.where(strict, w, 0.0)
        w_l.append(w)
    return a_l, b_l, w_l


SB_FWD_PAIRS = 4
SB_FWD_QBLOCKS = 4
SB_BWD_PAIRS = 2
SB_BWD_QBLOCKS = 4


def _any_alive(carries):
    top = carries[0]
    for c in carries[1:]:
        top = jnp.maximum(top, c)
    return (jnp.max(top) > SB_LOG_FLOOR).astype(jnp.int32)


def _sb_masks():
    row = lax.broadcasted_iota(jnp.int32, (BLK, BLK), 0)
    col = lax.broadcasted_iota(jnp.int32, (BLK, BLK), 1)
    tri_excl = jnp.where(col > row, 1.0, 0.0).astype(BF16)
    tri_incl = jnp.where(col >= row, 1.0, 0.0).astype(BF16)
    return row < HEAD_DIM, row < col, tri_excl, tri_incl


def _sb_fwd(qkv, kv_t, name, carry=None):
    S, D3 = qkv.shape
    D = D3 // 3
    npair, nb = D // LANES, S // BLK
    P = min(SB_FWD_PAIRS, npair)
    ngroup = npair // P
    W = P * LANES

    QB = SB_FWD_QBLOCKS if nb % SB_FWD_QBLOCKS == 0 else 1
    nch = QB * 2 * P

    def body(q_ref, k_ref, vt_ref, o_ref):
        i_first = pl.program_id(1) * QB
        m0 = _head_masks()
        top, strict, tri_excl, _ = _sb_masks()
        zq = jnp.zeros((BLK, LANES), BF16)
        lanes = [slice(p * LANES, (p + 1) * LANES) for p in range(P)]
        qh = []
        for qb in range(QB):
            for sl in lanes:
                q2 = q_ref[qb * BLK:(qb + 1) * BLK, sl] * ATTN_SCALE
                qh += [jnp.where(m0, q2, zq), jnp.where(m0, zq, q2)]

        def block(qbs, js, carry, acc, diag):
            offs = [pl.multiple_of(j * BLK, BLK) for j in js]
            ks, vth, qs = [], [], []
            for n_qb, qb in enumerate(qbs):
                qs += qh[qb * 2 * P:(qb + 1) * 2 * P]
                for sl in lanes:
                    k2 = k_ref[pl.ds(offs[n_qb], BLK), sl]
                    vt = vt_ref[sl, pl.ds(offs[n_qb], BLK)]
                    ks += [k2, k2]
                    vth += [jnp.where(top, vt, zq), jnp.where(top, zq, vt)]
            _, b_l, w_l = _sb_scores(qs, ks, carry, diag, tri_excl, strict)
            wb = [w.astype(BF16) for w in w_l]
            new_acc = [acc[m] + _dot(vth[2 * m], wb[2 * m], NN) + _dot(vth[2 * m + 1], wb[2 * m + 1], NN)
                       for m in range(len(qbs) * P)]
            new_carry = [carry[n] + jnp.sum(b_l[n], axis=0, keepdims=True) for n in range(len(carry))]
            return new_carry, new_acc

        every = list(range(QB))
        c0 = jnp.zeros((1, BLK), F32)
        carry, acc = block(every, [i_first + qb for qb in every], [c0] * nch,
                           [jnp.zeros((LANES, BLK), F32)] * (QB * P), True)
        carry = [jnp.where(i_first > 0, c, NEG_BIG) for c in carry[:2 * P]] + carry[2 * P:]
        carry, acc = block(every, [jnp.maximum(i_first + qb - 1, 0) for qb in every], carry, acc, False)

        for qb in range(QB):
            i_qb = i_first + qb
            sub = slice(qb * 2 * P, (qb + 1) * 2 * P)

            def cond(st):
                return jnp.logical_and(i_qb - st[0] >= 0, st[1] > 0)

            def step(st, qb=qb, i_qb=i_qb):
                t, _, c_qb, a_qb = st
                c_qb, a_qb = block([qb], [i_qb - t], c_qb, a_qb, False)
                return t + 1, _any_alive(c_qb), c_qb, a_qb

            st = lax.while_loop(cond, step, (2, _any_alive(carry[sub]), carry[sub], acc[qb * P:(qb + 1) * P]))
            for p, sl in enumerate(lanes):
                o_ref[qb * BLK:(qb + 1) * BLK, sl] = jnp.transpose(st[3][p])

    return _pcall(
        body, (qkv, qkv, kv_t), name=name, out_shape=jax.ShapeDtypeStruct((S, D), F32),
        grid=(ngroup, nb // QB),
        in_specs=[pl.BlockSpec((QB * BLK, W), lambda g, i: (i, g)),
                  pl.BlockSpec((S, W), lambda g, i: (0, ngroup + g)),
                  pl.BlockSpec((W, S), lambda g, i: (ngroup + g, 0))],
        out_specs=pl.BlockSpec((QB * BLK, W), lambda g, i: (i, g)),
        sem=("arbitrary", "arbitrary"), carry=carry)


def _sb_bwd(qkv, kv_t, o, do, name, carry=None):
    S, D3 = qkv.shape
    D = D3 // 3
    npair, nb = D // LANES, S // BLK
    P = min(SB_BWD_PAIRS, npair)
    ngroup = npair // P
    W = P * LANES

    QB = SB_BWD_QBLOCKS if nb % SB_BWD_QBLOCKS == 0 else 1
    nch = QB * 2 * P

    def body(q_ref, o_ref, do_ref, qkv_hbm, kt_hbm, dq_ref, dk_ref, dv_ref, k_ref, v_ref, kt_ref, sems):
        grp = pl.program_id(0)
        i_first = pl.program_id(1) * QB
        m0 = _head_masks()
        top, strict, tri_excl, tri_incl = _sb_masks()
        zq = jnp.zeros((BLK, LANES), BF16)
        lanes = [slice(p * LANES, (p + 1) * LANES) for p in range(P)]

        @pl.when(pl.program_id(1) == 0)
        def _():
            copies = [pltpu.make_async_copy(qkv_hbm.at[:, pl.ds(pl.multiple_of((c * ngroup + grp) * W, LANES), W)],
                                            ref, sems.at[c - 1]) for c, ref in ((1, k_ref), (2, v_ref))]
            copies.append(pltpu.make_async_copy(kt_hbm.at[pl.ds(pl.multiple_of(grp * W, LANES), W), :],
                                                kt_ref, sems.at[2]))
            for cp in copies:
                cp.start()
            dk_ref[...] = jnp.zeros_like(dk_ref)
            dv_ref[...] = jnp.zeros_like(dv_ref)
            for cp in copies:
                cp.wait()

        qh, doh, delta = [], [], []
        for qb in range(QB):
            rs = slice(qb * BLK, (qb + 1) * BLK)
            for sl in lanes:
                q2, do2 = q_ref[rs, sl] * ATTN_SCALE, do_ref[rs, sl]
                qh += [jnp.where(m0, q2, zq), jnp.where(m0, zq, q2)]
                doh += [jnp.where(m0, do2, zq), jnp.where(m0, zq, do2)]
                prod_t = jnp.transpose(do2.astype(F32) * o_ref[rs, sl])
                delta += [jnp.sum(jnp.where(top, prod_t, 0.0), axis=0, keepdims=True),
                          jnp.sum(jnp.where(top, 0.0, prod_t), axis=0, keepdims=True)]

        def block(qbs, js, valid, cb, cg, dq, diag):
            offs = [pl.multiple_of(j * BLK, BLK) for j in js]
            n_ch = len(qbs) * 2 * P
            ks, vs, kth, qs, dos, dls = [], [], [], [], [], []
            for n_qb, qb in enumerate(qbs):
                chains = slice(qb * 2 * P, (qb + 1) * 2 * P)
                qs, dos, dls = qs + qh[chains], dos + doh[chains], dls + delta[chains]
                for sl in lanes:
                    k2, v2 = k_ref[pl.ds(offs[n_qb], BLK), sl], v_ref[pl.ds(offs[n_qb], BLK), sl]
                    ks += [k2, k2]
                    vs += [v2, v2]
                    kt = kt_ref[sl, pl.ds(offs[n_qb], BLK)] * ATTN_SCALE
                    kth += [jnp.where(top, kt, zq), jnp.where(top, zq, kt)]
            dws = [_dot(vs[n], dos[n], NT) for n in range(n_ch)]
            a_l, b_l, w_l = _sb_scores(qs, ks, cb, diag, tri_excl, strict)
            wb = [w.astype(BF16) for w in w_l]
            g_l = [dws[n] * wb[n].astype(F32) for n in range(n_ch)]
            gsplit = [_split_bf16(g) for g in g_l]
            gincs = [_dot(tri_incl, hi, NN) + _dot(tri_incl, lo, NN) for hi, lo in gsplit]
            dzs = []
            for n in range(n_ch):
                beta = jnp.exp(a_l[n])
                dz = g_l[n] - beta * (g_l[n] + ((dls[n] - cg[n]) - gincs[n]))
                if diag:
                    dz = jnp.where(strict, dz, 0.0)
                if valid[n // (2 * P)] is not None:
                    dz = jnp.where(valid[n // (2 * P)], dz, 0.0)
                dzs.append(dz.astype(BF16))
            ndq = []
            for n_qb in range(len(qbs)):
                for p, sl in enumerate(lanes):
                    n0 = n_qb * 2 * P + 2 * p
                    ndq.append(dq[n_qb * P + p] + _dot(kth[n0], dzs[n0], NN) + _dot(kth[n0 + 1], dzs[n0 + 1], NN))
                    dk_ref[pl.ds(offs[n_qb], BLK), sl] += _dot(dzs[n0], qs[n0], NN) + _dot(dzs[n0 + 1], qs[n0 + 1], NN)
                    dv_ref[pl.ds(offs[n_qb], BLK), sl] += _dot(wb[n0], dos[n0], NN) + _dot(wb[n0 + 1], dos[n0 + 1], NN)
            ncb = [cb[n] + jnp.sum(b_l[n], axis=0, keepdims=True) for n in range(n_ch)]
            ncg = [cg[n] + jnp.sum(g_l[n], axis=0, keepdims=True) for n in range(n_ch)]
            return ncb, ncg, ndq

        every = list(range(QB))
        c0 = jnp.zeros((1, BLK), F32)
        cb, cg, dq = block(every, [i_first + qb for qb in every], [None] * QB, [c0] * nch, [c0] * nch,
                           [jnp.zeros((LANES, BLK), F32)] * (QB * P), True)
        has_prev = i_first > 0
        cb = [jnp.where(has_prev, c, NEG_BIG) for c in cb[:2 * P]] + cb[2 * P:]
        cb, cg, dq = block(every, [jnp.maximum(i_first + qb - 1, 0) for qb in every], [has_prev] + [None] * (QB - 1),
                           cb, cg, dq, False)

        for qb in range(QB):
            i_qb = i_first + qb
            sub = slice(qb * 2 * P, (qb + 1) * 2 * P)

            def cond(st):
                return jnp.logical_and(i_qb - st[0] >= 0, st[1] > 0)

            def step(st, qb=qb, i_qb=i_qb):
                t, _, b_qb, g_qb, dq_qb = st
                b_qb, g_qb, dq_qb = block([qb], [i_qb - t], [None], b_qb, g_qb, dq_qb, False)
                return t + 1, _any_alive(b_qb), b_qb, g_qb, dq_qb

            st = lax.while_loop(cond, step, (2, _any_alive(cb[sub]), cb[sub], cg[sub], dq[qb * P:(qb + 1) * P]))
            for p, sl in enumerate(lanes):
                dq_ref[qb * BLK:(qb + 1) * BLK, sl] = jnp.transpose(st[4][p]).astype(BF16)

    blk = pl.BlockSpec((QB * BLK, W), lambda g, i: (i, g))
    col_all = pl.BlockSpec((S, W), lambda g, i: (0, g))
    hbm = pl.BlockSpec(memory_space=pl.ANY)
    return _pcall(
        body, (qkv, o, do, qkv, kv_t), name=name,
        out_shape=(jax.ShapeDtypeStruct((S, D), BF16), jax.ShapeDtypeStruct((S, D), F32),
                   jax.ShapeDtypeStruct((S, D), F32)),
        grid=(ngroup, nb // QB),
        in_specs=[blk, blk, blk, hbm, hbm],
        out_specs=(blk, col_all, col_all),
        scratch_shapes=[pltpu.VMEM((S, W), BF16), pltpu.VMEM((S, W), BF16), pltpu.VMEM((W, S), BF16),
                        pltpu.SemaphoreType.DMA((3,))],
        sem=("arbitrary", "arbitrary"), carry=carry)


SWA_Q_GROUPS = 4


def _roll_heads(x):
    return pltpu.roll(x.astype(F32), HEAD_DIM, 1).astype(BF16)


def _roll_rows(x):
    return pltpu.roll(x.astype(F32), HEAD_DIM, 0).astype(BF16)


def _swa_valid(i):
    k = lax.broadcasted_iota(jnp.int32, (2 * BLK, BLK), 0)
    q = lax.broadcasted_iota(jnp.int32, (2 * BLK, BLK), 1)
    diff = q + BLK - k
    return (diff >= 0) & (diff < BLK) & ((i > 0) | (k >= BLK))


def _swa_probs(z, valid, sink):
    z = jnp.where(valid, z * ATTN_SCALE, NEG_BIG)
    mx = jnp.maximum(jnp.max(z, axis=0, keepdims=True), sink)
    p = jnp.exp(z - mx)
    ps = jnp.exp(sink - mx)
    inv = 1.0 / (jnp.sum(p, axis=0, keepdims=True) + ps)
    return p * inv, ps * inv


def _swa_operands(q_ref, kc_ref, kp_ref, vc_ref, vp_ref, tc_ref, tp_ref, s_ref, nkvp):
    m0 = _head_masks()
    top = lax.broadcasted_iota(jnp.int32, (LANES, 2 * BLK), 0) < HEAD_DIM
    heads = []
    for m in range(nkvp):
        pair = slice(m * LANES, (m + 1) * LANES)
        kk = jnp.concatenate([kp_ref[:, pair], kc_ref[:, pair]], axis=0)
        vv = jnp.concatenate([vp_ref[:, pair], vc_ref[:, pair]], axis=0)
        tt = jnp.concatenate([tp_ref[pair, :], tc_ref[pair, :]], axis=1)
        ksw, vsw, tsw = _roll_heads(kk), _roll_heads(vv), _roll_rows(tt)
        zt = jnp.zeros_like(tt)
        for c in range(SWA_Q_GROUPS):
            q_lanes = slice((m * SWA_Q_GROUPS + c) * LANES, (m * SWA_Q_GROUPS + c + 1) * LANES)
            qc = q_ref[:, q_lanes]
            zq = jnp.zeros_like(qc)
            for u in range(2):
                same = u == c // 2
                sel = (lambda x, z, mk: jnp.where(mk, x, z)) if u == 0 else (lambda x, z, mk: jnp.where(mk, z, x))
                heads.append(dict(
                    m=m, q_lanes=q_lanes, same=same, sel=sel, qm=sel(qc, zq, m0),
                    k=kk if same else ksw, v=vv if same else vsw,
                    tm=sel(tt if same else tsw, zt, top),
                    sink=s_ref[0, (m * SWA_Q_GROUPS + c) * 2 + u]))
    return heads, m0


def _swa_specs(D, half, t_block):
    prev = lambda i: jnp.maximum(i - 1, 0)
    return [pl.BlockSpec((BLK, D), lambda i: (i, 0)),
            pl.BlockSpec((BLK, half), lambda i: (i, 0)),
            pl.BlockSpec((BLK, half), lambda i: (prev(i), 0)),
            pl.BlockSpec((BLK, half), lambda i: (i, 1)),
            pl.BlockSpec((BLK, half), lambda i: (prev(i), 1)),
            pl.BlockSpec((half, BLK), lambda i: (t_block, i)),
            pl.BlockSpec((half, BLK), lambda i: (t_block, prev(i))),
            pl.BlockSpec(memory_space=pltpu.SMEM)]


def _swa_fwd(q, kv, kv_t, sinks, name):
    S, D = q.shape
    half = kv.shape[1] // 2
    nkvp = half // LANES

    def body(q_ref, kc_ref, kp_ref, vc_ref, vp_ref, tc_ref, tp_ref, s_ref, o_ref):
        valid = _swa_valid(pl.program_id(0))
        heads, _ = _swa_operands(q_ref, kc_ref, kp_ref, vc_ref, vp_ref, tc_ref, tp_ref, s_ref, nkvp)
        zs = [_dot(hd["k"], hd["qm"], NT) for hd in heads]
        ps = [_swa_probs(z, valid, hd["sink"])[0].astype(BF16) for z, hd in zip(zs, heads)]
        for n in range(0, len(heads), 2):
            o_t = _dot(heads[n]["tm"], ps[n], NN) + _dot(heads[n + 1]["tm"], ps[n + 1], NN)
            o_ref[:, heads[n]["q_lanes"]] = jnp.transpose(o_t)

    return pl.pallas_call(
        body, name=name, out_shape=jax.ShapeDtypeStruct((S, D), F32),
        grid=(S // BLK,),
        in_specs=_swa_specs(D, half, 1),
        out_specs=pl.BlockSpec((BLK, D), lambda i: (i, 0)),
        compiler_params=_params("arbitrary"),
    )(q, kv, kv, kv, kv, kv_t, kv_t, sinks)


def _swa_bwd(q, kv, kv_t, sinks, o, do, cos_t, sin_t, name, carry=None):
    S, D = q.shape
    half = kv.shape[1] // 2
    nkvp = half // LANES
    nh = nkvp * 2 * SWA_Q_GROUPS

    def body(q_ref, kc_ref, kp_ref, vc_ref, vp_ref, tc_ref, tp_ref, s_ref, o_ref, do_ref, c_ref, sn_ref,
             dq_ref, dk_ref, dv_ref, ds_ref):
        i = pl.program_id(0)
        valid = _swa_valid(i)
        heads, m0 = _swa_operands(q_ref, kc_ref, kp_ref, vc_ref, vp_ref, tc_ref, tp_ref, s_ref, nkvp)
        top_q = lax.broadcasted_iota(jnp.int32, (LANES, BLK), 0) < HEAD_DIM

        @pl.when(i == 0)
        def _():
            dk_ref[...] = jnp.zeros_like(dk_ref)
            dv_ref[...] = jnp.zeros_like(dv_ref)
            ds_ref[...] = jnp.zeros_like(ds_ref)

        doms, deltas = [], []
        for n in range(0, nh, 2):
            doc = do_ref[:, heads[n]["q_lanes"]]
            prod_t = jnp.transpose(doc.astype(F32) * o_ref[:, heads[n]["q_lanes"]])
            for hd in heads[n:n + 2]:
                doms.append(hd["sel"](doc, jnp.zeros_like(doc), m0))
                deltas.append(jnp.sum(hd["sel"](prod_t, 0.0, top_q), axis=0, keepdims=True))
        zs = [_dot(hd["k"], hd["qm"], NT) for hd in heads]
        dps = [_dot(hd["v"], dom, NT) for dom, hd in zip(doms, heads)]
        pbs, dscs = [], []
        for n, hd in enumerate(heads):
            p, psink = _swa_probs(zs[n], valid, hd["sink"])
            pbs.append(p.astype(BF16))
            dscs.append((p * (dps[n] - deltas[n]) * ATTN_SCALE).astype(BF16))
            ds_ref[n:n + 1, :] += -(psink * deltas[n])
        for n in range(0, nh, 2):
            dq_rot = jnp.transpose(_dot(heads[n]["tm"], dscs[n], NN) + _dot(heads[n + 1]["tm"], dscs[n + 1], NN))
            dq_ref[:, heads[n]["q_lanes"]] = (
                dq_rot * c_ref[...] + _swap_halves(dq_rot * sn_ref[...])).astype(BF16)
        acc = {}
        for n, hd in enumerate(heads):
            dk_n = _dot(dscs[n], hd["qm"], NN)
            dv_n = _dot(pbs[n], doms[n], NN)
            for key, val in ((("k", hd["m"], hd["same"]), dk_n), (("v", hd["m"], hd["same"]), dv_n)):
                acc[key] = val if key not in acc else acc[key] + val
        poff = pl.multiple_of(jnp.maximum(i - 1, 0) * BLK, BLK)
        coff = pl.multiple_of(i * BLK, BLK)
        for m in range(nkvp):
            pair = slice(m * LANES, (m + 1) * LANES)
            dkk = acc["k", m, True] + pltpu.roll(acc["k", m, False], HEAD_DIM, 1)
            dvv = acc["v", m, True] + pltpu.roll(acc["v", m, False], HEAD_DIM, 1)
            dk_ref[pl.ds(poff, BLK), pair] += dkk[:BLK]
            dv_ref[pl.ds(poff, BLK), pair] += dvv[:BLK]
            dk_ref[pl.ds(coff, BLK), pair] += dkk[BLK:]
            dv_ref[pl.ds(coff, BLK), pair] += dvv[BLK:]

    qblk = pl.BlockSpec((BLK, D), lambda i: (i, 0))
    whole = pl.BlockSpec((S, half), lambda i: (0, 0))
    tab = pl.BlockSpec((BLK, LANES), lambda i: (i, 0))
    return _pcall(
        body, (q, kv, kv, kv, kv, kv_t, kv_t, sinks, o, do, cos_t, sin_t), name=name,
        out_shape=(jax.ShapeDtypeStruct((S, D), BF16),
                   jax.ShapeDtypeStruct((S, half), F32),
                   jax.ShapeDtypeStruct((S, half), F32),
                   jax.ShapeDtypeStruct((nh, LANES), F32)),
        grid=(S // BLK,),
        in_specs=_swa_specs(D, half, 0) + [qblk, qblk, tab, tab],
        out_specs=(qblk, whole, whole, pl.BlockSpec((nh, LANES), lambda i: (0, 0))),
        sem=("arbitrary",), carry=carry)


def _dev_index(p):
    return 4 * p[0] + 2 * p[1] + p[2]


def _gather_plan(x_refs, out_refs, send_sems, recv_sems, local_sems):
    n = len(x_refs)
    x_, y_, c_ = lax.axis_index("x"), lax.axis_index("y"), lax.axis_index("c")
    me, sibling = (x_, y_, c_), (x_, y_, 1 - c_)
    chips = [(1 - x_, y_), (x_, 1 - y_), (1 - x_, 1 - y_)]

    def copy(t, k, block, to, src=None):
        dst = out_refs[t].at[_dev_index(block)]
        return pltpu.make_async_remote_copy(
            src_ref=dst if src is None else src, dst_ref=dst,
            send_sem=send_sems.at[7 * t + k], recv_sem=recv_sems.at[7 * t + k],
            device_id=to, device_id_type=MESH)

    mine = [pltpu.make_async_copy(x_refs[t], out_refs[t].at[_dev_index(me)], local_sems.at[t]) for t in range(n)]
    first = []
    for t in range(n):
        first.append(copy(t, 0, me, sibling, src=x_refs[t]))
        first += [copy(t, 1 + j, me, (*chip, c_), src=x_refs[t]) for j, chip in enumerate(chips)]
    arrived = lambda t, j: copy(t, 1 + j, (*chips[j], c_), me)
    forward = lambda t, j: copy(t, 4 + j, (*chips[j], c_), sibling)
    from_sibling = lambda t: copy(t, 0, sibling, me)
    forwarded = lambda t, j: copy(t, 4 + j, (*chips[j], 1 - c_), me)
    return n, mine, first, arrived, forward, from_sibling, forwarded


def _gather_start(x_refs, out_refs, send_sems, recv_sems, local_sems):
    _, mine, first, *_ = _gather_plan(x_refs, out_refs, send_sems, recv_sems, local_sems)
    for cp in mine + first:
        cp.start()


def _gather_forward(x_refs, out_refs, send_sems, recv_sems, local_sems):
    n, _, _, arrived, forward, _, _ = _gather_plan(x_refs, out_refs, send_sems, recv_sems, local_sems)
    for j in range(3):
        for t in range(n):
            arrived(t, j).wait_recv()
            forward(t, j).start()


def _gather_finish(x_refs, out_refs, send_sems, recv_sems, local_sems):
    n, mine, first, _, forward, from_sibling, forwarded = _gather_plan(
        x_refs, out_refs, send_sems, recv_sems, local_sems)
    for t in range(n):
        from_sibling(t).wait_recv()
    for j in range(3):
        for t in range(n):
            forwarded(t, j).wait_recv()
    for cp in first + [forward(t, j) for j in range(3) for t in range(n)]:
        cp.wait_send()
    for cp in mine:
        cp.wait()


def _scatter_plan(b_refs, out_refs, send_sems, recv_sems, local_sems):
    n = len(b_refs)
    x_, y_, c_ = lax.axis_index("x"), lax.axis_index("y"), lax.axis_index("c")
    my_idx = _dev_index((x_, y_, c_))
    mine = [pltpu.make_async_copy(b_refs[t].at[my_idx], out_refs[t].at[my_idx], local_sems.at[t]) for t in range(n)]
    copies = []
    for t in range(n):
        for k in range(1, N_DEV):
            peer = (x_ ^ ((k >> 2) & 1), y_ ^ ((k >> 1) & 1), c_ ^ (k & 1))
            copies.append(pltpu.make_async_remote_copy(
                src_ref=b_refs[t].at[_dev_index(peer)], dst_ref=out_refs[t].at[my_idx],
                send_sem=send_sems.at[7 * t + k - 1], recv_sem=recv_sems.at[7 * t + k - 1],
                device_id=peer, device_id_type=MESH))
    return mine, copies


def _scatter_start(b_refs, out_refs, send_sems, recv_sems, local_sems):
    mine, copies = _scatter_plan(b_refs, out_refs, send_sems, recv_sems, local_sems)
    for cp in mine + copies:
        cp.start()


def _scatter_finish(b_refs, out_refs, send_sems, recv_sems, local_sems):
    mine, copies = _scatter_plan(b_refs, out_refs, send_sems, recv_sems, local_sems)
    for cp in copies:
        cp.wait_recv()
    for cp in copies:
        cp.wait_send()
    for cp in mine:
        cp.wait()


def _exchange_operands(kind, tensors):
    if kind == "gather":
        args = list(tensors)
        shapes = [jax.ShapeDtypeStruct((N_DEV,) + t.shape, t.dtype) for t in tensors]
        return args, shapes, (_gather_start, _gather_forward, _gather_finish)
    args = [t.reshape(N_DEV, t.shape[0] // N_DEV, t.shape[1]) for t in tensors]
    shapes = [jax.ShapeDtypeStruct(a.shape, a.dtype) for a in args]
    return args, shapes, (_scatter_start, None, _scatter_finish)


def _exchange_results(kind, tensors, res):
    if kind == "gather":
        return [r.reshape(N_DEV * t.shape[0], t.shape[1]) for r, t in zip(res, tensors)]
    return list(res)


def _exchange_sems(n):
    return [pltpu.SemaphoreType.DMA((7 * n,)), pltpu.SemaphoreType.DMA((7 * n,)), pltpu.SemaphoreType.DMA((n,))]


def _exchange(kind, tensors, name):
    n = len(tensors)
    args, shapes, phases = _exchange_operands(kind, tensors)

    def body(*refs):
        for phase in phases:
            if phase is not None:
                phase(refs[:n], refs[n:2 * n], *refs[2 * n:])

    hbm = pl.BlockSpec(memory_space=pl.ANY)
    res = pl.pallas_call(body, name=name, out_shape=shapes, in_specs=[hbm] * n, out_specs=[hbm] * n,
                         scratch_shapes=_exchange_sems(n))(*args)
    return _exchange_results(kind, tensors, res)


def _pcall(body, args, *, name, out_shape, grid, in_specs, out_specs, sem, scratch_shapes=(), carry=None):
    if carry is None:
        out = pl.pallas_call(body, name=name, out_shape=out_shape, grid=grid, in_specs=list(in_specs),
                             out_specs=out_specs, scratch_shapes=list(scratch_shapes),
                             compiler_params=_params(*sem))(*args)
        return out, None
    kind, tensors = carry
    multi = isinstance(out_shape, (tuple, list))
    shapes = list(out_shape) if multi else [out_shape]
    ospecs = list(out_specs) if multi else [out_specs]
    n_in, n_out, n_scr, n_c = len(in_specs), len(shapes), len(scratch_shapes), len(tensors)
    c_args, c_shapes, (start, forward, finish) = _exchange_operands(kind, tensors)
    n_steps = 1
    for g in grid:
        n_steps *= g
    late = (3 * n_steps) // 4

    def wrapped(*refs):
        ins, rest = refs[:n_in], refs[n_in:]
        c_in, rest = rest[:n_c], rest[n_c:]
        outs, rest = rest[:n_out], rest[n_out:]
        c_out, rest = rest[:n_c], rest[n_c:]
        scr, sems = rest[:n_scr], rest[n_scr:]
        step = pl.program_id(0)
        for a in range(1, len(grid)):
            step = step * grid[a] + pl.program_id(a)

        @pl.when(step == 0)
        def _():
            start(c_in, c_out, *sems)

        body(*ins, *outs, *scr)

        if forward is not None:
            @pl.when(step == late)
            def _():
                forward(c_in, c_out, *sems)

        @pl.when(step == n_steps - 1)
        def _():
            finish(c_in, c_out, *sems)

    hbm = pl.BlockSpec(memory_space=pl.ANY)
    res = pl.pallas_call(
        wrapped, name=name, out_shape=shapes + c_shapes, grid=grid,
        in_specs=list(in_specs) + [hbm] * n_c, out_specs=ospecs + [hbm] * n_c,
        scratch_shapes=list(scratch_shapes) + _exchange_sems(n_c),
        compiler_params=_params(*sem))(*args, *c_args)
    outs = tuple(res[:n_out]) if multi else res[0]
    return outs, _exchange_results(kind, tensors, res[n_out:])


def _sum8(parts, name):
    _, R, C = parts.shape
    tr = _tile(R, 256, 16)

    def body(p_ref, g_ref):
        g = p_ref[0].astype(F32)
        for s in range(1, N_DEV):
            g = g + p_ref[s].astype(F32)
        g_ref[...] = g

    return pl.pallas_call(
        body, name=name, out_shape=jax.ShapeDtypeStruct((R, C), F32),
        grid=(R // tr,),
        in_specs=[pl.BlockSpec((N_DEV, tr, C), lambda i: (0, i, 0))],
        out_specs=pl.BlockSpec((tr, C), lambda i: (i, 0)),
        compiler_params=_params("parallel"),
    )(parts)


def _adamw(g, w, m, v, name):
    R, C = g.shape
    tr = _tile(R, 256, 8)
    c1 = 1.0 - ADAM_B1 ** ADAM_STEP
    c2 = 1.0 - ADAM_B2 ** ADAM_STEP

    def body(g_ref, w_ref, m_ref, v_ref, d_ref, nm_ref, nv_ref):
        gg = g_ref[...]
        nm = ADAM_B1 * m_ref[...] + (1.0 - ADAM_B1) * gg
        nv = ADAM_B2 * v_ref[...] + (1.0 - ADAM_B2) * (gg * gg)
        m_hat = nm / c1
        v_hat = nv / c2
        nm_ref[...] = nm
        nv_ref[...] = nv
        d_ref[...] = -ADAM_LR * (m_hat / (jnp.sqrt(v_hat) + ADAM_EPS) + ADAM_WD * w_ref[...])

    row = pl.BlockSpec((tr, C), lambda i: (i, 0))
    shp = jax.ShapeDtypeStruct((R, C), F32)
    return pl.pallas_call(
        body, name=name, out_shape=(shp, shp, shp),
        grid=(R // tr,), in_specs=[row, row, row, row], out_specs=(row, row, row),
        compiler_params=_params("parallel"),
    )(g, w, m, v)


def _sum_adamw(parts_list, w, m, v, name):
    L, R, C = w.shape
    tr = _tile(R, 256, 16)
    c1 = 1.0 - ADAM_B1 ** ADAM_STEP
    c2 = 1.0 - ADAM_B2 ** ADAM_STEP

    def body(*refs):
        p_refs = refs[:L]
        w_ref, m_ref, v_ref, g_ref, d_ref, nm_ref, nv_ref = refs[L:]
        for layer in range(L):
            @pl.when(pl.program_id(0) == layer)
            def _():
                g = p_refs[layer][0].astype(F32)
                for s in range(1, N_DEV):
                    g = g + p_refs[layer][s].astype(F32)
                nm = ADAM_B1 * m_ref[0] + (1.0 - ADAM_B1) * g
                nv = ADAM_B2 * v_ref[0] + (1.0 - ADAM_B2) * (g * g)
                g_ref[0] = g
                nm_ref[0] = nm
                nv_ref[0] = nv
                d_ref[0] = -ADAM_LR * ((nm / c1) / (jnp.sqrt(nv / c2) + ADAM_EPS) + ADAM_WD * w_ref[0])

    def parts_spec(layer):
        return pl.BlockSpec((N_DEV, tr, C), lambda l, i: (0, jnp.where(l == layer, i, 0), 0))

    blk = pl.BlockSpec((1, tr, C), lambda l, i: (l, i, 0))
    shp = jax.ShapeDtypeStruct((L, R, C), F32)
    return pl.pallas_call(
        body, name=name, out_shape=(shp, shp, shp, shp),
        grid=(L, R // tr),
        in_specs=[parts_spec(layer) for layer in range(L)] + [blk, blk, blk],
        out_specs=(blk, blk, blk, blk),
        compiler_params=_params("arbitrary", "arbitrary"),
    )(*parts_list, w, m, v)


def _ffn_down(act, wo, h, tag):
    return _mm(act, wo, NN, F32, f"{tag}_down", scale=FFN_RES_SCALE, res=h, tm=512, tn=1024, tk=2816)


def _ffn_fwd(h, g, win_t, wo, tag, carry=None, loss=None):
    return _ffn_fwd_fused(h, g, win_t, wo, f"{tag}_fwd", carry=carry, loss=loss)


def _ffn_bwd(dh, h, g, win_t, wo, saved, tag, scatter=False, carry=None):
    xn, silu, dsilu, up, act = saved
    dwo = _mm(act, dh, TN, BF16, f"{tag}_dwo", scale=FFN_RES_SCALE, tm=1408, tn=1024, tk=TN_CHUNK)
    if not scatter:
        (dh_in, dg, dgate, dup), got = _ffn_bwd_fused(dh, h, g, win_t, wo, silu, dsilu, up, f"{tag}_bwd", carry=carry)
        dwin_t, _ = _dw_rows([dgate, dup], xn, f"{tag}_dwin")
        return dh_in, dg, dwin_t, dwo, got
    dgate, dup = _ffn_dact(dh, wo, silu, dsilu, up, f"{tag}_dact")
    dwin_t, got_wo = _dw_rows([dgate, dup], xn, f"{tag}_dwin", carry=("scatter", [dwo]))
    (dh_in, dg), got_win = _dx_norm_bwd([(dgate, win_t, NN, 2, 0), (dup, win_t, NN, 2, 1)], h, g, dh, f"{tag}_dx",
                                        carry=("scatter", [dwin_t]))
    return dh_in, dg, got_win[0], got_wo[0]


def _proj(a, w, dims, out_dtype, name, res=None):
    return _mm(a, w, dims, out_dtype, name, res=res, tm=1024, tn=1024, tk=1024)


def _proj_dw(x, dy, name):
    return _mm(x, dy, TN, BF16, name, tm=1024, tn=1024, tk=TN_CHUNK)


def kernel(x, ffn1_norm, ffn1_w_in, ffn1_w_out, mix_norm, ffn2_norm, ffn2_w_in, ffn2_w_out, sb_w_qkv, sb_w_o, kv_norm, kv_w, swa_w_q, swa_sinks, swa_w_o, final_norm, loss_target, m_ffn1_norm, m_ffn1_w_in, m_ffn1_w_out, m_mix_norm, m_ffn2_norm, m_ffn2_w_in, m_ffn2_w_out, m_sb_w_qkv, m_sb_w_o, m_kv_norm, m_kv_w, m_swa_w_q, m_swa_sinks, m_swa_w_o, m_final_norm, v_ffn1_norm, v_ffn1_w_in, v_ffn1_w_out, v_mix_norm, v_ffn2_norm, v_ffn2_w_in, v_ffn2_w_out, v_sb_w_qkv, v_sb_w_o, v_kv_norm, v_kv_w, v_swa_w_q, v_swa_sinks, v_swa_w_o, v_final_norm):
    S, D = x.shape[1], x.shape[2]
    L = ffn1_w_in.shape[0]
    KV = kv_w.shape[1]
    assert L == 2 and swa_sinks.shape == (1, 2 * SWA_Q_GROUPS * KV // (2 * LANES))

    def bf(w):
        return w.astype(BF16)

    def bft(w):
        return jnp.transpose(w).astype(BF16)

    cos_t, sin_t = _rope_tables(S)
    h0 = x.reshape(S, D)
    tgt = loss_target.reshape(S, D)

    win1a_t, = _exchange("gather", [bft(ffn1_w_in[0])], "gather_first_weight")
    sv_a1, (wo1a, wqkv_t, w_sbo) = _ffn_up(
        h0, ffn1_norm[0], win1a_t, "ffn1a_up",
        carry=("gather", [bf(ffn1_w_out[0]), bft(sb_w_qkv[0]), bf(sb_w_o[0])]))
    h1 = _ffn_down(sv_a1[-1], wo1a, h0, "ffn1a")
    hn_a, qkv, kv_t = _norm_proj(h1, mix_norm[0], wqkv_t, NT, "sb_qkv", tail_t=2 * D)
    o_sb, (win2a_t, wo2a, w_kv) = _sb_fwd(qkv, kv_t, "sb_attn", carry=("gather", [
        bft(ffn2_w_in[0]), bf(ffn2_w_out[0]), bf(kv_w)]))
    h2 = _proj(o_sb, w_sbo, NN, F32, "sb_out", res=h1)
    h3, sv_a2, (win1b_t, wo1b, w_q, w_swo) = _ffn_fwd(h2, ffn2_norm[0], win2a_t, wo2a, "ffn2a", carry=("gather", [
        bft(ffn1_w_in[1]), bf(ffn1_w_out[1]), bf(swa_w_q[0]), bf(swa_w_o[0])]))
    kvn, kv_rot, kv_rot_t = _norm_proj(h3, kv_norm, w_kv, NN, "kv_proj", rope=(cos_t, sin_t, KV // (2 * LANES)),
                                       tail_t=KV)
    h4, sv_b1, (win2b_t, wo2b) = _ffn_fwd(h3, ffn1_norm[1], win1b_t, wo1b, "ffn1b", carry=("gather", [
        bft(ffn2_w_in[1]), bf(ffn2_w_out[1])]))
    hn_b, q_rot = _norm_proj(h4, mix_norm[1], w_q, NN, "swa_q", rope=(cos_t, sin_t, D // LANES))
    o_sw = _swa_fwd(q_rot, kv_rot, kv_rot_t, swa_sinks, "swa_attn")
    h5 = _proj(o_sw, w_swo, NN, F32, "swa_out", res=h4)
    (dh6, dg_final, sq_err), sv_b2, _ = _ffn_fwd(h5, ffn2_norm[1], win2b_t, wo2b, "ffn2b", loss=(final_norm, tgt))
    loss_local = 0.5 * jnp.sum(sq_err) / D

    dh5, dg_f2b, dwin2b_t, dwo2b, _ = _ffn_bwd(dh6, h5, ffn2_norm[1], win2b_t, wo2b, sv_b2, "ffn2b")
    do_sw = _proj(dh5, w_swo, NT, BF16, "swa_out_dx")
    dw_swo = _proj_dw(o_sw, dh5, "swa_out_dw")
    (dq, dk_sw, dv_sw, dsink), (p_win2b, p_swo) = _swa_bwd(
        q_rot, kv_rot, kv_rot_t, swa_sinks, o_sw, do_sw, cos_t, sin_t, "swa_attn_bwd",
        carry=("scatter", [dwin2b_t, dw_swo]))
    dw_q = _proj_dw(hn_b, dq, "swa_q_dw")
    (dh4, dg_mix_b), _ = _dx_norm_bwd([(dq, w_q, NT, 1, 0)], h4, mix_norm[1], dh5, "swa_q_dx", tm=512)
    dh3, dg_f1b, dwin1b_t, dwo1b, _ = _ffn_bwd(dh4, h3, ffn1_norm[1], win1b_t, wo1b, sv_b1, "ffn1b")
    dkv = _rotary_bwd([dk_sw, dv_sw], cos_t, sin_t, KV // (2 * LANES), "kv_rope_bwd")
    dw_kv = _proj_dw(kvn, dkv, "kv_proj_dw")
    (dh3, dg_kv), _ = _dx_norm_bwd([(dkv, w_kv, NT, 1, 0)], h3, kv_norm, dh3, "kv_proj_dx", tm=512)
    dh2, dg_f2a, dwin2a_t, dwo2a, (p_win1b, p_kv) = _ffn_bwd(
        dh3, h2, ffn2_norm[0], win2a_t, wo2a, sv_a2, "ffn2a", carry=("scatter", [dwin1b_t, dw_kv]))
    do_sb = _proj(dh2, w_sbo, NT, BF16, "sb_out_dx")
    dw_sbo = _proj_dw(o_sb, dh2, "sb_out_dw")
    (dq_sb, dk_sb, dv_sb), (p_win2a, p_wo2a, p_sbo, p_wo1b, p_q, p_wo2b) = _sb_bwd(
        qkv, kv_t, o_sb, do_sb, "sb_attn_bwd", carry=("scatter", [dwin2a_t, dwo2a, dw_sbo, dwo1b, dw_q, dwo2b]))
    dqkv = [dq_sb, dk_sb, dv_sb]
    dwqkv_t, _ = _dw_rows(dqkv, hn_a, "sb_qkv_dw", tk=TN_CHUNK // 2)
    (dh1, dg_mix_a), (p_qkv,) = _dx_norm_bwd([(dy, wqkv_t, NN, 3, n) for n, dy in enumerate(dqkv)], h1, mix_norm[0],
                                             dh2, "sb_qkv_dx", carry=("scatter", [dwqkv_t]), tm=512)
    dx, dg_f1a, p_win1a, p_wo1a = _ffn_bwd(dh1, h0, ffn1_norm[0], win1a_t, wo1a, sv_a1, "ffn1a", scatter=True)

    def from_t(parts, tag):
        return jnp.transpose(_sum8(parts, f"sum_{tag}"))

    grads = {
        "ffn1_w_in": jnp.stack([from_t(p_win1a, "win1a"), from_t(p_win1b, "win1b")]),
        "ffn2_w_in": jnp.stack([from_t(p_win2a, "win2a"), from_t(p_win2b, "win2b")]),
        "sb_w_qkv": from_t(p_qkv, "qkv")[None],
    }
    row_parts = {"ffn1_w_out": [p_wo1a, p_wo1b], "ffn2_w_out": [p_wo2a, p_wo2b], "sb_w_o": [p_sbo],
                 "kv_w": [p_kv], "swa_w_q": [p_q], "swa_w_o": [p_swo]}

    small_w = [ffn1_norm, mix_norm, ffn2_norm, kv_norm, final_norm, swa_sinks]
    small_m = [m_ffn1_norm, m_mix_norm, m_ffn2_norm, m_kv_norm, m_final_norm, m_swa_sinks]
    small_v = [v_ffn1_norm, v_mix_norm, v_ffn2_norm, v_kv_norm, v_final_norm, v_swa_sinks]
    SMALL_ROWS = 16

    def pack_small(ts):
        rows_ = [t.reshape(-1, D) for t in ts[:-1]]
        sink_row = jnp.pad(ts[-1].reshape(1, -1), ((0, 0), (0, D - ts[-1].size)))
        flat = jnp.concatenate(rows_ + [sink_row], axis=0)
        return jnp.pad(flat, ((0, SMALL_ROWS - flat.shape[0]), (0, 0)))

    def unpack_small(flat):
        out, r = [], 0
        for t in small_w[:-1]:
            n = t.size // D
            out.append(flat[r:r + n].reshape(t.shape))
            r += n
        out.append(flat[r, :swa_sinks.size].reshape(swa_sinks.shape))
        return out

    def gain(parts8):
        return jnp.sum(parts8, axis=0, keepdims=True)

    g_small_local = pack_small([
        jnp.concatenate([gain(dg_f1a), gain(dg_f1b)], axis=0),
        jnp.concatenate([gain(dg_mix_a), gain(dg_mix_b)], axis=0),
        jnp.concatenate([gain(dg_f2a), gain(dg_f2b)], axis=0),
        gain(dg_kv), gain(dg_final), jnp.sum(dsink, axis=-1).reshape(1, -1)])
    loss_row = sum(t.size for t in small_w[:-1]) // D + 1
    assert loss_row < SMALL_ROWS
    g_small_local = g_small_local.at[loss_row, 0].set(loss_local)
    small_parts = _exchange("gather", [g_small_local], "gather_small_grads")[0]
    g_small = _sum8(small_parts.reshape(N_DEV, SMALL_ROWS, D), "sum_small")
    loss = g_small[loss_row, 0]
    d_small, nm_small, nv_small = _adamw(g_small, pack_small(small_w), pack_small(small_m), pack_small(small_v), "adamw_small")
    small_names = ["ffn1_norm", "mix_norm", "ffn2_norm", "kv_norm", "final_norm", "swa_sinks"]
    result = {"grad": dict(zip(small_names, unpack_small(g_small))),
              "delta": dict(zip(small_names, unpack_small(d_small))),
              "new_m": dict(zip(small_names, unpack_small(nm_small))),
              "new_v": dict(zip(small_names, unpack_small(nv_small)))}

    big = {"ffn1_w_in": (ffn1_w_in, m_ffn1_w_in, v_ffn1_w_in), "ffn1_w_out": (ffn1_w_out, m_ffn1_w_out, v_ffn1_w_out),
           "ffn2_w_in": (ffn2_w_in, m_ffn2_w_in, v_ffn2_w_in), "ffn2_w_out": (ffn2_w_out, m_ffn2_w_out, v_ffn2_w_out),
           "sb_w_qkv": (sb_w_qkv, m_sb_w_qkv, v_sb_w_qkv), "sb_w_o": (sb_w_o, m_sb_w_o, v_sb_w_o),
           "kv_w": (kv_w, m_kv_w, v_kv_w), "swa_w_q": (swa_w_q, m_swa_w_q, v_swa_w_q),
           "swa_w_o": (swa_w_o, m_swa_w_o, v_swa_w_o)}
    for nm, (w, m, v) in big.items():
        if nm in row_parts:
            three_d = lambda t: t.reshape((len(row_parts[nm]),) + t.shape[-2:])
            g, d, new_m, new_v = _sum_adamw(row_parts[nm], three_d(w), three_d(m), three_d(v), f"adamw_{nm}")
        else:
            g = grads[nm]
            two_d = lambda t: t.reshape(-1, t.shape[-1])
            d, new_m, new_v = _adamw(two_d(g), two_d(w), two_d(m), two_d(v), f"adamw_{nm}")
        result["grad"][nm] = g.reshape(w.shape)
        result["delta"][nm] = d.reshape(w.shape)
        result["new_m"][nm] = new_m.reshape(w.shape)
        result["new_v"][nm] = new_v.reshape(w.shape)

    order = ["ffn1_norm", "ffn1_w_in", "ffn1_w_out", "mix_norm", "ffn2_norm", "ffn2_w_in", "ffn2_w_out",
             "sb_w_qkv", "sb_w_o", "kv_norm", "kv_w", "swa_w_q", "swa_sinks", "swa_w_o", "final_norm"]
    outs = [result[kind][nm] for kind in ("grad", "delta", "new_m", "new_v") for nm in order]
    return (loss, dx.reshape(x.shape), *outs)
```

```python
import jax
import jax.numpy as jnp
from jax import lax
from jax.experimental import pallas as pl
from jax.experimental.pallas import tpu as pltpu

F32 = jnp.float32
BF16 = jnp.bfloat16

N_DEV = 8
HEAD_DIM = 64
LANES = 128
BLK = 128
RMS_EPS = 1e-6
FFN_RES_SCALE = 0.5
ROPE_THETA = 10000.0
ATTN_SCALE = HEAD_DIM ** -0.5
SB_LOG_FLOOR = -88.0
NEG_BIG = -1e30
VMEM_LIMIT_V7X = 56 * 1024 * 1024

ADAM_LR = 0.001
ADAM_B1 = 0.9
ADAM_B2 = 0.999
ADAM_EPS = 1e-08
ADAM_WD = 0.01
ADAM_STEP = 10

NN = ((1,), (0,))
NT = ((1,), (1,))
TN = ((0,), (0,))
TN_CHUNK = 2048
MESH = pl.DeviceIdType.MESH


def _dot(a, b, dims):
    return lax.dot_general(a, b, (dims, ((), ())), preferred_element_type=F32)


def _tile(n, pref, mult=LANES):
    if n <= pref:
        return n
    t = (pref // mult) * mult
    while t >= mult:
        if n % t == 0:
            return t
        t -= mult
    return n


def _params(*sem):
    return pltpu.CompilerParams(dimension_semantics=sem, vmem_limit_bytes=VMEM_LIMIT_V7X)


def _mm(a, b, dims, out_dtype, name, scale=1.0, res=None, tm=512, tn=512, tk=512):
    if dims == NN:
        (M, K), (_, N) = a.shape, b.shape
    elif dims == NT:
        (M, K), (N, _) = a.shape, b.shape
    else:
        (K, M), (_, N) = a.shape, b.shape
    tm, tn, tk = _tile(M, tm), _tile(N, tn), _tile(K, tk)
    nk = K // tk
    if dims == TN:
        a_spec = pl.BlockSpec((tk, tm), lambda i, j, k: (k, i))
    else:
        a_spec = pl.BlockSpec((tm, tk), lambda i, j, k: (i, k))
    if dims == NT:
        b_spec = pl.BlockSpec((tn, tk), lambda i, j, k: (j, k))
    else:
        b_spec = pl.BlockSpec((tk, tn), lambda i, j, k: (k, j))
    o_spec = pl.BlockSpec((tm, tn), lambda i, j, k: (i, j))
    has_res = res is not None

    def body(*refs):
        a_ref, b_ref = refs[0], refs[1]
        r_ref = refs[2] if has_res else None
        o_ref = refs[3] if has_res else refs[2]

        def finish(acc):
            r = acc * scale if scale != 1.0 else acc
            if has_res:
                r = r + r_ref[...]
            o_ref[...] = r.astype(out_dtype)

        p = _dot(a_ref[...].astype(BF16), b_ref[...].astype(BF16), dims)
        if nk == 1:
            finish(p)
        else:
            acc_ref = refs[-1]
            k = pl.program_id(2)

            @pl.when(k == 0)
            def _():
                acc_ref[...] = p

            @pl.when(k > 0)
            def _():
                acc_ref[...] += p

            @pl.when(k == nk - 1)
            def _():
                finish(acc_ref[...])

    in_specs = [a_spec, b_spec] + ([o_spec] if has_res else [])
    args = (a, b) + ((res,) if has_res else ())
    return pl.pallas_call(
        body, name=name,
        out_shape=jax.ShapeDtypeStruct((M, N), out_dtype),
        grid=(M // tm, N // tn, nk),
        in_specs=in_specs, out_specs=o_spec,
        scratch_shapes=[pltpu.VMEM((tm, tn), F32)] if nk > 1 else [],
        compiler_params=_params("parallel", "parallel", "arbitrary"),
    )(*args)


def _rows8(x):
    r, d = x.shape
    return jnp.sum(x.reshape(r // 8, 8, d), axis=0)


def _norm_proj(h, g, w, dims, name, rope=None, tail_t=0):
    S, D = h.shape
    N = w.shape[1] if dims == NN else w.shape[0]
    tm = _tile(S, 512, 16)

    def body(h_ref, g_ref, w_ref, *rest):
        xn_ref, y_ref = rest[-3:-1] if tail_t else rest[-2:]
        x = h_ref[...]
        r = lax.rsqrt(jnp.mean(x * x, axis=-1, keepdims=True) + RMS_EPS)
        xn = ((x * r) * g_ref[...]).astype(BF16)
        xn_ref[...] = xn
        y = _dot(xn, w_ref[...], dims)
        if rope is not None:
            cs, sn = rest[0][...], rest[1][...]
            groups = [y[:, gidx * LANES:(gidx + 1) * LANES] for gidx in range(N // LANES)]
            y = jnp.concatenate([v * cs + _swap_halves(v) * sn if gidx < rope[2] else v
                                 for gidx, v in enumerate(groups)], axis=1)
        y_ref[...] = y.astype(BF16)
        if tail_t:
            rest[-1][...] = jnp.transpose(y[:, N - tail_t:]).astype(BF16)

    row = pl.BlockSpec((tm, D), lambda i: (i, 0))
    tab = pl.BlockSpec((tm, LANES), lambda i: (i, 0))
    in_specs = [row, pl.BlockSpec((1, D), lambda i: (0, 0)), pl.BlockSpec(w.shape, lambda i: (0, 0))]
    args = (h, g.reshape(1, D), w)
    if rope is not None:
        in_specs += [tab, tab]
        args += (rope[0], rope[1])
    out_shape = [jax.ShapeDtypeStruct((S, D), BF16), jax.ShapeDtypeStruct((S, N), BF16)]
    out_specs = [row, pl.BlockSpec((tm, N), lambda i: (i, 0))]
    if tail_t:
        out_shape.append(jax.ShapeDtypeStruct((tail_t, S), BF16))
        out_specs.append(pl.BlockSpec((tail_t, tm), lambda i: (0, i)))
    return pl.pallas_call(
        body, name=name, out_shape=out_shape, grid=(S // tm,),
        in_specs=in_specs, out_specs=out_specs,
        compiler_params=_params("parallel"),
    )(*args)


def _ffn_up(h, g, win_t, name, carry=None):
    S, D = h.shape
    F = win_t.shape[0] // 2
    tm = _tile(S, 256, 16)

    def body(h_ref, g_ref, win_hbm, xn_ref, silu_ref, dsilu_ref, up_ref, act_ref, win_v, sems):
        _load_resident([(win_hbm, win_v)], sems)
        x = h_ref[...]
        r = lax.rsqrt(jnp.mean(x * x, axis=-1, keepdims=True) + RMS_EPS)
        xn = ((x * r) * g_ref[...]).astype(BF16)
        xn_ref[...] = xn
        gate = _dot(xn, win_v[:F, :], NT)
        up = _dot(xn, win_v[F:, :], NT)
        sig = 1.0 / (1.0 + jnp.exp(-gate))
        silu = gate * sig
        up_ref[...] = up.astype(BF16)
        silu_ref[...] = silu.astype(BF16)
        dsilu_ref[...] = (sig + silu * (1.0 - sig)).astype(BF16)
        act_ref[...] = (silu * up).astype(BF16)

    row = pl.BlockSpec((tm, D), lambda i: (i, 0))
    wide = pl.BlockSpec((tm, F), lambda i: (i, 0))
    hid = jax.ShapeDtypeStruct((S, F), BF16)
    return _pcall(
        body, (h, g.reshape(1, D), win_t), name=name,
        out_shape=(jax.ShapeDtypeStruct((S, D), BF16), hid, hid, hid, hid),
        grid=(S // tm,),
        in_specs=[row, pl.BlockSpec((1, D), lambda i: (0, 0)), pl.BlockSpec(memory_space=pl.ANY)],
        out_specs=(row, wide, wide, wide, wide),
        scratch_shapes=[pltpu.VMEM(win_t.shape, BF16), pltpu.SemaphoreType.DMA((1,))],
        sem=("arbitrary",), carry=carry)


def _ffn_dact(dh, wo, silu, dsilu, up, name):
    S, D = dh.shape
    F = wo.shape[0]
    tm = _tile(S, 256, 16)

    def body(dh_ref, wo_hbm, s_ref, ds_ref, u_ref, dg_ref, du_ref, wo_v, sems):
        _load_resident([(wo_hbm, wo_v)], sems)
        d = _dot(dh_ref[...].astype(BF16), wo_v[...], NT) * FFN_RES_SCALE
        du_ref[...] = (d * s_ref[...].astype(F32)).astype(BF16)
        dg_ref[...] = (d * u_ref[...].astype(F32) * ds_ref[...].astype(F32)).astype(BF16)

    wide = pl.BlockSpec((tm, F), lambda i: (i, 0))
    hid = jax.ShapeDtypeStruct((S, F), BF16)
    return pl.pallas_call(
        body, name=name, out_shape=(hid, hid),
        grid=(S // tm,),
        in_specs=[pl.BlockSpec((tm, D), lambda i: (i, 0)), pl.BlockSpec(memory_space=pl.ANY), wide, wide, wide],
        out_specs=(wide, wide),
        scratch_shapes=[pltpu.VMEM(wo.shape, BF16), pltpu.SemaphoreType.DMA((1,))],
        compiler_params=_params("arbitrary"),
    )(dh, wo, silu, dsilu, up)


def _dw_rows(srcs, x, name, carry=None, tk=TN_CHUNK):
    n = len(srcs)
    S, F = srcs[0].shape
    D = x.shape[1]
    tr, tk = _tile(F, 1408), _tile(S, tk, 16)
    nf, nk = F // tr, S // tk

    def body(*refs):
        src_refs, (x_ref, o_ref, acc_ref) = refs[:n], refs[n:]
        r, k = pl.program_id(0), pl.program_id(1)
        for s in range(n):
            @pl.when(r // nf == s)
            def _():
                p = _dot(src_refs[s][...].astype(BF16), x_ref[...], TN)

                @pl.when(k == 0)
                def _():
                    acc_ref[...] = p

                @pl.when(k > 0)
                def _():
                    acc_ref[...] += p

        @pl.when(k == nk - 1)
        def _():
            o_ref[...] = acc_ref[...].astype(BF16)

    def src_spec(s):
        return pl.BlockSpec((tk, tr), lambda r, k: (jnp.where(r // nf == s, k, 0), jnp.clip(r - s * nf, 0, nf - 1)))

    return _pcall(
        body, (*srcs, x), name=name, out_shape=jax.ShapeDtypeStruct((n * F, D), BF16),
        grid=(n * nf, nk),
        in_specs=[src_spec(s) for s in range(n)] + [pl.BlockSpec((tk, D), lambda r, k: (k, 0))],
        out_specs=pl.BlockSpec((tr, D), lambda r, k: (r, 0)),
        scratch_shapes=[pltpu.VMEM((tr, D), F32)],
        sem=("arbitrary", "arbitrary"), carry=carry)


def _dx_norm_bwd(terms, h, g, res, name, carry=None, tm=256):
    S, D = h.shape
    tm = _tile(S, tm, 16)
    n = len(terms)

    def body(*refs):
        dy_refs, w_refs = refs[:n], refs[n:2 * n]
        h_ref, g_ref, r_ref, dh_ref, dg_ref = refs[2 * n:]
        d = _dot(dy_refs[0][...].astype(BF16), w_refs[0][...], terms[0][2])
        for t in range(1, n):
            d = d + _dot(dy_refs[t][...].astype(BF16), w_refs[t][...], terms[t][2])
        x = h_ref[...]
        r = lax.rsqrt(jnp.mean(x * x, axis=-1, keepdims=True) + RMS_EPS)
        xhat = x * r
        dxh = d * g_ref[...]
        c = jnp.mean(dxh * xhat, axis=-1, keepdims=True)
        dh_ref[...] = r * (dxh - xhat * c) + r_ref[...]
        part = _rows8(d * xhat)

        @pl.when(pl.program_id(0) == 0)
        def _():
            dg_ref[...] = part

        @pl.when(pl.program_id(0) > 0)
        def _():
            dg_ref[...] += part

    def w_spec(w, nblk, blk):
        return pl.BlockSpec((w.shape[0] // nblk, w.shape[1]), lambda i: (blk, 0))

    row = pl.BlockSpec((tm, D), lambda i: (i, 0))
    in_specs = [pl.BlockSpec((tm, t[0].shape[1]), lambda i: (i, 0)) for t in terms]
    in_specs += [w_spec(t[1], t[3], t[4]) for t in terms]
    in_specs += [row, pl.BlockSpec((1, D), lambda i: (0, 0)), row]
    return _pcall(
        body, (*[t[0] for t in terms], *[t[1] for t in terms], h, g.reshape(1, D), res), name=name,
        out_shape=(jax.ShapeDtypeStruct((S, D), F32), jax.ShapeDtypeStruct((8, D), F32)),
        grid=(S // tm,),
        in_specs=in_specs,
        out_specs=(row, pl.BlockSpec((8, D), lambda i: (0, 0))),
        sem=("arbitrary",), carry=carry)


def _load_resident(pairs, sems):
    @pl.when(pl.program_id(0) == 0)
    def _():
        copies = [pltpu.make_async_copy(src, dst, sems.at[n]) for n, (src, dst) in enumerate(pairs)]
        for cp in copies:
            cp.start()
        for cp in copies:
            cp.wait()


def _loss_tail(y_in, g, tgt):
    D = y_in.shape[-1]
    r = lax.rsqrt(jnp.mean(y_in * y_in, axis=-1, keepdims=True) + RMS_EPS)
    xhat = y_in * r
    err = xhat * g - tgt
    d = err * (1.0 / D)
    dxh = d * g
    c = jnp.mean(dxh * xhat, axis=-1, keepdims=True)
    return r * (dxh - xhat * c), _rows8(d * xhat), _rows8(err * err)


def _ffn_fwd_fused(h, g, win_t, wo, name, carry=None, loss=None):
    S, D = h.shape
    F = wo.shape[0]
    tm = _tile(S, 256, 16)
    n_head = 3 if loss is not None else 1

    def body(h_ref, g_ref, win_hbm, wo_hbm, *rest):
        lead, (xn_ref, silu_ref, dsilu_ref, up_ref, act_ref, win_v, wo_v, sems) = rest[:-8], rest[-8:]
        _load_resident([(win_hbm, win_v), (wo_hbm, wo_v)], sems)
        x = h_ref[...]
        r = lax.rsqrt(jnp.mean(x * x, axis=-1, keepdims=True) + RMS_EPS)
        xn = ((x * r) * g_ref[...]).astype(BF16)
        xn_ref[...] = xn
        gate = _dot(xn, win_v[:F, :], NT)
        up = _dot(xn, win_v[F:, :], NT)
        sig = 1.0 / (1.0 + jnp.exp(-gate))
        silu = gate * sig
        act = (silu * up).astype(BF16)
        up_ref[...] = up.astype(BF16)
        silu_ref[...] = silu.astype(BF16)
        dsilu_ref[...] = (sig + silu * (1.0 - sig)).astype(BF16)
        act_ref[...] = act
        out = x + FFN_RES_SCALE * _dot(act, wo_v[...], NN)
        if loss is None:
            lead[0][...] = out
        else:
            gf_ref, t_ref, dy_ref, dgf_ref, sq_ref = lead
            dy, dgf, sq = _loss_tail(out, gf_ref[...], t_ref[...])
            dy_ref[...] = dy

            @pl.when(pl.program_id(0) == 0)
            def _():
                dgf_ref[...] = dgf
                sq_ref[...] = sq

            @pl.when(pl.program_id(0) > 0)
            def _():
                dgf_ref[...] += dgf
                sq_ref[...] += sq

    row = pl.BlockSpec((tm, D), lambda i: (i, 0))
    vec = pl.BlockSpec((1, D), lambda i: (0, 0))
    acc = pl.BlockSpec((8, D), lambda i: (0, 0))
    wide = pl.BlockSpec((tm, F), lambda i: (i, 0))
    hbm = pl.BlockSpec(memory_space=pl.ANY)
    hid = jax.ShapeDtypeStruct((S, F), BF16)
    full = jax.ShapeDtypeStruct((S, D), F32)
    part = jax.ShapeDtypeStruct((8, D), F32)
    args, in_specs = (h, g.reshape(1, D), win_t, wo), [row, vec, hbm, hbm]
    lead_shapes, lead_specs = (full,), (row,)
    if loss is not None:
        args, in_specs = args + (loss[0].reshape(1, D), loss[1]), in_specs + [vec, row]
        lead_shapes, lead_specs = (full, part, part), (row, acc, acc)
    res, got = _pcall(
        body, args, name=name,
        out_shape=lead_shapes + (jax.ShapeDtypeStruct((S, D), BF16), hid, hid, hid, hid),
        grid=(S // tm,),
        in_specs=in_specs,
        out_specs=lead_specs + (row, wide, wide, wide, wide),
        scratch_shapes=[pltpu.VMEM(win_t.shape, BF16), pltpu.VMEM(wo.shape, BF16), pltpu.SemaphoreType.DMA((2,))],
        sem=("arbitrary",), carry=carry)
    first = res[0] if loss is None else tuple(res[:3])
    return first, tuple(res[n_head:]), got


def _ffn_bwd_fused(dh, h, g, win_t, wo, silu, dsilu, up, name, carry=None):
    S, D = h.shape
    F = wo.shape[0]
    tm = _tile(S, 256, 16)

    def body(dh_ref, h_ref, g_ref, s_ref, ds_ref, u_ref, win_hbm, wo_hbm,
             dhin_ref, dgain_ref, dgate_ref, dup_ref, win_v, wo_v, sems):
        _load_resident([(win_hbm, win_v), (wo_hbm, wo_v)], sems)
        dhv = dh_ref[...]
        d = _dot(dhv.astype(BF16), wo_v[...], NT) * FFN_RES_SCALE
        dup = (d * s_ref[...].astype(F32)).astype(BF16)
        dgate = (d * u_ref[...].astype(F32) * ds_ref[...].astype(F32)).astype(BF16)
        dup_ref[...] = dup
        dgate_ref[...] = dgate
        dxn = _dot(dgate, win_v[:F, :], NN) + _dot(dup, win_v[F:, :], NN)
        x = h_ref[...]
        r = lax.rsqrt(jnp.mean(x * x, axis=-1, keepdims=True) + RMS_EPS)
        xhat = x * r
        dxh = dxn * g_ref[...]
        c = jnp.mean(dxh * xhat, axis=-1, keepdims=True)
        dhin_ref[...] = r * (dxh - xhat * c) + dhv
        part = _rows8(dxn * xhat)

        @pl.when(pl.program_id(0) == 0)
        def _():
            dgain_ref[...] = part

        @pl.when(pl.program_id(0) > 0)
        def _():
            dgain_ref[...] += part

    row = pl.BlockSpec((tm, D), lambda i: (i, 0))
    wide = pl.BlockSpec((tm, F), lambda i: (i, 0))
    hbm = pl.BlockSpec(memory_space=pl.ANY)
    hid = jax.ShapeDtypeStruct((S, F), BF16)
    return _pcall(
        body, (dh, h, g.reshape(1, D), silu, dsilu, up, win_t, wo), name=name,
        out_shape=(jax.ShapeDtypeStruct((S, D), F32), jax.ShapeDtypeStruct((8, D), F32), hid, hid),
        grid=(S // tm,),
        in_specs=[row, row, pl.BlockSpec((1, D), lambda i: (0, 0)), wide, wide, wide, hbm, hbm],
        out_specs=(row, pl.BlockSpec((8, D), lambda i: (0, 0)), wide, wide),
        scratch_shapes=[pltpu.VMEM(win_t.shape, BF16), pltpu.VMEM(wo.shape, BF16), pltpu.SemaphoreType.DMA((2,))],
        sem=("arbitrary",), carry=carry)


def _rope_tables(S):
    half = HEAD_DIM // 2
    inv_freq = ROPE_THETA ** (-jnp.arange(half, dtype=F32) / half)
    ang = jnp.arange(S).astype(F32)[:, None] * inv_freq[None, :]
    cos, sin = jnp.cos(ang), jnp.sin(ang)
    cos_t = jnp.tile(cos, (1, LANES // half))
    sin_t = jnp.tile(jnp.concatenate([-sin, sin], axis=1), (1, LANES // HEAD_DIM))
    return cos_t, sin_t


def _swap_halves(x):
    lane = lax.broadcasted_iota(jnp.int32, x.shape, 1)
    first = (lane % HEAD_DIM) < (HEAD_DIM // 2)
    return jnp.where(first, pltpu.roll(x, LANES - HEAD_DIM // 2, 1), pltpu.roll(x, HEAD_DIM // 2, 1))


def _rotary_bwd(dys, cos_t, sin_t, n_rot, name):
    S = dys[0].shape[0]
    widths = [dy.shape[1] for dy in dys]
    ts = _tile(S, 512, 16)

    def body(*refs):
        x_refs, (c_ref, s_ref, o_ref) = refs[:len(dys)], refs[len(dys):]
        cs, sn = c_ref[...], s_ref[...]
        gidx = 0
        for x_ref, width in zip(x_refs, widths):
            for g in range(width // LANES):
                v = x_ref[:, g * LANES:(g + 1) * LANES].astype(F32)
                if gidx < n_rot:
                    v = v * cs + _swap_halves(v * sn)
                o_ref[:, gidx * LANES:(gidx + 1) * LANES] = v.astype(BF16)
                gidx += 1

    tab = pl.BlockSpec((ts, LANES), lambda i: (i, 0))
    return pl.pallas_call(
        body, name=name, out_shape=jax.ShapeDtypeStruct((S, sum(widths)), BF16),
        grid=(S // ts,),
        in_specs=[pl.BlockSpec((ts, width), lambda i: (i, 0)) for width in widths] + [tab, tab],
        out_specs=pl.BlockSpec((ts, sum(widths)), lambda i: (i, 0)),
        compiler_params=_params("parallel"),
    )(*dys, cos_t, sin_t)


def _head_masks():
    lane = lax.broadcasted_iota(jnp.int32, (BLK, LANES), 1)
    return lane < HEAD_DIM


def _split_bf16(x):
    hi = x.astype(BF16)
    lo = (x - hi.astype(F32)).astype(BF16)
    return hi, lo


def _sb_scores(qh, ks, carry, diag, tri_excl, strict):
    n_heads = len(qh)
    zs = [_dot(ks[n], qh[n], NT) for n in range(n_heads)]
    a_l, b_l, split_l = [], [], []
    for z in zs:
        a = jnp.minimum(z, 0.0) - jnp.log(1.0 + jnp.exp(-jnp.abs(z)))
        b = a - z
        if diag:
            b = jnp.where(strict, b, 0.0)
        a_l.append(a)
        b_l.append(b)
        split_l.append(_split_bf16(b))
    sufs = [_dot(tri_excl, hi, NN) + _dot(tri_excl, lo, NN) for hi, lo in split_l]
    w_l = []
    for n in range(n_heads):
        w = jnp.exp(a_l[n] + sufs[n] + carry[n])
        if diag:
            w = jnp.where(strict, w, 0.0)
        w_l.append(w)
    return a_l, b_l, w_l


SB_FWD_PAIRS = 4
SB_FWD_QBLOCKS = 4
SB_BWD_PAIRS = 2
SB_BWD_QBLOCKS = 4


def _any_alive(carries):
    top = carries[0]
    for c in carries[1:]:
        top = jnp.maximum(top, c)
    return (jnp.max(top) > SB_LOG_FLOOR).astype(jnp.int32)


def _sb_masks():
    row = lax.broadcasted_iota(jnp.int32, (BLK, BLK), 0)
    col = lax.broadcasted_iota(jnp.int32, (BLK, BLK), 1)
    tri_excl = jnp.where(col > row, 1.0, 0.0).astype(BF16)
    tri_incl = jnp.where(col >= row, 1.0, 0.0).astype(BF16)
    return row < HEAD_DIM, row < col, tri_excl, tri_incl


def _sb_fwd(qkv, kv_t, name, carry=None):
    S, D3 = qkv.shape
    D = D3 // 3
    npair, nb = D // LANES, S // BLK
    P = min(SB_FWD_PAIRS, npair)
    ngroup = npair // P
    W = P * LANES

    QB = SB_FWD_QBLOCKS if nb % SB_FWD_QBLOCKS == 0 else 1
    nch = QB * 2 * P

    def body(q_ref, k_ref, vt_ref, o_ref):
        i_first = pl.program_id(1) * QB
        m0 = _head_masks()
        top, strict, tri_excl, _ = _sb_masks()
        zq = jnp.zeros((BLK, LANES), BF16)
        lanes = [slice(p * LANES, (p + 1) * LANES) for p in range(P)]
        qh = []
        for qb in range(QB):
            for sl in lanes:
                q2 = q_ref[qb * BLK:(qb + 1) * BLK, sl] * ATTN_SCALE
                qh += [jnp.where(m0, q2, zq), jnp.where(m0, zq, q2)]

        def block(qbs, js, carry, acc, diag):
            offs = [pl.multiple_of(j * BLK, BLK) for j in js]
            ks, vth, qs = [], [], []
            for n_qb, qb in enumerate(qbs):
                qs += qh[qb * 2 * P:(qb + 1) * 2 * P]
                for sl in lanes:
                    k2 = k_ref[pl.ds(offs[n_qb], BLK), sl]
                    vt = vt_ref[sl, pl.ds(offs[n_qb], BLK)]
                    ks += [k2, k2]
                    vth += [jnp.where(top, vt, zq), jnp.where(top, zq, vt)]
            _, b_l, w_l = _sb_scores(qs, ks, carry, diag, tri_excl, strict)
            wb = [w.astype(BF16) for w in w_l]
            new_acc = [acc[m] + _dot(vth[2 * m], wb[2 * m], NN) + _dot(vth[2 * m + 1], wb[2 * m + 1], NN)
                       for m in range(len(qbs) * P)]
            new_carry = [carry[n] + jnp.sum(b_l[n], axis=0, keepdims=True) for n in range(len(carry))]
            return new_carry, new_acc

        every = list(range(QB))
        c0 = jnp.zeros((1, BLK), F32)
        carry, acc = block(every, [i_first + qb for qb in every], [c0] * nch,
                           [jnp.zeros((LANES, BLK), F32)] * (QB * P), True)
        carry = [jnp.where(i_first > 0, c, NEG_BIG) for c in carry[:2 * P]] + carry[2 * P:]
        carry, acc = block(every, [jnp.maximum(i_first + qb - 1, 0) for qb in every], carry, acc, False)

        for qb in range(QB):
            i_qb = i_first + qb
            sub = slice(qb * 2 * P, (qb + 1) * 2 * P)

            def cond(st):
                return jnp.logical_and(i_qb - st[0] >= 0, st[1] > 0)

            def step(st, qb=qb, i_qb=i_qb):
                t, _, c_qb, a_qb = st
                c_qb, a_qb = block([qb], [i_qb - t], c_qb, a_qb, False)
                return t + 1, _any_alive(c_qb), c_qb, a_qb

            st = lax.while_loop(cond, step, (2, _any_alive(carry[sub]), carry[sub], acc[qb * P:(qb + 1) * P]))
            for p, sl in enumerate(lanes):
                o_ref[qb * BLK:(qb + 1) * BLK, sl] = jnp.transpose(st[3][p])

    return _pcall(
        body, (qkv, qkv, kv_t), name=name, out_shape=jax.ShapeDtypeStruct((S, D), F32),
        grid=(ngroup, nb // QB),
        in_specs=[pl.BlockSpec((QB * BLK, W), lambda g, i: (i, g)),
                  pl.BlockSpec((S, W), lambda g, i: (0, ngroup + g)),
                  pl.BlockSpec((W, S), lambda g, i: (ngroup + g, 0))],
        out_specs=pl.BlockSpec((QB * BLK, W), lambda g, i: (i, g)),
        sem=("arbitrary", "arbitrary"), carry=carry)


def _sb_bwd(qkv, kv_t, o, do, name, carry=None):
    S, D3 = qkv.shape
    D = D3 // 3
    npair, nb = D // LANES, S // BLK
    P = min(SB_BWD_PAIRS, npair)
    ngroup = npair // P
    W = P * LANES

    QB = SB_BWD_QBLOCKS if nb % SB_BWD_QBLOCKS == 0 else 1
    nch = QB * 2 * P

    def body(q_ref, o_ref, do_ref, qkv_hbm, kt_hbm, dq_ref, dk_ref, dv_ref, k_ref, v_ref, kt_ref, sems):
        grp = pl.program_id(0)
        i_first = pl.program_id(1) * QB
        m0 = _head_masks()
        top, strict, tri_excl, tri_incl = _sb_masks()
        zq = jnp.zeros((BLK, LANES), BF16)
        lanes = [slice(p * LANES, (p + 1) * LANES) for p in range(P)]

        @pl.when(pl.program_id(1) == 0)
        def _():
            copies = [pltpu.make_async_copy(qkv_hbm.at[:, pl.ds(pl.multiple_of((c * ngroup + grp) * W, LANES), W)],
                                            ref, sems.at[c - 1]) for c, ref in ((1, k_ref), (2, v_ref))]
            copies.append(pltpu.make_async_copy(kt_hbm.at[pl.ds(pl.multiple_of(grp * W, LANES), W), :],
                                                kt_ref, sems.at[2]))
            for cp in copies:
                cp.start()
            dk_ref[...] = jnp.zeros_like(dk_ref)
            dv_ref[...] = jnp.zeros_like(dv_ref)
            for cp in copies:
                cp.wait()

        qh, doh, delta = [], [], []
        for qb in range(QB):
            rs = slice(qb * BLK, (qb + 1) * BLK)
            for sl in lanes:
                q2, do2 = q_ref[rs, sl] * ATTN_SCALE, do_ref[rs, sl]
                qh += [jnp.where(m0, q2, zq), jnp.where(m0, zq, q2)]
                doh += [jnp.where(m0, do2, zq), jnp.where(m0, zq, do2)]
                prod_t = jnp.transpose(do2.astype(F32) * o_ref[rs, sl])
                delta += [jnp.sum(jnp.where(top, prod_t, 0.0), axis=0, keepdims=True),
                          jnp.sum(jnp.where(top, 0.0, prod_t), axis=0, keepdims=True)]

        def block(qbs, js, valid, cb, cg, dq, diag):
            offs = [pl.multiple_of(j * BLK, BLK) for j in js]
            n_ch = len(qbs) * 2 * P
            ks, vs, kth, qs, dos, dls = [], [], [], [], [], []
            for n_qb, qb in enumerate(qbs):
                chains = slice(qb * 2 * P, (qb + 1) * 2 * P)
                qs, dos, dls = qs + qh[chains], dos + doh[chains], dls + delta[chains]
                for sl in lanes:
                    k2, v2 = k_ref[pl.ds(offs[n_qb], BLK), sl], v_ref[pl.ds(offs[n_qb], BLK), sl]
                    ks += [k2, k2]
                    vs += [v2, v2]
                    kt = kt_ref[sl, pl.ds(offs[n_qb], BLK)] * ATTN_SCALE
                    kth += [jnp.where(top, kt, zq), jnp.where(top, zq, kt)]
            dws = [_dot(vs[n], dos[n], NT) for n in range(n_ch)]
            a_l, b_l, w_l = _sb_scores(qs, ks, cb, diag, tri_excl, strict)
            wb = [w.astype(BF16) for w in w_l]
            g_l = [dws[n] * wb[n].astype(F32) for n in range(n_ch)]
            gsplit = [_split_bf16(g) for g in g_l]
            gincs = [_dot(tri_incl, hi, NN) + _dot(tri_incl, lo, NN) for hi, lo in gsplit]
            dzs = []
            for n in range(n_ch):
                beta = jnp.exp(a_l[n])
                dz = g_l[n] - beta * (g_l[n] + ((dls[n] - cg[n]) - gincs[n]))
                if diag:
                    dz = jnp.where(strict, dz, 0.0)
                if valid[n // (2 * P)] is not None:
                    dz = jnp.where(valid[n // (2 * P)], dz, 0.0)
                dzs.append(dz.astype(BF16))
            ndq = []
            for n_qb in range(len(qbs)):
                for p, sl in enumerate(lanes):
                    n0 = n_qb * 2 * P + 2 * p
                    ndq.append(dq[n_qb * P + p] + _dot(kth[n0], dzs[n0], NN) + _dot(kth[n0 + 1], dzs[n0 + 1], NN))
                    dk_ref[pl.ds(offs[n_qb], BLK), sl] += _dot(dzs[n0], qs[n0], NN) + _dot(dzs[n0 + 1], qs[n0 + 1], NN)
                    dv_ref[pl.ds(offs[n_qb], BLK), sl] += _dot(wb[n0], dos[n0], NN) + _dot(wb[n0 + 1], dos[n0 + 1], NN)
            ncb = [cb[n] + jnp.sum(b_l[n], axis=0, keepdims=True) for n in range(n_ch)]
            ncg = [cg[n] + jnp.sum(g_l[n], axis=0, keepdims=True) for n in range(n_ch)]
            return ncb, ncg, ndq

        every = list(range(QB))
        c0 = jnp.zeros((1, BLK), F32)
        cb, cg, dq = block(every, [i_first + qb for qb in every], [None] * QB, [c0] * nch, [c0] * nch,
                           [jnp.zeros((LANES, BLK), F32)] * (QB * P), True)
        has_prev = i_first > 0
        cb = [jnp.where(has_prev, c, NEG_BIG) for c in cb[:2 * P]] + cb[2 * P:]
        cb, cg, dq = block(every, [jnp.maximum(i_first + qb - 1, 0) for qb in every], [has_prev] + [None] * (QB - 1),
                           cb, cg, dq, False)

        for qb in range(QB):
            i_qb = i_first + qb
            sub = slice(qb * 2 * P, (qb + 1) * 2 * P)

            def cond(st):
                return jnp.logical_and(i_qb - st[0] >= 0, st[1] > 0)

            def step(st, qb=qb, i_qb=i_qb):
                t, _, b_qb, g_qb, dq_qb = st
                b_qb, g_qb, dq_qb = block([qb], [i_qb - t], [None], b_qb, g_qb, dq_qb, False)
                return t + 1, _any_alive(b_qb), b_qb, g_qb, dq_qb

            st = lax.while_loop(cond, step, (2, _any_alive(cb[sub]), cb[sub], cg[sub], dq[qb * P:(qb + 1) * P]))
            for p, sl in enumerate(lanes):
                dq_ref[qb * BLK:(qb + 1) * BLK, sl] = jnp.transpose(st[4][p]).astype(BF16)

    blk = pl.BlockSpec((QB * BLK, W), lambda g, i: (i, g))
    col_all = pl.BlockSpec((S, W), lambda g, i: (0, g))
    hbm = pl.BlockSpec(memory_space=pl.ANY)
    return _pcall(
        body, (qkv, o, do, qkv, kv_t), name=name,
        out_shape=(jax.ShapeDtypeStruct((S, D), BF16), jax.ShapeDtypeStruct((S, D), F32),
                   jax.ShapeDtypeStruct((S, D), F32)),
        grid=(ngroup, nb // QB),
        in_specs=[blk, blk, blk, hbm, hbm],
        out_specs=(blk, col_all, col_all),
        scratch_shapes=[pltpu.VMEM((S, W), BF16), pltpu.VMEM((S, W), BF16), pltpu.VMEM((W, S), BF16),
                        pltpu.SemaphoreType.DMA((3,))],
        sem=("arbitrary", "arbitrary"), carry=carry)


SWA_Q_GROUPS = 4


def _roll_heads(x):
    return pltpu.roll(x.astype(F32), HEAD_DIM, 1).astype(BF16)


def _roll_rows(x):
    return pltpu.roll(x.astype(F32), HEAD_DIM, 0).astype(BF16)


def _swa_valid(i):
    k = lax.broadcasted_iota(jnp.int32, (2 * BLK, BLK), 0)
    q = lax.broadcasted_iota(jnp.int32, (2 * BLK, BLK), 1)
    diff = q + BLK - k
    return (diff >= 0) & (diff < BLK) & ((i > 0) | (k >= BLK))


def _swa_probs(z, valid, sink):
    z = jnp.where(valid, z * ATTN_SCALE, NEG_BIG)
    mx = jnp.maximum(jnp.max(z, axis=0, keepdims=True), sink)
    p = jnp.exp(z - mx)
    ps = jnp.exp(sink - mx)
    inv = 1.0 / (jnp.sum(p, axis=0, keepdims=True) + ps)
    return p * inv, ps * inv


def _swa_operands(q_ref, kc_ref, kp_ref, vc_ref, vp_ref, tc_ref, tp_ref, s_ref, nkvp):
    m0 = _head_masks()
    top = lax.broadcasted_iota(jnp.int32, (LANES, 2 * BLK), 0) < HEAD_DIM
    heads = []
    for m in range(nkvp):
        pair = slice(m * LANES, (m + 1) * LANES)
        kk = jnp.concatenate([kp_ref[:, pair], kc_ref[:, pair]], axis=0)
        vv = jnp.concatenate([vp_ref[:, pair], vc_ref[:, pair]], axis=0)
        tt = jnp.concatenate([tp_ref[pair, :], tc_ref[pair, :]], axis=1)
        ksw, vsw, tsw = _roll_heads(kk), _roll_heads(vv), _roll_rows(tt)
        zt = jnp.zeros_like(tt)
        for c in range(SWA_Q_GROUPS):
            q_lanes = slice((m * SWA_Q_GROUPS + c) * LANES, (m * SWA_Q_GROUPS + c + 1) * LANES)
            qc = q_ref[:, q_lanes]
            zq = jnp.zeros_like(qc)
            for u in range(2):
                same = u == c // 2
                sel = (lambda x, z, mk: jnp.where(mk, x, z)) if u == 0 else (lambda x, z, mk: jnp.where(mk, z, x))
                heads.append(dict(
                    m=m, q_lanes=q_lanes, same=same, sel=sel, qm=sel(qc, zq, m0),
                    k=kk if same else ksw, v=vv if same else vsw,
                    tm=sel(tt if same else tsw, zt, top),
                    sink=s_ref[0, (m * SWA_Q_GROUPS + c) * 2 + u]))
    return heads, m0


def _swa_specs(D, half, t_block):
    prev = lambda i: jnp.maximum(i - 1, 0)
    return [pl.BlockSpec((BLK, D), lambda i: (i, 0)),
            pl.BlockSpec((BLK, half), lambda i: (i, 0)),
            pl.BlockSpec((BLK, half), lambda i: (prev(i), 0)),
            pl.BlockSpec((BLK, half), lambda i: (i, 1)),
            pl.BlockSpec((BLK, half), lambda i: (prev(i), 1)),
            pl.BlockSpec((half, BLK), lambda i: (t_block, i)),
            pl.BlockSpec((half, BLK), lambda i: (t_block, prev(i))),
            pl.BlockSpec(memory_space=pltpu.SMEM)]


def _swa_fwd(q, kv, kv_t, sinks, name):
    S, D = q.shape
    half = kv.shape[1] // 2
    nkvp = half // LANES

    def body(q_ref, kc_ref, kp_ref, vc_ref, vp_ref, tc_ref, tp_ref, s_ref, o_ref):
        valid = _swa_valid(pl.program_id(0))
        heads, _ = _swa_operands(q_ref, kc_ref, kp_ref, vc_ref, vp_ref, tc_ref, tp_ref, s_ref, nkvp)
        zs = [_dot(hd["k"], hd["qm"], NT) for hd in heads]
        ps = [_swa_probs(z, valid, hd["sink"])[0].astype(BF16) for z, hd in zip(zs, heads)]
        for n in range(0, len(heads), 2):
            o_t = _dot(heads[n]["tm"], ps[n], NN) + _dot(heads[n + 1]["tm"], ps[n + 1], NN)
            o_ref[:, heads[n]["q_lanes"]] = jnp.transpose(o_t)

    return pl.pallas_call(
        body, name=name, out_shape=jax.ShapeDtypeStruct((S, D), F32),
        grid=(S // BLK,),
        in_specs=_swa_specs(D, half, 1),
        out_specs=pl.BlockSpec((BLK, D), lambda i: (i, 0)),
        compiler_params=_params("arbitrary"),
    )(q, kv, kv, kv, kv, kv_t, kv_t, sinks)


def _swa_bwd(q, kv, kv_t, sinks, o, do, cos_t, sin_t, name, carry=None):
    S, D = q.shape
    half = kv.shape[1] // 2
    nkvp = half // LANES
    nh = nkvp * 2 * SWA_Q_GROUPS

    def body(q_ref, kc_ref, kp_ref, vc_ref, vp_ref, tc_ref, tp_ref, s_ref, o_ref, do_ref, c_ref, sn_ref,
             dq_ref, dk_ref, dv_ref, ds_ref):
        i = pl.program_id(0)
        valid = _swa_valid(i)
        heads, m0 = _swa_operands(q_ref, kc_ref, kp_ref, vc_ref, vp_ref, tc_ref, tp_ref, s_ref, nkvp)
        top_q = lax.broadcasted_iota(jnp.int32, (LANES, BLK), 0) < HEAD_DIM

        @pl.when(i == 0)
        def _():
            dk_ref[...] = jnp.zeros_like(dk_ref)
            dv_ref[...] = jnp.zeros_like(dv_ref)
            ds_ref[...] = jnp.zeros_like(ds_ref)

        doms, deltas = [], []
        for n in range(0, nh, 2):
            doc = do_ref[:, heads[n]["q_lanes"]]
            prod_t = jnp.transpose(doc.astype(F32) * o_ref[:, heads[n]["q_lanes"]])
            for hd in heads[n:n + 2]:
                doms.append(hd["sel"](doc, jnp.zeros_like(doc), m0))
                deltas.append(jnp.sum(hd["sel"](prod_t, 0.0, top_q), axis=0, keepdims=True))
        zs = [_dot(hd["k"], hd["qm"], NT) for hd in heads]
        dps = [_dot(hd["v"], dom, NT) for dom, hd in zip(doms, heads)]
        pbs, dscs = [], []
        for n, hd in enumerate(heads):
            p, psink = _swa_probs(zs[n], valid, hd["sink"])
            pbs.append(p.astype(BF16))
            dscs.append((p * (dps[n] - deltas[n]) * ATTN_SCALE).astype(BF16))
            ds_ref[n:n + 1, :] += -(psink * deltas[n])
        for n in range(0, nh, 2):
            dq_rot = jnp.transpose(_dot(heads[n]["tm"], dscs[n], NN) + _dot(heads[n + 1]["tm"], dscs[n + 1], NN))
            dq_ref[:, heads[n]["q_lanes"]] = (
                dq_rot * c_ref[...] + _swap_halves(dq_rot * sn_ref[...])).astype(BF16)
        acc = {}
        for n, hd in enumerate(heads):
            dk_n = _dot(dscs[n], hd["qm"], NN)
            dv_n = _dot(pbs[n], doms[n], NN)
            for key, val in ((("k", hd["m"], hd["same"]), dk_n), (("v", hd["m"], hd["same"]), dv_n)):
                acc[key] = val if key not in acc else acc[key] + val
        poff = pl.multiple_of(jnp.maximum(i - 1, 0) * BLK, BLK)
        coff = pl.multiple_of(i * BLK, BLK)
        for m in range(nkvp):
            pair = slice(m * LANES, (m + 1) * LANES)
            dkk = acc["k", m, True] + pltpu.roll(acc["k", m, False], HEAD_DIM, 1)
            dvv = acc["v", m, True] + pltpu.roll(acc["v", m, False], HEAD_DIM, 1)
            dk_ref[pl.ds(poff, BLK), pair] += dkk[:BLK]
            dv_ref[pl.ds(poff, BLK), pair] += dvv[:BLK]
            dk_ref[pl.ds(coff, BLK), pair] += dkk[BLK:]
            dv_ref[pl.ds(coff, BLK), pair] += dvv[BLK:]

    qblk = pl.BlockSpec((BLK, D), lambda i: (i, 0))
    whole = pl.BlockSpec((S, half), lambda i: (0, 0))
    tab = pl.BlockSpec((BLK, LANES), lambda i: (i, 0))
    return _pcall(
        body, (q, kv, kv, kv, kv, kv_t, kv_t, sinks, o, do, cos_t, sin_t), name=name,
        out_shape=(jax.ShapeDtypeStruct((S, D), BF16),
                   jax.ShapeDtypeStruct((S, half), F32),
                   jax.ShapeDtypeStruct((S, half), F32),
                   jax.ShapeDtypeStruct((nh, LANES), F32)),
        grid=(S // BLK,),
        in_specs=_swa_specs(D, half, 0) + [qblk, qblk, tab, tab],
        out_specs=(qblk, whole, whole, pl.BlockSpec((nh, LANES), lambda i: (0, 0))),
        sem=("arbitrary",), carry=carry)


def _dev_index(p):
    return 4 * p[0] + 2 * p[1] + p[2]


def _gather_plan(x_refs, out_refs, send_sems, recv_sems, local_sems):
    n = len(x_refs)
    x_, y_, c_ = lax.axis_index("x"), lax.axis_index("y"), lax.axis_index("c")
    me, sibling = (x_, y_, c_), (x_, y_, 1 - c_)
    chips = [(1 - x_, y_), (x_, 1 - y_), (1 - x_, 1 - y_)]

    def copy(t, k, block, to, src=None):
        dst = out_refs[t].at[_dev_index(block)]
        return pltpu.make_async_remote_copy(
            src_ref=dst if src is None else src, dst_ref=dst,
            send_sem=send_sems.at[7 * t + k], recv_sem=recv_sems.at[7 * t + k],
            device_id=to, device_id_type=MESH)

    mine = [pltpu.make_async_copy(x_refs[t], out_refs[t].at[_dev_index(me)], local_sems.at[t]) for t in range(n)]
    first = []
    for t in range(n):
        first.append(copy(t, 0, me, sibling, src=x_refs[t]))
        first += [copy(t, 1 + j, me, (*chip, c_), src=x_refs[t]) for j, chip in enumerate(chips)]
    arrived = lambda t, j: copy(t, 1 + j, (*chips[j], c_), me)
    forward = lambda t, j: copy(t, 4 + j, (*chips[j], c_), sibling)
    from_sibling = lambda t: copy(t, 0, sibling, me)
    forwarded = lambda t, j: copy(t, 4 + j, (*chips[j], 1 - c_), me)
    return n, mine, first, arrived, forward, from_sibling, forwarded


def _gather_start(x_refs, out_refs, send_sems, recv_sems, local_sems):
    _, mine, first, *_ = _gather_plan(x_refs, out_refs, send_sems, recv_sems, local_sems)
    for cp in mine + first:
        cp.start()


def _gather_forward(x_refs, out_refs, send_sems, recv_sems, local_sems):
    n, _, _, arrived, forward, _, _ = _gather_plan(x_refs, out_refs, send_sems, recv_sems, local_sems)
    for j in range(3):
        for t in range(n):
            arrived(t, j).wait_recv()
            forward(t, j).start()


def _gather_finish(x_refs, out_refs, send_sems, recv_sems, local_sems):
    n, mine, first, _, forward, from_sibling, forwarded = _gather_plan(
        x_refs, out_refs, send_sems, recv_sems, local_sems)
    for t in range(n):
        from_sibling(t).wait_recv()
    for j in range(3):
        for t in range(n):
            forwarded(t, j).wait_recv()
    for cp in first + [forward(t, j) for j in range(3) for t in range(n)]:
        cp.wait_send()
    for cp in mine:
        cp.wait()


def _scatter_plan(b_refs, out_refs, send_sems, recv_sems, local_sems):
    n = len(b_refs)
    x_, y_, c_ = lax.axis_index("x"), lax.axis_index("y"), lax.axis_index("c")
    my_idx = _dev_index((x_, y_, c_))
    mine = [pltpu.make_async_copy(b_refs[t].at[my_idx], out_refs[t].at[my_idx], local_sems.at[t]) for t in range(n)]
    copies = []
    for t in range(n):
        for k in range(1, N_DEV):
            peer = (x_ ^ ((k >> 2) & 1), y_ ^ ((k >> 1) & 1), c_ ^ (k & 1))
            copies.append(pltpu.make_async_remote_copy(
                src_ref=b_refs[t].at[_dev_index(peer)], dst_ref=out_refs[t].at[my_idx],
                send_sem=send_sems.at[7 * t + k - 1], recv_sem=recv_sems.at[7 * t + k - 1],
                device_id=peer, device_id_type=MESH))
    return mine, copies


def _scatter_start(b_refs, out_refs, send_sems, recv_sems, local_sems):
    mine, copies = _scatter_plan(b_refs, out_refs, send_sems, recv_sems, local_sems)
    for cp in mine + copies:
        cp.start()


def _scatter_finish(b_refs, out_refs, send_sems, recv_sems, local_sems):
    mine, copies = _scatter_plan(b_refs, out_refs, send_sems, recv_sems, local_sems)
    for cp in copies:
        cp.wait_recv()
    for cp in copies:
        cp.wait_send()
    for cp in mine:
        cp.wait()


def _exchange_operands(kind, tensors):
    if kind == "gather":
        args = list(tensors)
        shapes = [jax.ShapeDtypeStruct((N_DEV,) + t.shape, t.dtype) for t in tensors]
        return args, shapes, (_gather_start, _gather_forward, _gather_finish)
    args = [t.reshape(N_DEV, t.shape[0] // N_DEV, t.shape[1]) for t in tensors]
    shapes = [jax.ShapeDtypeStruct(a.shape, a.dtype) for a in args]
    return args, shapes, (_scatter_start, None, _scatter_finish)


def _exchange_results(kind, tensors, res):
    if kind == "gather":
        return [r.reshape(N_DEV * t.shape[0], t.shape[1]) for r, t in zip(res, tensors)]
    return list(res)


def _exchange_sems(n):
    return [pltpu.SemaphoreType.DMA((7 * n,)), pltpu.SemaphoreType.DMA((7 * n,)), pltpu.SemaphoreType.DMA((n,))]


def _exchange(kind, tensors, name):
    n = len(tensors)
    args, shapes, phases = _exchange_operands(kind, tensors)

    def body(*refs):
        for phase in phases:
            if phase is not None:
                phase(refs[:n], refs[n:2 * n], *refs[2 * n:])

    hbm = pl.BlockSpec(memory_space=pl.ANY)
    res = pl.pallas_call(body, name=name, out_shape=shapes, in_specs=[hbm] * n, out_specs=[hbm] * n,
                         scratch_shapes=_exchange_sems(n))(*args)
    return _exchange_results(kind, tensors, res)


def _pcall(body, args, *, name, out_shape, grid, in_specs, out_specs, sem, scratch_shapes=(), carry=None):
    if carry is None:
        out = pl.pallas_call(body, name=name, out_shape=out_shape, grid=grid, in_specs=list(in_specs),
                             out_specs=out_specs, scratch_shapes=list(scratch_shapes),
                             compiler_params=_params(*sem))(*args)
        return out, None
    kind, tensors = carry
    multi = isinstance(out_shape, (tuple, list))
    shapes = list(out_shape) if multi else [out_shape]
    ospecs = list(out_specs) if multi else [out_specs]
    n_in, n_out, n_scr, n_c = len(in_specs), len(shapes), len(scratch_shapes), len(tensors)
    c_args, c_shapes, (start, forward, finish) = _exchange_operands(kind, tensors)
    n_steps = 1
    for g in grid:
        n_steps *= g
    late = (3 * n_steps) // 4

    def wrapped(*refs):
        ins, rest = refs[:n_in], refs[n_in:]
        c_in, rest = rest[:n_c], rest[n_c:]
        outs, rest = rest[:n_out], rest[n_out:]
        c_out, rest = rest[:n_c], rest[n_c:]
        scr, sems = rest[:n_scr], rest[n_scr:]
        step = pl.program_id(0)
        for a in range(1, len(grid)):
            step = step * grid[a] + pl.program_id(a)

        @pl.when(step == 0)
        def _():
            start(c_in, c_out, *sems)

        body(*ins, *outs, *scr)

        if forward is not None:
            @pl.when(step == late)
            def _():
                forward(c_in, c_out, *sems)

        @pl.when(step == n_steps - 1)
        def _():
            finish(c_in, c_out, *sems)

    hbm = pl.BlockSpec(memory_space=pl.ANY)
    res = pl.pallas_call(
        wrapped, name=name, out_shape=shapes + c_shapes, grid=grid,
        in_specs=list(in_specs) + [hbm] * n_c, out_specs=ospecs + [hbm] * n_c,
        scratch_shapes=list(scratch_shapes) + _exchange_sems(n_c),
        compiler_params=_params(*sem))(*args, *c_args)
    outs = tuple(res[:n_out]) if multi else res[0]
    return outs, _exchange_results(kind, tensors, res[n_out:])


def _sum8(parts, name, transposed=False):
    _, R, C = parts.shape

    def total(p_ref):
        g = p_ref[0].astype(F32)
        for s in range(1, N_DEV):
            g = g + p_ref[s].astype(F32)
        return g

    if transposed:
        tc = _tile(C, 256)

        def body_t(p_ref, g_ref):
            g_ref[...] = jnp.transpose(total(p_ref))

        return pl.pallas_call(
            body_t, name=name, out_shape=jax.ShapeDtypeStruct((C, R), F32),
            grid=(C // tc,),
            in_specs=[pl.BlockSpec((N_DEV, R, tc), lambda j: (0, 0, j))],
            out_specs=pl.BlockSpec((tc, R), lambda j: (j, 0)),
            compiler_params=_params("parallel"),
        )(parts)

    tr = _tile(R, 256, 16)

    def body(p_ref, g_ref):
        g_ref[...] = total(p_ref)

    return pl.pallas_call(
        body, name=name, out_shape=jax.ShapeDtypeStruct((R, C), F32),
        grid=(R // tr,),
        in_specs=[pl.BlockSpec((N_DEV, tr, C), lambda i: (0, i, 0))],
        out_specs=pl.BlockSpec((tr, C), lambda i: (i, 0)),
        compiler_params=_params("parallel"),
    )(parts)


def _adamw(g, w, m, v, name):
    R, C = g.shape
    tr = _tile(R, 256, 8)
    c1 = 1.0 - ADAM_B1 ** ADAM_STEP
    c2 = 1.0 - ADAM_B2 ** ADAM_STEP

    def body(g_ref, w_ref, m_ref, v_ref, d_ref, nm_ref, nv_ref):
        gg = g_ref[...]
        nm = ADAM_B1 * m_ref[...] + (1.0 - ADAM_B1) * gg
        nv = ADAM_B2 * v_ref[...] + (1.0 - ADAM_B2) * (gg * gg)
        m_hat = nm / c1
        v_hat = nv / c2
        nm_ref[...] = nm
        nv_ref[...] = nv
        d_ref[...] = -ADAM_LR * (m_hat / (jnp.sqrt(v_hat) + ADAM_EPS) + ADAM_WD * w_ref[...])

    row = pl.BlockSpec((tr, C), lambda i: (i, 0))
    shp = jax.ShapeDtypeStruct((R, C), F32)
    return pl.pallas_call(
        body, name=name, out_shape=(shp, shp, shp),
        grid=(R // tr,), in_specs=[row, row, row, row], out_specs=(row, row, row),
        compiler_params=_params("parallel"),
    )(g, w, m, v)


def _sum_adamw(parts_list, w, m, v, name):
    L, R, C = w.shape
    tr = _tile(R, 256, 16)
    c1 = 1.0 - ADAM_B1 ** ADAM_STEP
    c2 = 1.0 - ADAM_B2 ** ADAM_STEP

    def body(*refs):
        p_refs = refs[:L]
        w_ref, m_ref, v_ref, g_ref, d_ref, nm_ref, nv_ref = refs[L:]
        for layer in range(L):
            @pl.when(pl.program_id(0) == layer)
            def _():
                g = p_refs[layer][0].astype(F32)
                for s in range(1, N_DEV):
                    g = g + p_refs[layer][s].astype(F32)
                nm = ADAM_B1 * m_ref[0] + (1.0 - ADAM_B1) * g
                nv = ADAM_B2 * v_ref[0] + (1.0 - ADAM_B2) * (g * g)
                g_ref[0] = g
                nm_ref[0] = nm
                nv_ref[0] = nv
                d_ref[0] = -ADAM_LR * ((nm / c1) / (jnp.sqrt(nv / c2) + ADAM_EPS) + ADAM_WD * w_ref[0])

    def parts_spec(layer):
        return pl.BlockSpec((N_DEV, tr, C), lambda l, i: (0, jnp.where(l == layer, i, 0), 0))

    blk = pl.BlockSpec((1, tr, C), lambda l, i: (l, i, 0))
    shp = jax.ShapeDtypeStruct((L, R, C), F32)
    return pl.pallas_call(
        body, name=name, out_shape=(shp, shp, shp, shp),
        grid=(L, R // tr),
        in_specs=[parts_spec(layer) for layer in range(L)] + [blk, blk, blk],
        out_specs=(blk, blk, blk, blk),
        compiler_params=_params("arbitrary", "arbitrary"),
    )(*parts_list, w, m, v)


def _ffn_down(act, wo, h, tag):
    return _mm(act, wo, NN, F32, f"{tag}_down", scale=FFN_RES_SCALE, res=h, tm=512, tn=1024, tk=2816)


def _ffn_fwd(h, g, win_t, wo, tag, carry=None, loss=None):
    return _ffn_fwd_fused(h, g, win_t, wo, f"{tag}_fwd", carry=carry, loss=loss)


def _ffn_bwd(dh, h, g, win_t, wo, saved, tag, scatter=False, carry=None):
    xn, silu, dsilu, up, act = saved
    dwo = _mm(act, dh, TN, BF16, f"{tag}_dwo", scale=FFN_RES_SCALE, tm=1408, tn=1024, tk=TN_CHUNK)
    if not scatter:
        (dh_in, dg, dgate, dup), got = _ffn_bwd_fused(dh, h, g, win_t, wo, silu, dsilu, up, f"{tag}_bwd", carry=carry)
        dwin_t, _ = _dw_rows([dgate, dup], xn, f"{tag}_dwin")
        return dh_in, dg, dwin_t, dwo, got
    dgate, dup = _ffn_dact(dh, wo, silu, dsilu, up, f"{tag}_dact")
    dwin_t, got_wo = _dw_rows([dgate, dup], xn, f"{tag}_dwin", carry=("scatter", [dwo]))
    (dh_in, dg), got_win = _dx_norm_bwd([(dgate, win_t, NN, 2, 0), (dup, win_t, NN, 2, 1)], h, g, dh, f"{tag}_dx",
                                        carry=("scatter", [dwin_t]))
    return dh_in, dg, got_win[0], got_wo[0]


def _proj(a, w, dims, out_dtype, name, res=None):
    return _mm(a, w, dims, out_dtype, name, res=res, tm=1024, tn=1024, tk=1024)


def _proj_dw(x, dy, name):
    return _mm(x, dy, TN, BF16, name, tm=1024, tn=1024, tk=TN_CHUNK)


def kernel(x, ffn1_norm, ffn1_w_in, ffn1_w_out, mix_norm, ffn2_norm, ffn2_w_in, ffn2_w_out, sb_w_qkv, sb_w_o, kv_norm, kv_w, swa_w_q, swa_sinks, swa_w_o, final_norm, loss_target, m_ffn1_norm, m_ffn1_w_in, m_ffn1_w_out, m_mix_norm, m_ffn2_norm, m_ffn2_w_in, m_ffn2_w_out, m_sb_w_qkv, m_sb_w_o, m_kv_norm, m_kv_w, m_swa_w_q, m_swa_sinks, m_swa_w_o, m_final_norm, v_ffn1_norm, v_ffn1_w_in, v_ffn1_w_out, v_mix_norm, v_ffn2_norm, v_ffn2_w_in, v_ffn2_w_out, v_sb_w_qkv, v_sb_w_o, v_kv_norm, v_kv_w, v_swa_w_q, v_swa_sinks, v_swa_w_o, v_final_norm):
    S, D = x.shape[1], x.shape[2]
    L = ffn1_w_in.shape[0]
    KV = kv_w.shape[1]
    assert L == 2 and swa_sinks.shape == (1, 2 * SWA_Q_GROUPS * KV // (2 * LANES))

    def bf(w):
        return w.astype(BF16)

    def bft(w):
        return jnp.transpose(w).astype(BF16)

    cos_t, sin_t = _rope_tables(S)
    h0 = x.reshape(S, D)
    tgt = loss_target.reshape(S, D)

    win1a_t, = _exchange("gather", [bft(ffn1_w_in[0])], "gather_first_weight")
    sv_a1, (wo1a, wqkv_t, w_sbo) = _ffn_up(
        h0, ffn1_norm[0], win1a_t, "ffn1a_up",
        carry=("gather", [bf(ffn1_w_out[0]), bft(sb_w_qkv[0]), bf(sb_w_o[0])]))
    h1 = _ffn_down(sv_a1[-1], wo1a, h0, "ffn1a")
    hn_a, qkv, kv_t = _norm_proj(h1, mix_norm[0], wqkv_t, NT, "sb_qkv", tail_t=2 * D)
    o_sb, (win2a_t, wo2a, w_kv) = _sb_fwd(qkv, kv_t, "sb_attn", carry=("gather", [
        bft(ffn2_w_in[0]), bf(ffn2_w_out[0]), bf(kv_w)]))
    h2 = _proj(o_sb, w_sbo, NN, F32, "sb_out", res=h1)
    h3, sv_a2, (win1b_t, wo1b, w_q, w_swo) = _ffn_fwd(h2, ffn2_norm[0], win2a_t, wo2a, "ffn2a", carry=("gather", [
        bft(ffn1_w_in[1]), bf(ffn1_w_out[1]), bf(swa_w_q[0]), bf(swa_w_o[0])]))
    kvn, kv_rot, kv_rot_t = _norm_proj(h3, kv_norm, w_kv, NN, "kv_proj", rope=(cos_t, sin_t, KV // (2 * LANES)),
                                       tail_t=KV)
    h4, sv_b1, (win2b_t, wo2b) = _ffn_fwd(h3, ffn1_norm[1], win1b_t, wo1b, "ffn1b", carry=("gather", [
        bft(ffn2_w_in[1]), bf(ffn2_w_out[1])]))
    hn_b, q_rot = _norm_proj(h4, mix_norm[1], w_q, NN, "swa_q", rope=(cos_t, sin_t, D // LANES))
    o_sw = _swa_fwd(q_rot, kv_rot, kv_rot_t, swa_sinks, "swa_attn")
    h5 = _proj(o_sw, w_swo, NN, F32, "swa_out", res=h4)
    (dh6, dg_final, sq_err), sv_b2, _ = _ffn_fwd(h5, ffn2_norm[1], win2b_t, wo2b, "ffn2b", loss=(final_norm, tgt))
    loss_local = 0.5 * jnp.sum(sq_err) / D

    dh5, dg_f2b, dwin2b_t, dwo2b, _ = _ffn_bwd(dh6, h5, ffn2_norm[1], win2b_t, wo2b, sv_b2, "ffn2b")
    do_sw = _proj(dh5, w_swo, NT, BF16, "swa_out_dx")
    dw_swo = _proj_dw(o_sw, dh5, "swa_out_dw")
    (dq, dk_sw, dv_sw, dsink), (p_win2b, p_swo) = _swa_bwd(
        q_rot, kv_rot, kv_rot_t, swa_sinks, o_sw, do_sw, cos_t, sin_t, "swa_attn_bwd",
        carry=("scatter", [dwin2b_t, dw_swo]))
    dw_q = _proj_dw(hn_b, dq, "swa_q_dw")
    (dh4, dg_mix_b), _ = _dx_norm_bwd([(dq, w_q, NT, 1, 0)], h4, mix_norm[1], dh5, "swa_q_dx", tm=512)
    dh3, dg_f1b, dwin1b_t, dwo1b, _ = _ffn_bwd(dh4, h3, ffn1_norm[1], win1b_t, wo1b, sv_b1, "ffn1b")
    dkv = _rotary_bwd([dk_sw, dv_sw], cos_t, sin_t, KV // (2 * LANES), "kv_rope_bwd")
    dw_kv = _proj_dw(kvn, dkv, "kv_proj_dw")
    (dh3, dg_kv), _ = _dx_norm_bwd([(dkv, w_kv, NT, 1, 0)], h3, kv_norm, dh3, "kv_proj_dx", tm=512)
    dh2, dg_f2a, dwin2a_t, dwo2a, (p_win1b, p_kv) = _ffn_bwd(
        dh3, h2, ffn2_norm[0], win2a_t, wo2a, sv_a2, "ffn2a", carry=("scatter", [dwin1b_t, dw_kv]))
    do_sb = _proj(dh2, w_sbo, NT, BF16, "sb_out_dx")
    dw_sbo = _proj_dw(o_sb, dh2, "sb_out_dw")
    (dq_sb, dk_sb, dv_sb), (p_win2a, p_wo2a, p_sbo, p_wo1b, p_q, p_wo2b) = _sb_bwd(
        qkv, kv_t, o_sb, do_sb, "sb_attn_bwd", carry=("scatter", [dwin2a_t, dwo2a, dw_sbo, dwo1b, dw_q, dwo2b]))
    dqkv = [dq_sb, dk_sb, dv_sb]
    dwqkv_t, _ = _dw_rows(dqkv, hn_a, "sb_qkv_dw", tk=TN_CHUNK // 2)
    (dh1, dg_mix_a), (p_qkv,) = _dx_norm_bwd([(dy, wqkv_t, NN, 3, n) for n, dy in enumerate(dqkv)], h1, mix_norm[0],
                                             dh2, "sb_qkv_dx", carry=("scatter", [dwqkv_t]), tm=512)
    dx, dg_f1a, p_win1a, p_wo1a = _ffn_bwd(dh1, h0, ffn1_norm[0], win1a_t, wo1a, sv_a1, "ffn1a", scatter=True)

    def from_t(parts, tag):
        return _sum8(parts, f"sum_{tag}", transposed=True)

    grads = {
        "ffn1_w_in": jnp.stack([from_t(p_win1a, "win1a"), from_t(p_win1b, "win1b")]),
        "ffn2_w_in": jnp.stack([from_t(p_win2a, "win2a"), from_t(p_win2b, "win2b")]),
        "sb_w_qkv": from_t(p_qkv, "qkv")[None],
    }
    row_parts = {"ffn1_w_out": [p_wo1a, p_wo1b], "ffn2_w_out": [p_wo2a, p_wo2b], "sb_w_o": [p_sbo],
                 "kv_w": [p_kv], "swa_w_q": [p_q], "swa_w_o": [p_swo]}

    small_w = [ffn1_norm, mix_norm, ffn2_norm, kv_norm, final_norm, swa_sinks]
    small_m = [m_ffn1_norm, m_mix_norm, m_ffn2_norm, m_kv_norm, m_final_norm, m_swa_sinks]
    small_v = [v_ffn1_norm, v_mix_norm, v_ffn2_norm, v_kv_norm, v_final_norm, v_swa_sinks]
    SMALL_ROWS = 16

    def pack_small(ts):
        rows_ = [t.reshape(-1, D) for t in ts[:-1]]
        sink_row = jnp.pad(ts[-1].reshape(1, -1), ((0, 0), (0, D - ts[-1].size)))
        flat = jnp.concatenate(rows_ + [sink_row], axis=0)
        return jnp.pad(flat, ((0, SMALL_ROWS - flat.shape[0]), (0, 0)))

    def unpack_small(flat):
        out, r = [], 0
        for t in small_w[:-1]:
            n = t.size // D
            out.append(flat[r:r + n].reshape(t.shape))
            r += n
        out.append(flat[r, :swa_sinks.size].reshape(swa_sinks.shape))
        return out

    def gain(parts8):
        return jnp.sum(parts8, axis=0, keepdims=True)

    g_small_local = pack_small([
        jnp.concatenate([gain(dg_f1a), gain(dg_f1b)], axis=0),
        jnp.concatenate([gain(dg_mix_a), gain(dg_mix_b)], axis=0),
        jnp.concatenate([gain(dg_f2a), gain(dg_f2b)], axis=0),
        gain(dg_kv), gain(dg_final), jnp.sum(dsink, axis=-1).reshape(1, -1)])
    loss_row = sum(t.size for t in small_w[:-1]) // D + 1
    assert loss_row < SMALL_ROWS
    g_small_local = g_small_local.at[loss_row, 0].set(loss_local)
    small_parts = _exchange("gather", [g_small_local], "gather_small_grads")[0]
    g_small = _sum8(small_parts.reshape(N_DEV, SMALL_ROWS, D), "sum_small")
    loss = g_small[loss_row, 0]
    d_small, nm_small, nv_small = _adamw(g_small, pack_small(small_w), pack_small(small_m), pack_small(small_v), "adamw_small")
    small_names = ["ffn1_norm", "mix_norm", "ffn2_norm", "kv_norm", "final_norm", "swa_sinks"]
    result = {"grad": dict(zip(small_names, unpack_small(g_small))),
              "delta": dict(zip(small_names, unpack_small(d_small))),
              "new_m": dict(zip(small_names, unpack_small(nm_small))),
              "new_v": dict(zip(small_names, unpack_small(nv_small)))}

    big = {"ffn1_w_in": (ffn1_w_in, m_ffn1_w_in, v_ffn1_w_in), "ffn1_w_out": (ffn1_w_out, m_ffn1_w_out, v_ffn1_w_out),
           "ffn2_w_in": (ffn2_w_in, m_ffn2_w_in, v_ffn2_w_in), "ffn2_w_out": (ffn2_w_out, m_ffn2_w_out, v_ffn2_w_out),
           "sb_w_qkv": (sb_w_qkv, m_sb_w_qkv, v_sb_w_qkv), "sb_w_o": (sb_w_o, m_sb_w_o, v_sb_w_o),
           "kv_w": (kv_w, m_kv_w, v_kv_w), "swa_w_q": (swa_w_q, m_swa_w_q, v_swa_w_q),
           "swa_w_o": (swa_w_o, m_swa_w_o, v_swa_w_o)}
    for nm, (w, m, v) in big.items():
        if nm in row_parts:
            three_d = lambda t: t.reshape((len(row_parts[nm]),) + t.shape[-2:])
            g, d, new_m, new_v = _sum_adamw(row_parts[nm], three_d(w), three_d(m), three_d(v), f"adamw_{nm}")
        else:
            g = grads[nm]
            two_d = lambda t: t.reshape(-1, t.shape[-1])
            d, new_m, new_v = _adamw(two_d(g), two_d(w), two_d(m), two_d(v), f"adamw_{nm}")
        result["grad"][nm] = g.reshape(w.shape)
        result["delta"][nm] = d.reshape(w.shape)
        result["new_m"][nm] = new_m.reshape(w.shape)
        result["new_v"][nm] = new_v.reshape(w.shape)

    order = ["ffn1_norm", "ffn1_w_in", "ffn1_w_out", "mix_norm", "ffn2_norm", "ffn2_w_in", "ffn2_w_out",
             "sb_w_qkv", "sb_w_o", "kv_norm", "kv_w", "swa_w_q", "swa_sinks", "swa_w_o", "final_norm"]
    outs = [result[kind][nm] for kind in ("grad", "delta", "new_m", "new_v") for nm in order]
    return (loss, dx.reshape(x.shape), *outs)
```

```python
import jax
import jax.numpy as jnp
from jax import lax
from jax.experimental import pallas as pl
from jax.experimental.pallas import tpu as pltpu

F32 = jnp.float32
BF16 = jnp.bfloat16

N_DEV = 8
HEAD_DIM = 64
LANES = 128
BLK = 128
RMS_EPS = 1e-6
FFN_RES_SCALE = 0.5
ROPE_THETA = 10000.0
ATTN_SCALE = HEAD_DIM ** -0.5
SB_LOG_FLOOR = -88.0
NEG_BIG = -1e30
VMEM_LIMIT_V7X = 56 * 1024 * 1024

ADAM_LR = 0.001
ADAM_B1 = 0.9
ADAM_B2 = 0.999
ADAM_EPS = 1e-08
ADAM_WD = 0.01
ADAM_STEP = 10

NN = ((1,), (0,))
NT = ((1,), (1,))
TN = ((0,), (0,))
TN_CHUNK = 2048
MESH = pl.DeviceIdType.MESH


def _dot(a, b, dims):
    return lax.dot_general(a, b, (dims, ((), ())), preferred_element_type=F32)


def _tile(n, pref, mult=LANES):
    if n <= pref:
        return n
    t = (pref // mult) * mult
    while t >= mult:
        if n % t == 0:
            return t
        t -= mult
    return n


def _params(*sem):
    return pltpu.CompilerParams(dimension_semantics=sem, vmem_limit_bytes=VMEM_LIMIT_V7X)


def _mm(a, b, dims, out_dtype, name, scale=1.0, res=None, tm=512, tn=512, tk=512):
    if dims == NN:
        (M, K), (_, N) = a.shape, b.shape
    elif dims == NT:
        (M, K), (N, _) = a.shape, b.shape
    else:
        (K, M), (_, N) = a.shape, b.shape
    tm, tn, tk = _tile(M, tm), _tile(N, tn), _tile(K, tk)
    nk = K // tk
    if dims == TN:
        a_spec = pl.BlockSpec((tk, tm), lambda i, j, k: (k, i))
    else:
        a_spec = pl.BlockSpec((tm, tk), lambda i, j, k: (i, k))
    if dims == NT:
        b_spec = pl.BlockSpec((tn, tk), lambda i, j, k: (j, k))
    else:
        b_spec = pl.BlockSpec((tk, tn), lambda i, j, k: (k, j))
    o_spec = pl.BlockSpec((tm, tn), lambda i, j, k: (i, j))
    has_res = res is not None

    def body(*refs):
        a_ref, b_ref = refs[0], refs[1]
        r_ref = refs[2] if has_res else None
        o_ref = refs[3] if has_res else refs[2]

        def finish(acc):
            r = acc * scale if scale != 1.0 else acc
            if has_res:
                r = r + r_ref[...]
            o_ref[...] = r.astype(out_dtype)

        p = _dot(a_ref[...].astype(BF16), b_ref[...].astype(BF16), dims)
        if nk == 1:
            finish(p)
        else:
            acc_ref = refs[-1]
            k = pl.program_id(2)

            @pl.when(k == 0)
            def _():
                acc_ref[...] = p

            @pl.when(k > 0)
            def _():
                acc_ref[...] += p

            @pl.when(k == nk - 1)
            def _():
                finish(acc_ref[...])

    in_specs = [a_spec, b_spec] + ([o_spec] if has_res else [])
    args = (a, b) + ((res,) if has_res else ())
    return pl.pallas_call(
        body, name=name,
        out_shape=jax.ShapeDtypeStruct((M, N), out_dtype),
        grid=(M // tm, N // tn, nk),
        in_specs=in_specs, out_specs=o_spec,
        scratch_shapes=[pltpu.VMEM((tm, tn), F32)] if nk > 1 else [],
        compiler_params=_params("parallel", "parallel", "arbitrary"),
    )(*args)


def _rows8(x):
    r, d = x.shape
    return jnp.sum(x.reshape(r // 8, 8, d), axis=0)


def _norm_proj(h, g, w, dims, name, rope=None, tail_t=0):
    S, D = h.shape
    N = w.shape[1] if dims == NN else w.shape[0]
    tm = _tile(S, 512, 16)

    def body(h_ref, g_ref, w_ref, *rest):
        xn_ref, y_ref = rest[-3:-1] if tail_t else rest[-2:]
        x = h_ref[...]
        r = lax.rsqrt(jnp.mean(x * x, axis=-1, keepdims=True) + RMS_EPS)
        xn = ((x * r) * g_ref[...]).astype(BF16)
        xn_ref[...] = xn
        y = _dot(xn, w_ref[...], dims)
        if rope is not None:
            cs, sn = rest[0][...], rest[1][...]
            groups = [y[:, gidx * LANES:(gidx + 1) * LANES] for gidx in range(N // LANES)]
            y = jnp.concatenate([v * cs + _swap_halves(v) * sn if gidx < rope[2] else v
                                 for gidx, v in enumerate(groups)], axis=1)
        y_ref[...] = y.astype(BF16)
        if tail_t:
            rest[-1][...] = jnp.transpose(y[:, N - tail_t:]).astype(BF16)

    row = pl.BlockSpec((tm, D), lambda i: (i, 0))
    tab = pl.BlockSpec((tm, LANES), lambda i: (i, 0))
    in_specs = [row, pl.BlockSpec((1, D), lambda i: (0, 0)), pl.BlockSpec(w.shape, lambda i: (0, 0))]
    args = (h, g.reshape(1, D), w)
    if rope is not None:
        in_specs += [tab, tab]
        args += (rope[0], rope[1])
    out_shape = [jax.ShapeDtypeStruct((S, D), BF16), jax.ShapeDtypeStruct((S, N), BF16)]
    out_specs = [row, pl.BlockSpec((tm, N), lambda i: (i, 0))]
    if tail_t:
        out_shape.append(jax.ShapeDtypeStruct((tail_t, S), BF16))
        out_specs.append(pl.BlockSpec((tail_t, tm), lambda i: (0, i)))
    return pl.pallas_call(
        body, name=name, out_shape=out_shape, grid=(S // tm,),
        in_specs=in_specs, out_specs=out_specs,
        compiler_params=_params("parallel"),
    )(*args)


def _ffn_up(h, g, win_t, name, carry=None):
    S, D = h.shape
    F = win_t.shape[0] // 2
    tm = _tile(S, 256, 16)

    def body(h_ref, g_ref, win_hbm, xn_ref, silu_ref, dsilu_ref, up_ref, act_ref, win_v, sems):
        _load_resident([(win_hbm, win_v)], sems)
        x = h_ref[...]
        r = lax.rsqrt(jnp.mean(x * x, axis=-1, keepdims=True) + RMS_EPS)
        xn = ((x * r) * g_ref[...]).astype(BF16)
        xn_ref[...] = xn
        gate = _dot(xn, win_v[:F, :], NT)
        up = _dot(xn, win_v[F:, :], NT)
        sig = 1.0 / (1.0 + jnp.exp(-gate))
        silu = gate * sig
        up_ref[...] = up.astype(BF16)
        silu_ref[...] = silu.astype(BF16)
        dsilu_ref[...] = (sig + silu * (1.0 - sig)).astype(BF16)
        act_ref[...] = (silu * up).astype(BF16)

    row = pl.BlockSpec((tm, D), lambda i: (i, 0))
    wide = pl.BlockSpec((tm, F), lambda i: (i, 0))
    hid = jax.ShapeDtypeStruct((S, F), BF16)
    return _pcall(
        body, (h, g.reshape(1, D), win_t), name=name,
        out_shape=(jax.ShapeDtypeStruct((S, D), BF16), hid, hid, hid, hid),
        grid=(S // tm,),
        in_specs=[row, pl.BlockSpec((1, D), lambda i: (0, 0)), pl.BlockSpec(memory_space=pl.ANY)],
        out_specs=(row, wide, wide, wide, wide),
        scratch_shapes=[pltpu.VMEM(win_t.shape, BF16), pltpu.SemaphoreType.DMA((1,))],
        sem=("arbitrary",), carry=carry)


def _ffn_dact(dh, wo, silu, dsilu, up, name):
    S, D = dh.shape
    F = wo.shape[0]
    tm = _tile(S, 256, 16)

    def body(dh_ref, wo_hbm, s_ref, ds_ref, u_ref, dg_ref, du_ref, wo_v, sems):
        _load_resident([(wo_hbm, wo_v)], sems)
        d = _dot(dh_ref[...].astype(BF16), wo_v[...], NT) * FFN_RES_SCALE
        du_ref[...] = (d * s_ref[...].astype(F32)).astype(BF16)
        dg_ref[...] = (d * u_ref[...].astype(F32) * ds_ref[...].astype(F32)).astype(BF16)

    wide = pl.BlockSpec((tm, F), lambda i: (i, 0))
    hid = jax.ShapeDtypeStruct((S, F), BF16)
    return pl.pallas_call(
        body, name=name, out_shape=(hid, hid),
        grid=(S // tm,),
        in_specs=[pl.BlockSpec((tm, D), lambda i: (i, 0)), pl.BlockSpec(memory_space=pl.ANY), wide, wide, wide],
        out_specs=(wide, wide),
        scratch_shapes=[pltpu.VMEM(wo.shape, BF16), pltpu.SemaphoreType.DMA((1,))],
        compiler_params=_params("arbitrary"),
    )(dh, wo, silu, dsilu, up)


def _dw_rows(srcs, x, name, carry=None, tk=TN_CHUNK):
    n = len(srcs)
    S, F = srcs[0].shape
    D = x.shape[1]
    tr, tk = _tile(F, 1408), _tile(S, tk, 16)
    nf, nk = F // tr, S // tk

    def body(*refs):
        src_refs, (x_ref, o_ref, acc_ref) = refs[:n], refs[n:]
        r, k = pl.program_id(0), pl.program_id(1)
        for s in range(n):
            @pl.when(r // nf == s)
            def _():
                p = _dot(src_refs[s][...].astype(BF16), x_ref[...], TN)

                @pl.when(k == 0)
                def _():
                    acc_ref[...] = p

                @pl.when(k > 0)
                def _():
                    acc_ref[...] += p

        @pl.when(k == nk - 1)
        def _():
            o_ref[...] = acc_ref[...].astype(BF16)

    def src_spec(s):
        return pl.BlockSpec((tk, tr), lambda r, k: (jnp.where(r // nf == s, k, 0), jnp.clip(r - s * nf, 0, nf - 1)))

    return _pcall(
        body, (*srcs, x), name=name, out_shape=jax.ShapeDtypeStruct((n * F, D), BF16),
        grid=(n * nf, nk),
        in_specs=[src_spec(s) for s in range(n)] + [pl.BlockSpec((tk, D), lambda r, k: (k, 0))],
        out_specs=pl.BlockSpec((tr, D), lambda r, k: (r, 0)),
        scratch_shapes=[pltpu.VMEM((tr, D), F32)],
        sem=("arbitrary", "arbitrary"), carry=carry)


def _dx_norm_bwd(terms, h, g, res, name, carry=None, tm=256):
    S, D = h.shape
    tm = _tile(S, tm, 16)
    n = len(terms)

    def body(*refs):
        dy_refs, w_refs = refs[:n], refs[n:2 * n]
        h_ref, g_ref, r_ref, dh_ref, dg_ref = refs[2 * n:]
        d = _dot(dy_refs[0][...].astype(BF16), w_refs[0][...], terms[0][2])
        for t in range(1, n):
            d = d + _dot(dy_refs[t][...].astype(BF16), w_refs[t][...], terms[t][2])
        x = h_ref[...]
        r = lax.rsqrt(jnp.mean(x * x, axis=-1, keepdims=True) + RMS_EPS)
        xhat = x * r
        dxh = d * g_ref[...]
        c = jnp.mean(dxh * xhat, axis=-1, keepdims=True)
        dh_ref[...] = r * (dxh - xhat * c) + r_ref[...]
        part = _rows8(d * xhat)

        @pl.when(pl.program_id(0) == 0)
        def _():
            dg_ref[...] = part

        @pl.when(pl.program_id(0) > 0)
        def _():
            dg_ref[...] += part

    def w_spec(w, nblk, blk):
        return pl.BlockSpec((w.shape[0] // nblk, w.shape[1]), lambda i: (blk, 0))

    row = pl.BlockSpec((tm, D), lambda i: (i, 0))
    in_specs = [pl.BlockSpec((tm, t[0].shape[1]), lambda i: (i, 0)) for t in terms]
    in_specs += [w_spec(t[1], t[3], t[4]) for t in terms]
    in_specs += [row, pl.BlockSpec((1, D), lambda i: (0, 0)), row]
    return _pcall(
        body, (*[t[0] for t in terms], *[t[1] for t in terms], h, g.reshape(1, D), res), name=name,
        out_shape=(jax.ShapeDtypeStruct((S, D), F32), jax.ShapeDtypeStruct((8, D), F32)),
        grid=(S // tm,),
        in_specs=in_specs,
        out_specs=(row, pl.BlockSpec((8, D), lambda i: (0, 0))),
        sem=("arbitrary",), carry=carry)


def _load_resident(pairs, sems):
    @pl.when(pl.program_id(0) == 0)
    def _():
        copies = [pltpu.make_async_copy(src, dst, sems.at[n]) for n, (src, dst) in enumerate(pairs)]
        for cp in copies:
            cp.start()
        for cp in copies:
            cp.wait()


def _loss_tail(y_in, g, tgt):
    D = y_in.shape[-1]
    r = lax.rsqrt(jnp.mean(y_in * y_in, axis=-1, keepdims=True) + RMS_EPS)
    xhat = y_in * r
    err = xhat * g - tgt
    d = err * (1.0 / D)
    dxh = d * g
    c = jnp.mean(dxh * xhat, axis=-1, keepdims=True)
    return r * (dxh - xhat * c), _rows8(d * xhat), _rows8(err * err)


def _ffn_fwd_fused(h, g, win_t, wo, name, carry=None, loss=None):
    S, D = h.shape
    F = wo.shape[0]
    tm = _tile(S, 256, 16)
    n_head = 3 if loss is not None else 1

    def body(h_ref, g_ref, win_hbm, wo_hbm, *rest):
        lead, (xn_ref, silu_ref, dsilu_ref, up_ref, act_ref, win_v, wo_v, sems) = rest[:-8], rest[-8:]
        _load_resident([(win_hbm, win_v), (wo_hbm, wo_v)], sems)
        x = h_ref[...]
        r = lax.rsqrt(jnp.mean(x * x, axis=-1, keepdims=True) + RMS_EPS)
        xn = ((x * r) * g_ref[...]).astype(BF16)
        xn_ref[...] = xn
        gate = _dot(xn, win_v[:F, :], NT)
        up = _dot(xn, win_v[F:, :], NT)
        sig = 1.0 / (1.0 + jnp.exp(-gate))
        silu = gate * sig
        act = (silu * up).astype(BF16)
        up_ref[...] = up.astype(BF16)
        silu_ref[...] = silu.astype(BF16)
        dsilu_ref[...] = (sig + silu * (1.0 - sig)).astype(BF16)
        act_ref[...] = act
        out = x + FFN_RES_SCALE * _dot(act, wo_v[...], NN)
        if loss is None:
            lead[0][...] = out
        else:
            gf_ref, t_ref, dy_ref, dgf_ref, sq_ref = lead
            dy, dgf, sq = _loss_tail(out, gf_ref[...], t_ref[...])
            dy_ref[...] = dy

            @pl.when(pl.program_id(0) == 0)
            def _():
                dgf_ref[...] = dgf
                sq_ref[...] = sq

            @pl.when(pl.program_id(0) > 0)
            def _():
                dgf_ref[...] += dgf
                sq_ref[...] += sq

    row = pl.BlockSpec((tm, D), lambda i: (i, 0))
    vec = pl.BlockSpec((1, D), lambda i: (0, 0))
    acc = pl.BlockSpec((8, D), lambda i: (0, 0))
    wide = pl.BlockSpec((tm, F), lambda i: (i, 0))
    hbm = pl.BlockSpec(memory_space=pl.ANY)
    hid = jax.ShapeDtypeStruct((S, F), BF16)
    full = jax.ShapeDtypeStruct((S, D), F32)
    part = jax.ShapeDtypeStruct((8, D), F32)
    args, in_specs = (h, g.reshape(1, D), win_t, wo), [row, vec, hbm, hbm]
    lead_shapes, lead_specs = (full,), (row,)
    if loss is not None:
        args, in_specs = args + (loss[0].reshape(1, D), loss[1]), in_specs + [vec, row]
        lead_shapes, lead_specs = (full, part, part), (row, acc, acc)
    res, got = _pcall(
        body, args, name=name,
        out_shape=lead_shapes + (jax.ShapeDtypeStruct((S, D), BF16), hid, hid, hid, hid),
        grid=(S // tm,),
        in_specs=in_specs,
        out_specs=lead_specs + (row, wide, wide, wide, wide),
        scratch_shapes=[pltpu.VMEM(win_t.shape, BF16), pltpu.VMEM(wo.shape, BF16), pltpu.SemaphoreType.DMA((2,))],
        sem=("arbitrary",), carry=carry)
    first = res[0] if loss is None else tuple(res[:3])
    return first, tuple(res[n_head:]), got


def _ffn_bwd_fused(dh, h, g, win_t, wo, silu, dsilu, up, name, carry=None):
    S, D = h.shape
    F = wo.shape[0]
    tm = _tile(S, 256, 16)

    def body(dh_ref, h_ref, g_ref, s_ref, ds_ref, u_ref, win_hbm, wo_hbm,
             dhin_ref, dgain_ref, dgate_ref, dup_ref, win_v, wo_v, sems):
        _load_resident([(win_hbm, win_v), (wo_hbm, wo_v)], sems)
        dhv = dh_ref[...]
        d = _dot(dhv.astype(BF16), wo_v[...], NT) * FFN_RES_SCALE
        dup = (d * s_ref[...].astype(F32)).astype(BF16)
        dgate = (d * u_ref[...].astype(F32) * ds_ref[...].astype(F32)).astype(BF16)
        dup_ref[...] = dup
        dgate_ref[...] = dgate
        dxn = _dot(dgate, win_v[:F, :], NN) + _dot(dup, win_v[F:, :], NN)
        x = h_ref[...]
        r = lax.rsqrt(jnp.mean(x * x, axis=-1, keepdims=True) + RMS_EPS)
        xhat = x * r
        dxh = dxn * g_ref[...]
        c = jnp.mean(dxh * xhat, axis=-1, keepdims=True)
        dhin_ref[...] = r * (dxh - xhat * c) + dhv
        part = _rows8(dxn * xhat)

        @pl.when(pl.program_id(0) == 0)
        def _():
            dgain_ref[...] = part

        @pl.when(pl.program_id(0) > 0)
        def _():
            dgain_ref[...] += part

    row = pl.BlockSpec((tm, D), lambda i: (i, 0))
    wide = pl.BlockSpec((tm, F), lambda i: (i, 0))
    hbm = pl.BlockSpec(memory_space=pl.ANY)
    hid = jax.ShapeDtypeStruct((S, F), BF16)
    return _pcall(
        body, (dh, h, g.reshape(1, D), silu, dsilu, up, win_t, wo), name=name,
        out_shape=(jax.ShapeDtypeStruct((S, D), F32), jax.ShapeDtypeStruct((8, D), F32), hid, hid),
        grid=(S // tm,),
        in_specs=[row, row, pl.BlockSpec((1, D), lambda i: (0, 0)), wide, wide, wide, hbm, hbm],
        out_specs=(row, pl.BlockSpec((8, D), lambda i: (0, 0)), wide, wide),
        scratch_shapes=[pltpu.VMEM(win_t.shape, BF16), pltpu.VMEM(wo.shape, BF16), pltpu.SemaphoreType.DMA((2,))],
        sem=("arbitrary",), carry=carry)


def _rope_tables(S):
    half = HEAD_DIM // 2
    inv_freq = ROPE_THETA ** (-jnp.arange(half, dtype=F32) / half)
    ang = jnp.arange(S).astype(F32)[:, None] * inv_freq[None, :]
    cos, sin = jnp.cos(ang), jnp.sin(ang)
    cos_t = jnp.tile(cos, (1, LANES // half))
    sin_t = jnp.tile(jnp.concatenate([-sin, sin], axis=1), (1, LANES // HEAD_DIM))
    return cos_t, sin_t


def _swap_halves(x):
    lane = lax.broadcasted_iota(jnp.int32, x.shape, 1)
    first = (lane % HEAD_DIM) < (HEAD_DIM // 2)
    return jnp.where(first, pltpu.roll(x, LANES - HEAD_DIM // 2, 1), pltpu.roll(x, HEAD_DIM // 2, 1))


def _rotary_bwd(dys, cos_t, sin_t, n_rot, name):
    S = dys[0].shape[0]
    widths = [dy.shape[1] for dy in dys]
    ts = _tile(S, 512, 16)

    def body(*refs):
        x_refs, (c_ref, s_ref, o_ref) = refs[:len(dys)], refs[len(dys):]
        cs, sn = c_ref[...], s_ref[...]
        gidx = 0
        for x_ref, width in zip(x_refs, widths):
            for g in range(width // LANES):
                v = x_ref[:, g * LANES:(g + 1) * LANES].astype(F32)
                if gidx < n_rot:
                    v = v * cs + _swap_halves(v * sn)
                o_ref[:, gidx * LANES:(gidx + 1) * LANES] = v.astype(BF16)
                gidx += 1

    tab = pl.BlockSpec((ts, LANES), lambda i: (i, 0))
    return pl.pallas_call(
        body, name=name, out_shape=jax.ShapeDtypeStruct((S, sum(widths)), BF16),
        grid=(S // ts,),
        in_specs=[pl.BlockSpec((ts, width), lambda i: (i, 0)) for width in widths] + [tab, tab],
        out_specs=pl.BlockSpec((ts, sum(widths)), lambda i: (i, 0)),
        compiler_params=_params("parallel"),
    )(*dys, cos_t, sin_t)


def _head_masks():
    lane = lax.broadcasted_iota(jnp.int32, (BLK, LANES), 1)
    return lane < HEAD_DIM


def _split_bf16(x):
    hi = x.astype(BF16)
    lo = (x - hi.astype(F32)).astype(BF16)
    return hi, lo


def _sb_scores(qh, ks, carry, diag, tri_excl, strict):
    n_heads = len(qh)
    zs = [_dot(ks[n], qh[n], NT) for n in range(n_heads)]
    a_l, b_l, split_l = [], [], []
    for z in zs:
        a = jnp.minimum(z, 0.0) - jnp.log(1.0 + jnp.exp(-jnp.abs(z)))
        b = a - z
        if diag:
            b = jnp.where(strict, b, 0.0)
        a_l.append(a)
        b_l.append(b)
        split_l.append(_split_bf16(b))
    sufs = [_dot(tri_excl, hi, NN) + _dot(tri_excl, lo, NN) for hi, lo in split_l]
    w_l = []
    for n in range(n_heads):
        w = jnp.exp(a_l[n] + sufs[n] + carry[n])
        if diag:
            w = jnp.where(strict, w, 0.0)
        w_l.append(w)
    return a_l, b_l, w_l


SB_FWD_PAIRS = 4
SB_FWD_QBLOCKS = 4
SB_BWD_PAIRS = 2
SB_BWD_QBLOCKS = 4


def _any_alive(carries):
    top = carries[0]
    for c in carries[1:]:
        top = jnp.maximum(top, c)
    return (jnp.max(top) > SB_LOG_FLOOR).astype(jnp.int32)


def _sb_masks():
    row = lax.broadcasted_iota(jnp.int32, (BLK, BLK), 0)
    col = lax.broadcasted_iota(jnp.int32, (BLK, BLK), 1)
    tri_excl = jnp.where(col > row, 1.0, 0.0).astype(BF16)
    tri_incl = jnp.where(col >= row, 1.0, 0.0).astype(BF16)
    return row < HEAD_DIM, row < col, tri_excl, tri_incl


def _sb_fwd(qkv, kv_t, name, carry=None):
    S, D3 = qkv.shape
    D = D3 // 3
    npair, nb = D // LANES, S // BLK
    P = min(SB_FWD_PAIRS, npair)
    ngroup = npair // P
    W = P * LANES

    QB = SB_FWD_QBLOCKS if nb % SB_FWD_QBLOCKS == 0 else 1
    nch = QB * 2 * P

    def body(q_ref, k_ref, vt_ref, o_ref):
        i_first = pl.program_id(1) * QB
        m0 = _head_masks()
        top, strict, tri_excl, _ = _sb_masks()
        zq = jnp.zeros((BLK, LANES), BF16)
        lanes = [slice(p * LANES, (p + 1) * LANES) for p in range(P)]
        qh = []
        for qb in range(QB):
            for sl in lanes:
                q2 = q_ref[qb * BLK:(qb + 1) * BLK, sl] * ATTN_SCALE
                qh += [jnp.where(m0, q2, zq), jnp.where(m0, zq, q2)]

        def block(qbs, js, carry, acc, diag):
            offs = [pl.multiple_of(j * BLK, BLK) for j in js]
            ks, vth, qs = [], [], []
            for n_qb, qb in enumerate(qbs):
                qs += qh[qb * 2 * P:(qb + 1) * 2 * P]
                for sl in lanes:
                    k2 = k_ref[pl.ds(offs[n_qb], BLK), sl]
                    vt = vt_ref[sl, pl.ds(offs[n_qb], BLK)]
                    ks += [k2, k2]
                    vth += [jnp.where(top, vt, zq), jnp.where(top, zq, vt)]
            _, b_l, w_l = _sb_scores(qs, ks, carry, diag, tri_excl, strict)
            wb = [w.astype(BF16) for w in w_l]
            new_acc = [acc[m] + _dot(vth[2 * m], wb[2 * m], NN) + _dot(vth[2 * m + 1], wb[2 * m + 1], NN)
                       for m in range(len(qbs) * P)]
            new_carry = [carry[n] + jnp.sum(b_l[n], axis=0, keepdims=True) for n in range(len(carry))]
            return new_carry, new_acc

        every = list(range(QB))
        c0 = jnp.zeros((1, BLK), F32)
        carry, acc = block(every, [i_first + qb for qb in every], [c0] * nch,
                           [jnp.zeros((LANES, BLK), F32)] * (QB * P), True)
        carry = [jnp.where(i_first > 0, c, NEG_BIG) for c in carry[:2 * P]] + carry[2 * P:]
        carry, acc = block(every, [jnp.maximum(i_first + qb - 1, 0) for qb in every], carry, acc, False)

        for qb in range(QB):
            i_qb = i_first + qb
            sub = slice(qb * 2 * P, (qb + 1) * 2 * P)

            def cond(st):
                return jnp.logical_and(i_qb - st[0] >= 0, st[1] > 0)

            def step(st, qb=qb, i_qb=i_qb):
                t, _, c_qb, a_qb = st
                c_qb, a_qb = block([qb], [i_qb - t], c_qb, a_qb, False)
                return t + 1, _any_alive(c_qb), c_qb, a_qb

            st = lax.while_loop(cond, step, (2, _any_alive(carry[sub]), carry[sub], acc[qb * P:(qb + 1) * P]))
            for p, sl in enumerate(lanes):
                o_ref[qb * BLK:(qb + 1) * BLK, sl] = jnp.transpose(st[3][p])

    return _pcall(
        body, (qkv, qkv, kv_t), name=name, out_shape=jax.ShapeDtypeStruct((S, D), F32),
        grid=(ngroup, nb // QB),
        in_specs=[pl.BlockSpec((QB * BLK, W), lambda g, i: (i, g)),
                  pl.BlockSpec((S, W), lambda g, i: (0, ngroup + g)),
                  pl.BlockSpec((W, S), lambda g, i: (ngroup + g, 0))],
        out_specs=pl.BlockSpec((QB * BLK, W), lambda g, i: (i, g)),
        sem=("arbitrary", "arbitrary"), carry=carry)


def _sb_bwd(qkv, kv_t, o, do, name, carry=None):
    S, D3 = qkv.shape
    D = D3 // 3
    npair, nb = D // LANES, S // BLK
    P = min(SB_BWD_PAIRS, npair)
    ngroup = npair // P
    W = P * LANES

    QB = SB_BWD_QBLOCKS if nb % SB_BWD_QBLOCKS == 0 else 1
    nch = QB * 2 * P

    def body(q_ref, o_ref, do_ref, qkv_hbm, kt_hbm, dq_ref, dk_ref, dv_ref, k_ref, v_ref, kt_ref, sems):
        grp = pl.program_id(0)
        i_first = pl.program_id(1) * QB
        m0 = _head_masks()
        top, strict, tri_excl, tri_incl = _sb_masks()
        zq = jnp.zeros((BLK, LANES), BF16)
        lanes = [slice(p * LANES, (p + 1) * LANES) for p in range(P)]

        @pl.when(pl.program_id(1) == 0)
        def _():
            copies = [pltpu.make_async_copy(qkv_hbm.at[:, pl.ds(pl.multiple_of((c * ngroup + grp) * W, LANES), W)],
                                            ref, sems.at[c - 1]) for c, ref in ((1, k_ref), (2, v_ref))]
            copies.append(pltpu.make_async_copy(kt_hbm.at[pl.ds(pl.multiple_of(grp * W, LANES), W), :],
                                                kt_ref, sems.at[2]))
            for cp in copies:
                cp.start()
            dk_ref[...] = jnp.zeros_like(dk_ref)
            dv_ref[...] = jnp.zeros_like(dv_ref)
            for cp in copies:
                cp.wait()

        qh, doh, delta = [], [], []
        for qb in range(QB):
            rs = slice(qb * BLK, (qb + 1) * BLK)
            for sl in lanes:
                q2, do2 = q_ref[rs, sl] * ATTN_SCALE, do_ref[rs, sl]
                qh += [jnp.where(m0, q2, zq), jnp.where(m0, zq, q2)]
                doh += [jnp.where(m0, do2, zq), jnp.where(m0, zq, do2)]
                prod_t = jnp.transpose(do2.astype(F32) * o_ref[rs, sl])
                delta += [jnp.sum(jnp.where(top, prod_t, 0.0), axis=0, keepdims=True),
                          jnp.sum(jnp.where(top, 0.0, prod_t), axis=0, keepdims=True)]

        def block(qbs, js, valid, cb, cg, dq, diag):
            offs = [pl.multiple_of(j * BLK, BLK) for j in js]
            n_ch = len(qbs) * 2 * P
            ks, vs, kth, qs, dos, dls = [], [], [], [], [], []
            for n_qb, qb in enumerate(qbs):
                chains = slice(qb * 2 * P, (qb + 1) * 2 * P)
                qs, dos, dls = qs + qh[chains], dos + doh[chains], dls + delta[chains]
                for sl in lanes:
                    k2, v2 = k_ref[pl.ds(offs[n_qb], BLK), sl], v_ref[pl.ds(offs[n_qb], BLK), sl]
                    ks += [k2, k2]
                    vs += [v2, v2]
                    kt = kt_ref[sl, pl.ds(offs[n_qb], BLK)] * ATTN_SCALE
                    kth += [jnp.where(top, kt, zq), jnp.where(top, zq, kt)]
            dws = [_dot(vs[n], dos[n], NT) for n in range(n_ch)]
            a_l, b_l, w_l = _sb_scores(qs, ks, cb, diag, tri_excl, strict)
            wb = [w.astype(BF16) for w in w_l]
            g_l = [dws[n] * wb[n].astype(F32) for n in range(n_ch)]
            gsplit = [_split_bf16(g) for g in g_l]
            gincs = [_dot(tri_incl, hi, NN) + _dot(tri_incl, lo, NN) for hi, lo in gsplit]
            dzs = []
            for n in range(n_ch):
                beta = jnp.exp(a_l[n])
                dz = g_l[n] - beta * (g_l[n] + ((dls[n] - cg[n]) - gincs[n]))
                if diag:
                    dz = jnp.where(strict, dz, 0.0)
                if valid[n // (2 * P)] is not None:
                    dz = jnp.where(valid[n // (2 * P)], dz, 0.0)
                dzs.append(dz.astype(BF16))
            ndq = []
            for n_qb in range(len(qbs)):
                for p, sl in enumerate(lanes):
                    n0 = n_qb * 2 * P + 2 * p
                    ndq.append(dq[n_qb * P + p] + _dot(kth[n0], dzs[n0], NN) + _dot(kth[n0 + 1], dzs[n0 + 1], NN))
                    dk_ref[pl.ds(offs[n_qb], BLK), sl] += _dot(dzs[n0], qs[n0], NN) + _dot(dzs[n0 + 1], qs[n0 + 1], NN)
                    dv_ref[pl.ds(offs[n_qb], BLK), sl] += _dot(wb[n0], dos[n0], NN) + _dot(wb[n0 + 1], dos[n0 + 1], NN)
            ncb = [cb[n] + jnp.sum(b_l[n], axis=0, keepdims=True) for n in range(n_ch)]
            ncg = [cg[n] + jnp.sum(g_l[n], axis=0, keepdims=True) for n in range(n_ch)]
            return ncb, ncg, ndq

        every = list(range(QB))
        c0 = jnp.zeros((1, BLK), F32)
        cb, cg, dq = block(every, [i_first + qb for qb in every], [None] * QB, [c0] * nch, [c0] * nch,
                           [jnp.zeros((LANES, BLK), F32)] * (QB * P), True)
        has_prev = i_first > 0
        cb = [jnp.where(has_prev, c, NEG_BIG) for c in cb[:2 * P]] + cb[2 * P:]
        cb, cg, dq = block(every, [jnp.maximum(i_first + qb - 1, 0) for qb in every], [has_prev] + [None] * (QB - 1),
                           cb, cg, dq, False)

        for qb in range(QB):
            i_qb = i_first + qb
            sub = slice(qb * 2 * P, (qb + 1) * 2 * P)

            def cond(st):
                return jnp.logical_and(i_qb - st[0] >= 0, st[1] > 0)

            def step(st, qb=qb, i_qb=i_qb):
                t, _, b_qb, g_qb, dq_qb = st
                b_qb, g_qb, dq_qb = block([qb], [i_qb - t], [None], b_qb, g_qb, dq_qb, False)
                return t + 1, _any_alive(b_qb), b_qb, g_qb, dq_qb

            st = lax.while_loop(cond, step, (2, _any_alive(cb[sub]), cb[sub], cg[sub], dq[qb * P:(qb + 1) * P]))
            for p, sl in enumerate(lanes):
                dq_ref[qb * BLK:(qb + 1) * BLK, sl] = jnp.transpose(st[4][p]).astype(BF16)

    blk = pl.BlockSpec((QB * BLK, W), lambda g, i: (i, g))
    col_all = pl.BlockSpec((S, W), lambda g, i: (0, g))
    hbm = pl.BlockSpec(memory_space=pl.ANY)
    return _pcall(
        body, (qkv, o, do, qkv, kv_t), name=name,
        out_shape=(jax.ShapeDtypeStruct((S, D), BF16), jax.ShapeDtypeStruct((S, D), F32),
                   jax.ShapeDtypeStruct((S, D), F32)),
        grid=(ngroup, nb // QB),
        in_specs=[blk, blk, blk, hbm, hbm],
        out_specs=(blk, col_all, col_all),
        scratch_shapes=[pltpu.VMEM((S, W), BF16), pltpu.VMEM((S, W), BF16), pltpu.VMEM((W, S), BF16),
                        pltpu.SemaphoreType.DMA((3,))],
        sem=("arbitrary", "arbitrary"), carry=carry)


SWA_Q_GROUPS = 4


def _roll_heads(x):
    return pltpu.roll(x.astype(F32), HEAD_DIM, 1).astype(BF16)


def _roll_rows(x):
    return pltpu.roll(x.astype(F32), HEAD_DIM, 0).astype(BF16)


def _swa_valid(i):
    k = lax.broadcasted_iota(jnp.int32, (2 * BLK, BLK), 0)
    q = lax.broadcasted_iota(jnp.int32, (2 * BLK, BLK), 1)
    diff = q + BLK - k
    return (diff >= 0) & (diff < BLK) & ((i > 0) | (k >= BLK))


def _swa_probs(z, valid, sink):
    z = jnp.where(valid, z * ATTN_SCALE, NEG_BIG)
    mx = jnp.maximum(jnp.max(z, axis=0, keepdims=True), sink)
    p = jnp.exp(z - mx)
    ps = jnp.exp(sink - mx)
    inv = 1.0 / (jnp.sum(p, axis=0, keepdims=True) + ps)
    return p * inv, ps * inv


def _swa_operands(q_ref, kc_ref, kp_ref, vc_ref, vp_ref, tc_ref, tp_ref, s_ref, nkvp):
    m0 = _head_masks()
    top = lax.broadcasted_iota(jnp.int32, (LANES, 2 * BLK), 0) < HEAD_DIM
    heads = []
    for m in range(nkvp):
        pair = slice(m * LANES, (m + 1) * LANES)
        kk = jnp.concatenate([kp_ref[:, pair], kc_ref[:, pair]], axis=0)
        vv = jnp.concatenate([vp_ref[:, pair], vc_ref[:, pair]], axis=0)
        tt = jnp.concatenate([tp_ref[pair, :], tc_ref[pair, :]], axis=1)
        ksw, vsw, tsw = _roll_heads(kk), _roll_heads(vv), _roll_rows(tt)
        zt = jnp.zeros_like(tt)
        for c in range(SWA_Q_GROUPS):
            q_lanes = slice((m * SWA_Q_GROUPS + c) * LANES, (m * SWA_Q_GROUPS + c + 1) * LANES)
            qc = q_ref[:, q_lanes]
            zq = jnp.zeros_like(qc)
            for u in range(2):
                same = u == c // 2
                sel = (lambda x, z, mk: jnp.where(mk, x, z)) if u == 0 else (lambda x, z, mk: jnp.where(mk, z, x))
                heads.append(dict(
                    m=m, q_lanes=q_lanes, same=same, sel=sel, qm=sel(qc, zq, m0),
                    k=kk if same else ksw, v=vv if same else vsw,
                    tm=sel(tt if same else tsw, zt, top),
                    sink=s_ref[0, (m * SWA_Q_GROUPS + c) * 2 + u]))
    return heads, m0


def _swa_specs(D, half, t_block):
    prev = lambda i: jnp.maximum(i - 1, 0)
    return [pl.BlockSpec((BLK, D), lambda i: (i, 0)),
            pl.BlockSpec((BLK, half), lambda i: (i, 0)),
            pl.BlockSpec((BLK, half), lambda i: (prev(i), 0)),
            pl.BlockSpec((BLK, half), lambda i: (i, 1)),
            pl.BlockSpec((BLK, half), lambda i: (prev(i), 1)),
            pl.BlockSpec((half, BLK), lambda i: (t_block, i)),
            pl.BlockSpec((half, BLK), lambda i: (t_block, prev(i))),
            pl.BlockSpec(memory_space=pltpu.SMEM)]


def _swa_fwd(q, kv, kv_t, sinks, name):
    S, D = q.shape
    half = kv.shape[1] // 2
    nkvp = half // LANES

    def body(q_ref, kc_ref, kp_ref, vc_ref, vp_ref, tc_ref, tp_ref, s_ref, o_ref):
        valid = _swa_valid(pl.program_id(0))
        heads, _ = _swa_operands(q_ref, kc_ref, kp_ref, vc_ref, vp_ref, tc_ref, tp_ref, s_ref, nkvp)
        zs = [_dot(hd["k"], hd["qm"], NT) for hd in heads]
        ps = [_swa_probs(z, valid, hd["sink"])[0].astype(BF16) for z, hd in zip(zs, heads)]
        for n in range(0, len(heads), 2):
            o_t = _dot(heads[n]["tm"], ps[n], NN) + _dot(heads[n + 1]["tm"], ps[n + 1], NN)
            o_ref[:, heads[n]["q_lanes"]] = jnp.transpose(o_t)

    return pl.pallas_call(
        body, name=name, out_shape=jax.ShapeDtypeStruct((S, D), F32),
        grid=(S // BLK,),
        in_specs=_swa_specs(D, half, 1),
        out_specs=pl.BlockSpec((BLK, D), lambda i: (i, 0)),
        compiler_params=_params("arbitrary"),
    )(q, kv, kv, kv, kv, kv_t, kv_t, sinks)


def _swa_bwd(q, kv, kv_t, sinks, o, do, cos_t, sin_t, name, carry=None):
    S, D = q.shape
    half = kv.shape[1] // 2
    nkvp = half // LANES
    nh = nkvp * 2 * SWA_Q_GROUPS

    def body(q_ref, kc_ref, kp_ref, vc_ref, vp_ref, tc_ref, tp_ref, s_ref, o_ref, do_ref, c_ref, sn_ref,
             dq_ref, dk_ref, dv_ref, ds_ref):
        i = pl.program_id(0)
        valid = _swa_valid(i)
        heads, m0 = _swa_operands(q_ref, kc_ref, kp_ref, vc_ref, vp_ref, tc_ref, tp_ref, s_ref, nkvp)
        top_q = lax.broadcasted_iota(jnp.int32, (LANES, BLK), 0) < HEAD_DIM

        @pl.when(i == 0)
        def _():
            dk_ref[...] = jnp.zeros_like(dk_ref)
            dv_ref[...] = jnp.zeros_like(dv_ref)
            ds_ref[...] = jnp.zeros_like(ds_ref)

        doms, deltas = [], []
        for n in range(0, nh, 2):
            doc = do_ref[:, heads[n]["q_lanes"]]
            prod_t = jnp.transpose(doc.astype(F32) * o_ref[:, heads[n]["q_lanes"]])
            for hd in heads[n:n + 2]:
                doms.append(hd["sel"](doc, jnp.zeros_like(doc), m0))
                deltas.append(jnp.sum(hd["sel"](prod_t, 0.0, top_q), axis=0, keepdims=True))
        zs = [_dot(hd["k"], hd["qm"], NT) for hd in heads]
        dps = [_dot(hd["v"], dom, NT) for dom, hd in zip(doms, heads)]
        pbs, dscs = [], []
        for n, hd in enumerate(heads):
            p, psink = _swa_probs(zs[n], valid, hd["sink"])
            pbs.append(p.astype(BF16))
            dscs.append((p * (dps[n] - deltas[n]) * ATTN_SCALE).astype(BF16))
            ds_ref[n:n + 1, :] += -(psink * deltas[n])
        for n in range(0, nh, 2):
            dq_rot = jnp.transpose(_dot(heads[n]["tm"], dscs[n], NN) + _dot(heads[n + 1]["tm"], dscs[n + 1], NN))
            dq_ref[:, heads[n]["q_lanes"]] = (
                dq_rot * c_ref[...] + _swap_halves(dq_rot * sn_ref[...])).astype(BF16)
        acc = {}
        for n, hd in enumerate(heads):
            dk_n = _dot(dscs[n], hd["qm"], NN)
            dv_n = _dot(pbs[n], doms[n], NN)
            for key, val in ((("k", hd["m"], hd["same"]), dk_n), (("v", hd["m"], hd["same"]), dv_n)):
                acc[key] = val if key not in acc else acc[key] + val
        poff = pl.multiple_of(jnp.maximum(i - 1, 0) * BLK, BLK)
        coff = pl.multiple_of(i * BLK, BLK)
        for m in range(nkvp):
            pair = slice(m * LANES, (m + 1) * LANES)
            dkk = acc["k", m, True] + pltpu.roll(acc["k", m, False], HEAD_DIM, 1)
            dvv = acc["v", m, True] + pltpu.roll(acc["v", m, False], HEAD_DIM, 1)
            dk_ref[pl.ds(poff, BLK), pair] += dkk[:BLK]
            dv_ref[pl.ds(poff, BLK), pair] += dvv[:BLK]
            dk_ref[pl.ds(coff, BLK), pair] += dkk[BLK:]
            dv_ref[pl.ds(coff, BLK), pair] += dvv[BLK:]

    qblk = pl.BlockSpec((BLK, D), lambda i: (i, 0))
    whole = pl.BlockSpec((S, half), lambda i: (0, 0))
    tab = pl.BlockSpec((BLK, LANES), lambda i: (i, 0))
    return _pcall(
        body, (q, kv, kv, kv, kv, kv_t, kv_t, sinks, o, do, cos_t, sin_t), name=name,
        out_shape=(jax.ShapeDtypeStruct((S, D), BF16),
                   jax.ShapeDtypeStruct((S, half), F32),
                   jax.ShapeDtypeStruct((S, half), F32),
                   jax.ShapeDtypeStruct((nh, LANES), F32)),
        grid=(S // BLK,),
        in_specs=_swa_specs(D, half, 0) + [qblk, qblk, tab, tab],
        out_specs=(qblk, whole, whole, pl.BlockSpec((nh, LANES), lambda i: (0, 0))),
        sem=("arbitrary",), carry=carry)


def _dev_index(p):
    return 4 * p[0] + 2 * p[1] + p[2]


def _gather_plan(x_refs, out_refs, send_sems, recv_sems, local_sems):
    n = len(x_refs)
    x_, y_, c_ = lax.axis_index("x"), lax.axis_index("y"), lax.axis_index("c")
    me, sibling = (x_, y_, c_), (x_, y_, 1 - c_)
    chips = [(1 - x_, y_), (x_, 1 - y_), (1 - x_, 1 - y_)]
    relay_block = (x_ ^ (1 - c_), y_ ^ c_, c_)
    relay_to = (x_ ^ c_, y_ ^ (1 - c_), c_)

    def copy(t, k, block, to, src=None):
        dst = out_refs[t].at[_dev_index(block)]
        return pltpu.make_async_remote_copy(
            src_ref=dst if src is None else src, dst_ref=dst,
            send_sem=send_sems.at[7 * t + k], recv_sem=recv_sems.at[7 * t + k],
            device_id=to, device_id_type=MESH)

    mine = [pltpu.make_async_copy(x_refs[t], out_refs[t].at[_dev_index(me)], local_sems.at[t]) for t in range(n)]
    first = []
    for t in range(n):
        first.append(copy(t, 0, me, sibling, src=x_refs[t]))
        first += [copy(t, 1 + j, me, (*chip, c_), src=x_refs[t]) for j, chip in enumerate(chips[:2])]
    relay = [copy(t, 3, relay_block, relay_to) for t in range(n)]
    arrived = lambda t, j: copy(t, 1 + j, (*chips[j], c_), me)
    forward = lambda t, j: copy(t, 4 + j, (*chips[j], c_), sibling)
    from_sibling = lambda t: copy(t, 0, sibling, me)
    forwarded = lambda t, j: copy(t, 4 + j, (*chips[j], 1 - c_), me)
    return n, mine, first, relay, arrived, forward, from_sibling, forwarded


def _gather_start(x_refs, out_refs, send_sems, recv_sems, local_sems):
    _, mine, first, *_ = _gather_plan(x_refs, out_refs, send_sems, recv_sems, local_sems)
    for cp in mine + first:
        cp.start()


def _gather_forward(x_refs, out_refs, send_sems, recv_sems, local_sems):
    n, _, _, relay, arrived, forward, _, _ = _gather_plan(x_refs, out_refs, send_sems, recv_sems, local_sems)
    for j in range(2):
        for t in range(n):
            arrived(t, j).wait_recv()
    for cp in relay:
        cp.start()
    for j in range(2):
        for t in range(n):
            forward(t, j).start()


def _gather_finish(x_refs, out_refs, send_sems, recv_sems, local_sems):
    n, mine, first, relay, arrived, forward, from_sibling, forwarded = _gather_plan(
        x_refs, out_refs, send_sems, recv_sems, local_sems)
    for t in range(n):
        arrived(t, 2).wait_recv()
        forward(t, 2).start()
    for t in range(n):
        from_sibling(t).wait_recv()
    for j in range(3):
        for t in range(n):
            forwarded(t, j).wait_recv()
    for cp in first + relay + [forward(t, j) for j in range(3) for t in range(n)]:
        cp.wait_send()
    for cp in mine:
        cp.wait()


def _scatter_plan(b_refs, out_refs, send_sems, recv_sems, local_sems):
    n = len(b_refs)
    x_, y_, c_ = lax.axis_index("x"), lax.axis_index("y"), lax.axis_index("c")
    my_idx = _dev_index((x_, y_, c_))
    mine = [pltpu.make_async_copy(b_refs[t].at[my_idx], out_refs[t].at[my_idx], local_sems.at[t]) for t in range(n)]
    copies = []
    for t in range(n):
        for k in range(1, N_DEV):
            peer = (x_ ^ ((k >> 2) & 1), y_ ^ ((k >> 1) & 1), c_ ^ (k & 1))
            copies.append(pltpu.make_async_remote_copy(
                src_ref=b_refs[t].at[_dev_index(peer)], dst_ref=out_refs[t].at[my_idx],
                send_sem=send_sems.at[7 * t + k - 1], recv_sem=recv_sems.at[7 * t + k - 1],
                device_id=peer, device_id_type=MESH))
    return mine, copies


def _scatter_start(b_refs, out_refs, send_sems, recv_sems, local_sems):
    mine, copies = _scatter_plan(b_refs, out_refs, send_sems, recv_sems, local_sems)
    for cp in mine + copies:
        cp.start()


def _scatter_finish(b_refs, out_refs, send_sems, recv_sems, local_sems):
    mine, copies = _scatter_plan(b_refs, out_refs, send_sems, recv_sems, local_sems)
    for cp in copies:
        cp.wait_recv()
    for cp in copies:
        cp.wait_send()
    for cp in mine:
        cp.wait()


def _exchange_operands(kind, tensors):
    if kind == "gather":
        args = list(tensors)
        shapes = [jax.ShapeDtypeStruct((N_DEV,) + t.shape, t.dtype) for t in tensors]
        return args, shapes, (_gather_start, _gather_forward, _gather_finish)
    args = [t.reshape(N_DEV, t.shape[0] // N_DEV, t.shape[1]) for t in tensors]
    shapes = [jax.ShapeDtypeStruct(a.shape, a.dtype) for a in args]
    return args, shapes, (_scatter_start, None, _scatter_finish)


def _exchange_results(kind, tensors, res):
    if kind == "gather":
        return [r.reshape(N_DEV * t.shape[0], t.shape[1]) for r, t in zip(res, tensors)]
    return list(res)


def _exchange_sems(n):
    return [pltpu.SemaphoreType.DMA((7 * n,)), pltpu.SemaphoreType.DMA((7 * n,)), pltpu.SemaphoreType.DMA((n,))]


def _exchange(kind, tensors, name):
    n = len(tensors)
    args, shapes, phases = _exchange_operands(kind, tensors)

    def body(*refs):
        for phase in phases:
            if phase is not None:
                phase(refs[:n], refs[n:2 * n], *refs[2 * n:])

    hbm = pl.BlockSpec(memory_space=pl.ANY)
    res = pl.pallas_call(body, name=name, out_shape=shapes, in_specs=[hbm] * n, out_specs=[hbm] * n,
                         scratch_shapes=_exchange_sems(n))(*args)
    return _exchange_results(kind, tensors, res)


def _pcall(body, args, *, name, out_shape, grid, in_specs, out_specs, sem, scratch_shapes=(), carry=None):
    if carry is None:
        out = pl.pallas_call(body, name=name, out_shape=out_shape, grid=grid, in_specs=list(in_specs),
                             out_specs=out_specs, scratch_shapes=list(scratch_shapes),
                             compiler_params=_params(*sem))(*args)
        return out, None
    kind, tensors = carry
    multi = isinstance(out_shape, (tuple, list))
    shapes = list(out_shape) if multi else [out_shape]
    ospecs = list(out_specs) if multi else [out_specs]
    n_in, n_out, n_scr, n_c = len(in_specs), len(shapes), len(scratch_shapes), len(tensors)
    c_args, c_shapes, (start, forward, finish) = _exchange_operands(kind, tensors)
    n_steps = 1
    for g in grid:
        n_steps *= g
    late = (3 * n_steps) // 4

    def wrapped(*refs):
        ins, rest = refs[:n_in], refs[n_in:]
        c_in, rest = rest[:n_c], rest[n_c:]
        outs, rest = rest[:n_out], rest[n_out:]
        c_out, rest = rest[:n_c], rest[n_c:]
        scr, sems = rest[:n_scr], rest[n_scr:]
        step = pl.program_id(0)
        for a in range(1, len(grid)):
            step = step * grid[a] + pl.program_id(a)

        @pl.when(step == 0)
        def _():
            start(c_in, c_out, *sems)

        body(*ins, *outs, *scr)

        if forward is not None:
            @pl.when(step == late)
            def _():
                forward(c_in, c_out, *sems)

        @pl.when(step == n_steps - 1)
        def _():
            finish(c_in, c_out, *sems)

    hbm = pl.BlockSpec(memory_space=pl.ANY)
    res = pl.pallas_call(
        wrapped, name=name, out_shape=shapes + c_shapes, grid=grid,
        in_specs=list(in_specs) + [hbm] * n_c, out_specs=ospecs + [hbm] * n_c,
        scratch_shapes=list(scratch_shapes) + _exchange_sems(n_c),
        compiler_params=_params(*sem))(*args, *c_args)
    outs = tuple(res[:n_out]) if multi else res[0]
    return outs, _exchange_results(kind, tensors, res[n_out:])


def _sum8(parts, name):
    _, R, C = parts.shape
    tr = _tile(R, 256, 16)

    def body(p_ref, g_ref):
        g = p_ref[0].astype(F32)
        for s in range(1, N_DEV):
            g = g + p_ref[s].astype(F32)
        g_ref[...] = g

    return pl.pallas_call(
        body, name=name, out_shape=jax.ShapeDtypeStruct((R, C), F32),
        grid=(R // tr,),
        in_specs=[pl.BlockSpec((N_DEV, tr, C), lambda i: (0, i, 0))],
        out_specs=pl.BlockSpec((tr, C), lambda i: (i, 0)),
        compiler_params=_params("parallel"),
    )(parts)


def _adamw(g, w, m, v, name):
    R, C = g.shape
    tr = _tile(R, 256, 8)
    c1 = 1.0 - ADAM_B1 ** ADAM_STEP
    c2 = 1.0 - ADAM_B2 ** ADAM_STEP

    def body(g_ref, w_ref, m_ref, v_ref, d_ref, nm_ref, nv_ref):
        gg = g_ref[...]
        nm = ADAM_B1 * m_ref[...] + (1.0 - ADAM_B1) * gg
        nv = ADAM_B2 * v_ref[...] + (1.0 - ADAM_B2) * (gg * gg)
        m_hat = nm / c1
        v_hat = nv / c2
        nm_ref[...] = nm
        nv_ref[...] = nv
        d_ref[...] = -ADAM_LR * (m_hat / (jnp.sqrt(v_hat) + ADAM_EPS) + ADAM_WD * w_ref[...])

    row = pl.BlockSpec((tr, C), lambda i: (i, 0))
    shp = jax.ShapeDtypeStruct((R, C), F32)
    return pl.pallas_call(
        body, name=name, out_shape=(shp, shp, shp),
        grid=(R // tr,), in_specs=[row, row, row, row], out_specs=(row, row, row),
        compiler_params=_params("parallel"),
    )(g, w, m, v)


def _sum_adamw(parts_list, w, m, v, name):
    L, R, C = w.shape
    tr = _tile(R, 256, 16)
    c1 = 1.0 - ADAM_B1 ** ADAM_STEP
    c2 = 1.0 - ADAM_B2 ** ADAM_STEP

    def body(*refs):
        p_refs = refs[:L]
        w_ref, m_ref, v_ref, g_ref, d_ref, nm_ref, nv_ref = refs[L:]
        for layer in range(L):
            @pl.when(pl.program_id(0) == layer)
            def _():
                g = p_refs[layer][0].astype(F32)
                for s in range(1, N_DEV):
                    g = g + p_refs[layer][s].astype(F32)
                nm = ADAM_B1 * m_ref[0] + (1.0 - ADAM_B1) * g
                nv = ADAM_B2 * v_ref[0] + (1.0 - ADAM_B2) * (g * g)
                g_ref[0] = g
                nm_ref[0] = nm
                nv_ref[0] = nv
                d_ref[0] = -ADAM_LR * ((nm / c1) / (jnp.sqrt(nv / c2) + ADAM_EPS) + ADAM_WD * w_ref[0])

    def parts_spec(layer):
        return pl.BlockSpec((N_DEV, tr, C), lambda l, i: (0, jnp.where(l == layer, i, 0), 0))

    blk = pl.BlockSpec((1, tr, C), lambda l, i: (l, i, 0))
    shp = jax.ShapeDtypeStruct((L, R, C), F32)
    return pl.pallas_call(
        body, name=name, out_shape=(shp, shp, shp, shp),
        grid=(L, R // tr),
        in_specs=[parts_spec(layer) for layer in range(L)] + [blk, blk, blk],
        out_specs=(blk, blk, blk, blk),
        compiler_params=_params("arbitrary", "arbitrary"),
    )(*parts_list, w, m, v)


def _ffn_down(act, wo, h, tag):
    return _mm(act, wo, NN, F32, f"{tag}_down", scale=FFN_RES_SCALE, res=h, tm=512, tn=1024, tk=2816)


def _ffn_fwd(h, g, win_t, wo, tag, carry=None, loss=None):
    return _ffn_fwd_fused(h, g, win_t, wo, f"{tag}_fwd", carry=carry, loss=loss)


def _ffn_bwd(dh, h, g, win_t, wo, saved, tag, scatter=False, carry=None):
    xn, silu, dsilu, up, act = saved
    dwo = _mm(act, dh, TN, BF16, f"{tag}_dwo", scale=FFN_RES_SCALE, tm=1408, tn=1024, tk=TN_CHUNK)
    if not scatter:
        (dh_in, dg, dgate, dup), got = _ffn_bwd_fused(dh, h, g, win_t, wo, silu, dsilu, up, f"{tag}_bwd", carry=carry)
        dwin_t, _ = _dw_rows([dgate, dup], xn, f"{tag}_dwin")
        return dh_in, dg, dwin_t, dwo, got
    dgate, dup = _ffn_dact(dh, wo, silu, dsilu, up, f"{tag}_dact")
    dwin_t, got_wo = _dw_rows([dgate, dup], xn, f"{tag}_dwin", carry=("scatter", [dwo]))
    (dh_in, dg), got_win = _dx_norm_bwd([(dgate, win_t, NN, 2, 0), (dup, win_t, NN, 2, 1)], h, g, dh, f"{tag}_dx",
                                        carry=("scatter", [dwin_t]))
    return dh_in, dg, got_win[0], got_wo[0]


def _proj(a, w, dims, out_dtype, name, res=None):
    return _mm(a, w, dims, out_dtype, name, res=res, tm=1024, tn=1024, tk=1024)


def _proj_dw(x, dy, name):
    return _mm(x, dy, TN, BF16, name, tm=1024, tn=1024, tk=TN_CHUNK)


def kernel(x, ffn1_norm, ffn1_w_in, ffn1_w_out, mix_norm, ffn2_norm, ffn2_w_in, ffn2_w_out, sb_w_qkv, sb_w_o, kv_norm, kv_w, swa_w_q, swa_sinks, swa_w_o, final_norm, loss_target, m_ffn1_norm, m_ffn1_w_in, m_ffn1_w_out, m_mix_norm, m_ffn2_norm, m_ffn2_w_in, m_ffn2_w_out, m_sb_w_qkv, m_sb_w_o, m_kv_norm, m_kv_w, m_swa_w_q, m_swa_sinks, m_swa_w_o, m_final_norm, v_ffn1_norm, v_ffn1_w_in, v_ffn1_w_out, v_mix_norm, v_ffn2_norm, v_ffn2_w_in, v_ffn2_w_out, v_sb_w_qkv, v_sb_w_o, v_kv_norm, v_kv_w, v_swa_w_q, v_swa_sinks, v_swa_w_o, v_final_norm):
    S, D = x.shape[1], x.shape[2]
    L = ffn1_w_in.shape[0]
    KV = kv_w.shape[1]
    assert L == 2 and swa_sinks.shape == (1, 2 * SWA_Q_GROUPS * KV // (2 * LANES))

    def bf(w):
        return w.astype(BF16)

    def bft(w):
        return jnp.transpose(w).astype(BF16)

    cos_t, sin_t = _rope_tables(S)
    h0 = x.reshape(S, D)
    tgt = loss_target.reshape(S, D)

    win1a_t, = _exchange("gather", [bft(ffn1_w_in[0])], "gather_first_weight")
    sv_a1, (wo1a, wqkv_t, w_sbo) = _ffn_up(
        h0, ffn1_norm[0], win1a_t, "ffn1a_up",
        carry=("gather", [bf(ffn1_w_out[0]), bft(sb_w_qkv[0]), bf(sb_w_o[0])]))
    h1 = _ffn_down(sv_a1[-1], wo1a, h0, "ffn1a")
    hn_a, qkv, kv_t = _norm_proj(h1, mix_norm[0], wqkv_t, NT, "sb_qkv", tail_t=2 * D)
    o_sb, (win2a_t, wo2a, w_kv) = _sb_fwd(qkv, kv_t, "sb_attn", carry=("gather", [
        bft(ffn2_w_in[0]), bf(ffn2_w_out[0]), bf(kv_w)]))
    h2 = _proj(o_sb, w_sbo, NN, F32, "sb_out", res=h1)
    h3, sv_a2, (win1b_t, wo1b, w_q, w_swo) = _ffn_fwd(h2, ffn2_norm[0], win2a_t, wo2a, "ffn2a", carry=("gather", [
        bft(ffn1_w_in[1]), bf(ffn1_w_out[1]), bf(swa_w_q[0]), bf(swa_w_o[0])]))
    kvn, kv_rot, kv_rot_t = _norm_proj(h3, kv_norm, w_kv, NN, "kv_proj", rope=(cos_t, sin_t, KV // (2 * LANES)),
                                       tail_t=KV)
    h4, sv_b1, (win2b_t, wo2b) = _ffn_fwd(h3, ffn1_norm[1], win1b_t, wo1b, "ffn1b", carry=("gather", [
        bft(ffn2_w_in[1]), bf(ffn2_w_out[1])]))
    hn_b, q_rot = _norm_proj(h4, mix_norm[1], w_q, NN, "swa_q", rope=(cos_t, sin_t, D // LANES))
    o_sw = _swa_fwd(q_rot, kv_rot, kv_rot_t, swa_sinks, "swa_attn")
    h5 = _proj(o_sw, w_swo, NN, F32, "swa_out", res=h4)
    (dh6, dg_final, sq_err), sv_b2, _ = _ffn_fwd(h5, ffn2_norm[1], win2b_t, wo2b, "ffn2b", loss=(final_norm, tgt))
    loss_local = 0.5 * jnp.sum(sq_err) / D

    dh5, dg_f2b, dwin2b_t, dwo2b, _ = _ffn_bwd(dh6, h5, ffn2_norm[1], win2b_t, wo2b, sv_b2, "ffn2b")
    do_sw = _proj(dh5, w_swo, NT, BF16, "swa_out_dx")
    dw_swo = _proj_dw(o_sw, dh5, "swa_out_dw")
    (dq, dk_sw, dv_sw, dsink), (p_win2b, p_swo) = _swa_bwd(
        q_rot, kv_rot, kv_rot_t, swa_sinks, o_sw, do_sw, cos_t, sin_t, "swa_attn_bwd",
        carry=("scatter", [dwin2b_t, dw_swo]))
    dw_q = _proj_dw(hn_b, dq, "swa_q_dw")
    (dh4, dg_mix_b), _ = _dx_norm_bwd([(dq, w_q, NT, 1, 0)], h4, mix_norm[1], dh5, "swa_q_dx", tm=512)
    dh3, dg_f1b, dwin1b_t, dwo1b, _ = _ffn_bwd(dh4, h3, ffn1_norm[1], win1b_t, wo1b, sv_b1, "ffn1b")
    dkv = _rotary_bwd([dk_sw, dv_sw], cos_t, sin_t, KV // (2 * LANES), "kv_rope_bwd")
    dw_kv = _proj_dw(kvn, dkv, "kv_proj_dw")
    (dh3, dg_kv), _ = _dx_norm_bwd([(dkv, w_kv, NT, 1, 0)], h3, kv_norm, dh3, "kv_proj_dx", tm=512)
    dh2, dg_f2a, dwin2a_t, dwo2a, (p_win1b, p_kv) = _ffn_bwd(
        dh3, h2, ffn2_norm[0], win2a_t, wo2a, sv_a2, "ffn2a", carry=("scatter", [dwin1b_t, dw_kv]))
    do_sb = _proj(dh2, w_sbo, NT, BF16, "sb_out_dx")
    dw_sbo = _proj_dw(o_sb, dh2, "sb_out_dw")
    (dq_sb, dk_sb, dv_sb), (p_win2a, p_wo2a, p_sbo, p_wo1b, p_q, p_wo2b) = _sb_bwd(
        qkv, kv_t, o_sb, do_sb, "sb_attn_bwd", carry=("scatter", [dwin2a_t, dwo2a, dw_sbo, dwo1b, dw_q, dwo2b]))
    dqkv = [dq_sb, dk_sb, dv_sb]
    dwqkv_t, _ = _dw_rows(dqkv, hn_a, "sb_qkv_dw", tk=TN_CHUNK // 2)
    (dh1, dg_mix_a), (p_qkv,) = _dx_norm_bwd([(dy, wqkv_t, NN, 3, n) for n, dy in enumerate(dqkv)], h1, mix_norm[0],
                                             dh2, "sb_qkv_dx", carry=("scatter", [dwqkv_t]), tm=512)
    dx, dg_f1a, p_win1a, p_wo1a = _ffn_bwd(dh1, h0, ffn1_norm[0], win1a_t, wo1a, sv_a1, "ffn1a", scatter=True)

    def from_t(parts, tag):
        return jnp.transpose(_sum8(parts, f"sum_{tag}"))

    grads = {
        "ffn1_w_in": jnp.stack([from_t(p_win1a, "win1a"), from_t(p_win1b, "win1b")]),
        "ffn2_w_in": jnp.stack([from_t(p_win2a, "win2a"), from_t(p_win2b, "win2b")]),
        "sb_w_qkv": from_t(p_qkv, "qkv")[None],
    }
    row_parts = {"ffn1_w_out": [p_wo1a, p_wo1b], "ffn2_w_out": [p_wo2a, p_wo2b], "sb_w_o": [p_sbo],
                 "kv_w": [p_kv], "swa_w_q": [p_q], "swa_w_o": [p_swo]}

    small_w = [ffn1_norm, mix_norm, ffn2_norm, kv_norm, final_norm, swa_sinks]
    small_m = [m_ffn1_norm, m_mix_norm, m_ffn2_norm, m_kv_norm, m_final_norm, m_swa_sinks]
    small_v = [v_ffn1_norm, v_mix_norm, v_ffn2_norm, v_kv_norm, v_final_norm, v_swa_sinks]
    SMALL_ROWS = 16

    def pack_small(ts):
        rows_ = [t.reshape(-1, D) for t in ts[:-1]]
        sink_row = jnp.pad(ts[-1].reshape(1, -1), ((0, 0), (0, D - ts[-1].size)))
        flat = jnp.concatenate(rows_ + [sink_row], axis=0)
        return jnp.pad(flat, ((0, SMALL_ROWS - flat.shape[0]), (0, 0)))

    def unpack_small(flat):
        out, r = [], 0
        for t in small_w[:-1]:
            n = t.size // D
            out.append(flat[r:r + n].reshape(t.shape))
            r += n
        out.append(flat[r, :swa_sinks.size].reshape(swa_sinks.shape))
        return out

    def gain(parts8):
        return jnp.sum(parts8, axis=0, keepdims=True)

    g_small_local = pack_small([
        jnp.concatenate([gain(dg_f1a), gain(dg_f1b)], axis=0),
        jnp.concatenate([gain(dg_mix_a), gain(dg_mix_b)], axis=0),
        jnp.concatenate([gain(dg_f2a), gain(dg_f2b)], axis=0),
        gain(dg_kv), gain(dg_final), jnp.sum(dsink, axis=-1).reshape(1, -1)])
    loss_row = sum(t.size for t in small_w[:-1]) // D + 1
    assert loss_row < SMALL_ROWS
    g_small_local = g_small_local.at[loss_row, 0].set(loss_local)
    small_parts = _exchange("gather", [g_small_local], "gather_small_grads")[0]
    g_small = _sum8(small_parts.reshape(N_DEV, SMALL_ROWS, D), "sum_small")
    loss = g_small[loss_row, 0]
    d_small, nm_small, nv_small = _adamw(g_small, pack_small(small_w), pack_small(small_m), pack_small(small_v), "adamw_small")
    small_names = ["ffn1_norm", "mix_norm", "ffn2_norm", "kv_norm", "final_norm", "swa_sinks"]
    result = {"grad": dict(zip(small_names, unpack_small(g_small))),
              "delta": dict(zip(small_names, unpack_small(d_small))),
              "new_m": dict(zip(small_names, unpack_small(nm_small))),
              "new_v": dict(zip(small_names, unpack_small(nv_small)))}

    big = {"ffn1_w_in": (ffn1_w_in, m_ffn1_w_in, v_ffn1_w_in), "ffn1_w_out": (ffn1_w_out, m_ffn1_w_out, v_ffn1_w_out),
           "ffn2_w_in": (ffn2_w_in, m_ffn2_w_in, v_ffn2_w_in), "ffn2_w_out": (ffn2_w_out, m_ffn2_w_out, v_ffn2_w_out),
           "sb_w_qkv": (sb_w_qkv, m_sb_w_qkv, v_sb_w_qkv), "sb_w_o": (sb_w_o, m_sb_w_o, v_sb_w_o),
           "kv_w": (kv_w, m_kv_w, v_kv_w), "swa_w_q": (swa_w_q, m_swa_w_q, v_swa_w_q),
           "swa_w_o": (swa_w_o, m_swa_w_o, v_swa_w_o)}
    for nm, (w, m, v) in big.items():
        if nm in row_parts:
            three_d = lambda t: t.reshape((len(row_parts[nm]),) + t.shape[-2:])
            g, d, new_m, new_v = _sum_adamw(row_parts[nm], three_d(w), three_d(m), three_d(v), f"adamw_{nm}")
        else:
            g = grads[nm]
            two_d = lambda t: t.reshape(-1, t.shape[-1])
            d, new_m, new_v = _adamw(two_d(g), two_d(w), two_d(m), two_d(v), f"adamw_{nm}")
        result["grad"][nm] = g.reshape(w.shape)
        result["delta"][nm] = d.reshape(w.shape)
        result["new_m"][nm] = new_m.reshape(w.shape)
        result["new_v"][nm] = new_v.reshape(w.shape)

    order = ["ffn1_norm", "ffn1_w_in", "ffn1_w_out", "mix_norm", "ffn2_norm", "ffn2_w_in", "ffn2_w_out",
             "sb_w_qkv", "sb_w_o", "kv_norm", "kv_w", "swa_w_q", "swa_sinks", "swa_w_o", "final_norm"]
    outs = [result[kind][nm] for kind in ("grad", "delta", "new_m", "new_v") for nm in order]
    return (loss, dx.reshape(x.shape), *outs)
```

```python
import jax
import jax.numpy as jnp
from jax import lax
from jax.experimental import pallas as pl
from jax.experimental.pallas import tpu as pltpu

F32 = jnp.float32
BF16 = jnp.bfloat16

N_DEV = 8
HEAD_DIM = 64
LANES = 128
BLK = 128
RMS_EPS = 1e-6
FFN_RES_SCALE = 0.5
ROPE_THETA = 10000.0
ATTN_SCALE = HEAD_DIM ** -0.5
SB_LOG_FLOOR = -88.0
NEG_BIG = -1e30
VMEM_LIMIT_V7X = 56 * 1024 * 1024

ADAM_LR = 0.001
ADAM_B1 = 0.9
ADAM_B2 = 0.999
ADAM_EPS = 1e-08
ADAM_WD = 0.01
ADAM_STEP = 10

NN = ((1,), (0,))
NT = ((1,), (1,))
TN = ((0,), (0,))
TN_CHUNK = 2048
MESH = pl.DeviceIdType.MESH


def _dot(a, b, dims):
    return lax.dot_general(a, b, (dims, ((), ())), preferred_element_type=F32)


def _tile(n, pref, mult=LANES):
    if n <= pref:
        return n
    t = (pref // mult) * mult
    while t >= mult:
        if n % t == 0:
            return t
        t -= mult
    return n


def _params(*sem):
    return pltpu.CompilerParams(dimension_semantics=sem, vmem_limit_bytes=VMEM_LIMIT_V7X)


def _mm(a, b, dims, out_dtype, name, scale=1.0, res=None, tm=512, tn=512, tk=512):
    if dims == NN:
        (M, K), (_, N) = a.shape, b.shape
    elif dims == NT:
        (M, K), (N, _) = a.shape, b.shape
    else:
        (K, M), (_, N) = a.shape, b.shape
    tm, tn, tk = _tile(M, tm), _tile(N, tn), _tile(K, tk)
    nk = K // tk
    if dims == TN:
        a_spec = pl.BlockSpec((tk, tm), lambda i, j, k: (k, i))
    else:
        a_spec = pl.BlockSpec((tm, tk), lambda i, j, k: (i, k))
    if dims == NT:
        b_spec = pl.BlockSpec((tn, tk), lambda i, j, k: (j, k))
    else:
        b_spec = pl.BlockSpec((tk, tn), lambda i, j, k: (k, j))
    o_spec = pl.BlockSpec((tm, tn), lambda i, j, k: (i, j))
    has_res = res is not None

    def body(*refs):
        a_ref, b_ref = refs[0], refs[1]
        r_ref = refs[2] if has_res else None
        o_ref = refs[3] if has_res else refs[2]

        def finish(acc):
            r = acc * scale if scale != 1.0 else acc
            if has_res:
                r = r + r_ref[...]
            o_ref[...] = r.astype(out_dtype)

        p = _dot(a_ref[...].astype(BF16), b_ref[...].astype(BF16), dims)
        if nk == 1:
            finish(p)
        else:
            acc_ref = refs[-1]
            k = pl.program_id(2)

            @pl.when(k == 0)
            def _():
                acc_ref[...] = p

            @pl.when(k > 0)
            def _():
                acc_ref[...] += p

            @pl.when(k == nk - 1)
            def _():
                finish(acc_ref[...])

    in_specs = [a_spec, b_spec] + ([o_spec] if has_res else [])
    args = (a, b) + ((res,) if has_res else ())
    return pl.pallas_call(
        body, name=name,
        out_shape=jax.ShapeDtypeStruct((M, N), out_dtype),
        grid=(M // tm, N // tn, nk),
        in_specs=in_specs, out_specs=o_spec,
        scratch_shapes=[pltpu.VMEM((tm, tn), F32)] if nk > 1 else [],
        compiler_params=_params("parallel", "parallel", "arbitrary"),
    )(*args)


def _rows8(x):
    r, d = x.shape
    return jnp.sum(x.reshape(r // 8, 8, d), axis=0)


def _norm_proj(h, g, w, dims, name, rope=None, tail_t=0):
    S, D = h.shape
    N = w.shape[1] if dims == NN else w.shape[0]
    tm = _tile(S, 512, 16)

    def body(h_ref, g_ref, w_ref, *rest):
        xn_ref, y_ref = rest[-3:-1] if tail_t else rest[-2:]
        x = h_ref[...]
        r = lax.rsqrt(jnp.mean(x * x, axis=-1, keepdims=True) + RMS_EPS)
        xn = ((x * r) * g_ref[...]).astype(BF16)
        xn_ref[...] = xn
        y = _dot(xn, w_ref[...], dims)
        if rope is not None:
            cs, sn = rest[0][...], rest[1][...]
            groups = [y[:, gidx * LANES:(gidx + 1) * LANES] for gidx in range(N // LANES)]
            y = jnp.concatenate([v * cs + _swap_halves(v) * sn if gidx < rope[2] else v
                                 for gidx, v in enumerate(groups)], axis=1)
        y_ref[...] = y.astype(BF16)
        if tail_t:
            rest[-1][...] = jnp.transpose(y[:, N - tail_t:]).astype(BF16)

    row = pl.BlockSpec((tm, D), lambda i: (i, 0))
    tab = pl.BlockSpec((tm, LANES), lambda i: (i, 0))
    in_specs = [row, pl.BlockSpec((1, D), lambda i: (0, 0)), pl.BlockSpec(w.shape, lambda i: (0, 0))]
    args = (h, g.reshape(1, D), w)
    if rope is not None:
        in_specs += [tab, tab]
        args += (rope[0], rope[1])
    out_shape = [jax.ShapeDtypeStruct((S, D), BF16), jax.ShapeDtypeStruct((S, N), BF16)]
    out_specs = [row, pl.BlockSpec((tm, N), lambda i: (i, 0))]
    if tail_t:
        out_shape.append(jax.ShapeDtypeStruct((tail_t, S), BF16))
        out_specs.append(pl.BlockSpec((tail_t, tm), lambda i: (0, i)))
    return pl.pallas_call(
        body, name=name, out_shape=out_shape, grid=(S // tm,),
        in_specs=in_specs, out_specs=out_specs,
        compiler_params=_params("parallel"),
    )(*args)


def _ffn_up(h, g, win_t, name, carry=None):
    S, D = h.shape
    F = win_t.shape[0] // 2
    tm = _tile(S, 256, 16)

    def body(h_ref, g_ref, win_hbm, xn_ref, silu_ref, dsilu_ref, up_ref, act_ref, win_v, sems):
        _load_resident([(win_hbm, win_v)], sems)
        x = h_ref[...]
        r = lax.rsqrt(jnp.mean(x * x, axis=-1, keepdims=True) + RMS_EPS)
        xn = ((x * r) * g_ref[...]).astype(BF16)
        xn_ref[...] = xn
        gate = _dot(xn, win_v[:F, :], NT)
        up = _dot(xn, win_v[F:, :], NT)
        sig = 1.0 / (1.0 + jnp.exp(-gate))
        silu = gate * sig
        up_ref[...] = up.astype(BF16)
        silu_ref[...] = silu.astype(BF16)
        dsilu_ref[...] = (sig + silu * (1.0 - sig)).astype(BF16)
        act_ref[...] = (silu * up).astype(BF16)

    row = pl.BlockSpec((tm, D), lambda i: (i, 0))
    wide = pl.BlockSpec((tm, F), lambda i: (i, 0))
    hid = jax.ShapeDtypeStruct((S, F), BF16)
    return _pcall(
        body, (h, g.reshape(1, D), win_t), name=name,
        out_shape=(jax.ShapeDtypeStruct((S, D), BF16), hid, hid, hid, hid),
        grid=(S // tm,),
        in_specs=[row, pl.BlockSpec((1, D), lambda i: (0, 0)), pl.BlockSpec(memory_space=pl.ANY)],
        out_specs=(row, wide, wide, wide, wide),
        scratch_shapes=[pltpu.VMEM(win_t.shape, BF16), pltpu.SemaphoreType.DMA((1,))],
        sem=("arbitrary",), carry=carry)


def _ffn_dact(dh, wo, silu, dsilu, up, name):
    S, D = dh.shape
    F = wo.shape[0]
    tm = _tile(S, 256, 16)

    def body(dh_ref, wo_hbm, s_ref, ds_ref, u_ref, dg_ref, du_ref, wo_v, sems):
        _load_resident([(wo_hbm, wo_v)], sems)
        d = _dot(dh_ref[...].astype(BF16), wo_v[...], NT) * FFN_RES_SCALE
        du_ref[...] = (d * s_ref[...].astype(F32)).astype(BF16)
        dg_ref[...] = (d * u_ref[...].astype(F32) * ds_ref[...].astype(F32)).astype(BF16)

    wide = pl.BlockSpec((tm, F), lambda i: (i, 0))
    hid = jax.ShapeDtypeStruct((S, F), BF16)
    return pl.pallas_call(
        body, name=name, out_shape=(hid, hid),
        grid=(S // tm,),
        in_specs=[pl.BlockSpec((tm, D), lambda i: (i, 0)), pl.BlockSpec(memory_space=pl.ANY), wide, wide, wide],
        out_specs=(wide, wide),
        scratch_shapes=[pltpu.VMEM(wo.shape, BF16), pltpu.SemaphoreType.DMA((1,))],
        compiler_params=_params("arbitrary"),
    )(dh, wo, silu, dsilu, up)


def _dw_rows(srcs, x, name, carry=None, tk=TN_CHUNK):
    n = len(srcs)
    S, F = srcs[0].shape
    D = x.shape[1]
    tr, tk = _tile(F, 1408), _tile(S, tk, 16)
    nf, nk = F // tr, S // tk

    def body(*refs):
        src_refs, (x_ref, o_ref, acc_ref) = refs[:n], refs[n:]
        r, k = pl.program_id(0), pl.program_id(1)
        for s in range(n):
            @pl.when(r // nf == s)
            def _():
                p = _dot(src_refs[s][...].astype(BF16), x_ref[...], TN)

                @pl.when(k == 0)
                def _():
                    acc_ref[...] = p

                @pl.when(k > 0)
                def _():
                    acc_ref[...] += p

        @pl.when(k == nk - 1)
        def _():
            o_ref[...] = acc_ref[...].astype(BF16)

    def src_spec(s):
        return pl.BlockSpec((tk, tr), lambda r, k: (jnp.where(r // nf == s, k, 0), jnp.clip(r - s * nf, 0, nf - 1)))

    return _pcall(
        body, (*srcs, x), name=name, out_shape=jax.ShapeDtypeStruct((n * F, D), BF16),
        grid=(n * nf, nk),
        in_specs=[src_spec(s) for s in range(n)] + [pl.BlockSpec((tk, D), lambda r, k: (k, 0))],
        out_specs=pl.BlockSpec((tr, D), lambda r, k: (r, 0)),
        scratch_shapes=[pltpu.VMEM((tr, D), F32)],
        sem=("arbitrary", "arbitrary"), carry=carry)


def _dx_norm_bwd(terms, h, g, res, name, carry=None, tm=256):
    S, D = h.shape
    tm = _tile(S, tm, 16)
    n = len(terms)

    def body(*refs):
        dy_refs, w_refs = refs[:n], refs[n:2 * n]
        h_ref, g_ref, r_ref, dh_ref, dg_ref = refs[2 * n:]
        d = _dot(dy_refs[0][...].astype(BF16), w_refs[0][...], terms[0][2])
        for t in range(1, n):
            d = d + _dot(dy_refs[t][...].astype(BF16), w_refs[t][...], terms[t][2])
        x = h_ref[...]
        r = lax.rsqrt(jnp.mean(x * x, axis=-1, keepdims=True) + RMS_EPS)
        xhat = x * r
        dxh = d * g_ref[...]
        c = jnp.mean(dxh * xhat, axis=-1, keepdims=True)
        dh_ref[...] = r * (dxh - xhat * c) + r_ref[...]
        part = _rows8(d * xhat)

        @pl.when(pl.program_id(0) == 0)
        def _():
            dg_ref[...] = part

        @pl.when(pl.program_id(0) > 0)
        def _():
            dg_ref[...] += part

    def w_spec(w, nblk, blk):
        return pl.BlockSpec((w.shape[0] // nblk, w.shape[1]), lambda i: (blk, 0))

    row = pl.BlockSpec((tm, D), lambda i: (i, 0))
    in_specs = [pl.BlockSpec((tm, t[0].shape[1]), lambda i: (i, 0)) for t in terms]
    in_specs += [w_spec(t[1], t[3], t[4]) for t in terms]
    in_specs += [row, pl.BlockSpec((1, D), lambda i: (0, 0)), row]
    return _pcall(
        body, (*[t[0] for t in terms], *[t[1] for t in terms], h, g.reshape(1, D), res), name=name,
        out_shape=(jax.ShapeDtypeStruct((S, D), F32), jax.ShapeDtypeStruct((8, D), F32)),
        grid=(S // tm,),
        in_specs=in_specs,
        out_specs=(row, pl.BlockSpec((8, D), lambda i: (0, 0))),
        sem=("arbitrary",), carry=carry)


def _load_resident(pairs, sems):
    @pl.when(pl.program_id(0) == 0)
    def _():
        copies = [pltpu.make_async_copy(src, dst, sems.at[n]) for n, (src, dst) in enumerate(pairs)]
        for cp in copies:
            cp.start()
        for cp in copies:
            cp.wait()


def _loss_tail(y_in, g, tgt):
    D = y_in.shape[-1]
    r = lax.rsqrt(jnp.mean(y_in * y_in, axis=-1, keepdims=True) + RMS_EPS)
    xhat = y_in * r
    err = xhat * g - tgt
    d = err * (1.0 / D)
    dxh = d * g
    c = jnp.mean(dxh * xhat, axis=-1, keepdims=True)
    return r * (dxh - xhat * c), _rows8(d * xhat), _rows8(err * err)


def _ffn_fwd_fused(h, g, win_t, wo, name, carry=None, loss=None):
    S, D = h.shape
    F = wo.shape[0]
    tm = _tile(S, 256, 16)
    n_head = 3 if loss is not None else 1

    def body(h_ref, g_ref, win_hbm, wo_hbm, *rest):
        lead, (xn_ref, silu_ref, dsilu_ref, up_ref, act_ref, win_v, wo_v, sems) = rest[:-8], rest[-8:]
        _load_resident([(win_hbm, win_v), (wo_hbm, wo_v)], sems)
        x = h_ref[...]
        r = lax.rsqrt(jnp.mean(x * x, axis=-1, keepdims=True) + RMS_EPS)
        xn = ((x * r) * g_ref[...]).astype(BF16)
        xn_ref[...] = xn
        gate = _dot(xn, win_v[:F, :], NT)
        up = _dot(xn, win_v[F:, :], NT)
        sig = 1.0 / (1.0 + jnp.exp(-gate))
        silu = gate * sig
        act = (silu * up).astype(BF16)
        up_ref[...] = up.astype(BF16)
        silu_ref[...] = silu.astype(BF16)
        dsilu_ref[...] = (sig + silu * (1.0 - sig)).astype(BF16)
        act_ref[...] = act
        out = x + FFN_RES_SCALE * _dot(act, wo_v[...], NN)
        if loss is None:
            lead[0][...] = out
        else:
            gf_ref, t_ref, dy_ref, dgf_ref, sq_ref = lead
            dy, dgf, sq = _loss_tail(out, gf_ref[...], t_ref[...])
            dy_ref[...] = dy

            @pl.when(pl.program_id(0) == 0)
            def _():
                dgf_ref[...] = dgf
                sq_ref[...] = sq

            @pl.when(pl.program_id(0) > 0)
            def _():
                dgf_ref[...] += dgf
                sq_ref[...] += sq

    row = pl.BlockSpec((tm, D), lambda i: (i, 0))
    vec = pl.BlockSpec((1, D), lambda i: (0, 0))
    acc = pl.BlockSpec((8, D), lambda i: (0, 0))
    wide = pl.BlockSpec((tm, F), lambda i: (i, 0))
    hbm = pl.BlockSpec(memory_space=pl.ANY)
    hid = jax.ShapeDtypeStruct((S, F), BF16)
    full = jax.ShapeDtypeStruct((S, D), F32)
    part = jax.ShapeDtypeStruct((8, D), F32)
    args, in_specs = (h, g.reshape(1, D), win_t, wo), [row, vec, hbm, hbm]
    lead_shapes, lead_specs = (full,), (row,)
    if loss is not None:
        args, in_specs = args + (loss[0].reshape(1, D), loss[1]), in_specs + [vec, row]
        lead_shapes, lead_specs = (full, part, part), (row, acc, acc)
    res, got = _pcall(
        body, args, name=name,
        out_shape=lead_shapes + (jax.ShapeDtypeStruct((S, D), BF16), hid, hid, hid, hid),
        grid=(S // tm,),
        in_specs=in_specs,
        out_specs=lead_specs + (row, wide, wide, wide, wide),
        scratch_shapes=[pltpu.VMEM(win_t.shape, BF16), pltpu.VMEM(wo.shape, BF16), pltpu.SemaphoreType.DMA((2,))],
        sem=("arbitrary",), carry=carry)
    first = res[0] if loss is None else tuple(res[:3])
    return first, tuple(res[n_head:]), got


def _ffn_bwd_fused(dh, h, g, win_t, wo, silu, dsilu, up, name, carry=None):
    S, D = h.shape
    F = wo.shape[0]
    tm = _tile(S, 256, 16)

    def body(dh_ref, h_ref, g_ref, s_ref, ds_ref, u_ref, win_hbm, wo_hbm,
             dhin_ref, dgain_ref, dgate_ref, dup_ref, win_v, wo_v, sems):
        _load_resident([(win_hbm, win_v), (wo_hbm, wo_v)], sems)
        dhv = dh_ref[...]
        d = _dot(dhv.astype(BF16), wo_v[...], NT) * FFN_RES_SCALE
        dup = (d * s_ref[...].astype(F32)).astype(BF16)
        dgate = (d * u_ref[...].astype(F32) * ds_ref[...].astype(F32)).astype(BF16)
        dup_ref[...] = dup
        dgate_ref[...] = dgate
        dxn = _dot(dgate, win_v[:F, :], NN) + _dot(dup, win_v[F:, :], NN)
        x = h_ref[...]
        r = lax.rsqrt(jnp.mean(x * x, axis=-1, keepdims=True) + RMS_EPS)
        xhat = x * r
        dxh = dxn * g_ref[...]
        c = jnp.mean(dxh * xhat, axis=-1, keepdims=True)
        dhin_ref[...] = r * (dxh - xhat * c) + dhv
        part = _rows8(dxn * xhat)

        @pl.when(pl.program_id(0) == 0)
        def _():
            dgain_ref[...] = part

        @pl.when(pl.program_id(0) > 0)
        def _():
            dgain_ref[...] += part

    row = pl.BlockSpec((tm, D), lambda i: (i, 0))
    wide = pl.BlockSpec((tm, F), lambda i: (i, 0))
    hbm = pl.BlockSpec(memory_space=pl.ANY)
    hid = jax.ShapeDtypeStruct((S, F), BF16)
    return _pcall(
        body, (dh, h, g.reshape(1, D), silu, dsilu, up, win_t, wo), name=name,
        out_shape=(jax.ShapeDtypeStruct((S, D), F32), jax.ShapeDtypeStruct((8, D), F32), hid, hid),
        grid=(S // tm,),
        in_specs=[row, row, pl.BlockSpec((1, D), lambda i: (0, 0)), wide, wide, wide, hbm, hbm],
        out_specs=(row, pl.BlockSpec((8, D), lambda i: (0, 0)), wide, wide),
        scratch_shapes=[pltpu.VMEM(win_t.shape, BF16), pltpu.VMEM(wo.shape, BF16), pltpu.SemaphoreType.DMA((2,))],
        sem=("arbitrary",), carry=carry)


def _rope_tables(S):
    half = HEAD_DIM // 2
    inv_freq = ROPE_THETA ** (-jnp.arange(half, dtype=F32) / half)
    ang = jnp.arange(S).astype(F32)[:, None] * inv_freq[None, :]
    cos, sin = jnp.cos(ang), jnp.sin(ang)
    cos_t = jnp.tile(cos, (1, LANES // half))
    sin_t = jnp.tile(jnp.concatenate([-sin, sin], axis=1), (1, LANES // HEAD_DIM))
    return cos_t, sin_t


def _swap_halves(x):
    lane = lax.broadcasted_iota(jnp.int32, x.shape, 1)
    first = (lane % HEAD_DIM) < (HEAD_DIM // 2)
    return jnp.where(first, pltpu.roll(x, LANES - HEAD_DIM // 2, 1), pltpu.roll(x, HEAD_DIM // 2, 1))


def _rotary_bwd(dys, cos_t, sin_t, n_rot, name):
    S = dys[0].shape[0]
    widths = [dy.shape[1] for dy in dys]
    ts = _tile(S, 512, 16)

    def body(*refs):
        x_refs, (c_ref, s_ref, o_ref) = refs[:len(dys)], refs[len(dys):]
        cs, sn = c_ref[...], s_ref[...]
        gidx = 0
        for x_ref, width in zip(x_refs, widths):
            for g in range(width // LANES):
                v = x_ref[:, g * LANES:(g + 1) * LANES].astype(F32)
                if gidx < n_rot:
                    v = v * cs + _swap_halves(v * sn)
                o_ref[:, gidx * LANES:(gidx + 1) * LANES] = v.astype(BF16)
                gidx += 1

    tab = pl.BlockSpec((ts, LANES), lambda i: (i, 0))
    return pl.pallas_call(
        body, name=name, out_shape=jax.ShapeDtypeStruct((S, sum(widths)), BF16),
        grid=(S // ts,),
        in_specs=[pl.BlockSpec((ts, width), lambda i: (i, 0)) for width in widths] + [tab, tab],
        out_specs=pl.BlockSpec((ts, sum(widths)), lambda i: (i, 0)),
        compiler_params=_params("parallel"),
    )(*dys, cos_t, sin_t)


def _head_masks():
    lane = lax.broadcasted_iota(jnp.int32, (BLK, LANES), 1)
    return lane < HEAD_DIM


def _split_bf16(x):
    hi = x.astype(BF16)
    lo = (x - hi.astype(F32)).astype(BF16)
    return hi, lo


def _sb_scores(qh, ks, carry, diag, tri_excl, strict):
    n_heads = len(qh)
    zs = [_dot(ks[n], qh[n], NT) for n in range(n_heads)]
    a_l, b_l, split_l = [], [], []
    for z in zs:
        a = jnp.minimum(z, 0.0) - jnp.log(1.0 + jnp.exp(-jnp.abs(z)))
        b = a - z
        if diag:
            b = jnp.where(strict, b, 0.0)
        a_l.append(a)
        b_l.append(b)
        split_l.append(_split_bf16(b))
    sufs = [_dot(tri_excl, hi, NN) + _dot(tri_excl, lo, NN) for hi, lo in split_l]
    w_l = []
    for n in range(n_heads):
        w = jnp.exp(a_l[n] + sufs[n] + carry[n])
        if diag:
            w = jnp.where(strict, w, 0.0)
        w_l.append(w)
    return a_l, b_l, w_l


SB_FWD_PAIRS = 4
SB_FWD_QBLOCKS = 4
SB_BWD_PAIRS = 2
SB_BWD_QBLOCKS = 4


def _any_alive(carries):
    top = carries[0]
    for c in carries[1:]:
        top = jnp.maximum(top, c)
    return (jnp.max(top) > SB_LOG_FLOOR).astype(jnp.int32)


def _sb_masks():
    row = lax.broadcasted_iota(jnp.int32, (BLK, BLK), 0)
    col = lax.broadcasted_iota(jnp.int32, (BLK, BLK), 1)
    tri_excl = jnp.where(col > row, 1.0, 0.0).astype(BF16)
    tri_incl = jnp.where(col >= row, 1.0, 0.0).astype(BF16)
    return row < HEAD_DIM, row < col, tri_excl, tri_incl


def _sb_fwd(qkv, kv_t, name, carry=None):
    S, D3 = qkv.shape
    D = D3 // 3
    npair, nb = D // LANES, S // BLK
    P = min(SB_FWD_PAIRS, npair)
    ngroup = npair // P
    W = P * LANES

    QB = SB_FWD_QBLOCKS if nb % SB_FWD_QBLOCKS == 0 else 1
    nch = QB * 2 * P

    def body(q_ref, k_ref, vt_ref, o_ref):
        i_first = pl.program_id(1) * QB
        m0 = _head_masks()
        top, strict, tri_excl, _ = _sb_masks()
        zq = jnp.zeros((BLK, LANES), BF16)
        lanes = [slice(p * LANES, (p + 1) * LANES) for p in range(P)]
        qh = []
        for qb in range(QB):
            for sl in lanes:
                q2 = q_ref[qb * BLK:(qb + 1) * BLK, sl] * ATTN_SCALE
                qh += [jnp.where(m0, q2, zq), jnp.where(m0, zq, q2)]

        def block(qbs, js, carry, acc, diag):
            offs = [pl.multiple_of(j * BLK, BLK) for j in js]
            ks, vth, qs = [], [], []
            for n_qb, qb in enumerate(qbs):
                qs += qh[qb * 2 * P:(qb + 1) * 2 * P]
                for sl in lanes:
                    k2 = k_ref[pl.ds(offs[n_qb], BLK), sl]
                    vt = vt_ref[sl, pl.ds(offs[n_qb], BLK)]
                    ks += [k2, k2]
                    vth += [jnp.where(top, vt, zq), jnp.where(top, zq, vt)]
            _, b_l, w_l = _sb_scores(qs, ks, carry, diag, tri_excl, strict)
            wb = [w.astype(BF16) for w in w_l]
            new_acc = [acc[m] + _dot(vth[2 * m], wb[2 * m], NN) + _dot(vth[2 * m + 1], wb[2 * m + 1], NN)
                       for m in range(len(qbs) * P)]
            new_carry = [carry[n] + jnp.sum(b_l[n], axis=0, keepdims=True) for n in range(len(carry))]
            return new_carry, new_acc

        every = list(range(QB))
        c0 = jnp.zeros((1, BLK), F32)
        carry, acc = block(every, [i_first + qb for qb in every], [c0] * nch,
                           [jnp.zeros((LANES, BLK), F32)] * (QB * P), True)
        carry = [jnp.where(i_first > 0, c, NEG_BIG) for c in carry[:2 * P]] + carry[2 * P:]
        carry, acc = block(every, [jnp.maximum(i_first + qb - 1, 0) for qb in every], carry, acc, False)

        for qb in range(QB):
            i_qb = i_first + qb
            sub = slice(qb * 2 * P, (qb + 1) * 2 * P)

            def cond(st):
                return jnp.logical_and(i_qb - st[0] >= 0, st[1] > 0)

            def step(st, qb=qb, i_qb=i_qb):
                t, _, c_qb, a_qb = st
                c_qb, a_qb = block([qb], [i_qb - t], c_qb, a_qb, False)
                return t + 1, _any_alive(c_qb), c_qb, a_qb

            st = lax.while_loop(cond, step, (2, _any_alive(carry[sub]), carry[sub], acc[qb * P:(qb + 1) * P]))
            for p, sl in enumerate(lanes):
                o_ref[qb * BLK:(qb + 1) * BLK, sl] = jnp.transpose(st[3][p])

    return _pcall(
        body, (qkv, qkv, kv_t), name=name, out_shape=jax.ShapeDtypeStruct((S, D), F32),
        grid=(ngroup, nb // QB),
        in_specs=[pl.BlockSpec((QB * BLK, W), lambda g, i: (i, g)),
                  pl.BlockSpec((S, W), lambda g, i: (0, ngroup + g)),
                  pl.BlockSpec((W, S), lambda g, i: (ngroup + g, 0))],
        out_specs=pl.BlockSpec((QB * BLK, W), lambda g, i: (i, g)),
        sem=("arbitrary", "arbitrary"), carry=carry)


def _sb_bwd(qkv, kv_t, o, do, name, carry=None):
    S, D3 = qkv.shape
    D = D3 // 3
    npair, nb = D // LANES, S // BLK
    P = min(SB_BWD_PAIRS, npair)
    ngroup = npair // P
    W = P * LANES

    QB = SB_BWD_QBLOCKS if nb % SB_BWD_QBLOCKS == 0 else 1
    nch = QB * 2 * P

    def body(q_ref, o_ref, do_ref, qkv_hbm, kt_hbm, dq_ref, dk_ref, dv_ref, k_ref, v_ref, kt_ref, sems):
        grp = pl.program_id(0)
        i_first = pl.program_id(1) * QB
        m0 = _head_masks()
        top, strict, tri_excl, tri_incl = _sb_masks()
        zq = jnp.zeros((BLK, LANES), BF16)
        lanes = [slice(p * LANES, (p + 1) * LANES) for p in range(P)]

        @pl.when(pl.program_id(1) == 0)
        def _():
            copies = [pltpu.make_async_copy(qkv_hbm.at[:, pl.ds(pl.multiple_of((c * ngroup + grp) * W, LANES), W)],
                                            ref, sems.at[c - 1]) for c, ref in ((1, k_ref), (2, v_ref))]
            copies.append(pltpu.make_async_copy(kt_hbm.at[pl.ds(pl.multiple_of(grp * W, LANES), W), :],
                                                kt_ref, sems.at[2]))
            for cp in copies:
                cp.start()
            dk_ref[...] = jnp.zeros_like(dk_ref)
            dv_ref[...] = jnp.zeros_like(dv_ref)
            for cp in copies:
                cp.wait()

        qh, doh, delta = [], [], []
        for qb in range(QB):
            rs = slice(qb * BLK, (qb + 1) * BLK)
            for sl in lanes:
                q2, do2 = q_ref[rs, sl] * ATTN_SCALE, do_ref[rs, sl]
                qh += [jnp.where(m0, q2, zq), jnp.where(m0, zq, q2)]
                doh += [jnp.where(m0, do2, zq), jnp.where(m0, zq, do2)]
                prod_t = jnp.transpose(do2.astype(F32) * o_ref[rs, sl])
                delta += [jnp.sum(jnp.where(top, prod_t, 0.0), axis=0, keepdims=True),
                          jnp.sum(jnp.where(top, 0.0, prod_t), axis=0, keepdims=True)]

        def block(qbs, js, valid, cb, cg, dq, diag):
            offs = [pl.multiple_of(j * BLK, BLK) for j in js]
            n_ch = len(qbs) * 2 * P
            ks, vs, kth, qs, dos, dls = [], [], [], [], [], []
            for n_qb, qb in enumerate(qbs):
                chains = slice(qb * 2 * P, (qb + 1) * 2 * P)
                qs, dos, dls = qs + qh[chains], dos + doh[chains], dls + delta[chains]
                for sl in lanes:
                    k2, v2 = k_ref[pl.ds(offs[n_qb], BLK), sl], v_ref[pl.ds(offs[n_qb], BLK), sl]
                    ks += [k2, k2]
                    vs += [v2, v2]
                    kt = kt_ref[sl, pl.ds(offs[n_qb], BLK)] * ATTN_SCALE
                    kth += [jnp.where(top, kt, zq), jnp.where(top, zq, kt)]
            dws = [_dot(vs[n], dos[n], NT) for n in range(n_ch)]
            a_l, b_l, w_l = _sb_scores(qs, ks, cb, diag, tri_excl, strict)
            wb = [w.astype(BF16) for w in w_l]
            g_l = [dws[n] * wb[n].astype(F32) for n in range(n_ch)]
            gsplit = [_split_bf16(g) for g in g_l]
            gincs = [_dot(tri_incl, hi, NN) + _dot(tri_incl, lo, NN) for hi, lo in gsplit]
            dzs = []
            for n in range(n_ch):
                beta = jnp.exp(a_l[n])
                dz = g_l[n] - beta * (g_l[n] + ((dls[n] - cg[n]) - gincs[n]))
                if diag:
                    dz = jnp.where(strict, dz, 0.0)
                if valid[n // (2 * P)] is not None:
                    dz = jnp.where(valid[n // (2 * P)], dz, 0.0)
                dzs.append(dz.astype(BF16))
            ndq = []
            for n_qb in range(len(qbs)):
                for p, sl in enumerate(lanes):
                    n0 = n_qb * 2 * P + 2 * p
                    ndq.append(dq[n_qb * P + p] + _dot(kth[n0], dzs[n0], NN) + _dot(kth[n0 + 1], dzs[n0 + 1], NN))
                    dk_ref[pl.ds(offs[n_qb], BLK), sl] += _dot(dzs[n0], qs[n0], NN) + _dot(dzs[n0 + 1], qs[n0 + 1], NN)
                    dv_ref[pl.ds(offs[n_qb], BLK), sl] += _dot(wb[n0], dos[n0], NN) + _dot(wb[n0 + 1], dos[n0 + 1], NN)
            ncb = [cb[n] + jnp.sum(b_l[n], axis=0, keepdims=True) for n in range(n_ch)]
            ncg = [cg[n] + jnp.sum(g_l[n], axis=0, keepdims=True) for n in range(n_ch)]
            return ncb, ncg, ndq

        every = list(range(QB))
        c0 = jnp.zeros((1, BLK), F32)
        cb, cg, dq = block(every, [i_first + qb for qb in every], [None] * QB, [c0] * nch, [c0] * nch,
                           [jnp.zeros((LANES, BLK), F32)] * (QB * P), True)
        has_prev = i_first > 0
        cb = [jnp.where(has_prev, c, NEG_BIG) for c in cb[:2 * P]] + cb[2 * P:]
        cb, cg, dq = block(every, [jnp.maximum(i_first + qb - 1, 0) for qb in every], [has_prev] + [None] * (QB - 1),
                           cb, cg, dq, False)

        for qb in range(QB):
            i_qb = i_first + qb
            sub = slice(qb * 2 * P, (qb + 1) * 2 * P)

            def cond(st):
                return jnp.logical_and(i_qb - st[0] >= 0, st[1] > 0)

            def step(st, qb=qb, i_qb=i_qb):
                t, _, b_qb, g_qb, dq_qb = st
                b_qb, g_qb, dq_qb = block([qb], [i_qb - t], [None], b_qb, g_qb, dq_qb, False)
                return t + 1, _any_alive(b_qb), b_qb, g_qb, dq_qb

            st = lax.while_loop(cond, step, (2, _any_alive(cb[sub]), cb[sub], cg[sub], dq[qb * P:(qb + 1) * P]))
            for p, sl in enumerate(lanes):
                dq_ref[qb * BLK:(qb + 1) * BLK, sl] = jnp.transpose(st[4][p]).astype(BF16)

    blk = pl.BlockSpec((QB * BLK, W), lambda g, i: (i, g))
    col_all = pl.BlockSpec((S, W), lambda g, i: (0, g))
    hbm = pl.BlockSpec(memory_space=pl.ANY)
    return _pcall(
        body, (qkv, o, do, qkv, kv_t), name=name,
        out_shape=(jax.ShapeDtypeStruct((S, D), BF16), jax.ShapeDtypeStruct((S, D), F32),
                   jax.ShapeDtypeStruct((S, D), F32)),
        grid=(ngroup, nb // QB),
        in_specs=[blk, blk, blk, hbm, hbm],
        out_specs=(blk, col_all, col_all),
        scratch_shapes=[pltpu.VMEM((S, W), BF16), pltpu.VMEM((S, W), BF16), pltpu.VMEM((W, S), BF16),
                        pltpu.SemaphoreType.DMA((3,))],
        sem=("arbitrary", "arbitrary"), carry=carry)


SWA_Q_GROUPS = 4


def _roll_heads(x):
    return pltpu.roll(x.astype(F32), HEAD_DIM, 1).astype(BF16)


def _roll_rows(x):
    return pltpu.roll(x.astype(F32), HEAD_DIM, 0).astype(BF16)


def _swa_valid(i):
    k = lax.broadcasted_iota(jnp.int32, (2 * BLK, BLK), 0)
    q = lax.broadcasted_iota(jnp.int32, (2 * BLK, BLK), 1)
    diff = q + BLK - k
    return (diff >= 0) & (diff < BLK) & ((i > 0) | (k >= BLK))


def _swa_probs(z, valid, sink):
    z = jnp.where(valid, z * ATTN_SCALE, NEG_BIG)
    mx = jnp.maximum(jnp.max(z, axis=0, keepdims=True), sink)
    p = jnp.exp(z - mx)
    ps = jnp.exp(sink - mx)
    inv = 1.0 / (jnp.sum(p, axis=0, keepdims=True) + ps)
    return p * inv, ps * inv


def _swa_operands(q_ref, kc_ref, kp_ref, vc_ref, vp_ref, tc_ref, tp_ref, s_ref, nkvp):
    m0 = _head_masks()
    top = lax.broadcasted_iota(jnp.int32, (LANES, 2 * BLK), 0) < HEAD_DIM
    heads = []
    for m in range(nkvp):
        pair = slice(m * LANES, (m + 1) * LANES)
        kk = jnp.concatenate([kp_ref[:, pair], kc_ref[:, pair]], axis=0)
        vv = jnp.concatenate([vp_ref[:, pair], vc_ref[:, pair]], axis=0)
        tt = jnp.concatenate([tp_ref[pair, :], tc_ref[pair, :]], axis=1)
        ksw, vsw, tsw = _roll_heads(kk), _roll_heads(vv), _roll_rows(tt)
        zt = jnp.zeros_like(tt)
        for c in range(SWA_Q_GROUPS):
            q_lanes = slice((m * SWA_Q_GROUPS + c) * LANES, (m * SWA_Q_GROUPS + c + 1) * LANES)
            qc = q_ref[:, q_lanes]
            zq = jnp.zeros_like(qc)
            for u in range(2):
                same = u == c // 2
                sel = (lambda x, z, mk: jnp.where(mk, x, z)) if u == 0 else (lambda x, z, mk: jnp.where(mk, z, x))
                heads.append(dict(
                    m=m, q_lanes=q_lanes, same=same, sel=sel, qm=sel(qc, zq, m0),
                    k=kk if same else ksw, v=vv if same else vsw,
                    tm=sel(tt if same else tsw, zt, top),
                    sink=s_ref[0, (m * SWA_Q_GROUPS + c) * 2 + u]))
    return heads, m0


def _swa_specs(D, half, t_block):
    prev = lambda i: jnp.maximum(i - 1, 0)
    return [pl.BlockSpec((BLK, D), lambda i: (i, 0)),
            pl.BlockSpec((BLK, half), lambda i: (i, 0)),
            pl.BlockSpec((BLK, half), lambda i: (prev(i), 0)),
            pl.BlockSpec((BLK, half), lambda i: (i, 1)),
            pl.BlockSpec((BLK, half), lambda i: (prev(i), 1)),
            pl.BlockSpec((half, BLK), lambda i: (t_block, i)),
            pl.BlockSpec((half, BLK), lambda i: (t_block, prev(i))),
            pl.BlockSpec(memory_space=pltpu.SMEM)]


def _swa_fwd(q, kv, kv_t, sinks, name):
    S, D = q.shape
    half = kv.shape[1] // 2
    nkvp = half // LANES

    def body(q_ref, kc_ref, kp_ref, vc_ref, vp_ref, tc_ref, tp_ref, s_ref, o_ref):
        valid = _swa_valid(pl.program_id(0))
        heads, _ = _swa_operands(q_ref, kc_ref, kp_ref, vc_ref, vp_ref, tc_ref, tp_ref, s_ref, nkvp)
        zs = [_dot(hd["k"], hd["qm"], NT) for hd in heads]
        ps = [_swa_probs(z, valid, hd["sink"])[0].astype(BF16) for z, hd in zip(zs, heads)]
        for n in range(0, len(heads), 2):
            o_t = _dot(heads[n]["tm"], ps[n], NN) + _dot(heads[n + 1]["tm"], ps[n + 1], NN)
            o_ref[:, heads[n]["q_lanes"]] = jnp.transpose(o_t)

    return pl.pallas_call(
        body, name=name, out_shape=jax.ShapeDtypeStruct((S, D), F32),
        grid=(S // BLK,),
        in_specs=_swa_specs(D, half, 1),
        out_specs=pl.BlockSpec((BLK, D), lambda i: (i, 0)),
        compiler_params=_params("arbitrary"),
    )(q, kv, kv, kv, kv, kv_t, kv_t, sinks)


def _swa_bwd(q, kv, kv_t, sinks, o, do, cos_t, sin_t, name, carry=None):
    S, D = q.shape
    half = kv.shape[1] // 2
    nkvp = half // LANES
    nh = nkvp * 2 * SWA_Q_GROUPS

    def body(q_ref, kc_ref, kp_ref, vc_ref, vp_ref, tc_ref, tp_ref, s_ref, o_ref, do_ref, c_ref, sn_ref,
             dq_ref, dk_ref, dv_ref, ds_ref):
        i = pl.program_id(0)
        valid = _swa_valid(i)
        heads, m0 = _swa_operands(q_ref, kc_ref, kp_ref, vc_ref, vp_ref, tc_ref, tp_ref, s_ref, nkvp)
        top_q = lax.broadcasted_iota(jnp.int32, (LANES, BLK), 0) < HEAD_DIM

        @pl.when(i == 0)
        def _():
            dk_ref[...] = jnp.zeros_like(dk_ref)
            dv_ref[...] = jnp.zeros_like(dv_ref)
            ds_ref[...] = jnp.zeros_like(ds_ref)

        doms, deltas = [], []
        for n in range(0, nh, 2):
            doc = do_ref[:, heads[n]["q_lanes"]]
            prod_t = jnp.transpose(doc.astype(F32) * o_ref[:, heads[n]["q_lanes"]])
            for hd in heads[n:n + 2]:
                doms.append(hd["sel"](doc, jnp.zeros_like(doc), m0))
                deltas.append(jnp.sum(hd["sel"](prod_t, 0.0, top_q), axis=0, keepdims=True))
        zs = [_dot(hd["k"], hd["qm"], NT) for hd in heads]
        dps = [_dot(hd["v"], dom, NT) for dom, hd in zip(doms, heads)]
        pbs, dscs = [], []
        for n, hd in enumerate(heads):
            p, psink = _swa_probs(zs[n], valid, hd["sink"])
            pbs.append(p.astype(BF16))
            dscs.append((p * (dps[n] - deltas[n]) * ATTN_SCALE).astype(BF16))
            ds_ref[n:n + 1, :] += -(psink * deltas[n])
        for n in range(0, nh, 2):
            dq_rot = jnp.transpose(_dot(heads[n]["tm"], dscs[n], NN) + _dot(heads[n + 1]["tm"], dscs[n + 1], NN))
            dq_ref[:, heads[n]["q_lanes"]] = (
                dq_rot * c_ref[...] + _swap_halves(dq_rot * sn_ref[...])).astype(BF16)
        acc = {}
        for n, hd in enumerate(heads):
            dk_n = _dot(dscs[n], hd["qm"], NN)
            dv_n = _dot(pbs[n], doms[n], NN)
            for key, val in ((("k", hd["m"], hd["same"]), dk_n), (("v", hd["m"], hd["same"]), dv_n)):
                acc[key] = val if key not in acc else acc[key] + val
        poff = pl.multiple_of(jnp.maximum(i - 1, 0) * BLK, BLK)
        coff = pl.multiple_of(i * BLK, BLK)
        for m in range(nkvp):
            pair = slice(m * LANES, (m + 1) * LANES)
            dkk = acc["k", m, True] + pltpu.roll(acc["k", m, False], HEAD_DIM, 1)
            dvv = acc["v", m, True] + pltpu.roll(acc["v", m, False], HEAD_DIM, 1)
            dk_ref[pl.ds(poff, BLK), pair] += dkk[:BLK]
            dv_ref[pl.ds(poff, BLK), pair] += dvv[:BLK]
            dk_ref[pl.ds(coff, BLK), pair] += dkk[BLK:]
            dv_ref[pl.ds(coff, BLK), pair] += dvv[BLK:]

    qblk = pl.BlockSpec((BLK, D), lambda i: (i, 0))
    whole = pl.BlockSpec((S, half), lambda i: (0, 0))
    tab = pl.BlockSpec((BLK, LANES), lambda i: (i, 0))
    return _pcall(
        body, (q, kv, kv, kv, kv, kv_t, kv_t, sinks, o, do, cos_t, sin_t), name=name,
        out_shape=(jax.ShapeDtypeStruct((S, D), BF16),
                   jax.ShapeDtypeStruct((S, half), F32),
                   jax.ShapeDtypeStruct((S, half), F32),
                   jax.ShapeDtypeStruct((nh, LANES), F32)),
        grid=(S // BLK,),
        in_specs=_swa_specs(D, half, 0) + [qblk, qblk, tab, tab],
        out_specs=(qblk, whole, whole, pl.BlockSpec((nh, LANES), lambda i: (0, 0))),
        sem=("arbitrary",), carry=carry)


def _dev_index(p):
    return 4 * p[0] + 2 * p[1] + p[2]


def _gather_plan(x_refs, out_refs, send_sems, recv_sems, local_sems):
    n = len(x_refs)
    x_, y_, c_ = lax.axis_index("x"), lax.axis_index("y"), lax.axis_index("c")
    me, sibling = (x_, y_, c_), (x_, y_, 1 - c_)
    chips = [(1 - x_, y_), (x_, 1 - y_), (1 - x_, 1 - y_)]
    relay_block = (x_ ^ (1 - c_), y_ ^ c_, c_)
    relay_to = (x_ ^ c_, y_ ^ (1 - c_), c_)

    def copy(t, k, block, to, src=None):
        dst = out_refs[t].at[_dev_index(block)]
        return pltpu.make_async_remote_copy(
            src_ref=dst if src is None else src, dst_ref=dst,
            send_sem=send_sems.at[7 * t + k], recv_sem=recv_sems.at[7 * t + k],
            device_id=to, device_id_type=MESH)

    mine = [pltpu.make_async_copy(x_refs[t], out_refs[t].at[_dev_index(me)], local_sems.at[t]) for t in range(n)]
    first = []
    for t in range(n):
        first.append(copy(t, 0, me, sibling, src=x_refs[t]))
        first += [copy(t, 1 + j, me, (*chip, c_), src=x_refs[t]) for j, chip in enumerate(chips[:2])]
    relay = [copy(t, 3, relay_block, relay_to) for t in range(n)]
    arrived = lambda t, j: copy(t, 1 + j, (*chips[j], c_), me)
    forward = lambda t, j: copy(t, 4 + j, (*chips[j], c_), sibling)
    from_sibling = lambda t: copy(t, 0, sibling, me)
    forwarded = lambda t, j: copy(t, 4 + j, (*chips[j], 1 - c_), me)
    return n, mine, first, relay, arrived, forward, from_sibling, forwarded


def _gather_start(x_refs, out_refs, send_sems, recv_sems, local_sems):
    _, mine, first, *_ = _gather_plan(x_refs, out_refs, send_sems, recv_sems, local_sems)
    for cp in mine + first:
        cp.start()


def _gather_forward(x_refs, out_refs, send_sems, recv_sems, local_sems):
    n, _, _, relay, arrived, forward, _, _ = _gather_plan(x_refs, out_refs, send_sems, recv_sems, local_sems)
    for j in range(2):
        for t in range(n):
            arrived(t, j).wait_recv()
    for cp in relay:
        cp.start()
    for j in range(2):
        for t in range(n):
            forward(t, j).start()


def _gather_finish(x_refs, out_refs, send_sems, recv_sems, local_sems):
    n, mine, first, relay, arrived, forward, from_sibling, forwarded = _gather_plan(
        x_refs, out_refs, send_sems, recv_sems, local_sems)
    for t in range(n):
        arrived(t, 2).wait_recv()
        forward(t, 2).start()
    for t in range(n):
        from_sibling(t).wait_recv()
    for j in range(3):
        for t in range(n):
            forwarded(t, j).wait_recv()
    for cp in first + relay + [forward(t, j) for j in range(3) for t in range(n)]:
        cp.wait_send()
    for cp in mine:
        cp.wait()


def _scatter_plan(b_refs, out_refs, send_sems, recv_sems, local_sems):
    n = len(b_refs)
    x_, y_, c_ = lax.axis_index("x"), lax.axis_index("y"), lax.axis_index("c")
    my_idx = _dev_index((x_, y_, c_))
    mine = [pltpu.make_async_copy(b_refs[t].at[my_idx], out_refs[t].at[my_idx], local_sems.at[t]) for t in range(n)]
    copies = []
    for t in range(n):
        for k in range(1, N_DEV):
            peer = (x_ ^ ((k >> 2) & 1), y_ ^ ((k >> 1) & 1), c_ ^ (k & 1))
            copies.append(pltpu.make_async_remote_copy(
                src_ref=b_refs[t].at[_dev_index(peer)], dst_ref=out_refs[t].at[my_idx],
                send_sem=send_sems.at[7 * t + k - 1], recv_sem=recv_sems.at[7 * t + k - 1],
                device_id=peer, device_id_type=MESH))
    return mine, copies


def _scatter_start(b_refs, out_refs, send_sems, recv_sems, local_sems):
    mine, copies = _scatter_plan(b_refs, out_refs, send_sems, recv_sems, local_sems)
    for cp in mine + copies:
        cp.start()


def _scatter_finish(b_refs, out_refs, send_sems, recv_sems, local_sems):
    mine, copies = _scatter_plan(b_refs, out_refs, send_sems, recv_sems, local_sems)
    for cp in copies:
        cp.wait_recv()
    for cp in copies:
        cp.wait_send()
    for cp in mine:
        cp.wait()


def _exchange_operands(kind, tensors):
    if kind == "gather":
        args = list(tensors)
        shapes = [jax.ShapeDtypeStruct((N_DEV,) + t.shape, t.dtype) for t in tensors]
        return args, shapes, (_gather_start, _gather_forward, _gather_finish)
    args = [t.reshape(N_DEV, t.shape[0] // N_DEV, t.shape[1]) for t in tensors]
    shapes = [jax.ShapeDtypeStruct(a.shape, a.dtype) for a in args]
    return args, shapes, (_scatter_start, None, _scatter_finish)


def _exchange_results(kind, tensors, res):
    if kind == "gather":
        return [r.reshape(N_DEV * t.shape[0], t.shape[1]) for r, t in zip(res, tensors)]
    return list(res)


def _exchange_sems(n):
    return [pltpu.SemaphoreType.DMA((7 * n,)), pltpu.SemaphoreType.DMA((7 * n,)), pltpu.SemaphoreType.DMA((n,))]


def _exchange(kind, tensors, name):
    n = len(tensors)
    args, shapes, phases = _exchange_operands(kind, tensors)

    def body(*refs):
        for phase in phases:
            if phase is not None:
                phase(refs[:n], refs[n:2 * n], *refs[2 * n:])

    hbm = pl.BlockSpec(memory_space=pl.ANY)
    res = pl.pallas_call(body, name=name, out_shape=shapes, in_specs=[hbm] * n, out_specs=[hbm] * n,
                         scratch_shapes=_exchange_sems(n))(*args)
    return _exchange_results(kind, tensors, res)


def _pcall(body, args, *, name, out_shape, grid, in_specs, out_specs, sem, scratch_shapes=(), carry=None):
    if carry is None:
        out = pl.pallas_call(body, name=name, out_shape=out_shape, grid=grid, in_specs=list(in_specs),
                             out_specs=out_specs, scratch_shapes=list(scratch_shapes),
                             compiler_params=_params(*sem))(*args)
        return out, None
    kind, tensors = carry
    multi = isinstance(out_shape, (tuple, list))
    shapes = list(out_shape) if multi else [out_shape]
    ospecs = list(out_specs) if multi else [out_specs]
    n_in, n_out, n_scr, n_c = len(in_specs), len(shapes), len(scratch_shapes), len(tensors)
    c_args, c_shapes, (start, forward, finish) = _exchange_operands(kind, tensors)
    n_steps = 1
    for g in grid:
        n_steps *= g
    late = n_steps // 2

    def wrapped(*refs):
        ins, rest = refs[:n_in], refs[n_in:]
        c_in, rest = rest[:n_c], rest[n_c:]
        outs, rest = rest[:n_out], rest[n_out:]
        c_out, rest = rest[:n_c], rest[n_c:]
        scr, sems = rest[:n_scr], rest[n_scr:]
        step = pl.program_id(0)
        for a in range(1, len(grid)):
            step = step * grid[a] + pl.program_id(a)

        @pl.when(step == 0)
        def _():
            start(c_in, c_out, *sems)

        body(*ins, *outs, *scr)

        if forward is not None:
            @pl.when(step == late)
            def _():
                forward(c_in, c_out, *sems)

        @pl.when(step == n_steps - 1)
        def _():
            finish(c_in, c_out, *sems)

    hbm = pl.BlockSpec(memory_space=pl.ANY)
    res = pl.pallas_call(
        wrapped, name=name, out_shape=shapes + c_shapes, grid=grid,
        in_specs=list(in_specs) + [hbm] * n_c, out_specs=ospecs + [hbm] * n_c,
        scratch_shapes=list(scratch_shapes) + _exchange_sems(n_c),
        compiler_params=_params(*sem))(*args, *c_args)
    outs = tuple(res[:n_out]) if multi else res[0]
    return outs, _exchange_results(kind, tensors, res[n_out:])


def _sum8(parts, name):
    _, R, C = parts.shape
    tr = _tile(R, 256, 16)

    def body(p_ref, g_ref):
        g = p_ref[0].astype(F32)
        for s in range(1, N_DEV):
            g = g + p_ref[s].astype(F32)
        g_ref[...] = g

    return pl.pallas_call(
        body, name=name, out_shape=jax.ShapeDtypeStruct((R, C), F32),
        grid=(R // tr,),
        in_specs=[pl.BlockSpec((N_DEV, tr, C), lambda i: (0, i, 0))],
        out_specs=pl.BlockSpec((tr, C), lambda i: (i, 0)),
        compiler_params=_params("parallel"),
    )(parts)


def _adamw(g, w, m, v, name):
    R, C = g.shape
    tr = _tile(R, 256, 8)
    c1 = 1.0 - ADAM_B1 ** ADAM_STEP
    c2 = 1.0 - ADAM_B2 ** ADAM_STEP

    def body(g_ref, w_ref, m_ref, v_ref, d_ref, nm_ref, nv_ref):
        gg = g_ref[...]
        nm = ADAM_B1 * m_ref[...] + (1.0 - ADAM_B1) * gg
        nv = ADAM_B2 * v_ref[...] + (1.0 - ADAM_B2) * (gg * gg)
        m_hat = nm / c1
        v_hat = nv / c2
        nm_ref[...] = nm
        nv_ref[...] = nv
        d_ref[...] = -ADAM_LR * (m_hat / (jnp.sqrt(v_hat) + ADAM_EPS) + ADAM_WD * w_ref[...])

    row = pl.BlockSpec((tr, C), lambda i: (i, 0))
    shp = jax.ShapeDtypeStruct((R, C), F32)
    return pl.pallas_call(
        body, name=name, out_shape=(shp, shp, shp),
        grid=(R // tr,), in_specs=[row, row, row, row], out_specs=(row, row, row),
        compiler_params=_params("parallel"),
    )(g, w, m, v)


def _sum_adamw(parts_list, w, m, v, name):
    L, R, C = w.shape
    tr = _tile(R, 256, 16)
    c1 = 1.0 - ADAM_B1 ** ADAM_STEP
    c2 = 1.0 - ADAM_B2 ** ADAM_STEP

    def body(*refs):
        p_refs = refs[:L]
        w_ref, m_ref, v_ref, g_ref, d_ref, nm_ref, nv_ref = refs[L:]
        for layer in range(L):
            @pl.when(pl.program_id(0) == layer)
            def _():
                g = p_refs[layer][0].astype(F32)
                for s in range(1, N_DEV):
                    g = g + p_refs[layer][s].astype(F32)
                nm = ADAM_B1 * m_ref[0] + (1.0 - ADAM_B1) * g
                nv = ADAM_B2 * v_ref[0] + (1.0 - ADAM_B2) * (g * g)
                g_ref[0] = g
                nm_ref[0] = nm
                nv_ref[0] = nv
                d_ref[0] = -ADAM_LR * ((nm / c1) / (jnp.sqrt(nv / c2) + ADAM_EPS) + ADAM_WD * w_ref[0])

    def parts_spec(layer):
        return pl.BlockSpec((N_DEV, tr, C), lambda l, i: (0, jnp.where(l == layer, i, 0), 0))

    blk = pl.BlockSpec((1, tr, C), lambda l, i: (l, i, 0))
    shp = jax.ShapeDtypeStruct((L, R, C), F32)
    return pl.pallas_call(
        body, name=name, out_shape=(shp, shp, shp, shp),
        grid=(L, R // tr),
        in_specs=[parts_spec(layer) for layer in range(L)] + [blk, blk, blk],
        out_specs=(blk, blk, blk, blk),
        compiler_params=_params("arbitrary", "arbitrary"),
    )(*parts_list, w, m, v)


def _ffn_down(act, wo, h, tag):
    return _mm(act, wo, NN, F32, f"{tag}_down", scale=FFN_RES_SCALE, res=h, tm=512, tn=1024, tk=2816)


def _ffn_fwd(h, g, win_t, wo, tag, carry=None, loss=None):
    return _ffn_fwd_fused(h, g, win_t, wo, f"{tag}_fwd", carry=carry, loss=loss)


def _ffn_bwd(dh, h, g, win_t, wo, saved, tag, scatter=False, carry=None):
    xn, silu, dsilu, up, act = saved
    dwo = _mm(act, dh, TN, BF16, f"{tag}_dwo", scale=FFN_RES_SCALE, tm=1408, tn=1024, tk=TN_CHUNK)
    if not scatter:
        (dh_in, dg, dgate, dup), got = _ffn_bwd_fused(dh, h, g, win_t, wo, silu, dsilu, up, f"{tag}_bwd", carry=carry)
        dwin_t, _ = _dw_rows([dgate, dup], xn, f"{tag}_dwin")
        return dh_in, dg, dwin_t, dwo, got
    dgate, dup = _ffn_dact(dh, wo, silu, dsilu, up, f"{tag}_dact")
    dwin_t, got_wo = _dw_rows([dgate, dup], xn, f"{tag}_dwin", carry=("scatter", [dwo]))
    (dh_in, dg), got_win = _dx_norm_bwd([(dgate, win_t, NN, 2, 0), (dup, win_t, NN, 2, 1)], h, g, dh, f"{tag}_dx",
                                        carry=("scatter", [dwin_t]))
    return dh_in, dg, got_win[0], got_wo[0]


def _proj(a, w, dims, out_dtype, name, res=None):
    return _mm(a, w, dims, out_dtype, name, res=res, tm=1024, tn=1024, tk=1024)


def _proj_dw(x, dy, name):
    return _mm(x, dy, TN, BF16, name, tm=1024, tn=1024, tk=TN_CHUNK)


def kernel(x, ffn1_norm, ffn1_w_in, ffn1_w_out, mix_norm, ffn2_norm, ffn2_w_in, ffn2_w_out, sb_w_qkv, sb_w_o, kv_norm, kv_w, swa_w_q, swa_sinks, swa_w_o, final_norm, loss_target, m_ffn1_norm, m_ffn1_w_in, m_ffn1_w_out, m_mix_norm, m_ffn2_norm, m_ffn2_w_in, m_ffn2_w_out, m_sb_w_qkv, m_sb_w_o, m_kv_norm, m_kv_w, m_swa_w_q, m_swa_sinks, m_swa_w_o, m_final_norm, v_ffn1_norm, v_ffn1_w_in, v_ffn1_w_out, v_mix_norm, v_ffn2_norm, v_ffn2_w_in, v_ffn2_w_out, v_sb_w_qkv, v_sb_w_o, v_kv_norm, v_kv_w, v_swa_w_q, v_swa_sinks, v_swa_w_o, v_final_norm):
    S, D = x.shape[1], x.shape[2]
    L = ffn1_w_in.shape[0]
    KV = kv_w.shape[1]
    assert L == 2 and swa_sinks.shape == (1, 2 * SWA_Q_GROUPS * KV // (2 * LANES))

    def bf(w):
        return w.astype(BF16)

    def bft(w):
        return jnp.transpose(w).astype(BF16)

    cos_t, sin_t = _rope_tables(S)
    h0 = x.reshape(S, D)
    tgt = loss_target.reshape(S, D)

    win1a_t, = _exchange("gather", [bft(ffn1_w_in[0])], "gather_first_weight")
    sv_a1, (wo1a, wqkv_t, w_sbo) = _ffn_up(
        h0, ffn1_norm[0], win1a_t, "ffn1a_up",
        carry=("gather", [bf(ffn1_w_out[0]), bft(sb_w_qkv[0]), bf(sb_w_o[0])]))
    h1 = _ffn_down(sv_a1[-1], wo1a, h0, "ffn1a")
    hn_a, qkv, kv_t = _norm_proj(h1, mix_norm[0], wqkv_t, NT, "sb_qkv", tail_t=2 * D)
    o_sb, (win2a_t, wo2a, w_kv) = _sb_fwd(qkv, kv_t, "sb_attn", carry=("gather", [
        bft(ffn2_w_in[0]), bf(ffn2_w_out[0]), bf(kv_w)]))
    h2 = _proj(o_sb, w_sbo, NN, F32, "sb_out", res=h1)
    h3, sv_a2, (win1b_t, wo1b, w_q, w_swo) = _ffn_fwd(h2, ffn2_norm[0], win2a_t, wo2a, "ffn2a", carry=("gather", [
        bft(ffn1_w_in[1]), bf(ffn1_w_out[1]), bf(swa_w_q[0]), bf(swa_w_o[0])]))
    kvn, kv_rot, kv_rot_t = _norm_proj(h3, kv_norm, w_kv, NN, "kv_proj", rope=(cos_t, sin_t, KV // (2 * LANES)),
                                       tail_t=KV)
    h4, sv_b1, (win2b_t, wo2b) = _ffn_fwd(h3, ffn1_norm[1], win1b_t, wo1b, "ffn1b", carry=("gather", [
        bft(ffn2_w_in[1]), bf(ffn2_w_out[1])]))
    hn_b, q_rot = _norm_proj(h4, mix_norm[1], w_q, NN, "swa_q", rope=(cos_t, sin_t, D // LANES))
    o_sw = _swa_fwd(q_rot, kv_rot, kv_rot_t, swa_sinks, "swa_attn")
    h5 = _proj(o_sw, w_swo, NN, F32, "swa_out", res=h4)
    (dh6, dg_final, sq_err), sv_b2, _ = _ffn_fwd(h5, ffn2_norm[1], win2b_t, wo2b, "ffn2b", loss=(final_norm, tgt))
    loss_local = 0.5 * jnp.sum(sq_err) / D

    dh5, dg_f2b, dwin2b_t, dwo2b, _ = _ffn_bwd(dh6, h5, ffn2_norm[1], win2b_t, wo2b, sv_b2, "ffn2b")
    do_sw = _proj(dh5, w_swo, NT, BF16, "swa_out_dx")
    dw_swo = _proj_dw(o_sw, dh5, "swa_out_dw")
    (dq, dk_sw, dv_sw, dsink), (p_win2b, p_swo) = _swa_bwd(
        q_rot, kv_rot, kv_rot_t, swa_sinks, o_sw, do_sw, cos_t, sin_t, "swa_attn_bwd",
        carry=("scatter", [dwin2b_t, dw_swo]))
    dw_q = _proj_dw(hn_b, dq, "swa_q_dw")
    (dh4, dg_mix_b), _ = _dx_norm_bwd([(dq, w_q, NT, 1, 0)], h4, mix_norm[1], dh5, "swa_q_dx", tm=512)
    dh3, dg_f1b, dwin1b_t, dwo1b, _ = _ffn_bwd(dh4, h3, ffn1_norm[1], win1b_t, wo1b, sv_b1, "ffn1b")
    dkv = _rotary_bwd([dk_sw, dv_sw], cos_t, sin_t, KV // (2 * LANES), "kv_rope_bwd")
    dw_kv = _proj_dw(kvn, dkv, "kv_proj_dw")
    (dh3, dg_kv), _ = _dx_norm_bwd([(dkv, w_kv, NT, 1, 0)], h3, kv_norm, dh3, "kv_proj_dx", tm=512)
    dh2, dg_f2a, dwin2a_t, dwo2a, (p_win1b, p_kv) = _ffn_bwd(
        dh3, h2, ffn2_norm[0], win2a_t, wo2a, sv_a2, "ffn2a", carry=("scatter", [dwin1b_t, dw_kv]))
    do_sb = _proj(dh2, w_sbo, NT, BF16, "sb_out_dx")
    dw_sbo = _proj_dw(o_sb, dh2, "sb_out_dw")
    (dq_sb, dk_sb, dv_sb), (p_win2a, p_wo2a, p_sbo, p_wo1b, p_q, p_wo2b) = _sb_bwd(
        qkv, kv_t, o_sb, do_sb, "sb_attn_bwd", carry=("scatter", [dwin2a_t, dwo2a, dw_sbo, dwo1b, dw_q, dwo2b]))
    dqkv = [dq_sb, dk_sb, dv_sb]
    dwqkv_t, _ = _dw_rows(dqkv, hn_a, "sb_qkv_dw", tk=TN_CHUNK // 2)
    (dh1, dg_mix_a), (p_qkv,) = _dx_norm_bwd([(dy, wqkv_t, NN, 3, n) for n, dy in enumerate(dqkv)], h1, mix_norm[0],
                                             dh2, "sb_qkv_dx", carry=("scatter", [dwqkv_t]), tm=512)
    dx, dg_f1a, p_win1a, p_wo1a = _ffn_bwd(dh1, h0, ffn1_norm[0], win1a_t, wo1a, sv_a1, "ffn1a", scatter=True)

    def from_t(parts, tag):
        return jnp.transpose(_sum8(parts, f"sum_{tag}"))

    grads = {
        "ffn1_w_in": jnp.stack([from_t(p_win1a, "win1a"), from_t(p_win1b, "win1b")]),
        "ffn2_w_in": jnp.stack([from_t(p_win2a, "win2a"), from_t(p_win2b, "win2b")]),
        "sb_w_qkv": from_t(p_qkv, "qkv")[None],
    }
    row_parts = {"ffn1_w_out": [p_wo1a, p_wo1b], "ffn2_w_out": [p_wo2a, p_wo2b], "sb_w_o": [p_sbo],
                 "kv_w": [p_kv], "swa_w_q": [p_q], "swa_w_o": [p_swo]}

    small_w = [ffn1_norm, mix_norm, ffn2_norm, kv_norm, final_norm, swa_sinks]
    small_m = [m_ffn1_norm, m_mix_norm, m_ffn2_norm, m_kv_norm, m_final_norm, m_swa_sinks]
    small_v = [v_ffn1_norm, v_mix_norm, v_ffn2_norm, v_kv_norm, v_final_norm, v_swa_sinks]
    SMALL_ROWS = 16

    def pack_small(ts):
        rows_ = [t.reshape(-1, D) for t in ts[:-1]]
        sink_row = jnp.pad(ts[-1].reshape(1, -1), ((0, 0), (0, D - ts[-1].size)))
        flat = jnp.concatenate(rows_ + [sink_row], axis=0)
        return jnp.pad(flat, ((0, SMALL_ROWS - flat.shape[0]), (0, 0)))

    def unpack_small(flat):
        out, r = [], 0
        for t in small_w[:-1]:
            n = t.size // D
            out.append(flat[r:r + n].reshape(t.shape))
            r += n
        out.append(flat[r, :swa_sinks.size].reshape(swa_sinks.shape))
        return out

    def gain(parts8):
        return jnp.sum(parts8, axis=0, keepdims=True)

    g_small_local = pack_small([
        jnp.concatenate([gain(dg_f1a), gain(dg_f1b)], axis=0),
        jnp.concatenate([gain(dg_mix_a), gain(dg_mix_b)], axis=0),
        jnp.concatenate([gain(dg_f2a), gain(dg_f2b)], axis=0),
        gain(dg_kv), gain(dg_final), jnp.sum(dsink, axis=-1).reshape(1, -1)])
    loss_row = sum(t.size for t in small_w[:-1]) // D + 1
    assert loss_row < SMALL_ROWS
    g_small_local = g_small_local.at[loss_row, 0].set(loss_local)
    small_parts = _exchange("gather", [g_small_local], "gather_small_grads")[0]
    g_small = _sum8(small_parts.reshape(N_DEV, SMALL_ROWS, D), "sum_small")
    loss = g_small[loss_row, 0]
    d_small, nm_small, nv_small = _adamw(g_small, pack_small(small_w), pack_small(small_m), pack_small(small_v), "adamw_small")
    small_names = ["ffn1_norm", "mix_norm", "ffn2_norm", "kv_norm", "final_norm", "swa_sinks"]
    result = {"grad": dict(zip(small_names, unpack_small(g_small))),
              "delta": dict(zip(small_names, unpack_small(d_small))),
              "new_m": dict(zip(small_names, unpack_small(nm_small))),
              "new_v": dict(zip(small_names, unpack_small(nv_small)))}

    big = {"ffn1_w_in": (ffn1_w_in, m_ffn1_w_in, v_ffn1_w_in), "ffn1_w_out": (ffn1_w_out, m_ffn1_w_out, v_ffn1_w_out),
           "ffn2_w_in": (ffn2_w_in, m_ffn2_w_in, v_ffn2_w_in), "ffn2_w_out": (ffn2_w_out, m_ffn2_w_out, v_ffn2_w_out),
           "sb_w_qkv": (sb_w_qkv, m_sb_w_qkv, v_sb_w_qkv), "sb_w_o": (sb_w_o, m_sb_w_o, v_sb_w_o),
           "kv_w": (kv_w, m_kv_w, v_kv_w), "swa_w_q": (swa_w_q, m_swa_w_q, v_swa_w_q),
           "swa_w_o": (swa_w_o, m_swa_w_o, v_swa_w_o)}
    for nm, (w, m, v) in big.items():
        if nm in row_parts:
            three_d = lambda t: t.reshape((len(row_parts[nm]),) + t.shape[-2:])
            g, d, new_m, new_v = _sum_adamw(row_parts[nm], three_d(w), three_d(m), three_d(v), f"adamw_{nm}")
        else:
            g = grads[nm]
            two_d = lambda t: t.reshape(-1, t.shape[-1])
            d, new_m, new_v = _adamw(two_d(g), two_d(w), two_d(m), two_d(v), f"adamw_{nm}")
        result["grad"][nm] = g.reshape(w.shape)
        result["delta"][nm] = d.reshape(w.shape)
        result["new_m"][nm] = new_m.reshape(w.shape)
        result["new_v"][nm] = new_v.reshape(w.shape)

    order = ["ffn1_norm", "ffn1_w_in", "ffn1_w_out", "mix_norm", "ffn2_norm", "ffn2_w_in", "ffn2_w_out",
             "sb_w_qkv", "sb_w_o", "kv_norm", "kv_w", "swa_w_q", "swa_sinks", "swa_w_o", "final_norm"]
    outs = [result[kind][nm] for kind in ("grad", "delta", "new_m", "new_v") for nm in order]
    return (loss, dx.reshape(x.shape), *outs)
```
